```python
import jax, jax.numpy as jnp
from jax import lax
import numpy as np

D_MODEL = 4096
BATCH = 8
SEQ = 4096
DEPTH = 1

HEAD_DIM = 64
MIX_WIDTH = D_MODEL
RWKV_WIDTH = 3 * D_MODEL // 8
FOX_WIDTH = 3 * D_MODEL // 8
MEM_WIDTH = D_MODEL // 4
RWKV_HEADS = RWKV_WIDTH // HEAD_DIM
FOX_HEADS = FOX_WIDTH // HEAD_DIM
MEM_HEADS = 4
MEM_HEAD_DIM = MEM_WIDTH // MEM_HEADS
N_MEM = 256
DECAY_LORA = 128
ICLR_LORA = 128
Q_BLOCK = 128
RMS_EPS = 1e-6
GN_EPS = 64e-5

RWKV_SHIFT_SPLITS = [RWKV_WIDTH, RWKV_WIDTH, RWKV_WIDTH, DECAY_LORA, ICLR_LORA]
RWKV_SHIFT_WIDTH = 3 * RWKV_WIDTH + DECAY_LORA + ICLR_LORA
REST_SPLITS = [RWKV_WIDTH, FOX_WIDTH, FOX_WIDTH, FOX_WIDTH, FOX_HEADS, FOX_WIDTH, MEM_WIDTH, MEM_WIDTH]
REST_WIDTH = RWKV_WIDTH + 4 * FOX_WIDTH + FOX_HEADS + 2 * MEM_WIDTH
IN_WIDTH = RWKV_SHIFT_WIDTH + REST_WIDTH

kernel_name = "hymba_rwkv7_fox_memxattn_layer"


def _offsets(sizes):
    return [int(s) for s in np.cumsum(sizes)[:-1]]


def rms_norm(x, g):
    xf = x.astype(jnp.float32)
    y = xf * lax.rsqrt(jnp.mean(xf * xf, axis=-1, keepdims=True) + RMS_EPS)
    return (y * g.astype(jnp.float32)).astype(x.dtype)


def token_shift(p, mu):
    prev = jnp.pad(p, ((0, 0), (1, 0), (0, 0)))[:, :-1]
    return p + (prev - p) * mu


def rwkv7_branch(r, k, v, wl, al, w0, w_decay_up, a0, w_iclr_up, k_k, k_a, r_k, ln_x_w, ln_x_b):
    B, T, _ = r.shape
    H, N = RWKV_HEADS, HEAD_DIM
    f32 = jnp.float32
    w_pre = -jax.nn.softplus(-(w0 + jnp.tanh(wl) @ w_decay_up).astype(f32)) - 0.5
    decay = jnp.exp(-jnp.exp(w_pre))
    alpha = jax.nn.sigmoid((a0 + al @ w_iclr_up).astype(f32))
    kk = (k * k_k).astype(f32).reshape(B, T, H, N)
    kk = kk * lax.rsqrt(jnp.maximum(jnp.sum(kk * kk, axis=-1, keepdims=True), 1e-24))
    k_mod = k.astype(f32) * (1.0 + (alpha - 1.0) * k_a)

    def heads(z):
        return z.astype(f32).reshape(B, T, H, N)

    r_h, k_h, v_h, w_h, al_h = heads(r), heads(k_mod), heads(v), heads(decay), heads(alpha)
    a_h = -kk
    b_h = kk * al_h

    def step(S, inp):
        r_t, w_t, k_t, v_t, a_t, b_t = inp
        Sa = jnp.einsum('bhij,bhj->bhi', S, a_t)
        S = S * w_t[:, :, None, :] + Sa[..., None] * b_t[:, :, None, :] + v_t[..., None] * k_t[:, :, None, :]
        y_t = jnp.einsum('bhij,bhj->bhi', S, r_t)
        return S, y_t

    tm = lambda z: jnp.moveaxis(z, 1, 0)
    S0 = jnp.zeros((B, H, N, N), f32)
    _, y = lax.scan(step, S0, (tm(r_h), tm(w_h), tm(k_h), tm(v_h), tm(a_h), tm(b_h)))
    y = jnp.moveaxis(y, 0, 1)
    mean = jnp.mean(y, axis=-1, keepdims=True)
    var = jnp.mean(jnp.square(y - mean), axis=-1, keepdims=True)
    y = (y - mean) * lax.rsqrt(var + GN_EPS) * ln_x_w.astype(f32).reshape(H, N) + ln_x_b.astype(f32).reshape(H, N)
    bonus = jnp.sum(r_h * k_h * r_k.astype(f32), axis=-1, keepdims=True) * v_h
    return (y + bonus).reshape(B, T, H * N).astype(r.dtype)


def fox_attention(q, k, v, log_f):
    B, H, T, D = q.shape
    n_blk = T // Q_BLOCK
    cum = jnp.cumsum(log_f.astype(jnp.float32), axis=-1)
    kpos = jnp.arange(T)
    scale = D ** -0.5

    def one_block(i):
        start = i * Q_BLOCK
        qb = lax.dynamic_slice_in_dim(q, start, Q_BLOCK, axis=2)
        cb = lax.dynamic_slice_in_dim(cum, start, Q_BLOCK, axis=2)
        s = jnp.einsum('bhqd,bhkd->bhqk', qb, k).astype(jnp.float32) * scale + cb[..., :, None] - cum[..., None, :]
        qpos = start + jnp.arange(Q_BLOCK)
        s = jnp.where(kpos[None, :] <= qpos[:, None], s, -jnp.inf)
        p = jax.nn.softmax(s, axis=-1)
        return jnp.einsum('bhqk,bhkd->bhqd', p.astype(v.dtype), v)

    out = lax.map(one_block, jnp.arange(n_blk))
    return jnp.moveaxis(out, 0, 2).reshape(B, H, T, D)


def memory_cross_attention(q, mem, g_mem, w_mem_kv):
    B, T, _ = q.shape
    M = mem.shape[1]
    mkv = rms_norm(mem, g_mem) @ w_mem_kv
    mk, mv = jnp.split(mkv, 2, axis=-1)
    qh = q.reshape(B, T, MEM_HEADS, MEM_HEAD_DIM)
    mk = mk.reshape(B, M, MEM_HEADS, MEM_HEAD_DIM)
    mv = mv.reshape(B, M, MEM_HEADS, MEM_HEAD_DIM)
    s = jnp.einsum('bthd,bmhd->bhtm', qh, mk).astype(jnp.float32) * (MEM_HEAD_DIM ** -0.5)
    p = jax.nn.softmax(s, axis=-1)
    o = jnp.einsum('bhtm,bmhd->bthd', p.astype(mv.dtype), mv)
    return o.reshape(B, T, MEM_WIDTH)


def hybrid_layer(x, mem, g_pre, w_in, mu_rwkv, w0, w_decay_up, a0, w_iclr_up, k_k, k_a, r_k,
                 ln_x_w, ln_x_b, b_f, g_mem, w_mem_kv, w_out, g_post):
    B, T, _ = x.shape
    h = rms_norm(x, g_pre)
    p = h @ w_in
    p_shift = token_shift(p[..., :RWKV_SHIFT_WIDTH], mu_rwkv)
    r, k, v, wl, al = jnp.split(p_shift, _offsets(RWKV_SHIFT_SPLITS), axis=-1)
    g_rwkv, fq, fk, fv, f_logit, g_fox, mq, g_mq = jnp.split(p[..., RWKV_SHIFT_WIDTH:], _offsets(REST_SPLITS), axis=-1)

    y_rwkv = rwkv7_branch(r, k, v, wl, al, w0, w_decay_up, a0, w_iclr_up, k_k, k_a, r_k, ln_x_w, ln_x_b)

    to_heads = lambda z: z.reshape(B, T, FOX_HEADS, HEAD_DIM).transpose(0, 2, 1, 3)
    log_f = jax.nn.log_sigmoid((f_logit + b_f).astype(jnp.float32)).transpose(0, 2, 1)
    y_fox = fox_attention(to_heads(fq), to_heads(fk), to_heads(fv), log_f)
    y_fox = y_fox.transpose(0, 2, 1, 3).reshape(B, T, FOX_WIDTH)

    y_mem = memory_cross_attention(mq, mem, g_mem, w_mem_kv)

    y = jnp.concatenate([y_rwkv * jax.nn.silu(g_rwkv),
                         y_fox * jax.nn.silu(g_fox),
                         y_mem * jax.nn.silu(g_mq)], axis=-1)
    y = y @ w_out
    return x + rms_norm(y, g_post)


def _fwd_setup_inputs(seed: int = 0) -> dict:
    key = jax.random.key(seed)
    ks = jax.random.split(key, 20)
    f32 = jnp.float32
    nrm = lambda k, shape, s: jax.random.normal(k, shape, f32) * s
    n = jnp.arange(RWKV_WIDTH, dtype=f32) / (RWKV_WIDTH - 1)
    return {
        "x": nrm(ks[0], (BATCH, SEQ, D_MODEL), 1.0),
        "mem": nrm(ks[1], (BATCH, N_MEM, D_MODEL), 1.0),
        "g_pre": 1.0 + nrm(ks[2], (DEPTH, D_MODEL), 0.02),
        "w_in": nrm(ks[3], (DEPTH, D_MODEL, IN_WIDTH), D_MODEL ** -0.5),
        "mu_rwkv": jax.random.uniform(ks[4], (DEPTH, RWKV_SHIFT_WIDTH), f32),
        "w0": (-5.5 + 5.0 * n ** 0.85)[None, :] + nrm(ks[5], (DEPTH, RWKV_WIDTH), 0.1),
        "w_decay_up": nrm(ks[6], (DEPTH, DECAY_LORA, RWKV_WIDTH), DECAY_LORA ** -0.5),
        "a0": nrm(ks[7], (DEPTH, RWKV_WIDTH), 0.1),
        "w_iclr_up": nrm(ks[8], (DEPTH, ICLR_LORA, RWKV_WIDTH), ICLR_LORA ** -0.5),
        "k_k": 0.85 + nrm(ks[9], (DEPTH, RWKV_WIDTH), 0.02),
        "k_a": 1.0 + nrm(ks[10], (DEPTH, RWKV_WIDTH), 0.02),
        "r_k": -0.04 + nrm(ks[11], (DEPTH, RWKV_HEADS, HEAD_DIM), 0.02),
        "ln_x_w": 1.0 + nrm(ks[12], (DEPTH, RWKV_WIDTH), 0.02),
        "ln_x_b": nrm(ks[13], (DEPTH, RWKV_WIDTH), 0.02),
        "b_f": 2.0 + nrm(ks[14], (DEPTH, FOX_HEADS), 0.5),
        "g_mem": 1.0 + nrm(ks[15], (DEPTH, D_MODEL), 0.02),
        "w_mem_kv": nrm(ks[16], (DEPTH, D_MODEL, 2 * MEM_WIDTH), D_MODEL ** -0.5),
        "w_out": nrm(ks[17], (DEPTH, MIX_WIDTH, D_MODEL), MIX_WIDTH ** -0.5),
        "g_post": 1.0 + nrm(ks[18], (DEPTH, D_MODEL), 0.02),
    }


def _fwd_reference(x, mem, g_pre, w_in, mu_rwkv, w0, w_decay_up, a0, w_iclr_up, k_k, k_a, r_k,
              ln_x_w, ln_x_b, b_f, g_mem, w_mem_kv, w_out, g_post):
    for l in range(DEPTH):
        x = hybrid_layer(x, mem, g_pre[l], w_in[l], mu_rwkv[l], w0[l], w_decay_up[l], a0[l],
                         w_iclr_up[l], k_k[l], k_a[l], r_k[l], ln_x_w[l], ln_x_b[l], b_f[l],
                         g_mem[l], w_mem_kv[l], w_out[l], g_post[l])
    return x


import jax as _jax
import jax.numpy as _jnp

TWIN_FORMAT = 'train_step'
FWD_PARAMS = ['x', 'mem', 'g_pre', 'w_in', 'mu_rwkv', 'w0', 'w_decay_up', 'a0', 'w_iclr_up', 'k_k', 'k_a', 'r_k', 'ln_x_w', 'ln_x_b', 'b_f', 'g_mem', 'w_mem_kv', 'w_out', 'g_post']
TWIN_WEIGHTS = ['g_pre', 'w_in', 'mu_rwkv', 'w0', 'w_decay_up', 'a0', 'w_iclr_up', 'k_k', 'k_a', 'r_k', 'ln_x_w', 'ln_x_b', 'b_f', 'g_mem', 'w_mem_kv', 'w_out', 'g_post']
TWIN_DIFF_INPUT = 'x'
TWIN_INPUTS = ['x', 'mem', 'g_pre', 'w_in', 'mu_rwkv', 'w0', 'w_decay_up', 'a0', 'w_iclr_up', 'k_k', 'k_a', 'r_k', 'ln_x_w', 'ln_x_b', 'b_f', 'g_mem', 'w_mem_kv', 'w_out', 'g_post', 'loss_target', 'm_g_pre', 'm_w_in', 'm_mu_rwkv', 'm_w0', 'm_w_decay_up', 'm_a0', 'm_w_iclr_up', 'm_k_k', 'm_k_a', 'm_r_k', 'm_ln_x_w', 'm_ln_x_b', 'm_b_f', 'm_g_mem', 'm_w_mem_kv', 'm_w_out', 'm_g_post', 'v_g_pre', 'v_w_in', 'v_mu_rwkv', 'v_w0', 'v_w_decay_up', 'v_a0', 'v_w_iclr_up', 'v_k_k', 'v_k_a', 'v_r_k', 'v_ln_x_w', 'v_ln_x_b', 'v_b_f', 'v_g_mem', 'v_w_mem_kv', 'v_w_out', 'v_g_post']
TWIN_OUTPUTS = ['loss', 'grad_x', 'grad_g_pre', 'grad_w_in', 'grad_mu_rwkv', 'grad_w0', 'grad_w_decay_up', 'grad_a0', 'grad_w_iclr_up', 'grad_k_k', 'grad_k_a', 'grad_r_k', 'grad_ln_x_w', 'grad_ln_x_b', 'grad_b_f', 'grad_g_mem', 'grad_w_mem_kv', 'grad_w_out', 'grad_g_post', 'delta_g_pre', 'delta_w_in', 'delta_mu_rwkv', 'delta_w0', 'delta_w_decay_up', 'delta_a0', 'delta_w_iclr_up', 'delta_k_k', 'delta_k_a', 'delta_r_k', 'delta_ln_x_w', 'delta_ln_x_b', 'delta_b_f', 'delta_g_mem', 'delta_w_mem_kv', 'delta_w_out', 'delta_g_post', 'new_m_g_pre', 'new_m_w_in', 'new_m_mu_rwkv', 'new_m_w0', 'new_m_w_decay_up', 'new_m_a0', 'new_m_w_iclr_up', 'new_m_k_k', 'new_m_k_a', 'new_m_r_k', 'new_m_ln_x_w', 'new_m_ln_x_b', 'new_m_b_f', 'new_m_g_mem', 'new_m_w_mem_kv', 'new_m_w_out', 'new_m_g_post', 'new_v_g_pre', 'new_v_w_in', 'new_v_mu_rwkv', 'new_v_w0', 'new_v_w_decay_up', 'new_v_a0', 'new_v_w_iclr_up', 'new_v_k_k', 'new_v_k_a', 'new_v_r_k', 'new_v_ln_x_w', 'new_v_ln_x_b', 'new_v_b_f', 'new_v_g_mem', 'new_v_w_mem_kv', 'new_v_w_out', 'new_v_g_post']
TWIN_LEAF_KINDS = {'loss': 'loss', 'grad_x': 'grad_x', 'grad_g_pre': 'grad_w', 'grad_w_in': 'grad_w', 'grad_mu_rwkv': 'grad_w', 'grad_w0': 'grad_w', 'grad_w_decay_up': 'grad_w', 'grad_a0': 'grad_w', 'grad_w_iclr_up': 'grad_w', 'grad_k_k': 'grad_w', 'grad_k_a': 'grad_w', 'grad_r_k': 'grad_w', 'grad_ln_x_w': 'grad_w', 'grad_ln_x_b': 'grad_w', 'grad_b_f': 'grad_w', 'grad_g_mem': 'grad_w', 'grad_w_mem_kv': 'grad_w', 'grad_w_out': 'grad_w', 'grad_g_post': 'grad_w', 'delta_g_pre': 'delta_w', 'delta_w_in': 'delta_w', 'delta_mu_rwkv': 'delta_w', 'delta_w0': 'delta_w', 'delta_w_decay_up': 'delta_w', 'delta_a0': 'delta_w', 'delta_w_iclr_up': 'delta_w', 'delta_k_k': 'delta_w', 'delta_k_a': 'delta_w', 'delta_r_k': 'delta_w', 'delta_ln_x_w': 'delta_w', 'delta_ln_x_b': 'delta_w', 'delta_b_f': 'delta_w', 'delta_g_mem': 'delta_w', 'delta_w_mem_kv': 'delta_w', 'delta_w_out': 'delta_w', 'delta_g_post': 'delta_w', 'new_m_g_pre': 'new_m', 'new_m_w_in': 'new_m', 'new_m_mu_rwkv': 'new_m', 'new_m_w0': 'new_m', 'new_m_w_decay_up': 'new_m', 'new_m_a0': 'new_m', 'new_m_w_iclr_up': 'new_m', 'new_m_k_k': 'new_m', 'new_m_k_a': 'new_m', 'new_m_r_k': 'new_m', 'new_m_ln_x_w': 'new_m', 'new_m_ln_x_b': 'new_m', 'new_m_b_f': 'new_m', 'new_m_g_mem': 'new_m', 'new_m_w_mem_kv': 'new_m', 'new_m_w_out': 'new_m', 'new_m_g_post': 'new_m', 'new_v_g_pre': 'new_v', 'new_v_w_in': 'new_v', 'new_v_mu_rwkv': 'new_v', 'new_v_w0': 'new_v', 'new_v_w_decay_up': 'new_v', 'new_v_a0': 'new_v', 'new_v_w_iclr_up': 'new_v', 'new_v_k_k': 'new_v', 'new_v_k_a': 'new_v', 'new_v_r_k': 'new_v', 'new_v_ln_x_w': 'new_v', 'new_v_ln_x_b': 'new_v', 'new_v_b_f': 'new_v', 'new_v_g_mem': 'new_v', 'new_v_w_mem_kv': 'new_v', 'new_v_w_out': 'new_v', 'new_v_g_post': 'new_v'}


def _forward(args):
    return _fwd_reference(*[args[k] for k in FWD_PARAMS])


def _output_shape():
    out = _jax.eval_shape(lambda: _forward(_fwd_setup_inputs(0)))
    return out.shape, out.dtype

N_MICROBATCH = 1
ADAM_LR = 0.001
ADAM_B1 = 0.9
ADAM_B2 = 0.999
ADAM_EPS = 1e-08
ADAM_WD = 0.01
ADAM_STEP = 10
PER_EXAMPLE_BATCH_AXIS = {'x': 0, 'mem': 0, 'loss_target': 0}
SHARED_INPUTS = []
_WEIGHT_DTYPES = {'g_pre': _jnp.float32, 'w_in': _jnp.float32, 'mu_rwkv': _jnp.float32, 'w0': _jnp.float32, 'w_decay_up': _jnp.float32, 'a0': _jnp.float32, 'w_iclr_up': _jnp.float32, 'k_k': _jnp.float32, 'k_a': _jnp.float32, 'r_k': _jnp.float32, 'ln_x_w': _jnp.float32, 'ln_x_b': _jnp.float32, 'b_f': _jnp.float32, 'g_mem': _jnp.float32, 'w_mem_kv': _jnp.float32, 'w_out': _jnp.float32, 'g_post': _jnp.float32}
MOMENT_SCALE = {'g_pre': 1.335957e-01, 'w_in': 6.988859e-02, 'mu_rwkv': 1.634136e-01, 'w0': 5.313498e-02, 'w_decay_up': 9.292234e-03, 'a0': 3.892179e-02, 'w_iclr_up': 3.461855e-02, 'k_k': 7.039460e-02, 'k_a': 1.027758e-01, 'r_k': 2.320820e-01, 'ln_x_w': 1.054325e-01, 'ln_x_b': 1.296429e-01, 'b_f': 1.569828e-01, 'g_mem': 7.411215e-03, 'w_mem_kv': 1.026368e-02, 'w_out': 6.287930e-02, 'g_post': 8.001974e+00}


def _to_microbatches(a, axis):
    t = _jnp.moveaxis(a, axis, 0)
    t = t.reshape((N_MICROBATCH, t.shape[0] // N_MICROBATCH) + t.shape[1:])
    return _jnp.moveaxis(t, 1, axis + 1)


def setup_inputs(seed: int = 0) -> dict:
    inp = _fwd_setup_inputs(seed)
    key = _jax.random.fold_in(_jax.random.key(seed), 7919)
    shape, _ = _output_shape()
    out = dict(inp)
    out["loss_target"] = _jax.random.normal(_jax.random.fold_in(key, 0), shape, _jnp.float32)
    for i, name in enumerate(TWIN_WEIGHTS):
        w = inp[name].astype(_jnp.float32)
        if MOMENT_SCALE is None:
            s = _jnp.sqrt(_jnp.mean(_jnp.square(w)) + 1e-30)
        else:
            s = MOMENT_SCALE[name]
        km, kv = _jax.random.split(_jax.random.fold_in(key, i + 1))
        out[name] = w
        out["m_" + name] = s * _jax.random.normal(km, w.shape, _jnp.float32)
        out["v_" + name] = (s * s) * _jax.random.uniform(kv, w.shape, _jnp.float32, 0.5, 1.5)
    if N_MICROBATCH > 1:
        for name, axis in PER_EXAMPLE_BATCH_AXIS.items():
            out[name] = _to_microbatches(out[name], axis)
    return {'x': out['x'], 'mem': out['mem'], 'g_pre': out['g_pre'], 'w_in': out['w_in'], 'mu_rwkv': out['mu_rwkv'], 'w0': out['w0'], 'w_decay_up': out['w_decay_up'], 'a0': out['a0'], 'w_iclr_up': out['w_iclr_up'], 'k_k': out['k_k'], 'k_a': out['k_a'], 'r_k': out['r_k'], 'ln_x_w': out['ln_x_w'], 'ln_x_b': out['ln_x_b'], 'b_f': out['b_f'], 'g_mem': out['g_mem'], 'w_mem_kv': out['w_mem_kv'], 'w_out': out['w_out'], 'g_post': out['g_post'], 'loss_target': out['loss_target'], 'm_g_pre': out['m_g_pre'], 'm_w_in': out['m_w_in'], 'm_mu_rwkv': out['m_mu_rwkv'], 'm_w0': out['m_w0'], 'm_w_decay_up': out['m_w_decay_up'], 'm_a0': out['m_a0'], 'm_w_iclr_up': out['m_w_iclr_up'], 'm_k_k': out['m_k_k'], 'm_k_a': out['m_k_a'], 'm_r_k': out['m_r_k'], 'm_ln_x_w': out['m_ln_x_w'], 'm_ln_x_b': out['m_ln_x_b'], 'm_b_f': out['m_b_f'], 'm_g_mem': out['m_g_mem'], 'm_w_mem_kv': out['m_w_mem_kv'], 'm_w_out': out['m_w_out'], 'm_g_post': out['m_g_post'], 'v_g_pre': out['v_g_pre'], 'v_w_in': out['v_w_in'], 'v_mu_rwkv': out['v_mu_rwkv'], 'v_w0': out['v_w0'], 'v_w_decay_up': out['v_w_decay_up'], 'v_a0': out['v_a0'], 'v_w_iclr_up': out['v_w_iclr_up'], 'v_k_k': out['v_k_k'], 'v_k_a': out['v_k_a'], 'v_r_k': out['v_r_k'], 'v_ln_x_w': out['v_ln_x_w'], 'v_ln_x_b': out['v_ln_x_b'], 'v_b_f': out['v_b_f'], 'v_g_mem': out['v_g_mem'], 'v_w_mem_kv': out['v_w_mem_kv'], 'v_w_out': out['v_w_out'], 'v_g_post': out['v_g_post']}


def _loss(weights, diff, rest, loss_target):
    with _jax.named_scope("forward"):
        args = {**rest, TWIN_DIFF_INPUT: diff, **{k: w.astype(_WEIGHT_DTYPES[k]) for k, w in weights.items()}}
        y = _forward(args)
    with _jax.named_scope("loss_head"):
        err = _jnp.square(y.astype(_jnp.float32) - loss_target)
        return 0.5 * _jnp.sum(_jnp.mean(err, axis=-1)) if err.ndim else 0.5 * err


def _adamw(w, g, m, v):
    m = ADAM_B1 * m + (1.0 - ADAM_B1) * g
    v = ADAM_B2 * v + (1.0 - ADAM_B2) * _jnp.square(g)
    m_hat = m / (1.0 - ADAM_B1 ** ADAM_STEP)
    v_hat = v / (1.0 - ADAM_B2 ** ADAM_STEP)
    delta = -ADAM_LR * (m_hat / (_jnp.sqrt(v_hat) + ADAM_EPS) + ADAM_WD * w)
    return delta, m, v


def reference(x, mem, g_pre, w_in, mu_rwkv, w0, w_decay_up, a0, w_iclr_up, k_k, k_a, r_k, ln_x_w, ln_x_b, b_f, g_mem, w_mem_kv, w_out, g_post, loss_target, m_g_pre, m_w_in, m_mu_rwkv, m_w0, m_w_decay_up, m_a0, m_w_iclr_up, m_k_k, m_k_a, m_r_k, m_ln_x_w, m_ln_x_b, m_b_f, m_g_mem, m_w_mem_kv, m_w_out, m_g_post, v_g_pre, v_w_in, v_mu_rwkv, v_w0, v_w_decay_up, v_a0, v_w_iclr_up, v_k_k, v_k_a, v_r_k, v_ln_x_w, v_ln_x_b, v_b_f, v_g_mem, v_w_mem_kv, v_w_out, v_g_post):
    given = dict(x=x, mem=mem, g_pre=g_pre, w_in=w_in, mu_rwkv=mu_rwkv, w0=w0, w_decay_up=w_decay_up, a0=a0, w_iclr_up=w_iclr_up, k_k=k_k, k_a=k_a, r_k=r_k, ln_x_w=ln_x_w, ln_x_b=ln_x_b, b_f=b_f, g_mem=g_mem, w_mem_kv=w_mem_kv, w_out=w_out, g_post=g_post, loss_target=loss_target, m_g_pre=m_g_pre, m_w_in=m_w_in, m_mu_rwkv=m_mu_rwkv, m_w0=m_w0, m_w_decay_up=m_w_decay_up, m_a0=m_a0, m_w_iclr_up=m_w_iclr_up, m_k_k=m_k_k, m_k_a=m_k_a, m_r_k=m_r_k, m_ln_x_w=m_ln_x_w, m_ln_x_b=m_ln_x_b, m_b_f=m_b_f, m_g_mem=m_g_mem, m_w_mem_kv=m_w_mem_kv, m_w_out=m_w_out, m_g_post=m_g_post, v_g_pre=v_g_pre, v_w_in=v_w_in, v_mu_rwkv=v_mu_rwkv, v_w0=v_w0, v_w_decay_up=v_w_decay_up, v_a0=v_a0, v_w_iclr_up=v_w_iclr_up, v_k_k=v_k_k, v_k_a=v_k_a, v_r_k=v_r_k, v_ln_x_w=v_ln_x_w, v_ln_x_b=v_ln_x_b, v_b_f=v_b_f, v_g_mem=v_g_mem, v_w_mem_kv=v_w_mem_kv, v_w_out=v_w_out, v_g_post=v_g_post)
    weights = {n: given[n] for n in TWIN_WEIGHTS}
    shared = {n: given[n] for n in SHARED_INPUTS}
    per_example = {n: given[n] for n in ['x', 'mem']}
    grad_fn = _jax.value_and_grad(_loss, argnums=(0, 1))

    def one_microbatch(ex, loss_target):
        ex = dict(ex)
        diff = ex.pop(TWIN_DIFF_INPUT)
        return grad_fn(weights, diff, {**shared, **ex}, loss_target)

    if N_MICROBATCH == 1:
        loss, (grad_w, grad_x) = one_microbatch(per_example, given["loss_target"])
    else:
        def body(carry, xs):
            loss_sum, grad_sum = carry
            l_k, (gw_k, gx_k) = one_microbatch(xs[0], xs[1])
            with _jax.named_scope("update"):
                return (loss_sum + l_k, _jax.tree.map(_jnp.add, grad_sum, gw_k)), gx_k

        init = (_jnp.zeros((), _jnp.float32), _jax.tree.map(_jnp.zeros_like, weights))
        (loss, grad_w), grad_x = _jax.lax.scan(body, init, (per_example, given["loss_target"]))
    with _jax.named_scope("update"):
        delta_w, new_m, new_v = {}, {}, {}
        for n in TWIN_WEIGHTS:
            delta_w[n], new_m[n], new_v[n] = _adamw(weights[n], grad_w[n], given["m_" + n], given["v_" + n])
    return (loss, grad_x, *[grad_w[n] for n in TWIN_WEIGHTS], *[delta_w[n] for n in TWIN_WEIGHTS],
            *[new_m[n] for n in TWIN_WEIGHTS], *[new_v[n] for n in TWIN_WEIGHTS])
```

```python
import functools

import jax
import jax.numpy as jnp
from jax import lax
from jax.experimental import pallas as pl
from jax.experimental.pallas import tpu as pltpu

F32 = jnp.float32
BF16 = jnp.bfloat16
HI = lax.Precision.HIGHEST
MESH = pl.DeviceIdType.MESH

HEAD_DIM = 64
MEM_HEADS = 4
LORA = 128
CHUNK = 64
RMS_EPS = 1e-6
GN_EPS = 64e-5
LANES = 128
VMEM_LIMIT = 56 * 1024 * 1024

ADAM_LR, ADAM_B1, ADAM_B2, ADAM_EPS, ADAM_WD, ADAM_STEP = 0.001, 0.9, 0.999, 1e-08, 0.01, 10


class Cfg:
    def __init__(self, d):
        self.d = d
        self.rw = 3 * d // 8
        self.mw = d // 4
        self.h = self.rw // HEAD_DIM
        self.mhd = self.mw // MEM_HEADS
        self.shift = 3 * self.rw + 2 * LORA
        self.in_width = self.shift + 5 * self.rw + self.h + 2 * self.mw
        o = self.shift
        self.o_grw = o; o += self.rw
        self.o_fq = o; o += self.rw
        self.o_fk = o; o += self.rw
        self.o_fv = o; o += self.rw
        self.o_gfox = o; o += self.rw
        self.o_mq = o; o += self.mw
        self.o_gmq = o; o += self.mw
        self.o_fl = o; o += LANES
        self.wp = o
        self.ref_fl = self.shift + 4 * self.rw


def _tile(n, pref, align=LANES):
    if n <= pref:
        return n
    t = (pref // align) * align
    while t >= align:
        if n % t == 0:
            return t
        t -= align
    return n


def _params(*sem):
    return pltpu.CompilerParams(dimension_semantics=sem, vmem_limit_bytes=VMEM_LIMIT)


def _sig(x):
    return 1.0 / (1.0 + jnp.exp(-x))


def _softplus(x):
    return jnp.maximum(x, 0.0) + jnp.log(1.0 + jnp.exp(-jnp.abs(x)))


def _dot(a, b, dims, prec=None):
    return lax.dot_general(a, b, (dims, ((), ())), precision=prec, preferred_element_type=F32)


def _mm(a, b, prec=None):
    return _dot(a, b, ((1,), (0,)), prec)


def _mm_nt(a, b, prec=None):
    return _dot(a, b, ((1,), (1,)), prec)


def _mm_tn(a, b, prec=None):
    return _dot(a, b, ((0,), (0,)), prec)


def _bmm(a, b):
    return lax.dot_general(a, b, (((2,), (1,)), ((0,), (0,))), precision=HI, preferred_element_type=F32)


def _bmm_nt(a, b):
    return lax.dot_general(a, b, (((2,), (2,)), ((0,), (0,))), precision=HI, preferred_element_type=F32)


def _bmm_tn(a, b):
    return lax.dot_general(a, b, (((1,), (1,)), ((0,), (0,))), precision=HI, preferred_element_type=F32)


def _matmul(a, b, *, ta=False, tb=False, out_dtype=F32, name, tm=1024, tn=1024, tk=1024):
    m, k = (a.shape[1], a.shape[0]) if ta else a.shape
    n = b.shape[0] if tb else b.shape[1]
    tm, tn, tk = _tile(m, tm), _tile(n, tn), _tile(k, tk)
    nk = k // tk
    dims = ((0 if ta else 1,), (1 if tb else 0,))

    def body(a_ref, b_ref, o_ref, acc):
        kk = pl.program_id(2)

        @pl.when(kk == 0)
        def _():
            acc[...] = jnp.zeros_like(acc)

        acc[...] += _dot(a_ref[...].astype(BF16), b_ref[...].astype(BF16), dims)

        @pl.when(kk == nk - 1)
        def _():
            o_ref[...] = acc[...].astype(o_ref.dtype)

    a_spec = pl.BlockSpec((tk, tm), lambda i, j, kk: (kk, i)) if ta else pl.BlockSpec((tm, tk), lambda i, j, kk: (i, kk))
    b_spec = pl.BlockSpec((tn, tk), lambda i, j, kk: (j, kk)) if tb else pl.BlockSpec((tk, tn), lambda i, j, kk: (kk, j))
    return pl.pallas_call(
        body, name=name, grid=(m // tm, n // tn, nk),
        in_specs=[a_spec, b_spec], out_specs=pl.BlockSpec((tm, tn), lambda i, j, kk: (i, j)),
        out_shape=jax.ShapeDtypeStruct((m, n), out_dtype),
        scratch_shapes=[pltpu.VMEM((tm, tn), F32)],
        compiler_params=_params("parallel", "parallel", "arbitrary"),
    )(a, b)


def _rms_fwd(x, g, name):
    t, d = x.shape
    tm = _tile(t, 256, 8)

    def body(x_ref, g_ref, h_ref, r_ref):
        xv = x_ref[...]
        r = lax.rsqrt(jnp.mean(xv * xv, axis=-1, keepdims=True) + RMS_EPS)
        h_ref[...] = (xv * r * g_ref[...]).astype(BF16)
        r_ref[...] = r

    return pl.pallas_call(
        body, name=name, grid=(t // tm,),
        in_specs=[pl.BlockSpec((tm, d), lambda i: (i, 0)), pl.BlockSpec((1, d), lambda i: (0, 0))],
        out_specs=[pl.BlockSpec((tm, d), lambda i: (i, 0)), pl.BlockSpec((tm, 1), lambda i: (i, 0))],
        out_shape=[jax.ShapeDtypeStruct((t, d), BF16), jax.ShapeDtypeStruct((t, 1), F32)],
        compiler_params=_params("parallel"),
    )(x, g)


def _rms_bwd(dh, x, rinv, g, add, name):
    t, d = x.shape
    tm = _tile(t, 256, 8)

    def body(dh_ref, x_ref, r_ref, g_ref, add_ref, dx_ref, dg_ref):
        @pl.when(pl.program_id(0) == 0)
        def _():
            dg_ref[...] = jnp.zeros_like(dg_ref)

        r = r_ref[...]
        xn = x_ref[...] * r
        dhv = dh_ref[...]
        dg_ref[...] += jnp.sum(dhv * xn, axis=0, keepdims=True)
        dxn = dhv * g_ref[...]
        dx_ref[...] = add_ref[...] + r * (dxn - xn * jnp.mean(dxn * xn, axis=-1, keepdims=True))

    row = pl.BlockSpec((tm, d), lambda i: (i, 0))
    vec = pl.BlockSpec((1, d), lambda i: (0, 0))
    return pl.pallas_call(
        body, name=name, grid=(t // tm,),
        in_specs=[row, row, pl.BlockSpec((tm, 1), lambda i: (i, 0)), vec, row],
        out_specs=[row, vec],
        out_shape=[jax.ShapeDtypeStruct((t, d), F32), jax.ShapeDtypeStruct((1, d), F32)],
        compiler_params=_params("arbitrary"),
    )(dh, x, rinv, g, add)


def _post_loss(yo, x, tgt, g, name):
    t, d = x.shape
    tm = _tile(t, 256, 8)

    def body(yo_ref, x_ref, t_ref, g_ref, loss_ref, dout_ref, dyo_ref, dg_ref):
        @pl.when(pl.program_id(0) == 0)
        def _():
            dg_ref[...] = jnp.zeros_like(dg_ref)
            loss_ref[...] = jnp.zeros_like(loss_ref)

        yv = yo_ref[...]
        r = lax.rsqrt(jnp.mean(yv * yv, axis=-1, keepdims=True) + RMS_EPS)
        n = yv * r
        err = x_ref[...] + n * g_ref[...] - t_ref[...]
        loss_ref[...] += 0.5 * jnp.sum(jnp.mean(err * err, axis=-1, keepdims=True), axis=0, keepdims=True)
        dout = err * (1.0 / d)
        dout_ref[...] = dout
        dg_ref[...] += jnp.sum(dout * n, axis=0, keepdims=True)
        dn = dout * g_ref[...]
        dyo_ref[...] = (r * (dn - n * jnp.mean(dn * n, axis=-1, keepdims=True))).astype(BF16)

    row = pl.BlockSpec((tm, d), lambda i: (i, 0))
    vec = pl.BlockSpec((1, d), lambda i: (0, 0))
    return pl.pallas_call(
        body, name=name, grid=(t // tm,),
        in_specs=[row, row, row, vec],
        out_specs=[pl.BlockSpec((1, 1), lambda i: (0, 0)), row, row, vec],
        out_shape=[jax.ShapeDtypeStruct((1, 1), F32), jax.ShapeDtypeStruct((t, d), F32),
                   jax.ShapeDtypeStruct((t, d), BF16), jax.ShapeDtypeStruct((1, d), F32)],
        compiler_params=_params("arbitrary"),
    )(yo, x, tgt, g)


def _head_sum(x):
    ri = lax.broadcasted_iota(jnp.int32, (LANES, LANES), 0) // HEAD_DIM
    ci = lax.broadcasted_iota(jnp.int32, (LANES, LANES), 1) // HEAD_DIM
    e = (ri == ci).astype(F32)
    parts = [_mm(x[:, i * LANES:(i + 1) * LANES], e, HI) for i in range(x.shape[1] // LANES)]
    return parts[0] if len(parts) == 1 else jnp.concatenate(parts, axis=1)


def _shifted(p_cur, before, first, mu):
    rolled = pltpu.roll(p_cur, 1, 0)
    prev_row = jnp.where(first, 0.0, before)
    row0 = lax.broadcasted_iota(jnp.int32, p_cur.shape, 0) == 0
    prev = jnp.where(row0, prev_row, rolled)
    return p_cur + (prev - p_cur) * mu, prev


def _rwkv_features(ps, rw, w0, a0, k_k, k_a, wd, wi):
    r, k, v = ps[:, 0:rw], ps[:, rw:2 * rw], ps[:, 2 * rw:3 * rw]
    wl, al = ps[:, 3 * rw:3 * rw + LORA], ps[:, 3 * rw + LORA:3 * rw + 2 * LORA]
    tw = jnp.tanh(wl)
    zw = w0 + _mm(tw.astype(BF16), wd)
    logw = -jnp.exp(-_softplus(-zw) - 0.5)
    alpha = _sig(a0 + _mm(al.astype(BF16), wi))
    kkr = k * k_k
    n2 = _head_sum(kkr * kkr)
    rn = lax.rsqrt(jnp.maximum(n2, 1e-24))
    kk = kkr * rn
    kmod = k * (1.0 + (alpha - 1.0) * k_a)
    return dict(r=r, k=k, v=v, tw=tw, al=al, zw=zw, logw=logw, alpha=alpha, kk=kk, rn=rn, n2=n2, kmod=kmod)


def _rwkv_pre_fwd(p, c, mu, w0, a0, k_k, k_a, wd, wi):
    t = p.shape[0]
    tm = _tile(t, 128, 8)
    rw, sh = c.rw, c.shift

    def body(p_ref, pp_ref, mu_ref, w0_ref, a0_ref, kk_ref, ka_ref, wd_ref, wi_ref,
             r_ref, lw_ref, km_ref, v_ref, a_ref, b_ref):
        ps, _ = _shifted(p_ref[...], pp_ref[7:8, :], pl.program_id(0) == 0, mu_ref[...])
        f = _rwkv_features(ps, rw, w0_ref[...], a0_ref[...], kk_ref[...], ka_ref[...], wd_ref[...], wi_ref[...])
        r_ref[...] = f["r"]
        lw_ref[...] = f["logw"]
        km_ref[...] = f["kmod"]
        v_ref[...] = f["v"]
        a_ref[...] = -f["kk"]
        b_ref[...] = f["kk"] * f["alpha"]

    vec = lambda n: pl.BlockSpec((1, n), lambda i: (0, 0))
    out = pl.BlockSpec((tm, rw), lambda i: (i, 0))
    return pl.pallas_call(
        body, name="rwkv_pre_fwd", grid=(t // tm,),
        in_specs=[pl.BlockSpec((tm, sh), lambda i: (i, 0)),
                  pl.BlockSpec((8, sh), lambda i: (jnp.maximum(i * (tm // 8) - 1, 0), 0)),
                  vec(sh), vec(rw), vec(rw), vec(rw), vec(rw),
                  pl.BlockSpec((LORA, rw), lambda i: (0, 0)), pl.BlockSpec((LORA, rw), lambda i: (0, 0))],
        out_specs=[out] * 6,
        out_shape=[jax.ShapeDtypeStruct((t, rw), F32)] * 6,
        compiler_params=_params("parallel"),
    )(p, p, mu, w0, a0, k_k, k_a, wd, wi)


def _rwkv_pre_bwd(p, c, mu, w0, a0, k_k, k_a, wd, wi, dr, dlw, dkm, dv, da, db, dr2, dkm2, dv2):
    t = p.shape[0]
    tm = _tile(t, 128, 8)
    rw, sh = c.rw, c.shift

    def body(p_ref, pp_ref, mu_ref, w0_ref, a0_ref, kk_ref, ka_ref, wd_ref, wi_ref,
             dr_ref, dlw_ref, dkm_ref, dv_ref, da_ref, db_ref, dr2_ref, dkm2_ref, dv2_ref,
             dps_ref, dzw_ref, dza_ref, tw_ref, al_ref, dw0_ref, da0_ref, dkk_ref, dka_ref):
        @pl.when(pl.program_id(0) == 0)
        def _():
            for ref in (dw0_ref, da0_ref, dkk_ref, dka_ref):
                ref[...] = jnp.zeros_like(ref)

        ps, _ = _shifted(p_ref[...], pp_ref[7:8, :], pl.program_id(0) == 0, mu_ref[...])
        k_k, k_a = kk_ref[...], ka_ref[...]
        f = _rwkv_features(ps, rw, w0_ref[...], a0_ref[...], k_k, k_a, wd_ref[...], wi_ref[...])
        alpha, kk, k = f["alpha"], f["kk"], f["k"]
        dkm = dkm_ref[...] + dkm2_ref[...]
        db = db_ref[...]
        dkk = db * alpha - da_ref[...]
        dalpha = db * kk + dkm * k * k_a
        dk = dkm * (1.0 + (alpha - 1.0) * k_a)
        dka_ref[...] += jnp.sum(dkm * k * (alpha - 1.0), axis=0, keepdims=True)
        dkkr = f["rn"] * jnp.where(f["n2"] > 1e-24, dkk - kk * _head_sum(dkk * kk), dkk)
        dk = dk + dkkr * k_k
        dkk_ref[...] += jnp.sum(dkkr * k, axis=0, keepdims=True)
        dza = dalpha * alpha * (1.0 - alpha)
        da0_ref[...] += jnp.sum(dza, axis=0, keepdims=True)
        dzw = dlw_ref[...] * f["logw"] * _sig(-f["zw"])
        dw0_ref[...] += jnp.sum(dzw, axis=0, keepdims=True)
        dza_b, dzw_b = dza.astype(BF16), dzw.astype(BF16)
        dal = _mm_nt(dza_b, wi_ref[...])
        dwl = _mm_nt(dzw_b, wd_ref[...]) * (1.0 - f["tw"] * f["tw"])
        dps_ref[:, 0:rw] = dr_ref[...] + dr2_ref[...]
        dps_ref[:, rw:2 * rw] = dk
        dps_ref[:, 2 * rw:3 * rw] = dv_ref[...] + dv2_ref[...]
        dps_ref[:, 3 * rw:3 * rw + LORA] = dwl
        dps_ref[:, 3 * rw + LORA:sh] = dal
        dzw_ref[...] = dzw_b
        dza_ref[...] = dza_b
        tw_ref[...] = f["tw"].astype(BF16)
        al_ref[...] = f["al"].astype(BF16)

    vec = lambda n: pl.BlockSpec((1, n), lambda i: (0, 0))
    blk = lambda n: pl.BlockSpec((tm, n), lambda i: (i, 0))
    return pl.pallas_call(
        body, name="rwkv_pre_bwd", grid=(t // tm,),
        in_specs=[blk(sh), pl.BlockSpec((8, sh), lambda i: (jnp.maximum(i * (tm // 8) - 1, 0), 0)),
                  vec(sh), vec(rw), vec(rw), vec(rw), vec(rw),
                  pl.BlockSpec((LORA, rw), lambda i: (0, 0)), pl.BlockSpec((LORA, rw), lambda i: (0, 0))]
                 + [blk(rw)] * 9,
        out_specs=[blk(sh), blk(rw), blk(rw), blk(LORA), blk(LORA), vec(rw), vec(rw), vec(rw), vec(rw)],
        out_shape=[jax.ShapeDtypeStruct((t, sh), F32), jax.ShapeDtypeStruct((t, rw), BF16),
                   jax.ShapeDtypeStruct((t, rw), BF16), jax.ShapeDtypeStruct((t, LORA), BF16),
                   jax.ShapeDtypeStruct((t, LORA), BF16)] + [jax.ShapeDtypeStruct((1, rw), F32)] * 4,
        compiler_params=_params("arbitrary"),
    )(p, p, mu, w0, a0, k_k, k_a, wd, wi, dr, dlw, dkm, dv, da, db, dr2, dkm2, dv2)


def _shift_bwd(dps, p, c, mu):
    t = p.shape[0]
    tm = _tile(t, 256, 8)
    sh = c.shift
    nt = t // tm

    def body(d_ref, dn_ref, p_ref, pp_ref, mu_ref, dp_ref, dmu_ref):
        i = pl.program_id(0)

        @pl.when(i == 0)
        def _():
            dmu_ref[...] = jnp.zeros_like(dmu_ref)

        mu = mu_ref[...]
        d = d_ref[...]
        pc = p_ref[...]
        _, prev = _shifted(pc, pp_ref[7:8, :], i == 0, mu)
        dmu_ref[...] += jnp.sum(d * (prev - pc), axis=0, keepdims=True)
        nxt_row = jnp.where(i == nt - 1, 0.0, dn_ref[0:1, :])
        last = lax.broadcasted_iota(jnp.int32, d.shape, 0) == tm - 1
        nxt = jnp.where(last, nxt_row, pltpu.roll(d, tm - 1, 0))
        dp_ref[...] = (d * (1.0 - mu) + nxt * mu).astype(BF16)

    blk = pl.BlockSpec((tm, sh), lambda i: (i, 0))
    return pl.pallas_call(
        body, name="shift_bwd", grid=(nt,),
        in_specs=[blk, pl.BlockSpec((8, sh), lambda i: (jnp.minimum((i + 1) * (tm // 8), t // 8 - 1), 0)),
                  blk, pl.BlockSpec((8, sh), lambda i: (jnp.maximum(i * (tm // 8) - 1, 0), 0)),
                  pl.BlockSpec((1, sh), lambda i: (0, 0))],
        out_specs=[blk, pl.BlockSpec((1, sh), lambda i: (0, 0))],
        out_shape=[jax.ShapeDtypeStruct((t, sh), BF16), jax.ShapeDtypeStruct((1, sh), F32)],
        compiler_params=_params("arbitrary"),
    )(dps, dps, p, p, mu)


def _tri(n, strict):
    ri = lax.broadcasted_iota(jnp.int32, (n, n), 0)
    ci = lax.broadcasted_iota(jnp.int32, (n, n), 1)
    return (ri > ci) if strict else (ri >= ci)


def _unit_lower_inverse(a):
    n = a.shape[-1]
    ri = lax.broadcasted_iota(jnp.int32, (n, n), 0)
    ci = lax.broadcasted_iota(jnp.int32, (n, n), 1)
    eye = (ri == ci).astype(F32)
    blk = lambda s: (ri // s) == (ci // s)
    ad = jnp.where(blk(16), a, 0.0)
    p = eye + ad
    for _ in range(3):
        ad = _bmm(ad, ad)
        p = p + _bmm(p, ad)
    s = 16
    while s < n:
        off = jnp.where(blk(2 * s) & ~blk(s), a, 0.0)
        p = p + _bmm(_bmm(p, off), p)
        s *= 2
    return p


def _chunk_common(r, lw, k, a, b):
    n = r.shape[1]
    tri_incl = jnp.broadcast_to(_tri(n, False).astype(F32), (r.shape[0], n, n))
    cum = _bmm(tri_incl, lw)
    e_pos, e_neg, e_exc = jnp.exp(cum), jnp.exp(-cum), jnp.exp(cum - lw)
    last = lax.broadcasted_iota(jnp.int32, (n, r.shape[2]), 0) == n - 1
    g_last = jnp.exp(jnp.sum(jnp.where(last, cum, 0.0), axis=1, keepdims=True))
    return g_last, r * e_pos, a * e_exc, b * e_neg, k * e_neg, e_pos, e_neg, e_exc


def _chunk_solve(rt, at, bt, kt, v, g0):
    strict, incl = _tri(rt.shape[1], True), _tri(rt.shape[1], False)
    a_ab = jnp.where(strict, _bmm_nt(at, bt), 0.0)
    a_ak = jnp.where(strict, _bmm_nt(at, kt), 0.0)
    a_rb = jnp.where(incl, _bmm_nt(rt, bt), 0.0)
    a_rk = jnp.where(incl, _bmm_nt(rt, kt), 0.0)
    tinv = _unit_lower_inverse(a_ab)
    u = _bmm(tinv, _bmm(at, g0) + _bmm(a_ak, v))
    return a_ab, a_ak, a_rb, a_rk, tinv, u


def _diag_col(row, n):
    ri = lax.broadcasted_iota(jnp.int32, (n, n), 0)
    ci = lax.broadcasted_iota(jnp.int32, (n, n), 1)
    return jnp.sum(jnp.where(ri == ci, row, 0.0), axis=2, keepdims=True)


def _diag_row(col, n):
    ri = lax.broadcasted_iota(jnp.int32, (n, n), 0)
    ci = lax.broadcasted_iota(jnp.int32, (n, n), 1)
    return jnp.sum(jnp.where(ri == ci, col, 0.0), axis=1, keepdims=True)


def _rwkv_scan_fwd(r, lw, k, v, a, b, hb):
    h, t, n = r.shape
    nc = t // CHUNK

    def body(r_ref, lw_ref, k_ref, v_ref, a_ref, b_ref, y_ref, st_ref, g_sc):
        @pl.when(pl.program_id(1) == 0)
        def _():
            g_sc[...] = jnp.zeros_like(g_sc)

        g0 = g_sc[...]
        st_ref[0] = g0
        vv = v_ref[...]
        g_last, rt, at, bt, kt, _, _, _ = _chunk_common(r_ref[...], lw_ref[...], k_ref[...], a_ref[...], b_ref[...])
        _, _, a_rb, a_rk, _, u = _chunk_solve(rt, at, bt, kt, vv, g0)
        y_ref[...] = _bmm(rt, g0) + _bmm(a_rb, u) + _bmm(a_rk, vv)
        z = g0 + _bmm_tn(bt, u) + _bmm_tn(kt, vv)
        g_sc[...] = _diag_col(g_last, n) * z

    blk = pl.BlockSpec((hb, CHUNK, n), lambda i, j: (i, j, 0))
    return pl.pallas_call(
        body, name="rwkv_scan_fwd", grid=(h // hb, nc),
        in_specs=[blk] * 6,
        out_specs=[blk, pl.BlockSpec((1, hb, n, n), lambda i, j: (j, i, 0, 0))],
        out_shape=[jax.ShapeDtypeStruct((h, t, n), F32), jax.ShapeDtypeStruct((nc, h, n, n), F32)],
        scratch_shapes=[pltpu.VMEM((hb, n, n), F32)],
        compiler_params=_params("parallel", "arbitrary"),
    )(r, lw, k, v, a, b)


def _rwkv_scan_bwd(r, lw, k, v, a, b, states, dy, hb):
    h, t, n = r.shape
    nc = t // CHUNK

    def body(r_ref, lw_ref, k_ref, v_ref, a_ref, b_ref, st_ref, dy_ref,
             dr_ref, dlw_ref, dk_ref, dv_ref, da_ref, db_ref, dg_sc):
        @pl.when(pl.program_id(1) == 0)
        def _():
            dg_sc[...] = jnp.zeros_like(dg_sc)

        g0 = st_ref[0]
        vv, dyv, dh = v_ref[...], dy_ref[...], dg_sc[...]
        lwv = lw_ref[...]
        g_last, rt, at, bt, kt, e_pos, e_neg, e_exc = _chunk_common(r_ref[...], lwv, k_ref[...], a_ref[...], b_ref[...])
        a_ab, a_ak, a_rb, a_rk, tinv, u = _chunk_solve(rt, at, bt, kt, vv, g0)
        strict, incl = _tri(CHUNK, True), _tri(CHUNK, False)
        gcol = _diag_col(g_last, n)
        z = g0 + _bmm_tn(bt, u) + _bmm_tn(kt, vv)
        dz = gcol * dh
        dc_last = _diag_row(jnp.sum(dh * gcol * z, axis=2, keepdims=True), n)
        du = _bmm_tn(a_rb, dyv) + _bmm(bt, dz)
        dx = _bmm_tn(tinv, du)
        dv_ref[...] = _bmm_tn(a_rk, dyv) + _bmm(kt, dz) + _bmm_tn(a_ak, dx)
        da_ab = jnp.where(strict, _bmm_nt(dx, u), 0.0)
        da_ak = jnp.where(strict, _bmm_nt(dx, vv), 0.0)
        da_rb = jnp.where(incl, _bmm_nt(dyv, u), 0.0)
        da_rk = jnp.where(incl, _bmm_nt(dyv, vv), 0.0)
        d_at = _bmm(da_ab, bt) + _bmm(da_ak, kt) + _bmm_nt(dx, g0)
        d_rt = _bmm(da_rb, bt) + _bmm(da_rk, kt) + _bmm_nt(dyv, g0)
        d_bt = _bmm_tn(da_ab, at) + _bmm_tn(da_rb, rt) + _bmm_nt(u, dz)
        d_kt = _bmm_tn(da_ak, at) + _bmm_tn(da_rk, rt) + _bmm_nt(vv, dz)
        dg_sc[...] = dz + _bmm_tn(rt, dyv) + _bmm_tn(at, dx)
        dr_ref[...] = d_rt * e_pos
        da_ref[...] = d_at * e_exc
        db_ref[...] = d_bt * e_neg
        dk_ref[...] = d_kt * e_neg
        last = lax.broadcasted_iota(jnp.int32, (CHUNK, n), 0) == CHUNK - 1
        dc = d_rt * rt - d_bt * bt - d_kt * kt + jnp.where(last, dc_last, 0.0)
        dce = d_at * at
        up_incl = jnp.broadcast_to(_tri(CHUNK, False).astype(F32).T, (hb, CHUNK, CHUNK))
        dlw_ref[...] = _bmm(up_incl, dc + dce) - dce

    rev = lambda i, j: (i, nc - 1 - j, 0)
    blk = pl.BlockSpec((hb, CHUNK, n), rev)
    return pl.pallas_call(
        body, name="rwkv_scan_bwd", grid=(h // hb, nc),
        in_specs=[blk] * 6 + [pl.BlockSpec((1, hb, n, n), lambda i, j: (nc - 1 - j, i, 0, 0)), blk],
        out_specs=[blk] * 6,
        out_shape=[jax.ShapeDtypeStruct((h, t, n), F32)] * 6,
        scratch_shapes=[pltpu.VMEM((hb, n, n), F32)],
        compiler_params=_params("parallel", "arbitrary"),
    )(r, lw, k, v, a, b, states, dy)


def _silu_grad(g):
    s = _sig(g)
    return s * (1.0 + g * (1.0 - s))


def _group_norm(ys):
    yc = ys - _head_sum(ys) * (1.0 / HEAD_DIM)
    rstd = lax.rsqrt(_head_sum(yc * yc) * (1.0 / HEAD_DIM) + GN_EPS)
    return yc * rstd, rstd


def _rwkv_post_fwd(ys, r, km, v, p, c, ln_w, ln_b, r_k):
    t = ys.shape[0]
    tm = _tile(t, 512, 8)
    goff = c.o_grw // LANES

    def body(ys_ref, r_ref, km_ref, v_ref, g_ref, lw_ref, lb_ref, rk_ref, o_ref):
        yn, _ = _group_norm(ys_ref[...])
        s = _head_sum(r_ref[...] * km_ref[...] * rk_ref[...])
        g = g_ref[...]
        o_ref[...] = ((yn * lw_ref[...] + lb_ref[...] + s * v_ref[...]) * g * _sig(g)).astype(BF16)

    blk = pl.BlockSpec((tm, LANES), lambda i, j: (i, j))
    vec = pl.BlockSpec((1, LANES), lambda i, j: (0, j))
    return pl.pallas_call(
        body, name="rwkv_post_fwd", grid=(t // tm, c.rw // LANES),
        in_specs=[blk] * 4 + [pl.BlockSpec((tm, LANES), lambda i, j: (i, goff + j)), vec, vec, vec],
        out_specs=blk, out_shape=jax.ShapeDtypeStruct((t, c.rw), BF16),
        compiler_params=_params("parallel", "parallel"),
    )(ys, r, km, v, p, ln_w, ln_b, r_k)


def _rwkv_post_bwd(dyc, ys, r, km, v, p, c, ln_w, ln_b, r_k):
    t = ys.shape[0]
    tm = _tile(t, 512, 8)
    goff = c.o_grw // LANES

    def body(dy_ref, ys_ref, r_ref, km_ref, v_ref, g_ref, lw_ref, lb_ref, rk_ref,
             dys_ref, dr_ref, dkm_ref, dv_ref, dg_ref, dlw_ref, dlb_ref, drk_ref):
        @pl.when(pl.program_id(1) == 0)
        def _():
            for ref in (dlw_ref, dlb_ref, drk_ref):
                ref[...] = jnp.zeros_like(ref)

        yn, rstd = _group_norm(ys_ref[...])
        rv, kmv, vv, rk, g = r_ref[...], km_ref[...], v_ref[...], rk_ref[...], g_ref[...]
        s = _head_sum(rv * kmv * rk)
        y = yn * lw_ref[...] + lb_ref[...] + s * vv
        dyc = dy_ref[...]
        dg_ref[...] = (dyc * y * _silu_grad(g)).astype(BF16)
        dy = dyc * g * _sig(g)
        dlb_ref[...] += jnp.sum(dy, axis=0, keepdims=True)
        dlw_ref[...] += jnp.sum(dy * yn, axis=0, keepdims=True)
        dyn = dy * lw_ref[...]
        inv = 1.0 / HEAD_DIM
        dys_ref[...] = rstd * (dyn - _head_sum(dyn) * inv - yn * _head_sum(dyn * yn) * inv)
        ds = _head_sum(dy * vv)
        dv_ref[...] = dy * s
        dr_ref[...] = ds * kmv * rk
        dkm_ref[...] = ds * rv * rk
        drk_ref[...] += jnp.sum(ds * rv * kmv, axis=0, keepdims=True)

    blk = pl.BlockSpec((tm, LANES), lambda j, i: (i, j))
    vec = pl.BlockSpec((1, LANES), lambda j, i: (0, j))
    f = jax.ShapeDtypeStruct((t, c.rw), F32)
    s1 = jax.ShapeDtypeStruct((1, c.rw), F32)
    return pl.pallas_call(
        body, name="rwkv_post_bwd", grid=(c.rw // LANES, t // tm),
        in_specs=[blk] * 5 + [pl.BlockSpec((tm, LANES), lambda j, i: (i, goff + j)), vec, vec, vec],
        out_specs=[blk] * 5 + [vec] * 3,
        out_shape=[f, f, f, f, jax.ShapeDtypeStruct((t, c.rw), BF16), s1, s1, s1],
        compiler_params=_params("parallel", "arbitrary"),
    )(dyc, ys, r, km, v, p, ln_w, ln_b, r_k)


def _gate_fwd(y, p, goff, name):
    t, w = y.shape
    tm = _tile(t, 512, 8)
    gb = goff // LANES

    def body(y_ref, g_ref, o_ref):
        g = g_ref[...]
        o_ref[...] = (y_ref[...] * g * _sig(g)).astype(BF16)

    blk = pl.BlockSpec((tm, LANES), lambda i, j: (i, j))
    return pl.pallas_call(
        body, name=name, grid=(t // tm, w // LANES),
        in_specs=[blk, pl.BlockSpec((tm, LANES), lambda i, j: (i, gb + j))],
        out_specs=blk, out_shape=jax.ShapeDtypeStruct((t, w), BF16),
        compiler_params=_params("parallel", "parallel"),
    )(y, p)


def _gate_bwd(dyc, yoff, y, p, goff, name):
    t, w = y.shape
    tm = _tile(t, 512, 8)
    gb, yb = goff // LANES, yoff // LANES

    def body(d_ref, y_ref, g_ref, dy_ref, dg_ref):
        g, d = g_ref[...], d_ref[...]
        dy_ref[...] = d * g * _sig(g)
        dg_ref[...] = (d * y_ref[...] * _silu_grad(g)).astype(BF16)

    blk = pl.BlockSpec((tm, LANES), lambda i, j: (i, j))
    return pl.pallas_call(
        body, name=name, grid=(t // tm, w // LANES),
        in_specs=[pl.BlockSpec((tm, LANES), lambda i, j: (i, yb + j)), blk,
                  pl.BlockSpec((tm, LANES), lambda i, j: (i, gb + j))],
        out_specs=[blk, blk],
        out_shape=[jax.ShapeDtypeStruct((t, w), F32), jax.ShapeDtypeStruct((t, w), BF16)],
        compiler_params=_params("parallel", "parallel"),
    )(dyc, y, p)


NEG = -1e30


def _fox_prep(p, c, b_f):
    t = p.shape[0]
    tm = _tile(t, 512, 8)
    fb = c.o_fl // LANES

    def body(f_ref, b_ref, o_ref, carry):
        @pl.when(pl.program_id(0) == 0)
        def _():
            carry[...] = jnp.zeros_like(carry)

        logf = -_softplus(-(f_ref[...] + b_ref[...]))
        cum = _mm(_tri(tm, False).astype(F32), logf, HI) + carry[...]
        o_ref[...] = cum
        carry[...] += jnp.sum(logf, axis=0, keepdims=True)

    return pl.pallas_call(
        body, name="fox_prep", grid=(t // tm,),
        in_specs=[pl.BlockSpec((tm, LANES), lambda i: (i, fb)), pl.BlockSpec((1, LANES), lambda i: (0, 0))],
        out_specs=pl.BlockSpec((tm, LANES), lambda i: (i, 0)),
        out_shape=jax.ShapeDtypeStruct((t, LANES), F32),
        scratch_shapes=[pltpu.VMEM((1, LANES), F32)],
        compiler_params=_params("arbitrary"),
    )(p, b_f)


def _fox_logit_bwd(dcum, p, c, b_f):
    t = p.shape[0]
    tm = _tile(t, 512, 8)
    fb = c.o_fl // LANES
    nt = t // tm

    def body(d_ref, f_ref, b_ref, o_ref, db_ref, carry):
        @pl.when(pl.program_id(0) == 0)
        def _():
            carry[...] = jnp.zeros_like(carry)
            db_ref[...] = jnp.zeros_like(db_ref)

        d = d_ref[0] + d_ref[1]
        dlogf = _mm(_tri(tm, False).astype(F32).T, d, HI) + carry[...]
        carry[...] += jnp.sum(d, axis=0, keepdims=True)
        df = dlogf * _sig(-(f_ref[...] + b_ref[...]))
        o_ref[...] = df.astype(BF16)
        db_ref[...] += jnp.sum(df, axis=0, keepdims=True)

    return pl.pallas_call(
        body, name="fox_logit_bwd", grid=(nt,),
        in_specs=[pl.BlockSpec((2, tm, LANES), lambda i: (0, nt - 1 - i, 0)),
                  pl.BlockSpec((tm, LANES), lambda i: (nt - 1 - i, fb)),
                  pl.BlockSpec((1, LANES), lambda i: (0, 0))],
        out_specs=[pl.BlockSpec((tm, LANES), lambda i: (nt - 1 - i, 0)), pl.BlockSpec((1, LANES), lambda i: (0, 0))],
        out_shape=[jax.ShapeDtypeStruct((t, LANES), BF16), jax.ShapeDtypeStruct((1, LANES), F32)],
        scratch_shapes=[pltpu.VMEM((1, LANES), F32)],
        compiler_params=_params("arbitrary"),
    )(dcum, p, b_f)


def _fox_scores(q, k, cq, ck, qi, ki, tq, tk):
    s = _mm_nt((q * (HEAD_DIM ** -0.5)).astype(BF16), k.astype(BF16)) + cq - ck
    qpos = qi * tq + lax.broadcasted_iota(jnp.int32, (tq, tk), 0)
    kpos = ki * tk + lax.broadcasted_iota(jnp.int32, (tq, tk), 1)
    mask = kpos <= qpos
    return jnp.where(mask, s, NEG), mask


def _fox_fwd(q, k, v, cq, ck, hb, tb):
    h, t, n = q.shape
    tq = tk = _tile(t, tb, LANES)
    nq = t // tq

    def body(q_ref, k_ref, v_ref, cq_ref, ck_ref, o_ref, lse_ref, m_sc, l_sc, acc_sc):
        qi, ki = pl.program_id(1), pl.program_id(2)

        @pl.when(ki == 0)
        def _():
            m_sc[...] = jnp.full_like(m_sc, NEG)
            l_sc[...] = jnp.zeros_like(l_sc)
            acc_sc[...] = jnp.zeros_like(acc_sc)

        @pl.when(ki <= qi)
        def _():
            for i in range(hb):
                s, _ = _fox_scores(q_ref[i], k_ref[i], cq_ref[i], ck_ref[i], qi, ki, tq, tk)
                m_old = m_sc[i]
                m_new = jnp.maximum(m_old, jnp.max(s, axis=1, keepdims=True))
                scale = jnp.exp(m_old - m_new)
                e = jnp.exp(s - m_new)
                l_sc[i] = scale * l_sc[i] + jnp.sum(e, axis=1, keepdims=True)
                acc_sc[i] = scale * acc_sc[i] + _mm(e.astype(BF16), v_ref[i].astype(BF16))
                m_sc[i] = m_new

        @pl.when(ki == qi)
        def _():
            o_ref[...] = acc_sc[...] / l_sc[...]
            lse_ref[...] = m_sc[...] + jnp.log(l_sc[...])

    qb = pl.BlockSpec((hb, tq, n), lambda g, i, j: (g, i, 0))
    kb = pl.BlockSpec((hb, tk, n), lambda g, i, j: (g, jnp.minimum(i, j), 0))
    col = pl.BlockSpec((hb, tq, 1), lambda g, i, j: (g, i, 0))
    return pl.pallas_call(
        body, name="fox_fwd", grid=(h // hb, nq, nq),
        in_specs=[qb, kb, kb, col, pl.BlockSpec((hb, 1, tk), lambda g, i, j: (g, 0, jnp.minimum(i, j)))],
        out_specs=[qb, col],
        out_shape=[jax.ShapeDtypeStruct((h, t, n), F32), jax.ShapeDtypeStruct((h, t, 1), F32)],
        scratch_shapes=[pltpu.VMEM((hb, tq, 1), F32), pltpu.VMEM((hb, tq, 1), F32), pltpu.VMEM((hb, tq, n), F32)],
        compiler_params=_params("parallel", "parallel", "arbitrary"),
    )(q, k, v, cq, ck)


def _fox_bwd_dq(q, k, v, cq, ck, lse, o, do, hb, tb):
    h, t, n = q.shape
    tq = tk = _tile(t, tb, LANES)
    nq = t // tq

    def body(q_ref, k_ref, v_ref, cq_ref, ck_ref, lse_ref, o_ref, do_ref, dq_ref, dcq_ref, acc_sc, row_sc):
        qi, ki = pl.program_id(1), pl.program_id(2)

        @pl.when(ki == 0)
        def _():
            acc_sc[...] = jnp.zeros_like(acc_sc)
            row_sc[...] = jnp.zeros_like(row_sc)

        @pl.when(ki <= qi)
        def _():
            for i in range(hb):
                s, mask = _fox_scores(q_ref[i], k_ref[i], cq_ref[i], ck_ref[i], qi, ki, tq, tk)
                dov = do_ref[i]
                delta = jnp.sum(dov * o_ref[i], axis=1, keepdims=True)
                pm = jnp.where(mask, jnp.exp(s - lse_ref[i]), 0.0)
                dp = _mm_nt(dov.astype(BF16), v_ref[i].astype(BF16))
                ds = pm * (dp - delta)
                acc_sc[i] += _mm(ds.astype(BF16), k_ref[i].astype(BF16))
                row_sc[i] += jnp.sum(ds, axis=1, keepdims=True)

        @pl.when(ki == qi)
        def _():
            dq_ref[...] = acc_sc[...] * (HEAD_DIM ** -0.5)
            dcq_ref[...] = row_sc[...]

    qb = pl.BlockSpec((hb, tq, n), lambda g, i, j: (g, i, 0))
    kb = pl.BlockSpec((hb, tk, n), lambda g, i, j: (g, jnp.minimum(i, j), 0))
    col = pl.BlockSpec((hb, tq, 1), lambda g, i, j: (g, i, 0))
    return pl.pallas_call(
        body, name="fox_bwd_dq", grid=(h // hb, nq, nq),
        in_specs=[qb, kb, kb, col, pl.BlockSpec((hb, 1, tk), lambda g, i, j: (g, 0, jnp.minimum(i, j))), col, qb, qb],
        out_specs=[qb, col],
        out_shape=[jax.ShapeDtypeStruct((h, t, n), F32), jax.ShapeDtypeStruct((h, t, 1), F32)],
        scratch_shapes=[pltpu.VMEM((hb, tq, n), F32), pltpu.VMEM((hb, tq, 1), F32)],
        compiler_params=_params("parallel", "parallel", "arbitrary"),
    )(q, k, v, cq, ck, lse, o, do)


def _fox_bwd_dkv(q, k, v, cq, ck, lse, o, do, hb, tb):
    h, t, n = q.shape
    tq = tk = _tile(t, tb, LANES)
    nq = t // tq

    def body(q_ref, k_ref, v_ref, cq_ref, ck_ref, lse_ref, o_ref, do_ref, dk_ref, dv_ref, dck_ref, dk_sc, dv_sc, dc_sc):
        ki, qi = pl.program_id(1), pl.program_id(2)

        @pl.when(qi == 0)
        def _():
            dk_sc[...] = jnp.zeros_like(dk_sc)
            dv_sc[...] = jnp.zeros_like(dv_sc)
            dc_sc[...] = jnp.zeros_like(dc_sc)

        @pl.when(qi >= ki)
        def _():
            for i in range(hb):
                s, mask = _fox_scores(q_ref[i], k_ref[i], cq_ref[i], ck_ref[i], qi, ki, tq, tk)
                dov = do_ref[i]
                delta = jnp.sum(dov * o_ref[i], axis=1, keepdims=True)
                pm = jnp.where(mask, jnp.exp(s - lse_ref[i]), 0.0)
                dob = dov.astype(BF16)
                dp = _mm_nt(dob, v_ref[i].astype(BF16))
                ds = pm * (dp - delta)
                dv_sc[i] += _mm_tn(pm.astype(BF16), dob)
                dk_sc[i] += _mm_tn(ds.astype(BF16), q_ref[i].astype(BF16))
                dc_sc[i] -= jnp.sum(ds, axis=0, keepdims=True)

        @pl.when(qi == nq - 1)
        def _():
            dk_ref[...] = dk_sc[...] * (HEAD_DIM ** -0.5)
            dv_ref[...] = dv_sc[...]
            dck_ref[...] = dc_sc[...]

    qb = pl.BlockSpec((hb, tq, n), lambda g, j, i: (g, jnp.maximum(i, j), 0))
    kb = pl.BlockSpec((hb, tk, n), lambda g, j, i: (g, j, 0))
    col = pl.BlockSpec((hb, tq, 1), lambda g, j, i: (g, jnp.maximum(i, j), 0))
    row = pl.BlockSpec((hb, 1, tk), lambda g, j, i: (g, 0, j))
    return pl.pallas_call(
        body, name="fox_bwd_dkv", grid=(h // hb, nq, nq),
        in_specs=[qb, kb, kb, col, row, col, qb, qb],
        out_specs=[kb, kb, row],
        out_shape=[jax.ShapeDtypeStruct((h, t, n), F32), jax.ShapeDtypeStruct((h, t, n), F32),
                   jax.ShapeDtypeStruct((h, 1, t), F32)],
        scratch_shapes=[pltpu.VMEM((hb, tk, n), F32), pltpu.VMEM((hb, tk, n), F32), pltpu.VMEM((hb, 1, tk), F32)],
        compiler_params=_params("parallel", "parallel", "arbitrary"),
    )(q, k, v, cq, ck, lse, o, do)


def _mem_probs(q, mk, scale):
    s = _mm_nt(q.astype(BF16), mk.astype(BF16)) * scale
    e = jnp.exp(s - jnp.max(s, axis=1, keepdims=True))
    return e / jnp.sum(e, axis=1, keepdims=True)


def _mem_attn_fwd(p, c, mkv):
    t = p.shape[0]
    tm = _tile(t, 512, 8)
    dh = c.mhd
    qb = c.o_mq // dh
    scale = dh ** -0.5

    def body(q_ref, mk_ref, mv_ref, o_ref):
        pm = _mem_probs(q_ref[...], mk_ref[...], scale)
        o_ref[...] = _mm(pm.astype(BF16), mv_ref[...].astype(BF16))

    m = mkv.shape[0]
    return pl.pallas_call(
        body, name="mem_attn_fwd", grid=(t // tm, MEM_HEADS),
        in_specs=[pl.BlockSpec((tm, dh), lambda i, j: (i, qb + j)),
                  pl.BlockSpec((m, dh), lambda i, j: (0, j)),
                  pl.BlockSpec((m, dh), lambda i, j: (0, MEM_HEADS + j))],
        out_specs=pl.BlockSpec((tm, dh), lambda i, j: (i, j)),
        out_shape=jax.ShapeDtypeStruct((t, c.mw), F32),
        compiler_params=_params("parallel", "parallel"),
    )(p, mkv, mkv)


def _mem_attn_bwd(p, c, mkv, do):
    t = p.shape[0]
    tm = _tile(t, 512, 8)
    dh = c.mhd
    qb = c.o_mq // dh
    scale = dh ** -0.5
    m = mkv.shape[0]

    def body(q_ref, mk_ref, mv_ref, do_ref, dq_ref, dmk_ref, dmv_ref):
        @pl.when(pl.program_id(1) == 0)
        def _():
            dmk_ref[...] = jnp.zeros_like(dmk_ref)
            dmv_ref[...] = jnp.zeros_like(dmv_ref)

        qv = q_ref[...].astype(BF16)
        pm = _mem_probs(qv, mk_ref[...], scale)
        dob = do_ref[...].astype(BF16)
        dmv_ref[...] += _mm_tn(pm.astype(BF16), dob)
        dp = _mm_nt(dob, mv_ref[...].astype(BF16))
        ds = (pm * (dp - jnp.sum(pm * dp, axis=1, keepdims=True)) * scale).astype(BF16)
        dq_ref[...] = _mm(ds, mk_ref[...].astype(BF16)).astype(BF16)
        dmk_ref[...] += _mm_tn(ds, qv)

    kvb = lambda off: pl.BlockSpec((m, dh), lambda j, i: (0, off + j))
    return pl.pallas_call(
        body, name="mem_attn_bwd", grid=(MEM_HEADS, t // tm),
        in_specs=[pl.BlockSpec((tm, dh), lambda j, i: (i, qb + j)), kvb(0), kvb(MEM_HEADS),
                  pl.BlockSpec((tm, dh), lambda j, i: (i, j))],
        out_specs=[pl.BlockSpec((tm, dh), lambda j, i: (i, j)), kvb(0), kvb(0)],
        out_shape=[jax.ShapeDtypeStruct((t, c.mw), BF16), jax.ShapeDtypeStruct((m, c.mw), F32),
                   jax.ShapeDtypeStruct((m, c.mw), F32)],
        compiler_params=_params("parallel", "arbitrary"),
    )(p, mkv, mkv, do)


def _adamw(w, g, m, v, name):
    rows, cols = w.shape
    tm = _tile(rows, max(8, (1 << 18) // cols // 8 * 8), 8)
    bc1 = 1.0 - ADAM_B1 ** ADAM_STEP
    bc2 = 1.0 - ADAM_B2 ** ADAM_STEP

    def body(w_ref, g_ref, m_ref, v_ref, go_ref, d_ref, mo_ref, vo_ref):
        gv = g_ref[:, 0:cols]
        mn = ADAM_B1 * m_ref[...] + (1.0 - ADAM_B1) * gv
        vn = ADAM_B2 * v_ref[...] + (1.0 - ADAM_B2) * (gv * gv)
        go_ref[...] = gv
        mo_ref[...] = mn
        vo_ref[...] = vn
        d_ref[...] = -ADAM_LR * ((mn / bc1) / (jnp.sqrt(vn / bc2) + ADAM_EPS) + ADAM_WD * w_ref[...])

    blk = pl.BlockSpec((tm, cols), lambda i: (i, 0))
    shp = jax.ShapeDtypeStruct((rows, cols), F32)
    return pl.pallas_call(
        body, name=name, grid=(rows // tm,),
        in_specs=[blk, pl.BlockSpec((tm, g.shape[1]), lambda i: (i, 0)), blk, blk],
        out_specs=[blk] * 4, out_shape=[shp] * 4,
        compiler_params=_params("parallel"),
    )(w, g, m, v)


SCAN_HEADS = 4
FOX_BLOCK = 512


def _local_step(c, x, mem, tgt, w):
    t = x.shape[0]
    rw = c.rw
    hd = lambda z: z.reshape(t, c.h, HEAD_DIM).transpose(1, 0, 2)
    uh = lambda z: z.transpose(1, 0, 2).reshape(t, rw)
    vecs = (w["mu"], w["w0"], w["a0"], w["k_k"], w["k_a"], w["wd"], w["wi"])

    h, rinv = _rms_fwd(x, w["g_pre"], "rms_pre")
    p = _matmul(h, w["wp"], name="in_proj")
    r, lw, km, v, a, b = _rwkv_pre_fwd(p, c, *vecs)
    scan_in = tuple(hd(z) for z in (r, lw, km, v, a, b))
    ysh, states = _rwkv_scan_fwd(*scan_in, SCAN_HEADS)
    ys = uh(ysh)
    yc_r = _rwkv_post_fwd(ys, r, km, v, p, c, w["ln_w"], w["ln_b"], w["r_k"])

    cumh = _fox_prep(p, c, w["b_f"])[:, :c.h].T
    cq, ck = cumh[:, :, None], cumh[:, None, :]
    fq, fk, fv = (hd(p[:, o:o + rw]) for o in (c.o_fq, c.o_fk, c.o_fv))
    ofh, lse = _fox_fwd(fq, fk, fv, cq, ck, SCAN_HEADS, FOX_BLOCK)
    yfox = uh(ofh)
    yc_f = _gate_fwd(yfox, p, c.o_gfox, "gate_fox")

    memn, rinv_m = _rms_fwd(mem, w["g_mem"], "rms_mem")
    mkv = _matmul(memn, w["w_mem_kv"], name="mem_kv")
    ymem = _mem_attn_fwd(p, c, mkv)
    yc_m = _gate_fwd(ymem, p, c.o_gmq, "gate_mem")

    ycat = jnp.concatenate([yc_r, yc_f, yc_m], axis=1)
    yo = _matmul(ycat, w["w_out"], name="out_proj")
    loss, dout, dyo, dg_post = _post_loss(yo, x, tgt, w["g_post"], "post_loss")

    dyc = _matmul(dyo, w["w_out"], tb=True, name="d_ycat")
    dw_out = _matmul(ycat, dyo, ta=True, name="d_w_out")
    dys, dr2, dkm2, dv2, dg_r, dln_w, dln_b, dr_k = _rwkv_post_bwd(
        dyc, ys, r, km, v, p, c, w["ln_w"], w["ln_b"], w["r_k"])
    dyf, dg_f = _gate_bwd(dyc, rw, yfox, p, c.o_gfox, "gate_fox_bwd")
    dym, dg_m = _gate_bwd(dyc, 2 * rw, ymem, p, c.o_gmq, "gate_mem_bwd")

    scan_g = _rwkv_scan_bwd(*scan_in, states, hd(dys), SCAN_HEADS)
    dps, dzw, dza, twb, alb, dw0, da0, dk_k, dk_a = _rwkv_pre_bwd(
        p, c, *vecs, *(uh(z) for z in scan_g), dr2, dkm2, dv2)
    dwd = _matmul(twb, dzw, ta=True, name="d_w_decay")
    dwi = _matmul(alb, dza, ta=True, name="d_w_iclr")
    dp_shift, dmu = _shift_bwd(dps, p, c, w["mu"])

    doh = hd(dyf)
    dqh, dcq = _fox_bwd_dq(fq, fk, fv, cq, ck, lse, ofh, doh, SCAN_HEADS, FOX_BLOCK)
    dkh, dvh, dck = _fox_bwd_dkv(fq, fk, fv, cq, ck, lse, ofh, doh, SCAN_HEADS, FOX_BLOCK)
    dcum = jnp.pad(jnp.stack([dcq.reshape(c.h, t).T, dck.reshape(c.h, t).T]), ((0, 0), (0, 0), (0, LANES - c.h)))
    dfl, db_f = _fox_logit_bwd(dcum, p, c, w["b_f"])

    dmq, dmk, dmv = _mem_attn_bwd(p, c, mkv, dym)
    dmkv = jnp.concatenate([dmk, dmv], axis=1)
    dw_mkv = _matmul(memn, dmkv, ta=True, name="d_w_mem_kv")
    dmemn = _matmul(dmkv, w["w_mem_kv"], tb=True, name="d_memn")
    _, dg_mem = _rms_bwd(dmemn, mem, rinv_m, w["g_mem"], jnp.zeros_like(mem), "rms_mem_bwd")

    dp = jnp.concatenate([dp_shift, dg_r] + [uh(z).astype(BF16) for z in (dqh, dkh, dvh)]
                         + [dg_f, dmq, dg_m, dfl], axis=1)
    dh = _matmul(dp, w["wp"], tb=True, name="d_h")
    dwp = _matmul(h, dp, ta=True, name="d_w_in")
    grad_x, dg_pre = _rms_bwd(dh, x, rinv, w["g_pre"], dout, "rms_pre_bwd")

    small = dict(g_pre=dg_pre, mu=dmu, w0=dw0, a0=da0, k_k=dk_k, k_a=dk_a, r_k=dr_k, ln_w=dln_w, ln_b=dln_b,
                 b_f=db_f, g_mem=dg_mem, g_post=dg_post)
    return loss, grad_x, dict(wp=dwp, wd=dwd, wi=dwi, w_mem_kv=dw_mkv, w_out=dw_out), small


CHIPS = ((1, 0, 0), (0, 1, 0), (1, 1, 0))
SIBLING = ((0, 0, 1),)
ALL_PEERS = tuple((i, j, k) for i in (0, 1) for j in (0, 1) for k in (0, 1))[1:]


def _chip_of(pos):
    return 2 * pos[0] + pos[1]


def _exchange(name, groups):
    n = len(groups)
    n_remote = sum(len(g["masks"]) for g in groups)

    def body(*refs):
        srcs, outs = refs[:n], refs[n:2 * n]
        send_sems, recv_sems, local_sems = refs[2 * n:]
        me = (lax.axis_index("x"), lax.axis_index("y"), lax.axis_index("c"))
        copies = []
        k = 0
        for gi, g in enumerate(groups):
            dst = outs[gi].at[g["slot"](me)]
            copies.append(pltpu.make_async_copy(g["view"](srcs[gi], me, me), dst, local_sems.at[gi]))
            for mask in g["masks"]:
                peer = tuple(1 - v if f else v for v, f in zip(me, mask))
                copies.append(pltpu.make_async_remote_copy(
                    src_ref=g["view"](srcs[gi], me, peer), dst_ref=dst,
                    send_sem=send_sems.at[k], recv_sem=recv_sems.at[k],
                    device_id=peer, device_id_type=MESH))
                k += 1
        for cp in copies:
            cp.start()
        for cp in copies:
            cp.wait()

    any_spec = pl.BlockSpec(memory_space=pl.ANY)
    return pl.pallas_call(
        body, name=name,
        in_specs=[any_spec] * n, out_specs=[any_spec] * n,
        out_shape=[jax.ShapeDtypeStruct((g["slots"],) + tuple(g["piece"]), g["src"].dtype) for g in groups],
        scratch_shapes=[pltpu.SemaphoreType.DMA((n_remote,)), pltpu.SemaphoreType.DMA((n_remote,)),
                        pltpu.SemaphoreType.DMA((n,))],
    )(*[g["src"] for g in groups])


def _whole(ref, me, peer):
    return ref


def _sum_slots(a, out_dtype, name):
    s, rows, cols = a.shape
    budget = max(16, (4 * 1024 * 1024 // (s * cols * a.dtype.itemsize)) // 16 * 16)
    tr = _tile(rows, budget, 16)

    def body(a_ref, o_ref):
        acc = a_ref[0].astype(F32)
        for i in range(1, s):
            acc = acc + a_ref[i].astype(F32)
        o_ref[...] = acc.astype(o_ref.dtype)

    return pl.pallas_call(
        body, name=name, grid=(rows // tr,),
        in_specs=[pl.BlockSpec((s, tr, cols), lambda i: (0, i, 0))],
        out_specs=pl.BlockSpec((tr, cols), lambda i: (i, 0)),
        out_shape=jax.ShapeDtypeStruct((rows, cols), out_dtype),
        compiler_params=_params("parallel"),
    )(a)


def _all_gather(shards):
    groups = [dict(src=s, slots=4, piece=s.shape, masks=CHIPS, view=_whole, slot=_chip_of) for s in shards]
    return _exchange("gather_weights", groups)


def _reduce_scatter(partials):
    halves = [q.reshape(4, 2, q.shape[1] // 2, q.shape[2]).transpose(1, 0, 2, 3) for q in partials]
    pair = _exchange("reduce_pair", [
        dict(src=q, slots=2, piece=q.shape[1:], masks=SIBLING, view=lambda ref, me, peer: ref.at[peer[2]],
             slot=lambda me: me[2]) for q in halves])
    chip_sums = [_sum_slots(e.reshape(2, -1, e.shape[-1]), BF16, "reduce_pair_sum").reshape(e.shape[1:])
                 for e in pair]
    crossed = _exchange("reduce_chips", [
        dict(src=q, slots=4, piece=q.shape[1:], masks=CHIPS, view=lambda ref, me, peer: ref.at[_chip_of(peer)],
             slot=_chip_of) for q in chip_sums])
    sums = [_sum_slots(e, F32, "reduce_chips_sum") for e in crossed]
    swapped = _exchange("reduce_swap", [
        dict(src=q, slots=2, piece=q.shape, masks=SIBLING, view=_whole, slot=lambda me: me[2]) for q in sums])
    return [e.reshape(-1, e.shape[-1]) for e in swapped]


def _all_reduce_small(vec):
    got = _exchange("reduce_small", [
        dict(src=vec, slots=8, piece=vec.shape, masks=ALL_PEERS, view=_whole,
             slot=lambda me: 4 * me[0] + 2 * me[1] + me[2])])[0]
    return _sum_slots(got, F32, "reduce_small_sum")


SMALL = ("g_pre", "mu", "w0", "a0", "k_k", "k_a", "r_k", "ln_w", "ln_b", "b_f", "g_mem", "g_post")


def _pad_cols(a, n):
    return jnp.pad(a, ((0, 0),) * (a.ndim - 1) + ((0, n - a.shape[-1]),))


def kernel(x, mem, g_pre, w_in, mu_rwkv, w0, w_decay_up, a0, w_iclr_up, k_k, k_a, r_k, ln_x_w, ln_x_b, b_f, g_mem, w_mem_kv, w_out, g_post, loss_target, m_g_pre, m_w_in, m_mu_rwkv, m_w0, m_w_decay_up, m_a0, m_w_iclr_up, m_k_k, m_k_a, m_r_k, m_ln_x_w, m_ln_x_b, m_b_f, m_g_mem, m_w_mem_kv, m_w_out, m_g_post, v_g_pre, v_w_in, v_mu_rwkv, v_w0, v_w_decay_up, v_a0, v_w_iclr_up, v_k_k, v_k_a, v_r_k, v_ln_x_w, v_ln_x_b, v_b_f, v_g_mem, v_w_mem_kv, v_w_out, v_g_post):
    d = x.shape[-1]
    c = Cfg(d)
    ws = w_in.shape[-1]
    wpad = -(-ws // LANES) * LANES
    nh = c.h

    g_in, g_out, g_mkv, g_wd, g_wi = _all_gather([
        _pad_cols(w_in[0].astype(BF16), wpad), w_out[0].astype(BF16), w_mem_kv[0].astype(BF16),
        w_decay_up[0].astype(BF16), w_iclr_up[0].astype(BF16)])
    w_full = jnp.concatenate([g_in[s, :, :ws] for s in range(4)], axis=1)
    fl = c.ref_fl
    wp = jnp.concatenate([w_full[:, :fl], w_full[:, fl + nh:], _pad_cols(w_full[:, fl:fl + nh], LANES)], axis=1)
    unshard = lambda g: g.transpose(1, 0, 2).reshape(g.shape[1], -1)
    weights = dict(wp=wp, w_out=g_out.reshape(-1, d), w_mem_kv=g_mkv.reshape(d, -1), wd=unshard(g_wd), wi=unshard(g_wi),
                   g_pre=g_pre, mu=mu_rwkv, w0=w0, a0=a0, k_k=k_k, k_a=k_a, r_k=r_k.reshape(1, -1),
                   ln_w=ln_x_w, ln_b=ln_x_b, b_f=_pad_cols(b_f, LANES), g_mem=g_mem, g_post=g_post)

    loss, grad_x, big, small = _local_step(c, x[0], mem[0], loss_target[0], weights)

    dwp = big["wp"]
    dw_full = jnp.concatenate([dwp[:, :fl], dwp[:, c.o_fl:c.o_fl + nh], dwp[:, fl:c.o_fl]], axis=1)
    by_chip = lambda g: jnp.stack(jnp.split(g, 4, axis=1))
    red = _reduce_scatter([
        jnp.stack([_pad_cols(dw_full[:, s * ws:(s + 1) * ws], wpad) for s in range(4)]).astype(BF16),
        big["w_out"].reshape(4, -1, d).astype(BF16), big["w_mem_kv"].reshape(4, d // 4, -1).astype(BF16),
        by_chip(big["wd"]).astype(BF16), by_chip(big["wi"]).astype(BF16)])
    big_w = (w_in[0], w_out[0], w_mem_kv[0], w_decay_up[0], w_iclr_up[0])
    big_m = (m_w_in[0], m_w_out[0], m_w_mem_kv[0], m_w_decay_up[0], m_w_iclr_up[0])
    big_v = (v_w_in[0], v_w_out[0], v_w_mem_kv[0], v_w_decay_up[0], v_w_iclr_up[0])
    big_names = ("w_in", "w_out", "w_mem_kv", "w_decay_up", "w_iclr_up")
    upd = {n: _adamw(w_, g_, m_, v_, "adamw_" + n) for n, w_, g_, m_, v_ in zip(big_names, big_w, red, big_m, big_v)}

    small_w = dict(g_pre=g_pre, mu=mu_rwkv, w0=w0, a0=a0, k_k=k_k, k_a=k_a, r_k=r_k.reshape(1, -1), ln_w=ln_x_w,
                   ln_b=ln_x_b, b_f=b_f, g_mem=g_mem, g_post=g_post)
    small_m = dict(g_pre=m_g_pre, mu=m_mu_rwkv, w0=m_w0, a0=m_a0, k_k=m_k_k, k_a=m_k_a, r_k=m_r_k.reshape(1, -1),
                   ln_w=m_ln_x_w, ln_b=m_ln_x_b, b_f=m_b_f, g_mem=m_g_mem, g_post=m_g_post)
    small_v = dict(g_pre=v_g_pre, mu=v_mu_rwkv, w0=v_w0, a0=v_a0, k_k=v_k_k, k_a=v_k_a, r_k=v_r_k.reshape(1, -1),
                   ln_w=v_ln_x_w, ln_b=v_ln_x_b, b_f=v_b_f, g_mem=v_g_mem, g_post=v_g_post)
    widths = [-(-small_w[n].shape[1] // LANES) * LANES for n in SMALL]
    pack = lambda t: jnp.concatenate([_pad_cols(t[n], wd_) for n, wd_ in zip(SMALL, widths)]
                                     + [jnp.zeros((1, LANES), F32)], axis=1)
    g_packed = jnp.concatenate([_pad_cols(small[n], wd_) for n, wd_ in zip(SMALL, widths)]
                               + [_pad_cols(loss, LANES)], axis=1)
    g_sum = _all_reduce_small(g_packed)
    s_upd = _adamw(pack(small_w), g_sum, pack(small_m), pack(small_v), "adamw_small")
    offs = [sum(widths[:i]) for i in range(len(SMALL))]

    def take(kind, n):
        i = SMALL.index(n)
        piece = s_upd[kind][:, offs[i]:offs[i] + small_w[n].shape[1]]
        return piece.reshape(r_k.shape) if n == "r_k" else piece

    total_loss = g_sum[0, sum(widths)]
    order = ("g_pre", "w_in", "mu", "w0", "w_decay_up", "a0", "w_iclr_up", "k_k", "k_a", "r_k", "ln_w", "ln_b", "b_f",
             "g_mem", "w_mem_kv", "w_out", "g_post")
    outs = [total_loss, grad_x[None]]
    for kind in range(4):
        for n in order:
            outs.append(upd[n][kind][None] if n in upd else take(kind, n))
    return tuple(outs)
```

```python
import functools

import jax
import jax.numpy as jnp
from jax import lax
from jax.experimental import pallas as pl
from jax.experimental.pallas import tpu as pltpu

F32 = jnp.float32
BF16 = jnp.bfloat16
HI = lax.Precision.HIGHEST
MESH = pl.DeviceIdType.MESH

HEAD_DIM = 64
MEM_HEADS = 4
LORA = 128
CHUNK = 64
RMS_EPS = 1e-6
GN_EPS = 64e-5
LANES = 128
VMEM_LIMIT = 56 * 1024 * 1024

ADAM_LR, ADAM_B1, ADAM_B2, ADAM_EPS, ADAM_WD, ADAM_STEP = 0.001, 0.9, 0.999, 1e-08, 0.01, 10


class Cfg:
    def __init__(self, d):
        self.d = d
        self.rw = 3 * d // 8
        self.mw = d // 4
        self.h = self.rw // HEAD_DIM
        self.mhd = self.mw // MEM_HEADS
        self.shift = 3 * self.rw + 2 * LORA
        self.in_width = self.shift + 5 * self.rw + self.h + 2 * self.mw
        o = self.shift
        self.o_grw = o; o += self.rw
        self.o_fq = o; o += self.rw
        self.o_fk = o; o += self.rw
        self.o_fv = o; o += self.rw
        self.o_gfox = o; o += self.rw
        self.o_mq = o; o += self.mw
        self.o_gmq = o; o += self.mw
        self.o_fl = o; o += LANES
        self.wp = o
        self.ref_fl = self.shift + 4 * self.rw


def _tile(n, pref, align=LANES):
    if n <= pref:
        return n
    t = (pref // align) * align
    while t >= align:
        if n % t == 0:
            return t
        t -= align
    return n


def _params(*sem):
    return pltpu.CompilerParams(dimension_semantics=sem, vmem_limit_bytes=VMEM_LIMIT)


def _sig(x):
    return 1.0 / (1.0 + jnp.exp(-x))


def _softplus(x):
    return jnp.maximum(x, 0.0) + jnp.log(1.0 + jnp.exp(-jnp.abs(x)))


def _dot(a, b, dims, prec=None):
    return lax.dot_general(a, b, (dims, ((), ())), precision=prec, preferred_element_type=F32)


def _mm(a, b, prec=None):
    return _dot(a, b, ((1,), (0,)), prec)


def _mm_nt(a, b, prec=None):
    return _dot(a, b, ((1,), (1,)), prec)


def _mm_tn(a, b, prec=None):
    return _dot(a, b, ((0,), (0,)), prec)


def _split(a):
    hi = a.astype(BF16)
    return hi, (a - hi.astype(F32)).astype(BF16)


def _dot3(a, b, dims):
    (ah, al), (bh, bl) = _split(a), _split(b)
    d = lambda x, y: lax.dot_general(x, y, dims, preferred_element_type=F32)
    return d(ah, bh) + (d(ah, bl) + d(al, bh))


def _bmm(a, b):
    return _dot3(a, b, (((2,), (1,)), ((0,), (0,))))


def _bmm_nt(a, b):
    return _dot3(a, b, (((2,), (2,)), ((0,), (0,))))


def _bmm_tn(a, b):
    return _dot3(a, b, (((1,), (1,)), ((0,), (0,))))


def _bmm_01(m01, x):
    x1 = x.astype(BF16)
    r1 = x - x1.astype(F32)
    x2 = r1.astype(BF16)
    x3 = (r1 - x2.astype(F32)).astype(BF16)
    d = lambda y: lax.dot_general(m01, y, (((2,), (1,)), ((0,), (0,))), preferred_element_type=F32)
    return d(x1) + (d(x2) + d(x3))


def _matmul(a, b, *, ta=False, tb=False, out_dtype=F32, name, tm=1024, tn=1024, tk=1024):
    m, k = (a.shape[1], a.shape[0]) if ta else a.shape
    n = b.shape[0] if tb else b.shape[1]
    tm, tn, tk = _tile(m, tm), _tile(n, tn), _tile(k, tk)
    nk = k // tk
    dims = ((0 if ta else 1,), (1 if tb else 0,))

    def body(a_ref, b_ref, o_ref, *acc):
        part = _dot(a_ref[...].astype(BF16), b_ref[...].astype(BF16), dims)
        if nk == 1:
            o_ref[...] = part.astype(o_ref.dtype)
            return
        kk = pl.program_id(2)

        @pl.when(kk == 0)
        def _():
            acc[0][...] = part

        @pl.when(kk > 0)
        def _():
            acc[0][...] += part

        @pl.when(kk == nk - 1)
        def _():
            o_ref[...] = acc[0][...].astype(o_ref.dtype)

    a_spec = pl.BlockSpec((tk, tm), lambda i, j, kk: (kk, i)) if ta else pl.BlockSpec((tm, tk), lambda i, j, kk: (i, kk))
    b_spec = pl.BlockSpec((tn, tk), lambda i, j, kk: (j, kk)) if tb else pl.BlockSpec((tk, tn), lambda i, j, kk: (kk, j))
    return pl.pallas_call(
        body, name=name, grid=(m // tm, n // tn, nk),
        in_specs=[a_spec, b_spec], out_specs=pl.BlockSpec((tm, tn), lambda i, j, kk: (i, j)),
        out_shape=jax.ShapeDtypeStruct((m, n), out_dtype),
        scratch_shapes=[pltpu.VMEM((tm, tn), F32)] if nk > 1 else [],
        compiler_params=_params("parallel", "parallel", "arbitrary"),
    )(a, b)


def _rms_fwd(x, g, name):
    t, d = x.shape
    tm = _tile(t, 256, 8)

    def body(x_ref, g_ref, h_ref, r_ref):
        xv = x_ref[...]
        r = lax.rsqrt(jnp.mean(xv * xv, axis=-1, keepdims=True) + RMS_EPS)
        h_ref[...] = (xv * r * g_ref[...]).astype(BF16)
        r_ref[...] = r

    return pl.pallas_call(
        body, name=name, grid=(t // tm,),
        in_specs=[pl.BlockSpec((tm, d), lambda i: (i, 0)), pl.BlockSpec((1, d), lambda i: (0, 0))],
        out_specs=[pl.BlockSpec((tm, d), lambda i: (i, 0)), pl.BlockSpec((tm, 1), lambda i: (i, 0))],
        out_shape=[jax.ShapeDtypeStruct((t, d), BF16), jax.ShapeDtypeStruct((t, 1), F32)],
        compiler_params=_params("parallel"),
    )(x, g)


def _rms_bwd(dh, x, rinv, g, add, name):
    t, d = x.shape
    tm = _tile(t, 256, 8)

    def body(dh_ref, x_ref, r_ref, g_ref, add_ref, dx_ref, dg_ref):
        @pl.when(pl.program_id(0) == 0)
        def _():
            dg_ref[...] = jnp.zeros_like(dg_ref)

        r = r_ref[...]
        xn = x_ref[...] * r
        dhv = dh_ref[...]
        dg_ref[...] += jnp.sum(dhv * xn, axis=0, keepdims=True)
        dxn = dhv * g_ref[...]
        dx_ref[...] = add_ref[...] + r * (dxn - xn * jnp.mean(dxn * xn, axis=-1, keepdims=True))

    row = pl.BlockSpec((tm, d), lambda i: (i, 0))
    vec = pl.BlockSpec((1, d), lambda i: (0, 0))
    return pl.pallas_call(
        body, name=name, grid=(t // tm,),
        in_specs=[row, row, pl.BlockSpec((tm, 1), lambda i: (i, 0)), vec, row],
        out_specs=[row, vec],
        out_shape=[jax.ShapeDtypeStruct((t, d), F32), jax.ShapeDtypeStruct((1, d), F32)],
        compiler_params=_params("arbitrary"),
    )(dh, x, rinv, g, add)


def _post_loss(yo, x, tgt, g, name):
    t, d = x.shape
    tm = _tile(t, 256, 8)

    def body(yo_ref, x_ref, t_ref, g_ref, loss_ref, dout_ref, dyo_ref, dg_ref):
        @pl.when(pl.program_id(0) == 0)
        def _():
            dg_ref[...] = jnp.zeros_like(dg_ref)
            loss_ref[...] = jnp.zeros_like(loss_ref)

        yv = yo_ref[...]
        r = lax.rsqrt(jnp.mean(yv * yv, axis=-1, keepdims=True) + RMS_EPS)
        n = yv * r
        err = x_ref[...] + n * g_ref[...] - t_ref[...]
        loss_ref[...] += 0.5 * jnp.sum(jnp.mean(err * err, axis=-1, keepdims=True), axis=0, keepdims=True)
        dout = err * (1.0 / d)
        dout_ref[...] = dout
        dg_ref[...] += jnp.sum(dout * n, axis=0, keepdims=True)
        dn = dout * g_ref[...]
        dyo_ref[...] = (r * (dn - n * jnp.mean(dn * n, axis=-1, keepdims=True))).astype(BF16)

    row = pl.BlockSpec((tm, d), lambda i: (i, 0))
    vec = pl.BlockSpec((1, d), lambda i: (0, 0))
    return pl.pallas_call(
        body, name=name, grid=(t // tm,),
        in_specs=[row, row, row, vec],
        out_specs=[pl.BlockSpec((1, 1), lambda i: (0, 0)), row, row, vec],
        out_shape=[jax.ShapeDtypeStruct((1, 1), F32), jax.ShapeDtypeStruct((t, d), F32),
                   jax.ShapeDtypeStruct((t, d), BF16), jax.ShapeDtypeStruct((1, d), F32)],
        compiler_params=_params("arbitrary"),
    )(yo, x, tgt, g)


def _head_sum(x):
    ri = lax.broadcasted_iota(jnp.int32, (LANES, LANES), 0) // HEAD_DIM
    ci = lax.broadcasted_iota(jnp.int32, (LANES, LANES), 1) // HEAD_DIM
    e = (ri == ci).astype(F32)
    parts = [_mm(x[:, i * LANES:(i + 1) * LANES], e, HI) for i in range(x.shape[1] // LANES)]
    return parts[0] if len(parts) == 1 else jnp.concatenate(parts, axis=1)


def _shifted(p_cur, before, first, mu):
    rolled = pltpu.roll(p_cur, 1, 0)
    prev_row = jnp.where(first, 0.0, before)
    row0 = lax.broadcasted_iota(jnp.int32, p_cur.shape, 0) == 0
    prev = jnp.where(row0, prev_row, rolled)
    return p_cur + (prev - p_cur) * mu, prev


def _rwkv_features(ps, rw, w0, a0, k_k, k_a, wd, wi):
    r, k, v = ps[:, 0:rw], ps[:, rw:2 * rw], ps[:, 2 * rw:3 * rw]
    wl, al = ps[:, 3 * rw:3 * rw + LORA], ps[:, 3 * rw + LORA:3 * rw + 2 * LORA]
    tw = jnp.tanh(wl)
    zw = w0 + _mm(tw.astype(BF16), wd)
    logw = -jnp.exp(-_softplus(-zw) - 0.5)
    alpha = _sig(a0 + _mm(al.astype(BF16), wi))
    kkr = k * k_k
    n2 = _head_sum(kkr * kkr)
    rn = lax.rsqrt(jnp.maximum(n2, 1e-24))
    kk = kkr * rn
    kmod = k * (1.0 + (alpha - 1.0) * k_a)
    return dict(r=r, k=k, v=v, tw=tw, al=al, zw=zw, logw=logw, alpha=alpha, kk=kk, rn=rn, n2=n2, kmod=kmod)


def _rwkv_pre_fwd(p, c, mu, w0, a0, k_k, k_a, wd, wi):
    t = p.shape[0]
    tm = _tile(t, 128, 8)
    rw, sh = c.rw, c.shift

    def body(p_ref, pp_ref, mu_ref, w0_ref, a0_ref, kk_ref, ka_ref, wd_ref, wi_ref,
             r_ref, lw_ref, km_ref, v_ref, a_ref, b_ref):
        ps, _ = _shifted(p_ref[...], pp_ref[7:8, :], pl.program_id(0) == 0, mu_ref[...])
        f = _rwkv_features(ps, rw, w0_ref[...], a0_ref[...], kk_ref[...], ka_ref[...], wd_ref[...], wi_ref[...])
        r_ref[...] = f["r"]
        lw_ref[...] = f["logw"]
        km_ref[...] = f["kmod"]
        v_ref[...] = f["v"]
        a_ref[...] = -f["kk"]
        b_ref[...] = f["kk"] * f["alpha"]

    vec = lambda n: pl.BlockSpec((1, n), lambda i: (0, 0))
    out = pl.BlockSpec((tm, rw), lambda i: (i, 0))
    return pl.pallas_call(
        body, name="rwkv_pre_fwd", grid=(t // tm,),
        in_specs=[pl.BlockSpec((tm, sh), lambda i: (i, 0)),
                  pl.BlockSpec((8, sh), lambda i: (jnp.maximum(i * (tm // 8) - 1, 0), 0)),
                  vec(sh), vec(rw), vec(rw), vec(rw), vec(rw),
                  pl.BlockSpec((LORA, rw), lambda i: (0, 0)), pl.BlockSpec((LORA, rw), lambda i: (0, 0))],
        out_specs=[out] * 6,
        out_shape=[jax.ShapeDtypeStruct((t, rw), F32)] * 6,
        compiler_params=_params("parallel"),
    )(p, p, mu, w0, a0, k_k, k_a, wd, wi)


def _rwkv_pre_bwd(p, c, mu, w0, a0, k_k, k_a, wd, wi, dr, dlw, dkm, dv, da, db, dr2, dkm2, dv2):
    t = p.shape[0]
    tm = _tile(t, 128, 8)
    rw, sh = c.rw, c.shift

    def body(p_ref, pp_ref, mu_ref, w0_ref, a0_ref, kk_ref, ka_ref, wd_ref, wi_ref,
             dr_ref, dlw_ref, dkm_ref, dv_ref, da_ref, db_ref, dr2_ref, dkm2_ref, dv2_ref,
             dps_ref, dzw_ref, dza_ref, tw_ref, al_ref, dw0_ref, da0_ref, dkk_ref, dka_ref):
        @pl.when(pl.program_id(0) == 0)
        def _():
            for ref in (dw0_ref, da0_ref, dkk_ref, dka_ref):
                ref[...] = jnp.zeros_like(ref)

        ps, _ = _shifted(p_ref[...], pp_ref[7:8, :], pl.program_id(0) == 0, mu_ref[...])
        k_k, k_a = kk_ref[...], ka_ref[...]
        f = _rwkv_features(ps, rw, w0_ref[...], a0_ref[...], k_k, k_a, wd_ref[...], wi_ref[...])
        alpha, kk, k = f["alpha"], f["kk"], f["k"]
        dkm = dkm_ref[...] + dkm2_ref[...]
        db = db_ref[...]
        dkk = db * alpha - da_ref[...]
        dalpha = db * kk + dkm * k * k_a
        dk = dkm * (1.0 + (alpha - 1.0) * k_a)
        dka_ref[...] += jnp.sum(dkm * k * (alpha - 1.0), axis=0, keepdims=True)
        dkkr = f["rn"] * jnp.where(f["n2"] > 1e-24, dkk - kk * _head_sum(dkk * kk), dkk)
        dk = dk + dkkr * k_k
        dkk_ref[...] += jnp.sum(dkkr * k, axis=0, keepdims=True)
        dza = dalpha * alpha * (1.0 - alpha)
        da0_ref[...] += jnp.sum(dza, axis=0, keepdims=True)
        dzw = dlw_ref[...] * f["logw"] * _sig(-f["zw"])
        dw0_ref[...] += jnp.sum(dzw, axis=0, keepdims=True)
        dza_b, dzw_b = dza.astype(BF16), dzw.astype(BF16)
        dal = _mm_nt(dza_b, wi_ref[...])
        dwl = _mm_nt(dzw_b, wd_ref[...]) * (1.0 - f["tw"] * f["tw"])
        dps_ref[:, 0:rw] = dr_ref[...] + dr2_ref[...]
        dps_ref[:, rw:2 * rw] = dk
        dps_ref[:, 2 * rw:3 * rw] = dv_ref[...] + dv2_ref[...]
        dps_ref[:, 3 * rw:3 * rw + LORA] = dwl
        dps_ref[:, 3 * rw + LORA:sh] = dal
        dzw_ref[...] = dzw_b
        dza_ref[...] = dza_b
        tw_ref[...] = f["tw"].astype(BF16)
        al_ref[...] = f["al"].astype(BF16)

    vec = lambda n: pl.BlockSpec((1, n), lambda i: (0, 0))
    blk = lambda n: pl.BlockSpec((tm, n), lambda i: (i, 0))
    return pl.pallas_call(
        body, name="rwkv_pre_bwd", grid=(t // tm,),
        in_specs=[blk(sh), pl.BlockSpec((8, sh), lambda i: (jnp.maximum(i * (tm // 8) - 1, 0), 0)),
                  vec(sh), vec(rw), vec(rw), vec(rw), vec(rw),
                  pl.BlockSpec((LORA, rw), lambda i: (0, 0)), pl.BlockSpec((LORA, rw), lambda i: (0, 0))]
                 + [blk(rw)] * 9,
        out_specs=[blk(sh), blk(rw), blk(rw), blk(LORA), blk(LORA), vec(rw), vec(rw), vec(rw), vec(rw)],
        out_shape=[jax.ShapeDtypeStruct((t, sh), F32), jax.ShapeDtypeStruct((t, rw), BF16),
                   jax.ShapeDtypeStruct((t, rw), BF16), jax.ShapeDtypeStruct((t, LORA), BF16),
                   jax.ShapeDtypeStruct((t, LORA), BF16)] + [jax.ShapeDtypeStruct((1, rw), F32)] * 4,
        compiler_params=_params("arbitrary"),
    )(p, p, mu, w0, a0, k_k, k_a, wd, wi, dr, dlw, dkm, dv, da, db, dr2, dkm2, dv2)


def _shift_bwd(dps, p, c, mu):
    t = p.shape[0]
    tm = _tile(t, 256, 8)
    sh = c.shift
    nt = t // tm

    def body(d_ref, dn_ref, p_ref, pp_ref, mu_ref, dp_ref, dmu_ref):
        i = pl.program_id(0)

        @pl.when(i == 0)
        def _():
            dmu_ref[...] = jnp.zeros_like(dmu_ref)

        mu = mu_ref[...]
        d = d_ref[...]
        pc = p_ref[...]
        _, prev = _shifted(pc, pp_ref[7:8, :], i == 0, mu)
        dmu_ref[...] += jnp.sum(d * (prev - pc), axis=0, keepdims=True)
        nxt_row = jnp.where(i == nt - 1, 0.0, dn_ref[0:1, :])
        last = lax.broadcasted_iota(jnp.int32, d.shape, 0) == tm - 1
        nxt = jnp.where(last, nxt_row, pltpu.roll(d, tm - 1, 0))
        dp_ref[...] = (d * (1.0 - mu) + nxt * mu).astype(BF16)

    blk = pl.BlockSpec((tm, sh), lambda i: (i, 0))
    return pl.pallas_call(
        body, name="shift_bwd", grid=(nt,),
        in_specs=[blk, pl.BlockSpec((8, sh), lambda i: (jnp.minimum((i + 1) * (tm // 8), t // 8 - 1), 0)),
                  blk, pl.BlockSpec((8, sh), lambda i: (jnp.maximum(i * (tm // 8) - 1, 0), 0)),
                  pl.BlockSpec((1, sh), lambda i: (0, 0))],
        out_specs=[blk, pl.BlockSpec((1, sh), lambda i: (0, 0))],
        out_shape=[jax.ShapeDtypeStruct((t, sh), BF16), jax.ShapeDtypeStruct((1, sh), F32)],
        compiler_params=_params("arbitrary"),
    )(dps, dps, p, p, mu)


def _tri(n, strict):
    ri = lax.broadcasted_iota(jnp.int32, (n, n), 0)
    ci = lax.broadcasted_iota(jnp.int32, (n, n), 1)
    return (ri > ci) if strict else (ri >= ci)


def _unit_lower_inverse(a):
    n = a.shape[-1]
    ri = lax.broadcasted_iota(jnp.int32, (n, n), 0)
    ci = lax.broadcasted_iota(jnp.int32, (n, n), 1)
    eye = (ri == ci).astype(F32)
    blk = lambda s: (ri // s) == (ci // s)
    ad = jnp.where(blk(16), a, 0.0)
    p = eye + ad
    for _ in range(3):
        ad = _bmm(ad, ad)
        p = p + _bmm(p, ad)
    s = 16
    while s < n:
        off = jnp.where(blk(2 * s) & ~blk(s), a, 0.0)
        p = p + _bmm(_bmm(p, off), p)
        s *= 2
    return p


def _chunk_common(r, lw, k, a, b):
    n = r.shape[1]
    tri_incl = jnp.broadcast_to(_tri(n, False).astype(BF16), (r.shape[0], n, n))
    cum = _bmm_01(tri_incl, lw)
    e_pos, e_neg, e_exc = jnp.exp(cum), jnp.exp(-cum), jnp.exp(cum - lw)
    last = lax.broadcasted_iota(jnp.int32, (n, r.shape[2]), 0) == n - 1
    g_last = jnp.exp(jnp.sum(jnp.where(last, cum, 0.0), axis=1, keepdims=True))
    return g_last, r * e_pos, a * e_exc, b * e_neg, k * e_neg, e_pos, e_neg, e_exc


def _chunk_solve(rt, at, bt, kt, v, g0):
    strict, incl = _tri(rt.shape[1], True), _tri(rt.shape[1], False)
    a_ab = jnp.where(strict, _bmm_nt(at, bt), 0.0)
    a_ak = jnp.where(strict, _bmm_nt(at, kt), 0.0)
    a_rb = jnp.where(incl, _bmm_nt(rt, bt), 0.0)
    a_rk = jnp.where(incl, _bmm_nt(rt, kt), 0.0)
    tinv = _unit_lower_inverse(a_ab)
    u = _bmm(tinv, _bmm(at, g0) + _bmm(a_ak, v))
    return a_ab, a_ak, a_rb, a_rk, tinv, u


def _diag_col(row, n):
    ri = lax.broadcasted_iota(jnp.int32, (n, n), 0)
    ci = lax.broadcasted_iota(jnp.int32, (n, n), 1)
    return jnp.sum(jnp.where(ri == ci, row, 0.0), axis=2, keepdims=True)


def _diag_row(col, n):
    ri = lax.broadcasted_iota(jnp.int32, (n, n), 0)
    ci = lax.broadcasted_iota(jnp.int32, (n, n), 1)
    return jnp.sum(jnp.where(ri == ci, col, 0.0), axis=1, keepdims=True)


def _rwkv_scan_fwd(r, lw, k, v, a, b, hb):
    h, t, n = r.shape
    nc = t // CHUNK

    def body(r_ref, lw_ref, k_ref, v_ref, a_ref, b_ref, y_ref, st_ref, g_sc):
        @pl.when(pl.program_id(1) == 0)
        def _():
            g_sc[...] = jnp.zeros_like(g_sc)

        g0 = g_sc[...]
        st_ref[0] = g0
        vv = v_ref[...]
        g_last, rt, at, bt, kt, _, _, _ = _chunk_common(r_ref[...], lw_ref[...], k_ref[...], a_ref[...], b_ref[...])
        _, _, a_rb, a_rk, _, u = _chunk_solve(rt, at, bt, kt, vv, g0)
        y_ref[...] = _bmm(rt, g0) + _bmm(a_rb, u) + _bmm(a_rk, vv)
        z = g0 + _bmm_tn(bt, u) + _bmm_tn(kt, vv)
        g_sc[...] = _diag_col(g_last, n) * z

    blk = pl.BlockSpec((hb, CHUNK, n), lambda i, j: (i, j, 0))
    return pl.pallas_call(
        body, name="rwkv_scan_fwd", grid=(h // hb, nc),
        in_specs=[blk] * 6,
        out_specs=[blk, pl.BlockSpec((1, hb, n, n), lambda i, j: (j, i, 0, 0))],
        out_shape=[jax.ShapeDtypeStruct((h, t, n), F32), jax.ShapeDtypeStruct((nc, h, n, n), F32)],
        scratch_shapes=[pltpu.VMEM((hb, n, n), F32)],
        compiler_params=_params("parallel", "arbitrary"),
    )(r, lw, k, v, a, b)


def _rwkv_scan_bwd(r, lw, k, v, a, b, states, dy, hb):
    h, t, n = r.shape
    nc = t // CHUNK

    def body(r_ref, lw_ref, k_ref, v_ref, a_ref, b_ref, st_ref, dy_ref,
             dr_ref, dlw_ref, dk_ref, dv_ref, da_ref, db_ref, dg_sc):
        @pl.when(pl.program_id(1) == 0)
        def _():
            dg_sc[...] = jnp.zeros_like(dg_sc)

        g0 = st_ref[0]
        vv, dyv, dh = v_ref[...], dy_ref[...], dg_sc[...]
        lwv = lw_ref[...]
        g_last, rt, at, bt, kt, e_pos, e_neg, e_exc = _chunk_common(r_ref[...], lwv, k_ref[...], a_ref[...], b_ref[...])
        a_ab, a_ak, a_rb, a_rk, tinv, u = _chunk_solve(rt, at, bt, kt, vv, g0)
        strict, incl = _tri(CHUNK, True), _tri(CHUNK, False)
        gcol = _diag_col(g_last, n)
        z = g0 + _bmm_tn(bt, u) + _bmm_tn(kt, vv)
        dz = gcol * dh
        dc_last = _diag_row(jnp.sum(dh * gcol * z, axis=2, keepdims=True), n)
        du = _bmm_tn(a_rb, dyv) + _bmm(bt, dz)
        dx = _bmm_tn(tinv, du)
        dv_ref[...] = _bmm_tn(a_rk, dyv) + _bmm(kt, dz) + _bmm_tn(a_ak, dx)
        da_ab = jnp.where(strict, _bmm_nt(dx, u), 0.0)
        da_ak = jnp.where(strict, _bmm_nt(dx, vv), 0.0)
        da_rb = jnp.where(incl, _bmm_nt(dyv, u), 0.0)
        da_rk = jnp.where(incl, _bmm_nt(dyv, vv), 0.0)
        d_at = _bmm(da_ab, bt) + _bmm(da_ak, kt) + _bmm_nt(dx, g0)
        d_rt = _bmm(da_rb, bt) + _bmm(da_rk, kt) + _bmm_nt(dyv, g0)
        d_bt = _bmm_tn(da_ab, at) + _bmm_tn(da_rb, rt) + _bmm_nt(u, dz)
        d_kt = _bmm_tn(da_ak, at) + _bmm_tn(da_rk, rt) + _bmm_nt(vv, dz)
        dg_sc[...] = dz + _bmm_tn(rt, dyv) + _bmm_tn(at, dx)
        dr_ref[...] = d_rt * e_pos
        da_ref[...] = d_at * e_exc
        db_ref[...] = d_bt * e_neg
        dk_ref[...] = d_kt * e_neg
        last = lax.broadcasted_iota(jnp.int32, (CHUNK, n), 0) == CHUNK - 1
        dc = d_rt * rt - d_bt * bt - d_kt * kt + jnp.where(last, dc_last, 0.0)
        dce = d_at * at
        ri = lax.broadcasted_iota(jnp.int32, (CHUNK, CHUNK), 0)
        ci = lax.broadcasted_iota(jnp.int32, (CHUNK, CHUNK), 1)
        up_incl = jnp.broadcast_to((ri <= ci).astype(BF16), (hb, CHUNK, CHUNK))
        dlw_ref[...] = _bmm_01(up_incl, dc + dce) - dce

    rev = lambda i, j: (i, nc - 1 - j, 0)
    blk = pl.BlockSpec((hb, CHUNK, n), rev)
    return pl.pallas_call(
        body, name="rwkv_scan_bwd", grid=(h // hb, nc),
        in_specs=[blk] * 6 + [pl.BlockSpec((1, hb, n, n), lambda i, j: (nc - 1 - j, i, 0, 0)), blk],
        out_specs=[blk] * 6,
        out_shape=[jax.ShapeDtypeStruct((h, t, n), F32)] * 6,
        scratch_shapes=[pltpu.VMEM((hb, n, n), F32)],
        compiler_params=_params("parallel", "arbitrary"),
    )(r, lw, k, v, a, b, states, dy)


def _silu_grad(g):
    s = _sig(g)
    return s * (1.0 + g * (1.0 - s))


def _group_norm(ys):
    yc = ys - _head_sum(ys) * (1.0 / HEAD_DIM)
    rstd = lax.rsqrt(_head_sum(yc * yc) * (1.0 / HEAD_DIM) + GN_EPS)
    return yc * rstd, rstd


def _rwkv_post_fwd(ys, r, km, v, p, c, ln_w, ln_b, r_k):
    t = ys.shape[0]
    tm = _tile(t, 512, 8)
    goff = c.o_grw // LANES

    def body(ys_ref, r_ref, km_ref, v_ref, g_ref, lw_ref, lb_ref, rk_ref, o_ref):
        yn, _ = _group_norm(ys_ref[...])
        s = _head_sum(r_ref[...] * km_ref[...] * rk_ref[...])
        g = g_ref[...]
        o_ref[...] = ((yn * lw_ref[...] + lb_ref[...] + s * v_ref[...]) * g * _sig(g)).astype(BF16)

    blk = pl.BlockSpec((tm, LANES), lambda i, j: (i, j))
    vec = pl.BlockSpec((1, LANES), lambda i, j: (0, j))
    return pl.pallas_call(
        body, name="rwkv_post_fwd", grid=(t // tm, c.rw // LANES),
        in_specs=[blk] * 4 + [pl.BlockSpec((tm, LANES), lambda i, j: (i, goff + j)), vec, vec, vec],
        out_specs=blk, out_shape=jax.ShapeDtypeStruct((t, c.rw), BF16),
        compiler_params=_params("parallel", "parallel"),
    )(ys, r, km, v, p, ln_w, ln_b, r_k)


def _rwkv_post_bwd(dyc, ys, r, km, v, p, c, ln_w, ln_b, r_k):
    t = ys.shape[0]
    tm = _tile(t, 512, 8)
    goff = c.o_grw // LANES

    def body(dy_ref, ys_ref, r_ref, km_ref, v_ref, g_ref, lw_ref, lb_ref, rk_ref,
             dys_ref, dr_ref, dkm_ref, dv_ref, dg_ref, dlw_ref, dlb_ref, drk_ref):
        @pl.when(pl.program_id(1) == 0)
        def _():
            for ref in (dlw_ref, dlb_ref, drk_ref):
                ref[...] = jnp.zeros_like(ref)

        yn, rstd = _group_norm(ys_ref[...])
        rv, kmv, vv, rk, g = r_ref[...], km_ref[...], v_ref[...], rk_ref[...], g_ref[...]
        s = _head_sum(rv * kmv * rk)
        y = yn * lw_ref[...] + lb_ref[...] + s * vv
        dyc = dy_ref[...]
        dg_ref[...] = (dyc * y * _silu_grad(g)).astype(BF16)
        dy = dyc * g * _sig(g)
        dlb_ref[...] += jnp.sum(dy, axis=0, keepdims=True)
        dlw_ref[...] += jnp.sum(dy * yn, axis=0, keepdims=True)
        dyn = dy * lw_ref[...]
        inv = 1.0 / HEAD_DIM
        dys_ref[...] = rstd * (dyn - _head_sum(dyn) * inv - yn * _head_sum(dyn * yn) * inv)
        ds = _head_sum(dy * vv)
        dv_ref[...] = dy * s
        dr_ref[...] = ds * kmv * rk
        dkm_ref[...] = ds * rv * rk
        drk_ref[...] += jnp.sum(ds * rv * kmv, axis=0, keepdims=True)

    blk = pl.BlockSpec((tm, LANES), lambda j, i: (i, j))
    vec = pl.BlockSpec((1, LANES), lambda j, i: (0, j))
    f = jax.ShapeDtypeStruct((t, c.rw), F32)
    s1 = jax.ShapeDtypeStruct((1, c.rw), F32)
    return pl.pallas_call(
        body, name="rwkv_post_bwd", grid=(c.rw // LANES, t // tm),
        in_specs=[blk] * 5 + [pl.BlockSpec((tm, LANES), lambda j, i: (i, goff + j)), vec, vec, vec],
        out_specs=[blk] * 5 + [vec] * 3,
        out_shape=[f, f, f, f, jax.ShapeDtypeStruct((t, c.rw), BF16), s1, s1, s1],
        compiler_params=_params("parallel", "arbitrary"),
    )(dyc, ys, r, km, v, p, ln_w, ln_b, r_k)


def _gate_fwd(y, p, goff, name):
    t, w = y.shape
    tm = _tile(t, 512, 8)
    gb = goff // LANES

    def body(y_ref, g_ref, o_ref):
        g = g_ref[...]
        o_ref[...] = (y_ref[...] * g * _sig(g)).astype(BF16)

    blk = pl.BlockSpec((tm, LANES), lambda i, j: (i, j))
    return pl.pallas_call(
        body, name=name, grid=(t // tm, w // LANES),
        in_specs=[blk, pl.BlockSpec((tm, LANES), lambda i, j: (i, gb + j))],
        out_specs=blk, out_shape=jax.ShapeDtypeStruct((t, w), BF16),
        compiler_params=_params("parallel", "parallel"),
    )(y, p)


def _gate_bwd(dyc, yoff, y, p, goff, name):
    t, w = y.shape
    tm = _tile(t, 512, 8)
    gb, yb = goff // LANES, yoff // LANES

    def body(d_ref, y_ref, g_ref, dy_ref, dg_ref):
        g, d = g_ref[...], d_ref[...]
        dy_ref[...] = d * g * _sig(g)
        dg_ref[...] = (d * y_ref[...] * _silu_grad(g)).astype(BF16)

    blk = pl.BlockSpec((tm, LANES), lambda i, j: (i, j))
    return pl.pallas_call(
        body, name=name, grid=(t // tm, w // LANES),
        in_specs=[pl.BlockSpec((tm, LANES), lambda i, j: (i, yb + j)), blk,
                  pl.BlockSpec((tm, LANES), lambda i, j: (i, gb + j))],
        out_specs=[blk, blk],
        out_shape=[jax.ShapeDtypeStruct((t, w), F32), jax.ShapeDtypeStruct((t, w), BF16)],
        compiler_params=_params("parallel", "parallel"),
    )(dyc, y, p)


NEG = -1e30


def _fox_prep(p, c, b_f):
    t = p.shape[0]
    tm = _tile(t, 512, 8)
    fb = c.o_fl // LANES

    def body(f_ref, b_ref, o_ref, carry):
        @pl.when(pl.program_id(0) == 0)
        def _():
            carry[...] = jnp.zeros_like(carry)

        logf = -_softplus(-(f_ref[...] + b_ref[...]))
        cum = _mm(_tri(tm, False).astype(F32), logf, HI) + carry[...]
        o_ref[...] = cum
        carry[...] += jnp.sum(logf, axis=0, keepdims=True)

    return pl.pallas_call(
        body, name="fox_prep", grid=(t // tm,),
        in_specs=[pl.BlockSpec((tm, LANES), lambda i: (i, fb)), pl.BlockSpec((1, LANES), lambda i: (0, 0))],
        out_specs=pl.BlockSpec((tm, LANES), lambda i: (i, 0)),
        out_shape=jax.ShapeDtypeStruct((t, LANES), F32),
        scratch_shapes=[pltpu.VMEM((1, LANES), F32)],
        compiler_params=_params("arbitrary"),
    )(p, b_f)


def _fox_logit_bwd(dcum, p, c, b_f):
    t = p.shape[0]
    tm = _tile(t, 512, 8)
    fb = c.o_fl // LANES
    nt = t // tm

    def body(d_ref, f_ref, b_ref, o_ref, db_ref, carry):
        @pl.when(pl.program_id(0) == 0)
        def _():
            carry[...] = jnp.zeros_like(carry)
            db_ref[...] = jnp.zeros_like(db_ref)

        d = d_ref[0] + d_ref[1]
        dlogf = _mm(_tri(tm, False).astype(F32).T, d, HI) + carry[...]
        carry[...] += jnp.sum(d, axis=0, keepdims=True)
        df = dlogf * _sig(-(f_ref[...] + b_ref[...]))
        o_ref[...] = df.astype(BF16)
        db_ref[...] += jnp.sum(df, axis=0, keepdims=True)

    return pl.pallas_call(
        body, name="fox_logit_bwd", grid=(nt,),
        in_specs=[pl.BlockSpec((2, tm, LANES), lambda i: (0, nt - 1 - i, 0)),
                  pl.BlockSpec((tm, LANES), lambda i: (nt - 1 - i, fb)),
                  pl.BlockSpec((1, LANES), lambda i: (0, 0))],
        out_specs=[pl.BlockSpec((tm, LANES), lambda i: (nt - 1 - i, 0)), pl.BlockSpec((1, LANES), lambda i: (0, 0))],
        out_shape=[jax.ShapeDtypeStruct((t, LANES), BF16), jax.ShapeDtypeStruct((1, LANES), F32)],
        scratch_shapes=[pltpu.VMEM((1, LANES), F32)],
        compiler_params=_params("arbitrary"),
    )(dcum, p, b_f)


def _fox_scores(q, k, cq, ck, qi, ki, tq, tk):
    s = _mm_nt((q * (HEAD_DIM ** -0.5)).astype(BF16), k.astype(BF16)) + cq - ck
    qpos = qi * tq + lax.broadcasted_iota(jnp.int32, (tq, tk), 0)
    kpos = ki * tk + lax.broadcasted_iota(jnp.int32, (tq, tk), 1)
    mask = kpos <= qpos
    return jnp.where(mask, s, NEG), mask


def _fox_fwd(q, k, v, cq, ck, hb, tb):
    h, t, n = q.shape
    tq = tk = _tile(t, tb, LANES)
    nq = t // tq

    def body(q_ref, k_ref, v_ref, cq_ref, ck_ref, o_ref, lse_ref, m_sc, l_sc, acc_sc):
        qi, ki = pl.program_id(1), pl.program_id(2)

        @pl.when(ki == 0)
        def _():
            m_sc[...] = jnp.full_like(m_sc, NEG)
            l_sc[...] = jnp.zeros_like(l_sc)
            acc_sc[...] = jnp.zeros_like(acc_sc)

        @pl.when(ki <= qi)
        def _():
            for i in range(hb):
                s, _ = _fox_scores(q_ref[i], k_ref[i], cq_ref[i], ck_ref[i], qi, ki, tq, tk)
                m_old = m_sc[i]
                m_new = jnp.maximum(m_old, jnp.max(s, axis=1, keepdims=True))
                scale = jnp.exp(m_old - m_new)
                e = jnp.exp(s - m_new)
                l_sc[i] = scale * l_sc[i] + jnp.sum(e, axis=1, keepdims=True)
                acc_sc[i] = scale * acc_sc[i] + _mm(e.astype(BF16), v_ref[i].astype(BF16))
                m_sc[i] = m_new

        @pl.when(ki == qi)
        def _():
            o_ref[...] = acc_sc[...] / l_sc[...]
            lse_ref[...] = m_sc[...] + jnp.log(l_sc[...])

    qb = pl.BlockSpec((hb, tq, n), lambda g, i, j: (g, i, 0))
    kb = pl.BlockSpec((hb, tk, n), lambda g, i, j: (g, jnp.minimum(i, j), 0))
    col = pl.BlockSpec((hb, tq, 1), lambda g, i, j: (g, i, 0))
    return pl.pallas_call(
        body, name="fox_fwd", grid=(h // hb, nq, nq),
        in_specs=[qb, kb, kb, col, pl.BlockSpec((hb, 1, tk), lambda g, i, j: (g, 0, jnp.minimum(i, j)))],
        out_specs=[qb, col],
        out_shape=[jax.ShapeDtypeStruct((h, t, n), F32), jax.ShapeDtypeStruct((h, t, 1), F32)],
        scratch_shapes=[pltpu.VMEM((hb, tq, 1), F32), pltpu.VMEM((hb, tq, 1), F32), pltpu.VMEM((hb, tq, n), F32)],
        compiler_params=_params("parallel", "parallel", "arbitrary"),
    )(q, k, v, cq, ck)


def _fox_bwd_dq(q, k, v, cq, ck, lse, o, do, hb, tb):
    h, t, n = q.shape
    tq = tk = _tile(t, tb, LANES)
    nq = t // tq

    def body(q_ref, k_ref, v_ref, cq_ref, ck_ref, lse_ref, o_ref, do_ref, dq_ref, dcq_ref, acc_sc, row_sc):
        qi, ki = pl.program_id(1), pl.program_id(2)

        @pl.when(ki == 0)
        def _():
            acc_sc[...] = jnp.zeros_like(acc_sc)
            row_sc[...] = jnp.zeros_like(row_sc)

        @pl.when(ki <= qi)
        def _():
            for i in range(hb):
                s, mask = _fox_scores(q_ref[i], k_ref[i], cq_ref[i], ck_ref[i], qi, ki, tq, tk)
                dov = do_ref[i]
                delta = jnp.sum(dov * o_ref[i], axis=1, keepdims=True)
                pm = jnp.where(mask, jnp.exp(s - lse_ref[i]), 0.0)
                dp = _mm_nt(dov.astype(BF16), v_ref[i].astype(BF16))
                ds = pm * (dp - delta)
                acc_sc[i] += _mm(ds.astype(BF16), k_ref[i].astype(BF16))
                row_sc[i] += jnp.sum(ds, axis=1, keepdims=True)

        @pl.when(ki == qi)
        def _():
            dq_ref[...] = acc_sc[...] * (HEAD_DIM ** -0.5)
            dcq_ref[...] = row_sc[...]

    qb = pl.BlockSpec((hb, tq, n), lambda g, i, j: (g, i, 0))
    kb = pl.BlockSpec((hb, tk, n), lambda g, i, j: (g, jnp.minimum(i, j), 0))
    col = pl.BlockSpec((hb, tq, 1), lambda g, i, j: (g, i, 0))
    return pl.pallas_call(
        body, name="fox_bwd_dq", grid=(h // hb, nq, nq),
        in_specs=[qb, kb, kb, col, pl.BlockSpec((hb, 1, tk), lambda g, i, j: (g, 0, jnp.minimum(i, j))), col, qb, qb],
        out_specs=[qb, col],
        out_shape=[jax.ShapeDtypeStruct((h, t, n), F32), jax.ShapeDtypeStruct((h, t, 1), F32)],
        scratch_shapes=[pltpu.VMEM((hb, tq, n), F32), pltpu.VMEM((hb, tq, 1), F32)],
        compiler_params=_params("parallel", "parallel", "arbitrary"),
    )(q, k, v, cq, ck, lse, o, do)


def _fox_bwd_dkv(q, k, v, cq, ck, lse, o, do, hb, tb):
    h, t, n = q.shape
    tq = tk = _tile(t, tb, LANES)
    nq = t // tq

    def body(q_ref, k_ref, v_ref, cq_ref, ck_ref, lse_ref, o_ref, do_ref, dk_ref, dv_ref, dck_ref, dk_sc, dv_sc, dc_sc):
        ki, qi = pl.program_id(1), pl.program_id(2)

        @pl.when(qi == 0)
        def _():
            dk_sc[...] = jnp.zeros_like(dk_sc)
            dv_sc[...] = jnp.zeros_like(dv_sc)
            dc_sc[...] = jnp.zeros_like(dc_sc)

        @pl.when(qi >= ki)
        def _():
            for i in range(hb):
                s, mask = _fox_scores(q_ref[i], k_ref[i], cq_ref[i], ck_ref[i], qi, ki, tq, tk)
                dov = do_ref[i]
                delta = jnp.sum(dov * o_ref[i], axis=1, keepdims=True)
                pm = jnp.where(mask, jnp.exp(s - lse_ref[i]), 0.0)
                dob = dov.astype(BF16)
                dp = _mm_nt(dob, v_ref[i].astype(BF16))
                ds = pm * (dp - delta)
                dv_sc[i] += _mm_tn(pm.astype(BF16), dob)
                dk_sc[i] += _mm_tn(ds.astype(BF16), q_ref[i].astype(BF16))
                dc_sc[i] -= jnp.sum(ds, axis=0, keepdims=True)

        @pl.when(qi == nq - 1)
        def _():
            dk_ref[...] = dk_sc[...] * (HEAD_DIM ** -0.5)
            dv_ref[...] = dv_sc[...]
            dck_ref[...] = dc_sc[...]

    qb = pl.BlockSpec((hb, tq, n), lambda g, j, i: (g, jnp.maximum(i, j), 0))
    kb = pl.BlockSpec((hb, tk, n), lambda g, j, i: (g, j, 0))
    col = pl.BlockSpec((hb, tq, 1), lambda g, j, i: (g, jnp.maximum(i, j), 0))
    row = pl.BlockSpec((hb, 1, tk), lambda g, j, i: (g, 0, j))
    return pl.pallas_call(
        body, name="fox_bwd_dkv", grid=(h // hb, nq, nq),
        in_specs=[qb, kb, kb, col, row, col, qb, qb],
        out_specs=[kb, kb, row],
        out_shape=[jax.ShapeDtypeStruct((h, t, n), F32), jax.ShapeDtypeStruct((h, t, n), F32),
                   jax.ShapeDtypeStruct((h, 1, t), F32)],
        scratch_shapes=[pltpu.VMEM((hb, tk, n), F32), pltpu.VMEM((hb, tk, n), F32), pltpu.VMEM((hb, 1, tk), F32)],
        compiler_params=_params("parallel", "parallel", "arbitrary"),
    )(q, k, v, cq, ck, lse, o, do)


def _mem_probs(q, mk, scale):
    s = _mm_nt(q.astype(BF16), mk.astype(BF16)) * scale
    e = jnp.exp(s - jnp.max(s, axis=1, keepdims=True))
    return e / jnp.sum(e, axis=1, keepdims=True)


def _mem_attn_fwd(p, c, mkv):
    t = p.shape[0]
    tm = _tile(t, 512, 8)
    dh = c.mhd
    qb = c.o_mq // dh
    scale = dh ** -0.5

    def body(q_ref, mk_ref, mv_ref, o_ref):
        pm = _mem_probs(q_ref[...], mk_ref[...], scale)
        o_ref[...] = _mm(pm.astype(BF16), mv_ref[...].astype(BF16))

    m = mkv.shape[0]
    return pl.pallas_call(
        body, name="mem_attn_fwd", grid=(t // tm, MEM_HEADS),
        in_specs=[pl.BlockSpec((tm, dh), lambda i, j: (i, qb + j)),
                  pl.BlockSpec((m, dh), lambda i, j: (0, j)),
                  pl.BlockSpec((m, dh), lambda i, j: (0, MEM_HEADS + j))],
        out_specs=pl.BlockSpec((tm, dh), lambda i, j: (i, j)),
        out_shape=jax.ShapeDtypeStruct((t, c.mw), F32),
        compiler_params=_params("parallel", "parallel"),
    )(p, mkv, mkv)


def _mem_attn_bwd(p, c, mkv, do):
    t = p.shape[0]
    tm = _tile(t, 512, 8)
    dh = c.mhd
    qb = c.o_mq // dh
    scale = dh ** -0.5
    m = mkv.shape[0]

    def body(q_ref, mk_ref, mv_ref, do_ref, dq_ref, dmk_ref, dmv_ref):
        @pl.when(pl.program_id(1) == 0)
        def _():
            dmk_ref[...] = jnp.zeros_like(dmk_ref)
            dmv_ref[...] = jnp.zeros_like(dmv_ref)

        qv = q_ref[...].astype(BF16)
        pm = _mem_probs(qv, mk_ref[...], scale)
        dob = do_ref[...].astype(BF16)
        dmv_ref[...] += _mm_tn(pm.astype(BF16), dob)
        dp = _mm_nt(dob, mv_ref[...].astype(BF16))
        ds = (pm * (dp - jnp.sum(pm * dp, axis=1, keepdims=True)) * scale).astype(BF16)
        dq_ref[...] = _mm(ds, mk_ref[...].astype(BF16)).astype(BF16)
        dmk_ref[...] += _mm_tn(ds, qv)

    kvb = lambda off: pl.BlockSpec((m, dh), lambda j, i: (0, off + j))
    return pl.pallas_call(
        body, name="mem_attn_bwd", grid=(MEM_HEADS, t // tm),
        in_specs=[pl.BlockSpec((tm, dh), lambda j, i: (i, qb + j)), kvb(0), kvb(MEM_HEADS),
                  pl.BlockSpec((tm, dh), lambda j, i: (i, j))],
        out_specs=[pl.BlockSpec((tm, dh), lambda j, i: (i, j)), kvb(0), kvb(0)],
        out_shape=[jax.ShapeDtypeStruct((t, c.mw), BF16), jax.ShapeDtypeStruct((m, c.mw), F32),
                   jax.ShapeDtypeStruct((m, c.mw), F32)],
        compiler_params=_params("parallel", "arbitrary"),
    )(p, mkv, mkv, do)


def _adamw(w, g, m, v, name):
    rows, cols = w.shape
    tm = _tile(rows, max(8, (1 << 18) // cols // 8 * 8), 8)
    bc1 = 1.0 - ADAM_B1 ** ADAM_STEP
    bc2 = 1.0 - ADAM_B2 ** ADAM_STEP

    def body(w_ref, g_ref, m_ref, v_ref, go_ref, d_ref, mo_ref, vo_ref):
        gv = g_ref[:, 0:cols]
        mn = ADAM_B1 * m_ref[...] + (1.0 - ADAM_B1) * gv
        vn = ADAM_B2 * v_ref[...] + (1.0 - ADAM_B2) * (gv * gv)
        go_ref[...] = gv
        mo_ref[...] = mn
        vo_ref[...] = vn
        d_ref[...] = -ADAM_LR * ((mn / bc1) / (jnp.sqrt(vn / bc2) + ADAM_EPS) + ADAM_WD * w_ref[...])

    blk = pl.BlockSpec((tm, cols), lambda i: (i, 0))
    shp = jax.ShapeDtypeStruct((rows, cols), F32)
    return pl.pallas_call(
        body, name=name, grid=(rows // tm,),
        in_specs=[blk, pl.BlockSpec((tm, g.shape[1]), lambda i: (i, 0)), blk, blk],
        out_specs=[blk] * 4, out_shape=[shp] * 4,
        compiler_params=_params("parallel"),
    )(w, g, m, v)


SCAN_HEADS = 4
FOX_BLOCK = 512


def _local_step(c, x, mem, tgt, w):
    t = x.shape[0]
    rw = c.rw
    hd = lambda z: z.reshape(t, c.h, HEAD_DIM).transpose(1, 0, 2)
    uh = lambda z: z.transpose(1, 0, 2).reshape(t, rw)
    vecs = (w["mu"], w["w0"], w["a0"], w["k_k"], w["k_a"], w["wd"], w["wi"])

    h, rinv = _rms_fwd(x, w["g_pre"], "rms_pre")
    p = _matmul(h, w["wp"], name="in_proj", tk=4096)
    r, lw, km, v, a, b = _rwkv_pre_fwd(p, c, *vecs)
    scan_in = tuple(hd(z) for z in (r, lw, km, v, a, b))
    ysh, states = _rwkv_scan_fwd(*scan_in, SCAN_HEADS)
    ys = uh(ysh)
    yc_r = _rwkv_post_fwd(ys, r, km, v, p, c, w["ln_w"], w["ln_b"], w["r_k"])

    cumh = _fox_prep(p, c, w["b_f"])[:, :c.h].T
    cq, ck = cumh[:, :, None], cumh[:, None, :]
    fq, fk, fv = (hd(p[:, o:o + rw]) for o in (c.o_fq, c.o_fk, c.o_fv))
    ofh, lse = _fox_fwd(fq, fk, fv, cq, ck, SCAN_HEADS, FOX_BLOCK)
    yfox = uh(ofh)
    yc_f = _gate_fwd(yfox, p, c.o_gfox, "gate_fox")

    memn, rinv_m = _rms_fwd(mem, w["g_mem"], "rms_mem")
    mkv = _matmul(memn, w["w_mem_kv"], name="mem_kv")
    ymem = _mem_attn_fwd(p, c, mkv)
    yc_m = _gate_fwd(ymem, p, c.o_gmq, "gate_mem")

    ycat = jnp.concatenate([yc_r, yc_f, yc_m], axis=1)
    yo = _matmul(ycat, w["w_out"], name="out_proj", tn=512, tk=4096)
    loss, dout, dyo, dg_post = _post_loss(yo, x, tgt, w["g_post"], "post_loss")

    dyc = _matmul(dyo, w["w_out"], tb=True, name="d_ycat", tn=512, tk=4096)
    dw_out = _matmul(ycat, dyo, ta=True, name="d_w_out", tn=512, tk=4096)
    dys, dr2, dkm2, dv2, dg_r, dln_w, dln_b, dr_k = _rwkv_post_bwd(
        dyc, ys, r, km, v, p, c, w["ln_w"], w["ln_b"], w["r_k"])
    dyf, dg_f = _gate_bwd(dyc, rw, yfox, p, c.o_gfox, "gate_fox_bwd")
    dym, dg_m = _gate_bwd(dyc, 2 * rw, ymem, p, c.o_gmq, "gate_mem_bwd")

    scan_g = _rwkv_scan_bwd(*scan_in, states, hd(dys), SCAN_HEADS)
    dps, dzw, dza, twb, alb, dw0, da0, dk_k, dk_a = _rwkv_pre_bwd(
        p, c, *vecs, *(uh(z) for z in scan_g), dr2, dkm2, dv2)
    dwd = _matmul(twb, dzw, ta=True, name="d_w_decay")
    dwi = _matmul(alb, dza, ta=True, name="d_w_iclr")
    dp_shift, dmu = _shift_bwd(dps, p, c, w["mu"])

    doh = hd(dyf)
    dqh, dcq = _fox_bwd_dq(fq, fk, fv, cq, ck, lse, ofh, doh, SCAN_HEADS, FOX_BLOCK)
    dkh, dvh, dck = _fox_bwd_dkv(fq, fk, fv, cq, ck, lse, ofh, doh, SCAN_HEADS, FOX_BLOCK)
    dcum = jnp.pad(jnp.stack([dcq.reshape(c.h, t).T, dck.reshape(c.h, t).T]), ((0, 0), (0, 0), (0, LANES - c.h)))
    dfl, db_f = _fox_logit_bwd(dcum, p, c, w["b_f"])

    dmq, dmk, dmv = _mem_attn_bwd(p, c, mkv, dym)
    dmkv = jnp.concatenate([dmk, dmv], axis=1)
    dw_mkv = _matmul(memn, dmkv, ta=True, name="d_w_mem_kv")
    dmemn = _matmul(dmkv, w["w_mem_kv"], tb=True, name="d_memn")
    _, dg_mem = _rms_bwd(dmemn, mem, rinv_m, w["g_mem"], jnp.zeros_like(mem), "rms_mem_bwd")

    dp = jnp.concatenate([dp_shift, dg_r] + [uh(z).astype(BF16) for z in (dqh, dkh, dvh)]
                         + [dg_f, dmq, dg_m, dfl], axis=1)
    dh = _matmul(dp, w["wp"], tb=True, name="d_h", tk=2944)
    dwp = _matmul(h, dp, ta=True, name="d_w_in", tk=4096)
    grad_x, dg_pre = _rms_bwd(dh, x, rinv, w["g_pre"], dout, "rms_pre_bwd")

    small = dict(g_pre=dg_pre, mu=dmu, w0=dw0, a0=da0, k_k=dk_k, k_a=dk_a, r_k=dr_k, ln_w=dln_w, ln_b=dln_b,
                 b_f=db_f, g_mem=dg_mem, g_post=dg_post)
    return loss, grad_x, dict(wp=dwp, wd=dwd, wi=dwi, w_mem_kv=dw_mkv, w_out=dw_out), small


CHIPS = ((1, 0, 0), (0, 1, 0), (1, 1, 0))
SIBLING = ((0, 0, 1),)
ALL_PEERS = tuple((i, j, k) for i in (0, 1) for j in (0, 1) for k in (0, 1))[1:]


def _chip_of(pos):
    return 2 * pos[0] + pos[1]


DMA_CHUNK = 4 << 20


def _pieces(shape, itemsize):
    lead, (rows, cols) = shape[:-2], shape[-2:]
    k = 1
    if rows % 16 == 0:
        k = max(1, min(rows // 16, -(-rows * cols * itemsize // DMA_CHUNK)))
        while rows % k or (rows // k) % 16:
            k -= 1
    band = rows // k
    idxs = [()]
    for n in lead:
        idxs = [i + (j,) for i in idxs for j in range(n)]
    return [i + (pl.ds(j * band, band),) for i in idxs for j in range(k)]


def _peer_of(me, mask):
    return tuple(1 - v if f else v for v, f in zip(me, mask))


def _exchange(name, groups):
    n = len(groups)
    plan = []
    for gi, g in enumerate(groups):
        for ti in range(len(g["transfers"])):
            for idx in _pieces(tuple(g["piece"]), g["src"].dtype.itemsize):
                plan.append((gi, ti, idx))

    def body(*refs):
        srcs, outs = refs[:n], refs[n:2 * n]
        send_sems, recv_sems = refs[2 * n:]
        me = (lax.axis_index("x"), lax.axis_index("y"), lax.axis_index("c"))
        copies = []
        for k, (gi, ti, idx) in enumerate(plan):
            mask, view, slot = groups[gi]["transfers"][ti]
            peer = _peer_of(me, mask)
            copies.append(pltpu.make_async_remote_copy(
                src_ref=view(srcs[gi], me, peer).at[idx], dst_ref=outs[gi].at[slot(me, peer)].at[idx],
                send_sem=send_sems.at[k], recv_sem=recv_sems.at[k],
                device_id=peer, device_id_type=MESH))
        for cp in copies:
            cp.start()
        for cp in copies:
            cp.wait()

    any_spec = pl.BlockSpec(memory_space=pl.ANY)
    return pl.pallas_call(
        body, name=name,
        in_specs=[any_spec] * n, out_specs=[any_spec] * n,
        out_shape=[jax.ShapeDtypeStruct((g["slots"],) + tuple(g["piece"]), g["src"].dtype) for g in groups],
        scratch_shapes=[pltpu.SemaphoreType.DMA((len(plan),)), pltpu.SemaphoreType.DMA((len(plan),))],
    )(*[g["src"] for g in groups])


def _my_chip():
    return 2 * lax.axis_index("x") + lax.axis_index("y")


def _put(buf, block, slot):
    return lax.dynamic_update_slice(buf, block[None], (slot,) + (0,) * block.ndim)


def _sum_slots(recv, own, k, out_dtype, name):
    s, rows, cols = recv.shape
    budget = max(16, ((4 << 20) // ((s + 1) * cols * 4)) // 16 * 16)
    tr = _tile(rows, budget, 16)
    own_many = own.shape[0] > 1

    def body(k_ref, *refs):
        out_ref = refs[s + 1]
        mine = refs[s][0].astype(F32)
        acc = None
        for i in range(s):
            term = jnp.where(k_ref[0] == i, mine, refs[i][0].astype(F32))
            acc = term if acc is None else acc + term
        out_ref[...] = acc.astype(out_ref.dtype)

    def slot_spec(i):
        return pl.BlockSpec((1, tr, cols), lambda j, kr: (jnp.where(kr[0] == i, (i + 1) % s, i), j, 0))

    grid_spec = pltpu.PrefetchScalarGridSpec(
        num_scalar_prefetch=1, grid=(rows // tr,),
        in_specs=[slot_spec(i) for i in range(s)]
                 + [pl.BlockSpec((1, tr, cols), lambda j, kr: (kr[0] if own_many else 0, j, 0))],
        out_specs=pl.BlockSpec((tr, cols), lambda j, kr: (j, 0)))
    return pl.pallas_call(
        body, name=name, grid_spec=grid_spec,
        out_shape=jax.ShapeDtypeStruct((rows, cols), out_dtype),
        compiler_params=_params("parallel"),
    )(k, *([recv] * s), own)


def _all_gather(shards):
    core, chip = lax.axis_index("c"), _my_chip()
    other_chip = lambda m: (lambda me: _chip_of(_peer_of(me, m)))
    halves = [s.reshape(2, s.shape[0] // 2, s.shape[1]) for s in shards]
    first = _exchange("gather_chips", [
        dict(src=q, slots=4, piece=q.shape[1:],
             transfers=[(m, lambda ref, me, peer: ref.at[me[2]], lambda me, peer: _chip_of(me)) for m in CHIPS])
        for q in halves])
    second = _exchange("gather_pair", [
        dict(src=q, slots=4, piece=q.shape[1:],
             transfers=[(SIBLING[0], (lambda f: lambda ref, me, peer: ref.at[f(me)])(other_chip(m)),
                         (lambda f: lambda me, peer: f(me))(other_chip(m))) for m in CHIPS])
        for q in first])
    out = []
    for s, a, b in zip(shards, first, second):
        full = jnp.concatenate([jnp.where(core == 0, a, b), jnp.where(core == 0, b, a)], axis=1)
        out.append(_put(full, s, chip))
    return out


def _reduce_scatter(partials):
    core, chip = lax.axis_index("c"), _my_chip()
    core1, chip1 = core.reshape(1).astype(jnp.int32), chip.reshape(1).astype(jnp.int32)
    halves = [q.reshape(4, 2, q.shape[1] // 2, q.shape[2]).transpose(1, 0, 2, 3) for q in partials]
    pair = _exchange("reduce_pair", [
        dict(src=q, slots=2, piece=q.shape[1:],
             transfers=[(SIBLING[0], lambda ref, me, peer: ref.at[peer[2]], lambda me, peer: me[2])])
        for q in halves])
    flat = lambda e: e.reshape(2, -1, e.shape[-1])
    chip_sums = [_sum_slots(flat(e), flat(q), core1, BF16, "reduce_pair_sum").reshape(q.shape[1:])
                 for e, q in zip(pair, halves)]
    crossed = _exchange("reduce_chips", [
        dict(src=q, slots=4, piece=q.shape[1:],
             transfers=[(m, lambda ref, me, peer: ref.at[_chip_of(peer)], lambda me, peer: _chip_of(me)) for m in CHIPS])
        for q in chip_sums])
    sums = [_sum_slots(e, q, chip1, F32, "reduce_chips_sum") for e, q in zip(crossed, chip_sums)]
    swapped = _exchange("reduce_swap", [
        dict(src=q, slots=2, piece=q.shape, transfers=[(SIBLING[0], lambda ref, me, peer: ref, lambda me, peer: me[2])])
        for q in sums])
    return [_put(e, q, core).reshape(-1, e.shape[-1]) for e, q in zip(swapped, sums)]


def _all_reduce_small(vec):
    dev = 4 * lax.axis_index("x") + 2 * lax.axis_index("y") + lax.axis_index("c")
    got = _exchange("reduce_small", [
        dict(src=vec, slots=8, piece=vec.shape,
             transfers=[(m, lambda ref, me, peer: ref, lambda me, peer: 4 * me[0] + 2 * me[1] + me[2])
                        for m in ALL_PEERS])])[0]
    return _sum_slots(got, vec[None], dev.reshape(1).astype(jnp.int32), F32, "reduce_small_sum")


SMALL = ("g_pre", "mu", "w0", "a0", "k_k", "k_a", "r_k", "ln_w", "ln_b", "b_f", "g_mem", "g_post")


def _pad_cols(a, n):
    return jnp.pad(a, ((0, 0),) * (a.ndim - 1) + ((0, n - a.shape[-1]),))


def kernel(x, mem, g_pre, w_in, mu_rwkv, w0, w_decay_up, a0, w_iclr_up, k_k, k_a, r_k, ln_x_w, ln_x_b, b_f, g_mem, w_mem_kv, w_out, g_post, loss_target, m_g_pre, m_w_in, m_mu_rwkv, m_w0, m_w_decay_up, m_a0, m_w_iclr_up, m_k_k, m_k_a, m_r_k, m_ln_x_w, m_ln_x_b, m_b_f, m_g_mem, m_w_mem_kv, m_w_out, m_g_post, v_g_pre, v_w_in, v_mu_rwkv, v_w0, v_w_decay_up, v_a0, v_w_iclr_up, v_k_k, v_k_a, v_r_k, v_ln_x_w, v_ln_x_b, v_b_f, v_g_mem, v_w_mem_kv, v_w_out, v_g_post):
    d = x.shape[-1]
    c = Cfg(d)
    ws = w_in.shape[-1]
    wpad = -(-ws // LANES) * LANES
    nh = c.h

    g_in, g_out, g_mkv, g_wd, g_wi = _all_gather([
        _pad_cols(w_in[0].astype(BF16), wpad), w_out[0].astype(BF16), w_mem_kv[0].astype(BF16),
        w_decay_up[0].astype(BF16), w_iclr_up[0].astype(BF16)])
    w_full = jnp.concatenate([g_in[s, :, :ws] for s in range(4)], axis=1)
    fl = c.ref_fl
    wp = jnp.concatenate([w_full[:, :fl], w_full[:, fl + nh:], _pad_cols(w_full[:, fl:fl + nh], LANES)], axis=1)
    unshard = lambda g: g.transpose(1, 0, 2).reshape(g.shape[1], -1)
    weights = dict(wp=wp, w_out=g_out.reshape(-1, d), w_mem_kv=g_mkv.reshape(d, -1), wd=unshard(g_wd), wi=unshard(g_wi),
                   g_pre=g_pre, mu=mu_rwkv, w0=w0, a0=a0, k_k=k_k, k_a=k_a, r_k=r_k.reshape(1, -1),
                   ln_w=ln_x_w, ln_b=ln_x_b, b_f=_pad_cols(b_f, LANES), g_mem=g_mem, g_post=g_post)

    loss, grad_x, big, small = _local_step(c, x[0], mem[0], loss_target[0], weights)

    dwp = big["wp"]
    dw_full = jnp.concatenate([dwp[:, :fl], dwp[:, c.o_fl:c.o_fl + nh], dwp[:, fl:c.o_fl]], axis=1)
    by_chip = lambda g: jnp.stack(jnp.split(g, 4, axis=1))
    red = _reduce_scatter([
        jnp.stack([_pad_cols(dw_full[:, s * ws:(s + 1) * ws], wpad) for s in range(4)]).astype(BF16),
        big["w_out"].reshape(4, -1, d).astype(BF16), big["w_mem_kv"].reshape(4, d // 4, -1).astype(BF16),
        by_chip(big["wd"]).astype(BF16), by_chip(big["wi"]).astype(BF16)])
    big_w = (w_in[0], w_out[0], w_mem_kv[0], w_decay_up[0], w_iclr_up[0])
    big_m = (m_w_in[0], m_w_out[0], m_w_mem_kv[0], m_w_decay_up[0], m_w_iclr_up[0])
    big_v = (v_w_in[0], v_w_out[0], v_w_mem_kv[0], v_w_decay_up[0], v_w_iclr_up[0])
    big_names = ("w_in", "w_out", "w_mem_kv", "w_decay_up", "w_iclr_up")
    upd = {n: _adamw(w_, g_, m_, v_, "adamw_" + n) for n, w_, g_, m_, v_ in zip(big_names, big_w, red, big_m, big_v)}

    small_w = dict(g_pre=g_pre, mu=mu_rwkv, w0=w0, a0=a0, k_k=k_k, k_a=k_a, r_k=r_k.reshape(1, -1), ln_w=ln_x_w,
                   ln_b=ln_x_b, b_f=b_f, g_mem=g_mem, g_post=g_post)
    small_m = dict(g_pre=m_g_pre, mu=m_mu_rwkv, w0=m_w0, a0=m_a0, k_k=m_k_k, k_a=m_k_a, r_k=m_r_k.reshape(1, -1),
                   ln_w=m_ln_x_w, ln_b=m_ln_x_b, b_f=m_b_f, g_mem=m_g_mem, g_post=m_g_post)
    small_v = dict(g_pre=v_g_pre, mu=v_mu_rwkv, w0=v_w0, a0=v_a0, k_k=v_k_k, k_a=v_k_a, r_k=v_r_k.reshape(1, -1),
                   ln_w=v_ln_x_w, ln_b=v_ln_x_b, b_f=v_b_f, g_mem=v_g_mem, g_post=v_g_post)
    widths = [-(-small_w[n].shape[1] // LANES) * LANES for n in SMALL]
    pack = lambda t: jnp.concatenate([_pad_cols(t[n], wd_) for n, wd_ in zip(SMALL, widths)]
                                     + [jnp.zeros((1, LANES), F32)], axis=1)
    g_packed = jnp.concatenate([_pad_cols(small[n], wd_) for n, wd_ in zip(SMALL, widths)]
                               + [_pad_cols(loss, LANES)], axis=1)
    g_sum = _all_reduce_small(g_packed)
    s_upd = _adamw(pack(small_w), g_sum, pack(small_m), pack(small_v), "adamw_small")
    offs = [sum(widths[:i]) for i in range(len(SMALL))]

    def take(kind, n):
        i = SMALL.index(n)
        piece = s_upd[kind][:, offs[i]:offs[i] + small_w[n].shape[1]]
        return piece.reshape(r_k.shape) if n == "r_k" else piece

    total_loss = g_sum[0, sum(widths)]
    order = ("g_pre", "w_in", "mu", "w0", "w_decay_up", "a0", "w_iclr_up", "k_k", "k_a", "r_k", "ln_w", "ln_b", "b_f",
             "g_mem", "w_mem_kv", "w_out", "g_post")
    outs = [total_loss, grad_x[None]]
    for kind in range(4):
        for n in order:
            outs.append(upd[n][kind][None] if n in upd else take(kind, n))
    return tuple(outs)
```

```python
import functools

import jax
import jax.numpy as jnp
from jax import lax
from jax.experimental import pallas as pl
from jax.experimental.pallas import tpu as pltpu

F32 = jnp.float32
BF16 = jnp.bfloat16
HI = lax.Precision.HIGHEST
MESH = pl.DeviceIdType.MESH

HEAD_DIM = 64
MEM_HEADS = 4
LORA = 128
CHUNK = 64
RMS_EPS = 1e-6
GN_EPS = 64e-5
LANES = 128
VMEM_LIMIT = 56 * 1024 * 1024

ADAM_LR, ADAM_B1, ADAM_B2, ADAM_EPS, ADAM_WD, ADAM_STEP = 0.001, 0.9, 0.999, 1e-08, 0.01, 10


class Cfg:
    def __init__(self, d):
        self.d = d
        self.rw = 3 * d // 8
        self.mw = d // 4
        self.h = self.rw // HEAD_DIM
        self.mhd = self.mw // MEM_HEADS
        self.shift = 3 * self.rw + 2 * LORA
        self.in_width = self.shift + 5 * self.rw + self.h + 2 * self.mw
        o = self.shift
        self.o_grw = o; o += self.rw
        self.o_fq = o; o += self.rw
        self.o_fk = o; o += self.rw
        self.o_fv = o; o += self.rw
        self.o_gfox = o; o += self.rw
        self.o_mq = o; o += self.mw
        self.o_gmq = o; o += self.mw
        self.o_fl = o; o += LANES
        self.wp = o
        self.ref_fl = self.shift + 4 * self.rw


def _tile(n, pref, align=LANES):
    if n <= pref:
        return n
    t = (pref // align) * align
    while t >= align:
        if n % t == 0:
            return t
        t -= align
    return n


def _params(*sem):
    return pltpu.CompilerParams(dimension_semantics=sem, vmem_limit_bytes=VMEM_LIMIT)


def _sig(x):
    return 1.0 / (1.0 + jnp.exp(-x))


def _softplus(x):
    return jnp.maximum(x, 0.0) + jnp.log(1.0 + jnp.exp(-jnp.abs(x)))


def _dot(a, b, dims, prec=None):
    return lax.dot_general(a, b, (dims, ((), ())), precision=prec, preferred_element_type=F32)


def _mm(a, b, prec=None):
    return _dot(a, b, ((1,), (0,)), prec)


def _mm_nt(a, b, prec=None):
    return _dot(a, b, ((1,), (1,)), prec)


def _mm_tn(a, b, prec=None):
    return _dot(a, b, ((0,), (0,)), prec)


def _split(a):
    hi = a.astype(BF16)
    return hi, (a - hi.astype(F32)).astype(BF16)


def _dot3(a, b, dims):
    (ah, al), (bh, bl) = _split(a), _split(b)
    d = lambda x, y: lax.dot_general(x, y, dims, preferred_element_type=F32)
    return d(ah, bh) + (d(ah, bl) + d(al, bh))


def _bmm(a, b):
    return _dot3(a, b, (((2,), (1,)), ((0,), (0,))))


def _bmm_nt(a, b):
    return _dot3(a, b, (((2,), (2,)), ((0,), (0,))))


def _bmm_tn(a, b):
    return _dot3(a, b, (((1,), (1,)), ((0,), (0,))))


def _bmm_01(m01, x):
    x1 = x.astype(BF16)
    r1 = x - x1.astype(F32)
    x2 = r1.astype(BF16)
    x3 = (r1 - x2.astype(F32)).astype(BF16)
    d = lambda y: lax.dot_general(m01, y, (((2,), (1,)), ((0,), (0,))), preferred_element_type=F32)
    return d(x1) + (d(x2) + d(x3))


def _matmul(a, b, *, ta=False, tb=False, out_dtype=F32, name, tm=1024, tn=1024, tk=1024):
    m, k = (a.shape[1], a.shape[0]) if ta else a.shape
    n = b.shape[0] if tb else b.shape[1]
    tm, tn, tk = _tile(m, tm), _tile(n, tn), _tile(k, tk)
    nk = k // tk
    dims = ((0 if ta else 1,), (1 if tb else 0,))

    def body(a_ref, b_ref, o_ref, *acc):
        part = _dot(a_ref[...].astype(BF16), b_ref[...].astype(BF16), dims)
        if nk == 1:
            o_ref[...] = part.astype(o_ref.dtype)
            return
        kk = pl.program_id(2)

        @pl.when(kk == 0)
        def _():
            acc[0][...] = part

        @pl.when(kk > 0)
        def _():
            acc[0][...] += part

        @pl.when(kk == nk - 1)
        def _():
            o_ref[...] = acc[0][...].astype(o_ref.dtype)

    a_spec = pl.BlockSpec((tk, tm), lambda i, j, kk: (kk, i)) if ta else pl.BlockSpec((tm, tk), lambda i, j, kk: (i, kk))
    b_spec = pl.BlockSpec((tn, tk), lambda i, j, kk: (j, kk)) if tb else pl.BlockSpec((tk, tn), lambda i, j, kk: (kk, j))
    return pl.pallas_call(
        body, name=name, grid=(m // tm, n // tn, nk),
        in_specs=[a_spec, b_spec], out_specs=pl.BlockSpec((tm, tn), lambda i, j, kk: (i, j)),
        out_shape=jax.ShapeDtypeStruct((m, n), out_dtype),
        scratch_shapes=[pltpu.VMEM((tm, tn), F32)] if nk > 1 else [],
        compiler_params=_params("parallel", "parallel", "arbitrary"),
    )(a, b)


def _rms_fwd(x, g, name):
    t, d = x.shape
    tm = _tile(t, 256, 8)

    def body(x_ref, g_ref, h_ref, r_ref):
        xv = x_ref[...]
        r = lax.rsqrt(jnp.mean(xv * xv, axis=-1, keepdims=True) + RMS_EPS)
        h_ref[...] = (xv * r * g_ref[...]).astype(BF16)
        r_ref[...] = r

    return pl.pallas_call(
        body, name=name, grid=(t // tm,),
        in_specs=[pl.BlockSpec((tm, d), lambda i: (i, 0)), pl.BlockSpec((1, d), lambda i: (0, 0))],
        out_specs=[pl.BlockSpec((tm, d), lambda i: (i, 0)), pl.BlockSpec((tm, 1), lambda i: (i, 0))],
        out_shape=[jax.ShapeDtypeStruct((t, d), BF16), jax.ShapeDtypeStruct((t, 1), F32)],
        compiler_params=_params("parallel"),
    )(x, g)


def _rms_bwd(dh, x, rinv, g, add, name):
    t, d = x.shape
    tm = _tile(t, 256, 8)

    def body(dh_ref, x_ref, r_ref, g_ref, add_ref, dx_ref, dg_ref):
        @pl.when(pl.program_id(0) == 0)
        def _():
            dg_ref[...] = jnp.zeros_like(dg_ref)

        r = r_ref[...]
        xn = x_ref[...] * r
        dhv = dh_ref[...]
        dg_ref[...] += jnp.sum(dhv * xn, axis=0, keepdims=True)
        dxn = dhv * g_ref[...]
        dx_ref[...] = add_ref[...] + r * (dxn - xn * jnp.mean(dxn * xn, axis=-1, keepdims=True))

    row = pl.BlockSpec((tm, d), lambda i: (i, 0))
    vec = pl.BlockSpec((1, d), lambda i: (0, 0))
    return pl.pallas_call(
        body, name=name, grid=(t // tm,),
        in_specs=[row, row, pl.BlockSpec((tm, 1), lambda i: (i, 0)), vec, row],
        out_specs=[row, vec],
        out_shape=[jax.ShapeDtypeStruct((t, d), F32), jax.ShapeDtypeStruct((1, d), F32)],
        compiler_params=_params("arbitrary"),
    )(dh, x, rinv, g, add)


def _post_loss(yo, x, tgt, g, name):
    t, d = x.shape
    tm = _tile(t, 256, 8)

    def body(yo_ref, x_ref, t_ref, g_ref, loss_ref, dout_ref, dyo_ref, dg_ref):
        @pl.when(pl.program_id(0) == 0)
        def _():
            dg_ref[...] = jnp.zeros_like(dg_ref)
            loss_ref[...] = jnp.zeros_like(loss_ref)

        yv = yo_ref[...]
        r = lax.rsqrt(jnp.mean(yv * yv, axis=-1, keepdims=True) + RMS_EPS)
        n = yv * r
        err = x_ref[...] + n * g_ref[...] - t_ref[...]
        loss_ref[...] += 0.5 * jnp.sum(jnp.mean(err * err, axis=-1, keepdims=True), axis=0, keepdims=True)
        dout = err * (1.0 / d)
        dout_ref[...] = dout
        dg_ref[...] += jnp.sum(dout * n, axis=0, keepdims=True)
        dn = dout * g_ref[...]
        dyo_ref[...] = (r * (dn - n * jnp.mean(dn * n, axis=-1, keepdims=True))).astype(BF16)

    row = pl.BlockSpec((tm, d), lambda i: (i, 0))
    vec = pl.BlockSpec((1, d), lambda i: (0, 0))
    return pl.pallas_call(
        body, name=name, grid=(t // tm,),
        in_specs=[row, row, row, vec],
        out_specs=[pl.BlockSpec((1, 1), lambda i: (0, 0)), row, row, vec],
        out_shape=[jax.ShapeDtypeStruct((1, 1), F32), jax.ShapeDtypeStruct((t, d), F32),
                   jax.ShapeDtypeStruct((t, d), BF16), jax.ShapeDtypeStruct((1, d), F32)],
        compiler_params=_params("arbitrary"),
    )(yo, x, tgt, g)


def _head_sum(x):
    ri = lax.broadcasted_iota(jnp.int32, (LANES, LANES), 0) // HEAD_DIM
    ci = lax.broadcasted_iota(jnp.int32, (LANES, LANES), 1) // HEAD_DIM
    e = (ri == ci).astype(BF16)
    x1 = x.astype(BF16)
    r1 = x - x1.astype(F32)
    x2 = r1.astype(BF16)
    x3 = (r1 - x2.astype(F32)).astype(BF16)
    parts = []
    for i in range(x.shape[1] // LANES):
        sl = slice(i * LANES, (i + 1) * LANES)
        parts.append(_mm(x1[:, sl], e) + (_mm(x2[:, sl], e) + _mm(x3[:, sl], e)))
    return parts[0] if len(parts) == 1 else jnp.concatenate(parts, axis=1)


def _shifted(p_cur, before, first, mu):
    rolled = pltpu.roll(p_cur, 1, 0)
    prev_row = jnp.where(first, 0.0, before)
    row0 = lax.broadcasted_iota(jnp.int32, p_cur.shape, 0) == 0
    prev = jnp.where(row0, prev_row, rolled)
    return p_cur + (prev - p_cur) * mu, prev


def _rwkv_features(ps, rw, w0, a0, k_k, k_a, wd, wi):
    r, k, v = ps[:, 0:rw], ps[:, rw:2 * rw], ps[:, 2 * rw:3 * rw]
    wl, al = ps[:, 3 * rw:3 * rw + LORA], ps[:, 3 * rw + LORA:3 * rw + 2 * LORA]
    tw = jnp.tanh(wl)
    zw = w0 + _mm(tw.astype(BF16), wd)
    logw = -jnp.exp(-_softplus(-zw) - 0.5)
    alpha = _sig(a0 + _mm(al.astype(BF16), wi))
    kkr = k * k_k
    n2 = _head_sum(kkr * kkr)
    rn = lax.rsqrt(jnp.maximum(n2, 1e-24))
    kk = kkr * rn
    kmod = k * (1.0 + (alpha - 1.0) * k_a)
    return dict(r=r, k=k, v=v, tw=tw, al=al, zw=zw, logw=logw, alpha=alpha, kk=kk, rn=rn, n2=n2, kmod=kmod)


def _rwkv_pre_fwd(p, c, mu, w0, a0, k_k, k_a, wd, wi):
    t = p.shape[0]
    tm = _tile(t, 128, 8)
    rw, sh = c.rw, c.shift

    def body(p_ref, pp_ref, mu_ref, w0_ref, a0_ref, kk_ref, ka_ref, wd_ref, wi_ref,
             r_ref, lw_ref, km_ref, v_ref, a_ref, b_ref):
        ps, _ = _shifted(p_ref[...], pp_ref[7:8, :], pl.program_id(0) == 0, mu_ref[...])
        f = _rwkv_features(ps, rw, w0_ref[...], a0_ref[...], kk_ref[...], ka_ref[...], wd_ref[...], wi_ref[...])
        r_ref[...] = f["r"]
        lw_ref[...] = f["logw"]
        km_ref[...] = f["kmod"]
        v_ref[...] = f["v"]
        a_ref[...] = -f["kk"]
        b_ref[...] = f["kk"] * f["alpha"]

    vec = lambda n: pl.BlockSpec((1, n), lambda i: (0, 0))
    out = pl.BlockSpec((tm, rw), lambda i: (i, 0))
    return pl.pallas_call(
        body, name="rwkv_pre_fwd", grid=(t // tm,),
        in_specs=[pl.BlockSpec((tm, sh), lambda i: (i, 0)),
                  pl.BlockSpec((8, sh), lambda i: (jnp.maximum(i * (tm // 8) - 1, 0), 0)),
                  vec(sh), vec(rw), vec(rw), vec(rw), vec(rw),
                  pl.BlockSpec((LORA, rw), lambda i: (0, 0)), pl.BlockSpec((LORA, rw), lambda i: (0, 0))],
        out_specs=[out] * 6,
        out_shape=[jax.ShapeDtypeStruct((t, rw), F32)] * 6,
        compiler_params=_params("parallel"),
    )(p, p, mu, w0, a0, k_k, k_a, wd, wi)


def _rwkv_pre_bwd(p, c, mu, w0, a0, k_k, k_a, wd, wi, dr, dlw, dkm, dv, da, db, dr2, dkm2, dv2):
    t = p.shape[0]
    tm = _tile(t, 128, 8)
    rw, sh = c.rw, c.shift

    def body(p_ref, pp_ref, mu_ref, w0_ref, a0_ref, kk_ref, ka_ref, wd_ref, wi_ref,
             dr_ref, dlw_ref, dkm_ref, dv_ref, da_ref, db_ref, dr2_ref, dkm2_ref, dv2_ref,
             dps_ref, dzw_ref, dza_ref, tw_ref, al_ref, dw0_ref, da0_ref, dkk_ref, dka_ref):
        @pl.when(pl.program_id(0) == 0)
        def _():
            for ref in (dw0_ref, da0_ref, dkk_ref, dka_ref):
                ref[...] = jnp.zeros_like(ref)

        ps, _ = _shifted(p_ref[...], pp_ref[7:8, :], pl.program_id(0) == 0, mu_ref[...])
        k_k, k_a = kk_ref[...], ka_ref[...]
        f = _rwkv_features(ps, rw, w0_ref[...], a0_ref[...], k_k, k_a, wd_ref[...], wi_ref[...])
        alpha, kk, k = f["alpha"], f["kk"], f["k"]
        dkm = dkm_ref[...] + dkm2_ref[...]
        db = db_ref[...]
        dkk = db * alpha - da_ref[...]
        dalpha = db * kk + dkm * k * k_a
        dk = dkm * (1.0 + (alpha - 1.0) * k_a)
        dka_ref[...] += jnp.sum(dkm * k * (alpha - 1.0), axis=0, keepdims=True)
        dkkr = f["rn"] * jnp.where(f["n2"] > 1e-24, dkk - kk * _head_sum(dkk * kk), dkk)
        dk = dk + dkkr * k_k
        dkk_ref[...] += jnp.sum(dkkr * k, axis=0, keepdims=True)
        dza = dalpha * alpha * (1.0 - alpha)
        da0_ref[...] += jnp.sum(dza, axis=0, keepdims=True)
        dzw = dlw_ref[...] * f["logw"] * _sig(-f["zw"])
        dw0_ref[...] += jnp.sum(dzw, axis=0, keepdims=True)
        dza_b, dzw_b = dza.astype(BF16), dzw.astype(BF16)
        dal = _mm_nt(dza_b, wi_ref[...])
        dwl = _mm_nt(dzw_b, wd_ref[...]) * (1.0 - f["tw"] * f["tw"])
        dps_ref[:, 0:rw] = dr_ref[...] + dr2_ref[...]
        dps_ref[:, rw:2 * rw] = dk
        dps_ref[:, 2 * rw:3 * rw] = dv_ref[...] + dv2_ref[...]
        dps_ref[:, 3 * rw:3 * rw + LORA] = dwl
        dps_ref[:, 3 * rw + LORA:sh] = dal
        dzw_ref[...] = dzw_b
        dza_ref[...] = dza_b
        tw_ref[...] = f["tw"].astype(BF16)
        al_ref[...] = f["al"].astype(BF16)

    vec = lambda n: pl.BlockSpec((1, n), lambda i: (0, 0))
    blk = lambda n: pl.BlockSpec((tm, n), lambda i: (i, 0))
    return pl.pallas_call(
        body, name="rwkv_pre_bwd", grid=(t // tm,),
        in_specs=[blk(sh), pl.BlockSpec((8, sh), lambda i: (jnp.maximum(i * (tm // 8) - 1, 0), 0)),
                  vec(sh), vec(rw), vec(rw), vec(rw), vec(rw),
                  pl.BlockSpec((LORA, rw), lambda i: (0, 0)), pl.BlockSpec((LORA, rw), lambda i: (0, 0))]
                 + [blk(rw)] * 9,
        out_specs=[blk(sh), blk(rw), blk(rw), blk(LORA), blk(LORA), vec(rw), vec(rw), vec(rw), vec(rw)],
        out_shape=[jax.ShapeDtypeStruct((t, sh), F32), jax.ShapeDtypeStruct((t, rw), BF16),
                   jax.ShapeDtypeStruct((t, rw), BF16), jax.ShapeDtypeStruct((t, LORA), BF16),
                   jax.ShapeDtypeStruct((t, LORA), BF16)] + [jax.ShapeDtypeStruct((1, rw), F32)] * 4,
        compiler_params=_params("arbitrary"),
    )(p, p, mu, w0, a0, k_k, k_a, wd, wi, dr, dlw, dkm, dv, da, db, dr2, dkm2, dv2)


def _shift_bwd(dps, p, c, mu):
    t = p.shape[0]
    tm = _tile(t, 256, 8)
    sh = c.shift
    nt = t // tm

    def body(d_ref, dn_ref, p_ref, pp_ref, mu_ref, dp_ref, dmu_ref):
        i = pl.program_id(0)

        @pl.when(i == 0)
        def _():
            dmu_ref[...] = jnp.zeros_like(dmu_ref)

        mu = mu_ref[...]
        d = d_ref[...]
        pc = p_ref[...]
        _, prev = _shifted(pc, pp_ref[7:8, :], i == 0, mu)
        dmu_ref[...] += jnp.sum(d * (prev - pc), axis=0, keepdims=True)
        nxt_row = jnp.where(i == nt - 1, 0.0, dn_ref[0:1, :])
        last = lax.broadcasted_iota(jnp.int32, d.shape, 0) == tm - 1
        nxt = jnp.where(last, nxt_row, pltpu.roll(d, tm - 1, 0))
        dp_ref[...] = (d * (1.0 - mu) + nxt * mu).astype(BF16)

    blk = pl.BlockSpec((tm, sh), lambda i: (i, 0))
    return pl.pallas_call(
        body, name="shift_bwd", grid=(nt,),
        in_specs=[blk, pl.BlockSpec((8, sh), lambda i: (jnp.minimum((i + 1) * (tm // 8), t // 8 - 1), 0)),
                  blk, pl.BlockSpec((8, sh), lambda i: (jnp.maximum(i * (tm // 8) - 1, 0), 0)),
                  pl.BlockSpec((1, sh), lambda i: (0, 0))],
        out_specs=[blk, pl.BlockSpec((1, sh), lambda i: (0, 0))],
        out_shape=[jax.ShapeDtypeStruct((t, sh), BF16), jax.ShapeDtypeStruct((1, sh), F32)],
        compiler_params=_params("arbitrary"),
    )(dps, dps, p, p, mu)


def _tri(n, strict):
    ri = lax.broadcasted_iota(jnp.int32, (n, n), 0)
    ci = lax.broadcasted_iota(jnp.int32, (n, n), 1)
    return (ri > ci) if strict else (ri >= ci)


def _unit_lower_inverse(a):
    n = a.shape[-1]
    ri = lax.broadcasted_iota(jnp.int32, (n, n), 0)
    ci = lax.broadcasted_iota(jnp.int32, (n, n), 1)
    eye = (ri == ci).astype(F32)
    blk = lambda s: (ri // s) == (ci // s)
    ad = jnp.where(blk(16), a, 0.0)
    p = eye + ad
    for _ in range(3):
        ad = _bmm(ad, ad)
        p = p + _bmm(p, ad)
    s = 16
    while s < n:
        off = jnp.where(blk(2 * s) & ~blk(s), a, 0.0)
        p = p + _bmm(_bmm(p, off), p)
        s *= 2
    return p


def _chunk_common(r, lw, k, a, b):
    n = r.shape[1]
    tri_incl = jnp.broadcast_to(_tri(n, False).astype(BF16), (r.shape[0], n, n))
    cum = _bmm_01(tri_incl, lw)
    e_pos, e_neg, e_exc = jnp.exp(cum), jnp.exp(-cum), jnp.exp(cum - lw)
    last = lax.broadcasted_iota(jnp.int32, (n, r.shape[2]), 0) == n - 1
    g_last = jnp.exp(jnp.sum(jnp.where(last, cum, 0.0), axis=1, keepdims=True))
    return g_last, r * e_pos, a * e_exc, b * e_neg, k * e_neg, e_pos, e_neg, e_exc


def _chunk_solve(rt, at, bt, kt, v, g0):
    strict, incl = _tri(rt.shape[1], True), _tri(rt.shape[1], False)
    a_ab = jnp.where(strict, _bmm_nt(at, bt), 0.0)
    a_ak = jnp.where(strict, _bmm_nt(at, kt), 0.0)
    a_rb = jnp.where(incl, _bmm_nt(rt, bt), 0.0)
    a_rk = jnp.where(incl, _bmm_nt(rt, kt), 0.0)
    tinv = _unit_lower_inverse(a_ab)
    u = _bmm(tinv, _bmm(at, g0) + _bmm(a_ak, v))
    return a_ab, a_ak, a_rb, a_rk, tinv, u


def _diag_col(row, n):
    ri = lax.broadcasted_iota(jnp.int32, (n, n), 0)
    ci = lax.broadcasted_iota(jnp.int32, (n, n), 1)
    return jnp.sum(jnp.where(ri == ci, row, 0.0), axis=2, keepdims=True)


def _diag_row(col, n):
    ri = lax.broadcasted_iota(jnp.int32, (n, n), 0)
    ci = lax.broadcasted_iota(jnp.int32, (n, n), 1)
    return jnp.sum(jnp.where(ri == ci, col, 0.0), axis=1, keepdims=True)


def _rwkv_scan_fwd(r, lw, k, v, a, b, hb):
    h, t, n = r.shape
    nc = t // CHUNK

    def body(r_ref, lw_ref, k_ref, v_ref, a_ref, b_ref, y_ref, st_ref, g_sc):
        @pl.when(pl.program_id(1) == 0)
        def _():
            g_sc[...] = jnp.zeros_like(g_sc)

        g0 = g_sc[...]
        st_ref[0] = g0
        vv = v_ref[...]
        g_last, rt, at, bt, kt, _, _, _ = _chunk_common(r_ref[...], lw_ref[...], k_ref[...], a_ref[...], b_ref[...])
        _, _, a_rb, a_rk, _, u = _chunk_solve(rt, at, bt, kt, vv, g0)
        y_ref[...] = _bmm(rt, g0) + _bmm(a_rb, u) + _bmm(a_rk, vv)
        z = g0 + _bmm_tn(bt, u) + _bmm_tn(kt, vv)
        g_sc[...] = _diag_col(g_last, n) * z

    blk = pl.BlockSpec((hb, CHUNK, n), lambda i, j: (i, j, 0))
    return pl.pallas_call(
        body, name="rwkv_scan_fwd", grid=(h // hb, nc),
        in_specs=[blk] * 6,
        out_specs=[blk, pl.BlockSpec((1, hb, n, n), lambda i, j: (j, i, 0, 0))],
        out_shape=[jax.ShapeDtypeStruct((h, t, n), F32), jax.ShapeDtypeStruct((nc, h, n, n), F32)],
        scratch_shapes=[pltpu.VMEM((hb, n, n), F32)],
        compiler_params=_params("parallel", "arbitrary"),
    )(r, lw, k, v, a, b)


def _rwkv_scan_bwd(r, lw, k, v, a, b, states, dy, hb):
    h, t, n = r.shape
    nc = t // CHUNK

    def body(r_ref, lw_ref, k_ref, v_ref, a_ref, b_ref, st_ref, dy_ref,
             dr_ref, dlw_ref, dk_ref, dv_ref, da_ref, db_ref, dg_sc):
        @pl.when(pl.program_id(1) == 0)
        def _():
            dg_sc[...] = jnp.zeros_like(dg_sc)

        g0 = st_ref[0]
        vv, dyv, dh = v_ref[...], dy_ref[...], dg_sc[...]
        lwv = lw_ref[...]
        g_last, rt, at, bt, kt, e_pos, e_neg, e_exc = _chunk_common(r_ref[...], lwv, k_ref[...], a_ref[...], b_ref[...])
        a_ab, a_ak, a_rb, a_rk, tinv, u = _chunk_solve(rt, at, bt, kt, vv, g0)
        strict, incl = _tri(CHUNK, True), _tri(CHUNK, False)
        gcol = _diag_col(g_last, n)
        z = g0 + _bmm_tn(bt, u) + _bmm_tn(kt, vv)
        dz = gcol * dh
        dc_last = _diag_row(jnp.sum(dh * gcol * z, axis=2, keepdims=True), n)
        du = _bmm_tn(a_rb, dyv) + _bmm(bt, dz)
        dx = _bmm_tn(tinv, du)
        dv_ref[...] = _bmm_tn(a_rk, dyv) + _bmm(kt, dz) + _bmm_tn(a_ak, dx)
        da_ab = jnp.where(strict, _bmm_nt(dx, u), 0.0)
        da_ak = jnp.where(strict, _bmm_nt(dx, vv), 0.0)
        da_rb = jnp.where(incl, _bmm_nt(dyv, u), 0.0)
        da_rk = jnp.where(incl, _bmm_nt(dyv, vv), 0.0)
        d_at = _bmm(da_ab, bt) + _bmm(da_ak, kt) + _bmm_nt(dx, g0)
        d_rt = _bmm(da_rb, bt) + _bmm(da_rk, kt) + _bmm_nt(dyv, g0)
        d_bt = _bmm_tn(da_ab, at) + _bmm_tn(da_rb, rt) + _bmm_nt(u, dz)
        d_kt = _bmm_tn(da_ak, at) + _bmm_tn(da_rk, rt) + _bmm_nt(vv, dz)
        dg_sc[...] = dz + _bmm_tn(rt, dyv) + _bmm_tn(at, dx)
        dr_ref[...] = d_rt * e_pos
        da_ref[...] = d_at * e_exc
        db_ref[...] = d_bt * e_neg
        dk_ref[...] = d_kt * e_neg
        last = lax.broadcasted_iota(jnp.int32, (CHUNK, n), 0) == CHUNK - 1
        dc = d_rt * rt - d_bt * bt - d_kt * kt + jnp.where(last, dc_last, 0.0)
        dce = d_at * at
        ri = lax.broadcasted_iota(jnp.int32, (CHUNK, CHUNK), 0)
        ci = lax.broadcasted_iota(jnp.int32, (CHUNK, CHUNK), 1)
        up_incl = jnp.broadcast_to((ri <= ci).astype(BF16), (hb, CHUNK, CHUNK))
        dlw_ref[...] = _bmm_01(up_incl, dc + dce) - dce

    rev = lambda i, j: (i, nc - 1 - j, 0)
    blk = pl.BlockSpec((hb, CHUNK, n), rev)
    return pl.pallas_call(
        body, name="rwkv_scan_bwd", grid=(h // hb, nc),
        in_specs=[blk] * 6 + [pl.BlockSpec((1, hb, n, n), lambda i, j: (nc - 1 - j, i, 0, 0)), blk],
        out_specs=[blk] * 6,
        out_shape=[jax.ShapeDtypeStruct((h, t, n), F32)] * 6,
        scratch_shapes=[pltpu.VMEM((hb, n, n), F32)],
        compiler_params=_params("parallel", "arbitrary"),
    )(r, lw, k, v, a, b, states, dy)


def _silu_grad(g):
    s = _sig(g)
    return s * (1.0 + g * (1.0 - s))


def _group_norm(ys):
    yc = ys - _head_sum(ys) * (1.0 / HEAD_DIM)
    rstd = lax.rsqrt(_head_sum(yc * yc) * (1.0 / HEAD_DIM) + GN_EPS)
    return yc * rstd, rstd


def _rwkv_post_fwd(ys, r, km, v, p, c, ln_w, ln_b, r_k):
    t = ys.shape[0]
    tm = _tile(t, 512, 8)
    goff = c.o_grw // LANES

    def body(ys_ref, r_ref, km_ref, v_ref, g_ref, lw_ref, lb_ref, rk_ref, o_ref):
        yn, _ = _group_norm(ys_ref[...])
        s = _head_sum(r_ref[...] * km_ref[...] * rk_ref[...])
        g = g_ref[...]
        o_ref[...] = ((yn * lw_ref[...] + lb_ref[...] + s * v_ref[...]) * g * _sig(g)).astype(BF16)

    blk = pl.BlockSpec((tm, LANES), lambda i, j: (i, j))
    vec = pl.BlockSpec((1, LANES), lambda i, j: (0, j))
    return pl.pallas_call(
        body, name="rwkv_post_fwd", grid=(t // tm, c.rw // LANES),
        in_specs=[blk] * 4 + [pl.BlockSpec((tm, LANES), lambda i, j: (i, goff + j)), vec, vec, vec],
        out_specs=blk, out_shape=jax.ShapeDtypeStruct((t, c.rw), BF16),
        compiler_params=_params("parallel", "parallel"),
    )(ys, r, km, v, p, ln_w, ln_b, r_k)


def _rwkv_post_bwd(dyc, ys, r, km, v, p, c, ln_w, ln_b, r_k):
    t = ys.shape[0]
    tm = _tile(t, 512, 8)
    goff = c.o_grw // LANES

    def body(dy_ref, ys_ref, r_ref, km_ref, v_ref, g_ref, lw_ref, lb_ref, rk_ref,
             dys_ref, dr_ref, dkm_ref, dv_ref, dg_ref, dlw_ref, dlb_ref, drk_ref):
        @pl.when(pl.program_id(1) == 0)
        def _():
            for ref in (dlw_ref, dlb_ref, drk_ref):
                ref[...] = jnp.zeros_like(ref)

        yn, rstd = _group_norm(ys_ref[...])
        rv, kmv, vv, rk, g = r_ref[...], km_ref[...], v_ref[...], rk_ref[...], g_ref[...]
        s = _head_sum(rv * kmv * rk)
        y = yn * lw_ref[...] + lb_ref[...] + s * vv
        dyc = dy_ref[...]
        dg_ref[...] = (dyc * y * _silu_grad(g)).astype(BF16)
        dy = dyc * g * _sig(g)
        dlb_ref[...] += jnp.sum(dy, axis=0, keepdims=True)
        dlw_ref[...] += jnp.sum(dy * yn, axis=0, keepdims=True)
        dyn = dy * lw_ref[...]
        inv = 1.0 / HEAD_DIM
        dys_ref[...] = rstd * (dyn - _head_sum(dyn) * inv - yn * _head_sum(dyn * yn) * inv)
        ds = _head_sum(dy * vv)
        dv_ref[...] = dy * s
        dr_ref[...] = ds * kmv * rk
        dkm_ref[...] = ds * rv * rk
        drk_ref[...] += jnp.sum(ds * rv * kmv, axis=0, keepdims=True)

    blk = pl.BlockSpec((tm, LANES), lambda j, i: (i, j))
    vec = pl.BlockSpec((1, LANES), lambda j, i: (0, j))
    f = jax.ShapeDtypeStruct((t, c.rw), F32)
    s1 = jax.ShapeDtypeStruct((1, c.rw), F32)
    return pl.pallas_call(
        body, name="rwkv_post_bwd", grid=(c.rw // LANES, t // tm),
        in_specs=[blk] * 5 + [pl.BlockSpec((tm, LANES), lambda j, i: (i, goff + j)), vec, vec, vec],
        out_specs=[blk] * 5 + [vec] * 3,
        out_shape=[f, f, f, f, jax.ShapeDtypeStruct((t, c.rw), BF16), s1, s1, s1],
        compiler_params=_params("parallel", "arbitrary"),
    )(dyc, ys, r, km, v, p, ln_w, ln_b, r_k)


def _gate_fwd(y, p, goff, name):
    t, w = y.shape
    tm = _tile(t, 512, 8)
    gb = goff // LANES

    def body(y_ref, g_ref, o_ref):
        g = g_ref[...]
        o_ref[...] = (y_ref[...] * g * _sig(g)).astype(BF16)

    blk = pl.BlockSpec((tm, LANES), lambda i, j: (i, j))
    return pl.pallas_call(
        body, name=name, grid=(t // tm, w // LANES),
        in_specs=[blk, pl.BlockSpec((tm, LANES), lambda i, j: (i, gb + j))],
        out_specs=blk, out_shape=jax.ShapeDtypeStruct((t, w), BF16),
        compiler_params=_params("parallel", "parallel"),
    )(y, p)


def _gate_bwd(dyc, yoff, y, p, goff, name):
    t, w = y.shape
    tm = _tile(t, 512, 8)
    gb, yb = goff // LANES, yoff // LANES

    def body(d_ref, y_ref, g_ref, dy_ref, dg_ref):
        g, d = g_ref[...], d_ref[...]
        dy_ref[...] = d * g * _sig(g)
        dg_ref[...] = (d * y_ref[...] * _silu_grad(g)).astype(BF16)

    blk = pl.BlockSpec((tm, LANES), lambda i, j: (i, j))
    return pl.pallas_call(
        body, name=name, grid=(t // tm, w // LANES),
        in_specs=[pl.BlockSpec((tm, LANES), lambda i, j: (i, yb + j)), blk,
                  pl.BlockSpec((tm, LANES), lambda i, j: (i, gb + j))],
        out_specs=[blk, blk],
        out_shape=[jax.ShapeDtypeStruct((t, w), F32), jax.ShapeDtypeStruct((t, w), BF16)],
        compiler_params=_params("parallel", "parallel"),
    )(dyc, y, p)


NEG = -1e30


def _fox_prep(p, c, b_f):
    t = p.shape[0]
    tm = _tile(t, 512, 8)
    fb = c.o_fl // LANES

    def body(f_ref, b_ref, o_ref, carry):
        @pl.when(pl.program_id(0) == 0)
        def _():
            carry[...] = jnp.zeros_like(carry)

        logf = -_softplus(-(f_ref[...] + b_ref[...]))
        cum = _mm(_tri(tm, False).astype(F32), logf, HI) + carry[...]
        o_ref[...] = cum
        carry[...] += jnp.sum(logf, axis=0, keepdims=True)

    return pl.pallas_call(
        body, name="fox_prep", grid=(t // tm,),
        in_specs=[pl.BlockSpec((tm, LANES), lambda i: (i, fb)), pl.BlockSpec((1, LANES), lambda i: (0, 0))],
        out_specs=pl.BlockSpec((tm, LANES), lambda i: (i, 0)),
        out_shape=jax.ShapeDtypeStruct((t, LANES), F32),
        scratch_shapes=[pltpu.VMEM((1, LANES), F32)],
        compiler_params=_params("arbitrary"),
    )(p, b_f)


def _fox_logit_bwd(dcum, p, c, b_f):
    t = p.shape[0]
    tm = _tile(t, 512, 8)
    fb = c.o_fl // LANES
    nt = t // tm

    def body(d_ref, f_ref, b_ref, o_ref, db_ref, carry):
        @pl.when(pl.program_id(0) == 0)
        def _():
            carry[...] = jnp.zeros_like(carry)
            db_ref[...] = jnp.zeros_like(db_ref)

        d = d_ref[0] + d_ref[1]
        dlogf = _mm(_tri(tm, False).astype(F32).T, d, HI) + carry[...]
        carry[...] += jnp.sum(d, axis=0, keepdims=True)
        df = dlogf * _sig(-(f_ref[...] + b_ref[...]))
        o_ref[...] = df.astype(BF16)
        db_ref[...] += jnp.sum(df, axis=0, keepdims=True)

    return pl.pallas_call(
        body, name="fox_logit_bwd", grid=(nt,),
        in_specs=[pl.BlockSpec((2, tm, LANES), lambda i: (0, nt - 1 - i, 0)),
                  pl.BlockSpec((tm, LANES), lambda i: (nt - 1 - i, fb)),
                  pl.BlockSpec((1, LANES), lambda i: (0, 0))],
        out_specs=[pl.BlockSpec((tm, LANES), lambda i: (nt - 1 - i, 0)), pl.BlockSpec((1, LANES), lambda i: (0, 0))],
        out_shape=[jax.ShapeDtypeStruct((t, LANES), BF16), jax.ShapeDtypeStruct((1, LANES), F32)],
        scratch_shapes=[pltpu.VMEM((1, LANES), F32)],
        compiler_params=_params("arbitrary"),
    )(dcum, p, b_f)


def _fox_scores(q, k, cq, ck, qi, ki, tq, tk):
    s = _mm_nt((q * (HEAD_DIM ** -0.5)).astype(BF16), k.astype(BF16)) + cq - ck
    qpos = qi * tq + lax.broadcasted_iota(jnp.int32, (tq, tk), 0)
    kpos = ki * tk + lax.broadcasted_iota(jnp.int32, (tq, tk), 1)
    mask = kpos <= qpos
    return jnp.where(mask, s, NEG), mask


def _fox_fwd(q, k, v, cq, ck, hb, tb):
    h, t, n = q.shape
    tq = tk = _tile(t, tb, LANES)
    nq = t // tq

    def body(q_ref, k_ref, v_ref, cq_ref, ck_ref, o_ref, lse_ref, m_sc, l_sc, acc_sc):
        qi, ki = pl.program_id(1), pl.program_id(2)

        @pl.when(ki == 0)
        def _():
            m_sc[...] = jnp.full_like(m_sc, NEG)
            l_sc[...] = jnp.zeros_like(l_sc)
            acc_sc[...] = jnp.zeros_like(acc_sc)

        @pl.when(ki <= qi)
        def _():
            for i in range(hb):
                s, _ = _fox_scores(q_ref[i], k_ref[i], cq_ref[i], ck_ref[i], qi, ki, tq, tk)
                m_old = m_sc[i]
                m_new = jnp.maximum(m_old, jnp.max(s, axis=1, keepdims=True))
                scale = jnp.exp(m_old - m_new)
                e = jnp.exp(s - m_new)
                l_sc[i] = scale * l_sc[i] + jnp.sum(e, axis=1, keepdims=True)
                acc_sc[i] = scale * acc_sc[i] + _mm(e.astype(BF16), v_ref[i].astype(BF16))
                m_sc[i] = m_new

        @pl.when(ki == qi)
        def _():
            o_ref[...] = acc_sc[...] / l_sc[...]
            lse_ref[...] = m_sc[...] + jnp.log(l_sc[...])

    qb = pl.BlockSpec((hb, tq, n), lambda g, i, j: (g, i, 0))
    kb = pl.BlockSpec((hb, tk, n), lambda g, i, j: (g, jnp.minimum(i, j), 0))
    col = pl.BlockSpec((hb, tq, 1), lambda g, i, j: (g, i, 0))
    return pl.pallas_call(
        body, name="fox_fwd", grid=(h // hb, nq, nq),
        in_specs=[qb, kb, kb, col, pl.BlockSpec((hb, 1, tk), lambda g, i, j: (g, 0, jnp.minimum(i, j)))],
        out_specs=[qb, col],
        out_shape=[jax.ShapeDtypeStruct((h, t, n), F32), jax.ShapeDtypeStruct((h, t, 1), F32)],
        scratch_shapes=[pltpu.VMEM((hb, tq, 1), F32), pltpu.VMEM((hb, tq, 1), F32), pltpu.VMEM((hb, tq, n), F32)],
        compiler_params=_params("parallel", "parallel", "arbitrary"),
    )(q, k, v, cq, ck)


def _fox_bwd_dq(q, k, v, cq, ck, lse, o, do, hb, tb):
    h, t, n = q.shape
    tq = tk = _tile(t, tb, LANES)
    nq = t // tq

    def body(q_ref, k_ref, v_ref, cq_ref, ck_ref, lse_ref, o_ref, do_ref, dq_ref, dcq_ref, acc_sc, row_sc):
        qi, ki = pl.program_id(1), pl.program_id(2)

        @pl.when(ki == 0)
        def _():
            acc_sc[...] = jnp.zeros_like(acc_sc)
            row_sc[...] = jnp.zeros_like(row_sc)

        @pl.when(ki <= qi)
        def _():
            for i in range(hb):
                s, mask = _fox_scores(q_ref[i], k_ref[i], cq_ref[i], ck_ref[i], qi, ki, tq, tk)
                dov = do_ref[i]
                delta = jnp.sum(dov * o_ref[i], axis=1, keepdims=True)
                pm = jnp.where(mask, jnp.exp(s - lse_ref[i]), 0.0)
                dp = _mm_nt(dov.astype(BF16), v_ref[i].astype(BF16))
                ds = pm * (dp - delta)
                acc_sc[i] += _mm(ds.astype(BF16), k_ref[i].astype(BF16))
                row_sc[i] += jnp.sum(ds, axis=1, keepdims=True)

        @pl.when(ki == qi)
        def _():
            dq_ref[...] = acc_sc[...] * (HEAD_DIM ** -0.5)
            dcq_ref[...] = row_sc[...]

    qb = pl.BlockSpec((hb, tq, n), lambda g, i, j: (g, i, 0))
    kb = pl.BlockSpec((hb, tk, n), lambda g, i, j: (g, jnp.minimum(i, j), 0))
    col = pl.BlockSpec((hb, tq, 1), lambda g, i, j: (g, i, 0))
    return pl.pallas_call(
        body, name="fox_bwd_dq", grid=(h // hb, nq, nq),
        in_specs=[qb, kb, kb, col, pl.BlockSpec((hb, 1, tk), lambda g, i, j: (g, 0, jnp.minimum(i, j))), col, qb, qb],
        out_specs=[qb, col],
        out_shape=[jax.ShapeDtypeStruct((h, t, n), F32), jax.ShapeDtypeStruct((h, t, 1), F32)],
        scratch_shapes=[pltpu.VMEM((hb, tq, n), F32), pltpu.VMEM((hb, tq, 1), F32)],
        compiler_params=_params("parallel", "parallel", "arbitrary"),
    )(q, k, v, cq, ck, lse, o, do)


def _fox_bwd_dkv(q, k, v, cq, ck, lse, o, do, hb, tb):
    h, t, n = q.shape
    tq = tk = _tile(t, tb, LANES)
    nq = t // tq

    def body(q_ref, k_ref, v_ref, cq_ref, ck_ref, lse_ref, o_ref, do_ref, dk_ref, dv_ref, dck_ref, dk_sc, dv_sc, dc_sc):
        ki, qi = pl.program_id(1), pl.program_id(2)

        @pl.when(qi == 0)
        def _():
            dk_sc[...] = jnp.zeros_like(dk_sc)
            dv_sc[...] = jnp.zeros_like(dv_sc)
            dc_sc[...] = jnp.zeros_like(dc_sc)

        @pl.when(qi >= ki)
        def _():
            for i in range(hb):
                s, mask = _fox_scores(q_ref[i], k_ref[i], cq_ref[i], ck_ref[i], qi, ki, tq, tk)
                dov = do_ref[i]
                delta = jnp.sum(dov * o_ref[i], axis=1, keepdims=True)
                pm = jnp.where(mask, jnp.exp(s - lse_ref[i]), 0.0)
                dob = dov.astype(BF16)
                dp = _mm_nt(dob, v_ref[i].astype(BF16))
                ds = pm * (dp - delta)
                dv_sc[i] += _mm_tn(pm.astype(BF16), dob)
                dk_sc[i] += _mm_tn(ds.astype(BF16), q_ref[i].astype(BF16))
                dc_sc[i] -= jnp.sum(ds, axis=0, keepdims=True)

        @pl.when(qi == nq - 1)
        def _():
            dk_ref[...] = dk_sc[...] * (HEAD_DIM ** -0.5)
            dv_ref[...] = dv_sc[...]
            dck_ref[...] = dc_sc[...]

    qb = pl.BlockSpec((hb, tq, n), lambda g, j, i: (g, jnp.maximum(i, j), 0))
    kb = pl.BlockSpec((hb, tk, n), lambda g, j, i: (g, j, 0))
    col = pl.BlockSpec((hb, tq, 1), lambda g, j, i: (g, jnp.maximum(i, j), 0))
    row = pl.BlockSpec((hb, 1, tk), lambda g, j, i: (g, 0, j))
    return pl.pallas_call(
        body, name="fox_bwd_dkv", grid=(h // hb, nq, nq),
        in_specs=[qb, kb, kb, col, row, col, qb, qb],
        out_specs=[kb, kb, row],
        out_shape=[jax.ShapeDtypeStruct((h, t, n), F32), jax.ShapeDtypeStruct((h, t, n), F32),
                   jax.ShapeDtypeStruct((h, 1, t), F32)],
        scratch_shapes=[pltpu.VMEM((hb, tk, n), F32), pltpu.VMEM((hb, tk, n), F32), pltpu.VMEM((hb, 1, tk), F32)],
        compiler_params=_params("parallel", "parallel", "arbitrary"),
    )(q, k, v, cq, ck, lse, o, do)


FOX_PAIRS = 2
FOX_HEADS_STEP = 2 * FOX_PAIRS


def _lane_half(shape, upper):
    li = lax.broadcasted_iota(jnp.int32, shape, len(shape) - 1)
    return (li >= HEAD_DIM) if upper else (li < HEAD_DIM)


def _col(block, j):
    li = lax.broadcasted_iota(jnp.int32, block.shape, 1)
    return jnp.sum(jnp.where(li == j, block, 0.0), axis=1, keepdims=True)


def _from_cols(cols):
    li = lax.broadcasted_iota(jnp.int32, (cols[0].shape[0], len(cols)), 1)
    out = jnp.zeros(li.shape, F32)
    for j, cj in enumerate(cols):
        out = jnp.where(li == j, cj, out)
    return out


def _from_rows(rows):
    si = lax.broadcasted_iota(jnp.int32, (len(rows), rows[0].shape[1]), 0)
    out = jnp.zeros(si.shape, F32)
    for j, rj in enumerate(rows):
        out = jnp.where(si == j, rj, out)
    return out


def _causal(tq, tk):
    return lax.broadcasted_iota(jnp.int32, (tq, tk), 1) <= lax.broadcasted_iota(jnp.int32, (tq, tk), 0)


def _fox_prep_t(p, c, b_f):
    t = p.shape[0]
    tm = _tile(t, 512, LANES)
    fb = c.o_fl // LANES

    def body(f_ref, b_ref, o_ref, carry):
        @pl.when(pl.program_id(0) == 0)
        def _():
            carry[...] = jnp.zeros_like(carry)

        logf = -_softplus(-(f_ref[...] + b_ref[...]))
        cum = _mm(_tri(tm, False).astype(F32), logf, HI) + carry[...]
        o_ref[...] = cum.T
        carry[...] += jnp.sum(logf, axis=0, keepdims=True)

    return pl.pallas_call(
        body, name="fox_prep", grid=(t // tm,),
        in_specs=[pl.BlockSpec((tm, LANES), lambda i: (i, fb)), pl.BlockSpec((1, LANES), lambda i: (0, 0))],
        out_specs=pl.BlockSpec((LANES, tm), lambda i: (0, i)),
        out_shape=jax.ShapeDtypeStruct((LANES, t), F32),
        scratch_shapes=[pltpu.VMEM((1, LANES), F32)],
        compiler_params=_params("arbitrary"),
    )(p, b_f)


def _fox2_fwd(p, c, cum_t, tb):
    t = p.shape[0]
    tq = tk = _tile(t, tb, LANES)
    nq = t // tq
    pw, nh = FOX_PAIRS * LANES, FOX_HEADS_STEP
    qb, kb, vb, gb = (o // pw for o in (c.o_fq, c.o_fk, c.o_fv, c.o_gfox))
    scale = HEAD_DIM ** -0.5

    def body(q_ref, k_ref, v_ref, g_ref, ck_ref, o_ref, y_ref, lse_ref, m_sc, l_sc, acc_sc):
        g, qi, ki = pl.program_id(0), pl.program_id(1), pl.program_id(2)

        @pl.when(ki == 0)
        def _():
            m_sc[...] = jnp.full_like(m_sc, NEG)
            l_sc[...] = jnp.zeros_like(l_sc)
            acc_sc[...] = jnp.zeros_like(acc_sc)

        def step(diag):
            for pi in range(FOX_PAIRS):
                lanes = slice(pi * LANES, (pi + 1) * LANES)
                q2 = (q_ref[:, lanes] * scale).astype(BF16)
                k2, v2 = k_ref[:, lanes].astype(BF16), v_ref[:, lanes].astype(BF16)
                acc = acc_sc[:, lanes]
                new_acc = acc
                for hh in range(2):
                    hi = 2 * pi + hh
                    mk = _lane_half((tq, LANES), hh == 1)
                    s = _mm_nt(jnp.where(mk, q2, jnp.zeros_like(q2)), k2) - ck_ref[pl.ds(g * nh + hi, 1), :]
                    if diag:
                        s = jnp.where(_causal(tq, tk), s, NEG)
                    m_old = m_sc[hi]
                    m_new = jnp.maximum(m_old, jnp.max(s, axis=1, keepdims=True))
                    a = jnp.exp(m_old - m_new)
                    e = jnp.exp(s - m_new)
                    l_sc[hi] = a * l_sc[hi] + jnp.sum(e, axis=1, keepdims=True)
                    m_sc[hi] = m_new
                    new_acc = jnp.where(mk, a * acc + _mm(e.astype(BF16), v2), new_acc)
                acc_sc[:, lanes] = new_acc

        @pl.when(ki < qi)
        def _():
            step(False)

        @pl.when(ki == qi)
        def _():
            step(True)
            for pi in range(FOX_PAIRS):
                lanes = slice(pi * LANES, (pi + 1) * LANES)
                inv = jnp.where(_lane_half((tq, LANES), False), 1.0 / l_sc[2 * pi], 1.0 / l_sc[2 * pi + 1])
                o = acc_sc[:, lanes] * inv
                gate = g_ref[:, lanes]
                o_ref[:, lanes] = o
                y_ref[:, lanes] = (o * gate * _sig(gate)).astype(BF16)
            lse_ref[0] = _from_cols([m_sc[h] + jnp.log(l_sc[h]) for h in range(nh)])

    row = lambda off: pl.BlockSpec((tq, pw), lambda g, i, j: (i, off + g))
    key = lambda off: pl.BlockSpec((tk, pw), lambda g, i, j: (jnp.minimum(i, j), off + g))
    out = pl.BlockSpec((tq, pw), lambda g, i, j: (i, g))
    return pl.pallas_call(
        body, name="fox_fwd", grid=(c.rw // pw, nq, nq),
        in_specs=[row(qb), key(kb), key(vb), row(gb),
                  pl.BlockSpec((LANES, tk), lambda g, i, j: (0, jnp.minimum(i, j)))],
        out_specs=[out, out, pl.BlockSpec((1, tq, nh), lambda g, i, j: (g, i, 0))],
        out_shape=[jax.ShapeDtypeStruct((t, c.rw), F32), jax.ShapeDtypeStruct((t, c.rw), BF16),
                   jax.ShapeDtypeStruct((c.rw // pw, t, nh), F32)],
        scratch_shapes=[pltpu.VMEM((nh, tq, 1), F32), pltpu.VMEM((nh, tq, 1), F32), pltpu.VMEM((tq, pw), F32)],
        compiler_params=_params("parallel", "parallel", "arbitrary"),
    )(p, p, p, p, cum_t)


def _fox2_grads(q2, k2, v2, do2, o2, lse_h, ck, mk, diag, tq, tk):
    zero = jnp.zeros_like(q2)
    s = _mm_nt(jnp.where(mk, q2, zero), k2) - ck
    if diag:
        s = jnp.where(_causal(tq, tk), s, NEG)
    pm = jnp.exp(s - lse_h)
    delta = jnp.sum(jnp.where(mk, do2 * o2, 0.0), axis=1, keepdims=True)
    dob = do2.astype(BF16)
    dp = _mm_nt(jnp.where(mk, dob, zero), v2)
    return pm, pm * (dp - delta), dob


def _fox2_bwd_dq(p, c, cum_t, lse, o, do, tb):
    t = p.shape[0]
    tq = tk = _tile(t, tb, LANES)
    nq = t // tq
    pw, nh = FOX_PAIRS * LANES, FOX_HEADS_STEP
    qb, kb, vb = (o_ // pw for o_ in (c.o_fq, c.o_fk, c.o_fv))
    scale = HEAD_DIM ** -0.5

    def body(q_ref, k_ref, v_ref, ck_ref, lse_ref, o_ref, do_ref, dq_ref, dcq_ref, acc_sc, row_sc):
        g, qi, ki = pl.program_id(0), pl.program_id(1), pl.program_id(2)

        @pl.when(ki == 0)
        def _():
            acc_sc[...] = jnp.zeros_like(acc_sc)
            row_sc[...] = jnp.zeros_like(row_sc)

        def step(diag):
            lse_blk = lse_ref[0]
            for pi in range(FOX_PAIRS):
                lanes = slice(pi * LANES, (pi + 1) * LANES)
                q2 = (q_ref[:, lanes] * scale).astype(BF16)
                k2, v2 = k_ref[:, lanes].astype(BF16), v_ref[:, lanes].astype(BF16)
                do2, o2 = do_ref[:, lanes], o_ref[:, lanes]
                acc = acc_sc[:, lanes]
                new_acc = acc
                for hh in range(2):
                    hi = 2 * pi + hh
                    mk = _lane_half((tq, LANES), hh == 1)
                    _, ds, _ = _fox2_grads(q2, k2, v2, do2, o2, _col(lse_blk, hi),
                                           ck_ref[pl.ds(g * nh + hi, 1), :], mk, diag, tq, tk)
                    row_sc[hi] += jnp.sum(ds, axis=1, keepdims=True)
                    new_acc = jnp.where(mk, acc + _mm(ds.astype(BF16), k2), new_acc)
                acc_sc[:, lanes] = new_acc

        @pl.when(ki < qi)
        def _():
            step(False)

        @pl.when(ki == qi)
        def _():
            step(True)
            dq_ref[...] = (acc_sc[...] * scale).astype(BF16)
            dcq_ref[0] = _from_cols([row_sc[h] for h in range(nh)])

    row = lambda off: pl.BlockSpec((tq, pw), lambda g, i, j: (i, off + g))
    key = lambda off: pl.BlockSpec((tk, pw), lambda g, i, j: (jnp.minimum(i, j), off + g))
    stat = pl.BlockSpec((1, tq, nh), lambda g, i, j: (g, i, 0))
    return pl.pallas_call(
        body, name="fox_bwd_dq", grid=(c.rw // pw, nq, nq),
        in_specs=[row(qb), key(kb), key(vb), pl.BlockSpec((LANES, tk), lambda g, i, j: (0, jnp.minimum(i, j))),
                  stat, row(0), row(0)],
        out_specs=[row(0), stat],
        out_shape=[jax.ShapeDtypeStruct((t, c.rw), BF16), jax.ShapeDtypeStruct((c.rw // pw, t, nh), F32)],
        scratch_shapes=[pltpu.VMEM((tq, pw), F32), pltpu.VMEM((nh, tq, 1), F32)],
        compiler_params=_params("parallel", "parallel", "arbitrary"),
    )(p, p, p, cum_t, lse, o, do)


def _fox2_bwd_dkv(p, c, cum_t, lse, o, do, tb):
    t = p.shape[0]
    tq = tk = _tile(t, tb, LANES)
    nq = t // tq
    pw, nh = FOX_PAIRS * LANES, FOX_HEADS_STEP
    qb, kb, vb = (o_ // pw for o_ in (c.o_fq, c.o_fk, c.o_fv))
    scale = HEAD_DIM ** -0.5

    def body(q_ref, k_ref, v_ref, ck_ref, lse_ref, o_ref, do_ref, dk_ref, dv_ref, dck_ref, dk_sc, dv_sc, dc_sc):
        g, ki, qi = pl.program_id(0), pl.program_id(1), pl.program_id(2)

        @pl.when(qi == 0)
        def _():
            dk_sc[...] = jnp.zeros_like(dk_sc)
            dv_sc[...] = jnp.zeros_like(dv_sc)
            dc_sc[...] = jnp.zeros_like(dc_sc)

        def step(diag):
            lse_blk = lse_ref[0]
            for pi in range(FOX_PAIRS):
                lanes = slice(pi * LANES, (pi + 1) * LANES)
                q2 = (q_ref[:, lanes] * scale).astype(BF16)
                k2, v2 = k_ref[:, lanes].astype(BF16), v_ref[:, lanes].astype(BF16)
                do2, o2 = do_ref[:, lanes], o_ref[:, lanes]
                dk, dv = dk_sc[:, lanes], dv_sc[:, lanes]
                new_dk, new_dv = dk, dv
                for hh in range(2):
                    hi = 2 * pi + hh
                    mk = _lane_half((tk, LANES), hh == 1)
                    pm, ds, dob = _fox2_grads(q2, k2, v2, do2, o2, _col(lse_blk, hi),
                                              ck_ref[pl.ds(g * nh + hi, 1), :], mk, diag, tq, tk)
                    dc_sc[hi] -= jnp.sum(ds, axis=0, keepdims=True)
                    new_dv = jnp.where(mk, dv + _mm_tn(pm.astype(BF16), dob), new_dv)
                    new_dk = jnp.where(mk, dk + _mm_tn(ds.astype(BF16), q2), new_dk)
                dk_sc[:, lanes] = new_dk
                dv_sc[:, lanes] = new_dv

        @pl.when(qi > ki)
        def _():
            step(False)

        @pl.when(qi == ki)
        def _():
            step(True)

        @pl.when(qi == nq - 1)
        def _():
            dk_ref[...] = dk_sc[...].astype(BF16)
            dv_ref[...] = dv_sc[...].astype(BF16)
            dck_ref[0] = _from_rows([dc_sc[h] for h in range(nh)])

    row = lambda off: pl.BlockSpec((tq, pw), lambda g, j, i: (jnp.maximum(i, j), off + g))
    key = lambda off: pl.BlockSpec((tk, pw), lambda g, j, i: (j, off + g))
    return pl.pallas_call(
        body, name="fox_bwd_dkv", grid=(c.rw // pw, nq, nq),
        in_specs=[row(qb), key(kb), key(vb), pl.BlockSpec((LANES, tk), lambda g, j, i: (0, j)),
                  pl.BlockSpec((1, tq, nh), lambda g, j, i: (g, jnp.maximum(i, j), 0)), row(0), row(0)],
        out_specs=[key(0), key(0), pl.BlockSpec((1, nh, tk), lambda g, j, i: (g, 0, j))],
        out_shape=[jax.ShapeDtypeStruct((t, c.rw), BF16), jax.ShapeDtypeStruct((t, c.rw), BF16),
                   jax.ShapeDtypeStruct((c.rw // pw, nh, t), F32)],
        scratch_shapes=[pltpu.VMEM((tk, pw), F32), pltpu.VMEM((tk, pw), F32), pltpu.VMEM((nh, 1, tk), F32)],
        compiler_params=_params("parallel", "parallel", "arbitrary"),
    )(p, p, p, cum_t, lse, o, do)


def _mem_probs(q, mk, scale):
    s = _mm_nt(q.astype(BF16), mk.astype(BF16)) * scale
    e = jnp.exp(s - jnp.max(s, axis=1, keepdims=True))
    return e / jnp.sum(e, axis=1, keepdims=True)


def _mem_attn_fwd(p, c, mkv):
    t = p.shape[0]
    tm = _tile(t, 512, 8)
    dh = c.mhd
    qb = c.o_mq // dh
    scale = dh ** -0.5

    def body(q_ref, mk_ref, mv_ref, o_ref):
        pm = _mem_probs(q_ref[...], mk_ref[...], scale)
        o_ref[...] = _mm(pm.astype(BF16), mv_ref[...].astype(BF16))

    m = mkv.shape[0]
    return pl.pallas_call(
        body, name="mem_attn_fwd", grid=(t // tm, MEM_HEADS),
        in_specs=[pl.BlockSpec((tm, dh), lambda i, j: (i, qb + j)),
                  pl.BlockSpec((m, dh), lambda i, j: (0, j)),
                  pl.BlockSpec((m, dh), lambda i, j: (0, MEM_HEADS + j))],
        out_specs=pl.BlockSpec((tm, dh), lambda i, j: (i, j)),
        out_shape=jax.ShapeDtypeStruct((t, c.mw), F32),
        compiler_params=_params("parallel", "parallel"),
    )(p, mkv, mkv)


def _mem_attn_bwd(p, c, mkv, do):
    t = p.shape[0]
    tm = _tile(t, 512, 8)
    dh = c.mhd
    qb = c.o_mq // dh
    scale = dh ** -0.5
    m = mkv.shape[0]

    def body(q_ref, mk_ref, mv_ref, do_ref, dq_ref, dmk_ref, dmv_ref):
        @pl.when(pl.program_id(1) == 0)
        def _():
            dmk_ref[...] = jnp.zeros_like(dmk_ref)
            dmv_ref[...] = jnp.zeros_like(dmv_ref)

        qv = q_ref[...].astype(BF16)
        pm = _mem_probs(qv, mk_ref[...], scale)
        dob = do_ref[...].astype(BF16)
        dmv_ref[...] += _mm_tn(pm.astype(BF16), dob)
        dp = _mm_nt(dob, mv_ref[...].astype(BF16))
        ds = (pm * (dp - jnp.sum(pm * dp, axis=1, keepdims=True)) * scale).astype(BF16)
        dq_ref[...] = _mm(ds, mk_ref[...].astype(BF16)).astype(BF16)
        dmk_ref[...] += _mm_tn(ds, qv)

    kvb = lambda off: pl.BlockSpec((m, dh), lambda j, i: (0, off + j))
    return pl.pallas_call(
        body, name="mem_attn_bwd", grid=(MEM_HEADS, t // tm),
        in_specs=[pl.BlockSpec((tm, dh), lambda j, i: (i, qb + j)), kvb(0), kvb(MEM_HEADS),
                  pl.BlockSpec((tm, dh), lambda j, i: (i, j))],
        out_specs=[pl.BlockSpec((tm, dh), lambda j, i: (i, j)), kvb(0), kvb(0)],
        out_shape=[jax.ShapeDtypeStruct((t, c.mw), BF16), jax.ShapeDtypeStruct((m, c.mw), F32),
                   jax.ShapeDtypeStruct((m, c.mw), F32)],
        compiler_params=_params("parallel", "arbitrary"),
    )(p, mkv, mkv, do)


def _adamw(w, g, m, v, name):
    rows, cols = w.shape
    tm = _tile(rows, max(8, (1 << 18) // cols // 8 * 8), 8)
    bc1 = 1.0 - ADAM_B1 ** ADAM_STEP
    bc2 = 1.0 - ADAM_B2 ** ADAM_STEP

    def body(w_ref, g_ref, m_ref, v_ref, go_ref, d_ref, mo_ref, vo_ref):
        gv = g_ref[:, 0:cols]
        mn = ADAM_B1 * m_ref[...] + (1.0 - ADAM_B1) * gv
        vn = ADAM_B2 * v_ref[...] + (1.0 - ADAM_B2) * (gv * gv)
        go_ref[...] = gv
        mo_ref[...] = mn
        vo_ref[...] = vn
        d_ref[...] = -ADAM_LR * ((mn / bc1) / (jnp.sqrt(vn / bc2) + ADAM_EPS) + ADAM_WD * w_ref[...])

    blk = pl.BlockSpec((tm, cols), lambda i: (i, 0))
    shp = jax.ShapeDtypeStruct((rows, cols), F32)
    return pl.pallas_call(
        body, name=name, grid=(rows // tm,),
        in_specs=[blk, pl.BlockSpec((tm, g.shape[1]), lambda i: (i, 0)), blk, blk],
        out_specs=[blk] * 4, out_shape=[shp] * 4,
        compiler_params=_params("parallel"),
    )(w, g, m, v)


SCAN_HEADS = 8
FOX_BLOCK = 512


def _local_step(c, x, mem, tgt, w):
    t = x.shape[0]
    rw = c.rw
    hd = lambda z: z.reshape(t, c.h, HEAD_DIM).transpose(1, 0, 2)
    uh = lambda z: z.transpose(1, 0, 2).reshape(t, rw)
    vecs = (w["mu"], w["w0"], w["a0"], w["k_k"], w["k_a"], w["wd"], w["wi"])

    h, rinv = _rms_fwd(x, w["g_pre"], "rms_pre")
    p = _matmul(h, w["wp"], name="in_proj", tk=4096)
    r, lw, km, v, a, b = _rwkv_pre_fwd(p, c, *vecs)
    scan_in = tuple(hd(z) for z in (r, lw, km, v, a, b))
    hb = max(n for n in range(1, SCAN_HEADS + 1) if c.h % n == 0)
    ysh, states = _rwkv_scan_fwd(*scan_in, hb)
    ys = uh(ysh)
    yc_r = _rwkv_post_fwd(ys, r, km, v, p, c, w["ln_w"], w["ln_b"], w["r_k"])

    cum_t = _fox_prep_t(p, c, w["b_f"])
    yfox, yc_f, lse = _fox2_fwd(p, c, cum_t, FOX_BLOCK)

    memn, rinv_m = _rms_fwd(mem, w["g_mem"], "rms_mem")
    mkv = _matmul(memn, w["w_mem_kv"], name="mem_kv")
    ymem = _mem_attn_fwd(p, c, mkv)
    yc_m = _gate_fwd(ymem, p, c.o_gmq, "gate_mem")

    ycat = jnp.concatenate([yc_r, yc_f, yc_m], axis=1)
    yo = _matmul(ycat, w["w_out"], name="out_proj", tn=512, tk=4096)
    loss, dout, dyo, dg_post = _post_loss(yo, x, tgt, w["g_post"], "post_loss")

    dyc = _matmul(dyo, w["w_out"], tb=True, name="d_ycat", tn=512, tk=4096)
    dw_out = _matmul(ycat, dyo, ta=True, name="d_w_out", tn=512, tk=4096, out_dtype=BF16)
    dys, dr2, dkm2, dv2, dg_r, dln_w, dln_b, dr_k = _rwkv_post_bwd(
        dyc, ys, r, km, v, p, c, w["ln_w"], w["ln_b"], w["r_k"])
    dyf, dg_f = _gate_bwd(dyc, rw, yfox, p, c.o_gfox, "gate_fox_bwd")
    dym, dg_m = _gate_bwd(dyc, 2 * rw, ymem, p, c.o_gmq, "gate_mem_bwd")

    scan_g = _rwkv_scan_bwd(*scan_in, states, hd(dys), hb)
    dps, dzw, dza, twb, alb, dw0, da0, dk_k, dk_a = _rwkv_pre_bwd(
        p, c, *vecs, *(uh(z) for z in scan_g), dr2, dkm2, dv2)
    dwd = _matmul(twb, dzw, ta=True, name="d_w_decay", out_dtype=BF16)
    dwi = _matmul(alb, dza, ta=True, name="d_w_iclr", out_dtype=BF16)
    dp_shift, dmu = _shift_bwd(dps, p, c, w["mu"])

    dfq, dcq = _fox2_bwd_dq(p, c, cum_t, lse, yfox, dyf, FOX_BLOCK)
    dfk, dfv, dck = _fox2_bwd_dkv(p, c, cum_t, lse, yfox, dyf, FOX_BLOCK)
    dcum = jnp.pad(jnp.stack([dcq.transpose(1, 0, 2).reshape(t, c.h), dck.reshape(c.h, t).T]),
                   ((0, 0), (0, 0), (0, LANES - c.h)))
    dfl, db_f = _fox_logit_bwd(dcum, p, c, w["b_f"])

    dmq, dmk, dmv = _mem_attn_bwd(p, c, mkv, dym)
    dmkv = jnp.concatenate([dmk, dmv], axis=1)
    dw_mkv = _matmul(memn, dmkv, ta=True, name="d_w_mem_kv", out_dtype=BF16)
    dmemn = _matmul(dmkv, w["w_mem_kv"], tb=True, name="d_memn")
    _, dg_mem = _rms_bwd(dmemn, mem, rinv_m, w["g_mem"], jnp.zeros_like(mem), "rms_mem_bwd")

    dp = jnp.concatenate([dp_shift, dg_r, dfq, dfk, dfv, dg_f, dmq, dg_m, dfl], axis=1)
    dh = _matmul(dp, w["wp"], tb=True, name="d_h", tk=2944)
    dwp = _matmul(h, dp, ta=True, name="d_w_in", tk=4096, out_dtype=BF16)
    grad_x, dg_pre = _rms_bwd(dh, x, rinv, w["g_pre"], dout, "rms_pre_bwd")

    small = dict(g_pre=dg_pre, mu=dmu, w0=dw0, a0=da0, k_k=dk_k, k_a=dk_a, r_k=dr_k, ln_w=dln_w, ln_b=dln_b,
                 b_f=db_f, g_mem=dg_mem, g_post=dg_post)
    return loss, grad_x, dict(wp=dwp, wd=dwd, wi=dwi, w_mem_kv=dw_mkv, w_out=dw_out), small


CHIPS = ((1, 0, 0), (0, 1, 0), (1, 1, 0))
SIBLING = ((0, 0, 1),)
ALL_PEERS = tuple((i, j, k) for i in (0, 1) for j in (0, 1) for k in (0, 1))[1:]


def _chip_of(pos):
    return 2 * pos[0] + pos[1]


DMA_CHUNK = 4 << 20


def _pieces(shape, itemsize):
    lead, (rows, cols) = shape[:-2], shape[-2:]
    k = 1
    if rows % 16 == 0:
        k = max(1, min(rows // 16, -(-rows * cols * itemsize // DMA_CHUNK)))
        while rows % k or (rows // k) % 16:
            k -= 1
    band = rows // k
    idxs = [()]
    for n in lead:
        idxs = [i + (j,) for i in idxs for j in range(n)]
    return [i + (pl.ds(j * band, band),) for i in idxs for j in range(k)]


def _peer_of(me, mask):
    return tuple(1 - v if f else v for v, f in zip(me, mask))


def _exchange(name, groups):
    n = len(groups)
    plan = []
    for gi, g in enumerate(groups):
        for ti in range(len(g["transfers"])):
            for idx in _pieces(tuple(g["piece"]), g["src"].dtype.itemsize):
                plan.append((gi, ti, idx))

    def body(*refs):
        srcs, outs = refs[:n], refs[n:2 * n]
        send_sems, recv_sems = refs[2 * n:]
        me = (lax.axis_index("x"), lax.axis_index("y"), lax.axis_index("c"))
        copies = []
        for k, (gi, ti, idx) in enumerate(plan):
            mask, view, slot = groups[gi]["transfers"][ti]
            peer = _peer_of(me, mask)
            copies.append(pltpu.make_async_remote_copy(
                src_ref=view(srcs[gi], me, peer).at[idx], dst_ref=outs[gi].at[slot(me, peer)].at[idx],
                send_sem=send_sems.at[k], recv_sem=recv_sems.at[k],
                device_id=peer, device_id_type=MESH))
        for cp in copies:
            cp.start()
        for cp in copies:
            cp.wait()

    any_spec = pl.BlockSpec(memory_space=pl.ANY)
    return pl.pallas_call(
        body, name=name,
        in_specs=[any_spec] * n, out_specs=[any_spec] * n,
        out_shape=[jax.ShapeDtypeStruct((g["slots"],) + tuple(g["piece"]), g["src"].dtype) for g in groups],
        scratch_shapes=[pltpu.SemaphoreType.DMA((len(plan),)), pltpu.SemaphoreType.DMA((len(plan),))],
    )(*[g["src"] for g in groups])


def _my_chip():
    return 2 * lax.axis_index("x") + lax.axis_index("y")


def _put(buf, block, slot):
    return lax.dynamic_update_slice(buf, block[None], (slot,) + (0,) * block.ndim)


def _sum_slots(recv, own, k, out_dtype, name):
    s, rows, cols = recv.shape
    budget = max(16, ((4 << 20) // ((s + 1) * cols * 4)) // 16 * 16)
    tr = _tile(rows, budget, 16)
    own_many = own.shape[0] > 1

    def body(k_ref, *refs):
        out_ref = refs[s + 1]
        mine = refs[s][0].astype(F32)
        acc = None
        for i in range(s):
            term = jnp.where(k_ref[0] == i, mine, refs[i][0].astype(F32))
            acc = term if acc is None else acc + term
        out_ref[...] = acc.astype(out_ref.dtype)

    def slot_spec(i):
        return pl.BlockSpec((1, tr, cols), lambda j, kr: (jnp.where(kr[0] == i, (i + 1) % s, i), j, 0))

    grid_spec = pltpu.PrefetchScalarGridSpec(
        num_scalar_prefetch=1, grid=(rows // tr,),
        in_specs=[slot_spec(i) for i in range(s)]
                 + [pl.BlockSpec((1, tr, cols), lambda j, kr: (kr[0] if own_many else 0, j, 0))],
        out_specs=pl.BlockSpec((tr, cols), lambda j, kr: (j, 0)))
    return pl.pallas_call(
        body, name=name, grid_spec=grid_spec,
        out_shape=jax.ShapeDtypeStruct((rows, cols), out_dtype),
        compiler_params=_params("parallel"),
    )(k, *([recv] * s), own)


def _all_gather(shards):
    core, chip = lax.axis_index("c"), _my_chip()
    other_chip = lambda m: (lambda me: _chip_of(_peer_of(me, m)))
    halves = [s.reshape(2, s.shape[0] // 2, s.shape[1]) for s in shards]
    first = _exchange("gather_chips", [
        dict(src=q, slots=4, piece=q.shape[1:],
             transfers=[(m, lambda ref, me, peer: ref.at[me[2]], lambda me, peer: _chip_of(me)) for m in CHIPS])
        for q in halves])
    second = _exchange("gather_pair", [
        dict(src=q, slots=4, piece=q.shape[1:],
             transfers=[(SIBLING[0], (lambda f: lambda ref, me, peer: ref.at[f(me)])(other_chip(m)),
                         (lambda f: lambda me, peer: f(me))(other_chip(m))) for m in CHIPS])
        for q in first])
    out = []
    for s, a, b in zip(shards, first, second):
        full = jnp.concatenate([jnp.where(core == 0, a, b), jnp.where(core == 0, b, a)], axis=1)
        out.append(_put(full, s, chip))
    return out


def _reduce_scatter(partials):
    core, chip = lax.axis_index("c"), _my_chip()
    core1, chip1 = core.reshape(1).astype(jnp.int32), chip.reshape(1).astype(jnp.int32)
    halves = [q.reshape(4, 2, q.shape[1] // 2, q.shape[2]).transpose(1, 0, 2, 3) for q in partials]
    pair = _exchange("reduce_pair", [
        dict(src=q, slots=2, piece=q.shape[1:],
             transfers=[(SIBLING[0], lambda ref, me, peer: ref.at[peer[2]], lambda me, peer: me[2])])
        for q in halves])
    flat = lambda e: e.reshape(2, -1, e.shape[-1])
    chip_sums = [_sum_slots(flat(e), flat(q), core1, BF16, "reduce_pair_sum").reshape(q.shape[1:])
                 for e, q in zip(pair, halves)]
    crossed = _exchange("reduce_chips", [
        dict(src=q, slots=4, piece=q.shape[1:],
             transfers=[(m, lambda ref, me, peer: ref.at[_chip_of(peer)], lambda me, peer: _chip_of(me)) for m in CHIPS])
        for q in chip_sums])
    sums = [_sum_slots(e, q, chip1, F32, "reduce_chips_sum") for e, q in zip(crossed, chip_sums)]
    swapped = _exchange("reduce_swap", [
        dict(src=q, slots=2, piece=q.shape, transfers=[(SIBLING[0], lambda ref, me, peer: ref, lambda me, peer: me[2])])
        for q in sums])
    return [_put(e, q, core).reshape(-1, e.shape[-1]) for e, q in zip(swapped, sums)]


def _all_reduce_small(vec):
    dev = 4 * lax.axis_index("x") + 2 * lax.axis_index("y") + lax.axis_index("c")
    got = _exchange("reduce_small", [
        dict(src=vec, slots=8, piece=vec.shape,
             transfers=[(m, lambda ref, me, peer: ref, lambda me, peer: 4 * me[0] + 2 * me[1] + me[2])
                        for m in ALL_PEERS])])[0]
    return _sum_slots(got, vec[None], dev.reshape(1).astype(jnp.int32), F32, "reduce_small_sum")


SMALL = ("g_pre", "mu", "w0", "a0", "k_k", "k_a", "r_k", "ln_w", "ln_b", "b_f", "g_mem", "g_post")


def _pad_cols(a, n):
    return jnp.pad(a, ((0, 0),) * (a.ndim - 1) + ((0, n - a.shape[-1]),))


def kernel(x, mem, g_pre, w_in, mu_rwkv, w0, w_decay_up, a0, w_iclr_up, k_k, k_a, r_k, ln_x_w, ln_x_b, b_f, g_mem, w_mem_kv, w_out, g_post, loss_target, m_g_pre, m_w_in, m_mu_rwkv, m_w0, m_w_decay_up, m_a0, m_w_iclr_up, m_k_k, m_k_a, m_r_k, m_ln_x_w, m_ln_x_b, m_b_f, m_g_mem, m_w_mem_kv, m_w_out, m_g_post, v_g_pre, v_w_in, v_mu_rwkv, v_w0, v_w_decay_up, v_a0, v_w_iclr_up, v_k_k, v_k_a, v_r_k, v_ln_x_w, v_ln_x_b, v_b_f, v_g_mem, v_w_mem_kv, v_w_out, v_g_post):
    d = x.shape[-1]
    c = Cfg(d)
    ws = w_in.shape[-1]
    wpad = -(-ws // LANES) * LANES
    nh = c.h

    g_in, g_out, g_mkv, g_wd, g_wi = _all_gather([
        _pad_cols(w_in[0].astype(BF16), wpad), w_out[0].astype(BF16), w_mem_kv[0].astype(BF16),
        w_decay_up[0].astype(BF16), w_iclr_up[0].astype(BF16)])
    w_full = jnp.concatenate([g_in[s, :, :ws] for s in range(4)], axis=1)
    fl = c.ref_fl
    wp = jnp.concatenate([w_full[:, :fl], w_full[:, fl + nh:], _pad_cols(w_full[:, fl:fl + nh], LANES)], axis=1)
    unshard = lambda g: g.transpose(1, 0, 2).reshape(g.shape[1], -1)
    weights = dict(wp=wp, w_out=g_out.reshape(-1, d), w_mem_kv=g_mkv.reshape(d, -1), wd=unshard(g_wd), wi=unshard(g_wi),
                   g_pre=g_pre, mu=mu_rwkv, w0=w0, a0=a0, k_k=k_k, k_a=k_a, r_k=r_k.reshape(1, -1),
                   ln_w=ln_x_w, ln_b=ln_x_b, b_f=_pad_cols(b_f, LANES), g_mem=g_mem, g_post=g_post)

    loss, grad_x, big, small = _local_step(c, x[0], mem[0], loss_target[0], weights)

    dwp = big["wp"]
    dw_full = jnp.concatenate([dwp[:, :fl], dwp[:, c.o_fl:c.o_fl + nh], dwp[:, fl:c.o_fl]], axis=1)
    by_chip = lambda g: jnp.stack(jnp.split(g, 4, axis=1))
    red = _reduce_scatter([
        jnp.stack([_pad_cols(dw_full[:, s * ws:(s + 1) * ws], wpad) for s in range(4)]).astype(BF16),
        big["w_out"].reshape(4, -1, d).astype(BF16), big["w_mem_kv"].reshape(4, d // 4, -1).astype(BF16),
        by_chip(big["wd"]).astype(BF16), by_chip(big["wi"]).astype(BF16)])
    big_w = (w_in[0], w_out[0], w_mem_kv[0], w_decay_up[0], w_iclr_up[0])
    big_m = (m_w_in[0], m_w_out[0], m_w_mem_kv[0], m_w_decay_up[0], m_w_iclr_up[0])
    big_v = (v_w_in[0], v_w_out[0], v_w_mem_kv[0], v_w_decay_up[0], v_w_iclr_up[0])
    big_names = ("w_in", "w_out", "w_mem_kv", "w_decay_up", "w_iclr_up")
    upd = {n: _adamw(w_, g_, m_, v_, "adamw_" + n) for n, w_, g_, m_, v_ in zip(big_names, big_w, red, big_m, big_v)}

    small_w = dict(g_pre=g_pre, mu=mu_rwkv, w0=w0, a0=a0, k_k=k_k, k_a=k_a, r_k=r_k.reshape(1, -1), ln_w=ln_x_w,
                   ln_b=ln_x_b, b_f=b_f, g_mem=g_mem, g_post=g_post)
    small_m = dict(g_pre=m_g_pre, mu=m_mu_rwkv, w0=m_w0, a0=m_a0, k_k=m_k_k, k_a=m_k_a, r_k=m_r_k.reshape(1, -1),
                   ln_w=m_ln_x_w, ln_b=m_ln_x_b, b_f=m_b_f, g_mem=m_g_mem, g_post=m_g_post)
    small_v = dict(g_pre=v_g_pre, mu=v_mu_rwkv, w0=v_w0, a0=v_a0, k_k=v_k_k, k_a=v_k_a, r_k=v_r_k.reshape(1, -1),
                   ln_w=v_ln_x_w, ln_b=v_ln_x_b, b_f=v_b_f, g_mem=v_g_mem, g_post=v_g_post)
    widths = [-(-small_w[n].shape[1] // LANES) * LANES for n in SMALL]
    pack = lambda t: jnp.concatenate([_pad_cols(t[n], wd_) for n, wd_ in zip(SMALL, widths)]
                                     + [jnp.zeros((1, LANES), F32)], axis=1)
    g_packed = jnp.concatenate([_pad_cols(small[n], wd_) for n, wd_ in zip(SMALL, widths)]
                               + [_pad_cols(loss, LANES)], axis=1)
    g_sum = _all_reduce_small(g_packed)
    s_upd = _adamw(pack(small_w), g_sum, pack(small_m), pack(small_v), "adamw_small")
    offs = [sum(widths[:i]) for i in range(len(SMALL))]

    def take(kind, n):
        i = SMALL.index(n)
        piece = s_upd[kind][:, offs[i]:offs[i] + small_w[n].shape[1]]
        return piece.reshape(r_k.shape) if n == "r_k" else piece

    total_loss = g_sum[0, sum(widths)]
    order = ("g_pre", "w_in", "mu", "w0", "w_decay_up", "a0", "w_iclr_up", "k_k", "k_a", "r_k", "ln_w", "ln_b", "b_f",
             "g_mem", "w_mem_kv", "w_out", "g_post")
    outs = [total_loss, grad_x[None]]
    for kind in range(4):
        for n in order:
            outs.append(upd[n][kind][None] if n in upd else take(kind, n))
    return tuple(outs)
```

```python
import functools

import jax
import jax.numpy as jnp
from jax import lax
from jax.experimental import pallas as pl
from jax.experimental.pallas import tpu as pltpu

F32 = jnp.float32
BF16 = jnp.bfloat16
HI = lax.Precision.HIGHEST
MESH = pl.DeviceIdType.MESH

HEAD_DIM = 64
MEM_HEADS = 4
LORA = 128
CHUNK = 64
RMS_EPS = 1e-6
GN_EPS = 64e-5
LANES = 128
VMEM_LIMIT = 56 * 1024 * 1024

ADAM_LR, ADAM_B1, ADAM_B2, ADAM_EPS, ADAM_WD, ADAM_STEP = 0.001, 0.9, 0.999, 1e-08, 0.01, 10


class Cfg:
    def __init__(self, d):
        self.d = d
        self.rw = 3 * d // 8
        self.mw = d // 4
        self.h = self.rw // HEAD_DIM
        self.mhd = self.mw // MEM_HEADS
        self.shift = 3 * self.rw + 2 * LORA
        self.in_width = self.shift + 5 * self.rw + self.h + 2 * self.mw
        o = self.shift
        self.o_grw = o; o += self.rw
        self.o_fq = o; o += self.rw
        self.o_fk = o; o += self.rw
        self.o_fv = o; o += self.rw
        self.o_gfox = o; o += self.rw
        self.o_mq = o; o += self.mw
        self.o_gmq = o; o += self.mw
        self.o_fl = o; o += LANES
        self.wp = o
        self.ref_fl = self.shift + 4 * self.rw


def _tile(n, pref, align=LANES):
    if n <= pref:
        return n
    t = (pref // align) * align
    while t >= align:
        if n % t == 0:
            return t
        t -= align
    return n


def _params(*sem):
    return pltpu.CompilerParams(dimension_semantics=sem, vmem_limit_bytes=VMEM_LIMIT)


def _sig(x):
    return 1.0 / (1.0 + jnp.exp(-x))


def _softplus(x):
    return jnp.maximum(x, 0.0) + jnp.log(1.0 + jnp.exp(-jnp.abs(x)))


def _dot(a, b, dims, prec=None):
    return lax.dot_general(a, b, (dims, ((), ())), precision=prec, preferred_element_type=F32)


def _mm(a, b, prec=None):
    return _dot(a, b, ((1,), (0,)), prec)


def _mm_nt(a, b, prec=None):
    return _dot(a, b, ((1,), (1,)), prec)


def _mm_tn(a, b, prec=None):
    return _dot(a, b, ((0,), (0,)), prec)


def _split(a):
    hi = a.astype(BF16)
    return hi, (a - hi.astype(F32)).astype(BF16)


def _dot3(a, b, dims):
    (ah, al), (bh, bl) = _split(a), _split(b)
    d = lambda x, y: lax.dot_general(x, y, dims, preferred_element_type=F32)
    return d(ah, bh) + (d(ah, bl) + d(al, bh))


def _bmm(a, b):
    return _dot3(a, b, (((2,), (1,)), ((0,), (0,))))


def _bmm_nt(a, b):
    return _dot3(a, b, (((2,), (2,)), ((0,), (0,))))


def _bmm_tn(a, b):
    return _dot3(a, b, (((1,), (1,)), ((0,), (0,))))


def _bmm_01(m01, x):
    x1 = x.astype(BF16)
    r1 = x - x1.astype(F32)
    x2 = r1.astype(BF16)
    x3 = (r1 - x2.astype(F32)).astype(BF16)
    d = lambda y: lax.dot_general(m01, y, (((2,), (1,)), ((0,), (0,))), preferred_element_type=F32)
    return d(x1) + (d(x2) + d(x3))


def _matmul(a, b, *, ta=False, tb=False, out_dtype=F32, name, tm=1024, tn=1024, tk=1024, attach=None):
    m, k = (a.shape[1], a.shape[0]) if ta else a.shape
    n = b.shape[0] if tb else b.shape[1]
    tm, tn, tk = _tile(m, tm), _tile(n, tn), _tile(k, tk)
    nk = k // tk
    grid = (m // tm, n // tn, nk)
    dims = ((0 if ta else 1,), (1 if tb else 0,))
    groups = attach or []
    ng = len(groups)
    plan = _plan(groups)

    def body(a_ref, b_ref, *rest):
        srcs, o_ref, outs, scratch = rest[:ng], rest[ng], rest[ng + 1:2 * ng + 1], rest[2 * ng + 1:]
        acc = scratch[0] if nk > 1 else None
        if ng:
            copies = _copies(groups, plan, srcs, outs, scratch[-2], scratch[-1])
            ids = [pl.program_id(ax) for ax in range(3)]

            @pl.when((ids[0] == 0) & (ids[1] == 0) & (ids[2] == 0))
            def _():
                for cp in copies:
                    cp.start()

        part = _dot(a_ref[...].astype(BF16), b_ref[...].astype(BF16), dims)
        if nk == 1:
            o_ref[...] = part.astype(o_ref.dtype)
        else:
            kk = pl.program_id(2)

            @pl.when(kk == 0)
            def _():
                acc[...] = part

            @pl.when(kk > 0)
            def _():
                acc[...] += part

            @pl.when(kk == nk - 1)
            def _():
                o_ref[...] = acc[...].astype(o_ref.dtype)

        if ng:
            @pl.when((ids[0] == grid[0] - 1) & (ids[1] == grid[1] - 1) & (ids[2] == grid[2] - 1))
            def _():
                for cp in copies:
                    cp.wait()

    a_spec = pl.BlockSpec((tk, tm), lambda i, j, kk: (kk, i)) if ta else pl.BlockSpec((tm, tk), lambda i, j, kk: (i, kk))
    b_spec = pl.BlockSpec((tn, tk), lambda i, j, kk: (j, kk)) if tb else pl.BlockSpec((tk, tn), lambda i, j, kk: (kk, j))
    any_spec = pl.BlockSpec(memory_space=pl.ANY)
    sems = [pltpu.SemaphoreType.DMA((len(plan),)), pltpu.SemaphoreType.DMA((len(plan),))] if ng else []
    res = pl.pallas_call(
        body, name=name, grid=grid,
        in_specs=[a_spec, b_spec] + [any_spec] * ng,
        out_specs=[pl.BlockSpec((tm, tn), lambda i, j, kk: (i, j))] + [any_spec] * ng,
        out_shape=[jax.ShapeDtypeStruct((m, n), out_dtype)] + _exchange_shapes(groups),
        scratch_shapes=([pltpu.VMEM((tm, tn), F32)] if nk > 1 else []) + sems,
        compiler_params=_params(*(("arbitrary",) * 3 if ng else ("parallel", "parallel", "arbitrary"))),
    )(a, b, *[g["src"] for g in groups])
    return (res[0], list(res[1:])) if ng else res[0]


def _rms_fwd(x, g, name):
    t, d = x.shape
    tm = _tile(t, 256, 8)

    def body(x_ref, g_ref, h_ref, r_ref):
        xv = x_ref[...]
        r = lax.rsqrt(jnp.mean(xv * xv, axis=-1, keepdims=True) + RMS_EPS)
        h_ref[...] = (xv * r * g_ref[...]).astype(BF16)
        r_ref[...] = r

    return pl.pallas_call(
        body, name=name, grid=(t // tm,),
        in_specs=[pl.BlockSpec((tm, d), lambda i: (i, 0)), pl.BlockSpec((1, d), lambda i: (0, 0))],
        out_specs=[pl.BlockSpec((tm, d), lambda i: (i, 0)), pl.BlockSpec((tm, 1), lambda i: (i, 0))],
        out_shape=[jax.ShapeDtypeStruct((t, d), BF16), jax.ShapeDtypeStruct((t, 1), F32)],
        compiler_params=_params("parallel"),
    )(x, g)


def _rms_bwd(dh, x, rinv, g, add, name):
    t, d = x.shape
    tm = _tile(t, 256, 8)

    def body(dh_ref, x_ref, r_ref, g_ref, add_ref, dx_ref, dg_ref):
        @pl.when(pl.program_id(0) == 0)
        def _():
            dg_ref[...] = jnp.zeros_like(dg_ref)

        r = r_ref[...]
        xn = x_ref[...] * r
        dhv = dh_ref[...]
        dg_ref[...] += jnp.sum(dhv * xn, axis=0, keepdims=True)
        dxn = dhv * g_ref[...]
        dx_ref[...] = add_ref[...] + r * (dxn - xn * jnp.mean(dxn * xn, axis=-1, keepdims=True))

    row = pl.BlockSpec((tm, d), lambda i: (i, 0))
    vec = pl.BlockSpec((1, d), lambda i: (0, 0))
    return pl.pallas_call(
        body, name=name, grid=(t // tm,),
        in_specs=[row, row, pl.BlockSpec((tm, 1), lambda i: (i, 0)), vec, row],
        out_specs=[row, vec],
        out_shape=[jax.ShapeDtypeStruct((t, d), F32), jax.ShapeDtypeStruct((1, d), F32)],
        compiler_params=_params("arbitrary"),
    )(dh, x, rinv, g, add)


def _post_loss(yo, x, tgt, g, name):
    t, d = x.shape
    tm = _tile(t, 256, 8)

    def body(yo_ref, x_ref, t_ref, g_ref, loss_ref, dout_ref, dyo_ref, dg_ref):
        @pl.when(pl.program_id(0) == 0)
        def _():
            dg_ref[...] = jnp.zeros_like(dg_ref)
            loss_ref[...] = jnp.zeros_like(loss_ref)

        yv = yo_ref[...]
        r = lax.rsqrt(jnp.mean(yv * yv, axis=-1, keepdims=True) + RMS_EPS)
        n = yv * r
        err = x_ref[...] + n * g_ref[...] - t_ref[...]
        loss_ref[...] += 0.5 * jnp.sum(jnp.mean(err * err, axis=-1, keepdims=True), axis=0, keepdims=True)
        dout = err * (1.0 / d)
        dout_ref[...] = dout
        dg_ref[...] += jnp.sum(dout * n, axis=0, keepdims=True)
        dn = dout * g_ref[...]
        dyo_ref[...] = (r * (dn - n * jnp.mean(dn * n, axis=-1, keepdims=True))).astype(BF16)

    row = pl.BlockSpec((tm, d), lambda i: (i, 0))
    vec = pl.BlockSpec((1, d), lambda i: (0, 0))
    return pl.pallas_call(
        body, name=name, grid=(t // tm,),
        in_specs=[row, row, row, vec],
        out_specs=[pl.BlockSpec((1, 1), lambda i: (0, 0)), row, row, vec],
        out_shape=[jax.ShapeDtypeStruct((1, 1), F32), jax.ShapeDtypeStruct((t, d), F32),
                   jax.ShapeDtypeStruct((t, d), BF16), jax.ShapeDtypeStruct((1, d), F32)],
        compiler_params=_params("arbitrary"),
    )(yo, x, tgt, g)


def _head_sum(x):
    ri = lax.broadcasted_iota(jnp.int32, (LANES, LANES), 0) // HEAD_DIM
    ci = lax.broadcasted_iota(jnp.int32, (LANES, LANES), 1) // HEAD_DIM
    e = (ri == ci).astype(BF16)
    x1 = x.astype(BF16)
    r1 = x - x1.astype(F32)
    x2 = r1.astype(BF16)
    x3 = (r1 - x2.astype(F32)).astype(BF16)
    parts = []
    for i in range(x.shape[1] // LANES):
        sl = slice(i * LANES, (i + 1) * LANES)
        parts.append(_mm(x1[:, sl], e) + (_mm(x2[:, sl], e) + _mm(x3[:, sl], e)))
    return parts[0] if len(parts) == 1 else jnp.concatenate(parts, axis=1)


def _shifted(p_cur, before, first, mu):
    rolled = pltpu.roll(p_cur, 1, 0)
    prev_row = jnp.where(first, 0.0, before)
    row0 = lax.broadcasted_iota(jnp.int32, p_cur.shape, 0) == 0
    prev = jnp.where(row0, prev_row, rolled)
    return p_cur + (prev - p_cur) * mu, prev


def _rwkv_features(ps, rw, w0, a0, k_k, k_a, wd, wi):
    r, k, v = ps[:, 0:rw], ps[:, rw:2 * rw], ps[:, 2 * rw:3 * rw]
    wl, al = ps[:, 3 * rw:3 * rw + LORA], ps[:, 3 * rw + LORA:3 * rw + 2 * LORA]
    tw = jnp.tanh(wl)
    zw = w0 + _mm(tw.astype(BF16), wd)
    logw = -jnp.exp(-_softplus(-zw) - 0.5)
    alpha = _sig(a0 + _mm(al.astype(BF16), wi))
    kkr = k * k_k
    n2 = _head_sum(kkr * kkr)
    rn = lax.rsqrt(jnp.maximum(n2, 1e-24))
    kk = kkr * rn
    kmod = k * (1.0 + (alpha - 1.0) * k_a)
    return dict(r=r, k=k, v=v, tw=tw, al=al, zw=zw, logw=logw, alpha=alpha, kk=kk, rn=rn, n2=n2, kmod=kmod)


def _rwkv_pre_fwd(p, c, mu, w0, a0, k_k, k_a, wd, wi):
    t = p.shape[0]
    tm = _tile(t, 128, 8)
    rw, sh = c.rw, c.shift

    def body(p_ref, pp_ref, mu_ref, w0_ref, a0_ref, kk_ref, ka_ref, wd_ref, wi_ref,
             r_ref, lw_ref, km_ref, v_ref, a_ref, b_ref):
        ps, _ = _shifted(p_ref[...], pp_ref[7:8, :], pl.program_id(0) == 0, mu_ref[...])
        f = _rwkv_features(ps, rw, w0_ref[...], a0_ref[...], kk_ref[...], ka_ref[...], wd_ref[...], wi_ref[...])
        r_ref[...] = f["r"]
        lw_ref[...] = f["logw"]
        km_ref[...] = f["kmod"]
        v_ref[...] = f["v"]
        a_ref[...] = -f["kk"]
        b_ref[...] = f["kk"] * f["alpha"]

    vec = lambda n: pl.BlockSpec((1, n), lambda i: (0, 0))
    out = pl.BlockSpec((tm, rw), lambda i: (i, 0))
    return pl.pallas_call(
        body, name="rwkv_pre_fwd", grid=(t // tm,),
        in_specs=[pl.BlockSpec((tm, sh), lambda i: (i, 0)),
                  pl.BlockSpec((8, sh), lambda i: (jnp.maximum(i * (tm // 8) - 1, 0), 0)),
                  vec(sh), vec(rw), vec(rw), vec(rw), vec(rw),
                  pl.BlockSpec((LORA, rw), lambda i: (0, 0)), pl.BlockSpec((LORA, rw), lambda i: (0, 0))],
        out_specs=[out] * 6,
        out_shape=[jax.ShapeDtypeStruct((t, rw), F32)] * 6,
        compiler_params=_params("parallel"),
    )(p, p, mu, w0, a0, k_k, k_a, wd, wi)


def _rwkv_pre_bwd(p, c, mu, w0, a0, k_k, k_a, wd, wi, dr, dlw, dkm, dv, da, db, dr2, dkm2, dv2):
    t = p.shape[0]
    tm = _tile(t, 128, 8)
    rw, sh = c.rw, c.shift

    def body(p_ref, pp_ref, mu_ref, w0_ref, a0_ref, kk_ref, ka_ref, wd_ref, wi_ref,
             dr_ref, dlw_ref, dkm_ref, dv_ref, da_ref, db_ref, dr2_ref, dkm2_ref, dv2_ref,
             dps_ref, dzw_ref, dza_ref, tw_ref, al_ref, dw0_ref, da0_ref, dkk_ref, dka_ref):
        @pl.when(pl.program_id(0) == 0)
        def _():
            for ref in (dw0_ref, da0_ref, dkk_ref, dka_ref):
                ref[...] = jnp.zeros_like(ref)

        ps, _ = _shifted(p_ref[...], pp_ref[7:8, :], pl.program_id(0) == 0, mu_ref[...])
        k_k, k_a = kk_ref[...], ka_ref[...]
        f = _rwkv_features(ps, rw, w0_ref[...], a0_ref[...], k_k, k_a, wd_ref[...], wi_ref[...])
        alpha, kk, k = f["alpha"], f["kk"], f["k"]
        dkm = dkm_ref[...] + dkm2_ref[...]
        db = db_ref[...]
        dkk = db * alpha - da_ref[...]
        dalpha = db * kk + dkm * k * k_a
        dk = dkm * (1.0 + (alpha - 1.0) * k_a)
        dka_ref[...] += jnp.sum(dkm * k * (alpha - 1.0), axis=0, keepdims=True)
        dkkr = f["rn"] * jnp.where(f["n2"] > 1e-24, dkk - kk * _head_sum(dkk * kk), dkk)
        dk = dk + dkkr * k_k
        dkk_ref[...] += jnp.sum(dkkr * k, axis=0, keepdims=True)
        dza = dalpha * alpha * (1.0 - alpha)
        da0_ref[...] += jnp.sum(dza, axis=0, keepdims=True)
        dzw = dlw_ref[...] * f["logw"] * _sig(-f["zw"])
        dw0_ref[...] += jnp.sum(dzw, axis=0, keepdims=True)
        dza_b, dzw_b = dza.astype(BF16), dzw.astype(BF16)
        dal = _mm_nt(dza_b, wi_ref[...])
        dwl = _mm_nt(dzw_b, wd_ref[...]) * (1.0 - f["tw"] * f["tw"])
        dps_ref[:, 0:rw] = dr_ref[...] + dr2_ref[...]
        dps_ref[:, rw:2 * rw] = dk
        dps_ref[:, 2 * rw:3 * rw] = dv_ref[...] + dv2_ref[...]
        dps_ref[:, 3 * rw:3 * rw + LORA] = dwl
        dps_ref[:, 3 * rw + LORA:sh] = dal
        dzw_ref[...] = dzw_b
        dza_ref[...] = dza_b
        tw_ref[...] = f["tw"].astype(BF16)
        al_ref[...] = f["al"].astype(BF16)

    vec = lambda n: pl.BlockSpec((1, n), lambda i: (0, 0))
    blk = lambda n: pl.BlockSpec((tm, n), lambda i: (i, 0))
    return pl.pallas_call(
        body, name="rwkv_pre_bwd", grid=(t // tm,),
        in_specs=[blk(sh), pl.BlockSpec((8, sh), lambda i: (jnp.maximum(i * (tm // 8) - 1, 0), 0)),
                  vec(sh), vec(rw), vec(rw), vec(rw), vec(rw),
                  pl.BlockSpec((LORA, rw), lambda i: (0, 0)), pl.BlockSpec((LORA, rw), lambda i: (0, 0))]
                 + [blk(rw)] * 9,
        out_specs=[blk(sh), blk(rw), blk(rw), blk(LORA), blk(LORA), vec(rw), vec(rw), vec(rw), vec(rw)],
        out_shape=[jax.ShapeDtypeStruct((t, sh), F32), jax.ShapeDtypeStruct((t, rw), BF16),
                   jax.ShapeDtypeStruct((t, rw), BF16), jax.ShapeDtypeStruct((t, LORA), BF16),
                   jax.ShapeDtypeStruct((t, LORA), BF16)] + [jax.ShapeDtypeStruct((1, rw), F32)] * 4,
        compiler_params=_params("arbitrary"),
    )(p, p, mu, w0, a0, k_k, k_a, wd, wi, dr, dlw, dkm, dv, da, db, dr2, dkm2, dv2)


def _shift_bwd(dps, p, c, mu):
    t = p.shape[0]
    tm = _tile(t, 256, 8)
    sh = c.shift
    nt = t // tm

    def body(d_ref, dn_ref, p_ref, pp_ref, mu_ref, dp_ref, dmu_ref):
        i = pl.program_id(0)

        @pl.when(i == 0)
        def _():
            dmu_ref[...] = jnp.zeros_like(dmu_ref)

        mu = mu_ref[...]
        d = d_ref[...]
        pc = p_ref[...]
        _, prev = _shifted(pc, pp_ref[7:8, :], i == 0, mu)
        dmu_ref[...] += jnp.sum(d * (prev - pc), axis=0, keepdims=True)
        nxt_row = jnp.where(i == nt - 1, 0.0, dn_ref[0:1, :])
        last = lax.broadcasted_iota(jnp.int32, d.shape, 0) == tm - 1
        nxt = jnp.where(last, nxt_row, pltpu.roll(d, tm - 1, 0))
        dp_ref[...] = (d * (1.0 - mu) + nxt * mu).astype(BF16)

    blk = pl.BlockSpec((tm, sh), lambda i: (i, 0))
    return pl.pallas_call(
        body, name="shift_bwd", grid=(nt,),
        in_specs=[blk, pl.BlockSpec((8, sh), lambda i: (jnp.minimum((i + 1) * (tm // 8), t // 8 - 1), 0)),
                  blk, pl.BlockSpec((8, sh), lambda i: (jnp.maximum(i * (tm // 8) - 1, 0), 0)),
                  pl.BlockSpec((1, sh), lambda i: (0, 0))],
        out_specs=[blk, pl.BlockSpec((1, sh), lambda i: (0, 0))],
        out_shape=[jax.ShapeDtypeStruct((t, sh), BF16), jax.ShapeDtypeStruct((1, sh), F32)],
        compiler_params=_params("arbitrary"),
    )(dps, dps, p, p, mu)


def _tri(n, strict):
    ri = lax.broadcasted_iota(jnp.int32, (n, n), 0)
    ci = lax.broadcasted_iota(jnp.int32, (n, n), 1)
    return (ri > ci) if strict else (ri >= ci)


def _unit_lower_inverse(a):
    n = a.shape[-1]
    ri = lax.broadcasted_iota(jnp.int32, (n, n), 0)
    ci = lax.broadcasted_iota(jnp.int32, (n, n), 1)
    eye = (ri == ci).astype(F32)
    blk = lambda s: (ri // s) == (ci // s)
    ad = jnp.where(blk(16), a, 0.0)
    p = eye + ad
    for _ in range(3):
        ad = _bmm(ad, ad)
        p = p + _bmm(p, ad)
    s = 16
    while s < n:
        off = jnp.where(blk(2 * s) & ~blk(s), a, 0.0)
        p = p + _bmm(_bmm(p, off), p)
        s *= 2
    return p


def _chunk_common(r, lw, k, a, b):
    n = r.shape[1]
    tri_incl = jnp.broadcast_to(_tri(n, False).astype(BF16), (r.shape[0], n, n))
    cum = _bmm_01(tri_incl, lw)
    e_pos, e_neg, e_exc = jnp.exp(cum), jnp.exp(-cum), jnp.exp(cum - lw)
    last = lax.broadcasted_iota(jnp.int32, (n, r.shape[2]), 0) == n - 1
    g_last = jnp.exp(jnp.sum(jnp.where(last, cum, 0.0), axis=1, keepdims=True))
    return g_last, r * e_pos, a * e_exc, b * e_neg, k * e_neg, e_pos, e_neg, e_exc


def _chunk_solve(rt, at, bt, kt, v, g0):
    strict, incl = _tri(rt.shape[1], True), _tri(rt.shape[1], False)
    a_ab = jnp.where(strict, _bmm_nt(at, bt), 0.0)
    a_ak = jnp.where(strict, _bmm_nt(at, kt), 0.0)
    a_rb = jnp.where(incl, _bmm_nt(rt, bt), 0.0)
    a_rk = jnp.where(incl, _bmm_nt(rt, kt), 0.0)
    tinv = _unit_lower_inverse(a_ab)
    u = _bmm(tinv, _bmm(at, g0) + _bmm(a_ak, v))
    return a_ab, a_ak, a_rb, a_rk, tinv, u


def _diag_col(row, n):
    ri = lax.broadcasted_iota(jnp.int32, (n, n), 0)
    ci = lax.broadcasted_iota(jnp.int32, (n, n), 1)
    return jnp.sum(jnp.where(ri == ci, row, 0.0), axis=2, keepdims=True)


def _diag_row(col, n):
    ri = lax.broadcasted_iota(jnp.int32, (n, n), 0)
    ci = lax.broadcasted_iota(jnp.int32, (n, n), 1)
    return jnp.sum(jnp.where(ri == ci, col, 0.0), axis=1, keepdims=True)


def _rwkv_scan_fwd(r, lw, k, v, a, b, hb):
    h, t, n = r.shape
    nc = t // CHUNK

    def body(r_ref, lw_ref, k_ref, v_ref, a_ref, b_ref, y_ref, st_ref, g_sc):
        @pl.when(pl.program_id(1) == 0)
        def _():
            g_sc[...] = jnp.zeros_like(g_sc)

        g0 = g_sc[...]
        st_ref[0] = g0
        vv = v_ref[...]
        g_last, rt, at, bt, kt, _, _, _ = _chunk_common(r_ref[...], lw_ref[...], k_ref[...], a_ref[...], b_ref[...])
        _, _, a_rb, a_rk, _, u = _chunk_solve(rt, at, bt, kt, vv, g0)
        y_ref[...] = _bmm(rt, g0) + _bmm(a_rb, u) + _bmm(a_rk, vv)
        z = g0 + _bmm_tn(bt, u) + _bmm_tn(kt, vv)
        g_sc[...] = _diag_col(g_last, n) * z

    blk = pl.BlockSpec((hb, CHUNK, n), lambda i, j: (i, j, 0))
    return pl.pallas_call(
        body, name="rwkv_scan_fwd", grid=(h // hb, nc),
        in_specs=[blk] * 6,
        out_specs=[blk, pl.BlockSpec((1, hb, n, n), lambda i, j: (j, i, 0, 0))],
        out_shape=[jax.ShapeDtypeStruct((h, t, n), F32), jax.ShapeDtypeStruct((nc, h, n, n), F32)],
        scratch_shapes=[pltpu.VMEM((hb, n, n), F32)],
        compiler_params=_params("parallel", "arbitrary"),
    )(r, lw, k, v, a, b)


def _rwkv_scan_bwd(r, lw, k, v, a, b, states, dy, hb):
    h, t, n = r.shape
    nc = t // CHUNK

    def body(r_ref, lw_ref, k_ref, v_ref, a_ref, b_ref, st_ref, dy_ref,
             dr_ref, dlw_ref, dk_ref, dv_ref, da_ref, db_ref, dg_sc):
        @pl.when(pl.program_id(1) == 0)
        def _():
            dg_sc[...] = jnp.zeros_like(dg_sc)

        g0 = st_ref[0]
        vv, dyv, dh = v_ref[...], dy_ref[...], dg_sc[...]
        lwv = lw_ref[...]
        g_last, rt, at, bt, kt, e_pos, e_neg, e_exc = _chunk_common(r_ref[...], lwv, k_ref[...], a_ref[...], b_ref[...])
        a_ab, a_ak, a_rb, a_rk, tinv, u = _chunk_solve(rt, at, bt, kt, vv, g0)
        strict, incl = _tri(CHUNK, True), _tri(CHUNK, False)
        gcol = _diag_col(g_last, n)
        z = g0 + _bmm_tn(bt, u) + _bmm_tn(kt, vv)
        dz = gcol * dh
        dc_last = _diag_row(jnp.sum(dh * gcol * z, axis=2, keepdims=True), n)
        du = _bmm_tn(a_rb, dyv) + _bmm(bt, dz)
        dx = _bmm_tn(tinv, du)
        dv_ref[...] = _bmm_tn(a_rk, dyv) + _bmm(kt, dz) + _bmm_tn(a_ak, dx)
        da_ab = jnp.where(strict, _bmm_nt(dx, u), 0.0)
        da_ak = jnp.where(strict, _bmm_nt(dx, vv), 0.0)
        da_rb = jnp.where(incl, _bmm_nt(dyv, u), 0.0)
        da_rk = jnp.where(incl, _bmm_nt(dyv, vv), 0.0)
        d_at = _bmm(da_ab, bt) + _bmm(da_ak, kt) + _bmm_nt(dx, g0)
        d_rt = _bmm(da_rb, bt) + _bmm(da_rk, kt) + _bmm_nt(dyv, g0)
        d_bt = _bmm_tn(da_ab, at) + _bmm_tn(da_rb, rt) + _bmm_nt(u, dz)
        d_kt = _bmm_tn(da_ak, at) + _bmm_tn(da_rk, rt) + _bmm_nt(vv, dz)
        dg_sc[...] = dz + _bmm_tn(rt, dyv) + _bmm_tn(at, dx)
        dr_ref[...] = d_rt * e_pos
        da_ref[...] = d_at * e_exc
        db_ref[...] = d_bt * e_neg
        dk_ref[...] = d_kt * e_neg
        last = lax.broadcasted_iota(jnp.int32, (CHUNK, n), 0) == CHUNK - 1
        dc = d_rt * rt - d_bt * bt - d_kt * kt + jnp.where(last, dc_last, 0.0)
        dce = d_at * at
        ri = lax.broadcasted_iota(jnp.int32, (CHUNK, CHUNK), 0)
        ci = lax.broadcasted_iota(jnp.int32, (CHUNK, CHUNK), 1)
        up_incl = jnp.broadcast_to((ri <= ci).astype(BF16), (hb, CHUNK, CHUNK))
        dlw_ref[...] = _bmm_01(up_incl, dc + dce) - dce

    rev = lambda i, j: (i, nc - 1 - j, 0)
    blk = pl.BlockSpec((hb, CHUNK, n), rev)
    return pl.pallas_call(
        body, name="rwkv_scan_bwd", grid=(h // hb, nc),
        in_specs=[blk] * 6 + [pl.BlockSpec((1, hb, n, n), lambda i, j: (nc - 1 - j, i, 0, 0)), blk],
        out_specs=[blk] * 6,
        out_shape=[jax.ShapeDtypeStruct((h, t, n), F32)] * 6,
        scratch_shapes=[pltpu.VMEM((hb, n, n), F32)],
        compiler_params=_params("parallel", "arbitrary"),
    )(r, lw, k, v, a, b, states, dy)


def _silu_grad(g):
    s = _sig(g)
    return s * (1.0 + g * (1.0 - s))


def _group_norm(ys):
    yc = ys - _head_sum(ys) * (1.0 / HEAD_DIM)
    rstd = lax.rsqrt(_head_sum(yc * yc) * (1.0 / HEAD_DIM) + GN_EPS)
    return yc * rstd, rstd


def _rwkv_post_fwd(ys, r, km, v, p, c, ln_w, ln_b, r_k):
    t = ys.shape[0]
    tm = _tile(t, 512, 8)
    goff = c.o_grw // LANES

    def body(ys_ref, r_ref, km_ref, v_ref, g_ref, lw_ref, lb_ref, rk_ref, o_ref):
        yn, _ = _group_norm(ys_ref[...])
        s = _head_sum(r_ref[...] * km_ref[...] * rk_ref[...])
        g = g_ref[...]
        o_ref[...] = ((yn * lw_ref[...] + lb_ref[...] + s * v_ref[...]) * g * _sig(g)).astype(BF16)

    blk = pl.BlockSpec((tm, LANES), lambda i, j: (i, j))
    vec = pl.BlockSpec((1, LANES), lambda i, j: (0, j))
    return pl.pallas_call(
        body, name="rwkv_post_fwd", grid=(t // tm, c.rw // LANES),
        in_specs=[blk] * 4 + [pl.BlockSpec((tm, LANES), lambda i, j: (i, goff + j)), vec, vec, vec],
        out_specs=blk, out_shape=jax.ShapeDtypeStruct((t, c.rw), BF16),
        compiler_params=_params("parallel", "parallel"),
    )(ys, r, km, v, p, ln_w, ln_b, r_k)


def _rwkv_post_bwd(dyc, ys, r, km, v, p, c, ln_w, ln_b, r_k):
    t = ys.shape[0]
    tm = _tile(t, 512, 8)
    goff = c.o_grw // LANES

    def body(dy_ref, ys_ref, r_ref, km_ref, v_ref, g_ref, lw_ref, lb_ref, rk_ref,
             dys_ref, dr_ref, dkm_ref, dv_ref, dg_ref, dlw_ref, dlb_ref, drk_ref):
        @pl.when(pl.program_id(1) == 0)
        def _():
            for ref in (dlw_ref, dlb_ref, drk_ref):
                ref[...] = jnp.zeros_like(ref)

        yn, rstd = _group_norm(ys_ref[...])
        rv, kmv, vv, rk, g = r_ref[...], km_ref[...], v_ref[...], rk_ref[...], g_ref[...]
        s = _head_sum(rv * kmv * rk)
        y = yn * lw_ref[...] + lb_ref[...] + s * vv
        dyc = dy_ref[...]
        dg_ref[...] = (dyc * y * _silu_grad(g)).astype(BF16)
        dy = dyc * g * _sig(g)
        dlb_ref[...] += jnp.sum(dy, axis=0, keepdims=True)
        dlw_ref[...] += jnp.sum(dy * yn, axis=0, keepdims=True)
        dyn = dy * lw_ref[...]
        inv = 1.0 / HEAD_DIM
        dys_ref[...] = rstd * (dyn - _head_sum(dyn) * inv - yn * _head_sum(dyn * yn) * inv)
        ds = _head_sum(dy * vv)
        dv_ref[...] = dy * s
        dr_ref[...] = ds * kmv * rk
        dkm_ref[...] = ds * rv * rk
        drk_ref[...] += jnp.sum(ds * rv * kmv, axis=0, keepdims=True)

    blk = pl.BlockSpec((tm, LANES), lambda j, i: (i, j))
    vec = pl.BlockSpec((1, LANES), lambda j, i: (0, j))
    f = jax.ShapeDtypeStruct((t, c.rw), F32)
    s1 = jax.ShapeDtypeStruct((1, c.rw), F32)
    return pl.pallas_call(
        body, name="rwkv_post_bwd", grid=(c.rw // LANES, t // tm),
        in_specs=[blk] * 5 + [pl.BlockSpec((tm, LANES), lambda j, i: (i, goff + j)), vec, vec, vec],
        out_specs=[blk] * 5 + [vec] * 3,
        out_shape=[f, f, f, f, jax.ShapeDtypeStruct((t, c.rw), BF16), s1, s1, s1],
        compiler_params=_params("parallel", "arbitrary"),
    )(dyc, ys, r, km, v, p, ln_w, ln_b, r_k)


def _gate_fwd(y, p, goff, name):
    t, w = y.shape
    tm = _tile(t, 512, 8)
    gb = goff // LANES

    def body(y_ref, g_ref, o_ref):
        g = g_ref[...]
        o_ref[...] = (y_ref[...] * g * _sig(g)).astype(BF16)

    blk = pl.BlockSpec((tm, LANES), lambda i, j: (i, j))
    return pl.pallas_call(
        body, name=name, grid=(t // tm, w // LANES),
        in_specs=[blk, pl.BlockSpec((tm, LANES), lambda i, j: (i, gb + j))],
        out_specs=blk, out_shape=jax.ShapeDtypeStruct((t, w), BF16),
        compiler_params=_params("parallel", "parallel"),
    )(y, p)


def _gate_bwd(dyc, yoff, y, p, goff, name):
    t, w = y.shape
    tm = _tile(t, 512, 8)
    gb, yb = goff // LANES, yoff // LANES

    def body(d_ref, y_ref, g_ref, dy_ref, dg_ref):
        g, d = g_ref[...], d_ref[...]
        dy_ref[...] = d * g * _sig(g)
        dg_ref[...] = (d * y_ref[...] * _silu_grad(g)).astype(BF16)

    blk = pl.BlockSpec((tm, LANES), lambda i, j: (i, j))
    return pl.pallas_call(
        body, name=name, grid=(t // tm, w // LANES),
        in_specs=[pl.BlockSpec((tm, LANES), lambda i, j: (i, yb + j)), blk,
                  pl.BlockSpec((tm, LANES), lambda i, j: (i, gb + j))],
        out_specs=[blk, blk],
        out_shape=[jax.ShapeDtypeStruct((t, w), F32), jax.ShapeDtypeStruct((t, w), BF16)],
        compiler_params=_params("parallel", "parallel"),
    )(dyc, y, p)


NEG = -1e30


def _fox_prep(p, c, b_f):
    t = p.shape[0]
    tm = _tile(t, 512, 8)
    fb = c.o_fl // LANES

    def body(f_ref, b_ref, o_ref, carry):
        @pl.when(pl.program_id(0) == 0)
        def _():
            carry[...] = jnp.zeros_like(carry)

        logf = -_softplus(-(f_ref[...] + b_ref[...]))
        cum = _mm(_tri(tm, False).astype(F32), logf, HI) + carry[...]
        o_ref[...] = cum
        carry[...] += jnp.sum(logf, axis=0, keepdims=True)

    return pl.pallas_call(
        body, name="fox_prep", grid=(t // tm,),
        in_specs=[pl.BlockSpec((tm, LANES), lambda i: (i, fb)), pl.BlockSpec((1, LANES), lambda i: (0, 0))],
        out_specs=pl.BlockSpec((tm, LANES), lambda i: (i, 0)),
        out_shape=jax.ShapeDtypeStruct((t, LANES), F32),
        scratch_shapes=[pltpu.VMEM((1, LANES), F32)],
        compiler_params=_params("arbitrary"),
    )(p, b_f)


def _fox_logit_bwd(dcum, p, c, b_f):
    t = p.shape[0]
    tm = _tile(t, 512, 8)
    fb = c.o_fl // LANES
    nt = t // tm

    def body(d_ref, f_ref, b_ref, o_ref, db_ref, carry):
        @pl.when(pl.program_id(0) == 0)
        def _():
            carry[...] = jnp.zeros_like(carry)
            db_ref[...] = jnp.zeros_like(db_ref)

        d = d_ref[0] + d_ref[1]
        dlogf = _mm(_tri(tm, False).astype(F32).T, d, HI) + carry[...]
        carry[...] += jnp.sum(d, axis=0, keepdims=True)
        df = dlogf * _sig(-(f_ref[...] + b_ref[...]))
        o_ref[...] = df.astype(BF16)
        db_ref[...] += jnp.sum(df, axis=0, keepdims=True)

    return pl.pallas_call(
        body, name="fox_logit_bwd", grid=(nt,),
        in_specs=[pl.BlockSpec((2, tm, LANES), lambda i: (0, nt - 1 - i, 0)),
                  pl.BlockSpec((tm, LANES), lambda i: (nt - 1 - i, fb)),
                  pl.BlockSpec((1, LANES), lambda i: (0, 0))],
        out_specs=[pl.BlockSpec((tm, LANES), lambda i: (nt - 1 - i, 0)), pl.BlockSpec((1, LANES), lambda i: (0, 0))],
        out_shape=[jax.ShapeDtypeStruct((t, LANES), BF16), jax.ShapeDtypeStruct((1, LANES), F32)],
        scratch_shapes=[pltpu.VMEM((1, LANES), F32)],
        compiler_params=_params("arbitrary"),
    )(dcum, p, b_f)


def _fox_scores(q, k, cq, ck, qi, ki, tq, tk):
    s = _mm_nt((q * (HEAD_DIM ** -0.5)).astype(BF16), k.astype(BF16)) + cq - ck
    qpos = qi * tq + lax.broadcasted_iota(jnp.int32, (tq, tk), 0)
    kpos = ki * tk + lax.broadcasted_iota(jnp.int32, (tq, tk), 1)
    mask = kpos <= qpos
    return jnp.where(mask, s, NEG), mask


def _fox_fwd(q, k, v, cq, ck, hb, tb):
    h, t, n = q.shape
    tq = tk = _tile(t, tb, LANES)
    nq = t // tq

    def body(q_ref, k_ref, v_ref, cq_ref, ck_ref, o_ref, lse_ref, m_sc, l_sc, acc_sc):
        qi, ki = pl.program_id(1), pl.program_id(2)

        @pl.when(ki == 0)
        def _():
            m_sc[...] = jnp.full_like(m_sc, NEG)
            l_sc[...] = jnp.zeros_like(l_sc)
            acc_sc[...] = jnp.zeros_like(acc_sc)

        @pl.when(ki <= qi)
        def _():
            for i in range(hb):
                s, _ = _fox_scores(q_ref[i], k_ref[i], cq_ref[i], ck_ref[i], qi, ki, tq, tk)
                m_old = m_sc[i]
                m_new = jnp.maximum(m_old, jnp.max(s, axis=1, keepdims=True))
                scale = jnp.exp(m_old - m_new)
                e = jnp.exp(s - m_new)
                l_sc[i] = scale * l_sc[i] + jnp.sum(e, axis=1, keepdims=True)
                acc_sc[i] = scale * acc_sc[i] + _mm(e.astype(BF16), v_ref[i].astype(BF16))
                m_sc[i] = m_new

        @pl.when(ki == qi)
        def _():
            o_ref[...] = acc_sc[...] / l_sc[...]
            lse_ref[...] = m_sc[...] + jnp.log(l_sc[...])

    qb = pl.BlockSpec((hb, tq, n), lambda g, i, j: (g, i, 0))
    kb = pl.BlockSpec((hb, tk, n), lambda g, i, j: (g, jnp.minimum(i, j), 0))
    col = pl.BlockSpec((hb, tq, 1), lambda g, i, j: (g, i, 0))
    return pl.pallas_call(
        body, name="fox_fwd", grid=(h // hb, nq, nq),
        in_specs=[qb, kb, kb, col, pl.BlockSpec((hb, 1, tk), lambda g, i, j: (g, 0, jnp.minimum(i, j)))],
        out_specs=[qb, col],
        out_shape=[jax.ShapeDtypeStruct((h, t, n), F32), jax.ShapeDtypeStruct((h, t, 1), F32)],
        scratch_shapes=[pltpu.VMEM((hb, tq, 1), F32), pltpu.VMEM((hb, tq, 1), F32), pltpu.VMEM((hb, tq, n), F32)],
        compiler_params=_params("parallel", "parallel", "arbitrary"),
    )(q, k, v, cq, ck)


def _fox_bwd_dq(q, k, v, cq, ck, lse, o, do, hb, tb):
    h, t, n = q.shape
    tq = tk = _tile(t, tb, LANES)
    nq = t // tq

    def body(q_ref, k_ref, v_ref, cq_ref, ck_ref, lse_ref, o_ref, do_ref, dq_ref, dcq_ref, acc_sc, row_sc):
        qi, ki = pl.program_id(1), pl.program_id(2)

        @pl.when(ki == 0)
        def _():
            acc_sc[...] = jnp.zeros_like(acc_sc)
            row_sc[...] = jnp.zeros_like(row_sc)

        @pl.when(ki <= qi)
        def _():
            for i in range(hb):
                s, mask = _fox_scores(q_ref[i], k_ref[i], cq_ref[i], ck_ref[i], qi, ki, tq, tk)
                dov = do_ref[i]
                delta = jnp.sum(dov * o_ref[i], axis=1, keepdims=True)
                pm = jnp.where(mask, jnp.exp(s - lse_ref[i]), 0.0)
                dp = _mm_nt(dov.astype(BF16), v_ref[i].astype(BF16))
                ds = pm * (dp - delta)
                acc_sc[i] += _mm(ds.astype(BF16), k_ref[i].astype(BF16))
                row_sc[i] += jnp.sum(ds, axis=1, keepdims=True)

        @pl.when(ki == qi)
        def _():
            dq_ref[...] = acc_sc[...] * (HEAD_DIM ** -0.5)
            dcq_ref[...] = row_sc[...]

    qb = pl.BlockSpec((hb, tq, n), lambda g, i, j: (g, i, 0))
    kb = pl.BlockSpec((hb, tk, n), lambda g, i, j: (g, jnp.minimum(i, j), 0))
    col = pl.BlockSpec((hb, tq, 1), lambda g, i, j: (g, i, 0))
    return pl.pallas_call(
        body, name="fox_bwd_dq", grid=(h // hb, nq, nq),
        in_specs=[qb, kb, kb, col, pl.BlockSpec((hb, 1, tk), lambda g, i, j: (g, 0, jnp.minimum(i, j))), col, qb, qb],
        out_specs=[qb, col],
        out_shape=[jax.ShapeDtypeStruct((h, t, n), F32), jax.ShapeDtypeStruct((h, t, 1), F32)],
        scratch_shapes=[pltpu.VMEM((hb, tq, n), F32), pltpu.VMEM((hb, tq, 1), F32)],
        compiler_params=_params("parallel", "parallel", "arbitrary"),
    )(q, k, v, cq, ck, lse, o, do)


def _fox_bwd_dkv(q, k, v, cq, ck, lse, o, do, hb, tb):
    h, t, n = q.shape
    tq = tk = _tile(t, tb, LANES)
    nq = t // tq

    def body(q_ref, k_ref, v_ref, cq_ref, ck_ref, lse_ref, o_ref, do_ref, dk_ref, dv_ref, dck_ref, dk_sc, dv_sc, dc_sc):
        ki, qi = pl.program_id(1), pl.program_id(2)

        @pl.when(qi == 0)
        def _():
            dk_sc[...] = jnp.zeros_like(dk_sc)
            dv_sc[...] = jnp.zeros_like(dv_sc)
            dc_sc[...] = jnp.zeros_like(dc_sc)

        @pl.when(qi >= ki)
        def _():
            for i in range(hb):
                s, mask = _fox_scores(q_ref[i], k_ref[i], cq_ref[i], ck_ref[i], qi, ki, tq, tk)
                dov = do_ref[i]
                delta = jnp.sum(dov * o_ref[i], axis=1, keepdims=True)
                pm = jnp.where(mask, jnp.exp(s - lse_ref[i]), 0.0)
                dob = dov.astype(BF16)
                dp = _mm_nt(dob, v_ref[i].astype(BF16))
                ds = pm * (dp - delta)
                dv_sc[i] += _mm_tn(pm.astype(BF16), dob)
                dk_sc[i] += _mm_tn(ds.astype(BF16), q_ref[i].astype(BF16))
                dc_sc[i] -= jnp.sum(ds, axis=0, keepdims=True)

        @pl.when(qi == nq - 1)
        def _():
            dk_ref[...] = dk_sc[...] * (HEAD_DIM ** -0.5)
            dv_ref[...] = dv_sc[...]
            dck_ref[...] = dc_sc[...]

    qb = pl.BlockSpec((hb, tq, n), lambda g, j, i: (g, jnp.maximum(i, j), 0))
    kb = pl.BlockSpec((hb, tk, n), lambda g, j, i: (g, j, 0))
    col = pl.BlockSpec((hb, tq, 1), lambda g, j, i: (g, jnp.maximum(i, j), 0))
    row = pl.BlockSpec((hb, 1, tk), lambda g, j, i: (g, 0, j))
    return pl.pallas_call(
        body, name="fox_bwd_dkv", grid=(h // hb, nq, nq),
        in_specs=[qb, kb, kb, col, row, col, qb, qb],
        out_specs=[kb, kb, row],
        out_shape=[jax.ShapeDtypeStruct((h, t, n), F32), jax.ShapeDtypeStruct((h, t, n), F32),
                   jax.ShapeDtypeStruct((h, 1, t), F32)],
        scratch_shapes=[pltpu.VMEM((hb, tk, n), F32), pltpu.VMEM((hb, tk, n), F32), pltpu.VMEM((hb, 1, tk), F32)],
        compiler_params=_params("parallel", "parallel", "arbitrary"),
    )(q, k, v, cq, ck, lse, o, do)


FOX_PAIRS = 2
FOX_HEADS_STEP = 2 * FOX_PAIRS


def _lane_half(shape, upper):
    li = lax.broadcasted_iota(jnp.int32, shape, len(shape) - 1)
    return (li >= HEAD_DIM) if upper else (li < HEAD_DIM)


def _col(block, j):
    li = lax.broadcasted_iota(jnp.int32, block.shape, 1)
    return jnp.sum(jnp.where(li == j, block, 0.0), axis=1, keepdims=True)


def _from_cols(cols):
    li = lax.broadcasted_iota(jnp.int32, (cols[0].shape[0], len(cols)), 1)
    out = jnp.zeros(li.shape, F32)
    for j, cj in enumerate(cols):
        out = jnp.where(li == j, cj, out)
    return out


def _from_rows(rows):
    si = lax.broadcasted_iota(jnp.int32, (len(rows), rows[0].shape[1]), 0)
    out = jnp.zeros(si.shape, F32)
    for j, rj in enumerate(rows):
        out = jnp.where(si == j, rj, out)
    return out


def _causal(tq, tk):
    return lax.broadcasted_iota(jnp.int32, (tq, tk), 1) <= lax.broadcasted_iota(jnp.int32, (tq, tk), 0)


def _fox_prep_t(p, c, b_f):
    t = p.shape[0]
    tm = _tile(t, 512, LANES)
    fb = c.o_fl // LANES

    def body(f_ref, b_ref, o_ref, carry):
        @pl.when(pl.program_id(0) == 0)
        def _():
            carry[...] = jnp.zeros_like(carry)

        logf = -_softplus(-(f_ref[...] + b_ref[...]))
        cum = _mm(_tri(tm, False).astype(F32), logf, HI) + carry[...]
        o_ref[...] = cum.T
        carry[...] += jnp.sum(logf, axis=0, keepdims=True)

    return pl.pallas_call(
        body, name="fox_prep", grid=(t // tm,),
        in_specs=[pl.BlockSpec((tm, LANES), lambda i: (i, fb)), pl.BlockSpec((1, LANES), lambda i: (0, 0))],
        out_specs=pl.BlockSpec((LANES, tm), lambda i: (0, i)),
        out_shape=jax.ShapeDtypeStruct((LANES, t), F32),
        scratch_shapes=[pltpu.VMEM((1, LANES), F32)],
        compiler_params=_params("arbitrary"),
    )(p, b_f)


def _fox2_fwd(p, c, cum_t, tb):
    t = p.shape[0]
    tq = tk = _tile(t, tb, LANES)
    nq = t // tq
    pw, nh = FOX_PAIRS * LANES, FOX_HEADS_STEP
    qb, kb, vb, gb = (o // pw for o in (c.o_fq, c.o_fk, c.o_fv, c.o_gfox))
    scale = HEAD_DIM ** -0.5

    def body(q_ref, k_ref, v_ref, g_ref, ck_ref, o_ref, y_ref, lse_ref, m_sc, l_sc, acc_sc):
        g, qi, ki = pl.program_id(0), pl.program_id(1), pl.program_id(2)

        @pl.when(ki == 0)
        def _():
            m_sc[...] = jnp.full_like(m_sc, NEG)
            l_sc[...] = jnp.zeros_like(l_sc)
            acc_sc[...] = jnp.zeros_like(acc_sc)

        def step(diag):
            for pi in range(FOX_PAIRS):
                lanes = slice(pi * LANES, (pi + 1) * LANES)
                q2 = (q_ref[:, lanes] * scale).astype(BF16)
                k2, v2 = k_ref[:, lanes].astype(BF16), v_ref[:, lanes].astype(BF16)
                acc = acc_sc[:, lanes]
                new_acc = acc
                for hh in range(2):
                    hi = 2 * pi + hh
                    mk = _lane_half((tq, LANES), hh == 1)
                    s = _mm_nt(jnp.where(mk, q2, jnp.zeros_like(q2)), k2) - ck_ref[pl.ds(g * nh + hi, 1), :]
                    if diag:
                        s = jnp.where(_causal(tq, tk), s, NEG)
                    m_old = m_sc[hi]
                    m_new = jnp.maximum(m_old, jnp.max(s, axis=1, keepdims=True))
                    a = jnp.exp(m_old - m_new)
                    e = jnp.exp(s - m_new)
                    l_sc[hi] = a * l_sc[hi] + jnp.sum(e, axis=1, keepdims=True)
                    m_sc[hi] = m_new
                    new_acc = jnp.where(mk, a * acc + _mm(e.astype(BF16), v2), new_acc)
                acc_sc[:, lanes] = new_acc

        @pl.when(ki < qi)
        def _():
            step(False)

        @pl.when(ki == qi)
        def _():
            step(True)
            for pi in range(FOX_PAIRS):
                lanes = slice(pi * LANES, (pi + 1) * LANES)
                inv = jnp.where(_lane_half((tq, LANES), False), 1.0 / l_sc[2 * pi], 1.0 / l_sc[2 * pi + 1])
                o = acc_sc[:, lanes] * inv
                gate = g_ref[:, lanes]
                o_ref[:, lanes] = o
                y_ref[:, lanes] = (o * gate * _sig(gate)).astype(BF16)
            lse_ref[0] = _from_cols([m_sc[h] + jnp.log(l_sc[h]) for h in range(nh)])

    row = lambda off: pl.BlockSpec((tq, pw), lambda g, i, j: (i, off + g))
    key = lambda off: pl.BlockSpec((tk, pw), lambda g, i, j: (jnp.minimum(i, j), off + g))
    out = pl.BlockSpec((tq, pw), lambda g, i, j: (i, g))
    return pl.pallas_call(
        body, name="fox_fwd", grid=(c.rw // pw, nq, nq),
        in_specs=[row(qb), key(kb), key(vb), row(gb),
                  pl.BlockSpec((LANES, tk), lambda g, i, j: (0, jnp.minimum(i, j)))],
        out_specs=[out, out, pl.BlockSpec((1, tq, nh), lambda g, i, j: (g, i, 0))],
        out_shape=[jax.ShapeDtypeStruct((t, c.rw), F32), jax.ShapeDtypeStruct((t, c.rw), BF16),
                   jax.ShapeDtypeStruct((c.rw // pw, t, nh), F32)],
        scratch_shapes=[pltpu.VMEM((nh, tq, 1), F32), pltpu.VMEM((nh, tq, 1), F32), pltpu.VMEM((tq, pw), F32)],
        compiler_params=_params("parallel", "parallel", "arbitrary"),
    )(p, p, p, p, cum_t)


def _fox2_grads(q2, k2, v2, do2, o2, lse_h, ck, mk, diag, tq, tk):
    zero = jnp.zeros_like(q2)
    s = _mm_nt(jnp.where(mk, q2, zero), k2) - ck
    if diag:
        s = jnp.where(_causal(tq, tk), s, NEG)
    pm = jnp.exp(s - lse_h)
    delta = jnp.sum(jnp.where(mk, do2 * o2, 0.0), axis=1, keepdims=True)
    dob = do2.astype(BF16)
    dp = _mm_nt(jnp.where(mk, dob, zero), v2)
    return pm, pm * (dp - delta), dob


def _fox2_bwd_dq(p, c, cum_t, lse, o, do, tb):
    t = p.shape[0]
    tq = tk = _tile(t, tb, LANES)
    nq = t // tq
    pw, nh = FOX_PAIRS * LANES, FOX_HEADS_STEP
    qb, kb, vb = (o_ // pw for o_ in (c.o_fq, c.o_fk, c.o_fv))
    scale = HEAD_DIM ** -0.5

    def body(q_ref, k_ref, v_ref, ck_ref, lse_ref, o_ref, do_ref, dq_ref, dcq_ref, acc_sc, row_sc):
        g, qi, ki = pl.program_id(0), pl.program_id(1), pl.program_id(2)

        @pl.when(ki == 0)
        def _():
            acc_sc[...] = jnp.zeros_like(acc_sc)
            row_sc[...] = jnp.zeros_like(row_sc)

        def step(diag):
            lse_blk = lse_ref[0]
            for pi in range(FOX_PAIRS):
                lanes = slice(pi * LANES, (pi + 1) * LANES)
                q2 = (q_ref[:, lanes] * scale).astype(BF16)
                k2, v2 = k_ref[:, lanes].astype(BF16), v_ref[:, lanes].astype(BF16)
                do2, o2 = do_ref[:, lanes], o_ref[:, lanes]
                acc = acc_sc[:, lanes]
                new_acc = acc
                for hh in range(2):
                    hi = 2 * pi + hh
                    mk = _lane_half((tq, LANES), hh == 1)
                    _, ds, _ = _fox2_grads(q2, k2, v2, do2, o2, _col(lse_blk, hi),
                                           ck_ref[pl.ds(g * nh + hi, 1), :], mk, diag, tq, tk)
                    row_sc[hi] += jnp.sum(ds, axis=1, keepdims=True)
                    new_acc = jnp.where(mk, acc + _mm(ds.astype(BF16), k2), new_acc)
                acc_sc[:, lanes] = new_acc

        @pl.when(ki < qi)
        def _():
            step(False)

        @pl.when(ki == qi)
        def _():
            step(True)
            dq_ref[...] = (acc_sc[...] * scale).astype(BF16)
            dcq_ref[0] = _from_cols([row_sc[h] for h in range(nh)])

    row = lambda off: pl.BlockSpec((tq, pw), lambda g, i, j: (i, off + g))
    key = lambda off: pl.BlockSpec((tk, pw), lambda g, i, j: (jnp.minimum(i, j), off + g))
    stat = pl.BlockSpec((1, tq, nh), lambda g, i, j: (g, i, 0))
    return pl.pallas_call(
        body, name="fox_bwd_dq", grid=(c.rw // pw, nq, nq),
        in_specs=[row(qb), key(kb), key(vb), pl.BlockSpec((LANES, tk), lambda g, i, j: (0, jnp.minimum(i, j))),
                  stat, row(0), row(0)],
        out_specs=[row(0), stat],
        out_shape=[jax.ShapeDtypeStruct((t, c.rw), BF16), jax.ShapeDtypeStruct((c.rw // pw, t, nh), F32)],
        scratch_shapes=[pltpu.VMEM((tq, pw), F32), pltpu.VMEM((nh, tq, 1), F32)],
        compiler_params=_params("parallel", "parallel", "arbitrary"),
    )(p, p, p, cum_t, lse, o, do)


def _fox2_bwd_dkv(p, c, cum_t, lse, o, do, tb):
    t = p.shape[0]
    tq = tk = _tile(t, tb, LANES)
    nq = t // tq
    pw, nh = FOX_PAIRS * LANES, FOX_HEADS_STEP
    qb, kb, vb = (o_ // pw for o_ in (c.o_fq, c.o_fk, c.o_fv))
    scale = HEAD_DIM ** -0.5

    def body(q_ref, k_ref, v_ref, ck_ref, lse_ref, o_ref, do_ref, dk_ref, dv_ref, dck_ref, dk_sc, dv_sc, dc_sc):
        g, ki, qi = pl.program_id(0), pl.program_id(1), pl.program_id(2)

        @pl.when(qi == 0)
        def _():
            dk_sc[...] = jnp.zeros_like(dk_sc)
            dv_sc[...] = jnp.zeros_like(dv_sc)
            dc_sc[...] = jnp.zeros_like(dc_sc)

        def step(diag):
            lse_blk = lse_ref[0]
            for pi in range(FOX_PAIRS):
                lanes = slice(pi * LANES, (pi + 1) * LANES)
                q2 = (q_ref[:, lanes] * scale).astype(BF16)
                k2, v2 = k_ref[:, lanes].astype(BF16), v_ref[:, lanes].astype(BF16)
                do2, o2 = do_ref[:, lanes], o_ref[:, lanes]
                dk, dv = dk_sc[:, lanes], dv_sc[:, lanes]
                new_dk, new_dv = dk, dv
                for hh in range(2):
                    hi = 2 * pi + hh
                    mk = _lane_half((tk, LANES), hh == 1)
                    pm, ds, dob = _fox2_grads(q2, k2, v2, do2, o2, _col(lse_blk, hi),
                                              ck_ref[pl.ds(g * nh + hi, 1), :], mk, diag, tq, tk)
                    dc_sc[hi] -= jnp.sum(ds, axis=0, keepdims=True)
                    new_dv = jnp.where(mk, dv + _mm_tn(pm.astype(BF16), dob), new_dv)
                    new_dk = jnp.where(mk, dk + _mm_tn(ds.astype(BF16), q2), new_dk)
                dk_sc[:, lanes] = new_dk
                dv_sc[:, lanes] = new_dv

        @pl.when(qi > ki)
        def _():
            step(False)

        @pl.when(qi == ki)
        def _():
            step(True)

        @pl.when(qi == nq - 1)
        def _():
            dk_ref[...] = dk_sc[...].astype(BF16)
            dv_ref[...] = dv_sc[...].astype(BF16)
            dck_ref[0] = _from_rows([dc_sc[h] for h in range(nh)])

    row = lambda off: pl.BlockSpec((tq, pw), lambda g, j, i: (jnp.maximum(i, j), off + g))
    key = lambda off: pl.BlockSpec((tk, pw), lambda g, j, i: (j, off + g))
    return pl.pallas_call(
        body, name="fox_bwd_dkv", grid=(c.rw // pw, nq, nq),
        in_specs=[row(qb), key(kb), key(vb), pl.BlockSpec((LANES, tk), lambda g, j, i: (0, j)),
                  pl.BlockSpec((1, tq, nh), lambda g, j, i: (g, jnp.maximum(i, j), 0)), row(0), row(0)],
        out_specs=[key(0), key(0), pl.BlockSpec((1, nh, tk), lambda g, j, i: (g, 0, j))],
        out_shape=[jax.ShapeDtypeStruct((t, c.rw), BF16), jax.ShapeDtypeStruct((t, c.rw), BF16),
                   jax.ShapeDtypeStruct((c.rw // pw, nh, t), F32)],
        scratch_shapes=[pltpu.VMEM((tk, pw), F32), pltpu.VMEM((tk, pw), F32), pltpu.VMEM((nh, 1, tk), F32)],
        compiler_params=_params("parallel", "parallel", "arbitrary"),
    )(p, p, p, cum_t, lse, o, do)


def _mem_probs(q, mk, scale):
    s = _mm_nt(q.astype(BF16), mk.astype(BF16)) * scale
    e = jnp.exp(s - jnp.max(s, axis=1, keepdims=True))
    return e / jnp.sum(e, axis=1, keepdims=True)


def _mem_attn_fwd(p, c, mkv):
    t = p.shape[0]
    tm = _tile(t, 512, 8)
    dh = c.mhd
    qb = c.o_mq // dh
    scale = dh ** -0.5

    def body(q_ref, mk_ref, mv_ref, o_ref):
        pm = _mem_probs(q_ref[...], mk_ref[...], scale)
        o_ref[...] = _mm(pm.astype(BF16), mv_ref[...].astype(BF16))

    m = mkv.shape[0]
    return pl.pallas_call(
        body, name="mem_attn_fwd", grid=(t // tm, MEM_HEADS),
        in_specs=[pl.BlockSpec((tm, dh), lambda i, j: (i, qb + j)),
                  pl.BlockSpec((m, dh), lambda i, j: (0, j)),
                  pl.BlockSpec((m, dh), lambda i, j: (0, MEM_HEADS + j))],
        out_specs=pl.BlockSpec((tm, dh), lambda i, j: (i, j)),
        out_shape=jax.ShapeDtypeStruct((t, c.mw), F32),
        compiler_params=_params("parallel", "parallel"),
    )(p, mkv, mkv)


def _mem_attn_bwd(p, c, mkv, do):
    t = p.shape[0]
    tm = _tile(t, 512, 8)
    dh = c.mhd
    qb = c.o_mq // dh
    scale = dh ** -0.5
    m = mkv.shape[0]

    def body(q_ref, mk_ref, mv_ref, do_ref, dq_ref, dmk_ref, dmv_ref):
        @pl.when(pl.program_id(1) == 0)
        def _():
            dmk_ref[...] = jnp.zeros_like(dmk_ref)
            dmv_ref[...] = jnp.zeros_like(dmv_ref)

        qv = q_ref[...].astype(BF16)
        pm = _mem_probs(qv, mk_ref[...], scale)
        dob = do_ref[...].astype(BF16)
        dmv_ref[...] += _mm_tn(pm.astype(BF16), dob)
        dp = _mm_nt(dob, mv_ref[...].astype(BF16))
        ds = (pm * (dp - jnp.sum(pm * dp, axis=1, keepdims=True)) * scale).astype(BF16)
        dq_ref[...] = _mm(ds, mk_ref[...].astype(BF16)).astype(BF16)
        dmk_ref[...] += _mm_tn(ds, qv)

    kvb = lambda off: pl.BlockSpec((m, dh), lambda j, i: (0, off + j))
    return pl.pallas_call(
        body, name="mem_attn_bwd", grid=(MEM_HEADS, t // tm),
        in_specs=[pl.BlockSpec((tm, dh), lambda j, i: (i, qb + j)), kvb(0), kvb(MEM_HEADS),
                  pl.BlockSpec((tm, dh), lambda j, i: (i, j))],
        out_specs=[pl.BlockSpec((tm, dh), lambda j, i: (i, j)), kvb(0), kvb(0)],
        out_shape=[jax.ShapeDtypeStruct((t, c.mw), BF16), jax.ShapeDtypeStruct((m, c.mw), F32),
                   jax.ShapeDtypeStruct((m, c.mw), F32)],
        compiler_params=_params("parallel", "arbitrary"),
    )(p, mkv, mkv, do)


def _adamw(w, g, m, v, name):
    rows, cols = w.shape
    tm = _tile(rows, max(8, (1 << 18) // cols // 8 * 8), 8)
    bc1 = 1.0 - ADAM_B1 ** ADAM_STEP
    bc2 = 1.0 - ADAM_B2 ** ADAM_STEP

    def body(w_ref, g_ref, m_ref, v_ref, go_ref, d_ref, mo_ref, vo_ref):
        gv = g_ref[:, 0:cols]
        mn = ADAM_B1 * m_ref[...] + (1.0 - ADAM_B1) * gv
        vn = ADAM_B2 * v_ref[...] + (1.0 - ADAM_B2) * (gv * gv)
        go_ref[...] = gv
        mo_ref[...] = mn
        vo_ref[...] = vn
        d_ref[...] = -ADAM_LR * ((mn / bc1) / (jnp.sqrt(vn / bc2) + ADAM_EPS) + ADAM_WD * w_ref[...])

    blk = pl.BlockSpec((tm, cols), lambda i: (i, 0))
    shp = jax.ShapeDtypeStruct((rows, cols), F32)
    return pl.pallas_call(
        body, name=name, grid=(rows // tm,),
        in_specs=[blk, pl.BlockSpec((tm, g.shape[1]), lambda i: (i, 0)), blk, blk],
        out_specs=[blk] * 4, out_shape=[shp] * 4,
        compiler_params=_params("parallel"),
    )(w, g, m, v)


SCAN_HEADS = 12
FOX_BLOCK = 512


def _local_step(c, x, mem, tgt, w, riders=None):
    t = x.shape[0]
    rw = c.rw
    riders = riders or {}
    carried = {}
    hd = lambda z: z.reshape(t, c.h, HEAD_DIM).transpose(1, 0, 2)
    uh = lambda z: z.transpose(1, 0, 2).reshape(t, rw)
    vecs = (w["mu"], w["w0"], w["a0"], w["k_k"], w["k_a"], w["wd"], w["wi"])

    h, rinv = _rms_fwd(x, w["g_pre"], "rms_pre")
    if "in_proj" in riders:
        groups, finish = riders["in_proj"]
        p, late = _matmul(h, w["wp"], name="in_proj", tk=4096, attach=groups)
        w = dict(w, **finish(late))
    else:
        p = _matmul(h, w["wp"], name="in_proj", tk=4096)
    r, lw, km, v, a, b = _rwkv_pre_fwd(p, c, *vecs)
    scan_in = tuple(hd(z) for z in (r, lw, km, v, a, b))
    hb = max(n for n in range(1, SCAN_HEADS + 1) if c.h % n == 0)
    ysh, states = _rwkv_scan_fwd(*scan_in, hb)
    ys = uh(ysh)
    yc_r = _rwkv_post_fwd(ys, r, km, v, p, c, w["ln_w"], w["ln_b"], w["r_k"])

    cum_t = _fox_prep_t(p, c, w["b_f"])
    yfox, yc_f, lse = _fox2_fwd(p, c, cum_t, FOX_BLOCK)

    memn, rinv_m = _rms_fwd(mem, w["g_mem"], "rms_mem")
    mkv = _matmul(memn, w["w_mem_kv"], name="mem_kv")
    ymem = _mem_attn_fwd(p, c, mkv)
    yc_m = _gate_fwd(ymem, p, c.o_gmq, "gate_mem")

    ycat = jnp.concatenate([yc_r, yc_f, yc_m], axis=1)
    yo = _matmul(ycat, w["w_out"], name="out_proj", tn=512, tk=4096)
    loss, dout, dyo, dg_post = _post_loss(yo, x, tgt, w["g_post"], "post_loss")

    dyc = _matmul(dyo, w["w_out"], tb=True, name="d_ycat", tn=512, tk=4096)
    dw_out = _matmul(ycat, dyo, ta=True, name="d_w_out", tn=512, tk=4096, out_dtype=BF16)
    dys, dr2, dkm2, dv2, dg_r, dln_w, dln_b, dr_k = _rwkv_post_bwd(
        dyc, ys, r, km, v, p, c, w["ln_w"], w["ln_b"], w["r_k"])
    dyf, dg_f = _gate_bwd(dyc, rw, yfox, p, c.o_gfox, "gate_fox_bwd")
    dym, dg_m = _gate_bwd(dyc, 2 * rw, ymem, p, c.o_gmq, "gate_mem_bwd")

    scan_g = _rwkv_scan_bwd(*scan_in, states, hd(dys), hb)
    dps, dzw, dza, twb, alb, dw0, da0, dk_k, dk_a = _rwkv_pre_bwd(
        p, c, *vecs, *(uh(z) for z in scan_g), dr2, dkm2, dv2)
    dwd = _matmul(twb, dzw, ta=True, name="d_w_decay", out_dtype=BF16)
    dwi = _matmul(alb, dza, ta=True, name="d_w_iclr", out_dtype=BF16)
    dp_shift, dmu = _shift_bwd(dps, p, c, w["mu"])

    dfq, dcq = _fox2_bwd_dq(p, c, cum_t, lse, yfox, dyf, FOX_BLOCK)
    dfk, dfv, dck = _fox2_bwd_dkv(p, c, cum_t, lse, yfox, dyf, FOX_BLOCK)
    dcum = jnp.pad(jnp.stack([dcq.transpose(1, 0, 2).reshape(t, c.h), dck.reshape(c.h, t).T]),
                   ((0, 0), (0, 0), (0, LANES - c.h)))
    dfl, db_f = _fox_logit_bwd(dcum, p, c, w["b_f"])

    dmq, dmk, dmv = _mem_attn_bwd(p, c, mkv, dym)
    dmkv = jnp.concatenate([dmk, dmv], axis=1)
    dw_mkv = _matmul(memn, dmkv, ta=True, name="d_w_mem_kv", out_dtype=BF16)
    dmemn = _matmul(dmkv, w["w_mem_kv"], tb=True, name="d_memn")
    _, dg_mem = _rms_bwd(dmemn, mem, rinv_m, w["g_mem"], jnp.zeros_like(mem), "rms_mem_bwd")

    dp = jnp.concatenate([dp_shift, dg_r, dfq, dfk, dfv, dg_f, dmq, dg_m, dfl], axis=1)
    rest = dict(wd=dwd, wi=dwi, w_mem_kv=dw_mkv, w_out=dw_out)
    if "d_w_in" in riders:
        dwp, carried["rest"] = _matmul(h, dp, ta=True, name="d_w_in", tk=4096, out_dtype=BF16,
                                       attach=riders["d_w_in"](rest))
    else:
        dwp = _matmul(h, dp, ta=True, name="d_w_in", tk=4096, out_dtype=BF16)
    if "d_h" in riders:
        dh, carried["wp"] = _matmul(dp, w["wp"], tb=True, name="d_h", tk=2944, attach=riders["d_h"](dwp))
    else:
        dh = _matmul(dp, w["wp"], tb=True, name="d_h", tk=2944)
    grad_x, dg_pre = _rms_bwd(dh, x, rinv, w["g_pre"], dout, "rms_pre_bwd")

    small = dict(g_pre=dg_pre, mu=dmu, w0=dw0, a0=da0, k_k=dk_k, k_a=dk_a, r_k=dr_k, ln_w=dln_w, ln_b=dln_b,
                 b_f=db_f, g_mem=dg_mem, g_post=dg_post)
    return loss, grad_x, dict(wp=dwp, **rest), small, carried


CHIPS = ((1, 0, 0), (0, 1, 0), (1, 1, 0))
SIBLING = ((0, 0, 1),)
ALL_PEERS = tuple((i, j, k) for i in (0, 1) for j in (0, 1) for k in (0, 1))[1:]


def _chip_of(pos):
    return 2 * pos[0] + pos[1]


DMA_CHUNK = 4 << 20


def _pieces(shape, itemsize):
    lead, (rows, cols) = shape[:-2], shape[-2:]
    k = 1
    if rows % 16 == 0:
        k = max(1, min(rows // 16, -(-rows * cols * itemsize // DMA_CHUNK)))
        while rows % k or (rows // k) % 16:
            k -= 1
    band = rows // k
    idxs = [()]
    for n in lead:
        idxs = [i + (j,) for i in idxs for j in range(n)]
    return [i + (pl.ds(j * band, band),) for i in idxs for j in range(k)]


def _peer_of(me, mask):
    return tuple(1 - v if f else v for v, f in zip(me, mask))


def _exchange(name, groups):
    n = len(groups)
    plan = _plan(groups)

    def body(*refs):
        copies = _copies(groups, plan, refs[:n], refs[n:2 * n], refs[2 * n], refs[2 * n + 1])
        for cp in copies:
            cp.start()
        for cp in copies:
            cp.wait()

    any_spec = pl.BlockSpec(memory_space=pl.ANY)
    return pl.pallas_call(
        body, name=name,
        in_specs=[any_spec] * n, out_specs=[any_spec] * n,
        out_shape=_exchange_shapes(groups),
        scratch_shapes=[pltpu.SemaphoreType.DMA((len(plan),)), pltpu.SemaphoreType.DMA((len(plan),))],
    )(*[g["src"] for g in groups])


def _plan(groups):
    return [(gi, ti, idx) for gi, g in enumerate(groups) for ti in range(len(g["transfers"]))
            for idx in _pieces(tuple(g["piece"]), g["src"].dtype.itemsize)]


def _exchange_shapes(groups):
    return [jax.ShapeDtypeStruct((g["slots"],) + tuple(g["piece"]), g["src"].dtype) for g in groups]


def _copies(groups, plan, srcs, outs, send_sems, recv_sems):
    me = (lax.axis_index("x"), lax.axis_index("y"), lax.axis_index("c"))
    copies = []
    for k, (gi, ti, idx) in enumerate(plan):
        mask, view, slot = groups[gi]["transfers"][ti]
        peer = _peer_of(me, mask)
        copies.append(pltpu.make_async_remote_copy(
            src_ref=view(srcs[gi], me, peer).at[idx], dst_ref=outs[gi].at[slot(me, peer)].at[idx],
            send_sem=send_sems.at[k], recv_sem=recv_sems.at[k],
            device_id=peer, device_id_type=MESH))
    return copies


def _my_chip():
    return 2 * lax.axis_index("x") + lax.axis_index("y")


def _put(buf, block, slot):
    return lax.dynamic_update_slice(buf, block[None], (slot,) + (0,) * block.ndim)


def _sum_slots(recv, own, k, out_dtype, name):
    s, rows, cols = recv.shape
    budget = max(16, ((4 << 20) // ((s + 1) * cols * 4)) // 16 * 16)
    tr = _tile(rows, budget, 16)
    own_many = own.shape[0] > 1

    def body(k_ref, *refs):
        out_ref = refs[s + 1]
        mine = refs[s][0].astype(F32)
        acc = None
        for i in range(s):
            term = jnp.where(k_ref[0] == i, mine, refs[i][0].astype(F32))
            acc = term if acc is None else acc + term
        out_ref[...] = acc.astype(out_ref.dtype)

    def slot_spec(i):
        return pl.BlockSpec((1, tr, cols), lambda j, kr: (jnp.where(kr[0] == i, (i + 1) % s, i), j, 0))

    grid_spec = pltpu.PrefetchScalarGridSpec(
        num_scalar_prefetch=1, grid=(rows // tr,),
        in_specs=[slot_spec(i) for i in range(s)]
                 + [pl.BlockSpec((1, tr, cols), lambda j, kr: (kr[0] if own_many else 0, j, 0))],
        out_specs=pl.BlockSpec((tr, cols), lambda j, kr: (j, 0)))
    return pl.pallas_call(
        body, name=name, grid_spec=grid_spec,
        out_shape=jax.ShapeDtypeStruct((rows, cols), out_dtype),
        compiler_params=_params("parallel"),
    )(k, *([recv] * s), own)


def _all_gather(shards):
    return _gather_finish(shards, _exchange("gather_chips", _gather_groups(shards)), "gather_pair")


def _gather_groups(shards):
    halves = [s.reshape(2, s.shape[0] // 2, s.shape[1]) for s in shards]
    return [dict(src=q, slots=4, piece=q.shape[1:],
                 transfers=[(m, lambda ref, me, peer: ref.at[me[2]], lambda me, peer: _chip_of(me)) for m in CHIPS])
            for q in halves]


def _gather_finish(shards, first, name):
    core, chip = lax.axis_index("c"), _my_chip()
    other_chip = lambda m: (lambda me: _chip_of(_peer_of(me, m)))
    second = _exchange(name, [
        dict(src=q, slots=4, piece=q.shape[1:],
             transfers=[(SIBLING[0], (lambda f: lambda ref, me, peer: ref.at[f(me)])(other_chip(m)),
                         (lambda f: lambda me, peer: f(me))(other_chip(m))) for m in CHIPS])
        for q in first])
    out = []
    for s, a, b in zip(shards, first, second):
        full = jnp.concatenate([jnp.where(core == 0, a, b), jnp.where(core == 0, b, a)], axis=1)
        out.append(_put(full, s, chip))
    return out


def _reduce_pair(partials, tag):
    core1 = lax.axis_index("c").reshape(1).astype(jnp.int32)
    halves = [q.reshape(4, 2, q.shape[1] // 2, q.shape[2]).transpose(1, 0, 2, 3) for q in partials]
    pair = _exchange("reduce_pair_" + tag, [
        dict(src=q, slots=2, piece=q.shape[1:],
             transfers=[(SIBLING[0], lambda ref, me, peer: ref.at[peer[2]], lambda me, peer: me[2])])
        for q in halves])
    flat = lambda e: e.reshape(2, -1, e.shape[-1])
    return [_sum_slots(flat(e), flat(q), core1, BF16, "reduce_pair_sum_" + tag).reshape(q.shape[1:])
            for e, q in zip(pair, halves)]


def _reduce_chips_groups(chip_sums):
    return [dict(src=q, slots=4, piece=q.shape[1:],
                 transfers=[(m, lambda ref, me, peer: ref.at[_chip_of(peer)], lambda me, peer: _chip_of(me))
                            for m in CHIPS])
            for q in chip_sums]


def _reduce_finish(crossed, chip_sums, tag):
    core = lax.axis_index("c")
    chip1 = _my_chip().reshape(1).astype(jnp.int32)
    sums = [_sum_slots(e, q, chip1, F32, "reduce_chips_sum_" + tag) for e, q in zip(crossed, chip_sums)]
    swapped = _exchange("reduce_swap_" + tag, [
        dict(src=q, slots=2, piece=q.shape, transfers=[(SIBLING[0], lambda ref, me, peer: ref, lambda me, peer: me[2])])
        for q in sums])
    return [_put(e, q, core).reshape(-1, e.shape[-1]) for e, q in zip(swapped, sums)]


def _reduce_scatter(partials):
    chip_sums = _reduce_pair(partials, "all")
    return _reduce_finish(_exchange("reduce_chips", _reduce_chips_groups(chip_sums)), chip_sums, "all")


def _all_reduce_small(vec):
    dev = 4 * lax.axis_index("x") + 2 * lax.axis_index("y") + lax.axis_index("c")
    got = _exchange("reduce_small", [
        dict(src=vec, slots=8, piece=vec.shape,
             transfers=[(m, lambda ref, me, peer: ref, lambda me, peer: 4 * me[0] + 2 * me[1] + me[2])
                        for m in ALL_PEERS])])[0]
    return _sum_slots(got, vec[None], dev.reshape(1).astype(jnp.int32), F32, "reduce_small_sum")


SMALL = ("g_pre", "mu", "w0", "a0", "k_k", "k_a", "r_k", "ln_w", "ln_b", "b_f", "g_mem", "g_post")


def _pad_cols(a, n):
    return jnp.pad(a, ((0, 0),) * (a.ndim - 1) + ((0, n - a.shape[-1]),))


def kernel(x, mem, g_pre, w_in, mu_rwkv, w0, w_decay_up, a0, w_iclr_up, k_k, k_a, r_k, ln_x_w, ln_x_b, b_f, g_mem, w_mem_kv, w_out, g_post, loss_target, m_g_pre, m_w_in, m_mu_rwkv, m_w0, m_w_decay_up, m_a0, m_w_iclr_up, m_k_k, m_k_a, m_r_k, m_ln_x_w, m_ln_x_b, m_b_f, m_g_mem, m_w_mem_kv, m_w_out, m_g_post, v_g_pre, v_w_in, v_mu_rwkv, v_w0, v_w_decay_up, v_a0, v_w_iclr_up, v_k_k, v_k_a, v_r_k, v_ln_x_w, v_ln_x_b, v_b_f, v_g_mem, v_w_mem_kv, v_w_out, v_g_post):
    d = x.shape[-1]
    c = Cfg(d)
    ws = w_in.shape[-1]
    wpad = -(-ws // LANES) * LANES
    nh = c.h

    g_in, g_wd, g_wi = _all_gather([
        _pad_cols(w_in[0].astype(BF16), wpad), w_decay_up[0].astype(BF16), w_iclr_up[0].astype(BF16)])
    w_full = jnp.concatenate([g_in[s, :, :ws] for s in range(4)], axis=1)
    fl = c.ref_fl
    wp = jnp.concatenate([w_full[:, :fl], w_full[:, fl + nh:], _pad_cols(w_full[:, fl:fl + nh], LANES)], axis=1)
    unshard = lambda g: g.transpose(1, 0, 2).reshape(g.shape[1], -1)
    weights = dict(wp=wp, wd=unshard(g_wd), wi=unshard(g_wi),
                   g_pre=g_pre, mu=mu_rwkv, w0=w0, a0=a0, k_k=k_k, k_a=k_a, r_k=r_k.reshape(1, -1),
                   ln_w=ln_x_w, ln_b=ln_x_b, b_f=_pad_cols(b_f, LANES), g_mem=g_mem, g_post=g_post)
    late_shards = [w_out[0].astype(BF16), w_mem_kv[0].astype(BF16)]

    def late_weights(first):
        g_out, g_mkv = _gather_finish(late_shards, first, "gather_pair_late")
        return dict(w_out=g_out.reshape(-1, d), w_mem_kv=g_mkv.reshape(d, -1))

    by_chip = lambda g: jnp.stack(jnp.split(g, 4, axis=1))
    pair_sums = {}

    def ride_rest(g):
        pair_sums["rest"] = _reduce_pair([g["w_out"].reshape(4, -1, d), g["w_mem_kv"].reshape(4, d // 4, -1),
                                          by_chip(g["wd"]), by_chip(g["wi"])], "rest")
        return _reduce_chips_groups(pair_sums["rest"])

    def ride_wp(dwp):
        dw_full = jnp.concatenate([dwp[:, :fl], dwp[:, c.o_fl:c.o_fl + nh], dwp[:, fl:c.o_fl]], axis=1)
        pair_sums["wp"] = _reduce_pair(
            [jnp.stack([_pad_cols(dw_full[:, s * ws:(s + 1) * ws], wpad) for s in range(4)])], "w_in")
        return _reduce_chips_groups(pair_sums["wp"])

    loss, grad_x, _, small, carried = _local_step(
        c, x[0], mem[0], loss_target[0], weights,
        riders={"in_proj": (_gather_groups(late_shards), late_weights), "d_w_in": ride_rest, "d_h": ride_wp})
    red = (_reduce_finish(carried["wp"], pair_sums["wp"], "w_in")
           + _reduce_finish(carried["rest"], pair_sums["rest"], "rest"))
    big_w = (w_in[0], w_out[0], w_mem_kv[0], w_decay_up[0], w_iclr_up[0])
    big_m = (m_w_in[0], m_w_out[0], m_w_mem_kv[0], m_w_decay_up[0], m_w_iclr_up[0])
    big_v = (v_w_in[0], v_w_out[0], v_w_mem_kv[0], v_w_decay_up[0], v_w_iclr_up[0])
    big_names = ("w_in", "w_out", "w_mem_kv", "w_decay_up", "w_iclr_up")
    upd = {n: _adamw(w_, g_, m_, v_, "adamw_" + n) for n, w_, g_, m_, v_ in zip(big_names, big_w, red, big_m, big_v)}

    small_w = dict(g_pre=g_pre, mu=mu_rwkv, w0=w0, a0=a0, k_k=k_k, k_a=k_a, r_k=r_k.reshape(1, -1), ln_w=ln_x_w,
                   ln_b=ln_x_b, b_f=b_f, g_mem=g_mem, g_post=g_post)
    small_m = dict(g_pre=m_g_pre, mu=m_mu_rwkv, w0=m_w0, a0=m_a0, k_k=m_k_k, k_a=m_k_a, r_k=m_r_k.reshape(1, -1),
                   ln_w=m_ln_x_w, ln_b=m_ln_x_b, b_f=m_b_f, g_mem=m_g_mem, g_post=m_g_post)
    small_v = dict(g_pre=v_g_pre, mu=v_mu_rwkv, w0=v_w0, a0=v_a0, k_k=v_k_k, k_a=v_k_a, r_k=v_r_k.reshape(1, -1),
                   ln_w=v_ln_x_w, ln_b=v_ln_x_b, b_f=v_b_f, g_mem=v_g_mem, g_post=v_g_post)
    widths = [-(-small_w[n].shape[1] // LANES) * LANES for n in SMALL]
    pack = lambda t: jnp.concatenate([_pad_cols(t[n], wd_) for n, wd_ in zip(SMALL, widths)]
                                     + [jnp.zeros((1, LANES), F32)], axis=1)
    g_packed = jnp.concatenate([_pad_cols(small[n], wd_) for n, wd_ in zip(SMALL, widths)]
                               + [_pad_cols(loss, LANES)], axis=1)
    g_sum = _all_reduce_small(g_packed)
    s_upd = _adamw(pack(small_w), g_sum, pack(small_m), pack(small_v), "adamw_small")
    offs = [sum(widths[:i]) for i in range(len(SMALL))]

    def take(kind, n):
        i = SMALL.index(n)
        piece = s_upd[kind][:, offs[i]:offs[i] + small_w[n].shape[1]]
        return piece.reshape(r_k.shape) if n == "r_k" else piece

    total_loss = g_sum[0, sum(widths)]
    order = ("g_pre", "w_in", "mu", "w0", "w_decay_up", "a0", "w_iclr_up", "k_k", "k_a", "r_k", "ln_w", "ln_b", "b_f",
             "g_mem", "w_mem_kv", "w_out", "g_post")
    outs = [total_loss, grad_x[None]]
    for kind in range(4):
        for n in order:
            outs.append(upd[n][kind][None] if n in upd else take(kind, n))
    return tuple(outs)
```

```python
import functools

import jax
import jax.numpy as jnp
from jax import lax
from jax.experimental import pallas as pl
from jax.experimental.pallas import tpu as pltpu

F32 = jnp.float32
BF16 = jnp.bfloat16
HI = lax.Precision.HIGHEST
MESH = pl.DeviceIdType.MESH

HEAD_DIM = 64
MEM_HEADS = 4
LORA = 128
CHUNK = 64
RMS_EPS = 1e-6
GN_EPS = 64e-5
LANES = 128
VMEM_LIMIT = 56 * 1024 * 1024

ADAM_LR, ADAM_B1, ADAM_B2, ADAM_EPS, ADAM_WD, ADAM_STEP = 0.001, 0.9, 0.999, 1e-08, 0.01, 10


class Cfg:
    def __init__(self, d):
        self.d = d
        self.rw = 3 * d // 8
        self.mw = d // 4
        self.h = self.rw // HEAD_DIM
        self.mhd = self.mw // MEM_HEADS
        self.shift = 3 * self.rw + 2 * LORA
        self.in_width = self.shift + 5 * self.rw + self.h + 2 * self.mw
        o = self.shift
        self.o_grw = o; o += self.rw
        self.o_fq = o; o += self.rw
        self.o_fk = o; o += self.rw
        self.o_fv = o; o += self.rw
        self.o_gfox = o; o += self.rw
        self.o_mq = o; o += self.mw
        self.o_gmq = o; o += self.mw
        self.o_fl = o; o += LANES
        self.wp = o
        self.ref_fl = self.shift + 4 * self.rw


def _tile(n, pref, align=LANES):
    if n <= pref:
        return n
    t = (pref // align) * align
    while t >= align:
        if n % t == 0:
            return t
        t -= align
    return n


def _params(*sem):
    return pltpu.CompilerParams(dimension_semantics=sem, vmem_limit_bytes=VMEM_LIMIT)


def _sig(x):
    return 1.0 / (1.0 + jnp.exp(-x))


def _softplus(x):
    return jnp.maximum(x, 0.0) + jnp.log(1.0 + jnp.exp(-jnp.abs(x)))


def _dot(a, b, dims, prec=None):
    return lax.dot_general(a, b, (dims, ((), ())), precision=prec, preferred_element_type=F32)


def _mm(a, b, prec=None):
    return _dot(a, b, ((1,), (0,)), prec)


def _mm_nt(a, b, prec=None):
    return _dot(a, b, ((1,), (1,)), prec)


def _mm_tn(a, b, prec=None):
    return _dot(a, b, ((0,), (0,)), prec)


def _split(a):
    hi = a.astype(BF16)
    return hi, (a - hi.astype(F32)).astype(BF16)


def _dot3(a, b, dims, passes=3):
    d = lambda x, y: lax.dot_general(x, y, dims, preferred_element_type=F32)
    if passes == 1:
        return d(a.astype(BF16), b.astype(BF16))
    (ah, al), (bh, bl) = _split(a), _split(b)
    return d(ah, bh) + (d(ah, bl) + d(al, bh))


def _bmm(a, b, passes=3):
    return _dot3(a, b, (((2,), (1,)), ((0,), (0,))), passes)


def _bmm_nt(a, b, passes=3):
    return _dot3(a, b, (((2,), (2,)), ((0,), (0,))), passes)


def _bmm_tn(a, b, passes=3):
    return _dot3(a, b, (((1,), (1,)), ((0,), (0,))), passes)


def _bmm_01(m01, x):
    x1 = x.astype(BF16)
    r1 = x - x1.astype(F32)
    x2 = r1.astype(BF16)
    x3 = (r1 - x2.astype(F32)).astype(BF16)
    d = lambda y: lax.dot_general(m01, y, (((2,), (1,)), ((0,), (0,))), preferred_element_type=F32)
    return d(x1) + (d(x2) + d(x3))


def _matmul(a, b, *, ta=False, tb=False, out_dtype=F32, name, tm=1024, tn=1024, tk=1024, attach=None):
    m, k = (a.shape[1], a.shape[0]) if ta else a.shape
    n = b.shape[0] if tb else b.shape[1]
    tm, tn, tk = _tile(m, tm), _tile(n, tn), _tile(k, tk)
    nk = k // tk
    grid = (m // tm, n // tn, nk)
    dims = ((0 if ta else 1,), (1 if tb else 0,))
    groups = attach or []
    ng = len(groups)
    plan = _plan(groups)

    def body(a_ref, b_ref, *rest):
        srcs, o_ref, outs, scratch = rest[:ng], rest[ng], rest[ng + 1:2 * ng + 1], rest[2 * ng + 1:]
        acc = scratch[0] if nk > 1 else None
        if ng:
            copies = _copies(groups, plan, srcs, outs, scratch[-2], scratch[-1])
            ids = [pl.program_id(ax) for ax in range(3)]

            @pl.when((ids[0] == 0) & (ids[1] == 0) & (ids[2] == 0))
            def _():
                for cp in copies:
                    cp.start()

        part = _dot(a_ref[...].astype(BF16), b_ref[...].astype(BF16), dims)
        if nk == 1:
            o_ref[...] = part.astype(o_ref.dtype)
        else:
            kk = pl.program_id(2)

            @pl.when(kk == 0)
            def _():
                acc[...] = part

            @pl.when(kk > 0)
            def _():
                acc[...] += part

            @pl.when(kk == nk - 1)
            def _():
                o_ref[...] = acc[...].astype(o_ref.dtype)

        if ng:
            @pl.when((ids[0] == grid[0] - 1) & (ids[1] == grid[1] - 1) & (ids[2] == grid[2] - 1))
            def _():
                for cp in copies:
                    cp.wait()

    a_spec = pl.BlockSpec((tk, tm), lambda i, j, kk: (kk, i)) if ta else pl.BlockSpec((tm, tk), lambda i, j, kk: (i, kk))
    b_spec = pl.BlockSpec((tn, tk), lambda i, j, kk: (j, kk)) if tb else pl.BlockSpec((tk, tn), lambda i, j, kk: (kk, j))
    any_spec = pl.BlockSpec(memory_space=pl.ANY)
    sems = [pltpu.SemaphoreType.DMA((len(plan),)), pltpu.SemaphoreType.DMA((len(plan),))] if ng else []
    res = pl.pallas_call(
        body, name=name, grid=grid,
        in_specs=[a_spec, b_spec] + [any_spec] * ng,
        out_specs=[pl.BlockSpec((tm, tn), lambda i, j, kk: (i, j))] + [any_spec] * ng,
        out_shape=[jax.ShapeDtypeStruct((m, n), out_dtype)] + _exchange_shapes(groups),
        scratch_shapes=([pltpu.VMEM((tm, tn), F32)] if nk > 1 else []) + sems,
        compiler_params=_params(*(("arbitrary",) * 3 if ng else ("parallel", "parallel", "arbitrary"))),
    )(a, b, *[g["src"] for g in groups])
    return (res[0], list(res[1:])) if ng else res[0]


def _rms_fwd(x, g, name):
    t, d = x.shape
    tm = _tile(t, 256, 8)

    def body(x_ref, g_ref, h_ref, r_ref):
        xv = x_ref[...]
        r = lax.rsqrt(jnp.mean(xv * xv, axis=-1, keepdims=True) + RMS_EPS)
        h_ref[...] = (xv * r * g_ref[...]).astype(BF16)
        r_ref[...] = r

    return pl.pallas_call(
        body, name=name, grid=(t // tm,),
        in_specs=[pl.BlockSpec((tm, d), lambda i: (i, 0)), pl.BlockSpec((1, d), lambda i: (0, 0))],
        out_specs=[pl.BlockSpec((tm, d), lambda i: (i, 0)), pl.BlockSpec((tm, 1), lambda i: (i, 0))],
        out_shape=[jax.ShapeDtypeStruct((t, d), BF16), jax.ShapeDtypeStruct((t, 1), F32)],
        compiler_params=_params("parallel"),
    )(x, g)


def _rms_bwd(dh, x, rinv, g, add, name):
    t, d = x.shape
    tm = _tile(t, 256, 8)

    def body(dh_ref, x_ref, r_ref, g_ref, add_ref, dx_ref, dg_ref):
        @pl.when(pl.program_id(0) == 0)
        def _():
            dg_ref[...] = jnp.zeros_like(dg_ref)

        r = r_ref[...]
        xn = x_ref[...] * r
        dhv = dh_ref[...]
        dg_ref[...] += jnp.sum(dhv * xn, axis=0, keepdims=True)
        dxn = dhv * g_ref[...]
        dx_ref[...] = add_ref[...] + r * (dxn - xn * jnp.mean(dxn * xn, axis=-1, keepdims=True))

    row = pl.BlockSpec((tm, d), lambda i: (i, 0))
    vec = pl.BlockSpec((1, d), lambda i: (0, 0))
    return pl.pallas_call(
        body, name=name, grid=(t // tm,),
        in_specs=[row, row, pl.BlockSpec((tm, 1), lambda i: (i, 0)), vec, row],
        out_specs=[row, vec],
        out_shape=[jax.ShapeDtypeStruct((t, d), F32), jax.ShapeDtypeStruct((1, d), F32)],
        compiler_params=_params("arbitrary"),
    )(dh, x, rinv, g, add)


def _post_loss(yo, x, tgt, g, name):
    t, d = x.shape
    tm = _tile(t, 256, 8)

    def body(yo_ref, x_ref, t_ref, g_ref, loss_ref, dout_ref, dyo_ref, dg_ref):
        @pl.when(pl.program_id(0) == 0)
        def _():
            dg_ref[...] = jnp.zeros_like(dg_ref)
            loss_ref[...] = jnp.zeros_like(loss_ref)

        yv = yo_ref[...]
        r = lax.rsqrt(jnp.mean(yv * yv, axis=-1, keepdims=True) + RMS_EPS)
        n = yv * r
        err = x_ref[...] + n * g_ref[...] - t_ref[...]
        loss_ref[...] += 0.5 * jnp.sum(jnp.mean(err * err, axis=-1, keepdims=True), axis=0, keepdims=True)
        dout = err * (1.0 / d)
        dout_ref[...] = dout
        dg_ref[...] += jnp.sum(dout * n, axis=0, keepdims=True)
        dn = dout * g_ref[...]
        dyo_ref[...] = (r * (dn - n * jnp.mean(dn * n, axis=-1, keepdims=True))).astype(BF16)

    row = pl.BlockSpec((tm, d), lambda i: (i, 0))
    vec = pl.BlockSpec((1, d), lambda i: (0, 0))
    return pl.pallas_call(
        body, name=name, grid=(t // tm,),
        in_specs=[row, row, row, vec],
        out_specs=[pl.BlockSpec((1, 1), lambda i: (0, 0)), row, row, vec],
        out_shape=[jax.ShapeDtypeStruct((1, 1), F32), jax.ShapeDtypeStruct((t, d), F32),
                   jax.ShapeDtypeStruct((t, d), BF16), jax.ShapeDtypeStruct((1, d), F32)],
        compiler_params=_params("arbitrary"),
    )(yo, x, tgt, g)


def _head_sum(x):
    ri = lax.broadcasted_iota(jnp.int32, (LANES, LANES), 0) // HEAD_DIM
    ci = lax.broadcasted_iota(jnp.int32, (LANES, LANES), 1) // HEAD_DIM
    e = (ri == ci).astype(BF16)
    x1 = x.astype(BF16)
    r1 = x - x1.astype(F32)
    x2 = r1.astype(BF16)
    x3 = (r1 - x2.astype(F32)).astype(BF16)
    parts = []
    for i in range(x.shape[1] // LANES):
        sl = slice(i * LANES, (i + 1) * LANES)
        parts.append(_mm(x1[:, sl], e) + (_mm(x2[:, sl], e) + _mm(x3[:, sl], e)))
    return parts[0] if len(parts) == 1 else jnp.concatenate(parts, axis=1)


def _shifted(p_cur, before, first, mu):
    rolled = pltpu.roll(p_cur, 1, 0)
    prev_row = jnp.where(first, 0.0, before)
    row0 = lax.broadcasted_iota(jnp.int32, p_cur.shape, 0) == 0
    prev = jnp.where(row0, prev_row, rolled)
    return p_cur + (prev - p_cur) * mu, prev


def _rwkv_features(ps, rw, w0, a0, k_k, k_a, wd, wi):
    r, k, v = ps[:, 0:rw], ps[:, rw:2 * rw], ps[:, 2 * rw:3 * rw]
    wl, al = ps[:, 3 * rw:3 * rw + LORA], ps[:, 3 * rw + LORA:3 * rw + 2 * LORA]
    tw = jnp.tanh(wl)
    zw = w0 + _mm(tw.astype(BF16), wd)
    logw = -jnp.exp(-_softplus(-zw) - 0.5)
    alpha = _sig(a0 + _mm(al.astype(BF16), wi))
    kkr = k * k_k
    n2 = _head_sum(kkr * kkr)
    rn = lax.rsqrt(jnp.maximum(n2, 1e-24))
    kk = kkr * rn
    kmod = k * (1.0 + (alpha - 1.0) * k_a)
    return dict(r=r, k=k, v=v, tw=tw, al=al, zw=zw, logw=logw, alpha=alpha, kk=kk, rn=rn, n2=n2, kmod=kmod)


def _rwkv_pre_fwd(p, c, mu, w0, a0, k_k, k_a, wd, wi):
    t = p.shape[0]
    tm = _tile(t, 128, 8)
    rw, sh = c.rw, c.shift

    def body(p_ref, pp_ref, mu_ref, w0_ref, a0_ref, kk_ref, ka_ref, wd_ref, wi_ref,
             r_ref, lw_ref, km_ref, v_ref, a_ref, b_ref):
        ps, _ = _shifted(p_ref[...], pp_ref[7:8, :], pl.program_id(0) == 0, mu_ref[...])
        f = _rwkv_features(ps, rw, w0_ref[...], a0_ref[...], kk_ref[...], ka_ref[...], wd_ref[...], wi_ref[...])
        r_ref[...] = f["r"]
        lw_ref[...] = f["logw"]
        km_ref[...] = f["kmod"]
        v_ref[...] = f["v"]
        a_ref[...] = -f["kk"]
        b_ref[...] = f["kk"] * f["alpha"]

    vec = lambda n: pl.BlockSpec((1, n), lambda i: (0, 0))
    out = pl.BlockSpec((tm, rw), lambda i: (i, 0))
    return pl.pallas_call(
        body, name="rwkv_pre_fwd", grid=(t // tm,),
        in_specs=[pl.BlockSpec((tm, sh), lambda i: (i, 0)),
                  pl.BlockSpec((8, sh), lambda i: (jnp.maximum(i * (tm // 8) - 1, 0), 0)),
                  vec(sh), vec(rw), vec(rw), vec(rw), vec(rw),
                  pl.BlockSpec((LORA, rw), lambda i: (0, 0)), pl.BlockSpec((LORA, rw), lambda i: (0, 0))],
        out_specs=[out] * 6,
        out_shape=[jax.ShapeDtypeStruct((t, rw), F32)] * 6,
        compiler_params=_params("parallel"),
    )(p, p, mu, w0, a0, k_k, k_a, wd, wi)


def _rwkv_pre_bwd(p, c, mu, w0, a0, k_k, k_a, wd, wi, dr, dlw, dkm, dv, da, db, dr2, dkm2, dv2):
    t = p.shape[0]
    tm = _tile(t, 128, 8)
    rw, sh = c.rw, c.shift

    def body(p_ref, pp_ref, mu_ref, w0_ref, a0_ref, kk_ref, ka_ref, wd_ref, wi_ref,
             dr_ref, dlw_ref, dkm_ref, dv_ref, da_ref, db_ref, dr2_ref, dkm2_ref, dv2_ref,
             dps_ref, dzw_ref, dza_ref, tw_ref, al_ref, dw0_ref, da0_ref, dkk_ref, dka_ref):
        @pl.when(pl.program_id(0) == 0)
        def _():
            for ref in (dw0_ref, da0_ref, dkk_ref, dka_ref):
                ref[...] = jnp.zeros_like(ref)

        ps, _ = _shifted(p_ref[...], pp_ref[7:8, :], pl.program_id(0) == 0, mu_ref[...])
        k_k, k_a = kk_ref[...], ka_ref[...]
        f = _rwkv_features(ps, rw, w0_ref[...], a0_ref[...], k_k, k_a, wd_ref[...], wi_ref[...])
        alpha, kk, k = f["alpha"], f["kk"], f["k"]
        dkm = dkm_ref[...] + dkm2_ref[...]
        db = db_ref[...]
        dkk = db * alpha - da_ref[...]
        dalpha = db * kk + dkm * k * k_a
        dk = dkm * (1.0 + (alpha - 1.0) * k_a)
        dka_ref[...] += jnp.sum(dkm * k * (alpha - 1.0), axis=0, keepdims=True)
        dkkr = f["rn"] * jnp.where(f["n2"] > 1e-24, dkk - kk * _head_sum(dkk * kk), dkk)
        dk = dk + dkkr * k_k
        dkk_ref[...] += jnp.sum(dkkr * k, axis=0, keepdims=True)
        dza = dalpha * alpha * (1.0 - alpha)
        da0_ref[...] += jnp.sum(dza, axis=0, keepdims=True)
        dzw = dlw_ref[...] * f["logw"] * _sig(-f["zw"])
        dw0_ref[...] += jnp.sum(dzw, axis=0, keepdims=True)
        dza_b, dzw_b = dza.astype(BF16), dzw.astype(BF16)
        dal = _mm_nt(dza_b, wi_ref[...])
        dwl = _mm_nt(dzw_b, wd_ref[...]) * (1.0 - f["tw"] * f["tw"])
        dps_ref[:, 0:rw] = dr_ref[...] + dr2_ref[...]
        dps_ref[:, rw:2 * rw] = dk
        dps_ref[:, 2 * rw:3 * rw] = dv_ref[...] + dv2_ref[...]
        dps_ref[:, 3 * rw:3 * rw + LORA] = dwl
        dps_ref[:, 3 * rw + LORA:sh] = dal
        dzw_ref[...] = dzw_b
        dza_ref[...] = dza_b
        tw_ref[...] = f["tw"].astype(BF16)
        al_ref[...] = f["al"].astype(BF16)

    vec = lambda n: pl.BlockSpec((1, n), lambda i: (0, 0))
    blk = lambda n: pl.BlockSpec((tm, n), lambda i: (i, 0))
    return pl.pallas_call(
        body, name="rwkv_pre_bwd", grid=(t // tm,),
        in_specs=[blk(sh), pl.BlockSpec((8, sh), lambda i: (jnp.maximum(i * (tm // 8) - 1, 0), 0)),
                  vec(sh), vec(rw), vec(rw), vec(rw), vec(rw),
                  pl.BlockSpec((LORA, rw), lambda i: (0, 0)), pl.BlockSpec((LORA, rw), lambda i: (0, 0))]
                 + [blk(rw)] * 9,
        out_specs=[blk(sh), blk(rw), blk(rw), blk(LORA), blk(LORA), vec(rw), vec(rw), vec(rw), vec(rw)],
        out_shape=[jax.ShapeDtypeStruct((t, sh), F32), jax.ShapeDtypeStruct((t, rw), BF16),
                   jax.ShapeDtypeStruct((t, rw), BF16), jax.ShapeDtypeStruct((t, LORA), BF16),
                   jax.ShapeDtypeStruct((t, LORA), BF16)] + [jax.ShapeDtypeStruct((1, rw), F32)] * 4,
        compiler_params=_params("arbitrary"),
    )(p, p, mu, w0, a0, k_k, k_a, wd, wi, dr, dlw, dkm, dv, da, db, dr2, dkm2, dv2)


def _shift_bwd(dps, p, c, mu):
    t = p.shape[0]
    tm = _tile(t, 256, 8)
    sh = c.shift
    nt = t // tm

    def body(d_ref, dn_ref, p_ref, pp_ref, mu_ref, dp_ref, dmu_ref):
        i = pl.program_id(0)

        @pl.when(i == 0)
        def _():
            dmu_ref[...] = jnp.zeros_like(dmu_ref)

        mu = mu_ref[...]
        d = d_ref[...]
        pc = p_ref[...]
        _, prev = _shifted(pc, pp_ref[7:8, :], i == 0, mu)
        dmu_ref[...] += jnp.sum(d * (prev - pc), axis=0, keepdims=True)
        nxt_row = jnp.where(i == nt - 1, 0.0, dn_ref[0:1, :])
        last = lax.broadcasted_iota(jnp.int32, d.shape, 0) == tm - 1
        nxt = jnp.where(last, nxt_row, pltpu.roll(d, tm - 1, 0))
        dp_ref[...] = (d * (1.0 - mu) + nxt * mu).astype(BF16)

    blk = pl.BlockSpec((tm, sh), lambda i: (i, 0))
    return pl.pallas_call(
        body, name="shift_bwd", grid=(nt,),
        in_specs=[blk, pl.BlockSpec((8, sh), lambda i: (jnp.minimum((i + 1) * (tm // 8), t // 8 - 1), 0)),
                  blk, pl.BlockSpec((8, sh), lambda i: (jnp.maximum(i * (tm // 8) - 1, 0), 0)),
                  pl.BlockSpec((1, sh), lambda i: (0, 0))],
        out_specs=[blk, pl.BlockSpec((1, sh), lambda i: (0, 0))],
        out_shape=[jax.ShapeDtypeStruct((t, sh), BF16), jax.ShapeDtypeStruct((1, sh), F32)],
        compiler_params=_params("arbitrary"),
    )(dps, dps, p, p, mu)


def _tri(n, strict):
    ri = lax.broadcasted_iota(jnp.int32, (n, n), 0)
    ci = lax.broadcasted_iota(jnp.int32, (n, n), 1)
    return (ri > ci) if strict else (ri >= ci)


def _unit_lower_inverse(a):
    n = a.shape[-1]
    ri = lax.broadcasted_iota(jnp.int32, (n, n), 0)
    ci = lax.broadcasted_iota(jnp.int32, (n, n), 1)
    eye = (ri == ci).astype(F32)
    blk = lambda s: (ri // s) == (ci // s)
    ad = jnp.where(blk(16), a, 0.0)
    p = eye + ad
    for _ in range(3):
        ad = _bmm(ad, ad, P_SOLVE)
        p = p + _bmm(p, ad, P_SOLVE)
    s = 16
    while s < n:
        off = jnp.where(blk(2 * s) & ~blk(s), a, 0.0)
        p = p + _bmm(_bmm(p, off, P_SOLVE), p, P_SOLVE)
        s *= 2
    return p


P_SOLVE, P_STATE, P_OUT, P_GRAD, P_DECAY = 1, 3, 1, 1, 3


def _chunk_common(r, lw, k, a, b):
    n = r.shape[1]
    tri_incl = jnp.broadcast_to(_tri(n, False).astype(BF16), (r.shape[0], n, n))
    cum = _bmm_01(tri_incl, lw)
    e_pos, e_neg, e_exc = jnp.exp(cum), jnp.exp(-cum), jnp.exp(cum - lw)
    last = lax.broadcasted_iota(jnp.int32, (n, r.shape[2]), 0) == n - 1
    g_last = jnp.exp(jnp.sum(jnp.where(last, cum, 0.0), axis=1, keepdims=True))
    return g_last, r * e_pos, a * e_exc, b * e_neg, k * e_neg, e_pos, e_neg, e_exc


def _chunk_solve(rt, at, bt, kt, v, g0):
    strict, incl = _tri(rt.shape[1], True), _tri(rt.shape[1], False)
    a_ab = jnp.where(strict, _bmm_nt(at, bt, P_SOLVE), 0.0)
    a_ak = jnp.where(strict, _bmm_nt(at, kt, P_SOLVE), 0.0)
    a_rb = jnp.where(incl, _bmm_nt(rt, bt, P_OUT), 0.0)
    a_rk = jnp.where(incl, _bmm_nt(rt, kt, P_OUT), 0.0)
    tinv = _unit_lower_inverse(a_ab)
    u = _bmm(tinv, _bmm(at, g0, P_SOLVE) + _bmm(a_ak, v, P_SOLVE), P_SOLVE)
    return a_ab, a_ak, a_rb, a_rk, tinv, u


def _diag_col(row, n):
    ri = lax.broadcasted_iota(jnp.int32, (n, n), 0)
    ci = lax.broadcasted_iota(jnp.int32, (n, n), 1)
    return jnp.sum(jnp.where(ri == ci, row, 0.0), axis=2, keepdims=True)


def _diag_row(col, n):
    ri = lax.broadcasted_iota(jnp.int32, (n, n), 0)
    ci = lax.broadcasted_iota(jnp.int32, (n, n), 1)
    return jnp.sum(jnp.where(ri == ci, col, 0.0), axis=1, keepdims=True)


def _rwkv_scan_fwd(r, lw, k, v, a, b, hb):
    h, t, n = r.shape
    nc = t // CHUNK

    def body(r_ref, lw_ref, k_ref, v_ref, a_ref, b_ref, y_ref, st_ref, g_sc):
        @pl.when(pl.program_id(1) == 0)
        def _():
            g_sc[...] = jnp.zeros_like(g_sc)

        g0 = g_sc[...]
        st_ref[0] = g0
        vv = v_ref[...]
        g_last, rt, at, bt, kt, _, _, _ = _chunk_common(r_ref[...], lw_ref[...], k_ref[...], a_ref[...], b_ref[...])
        _, _, a_rb, a_rk, _, u = _chunk_solve(rt, at, bt, kt, vv, g0)
        y_ref[...] = _bmm(rt, g0, P_OUT) + _bmm(a_rb, u, P_OUT) + _bmm(a_rk, vv, P_OUT)
        z = g0 + _bmm_tn(bt, u, P_STATE) + _bmm_tn(kt, vv, P_STATE)
        g_sc[...] = _diag_col(g_last, n) * z

    blk = pl.BlockSpec((hb, CHUNK, n), lambda i, j: (i, j, 0))
    return pl.pallas_call(
        body, name="rwkv_scan_fwd", grid=(h // hb, nc),
        in_specs=[blk] * 6,
        out_specs=[blk, pl.BlockSpec((1, hb, n, n), lambda i, j: (j, i, 0, 0))],
        out_shape=[jax.ShapeDtypeStruct((h, t, n), F32), jax.ShapeDtypeStruct((nc, h, n, n), F32)],
        scratch_shapes=[pltpu.VMEM((hb, n, n), F32)],
        compiler_params=_params("parallel", "arbitrary"),
    )(r, lw, k, v, a, b)


def _rwkv_scan_bwd(r, lw, k, v, a, b, states, dy, hb):
    h, t, n = r.shape
    nc = t // CHUNK

    def body(r_ref, lw_ref, k_ref, v_ref, a_ref, b_ref, st_ref, dy_ref,
             dr_ref, dlw_ref, dk_ref, dv_ref, da_ref, db_ref, dg_sc):
        @pl.when(pl.program_id(1) == 0)
        def _():
            dg_sc[...] = jnp.zeros_like(dg_sc)

        g0 = st_ref[0]
        vv, dyv, dh = v_ref[...], dy_ref[...], dg_sc[...]
        lwv = lw_ref[...]
        g_last, rt, at, bt, kt, e_pos, e_neg, e_exc = _chunk_common(r_ref[...], lwv, k_ref[...], a_ref[...], b_ref[...])
        a_ab, a_ak, a_rb, a_rk, tinv, u = _chunk_solve(rt, at, bt, kt, vv, g0)
        strict, incl = _tri(CHUNK, True), _tri(CHUNK, False)
        gcol = _diag_col(g_last, n)
        z = g0 + _bmm_tn(bt, u, P_STATE) + _bmm_tn(kt, vv, P_STATE)
        dz = gcol * dh
        dc_last = _diag_row(jnp.sum(dh * gcol * z, axis=2, keepdims=True), n)
        g = P_GRAD
        du = _bmm_tn(a_rb, dyv, g) + _bmm(bt, dz, g)
        dx = _bmm_tn(tinv, du, P_SOLVE)
        dv_ref[...] = _bmm_tn(a_rk, dyv, g) + _bmm(kt, dz, g) + _bmm_tn(a_ak, dx, g)
        da_ab = jnp.where(strict, _bmm_nt(dx, u, g), 0.0)
        da_ak = jnp.where(strict, _bmm_nt(dx, vv, g), 0.0)
        da_rb = jnp.where(incl, _bmm_nt(dyv, u, g), 0.0)
        da_rk = jnp.where(incl, _bmm_nt(dyv, vv, g), 0.0)
        g = P_DECAY
        d_at = _bmm(da_ab, bt, g) + _bmm(da_ak, kt, g) + _bmm_nt(dx, g0, g)
        d_rt = _bmm(da_rb, bt, g) + _bmm(da_rk, kt, g) + _bmm_nt(dyv, g0, g)
        d_bt = _bmm_tn(da_ab, at, g) + _bmm_tn(da_rb, rt, g) + _bmm_nt(u, dz, g)
        d_kt = _bmm_tn(da_ak, at, g) + _bmm_tn(da_rk, rt, g) + _bmm_nt(vv, dz, g)
        dg_sc[...] = dz + _bmm_tn(rt, dyv, P_STATE) + _bmm_tn(at, dx, P_STATE)
        dr_ref[...] = d_rt * e_pos
        da_ref[...] = d_at * e_exc
        db_ref[...] = d_bt * e_neg
        dk_ref[...] = d_kt * e_neg
        last = lax.broadcasted_iota(jnp.int32, (CHUNK, n), 0) == CHUNK - 1
        dc = d_rt * rt - d_bt * bt - d_kt * kt + jnp.where(last, dc_last, 0.0)
        dce = d_at * at
        ri = lax.broadcasted_iota(jnp.int32, (CHUNK, CHUNK), 0)
        ci = lax.broadcasted_iota(jnp.int32, (CHUNK, CHUNK), 1)
        up_incl = jnp.broadcast_to((ri <= ci).astype(BF16), (hb, CHUNK, CHUNK))
        dlw_ref[...] = _bmm_01(up_incl, dc + dce) - dce

    rev = lambda i, j: (i, nc - 1 - j, 0)
    blk = pl.BlockSpec((hb, CHUNK, n), rev)
    return pl.pallas_call(
        body, name="rwkv_scan_bwd", grid=(h // hb, nc),
        in_specs=[blk] * 6 + [pl.BlockSpec((1, hb, n, n), lambda i, j: (nc - 1 - j, i, 0, 0)), blk],
        out_specs=[blk] * 6,
        out_shape=[jax.ShapeDtypeStruct((h, t, n), F32)] * 6,
        scratch_shapes=[pltpu.VMEM((hb, n, n), F32)],
        compiler_params=_params("parallel", "arbitrary"),
    )(r, lw, k, v, a, b, states, dy)


def _silu_grad(g):
    s = _sig(g)
    return s * (1.0 + g * (1.0 - s))


def _group_norm(ys):
    yc = ys - _head_sum(ys) * (1.0 / HEAD_DIM)
    rstd = lax.rsqrt(_head_sum(yc * yc) * (1.0 / HEAD_DIM) + GN_EPS)
    return yc * rstd, rstd


def _rwkv_post_fwd(ys, r, km, v, p, c, ln_w, ln_b, r_k):
    t = ys.shape[0]
    tm = _tile(t, 512, 8)
    goff = c.o_grw // LANES

    def body(ys_ref, r_ref, km_ref, v_ref, g_ref, lw_ref, lb_ref, rk_ref, o_ref):
        yn, _ = _group_norm(ys_ref[...])
        s = _head_sum(r_ref[...] * km_ref[...] * rk_ref[...])
        g = g_ref[...]
        o_ref[...] = ((yn * lw_ref[...] + lb_ref[...] + s * v_ref[...]) * g * _sig(g)).astype(BF16)

    blk = pl.BlockSpec((tm, LANES), lambda i, j: (i, j))
    vec = pl.BlockSpec((1, LANES), lambda i, j: (0, j))
    return pl.pallas_call(
        body, name="rwkv_post_fwd", grid=(t // tm, c.rw // LANES),
        in_specs=[blk] * 4 + [pl.BlockSpec((tm, LANES), lambda i, j: (i, goff + j)), vec, vec, vec],
        out_specs=blk, out_shape=jax.ShapeDtypeStruct((t, c.rw), BF16),
        compiler_params=_params("parallel", "parallel"),
    )(ys, r, km, v, p, ln_w, ln_b, r_k)


def _rwkv_post_bwd(dyc, ys, r, km, v, p, c, ln_w, ln_b, r_k):
    t = ys.shape[0]
    tm = _tile(t, 512, 8)
    goff = c.o_grw // LANES

    def body(dy_ref, ys_ref, r_ref, km_ref, v_ref, g_ref, lw_ref, lb_ref, rk_ref,
             dys_ref, dr_ref, dkm_ref, dv_ref, dg_ref, dlw_ref, dlb_ref, drk_ref):
        @pl.when(pl.program_id(1) == 0)
        def _():
            for ref in (dlw_ref, dlb_ref, drk_ref):
                ref[...] = jnp.zeros_like(ref)

        yn, rstd = _group_norm(ys_ref[...])
        rv, kmv, vv, rk, g = r_ref[...], km_ref[...], v_ref[...], rk_ref[...], g_ref[...]
        s = _head_sum(rv * kmv * rk)
        y = yn * lw_ref[...] + lb_ref[...] + s * vv
        dyc = dy_ref[...]
        dg_ref[...] = (dyc * y * _silu_grad(g)).astype(BF16)
        dy = dyc * g * _sig(g)
        dlb_ref[...] += jnp.sum(dy, axis=0, keepdims=True)
        dlw_ref[...] += jnp.sum(dy * yn, axis=0, keepdims=True)
        dyn = dy * lw_ref[...]
        inv = 1.0 / HEAD_DIM
        dys_ref[...] = rstd * (dyn - _head_sum(dyn) * inv - yn * _head_sum(dyn * yn) * inv)
        ds = _head_sum(dy * vv)
        dv_ref[...] = dy * s
        dr_ref[...] = ds * kmv * rk
        dkm_ref[...] = ds * rv * rk
        drk_ref[...] += jnp.sum(ds * rv * kmv, axis=0, keepdims=True)

    blk = pl.BlockSpec((tm, LANES), lambda j, i: (i, j))
    vec = pl.BlockSpec((1, LANES), lambda j, i: (0, j))
    f = jax.ShapeDtypeStruct((t, c.rw), F32)
    s1 = jax.ShapeDtypeStruct((1, c.rw), F32)
    return pl.pallas_call(
        body, name="rwkv_post_bwd", grid=(c.rw // LANES, t // tm),
        in_specs=[blk] * 5 + [pl.BlockSpec((tm, LANES), lambda j, i: (i, goff + j)), vec, vec, vec],
        out_specs=[blk] * 5 + [vec] * 3,
        out_shape=[f, f, f, f, jax.ShapeDtypeStruct((t, c.rw), BF16), s1, s1, s1],
        compiler_params=_params("parallel", "arbitrary"),
    )(dyc, ys, r, km, v, p, ln_w, ln_b, r_k)


def _gate_fwd(y, p, goff, name):
    t, w = y.shape
    tm = _tile(t, 512, 8)
    gb = goff // LANES

    def body(y_ref, g_ref, o_ref):
        g = g_ref[...]
        o_ref[...] = (y_ref[...] * g * _sig(g)).astype(BF16)

    blk = pl.BlockSpec((tm, LANES), lambda i, j: (i, j))
    return pl.pallas_call(
        body, name=name, grid=(t // tm, w // LANES),
        in_specs=[blk, pl.BlockSpec((tm, LANES), lambda i, j: (i, gb + j))],
        out_specs=blk, out_shape=jax.ShapeDtypeStruct((t, w), BF16),
        compiler_params=_params("parallel", "parallel"),
    )(y, p)


def _gate_bwd(dyc, yoff, y, p, goff, name):
    t, w = y.shape
    tm = _tile(t, 512, 8)
    gb, yb = goff // LANES, yoff // LANES

    def body(d_ref, y_ref, g_ref, dy_ref, dg_ref):
        g, d = g_ref[...], d_ref[...]
        dy_ref[...] = d * g * _sig(g)
        dg_ref[...] = (d * y_ref[...] * _silu_grad(g)).astype(BF16)

    blk = pl.BlockSpec((tm, LANES), lambda i, j: (i, j))
    return pl.pallas_call(
        body, name=name, grid=(t // tm, w // LANES),
        in_specs=[pl.BlockSpec((tm, LANES), lambda i, j: (i, yb + j)), blk,
                  pl.BlockSpec((tm, LANES), lambda i, j: (i, gb + j))],
        out_specs=[blk, blk],
        out_shape=[jax.ShapeDtypeStruct((t, w), F32), jax.ShapeDtypeStruct((t, w), BF16)],
        compiler_params=_params("parallel", "parallel"),
    )(dyc, y, p)


NEG = -1e30


def _fox_prep(p, c, b_f):
    t = p.shape[0]
    tm = _tile(t, 512, 8)
    fb = c.o_fl // LANES

    def body(f_ref, b_ref, o_ref, carry):
        @pl.when(pl.program_id(0) == 0)
        def _():
            carry[...] = jnp.zeros_like(carry)

        logf = -_softplus(-(f_ref[...] + b_ref[...]))
        cum = _mm(_tri(tm, False).astype(F32), logf, HI) + carry[...]
        o_ref[...] = cum
        carry[...] += jnp.sum(logf, axis=0, keepdims=True)

    return pl.pallas_call(
        body, name="fox_prep", grid=(t // tm,),
        in_specs=[pl.BlockSpec((tm, LANES), lambda i: (i, fb)), pl.BlockSpec((1, LANES), lambda i: (0, 0))],
        out_specs=pl.BlockSpec((tm, LANES), lambda i: (i, 0)),
        out_shape=jax.ShapeDtypeStruct((t, LANES), F32),
        scratch_shapes=[pltpu.VMEM((1, LANES), F32)],
        compiler_params=_params("arbitrary"),
    )(p, b_f)


def _fox_logit_bwd(dcum, p, c, b_f):
    t = p.shape[0]
    tm = _tile(t, 512, 8)
    fb = c.o_fl // LANES
    nt = t // tm

    def body(d_ref, f_ref, b_ref, o_ref, db_ref, carry):
        @pl.when(pl.program_id(0) == 0)
        def _():
            carry[...] = jnp.zeros_like(carry)
            db_ref[...] = jnp.zeros_like(db_ref)

        d = d_ref[0] + d_ref[1]
        dlogf = _mm(_tri(tm, False).astype(F32).T, d, HI) + carry[...]
        carry[...] += jnp.sum(d, axis=0, keepdims=True)
        df = dlogf * _sig(-(f_ref[...] + b_ref[...]))
        o_ref[...] = df.astype(BF16)
        db_ref[...] += jnp.sum(df, axis=0, keepdims=True)

    return pl.pallas_call(
        body, name="fox_logit_bwd", grid=(nt,),
        in_specs=[pl.BlockSpec((2, tm, LANES), lambda i: (0, nt - 1 - i, 0)),
                  pl.BlockSpec((tm, LANES), lambda i: (nt - 1 - i, fb)),
                  pl.BlockSpec((1, LANES), lambda i: (0, 0))],
        out_specs=[pl.BlockSpec((tm, LANES), lambda i: (nt - 1 - i, 0)), pl.BlockSpec((1, LANES), lambda i: (0, 0))],
        out_shape=[jax.ShapeDtypeStruct((t, LANES), BF16), jax.ShapeDtypeStruct((1, LANES), F32)],
        scratch_shapes=[pltpu.VMEM((1, LANES), F32)],
        compiler_params=_params("arbitrary"),
    )(dcum, p, b_f)


def _fox_scores(q, k, cq, ck, qi, ki, tq, tk):
    s = _mm_nt((q * (HEAD_DIM ** -0.5)).astype(BF16), k.astype(BF16)) + cq - ck
    qpos = qi * tq + lax.broadcasted_iota(jnp.int32, (tq, tk), 0)
    kpos = ki * tk + lax.broadcasted_iota(jnp.int32, (tq, tk), 1)
    mask = kpos <= qpos
    return jnp.where(mask, s, NEG), mask


def _fox_fwd(q, k, v, cq, ck, hb, tb):
    h, t, n = q.shape
    tq = tk = _tile(t, tb, LANES)
    nq = t // tq

    def body(q_ref, k_ref, v_ref, cq_ref, ck_ref, o_ref, lse_ref, m_sc, l_sc, acc_sc):
        qi, ki = pl.program_id(1), pl.program_id(2)

        @pl.when(ki == 0)
        def _():
            m_sc[...] = jnp.full_like(m_sc, NEG)
            l_sc[...] = jnp.zeros_like(l_sc)
            acc_sc[...] = jnp.zeros_like(acc_sc)

        @pl.when(ki <= qi)
        def _():
            for i in range(hb):
                s, _ = _fox_scores(q_ref[i], k_ref[i], cq_ref[i], ck_ref[i], qi, ki, tq, tk)
                m_old = m_sc[i]
                m_new = jnp.maximum(m_old, jnp.max(s, axis=1, keepdims=True))
                scale = jnp.exp(m_old - m_new)
                e = jnp.exp(s - m_new)
                l_sc[i] = scale * l_sc[i] + jnp.sum(e, axis=1, keepdims=True)
                acc_sc[i] = scale * acc_sc[i] + _mm(e.astype(BF16), v_ref[i].astype(BF16))
                m_sc[i] = m_new

        @pl.when(ki == qi)
        def _():
            o_ref[...] = acc_sc[...] / l_sc[...]
            lse_ref[...] = m_sc[...] + jnp.log(l_sc[...])

    qb = pl.BlockSpec((hb, tq, n), lambda g, i, j: (g, i, 0))
    kb = pl.BlockSpec((hb, tk, n), lambda g, i, j: (g, jnp.minimum(i, j), 0))
    col = pl.BlockSpec((hb, tq, 1), lambda g, i, j: (g, i, 0))
    return pl.pallas_call(
        body, name="fox_fwd", grid=(h // hb, nq, nq),
        in_specs=[qb, kb, kb, col, pl.BlockSpec((hb, 1, tk), lambda g, i, j: (g, 0, jnp.minimum(i, j)))],
        out_specs=[qb, col],
        out_shape=[jax.ShapeDtypeStruct((h, t, n), F32), jax.ShapeDtypeStruct((h, t, 1), F32)],
        scratch_shapes=[pltpu.VMEM((hb, tq, 1), F32), pltpu.VMEM((hb, tq, 1), F32), pltpu.VMEM((hb, tq, n), F32)],
        compiler_params=_params("parallel", "parallel", "arbitrary"),
    )(q, k, v, cq, ck)


def _fox_bwd_dq(q, k, v, cq, ck, lse, o, do, hb, tb):
    h, t, n = q.shape
    tq = tk = _tile(t, tb, LANES)
    nq = t // tq

    def body(q_ref, k_ref, v_ref, cq_ref, ck_ref, lse_ref, o_ref, do_ref, dq_ref, dcq_ref, acc_sc, row_sc):
        qi, ki = pl.program_id(1), pl.program_id(2)

        @pl.when(ki == 0)
        def _():
            acc_sc[...] = jnp.zeros_like(acc_sc)
            row_sc[...] = jnp.zeros_like(row_sc)

        @pl.when(ki <= qi)
        def _():
            for i in range(hb):
                s, mask = _fox_scores(q_ref[i], k_ref[i], cq_ref[i], ck_ref[i], qi, ki, tq, tk)
                dov = do_ref[i]
                delta = jnp.sum(dov * o_ref[i], axis=1, keepdims=True)
                pm = jnp.where(mask, jnp.exp(s - lse_ref[i]), 0.0)
                dp = _mm_nt(dov.astype(BF16), v_ref[i].astype(BF16))
                ds = pm * (dp - delta)
                acc_sc[i] += _mm(ds.astype(BF16), k_ref[i].astype(BF16))
                row_sc[i] += jnp.sum(ds, axis=1, keepdims=True)

        @pl.when(ki == qi)
        def _():
            dq_ref[...] = acc_sc[...] * (HEAD_DIM ** -0.5)
            dcq_ref[...] = row_sc[...]

    qb = pl.BlockSpec((hb, tq, n), lambda g, i, j: (g, i, 0))
    kb = pl.BlockSpec((hb, tk, n), lambda g, i, j: (g, jnp.minimum(i, j), 0))
    col = pl.BlockSpec((hb, tq, 1), lambda g, i, j: (g, i, 0))
    return pl.pallas_call(
        body, name="fox_bwd_dq", grid=(h // hb, nq, nq),
        in_specs=[qb, kb, kb, col, pl.BlockSpec((hb, 1, tk), lambda g, i, j: (g, 0, jnp.minimum(i, j))), col, qb, qb],
        out_specs=[qb, col],
        out_shape=[jax.ShapeDtypeStruct((h, t, n), F32), jax.ShapeDtypeStruct((h, t, 1), F32)],
        scratch_shapes=[pltpu.VMEM((hb, tq, n), F32), pltpu.VMEM((hb, tq, 1), F32)],
        compiler_params=_params("parallel", "parallel", "arbitrary"),
    )(q, k, v, cq, ck, lse, o, do)


def _fox_bwd_dkv(q, k, v, cq, ck, lse, o, do, hb, tb):
    h, t, n = q.shape
    tq = tk = _tile(t, tb, LANES)
    nq = t // tq

    def body(q_ref, k_ref, v_ref, cq_ref, ck_ref, lse_ref, o_ref, do_ref, dk_ref, dv_ref, dck_ref, dk_sc, dv_sc, dc_sc):
        ki, qi = pl.program_id(1), pl.program_id(2)

        @pl.when(qi == 0)
        def _():
            dk_sc[...] = jnp.zeros_like(dk_sc)
            dv_sc[...] = jnp.zeros_like(dv_sc)
            dc_sc[...] = jnp.zeros_like(dc_sc)

        @pl.when(qi >= ki)
        def _():
            for i in range(hb):
                s, mask = _fox_scores(q_ref[i], k_ref[i], cq_ref[i], ck_ref[i], qi, ki, tq, tk)
                dov = do_ref[i]
                delta = jnp.sum(dov * o_ref[i], axis=1, keepdims=True)
                pm = jnp.where(mask, jnp.exp(s - lse_ref[i]), 0.0)
                dob = dov.astype(BF16)
                dp = _mm_nt(dob, v_ref[i].astype(BF16))
                ds = pm * (dp - delta)
                dv_sc[i] += _mm_tn(pm.astype(BF16), dob)
                dk_sc[i] += _mm_tn(ds.astype(BF16), q_ref[i].astype(BF16))
                dc_sc[i] -= jnp.sum(ds, axis=0, keepdims=True)

        @pl.when(qi == nq - 1)
        def _():
            dk_ref[...] = dk_sc[...] * (HEAD_DIM ** -0.5)
            dv_ref[...] = dv_sc[...]
            dck_ref[...] = dc_sc[...]

    qb = pl.BlockSpec((hb, tq, n), lambda g, j, i: (g, jnp.maximum(i, j), 0))
    kb = pl.BlockSpec((hb, tk, n), lambda g, j, i: (g, j, 0))
    col = pl.BlockSpec((hb, tq, 1), lambda g, j, i: (g, jnp.maximum(i, j), 0))
    row = pl.BlockSpec((hb, 1, tk), lambda g, j, i: (g, 0, j))
    return pl.pallas_call(
        body, name="fox_bwd_dkv", grid=(h // hb, nq, nq),
        in_specs=[qb, kb, kb, col, row, col, qb, qb],
        out_specs=[kb, kb, row],
        out_shape=[jax.ShapeDtypeStruct((h, t, n), F32), jax.ShapeDtypeStruct((h, t, n), F32),
                   jax.ShapeDtypeStruct((h, 1, t), F32)],
        scratch_shapes=[pltpu.VMEM((hb, tk, n), F32), pltpu.VMEM((hb, tk, n), F32), pltpu.VMEM((hb, 1, tk), F32)],
        compiler_params=_params("parallel", "parallel", "arbitrary"),
    )(q, k, v, cq, ck, lse, o, do)


FOX_PAIRS = 2
FOX_HEADS_STEP = 2 * FOX_PAIRS


def _lane_half(shape, upper):
    li = lax.broadcasted_iota(jnp.int32, shape, len(shape) - 1)
    return (li >= HEAD_DIM) if upper else (li < HEAD_DIM)


def _col(block, j):
    li = lax.broadcasted_iota(jnp.int32, block.shape, 1)
    return jnp.sum(jnp.where(li == j, block, 0.0), axis=1, keepdims=True)


def _from_cols(cols):
    li = lax.broadcasted_iota(jnp.int32, (cols[0].shape[0], len(cols)), 1)
    out = jnp.zeros(li.shape, F32)
    for j, cj in enumerate(cols):
        out = jnp.where(li == j, cj, out)
    return out


def _from_rows(rows):
    si = lax.broadcasted_iota(jnp.int32, (len(rows), rows[0].shape[1]), 0)
    out = jnp.zeros(si.shape, F32)
    for j, rj in enumerate(rows):
        out = jnp.where(si == j, rj, out)
    return out


def _causal(tq, tk):
    return lax.broadcasted_iota(jnp.int32, (tq, tk), 1) <= lax.broadcasted_iota(jnp.int32, (tq, tk), 0)


def _fox_prep_t(p, c, b_f):
    t = p.shape[0]
    tm = _tile(t, 512, LANES)
    fb = c.o_fl // LANES

    def body(f_ref, b_ref, o_ref, carry):
        @pl.when(pl.program_id(0) == 0)
        def _():
            carry[...] = jnp.zeros_like(carry)

        logf = -_softplus(-(f_ref[...] + b_ref[...]))
        cum = _mm(_tri(tm, False).astype(F32), logf, HI) + carry[...]
        o_ref[...] = cum.T
        carry[...] += jnp.sum(logf, axis=0, keepdims=True)

    return pl.pallas_call(
        body, name="fox_prep", grid=(t // tm,),
        in_specs=[pl.BlockSpec((tm, LANES), lambda i: (i, fb)), pl.BlockSpec((1, LANES), lambda i: (0, 0))],
        out_specs=pl.BlockSpec((LANES, tm), lambda i: (0, i)),
        out_shape=jax.ShapeDtypeStruct((LANES, t), F32),
        scratch_shapes=[pltpu.VMEM((1, LANES), F32)],
        compiler_params=_params("arbitrary"),
    )(p, b_f)


def _fox2_fwd(p, c, cum_t, tb):
    t = p.shape[0]
    tq = tk = _tile(t, tb, LANES)
    nq = t // tq
    pw, nh = FOX_PAIRS * LANES, FOX_HEADS_STEP
    qb, kb, vb, gb = (o // pw for o in (c.o_fq, c.o_fk, c.o_fv, c.o_gfox))
    scale = HEAD_DIM ** -0.5

    def body(q_ref, k_ref, v_ref, g_ref, ck_ref, o_ref, y_ref, lse_ref, m_sc, l_sc, acc_sc):
        g, qi, ki = pl.program_id(0), pl.program_id(1), pl.program_id(2)

        @pl.when(ki == 0)
        def _():
            m_sc[...] = jnp.full_like(m_sc, NEG)
            l_sc[...] = jnp.zeros_like(l_sc)
            acc_sc[...] = jnp.zeros_like(acc_sc)

        def step(diag):
            for pi in range(FOX_PAIRS):
                lanes = slice(pi * LANES, (pi + 1) * LANES)
                q2 = (q_ref[:, lanes] * scale).astype(BF16)
                k2, v2 = k_ref[:, lanes].astype(BF16), v_ref[:, lanes].astype(BF16)
                acc = acc_sc[:, lanes]
                new_acc = acc
                for hh in range(2):
                    hi = 2 * pi + hh
                    mk = _lane_half((tq, LANES), hh == 1)
                    s = _mm_nt(jnp.where(mk, q2, jnp.zeros_like(q2)), k2) - ck_ref[pl.ds(g * nh + hi, 1), :]
                    if diag:
                        s = jnp.where(_causal(tq, tk), s, NEG)
                    m_old = m_sc[hi]
                    m_new = jnp.maximum(m_old, jnp.max(s, axis=1, keepdims=True))
                    a = jnp.exp(m_old - m_new)
                    e = jnp.exp(s - m_new)
                    l_sc[hi] = a * l_sc[hi] + jnp.sum(e, axis=1, keepdims=True)
                    m_sc[hi] = m_new
                    new_acc = jnp.where(mk, a * acc + _mm(e.astype(BF16), v2), new_acc)
                acc_sc[:, lanes] = new_acc

        @pl.when(ki < qi)
        def _():
            step(False)

        @pl.when(ki == qi)
        def _():
            step(True)
            for pi in range(FOX_PAIRS):
                lanes = slice(pi * LANES, (pi + 1) * LANES)
                inv = jnp.where(_lane_half((tq, LANES), False), 1.0 / l_sc[2 * pi], 1.0 / l_sc[2 * pi + 1])
                o = acc_sc[:, lanes] * inv
                gate = g_ref[:, lanes]
                o_ref[:, lanes] = o
                y_ref[:, lanes] = (o * gate * _sig(gate)).astype(BF16)
            lse_ref[0] = _from_cols([m_sc[h] + jnp.log(l_sc[h]) for h in range(nh)])

    row = lambda off: pl.BlockSpec((tq, pw), lambda g, i, j: (i, off + g))
    key = lambda off: pl.BlockSpec((tk, pw), lambda g, i, j: (jnp.minimum(i, j), off + g))
    out = pl.BlockSpec((tq, pw), lambda g, i, j: (i, g))
    return pl.pallas_call(
        body, name="fox_fwd", grid=(c.rw // pw, nq, nq),
        in_specs=[row(qb), key(kb), key(vb), row(gb),
                  pl.BlockSpec((LANES, tk), lambda g, i, j: (0, jnp.minimum(i, j)))],
        out_specs=[out, out, pl.BlockSpec((1, tq, nh), lambda g, i, j: (g, i, 0))],
        out_shape=[jax.ShapeDtypeStruct((t, c.rw), F32), jax.ShapeDtypeStruct((t, c.rw), BF16),
                   jax.ShapeDtypeStruct((c.rw // pw, t, nh), F32)],
        scratch_shapes=[pltpu.VMEM((nh, tq, 1), F32), pltpu.VMEM((nh, tq, 1), F32), pltpu.VMEM((tq, pw), F32)],
        compiler_params=_params("parallel", "parallel", "arbitrary"),
    )(p, p, p, p, cum_t)


def _fox2_grads(q2, k2, v2, do2, o2, lse_h, ck, mk, diag, tq, tk):
    zero = jnp.zeros_like(q2)
    s = _mm_nt(jnp.where(mk, q2, zero), k2) - ck
    if diag:
        s = jnp.where(_causal(tq, tk), s, NEG)
    pm = jnp.exp(s - lse_h)
    delta = jnp.sum(jnp.where(mk, do2 * o2, 0.0), axis=1, keepdims=True)
    dob = do2.astype(BF16)
    dp = _mm_nt(jnp.where(mk, dob, zero), v2)
    return pm, pm * (dp - delta), dob


def _fox2_bwd_dq(p, c, cum_t, lse, o, do, tb):
    t = p.shape[0]
    tq = tk = _tile(t, tb, LANES)
    nq = t // tq
    pw, nh = FOX_PAIRS * LANES, FOX_HEADS_STEP
    qb, kb, vb = (o_ // pw for o_ in (c.o_fq, c.o_fk, c.o_fv))
    scale = HEAD_DIM ** -0.5

    def body(q_ref, k_ref, v_ref, ck_ref, lse_ref, o_ref, do_ref, dq_ref, dcq_ref, acc_sc, row_sc):
        g, qi, ki = pl.program_id(0), pl.program_id(1), pl.program_id(2)

        @pl.when(ki == 0)
        def _():
            acc_sc[...] = jnp.zeros_like(acc_sc)
            row_sc[...] = jnp.zeros_like(row_sc)

        def step(diag):
            lse_blk = lse_ref[0]
            for pi in range(FOX_PAIRS):
                lanes = slice(pi * LANES, (pi + 1) * LANES)
                q2 = (q_ref[:, lanes] * scale).astype(BF16)
                k2, v2 = k_ref[:, lanes].astype(BF16), v_ref[:, lanes].astype(BF16)
                do2, o2 = do_ref[:, lanes], o_ref[:, lanes]
                acc = acc_sc[:, lanes]
                new_acc = acc
                for hh in range(2):
                    hi = 2 * pi + hh
                    mk = _lane_half((tq, LANES), hh == 1)
                    _, ds, _ = _fox2_grads(q2, k2, v2, do2, o2, _col(lse_blk, hi),
                                           ck_ref[pl.ds(g * nh + hi, 1), :], mk, diag, tq, tk)
                    row_sc[hi] += jnp.sum(ds, axis=1, keepdims=True)
                    new_acc = jnp.where(mk, acc + _mm(ds.astype(BF16), k2), new_acc)
                acc_sc[:, lanes] = new_acc

        @pl.when(ki < qi)
        def _():
            step(False)

        @pl.when(ki == qi)
        def _():
            step(True)
            dq_ref[...] = (acc_sc[...] * scale).astype(BF16)
            dcq_ref[0] = _from_cols([row_sc[h] for h in range(nh)])

    row = lambda off: pl.BlockSpec((tq, pw), lambda g, i, j: (i, off + g))
    key = lambda off: pl.BlockSpec((tk, pw), lambda g, i, j: (jnp.minimum(i, j), off + g))
    stat = pl.BlockSpec((1, tq, nh), lambda g, i, j: (g, i, 0))
    return pl.pallas_call(
        body, name="fox_bwd_dq", grid=(c.rw // pw, nq, nq),
        in_specs=[row(qb), key(kb), key(vb), pl.BlockSpec((LANES, tk), lambda g, i, j: (0, jnp.minimum(i, j))),
                  stat, row(0), row(0)],
        out_specs=[row(0), stat],
        out_shape=[jax.ShapeDtypeStruct((t, c.rw), BF16), jax.ShapeDtypeStruct((c.rw // pw, t, nh), F32)],
        scratch_shapes=[pltpu.VMEM((tq, pw), F32), pltpu.VMEM((nh, tq, 1), F32)],
        compiler_params=_params("parallel", "parallel", "arbitrary"),
    )(p, p, p, cum_t, lse, o, do)


def _fox2_bwd_dkv(p, c, cum_t, lse, o, do, tb):
    t = p.shape[0]
    tq = tk = _tile(t, tb, LANES)
    nq = t // tq
    pw, nh = FOX_PAIRS * LANES, FOX_HEADS_STEP
    qb, kb, vb = (o_ // pw for o_ in (c.o_fq, c.o_fk, c.o_fv))
    scale = HEAD_DIM ** -0.5

    def body(q_ref, k_ref, v_ref, ck_ref, lse_ref, o_ref, do_ref, dk_ref, dv_ref, dck_ref, dk_sc, dv_sc, dc_sc):
        g, ki, qi = pl.program_id(0), pl.program_id(1), pl.program_id(2)

        @pl.when(qi == 0)
        def _():
            dk_sc[...] = jnp.zeros_like(dk_sc)
            dv_sc[...] = jnp.zeros_like(dv_sc)
            dc_sc[...] = jnp.zeros_like(dc_sc)

        def step(diag):
            lse_blk = lse_ref[0]
            for pi in range(FOX_PAIRS):
                lanes = slice(pi * LANES, (pi + 1) * LANES)
                q2 = (q_ref[:, lanes] * scale).astype(BF16)
                k2, v2 = k_ref[:, lanes].astype(BF16), v_ref[:, lanes].astype(BF16)
                do2, o2 = do_ref[:, lanes], o_ref[:, lanes]
                dk, dv = dk_sc[:, lanes], dv_sc[:, lanes]
                new_dk, new_dv = dk, dv
                for hh in range(2):
                    hi = 2 * pi + hh
                    mk = _lane_half((tk, LANES), hh == 1)
                    pm, ds, dob = _fox2_grads(q2, k2, v2, do2, o2, _col(lse_blk, hi),
                                              ck_ref[pl.ds(g * nh + hi, 1), :], mk, diag, tq, tk)
                    dc_sc[hi] -= jnp.sum(ds, axis=0, keepdims=True)
                    new_dv = jnp.where(mk, dv + _mm_tn(pm.astype(BF16), dob), new_dv)
                    new_dk = jnp.where(mk, dk + _mm_tn(ds.astype(BF16), q2), new_dk)
                dk_sc[:, lanes] = new_dk
                dv_sc[:, lanes] = new_dv

        @pl.when(qi > ki)
        def _():
            step(False)

        @pl.when(qi == ki)
        def _():
            step(True)

        @pl.when(qi == nq - 1)
        def _():
            dk_ref[...] = dk_sc[...].astype(BF16)
            dv_ref[...] = dv_sc[...].astype(BF16)
            dck_ref[0] = _from_rows([dc_sc[h] for h in range(nh)])

    row = lambda off: pl.BlockSpec((tq, pw), lambda g, j, i: (jnp.maximum(i, j), off + g))
    key = lambda off: pl.BlockSpec((tk, pw), lambda g, j, i: (j, off + g))
    return pl.pallas_call(
        body, name="fox_bwd_dkv", grid=(c.rw // pw, nq, nq),
        in_specs=[row(qb), key(kb), key(vb), pl.BlockSpec((LANES, tk), lambda g, j, i: (0, j)),
                  pl.BlockSpec((1, tq, nh), lambda g, j, i: (g, jnp.maximum(i, j), 0)), row(0), row(0)],
        out_specs=[key(0), key(0), pl.BlockSpec((1, nh, tk), lambda g, j, i: (g, 0, j))],
        out_shape=[jax.ShapeDtypeStruct((t, c.rw), BF16), jax.ShapeDtypeStruct((t, c.rw), BF16),
                   jax.ShapeDtypeStruct((c.rw // pw, nh, t), F32)],
        scratch_shapes=[pltpu.VMEM((tk, pw), F32), pltpu.VMEM((tk, pw), F32), pltpu.VMEM((nh, 1, tk), F32)],
        compiler_params=_params("parallel", "parallel", "arbitrary"),
    )(p, p, p, cum_t, lse, o, do)


def _mem_probs(q, mk, scale):
    s = _mm_nt(q.astype(BF16), mk.astype(BF16)) * scale
    e = jnp.exp(s - jnp.max(s, axis=1, keepdims=True))
    return e / jnp.sum(e, axis=1, keepdims=True)


def _mem_attn_fwd(p, c, mkv):
    t = p.shape[0]
    tm = _tile(t, 512, 8)
    dh = c.mhd
    qb = c.o_mq // dh
    scale = dh ** -0.5

    def body(q_ref, mk_ref, mv_ref, o_ref):
        pm = _mem_probs(q_ref[...], mk_ref[...], scale)
        o_ref[...] = _mm(pm.astype(BF16), mv_ref[...].astype(BF16))

    m = mkv.shape[0]
    return pl.pallas_call(
        body, name="mem_attn_fwd", grid=(t // tm, MEM_HEADS),
        in_specs=[pl.BlockSpec((tm, dh), lambda i, j: (i, qb + j)),
                  pl.BlockSpec((m, dh), lambda i, j: (0, j)),
                  pl.BlockSpec((m, dh), lambda i, j: (0, MEM_HEADS + j))],
        out_specs=pl.BlockSpec((tm, dh), lambda i, j: (i, j)),
        out_shape=jax.ShapeDtypeStruct((t, c.mw), F32),
        compiler_params=_params("parallel", "parallel"),
    )(p, mkv, mkv)


def _mem_attn_bwd(p, c, mkv, do):
    t = p.shape[0]
    tm = _tile(t, 512, 8)
    dh = c.mhd
    qb = c.o_mq // dh
    scale = dh ** -0.5
    m = mkv.shape[0]

    def body(q_ref, mk_ref, mv_ref, do_ref, dq_ref, dmk_ref, dmv_ref):
        @pl.when(pl.program_id(1) == 0)
        def _():
            dmk_ref[...] = jnp.zeros_like(dmk_ref)
            dmv_ref[...] = jnp.zeros_like(dmv_ref)

        qv = q_ref[...].astype(BF16)
        pm = _mem_probs(qv, mk_ref[...], scale)
        dob = do_ref[...].astype(BF16)
        dmv_ref[...] += _mm_tn(pm.astype(BF16), dob)
        dp = _mm_nt(dob, mv_ref[...].astype(BF16))
        ds = (pm * (dp - jnp.sum(pm * dp, axis=1, keepdims=True)) * scale).astype(BF16)
        dq_ref[...] = _mm(ds, mk_ref[...].astype(BF16)).astype(BF16)
        dmk_ref[...] += _mm_tn(ds, qv)

    kvb = lambda off: pl.BlockSpec((m, dh), lambda j, i: (0, off + j))
    return pl.pallas_call(
        body, name="mem_attn_bwd", grid=(MEM_HEADS, t // tm),
        in_specs=[pl.BlockSpec((tm, dh), lambda j, i: (i, qb + j)), kvb(0), kvb(MEM_HEADS),
                  pl.BlockSpec((tm, dh), lambda j, i: (i, j))],
        out_specs=[pl.BlockSpec((tm, dh), lambda j, i: (i, j)), kvb(0), kvb(0)],
        out_shape=[jax.ShapeDtypeStruct((t, c.mw), BF16), jax.ShapeDtypeStruct((m, c.mw), F32),
                   jax.ShapeDtypeStruct((m, c.mw), F32)],
        compiler_params=_params("parallel", "arbitrary"),
    )(p, mkv, mkv, do)


def _adamw(w, g, m, v, name):
    rows, cols = w.shape
    tm = _tile(rows, max(8, (1 << 18) // cols // 8 * 8), 8)
    bc1 = 1.0 - ADAM_B1 ** ADAM_STEP
    bc2 = 1.0 - ADAM_B2 ** ADAM_STEP

    def body(w_ref, g_ref, m_ref, v_ref, go_ref, d_ref, mo_ref, vo_ref):
        gv = g_ref[:, 0:cols]
        mn = ADAM_B1 * m_ref[...] + (1.0 - ADAM_B1) * gv
        vn = ADAM_B2 * v_ref[...] + (1.0 - ADAM_B2) * (gv * gv)
        go_ref[...] = gv
        mo_ref[...] = mn
        vo_ref[...] = vn
        d_ref[...] = -ADAM_LR * ((mn / bc1) / (jnp.sqrt(vn / bc2) + ADAM_EPS) + ADAM_WD * w_ref[...])

    blk = pl.BlockSpec((tm, cols), lambda i: (i, 0))
    shp = jax.ShapeDtypeStruct((rows, cols), F32)
    return pl.pallas_call(
        body, name=name, grid=(rows // tm,),
        in_specs=[blk, pl.BlockSpec((tm, g.shape[1]), lambda i: (i, 0)), blk, blk],
        out_specs=[blk] * 4, out_shape=[shp] * 4,
        compiler_params=_params("parallel"),
    )(w, g, m, v)


SCAN_HEADS = 12
FOX_BLOCK = 512


def _local_step(c, x, mem, tgt, w, riders=None):
    t = x.shape[0]
    rw = c.rw
    riders = riders or {}
    carried = {}
    hd = lambda z: z.reshape(t, c.h, HEAD_DIM).transpose(1, 0, 2)
    uh = lambda z: z.transpose(1, 0, 2).reshape(t, rw)
    vecs = (w["mu"], w["w0"], w["a0"], w["k_k"], w["k_a"], w["wd"], w["wi"])

    h, rinv = _rms_fwd(x, w["g_pre"], "rms_pre")
    if "in_proj" in riders:
        groups, finish = riders["in_proj"]
        p, late = _matmul(h, w["wp"], name="in_proj", tk=4096, attach=groups)
        w = dict(w, **finish(late))
    else:
        p = _matmul(h, w["wp"], name="in_proj", tk=4096)
    r, lw, km, v, a, b = _rwkv_pre_fwd(p, c, *vecs)
    scan_in = tuple(hd(z) for z in (r, lw, km, v, a, b))
    hb = max(n for n in range(1, SCAN_HEADS + 1) if c.h % n == 0)
    ysh, states = _rwkv_scan_fwd(*scan_in, hb)
    ys = uh(ysh)
    yc_r = _rwkv_post_fwd(ys, r, km, v, p, c, w["ln_w"], w["ln_b"], w["r_k"])

    cum_t = _fox_prep_t(p, c, w["b_f"])
    yfox, yc_f, lse = _fox2_fwd(p, c, cum_t, FOX_BLOCK)

    memn, rinv_m = _rms_fwd(mem, w["g_mem"], "rms_mem")
    mkv = _matmul(memn, w["w_mem_kv"], name="mem_kv")
    ymem = _mem_attn_fwd(p, c, mkv)
    yc_m = _gate_fwd(ymem, p, c.o_gmq, "gate_mem")

    ycat = jnp.concatenate([yc_r, yc_f, yc_m], axis=1)
    yo = _matmul(ycat, w["w_out"], name="out_proj", tn=512, tk=4096)
    loss, dout, dyo, dg_post = _post_loss(yo, x, tgt, w["g_post"], "post_loss")

    dyc = _matmul(dyo, w["w_out"], tb=True, name="d_ycat", tn=512, tk=4096)
    dw_out = _matmul(ycat, dyo, ta=True, name="d_w_out", tn=512, tk=4096, out_dtype=BF16)
    dys, dr2, dkm2, dv2, dg_r, dln_w, dln_b, dr_k = _rwkv_post_bwd(
        dyc, ys, r, km, v, p, c, w["ln_w"], w["ln_b"], w["r_k"])
    dyf, dg_f = _gate_bwd(dyc, rw, yfox, p, c.o_gfox, "gate_fox_bwd")
    dym, dg_m = _gate_bwd(dyc, 2 * rw, ymem, p, c.o_gmq, "gate_mem_bwd")

    scan_g = _rwkv_scan_bwd(*scan_in, states, hd(dys), hb)
    dps, dzw, dza, twb, alb, dw0, da0, dk_k, dk_a = _rwkv_pre_bwd(
        p, c, *vecs, *(uh(z) for z in scan_g), dr2, dkm2, dv2)
    dwd = _matmul(twb, dzw, ta=True, name="d_w_decay", out_dtype=BF16)
    dwi = _matmul(alb, dza, ta=True, name="d_w_iclr", out_dtype=BF16)
    dp_shift, dmu = _shift_bwd(dps, p, c, w["mu"])

    dfq, dcq = _fox2_bwd_dq(p, c, cum_t, lse, yfox, dyf, FOX_BLOCK)
    dfk, dfv, dck = _fox2_bwd_dkv(p, c, cum_t, lse, yfox, dyf, FOX_BLOCK)
    dcum = jnp.pad(jnp.stack([dcq.transpose(1, 0, 2).reshape(t, c.h), dck.reshape(c.h, t).T]),
                   ((0, 0), (0, 0), (0, LANES - c.h)))
    dfl, db_f = _fox_logit_bwd(dcum, p, c, w["b_f"])

    dmq, dmk, dmv = _mem_attn_bwd(p, c, mkv, dym)
    dmkv = jnp.concatenate([dmk, dmv], axis=1)
    dw_mkv = _matmul(memn, dmkv, ta=True, name="d_w_mem_kv", out_dtype=BF16)
    dmemn = _matmul(dmkv, w["w_mem_kv"], tb=True, name="d_memn")
    _, dg_mem = _rms_bwd(dmemn, mem, rinv_m, w["g_mem"], jnp.zeros_like(mem), "rms_mem_bwd")

    dp = jnp.concatenate([dp_shift, dg_r, dfq, dfk, dfv, dg_f, dmq, dg_m, dfl], axis=1)
    rest = dict(wd=dwd, wi=dwi, w_mem_kv=dw_mkv, w_out=dw_out)
    if "d_w_in" in riders:
        dwp, carried["rest"] = _matmul(h, dp, ta=True, name="d_w_in", tk=4096, out_dtype=BF16,
                                       attach=riders["d_w_in"](rest))
    else:
        dwp = _matmul(h, dp, ta=True, name="d_w_in", tk=4096, out_dtype=BF16)
    if "d_h" in riders:
        dh, carried["wp"] = _matmul(dp, w["wp"], tb=True, name="d_h", tk=2944, attach=riders["d_h"](dwp))
    else:
        dh = _matmul(dp, w["wp"], tb=True, name="d_h", tk=2944)
    grad_x, dg_pre = _rms_bwd(dh, x, rinv, w["g_pre"], dout, "rms_pre_bwd")

    small = dict(g_pre=dg_pre, mu=dmu, w0=dw0, a0=da0, k_k=dk_k, k_a=dk_a, r_k=dr_k, ln_w=dln_w, ln_b=dln_b,
                 b_f=db_f, g_mem=dg_mem, g_post=dg_post)
    return loss, grad_x, dict(wp=dwp, **rest), small, carried


CHIPS = ((1, 0, 0), (0, 1, 0), (1, 1, 0))
SIBLING = ((0, 0, 1),)
ALL_PEERS = tuple((i, j, k) for i in (0, 1) for j in (0, 1) for k in (0, 1))[1:]


def _chip_of(pos):
    return 2 * pos[0] + pos[1]


DMA_CHUNK = 4 << 20


def _pieces(shape, itemsize):
    lead, (rows, cols) = shape[:-2], shape[-2:]
    k = 1
    if rows % 16 == 0:
        k = max(1, min(rows // 16, -(-rows * cols * itemsize // DMA_CHUNK)))
        while rows % k or (rows // k) % 16:
            k -= 1
    band = rows // k
    idxs = [()]
    for n in lead:
        idxs = [i + (j,) for i in idxs for j in range(n)]
    return [i + (pl.ds(j * band, band),) for i in idxs for j in range(k)]


def _peer_of(me, mask):
    return tuple(1 - v if f else v for v, f in zip(me, mask))


def _exchange(name, groups):
    n = len(groups)
    plan = _plan(groups)

    def body(*refs):
        copies = _copies(groups, plan, refs[:n], refs[n:2 * n], refs[2 * n], refs[2 * n + 1])
        for cp in copies:
            cp.start()
        for cp in copies:
            cp.wait()

    any_spec = pl.BlockSpec(memory_space=pl.ANY)
    return pl.pallas_call(
        body, name=name,
        in_specs=[any_spec] * n, out_specs=[any_spec] * n,
        out_shape=_exchange_shapes(groups),
        scratch_shapes=[pltpu.SemaphoreType.DMA((len(plan),)), pltpu.SemaphoreType.DMA((len(plan),))],
    )(*[g["src"] for g in groups])


def _plan(groups):
    return [(gi, ti, idx) for gi, g in enumerate(groups) for ti in range(len(g["transfers"]))
            for idx in _pieces(tuple(g["piece"]), g["src"].dtype.itemsize)]


def _exchange_shapes(groups):
    return [jax.ShapeDtypeStruct((g["slots"],) + tuple(g["piece"]), g["src"].dtype) for g in groups]


def _copies(groups, plan, srcs, outs, send_sems, recv_sems):
    me = (lax.axis_index("x"), lax.axis_index("y"), lax.axis_index("c"))
    copies = []
    for k, (gi, ti, idx) in enumerate(plan):
        mask, view, slot = groups[gi]["transfers"][ti]
        peer = _peer_of(me, mask)
        copies.append(pltpu.make_async_remote_copy(
            src_ref=view(srcs[gi], me, peer).at[idx], dst_ref=outs[gi].at[slot(me, peer)].at[idx],
            send_sem=send_sems.at[k], recv_sem=recv_sems.at[k],
            device_id=peer, device_id_type=MESH))
    return copies


def _my_chip():
    return 2 * lax.axis_index("x") + lax.axis_index("y")


def _put(buf, block, slot):
    return lax.dynamic_update_slice(buf, block[None], (slot,) + (0,) * block.ndim)


def _sum_slots(recv, own, k, out_dtype, name):
    s, rows, cols = recv.shape
    budget = max(16, ((4 << 20) // ((s + 1) * cols * 4)) // 16 * 16)
    tr = _tile(rows, budget, 16)
    own_many = own.shape[0] > 1

    def body(k_ref, *refs):
        out_ref = refs[s + 1]
        mine = refs[s][0].astype(F32)
        acc = None
        for i in range(s):
            term = jnp.where(k_ref[0] == i, mine, refs[i][0].astype(F32))
            acc = term if acc is None else acc + term
        out_ref[...] = acc.astype(out_ref.dtype)

    def slot_spec(i):
        return pl.BlockSpec((1, tr, cols), lambda j, kr: (jnp.where(kr[0] == i, (i + 1) % s, i), j, 0))

    grid_spec = pltpu.PrefetchScalarGridSpec(
        num_scalar_prefetch=1, grid=(rows // tr,),
        in_specs=[slot_spec(i) for i in range(s)]
                 + [pl.BlockSpec((1, tr, cols), lambda j, kr: (kr[0] if own_many else 0, j, 0))],
        out_specs=pl.BlockSpec((tr, cols), lambda j, kr: (j, 0)))
    return pl.pallas_call(
        body, name=name, grid_spec=grid_spec,
        out_shape=jax.ShapeDtypeStruct((rows, cols), out_dtype),
        compiler_params=_params("parallel"),
    )(k, *([recv] * s), own)


def _all_gather(shards):
    return _gather_finish(shards, _exchange("gather_chips", _gather_groups(shards)), "gather_pair")


def _gather_groups(shards):
    halves = [s.reshape(2, s.shape[0] // 2, s.shape[1]) for s in shards]
    return [dict(src=q, slots=4, piece=q.shape[1:],
                 transfers=[(m, lambda ref, me, peer: ref.at[me[2]], lambda me, peer: _chip_of(me)) for m in CHIPS])
            for q in halves]


def _gather_finish(shards, first, name):
    core, chip = lax.axis_index("c"), _my_chip()
    other_chip = lambda m: (lambda me: _chip_of(_peer_of(me, m)))
    second = _exchange(name, [
        dict(src=q, slots=4, piece=q.shape[1:],
             transfers=[(SIBLING[0], (lambda f: lambda ref, me, peer: ref.at[f(me)])(other_chip(m)),
                         (lambda f: lambda me, peer: f(me))(other_chip(m))) for m in CHIPS])
        for q in first])
    out = []
    for s, a, b in zip(shards, first, second):
        full = jnp.concatenate([jnp.where(core == 0, a, b), jnp.where(core == 0, b, a)], axis=1)
        out.append(_put(full, s, chip))
    return out


def _reduce_pair(partials, tag):
    core1 = lax.axis_index("c").reshape(1).astype(jnp.int32)
    halves = [q.reshape(4, 2, q.shape[1] // 2, q.shape[2]).transpose(1, 0, 2, 3) for q in partials]
    pair = _exchange("reduce_pair_" + tag, [
        dict(src=q, slots=2, piece=q.shape[1:],
             transfers=[(SIBLING[0], lambda ref, me, peer: ref.at[peer[2]], lambda me, peer: me[2])])
        for q in halves])
    flat = lambda e: e.reshape(2, -1, e.shape[-1])
    return [_sum_slots(flat(e), flat(q), core1, BF16, "reduce_pair_sum_" + tag).reshape(q.shape[1:])
            for e, q in zip(pair, halves)]


def _reduce_chips_groups(chip_sums):
    return [dict(src=q, slots=4, piece=q.shape[1:],
                 transfers=[(m, lambda ref, me, peer: ref.at[_chip_of(peer)], lambda me, peer: _chip_of(me))
                            for m in CHIPS])
            for q in chip_sums]


def _reduce_finish(crossed, chip_sums, tag):
    core = lax.axis_index("c")
    chip1 = _my_chip().reshape(1).astype(jnp.int32)
    sums = [_sum_slots(e, q, chip1, F32, "reduce_chips_sum_" + tag) for e, q in zip(crossed, chip_sums)]
    swapped = _exchange("reduce_swap_" + tag, [
        dict(src=q, slots=2, piece=q.shape, transfers=[(SIBLING[0], lambda ref, me, peer: ref, lambda me, peer: me[2])])
        for q in sums])
    return [_put(e, q, core).reshape(-1, e.shape[-1]) for e, q in zip(swapped, sums)]


def _reduce_scatter(partials):
    chip_sums = _reduce_pair(partials, "all")
    return _reduce_finish(_exchange("reduce_chips", _reduce_chips_groups(chip_sums)), chip_sums, "all")


def _all_reduce_small(vec):
    dev = 4 * lax.axis_index("x") + 2 * lax.axis_index("y") + lax.axis_index("c")
    got = _exchange("reduce_small", [
        dict(src=vec, slots=8, piece=vec.shape,
             transfers=[(m, lambda ref, me, peer: ref, lambda me, peer: 4 * me[0] + 2 * me[1] + me[2])
                        for m in ALL_PEERS])])[0]
    return _sum_slots(got, vec[None], dev.reshape(1).astype(jnp.int32), F32, "reduce_small_sum")


SMALL = ("g_pre", "mu", "w0", "a0", "k_k", "k_a", "r_k", "ln_w", "ln_b", "b_f", "g_mem", "g_post")


def _pad_cols(a, n):
    return jnp.pad(a, ((0, 0),) * (a.ndim - 1) + ((0, n - a.shape[-1]),))


def kernel(x, mem, g_pre, w_in, mu_rwkv, w0, w_decay_up, a0, w_iclr_up, k_k, k_a, r_k, ln_x_w, ln_x_b, b_f, g_mem, w_mem_kv, w_out, g_post, loss_target, m_g_pre, m_w_in, m_mu_rwkv, m_w0, m_w_decay_up, m_a0, m_w_iclr_up, m_k_k, m_k_a, m_r_k, m_ln_x_w, m_ln_x_b, m_b_f, m_g_mem, m_w_mem_kv, m_w_out, m_g_post, v_g_pre, v_w_in, v_mu_rwkv, v_w0, v_w_decay_up, v_a0, v_w_iclr_up, v_k_k, v_k_a, v_r_k, v_ln_x_w, v_ln_x_b, v_b_f, v_g_mem, v_w_mem_kv, v_w_out, v_g_post):
    d = x.shape[-1]
    c = Cfg(d)
    ws = w_in.shape[-1]
    wpad = -(-ws // LANES) * LANES
    nh = c.h

    g_in, g_wd, g_wi = _all_gather([
        _pad_cols(w_in[0].astype(BF16), wpad), w_decay_up[0].astype(BF16), w_iclr_up[0].astype(BF16)])
    fl = c.ref_fl
    runs = [(0, fl, 0), (fl, fl + nh, c.o_fl), (fl + nh, c.in_width, fl)]
    pieces = []
    for lo, hi, _ in sorted(runs, key=lambda r: r[2]):
        for s in range(4):
            a, b = max(lo, s * ws), min(hi, (s + 1) * ws)
            if a < b:
                pieces.append(g_in[s, :, a - s * ws:b - s * ws])
    wp = jnp.concatenate(pieces + [jnp.zeros((d, LANES - nh), BF16)], axis=1)
    unshard = lambda g: g.transpose(1, 0, 2).reshape(g.shape[1], -1)
    weights = dict(wp=wp, wd=unshard(g_wd), wi=unshard(g_wi),
                   g_pre=g_pre, mu=mu_rwkv, w0=w0, a0=a0, k_k=k_k, k_a=k_a, r_k=r_k.reshape(1, -1),
                   ln_w=ln_x_w, ln_b=ln_x_b, b_f=_pad_cols(b_f, LANES), g_mem=g_mem, g_post=g_post)
    late_shards = [w_out[0].astype(BF16), w_mem_kv[0].astype(BF16)]

    def late_weights(first):
        g_out, g_mkv = _gather_finish(late_shards, first, "gather_pair_late")
        return dict(w_out=g_out.reshape(-1, d), w_mem_kv=g_mkv.reshape(d, -1))

    by_chip = lambda g: jnp.stack(jnp.split(g, 4, axis=1))
    pair_sums = {}

    def ride_rest(g):
        pair_sums["rest"] = _reduce_pair([g["w_out"].reshape(4, -1, d), g["w_mem_kv"].reshape(4, d // 4, -1),
                                          by_chip(g["wd"]), by_chip(g["wi"])], "rest")
        return _reduce_chips_groups(pair_sums["rest"])

    def ride_wp(dwp):
        shards = []
        for s in range(4):
            cols = []
            for lo, hi, at in runs:
                a, b = max(lo, s * ws), min(hi, (s + 1) * ws)
                if a < b:
                    cols.append(dwp[:, at + a - lo:at + b - lo])
            shards.append(jnp.concatenate(cols + [jnp.zeros((d, wpad - ws), BF16)], axis=1))
        pair_sums["wp"] = _reduce_pair([jnp.stack(shards)], "w_in")
        return _reduce_chips_groups(pair_sums["wp"])

    loss, grad_x, _, small, carried = _local_step(
        c, x[0], mem[0], loss_target[0], weights,
        riders={"in_proj": (_gather_groups(late_shards), late_weights), "d_w_in": ride_rest, "d_h": ride_wp})
    red = (_reduce_finish(carried["wp"], pair_sums["wp"], "w_in")
           + _reduce_finish(carried["rest"], pair_sums["rest"], "rest"))
    big_w = (w_in[0], w_out[0], w_mem_kv[0], w_decay_up[0], w_iclr_up[0])
    big_m = (m_w_in[0], m_w_out[0], m_w_mem_kv[0], m_w_decay_up[0], m_w_iclr_up[0])
    big_v = (v_w_in[0], v_w_out[0], v_w_mem_kv[0], v_w_decay_up[0], v_w_iclr_up[0])
    big_names = ("w_in", "w_out", "w_mem_kv", "w_decay_up", "w_iclr_up")
    upd = {n: _adamw(w_, g_, m_, v_, "adamw_" + n) for n, w_, g_, m_, v_ in zip(big_names, big_w, red, big_m, big_v)}

    small_w = dict(g_pre=g_pre, mu=mu_rwkv, w0=w0, a0=a0, k_k=k_k, k_a=k_a, r_k=r_k.reshape(1, -1), ln_w=ln_x_w,
                   ln_b=ln_x_b, b_f=b_f, g_mem=g_mem, g_post=g_post)
    small_m = dict(g_pre=m_g_pre, mu=m_mu_rwkv, w0=m_w0, a0=m_a0, k_k=m_k_k, k_a=m_k_a, r_k=m_r_k.reshape(1, -1),
                   ln_w=m_ln_x_w, ln_b=m_ln_x_b, b_f=m_b_f, g_mem=m_g_mem, g_post=m_g_post)
    small_v = dict(g_pre=v_g_pre, mu=v_mu_rwkv, w0=v_w0, a0=v_a0, k_k=v_k_k, k_a=v_k_a, r_k=v_r_k.reshape(1, -1),
                   ln_w=v_ln_x_w, ln_b=v_ln_x_b, b_f=v_b_f, g_mem=v_g_mem, g_post=v_g_post)
    widths = [-(-small_w[n].shape[1] // LANES) * LANES for n in SMALL]
    pack = lambda t: jnp.concatenate([_pad_cols(t[n], wd_) for n, wd_ in zip(SMALL, widths)]
                                     + [jnp.zeros((1, LANES), F32)], axis=1)
    g_packed = jnp.concatenate([_pad_cols(small[n], wd_) for n, wd_ in zip(SMALL, widths)]
                               + [_pad_cols(loss, LANES)], axis=1)
    g_sum = _all_reduce_small(g_packed)
    s_upd = _adamw(pack(small_w), g_sum, pack(small_m), pack(small_v), "adamw_small")
    offs = [sum(widths[:i]) for i in range(len(SMALL))]

    def take(kind, n):
        i = SMALL.index(n)
        piece = s_upd[kind][:, offs[i]:offs[i] + small_w[n].shape[1]]
        return piece.reshape(r_k.shape) if n == "r_k" else piece

    total_loss = g_sum[0, sum(widths)]
    order = ("g_pre", "w_in", "mu", "w0", "w_decay_up", "a0", "w_iclr_up", "k_k", "k_a", "r_k", "ln_w", "ln_b", "b_f",
             "g_mem", "w_mem_kv", "w_out", "g_post")
    outs = [total_loss, grad_x[None]]
    for kind in range(4):
        for n in order:
            outs.append(upd[n][kind][None] if n in upd else take(kind, n))
    return tuple(outs)
```

```python
import functools

import jax
import jax.numpy as jnp
from jax import lax
from jax.experimental import pallas as pl
from jax.experimental.pallas import tpu as pltpu

F32 = jnp.float32
BF16 = jnp.bfloat16
HI = lax.Precision.HIGHEST
MESH = pl.DeviceIdType.MESH

HEAD_DIM = 64
MEM_HEADS = 4
LORA = 128
CHUNK = 64
RMS_EPS = 1e-6
GN_EPS = 64e-5
LANES = 128
VMEM_LIMIT = 56 * 1024 * 1024

ADAM_LR, ADAM_B1, ADAM_B2, ADAM_EPS, ADAM_WD, ADAM_STEP = 0.001, 0.9, 0.999, 1e-08, 0.01, 10


class Cfg:
    def __init__(self, d):
        self.d = d
        self.rw = 3 * d // 8
        self.mw = d // 4
        self.h = self.rw // HEAD_DIM
        self.mhd = self.mw // MEM_HEADS
        self.shift = 3 * self.rw + 2 * LORA
        self.in_width = self.shift + 5 * self.rw + self.h + 2 * self.mw
        o = self.shift
        self.o_grw = o; o += self.rw
        self.o_fq = o; o += self.rw
        self.o_fk = o; o += self.rw
        self.o_fv = o; o += self.rw
        self.o_gfox = o; o += self.rw
        self.o_mq = o; o += self.mw
        self.o_gmq = o; o += self.mw
        self.o_fl = o; o += LANES
        self.wp = o
        self.ref_fl = self.shift + 4 * self.rw


def _tile(n, pref, align=LANES):
    if n <= pref:
        return n
    t = (pref // align) * align
    while t >= align:
        if n % t == 0:
            return t
        t -= align
    return n


def _params(*sem):
    return pltpu.CompilerParams(dimension_semantics=sem, vmem_limit_bytes=VMEM_LIMIT)


def _pallas_into(body, into, n_in, out_index, in_specs, **kw):
    if into is None:
        return pl.pallas_call(body, in_specs=in_specs, **kw)

    def body_with_alias(*refs):
        return body(*refs[:n_in], *refs[n_in + 1:])

    call = pl.pallas_call(body_with_alias, in_specs=list(in_specs) + [pl.BlockSpec(memory_space=pl.ANY)],
                          input_output_aliases={n_in: out_index}, **kw)
    return lambda *args: call(*args, into)


def _sig(x):
    return 1.0 / (1.0 + jnp.exp(-x))


def _softplus(x):
    return jnp.maximum(x, 0.0) + jnp.log(1.0 + jnp.exp(-jnp.abs(x)))


def _dot(a, b, dims, prec=None):
    return lax.dot_general(a, b, (dims, ((), ())), precision=prec, preferred_element_type=F32)


def _mm(a, b, prec=None):
    return _dot(a, b, ((1,), (0,)), prec)


def _mm_nt(a, b, prec=None):
    return _dot(a, b, ((1,), (1,)), prec)


def _mm_tn(a, b, prec=None):
    return _dot(a, b, ((0,), (0,)), prec)


def _split(a):
    hi = a.astype(BF16)
    return hi, (a - hi.astype(F32)).astype(BF16)


def _dot3(a, b, dims, passes=3):
    d = lambda x, y: lax.dot_general(x, y, dims, preferred_element_type=F32)
    if passes == 1:
        return d(a.astype(BF16), b.astype(BF16))
    (ah, al), (bh, bl) = _split(a), _split(b)
    return d(ah, bh) + (d(ah, bl) + d(al, bh))


def _bmm(a, b, passes=3):
    return _dot3(a, b, (((2,), (1,)), ((0,), (0,))), passes)


def _bmm_nt(a, b, passes=3):
    return _dot3(a, b, (((2,), (2,)), ((0,), (0,))), passes)


def _bmm_tn(a, b, passes=3):
    return _dot3(a, b, (((1,), (1,)), ((0,), (0,))), passes)


def _bmm_01(m01, x):
    x1 = x.astype(BF16)
    r1 = x - x1.astype(F32)
    x2 = r1.astype(BF16)
    x3 = (r1 - x2.astype(F32)).astype(BF16)
    d = lambda y: lax.dot_general(m01, y, (((2,), (1,)), ((0,), (0,))), preferred_element_type=F32)
    return d(x1) + (d(x2) + d(x3))


def _matmul(a, b, *, ta=False, tb=False, out_dtype=F32, name, tm=1024, tn=1024, tk=1024, attach=None):
    m, k = (a.shape[1], a.shape[0]) if ta else a.shape
    n = b.shape[0] if tb else b.shape[1]
    tm, tn, tk = _tile(m, tm), _tile(n, tn), _tile(k, tk)
    nk = k // tk
    grid = (m // tm, n // tn, nk)
    dims = ((0 if ta else 1,), (1 if tb else 0,))
    groups = attach or []
    ng = len(groups)
    plan = _plan(groups)

    def body(a_ref, b_ref, *rest):
        srcs, o_ref, outs, scratch = rest[:ng], rest[ng], rest[ng + 1:2 * ng + 1], rest[2 * ng + 1:]
        acc = scratch[0] if nk > 1 else None
        if ng:
            copies = _copies(groups, plan, srcs, outs, scratch[-2], scratch[-1])
            ids = [pl.program_id(ax) for ax in range(3)]

            @pl.when((ids[0] == 0) & (ids[1] == 0) & (ids[2] == 0))
            def _():
                for cp in copies:
                    cp.start()

        part = _dot(a_ref[...].astype(BF16), b_ref[...].astype(BF16), dims)
        if nk == 1:
            o_ref[...] = part.astype(o_ref.dtype)
        else:
            kk = pl.program_id(2)

            @pl.when(kk == 0)
            def _():
                acc[...] = part

            @pl.when(kk > 0)
            def _():
                acc[...] += part

            @pl.when(kk == nk - 1)
            def _():
                o_ref[...] = acc[...].astype(o_ref.dtype)

        if ng:
            @pl.when((ids[0] == grid[0] - 1) & (ids[1] == grid[1] - 1) & (ids[2] == grid[2] - 1))
            def _():
                for cp in copies:
                    cp.wait()

    a_spec = pl.BlockSpec((tk, tm), lambda i, j, kk: (kk, i)) if ta else pl.BlockSpec((tm, tk), lambda i, j, kk: (i, kk))
    b_spec = pl.BlockSpec((tn, tk), lambda i, j, kk: (j, kk)) if tb else pl.BlockSpec((tk, tn), lambda i, j, kk: (kk, j))
    any_spec = pl.BlockSpec(memory_space=pl.ANY)
    sems = [pltpu.SemaphoreType.DMA((len(plan),)), pltpu.SemaphoreType.DMA((len(plan),))] if ng else []
    res = pl.pallas_call(
        body, name=name, grid=grid,
        in_specs=[a_spec, b_spec] + [any_spec] * ng,
        out_specs=[pl.BlockSpec((tm, tn), lambda i, j, kk: (i, j))] + [any_spec] * ng,
        out_shape=[jax.ShapeDtypeStruct((m, n), out_dtype)] + _exchange_shapes(groups),
        scratch_shapes=([pltpu.VMEM((tm, tn), F32)] if nk > 1 else []) + sems,
        compiler_params=_params(*(("arbitrary",) * 3 if ng else ("parallel", "parallel", "arbitrary"))),
    )(a, b, *[g["src"] for g in groups])
    return (res[0], list(res[1:])) if ng else res[0]


def _rms_fwd(x, g, name):
    t, d = x.shape
    tm = _tile(t, 256, 8)

    def body(x_ref, g_ref, h_ref, r_ref):
        xv = x_ref[...]
        r = lax.rsqrt(jnp.mean(xv * xv, axis=-1, keepdims=True) + RMS_EPS)
        h_ref[...] = (xv * r * g_ref[...]).astype(BF16)
        r_ref[...] = r

    return pl.pallas_call(
        body, name=name, grid=(t // tm,),
        in_specs=[pl.BlockSpec((tm, d), lambda i: (i, 0)), pl.BlockSpec((1, d), lambda i: (0, 0))],
        out_specs=[pl.BlockSpec((tm, d), lambda i: (i, 0)), pl.BlockSpec((tm, 1), lambda i: (i, 0))],
        out_shape=[jax.ShapeDtypeStruct((t, d), BF16), jax.ShapeDtypeStruct((t, 1), F32)],
        compiler_params=_params("parallel"),
    )(x, g)


def _rms_bwd(dh, x, rinv, g, add, name):
    t, d = x.shape
    tm = _tile(t, 256, 8)

    def body(dh_ref, x_ref, r_ref, g_ref, add_ref, dx_ref, dg_ref):
        @pl.when(pl.program_id(0) == 0)
        def _():
            dg_ref[...] = jnp.zeros_like(dg_ref)

        r = r_ref[...]
        xn = x_ref[...] * r
        dhv = dh_ref[...]
        dg_ref[...] += jnp.sum(dhv * xn, axis=0, keepdims=True)
        dxn = dhv * g_ref[...]
        dx_ref[...] = add_ref[...] + r * (dxn - xn * jnp.mean(dxn * xn, axis=-1, keepdims=True))

    row = pl.BlockSpec((tm, d), lambda i: (i, 0))
    vec = pl.BlockSpec((1, d), lambda i: (0, 0))
    return pl.pallas_call(
        body, name=name, grid=(t // tm,),
        in_specs=[row, row, pl.BlockSpec((tm, 1), lambda i: (i, 0)), vec, row],
        out_specs=[row, vec],
        out_shape=[jax.ShapeDtypeStruct((t, d), F32), jax.ShapeDtypeStruct((1, d), F32)],
        compiler_params=_params("arbitrary"),
    )(dh, x, rinv, g, add)


def _post_loss(yo, x, tgt, g, name):
    t, d = x.shape
    tm = _tile(t, 256, 8)

    def body(yo_ref, x_ref, t_ref, g_ref, loss_ref, dout_ref, dyo_ref, dg_ref):
        @pl.when(pl.program_id(0) == 0)
        def _():
            dg_ref[...] = jnp.zeros_like(dg_ref)
            loss_ref[...] = jnp.zeros_like(loss_ref)

        yv = yo_ref[...]
        r = lax.rsqrt(jnp.mean(yv * yv, axis=-1, keepdims=True) + RMS_EPS)
        n = yv * r
        err = x_ref[...] + n * g_ref[...] - t_ref[...]
        loss_ref[...] += 0.5 * jnp.sum(jnp.mean(err * err, axis=-1, keepdims=True), axis=0, keepdims=True)
        dout = err * (1.0 / d)
        dout_ref[...] = dout
        dg_ref[...] += jnp.sum(dout * n, axis=0, keepdims=True)
        dn = dout * g_ref[...]
        dyo_ref[...] = (r * (dn - n * jnp.mean(dn * n, axis=-1, keepdims=True))).astype(BF16)

    row = pl.BlockSpec((tm, d), lambda i: (i, 0))
    vec = pl.BlockSpec((1, d), lambda i: (0, 0))
    return pl.pallas_call(
        body, name=name, grid=(t // tm,),
        in_specs=[row, row, row, vec],
        out_specs=[pl.BlockSpec((1, 1), lambda i: (0, 0)), row, row, vec],
        out_shape=[jax.ShapeDtypeStruct((1, 1), F32), jax.ShapeDtypeStruct((t, d), F32),
                   jax.ShapeDtypeStruct((t, d), BF16), jax.ShapeDtypeStruct((1, d), F32)],
        compiler_params=_params("arbitrary"),
    )(yo, x, tgt, g)


def _head_sum(x):
    ri = lax.broadcasted_iota(jnp.int32, (LANES, LANES), 0) // HEAD_DIM
    ci = lax.broadcasted_iota(jnp.int32, (LANES, LANES), 1) // HEAD_DIM
    e = (ri == ci).astype(BF16)
    x1 = x.astype(BF16)
    r1 = x - x1.astype(F32)
    x2 = r1.astype(BF16)
    x3 = (r1 - x2.astype(F32)).astype(BF16)
    parts = []
    for i in range(x.shape[1] // LANES):
        sl = slice(i * LANES, (i + 1) * LANES)
        parts.append(_mm(x1[:, sl], e) + (_mm(x2[:, sl], e) + _mm(x3[:, sl], e)))
    return parts[0] if len(parts) == 1 else jnp.concatenate(parts, axis=1)


def _shifted(p_cur, before, first, mu):
    rolled = pltpu.roll(p_cur, 1, 0)
    prev_row = jnp.where(first, 0.0, before)
    row0 = lax.broadcasted_iota(jnp.int32, p_cur.shape, 0) == 0
    prev = jnp.where(row0, prev_row, rolled)
    return p_cur + (prev - p_cur) * mu, prev


def _rwkv_features(ps, rw, w0, a0, k_k, k_a, wd, wi):
    r, k, v = ps[:, 0:rw], ps[:, rw:2 * rw], ps[:, 2 * rw:3 * rw]
    wl, al = ps[:, 3 * rw:3 * rw + LORA], ps[:, 3 * rw + LORA:3 * rw + 2 * LORA]
    tw = jnp.tanh(wl)
    zw = w0 + _mm(tw.astype(BF16), wd)
    logw = -jnp.exp(-_softplus(-zw) - 0.5)
    alpha = _sig(a0 + _mm(al.astype(BF16), wi))
    kkr = k * k_k
    n2 = _head_sum(kkr * kkr)
    rn = lax.rsqrt(jnp.maximum(n2, 1e-24))
    kk = kkr * rn
    kmod = k * (1.0 + (alpha - 1.0) * k_a)
    return dict(r=r, k=k, v=v, tw=tw, al=al, zw=zw, logw=logw, alpha=alpha, kk=kk, rn=rn, n2=n2, kmod=kmod)


def _rwkv_pre_fwd(p, c, mu, w0, a0, k_k, k_a, wd, wi):
    t = p.shape[0]
    tm = _tile(t, 128, 8)
    rw, sh = c.rw, c.shift

    def body(p_ref, pp_ref, mu_ref, w0_ref, a0_ref, kk_ref, ka_ref, wd_ref, wi_ref,
             r_ref, lw_ref, km_ref, v_ref, a_ref, b_ref):
        ps, _ = _shifted(p_ref[...], pp_ref[7:8, :], pl.program_id(0) == 0, mu_ref[...])
        f = _rwkv_features(ps, rw, w0_ref[...], a0_ref[...], kk_ref[...], ka_ref[...], wd_ref[...], wi_ref[...])
        r_ref[...] = f["r"]
        lw_ref[...] = f["logw"]
        km_ref[...] = f["kmod"]
        v_ref[...] = f["v"]
        a_ref[...] = -f["kk"]
        b_ref[...] = f["kk"] * f["alpha"]

    vec = lambda n: pl.BlockSpec((1, n), lambda i: (0, 0))
    out = pl.BlockSpec((tm, rw), lambda i: (i, 0))
    return pl.pallas_call(
        body, name="rwkv_pre_fwd", grid=(t // tm,),
        in_specs=[pl.BlockSpec((tm, sh), lambda i: (i, 0)),
                  pl.BlockSpec((8, sh), lambda i: (jnp.maximum(i * (tm // 8) - 1, 0), 0)),
                  vec(sh), vec(rw), vec(rw), vec(rw), vec(rw),
                  pl.BlockSpec((LORA, rw), lambda i: (0, 0)), pl.BlockSpec((LORA, rw), lambda i: (0, 0))],
        out_specs=[out] * 6,
        out_shape=[jax.ShapeDtypeStruct((t, rw), F32)] * 6,
        compiler_params=_params("parallel"),
    )(p, p, mu, w0, a0, k_k, k_a, wd, wi)


def _rwkv_pre_bwd(p, c, mu, w0, a0, k_k, k_a, wd, wi, dr, dlw, dkm, dv, da, db, dr2, dkm2, dv2):
    t = p.shape[0]
    tm = _tile(t, 128, 8)
    rw, sh = c.rw, c.shift

    def body(p_ref, pp_ref, mu_ref, w0_ref, a0_ref, kk_ref, ka_ref, wd_ref, wi_ref,
             dr_ref, dlw_ref, dkm_ref, dv_ref, da_ref, db_ref, dr2_ref, dkm2_ref, dv2_ref,
             dps_ref, dzw_ref, dza_ref, tw_ref, al_ref, dw0_ref, da0_ref, dkk_ref, dka_ref):
        @pl.when(pl.program_id(0) == 0)
        def _():
            for ref in (dw0_ref, da0_ref, dkk_ref, dka_ref):
                ref[...] = jnp.zeros_like(ref)

        ps, _ = _shifted(p_ref[...], pp_ref[7:8, :], pl.program_id(0) == 0, mu_ref[...])
        k_k, k_a = kk_ref[...], ka_ref[...]
        f = _rwkv_features(ps, rw, w0_ref[...], a0_ref[...], k_k, k_a, wd_ref[...], wi_ref[...])
        alpha, kk, k = f["alpha"], f["kk"], f["k"]
        dkm = dkm_ref[...] + dkm2_ref[...]
        db = db_ref[...]
        dkk = db * alpha - da_ref[...]
        dalpha = db * kk + dkm * k * k_a
        dk = dkm * (1.0 + (alpha - 1.0) * k_a)
        dka_ref[...] += jnp.sum(dkm * k * (alpha - 1.0), axis=0, keepdims=True)
        dkkr = f["rn"] * jnp.where(f["n2"] > 1e-24, dkk - kk * _head_sum(dkk * kk), dkk)
        dk = dk + dkkr * k_k
        dkk_ref[...] += jnp.sum(dkkr * k, axis=0, keepdims=True)
        dza = dalpha * alpha * (1.0 - alpha)
        da0_ref[...] += jnp.sum(dza, axis=0, keepdims=True)
        dzw = dlw_ref[...] * f["logw"] * _sig(-f["zw"])
        dw0_ref[...] += jnp.sum(dzw, axis=0, keepdims=True)
        dza_b, dzw_b = dza.astype(BF16), dzw.astype(BF16)
        dal = _mm_nt(dza_b, wi_ref[...])
        dwl = _mm_nt(dzw_b, wd_ref[...]) * (1.0 - f["tw"] * f["tw"])
        dps_ref[:, 0:rw] = dr_ref[...] + dr2_ref[...]
        dps_ref[:, rw:2 * rw] = dk
        dps_ref[:, 2 * rw:3 * rw] = dv_ref[...] + dv2_ref[...]
        dps_ref[:, 3 * rw:3 * rw + LORA] = dwl
        dps_ref[:, 3 * rw + LORA:sh] = dal
        dzw_ref[...] = dzw_b
        dza_ref[...] = dza_b
        tw_ref[...] = f["tw"].astype(BF16)
        al_ref[...] = f["al"].astype(BF16)

    vec = lambda n: pl.BlockSpec((1, n), lambda i: (0, 0))
    blk = lambda n: pl.BlockSpec((tm, n), lambda i: (i, 0))
    return pl.pallas_call(
        body, name="rwkv_pre_bwd", grid=(t // tm,),
        in_specs=[blk(sh), pl.BlockSpec((8, sh), lambda i: (jnp.maximum(i * (tm // 8) - 1, 0), 0)),
                  vec(sh), vec(rw), vec(rw), vec(rw), vec(rw),
                  pl.BlockSpec((LORA, rw), lambda i: (0, 0)), pl.BlockSpec((LORA, rw), lambda i: (0, 0))]
                 + [blk(rw)] * 9,
        out_specs=[blk(sh), blk(rw), blk(rw), blk(LORA), blk(LORA), vec(rw), vec(rw), vec(rw), vec(rw)],
        out_shape=[jax.ShapeDtypeStruct((t, sh), F32), jax.ShapeDtypeStruct((t, rw), BF16),
                   jax.ShapeDtypeStruct((t, rw), BF16), jax.ShapeDtypeStruct((t, LORA), BF16),
                   jax.ShapeDtypeStruct((t, LORA), BF16)] + [jax.ShapeDtypeStruct((1, rw), F32)] * 4,
        compiler_params=_params("arbitrary"),
    )(p, p, mu, w0, a0, k_k, k_a, wd, wi, dr, dlw, dkm, dv, da, db, dr2, dkm2, dv2)


def _shift_bwd(dps, p, c, mu, dp):
    t = p.shape[0]
    tm = _tile(t, 256, 8)
    sh = c.shift
    nt = t // tm

    def body(d_ref, dn_ref, p_ref, pp_ref, mu_ref, dp_ref, dmu_ref):
        i = pl.program_id(0)

        @pl.when(i == 0)
        def _():
            dmu_ref[...] = jnp.zeros_like(dmu_ref)

        mu = mu_ref[...]
        d = d_ref[...]
        pc = p_ref[...]
        _, prev = _shifted(pc, pp_ref[7:8, :], i == 0, mu)
        dmu_ref[...] += jnp.sum(d * (prev - pc), axis=0, keepdims=True)
        nxt_row = jnp.where(i == nt - 1, 0.0, dn_ref[0:1, :])
        last = lax.broadcasted_iota(jnp.int32, d.shape, 0) == tm - 1
        nxt = jnp.where(last, nxt_row, pltpu.roll(d, tm - 1, 0))
        dp_ref[...] = (d * (1.0 - mu) + nxt * mu).astype(BF16)

    blk = pl.BlockSpec((tm, sh), lambda i: (i, 0))
    return _pallas_into(
        body, dp, 5, 0, name="shift_bwd", grid=(nt,),
        in_specs=[blk, pl.BlockSpec((8, sh), lambda i: (jnp.minimum((i + 1) * (tm // 8), t // 8 - 1), 0)),
                  blk, pl.BlockSpec((8, sh), lambda i: (jnp.maximum(i * (tm // 8) - 1, 0), 0)),
                  pl.BlockSpec((1, sh), lambda i: (0, 0))],
        out_specs=[blk, pl.BlockSpec((1, sh), lambda i: (0, 0))],
        out_shape=[jax.ShapeDtypeStruct((t, c.wp), BF16), jax.ShapeDtypeStruct((1, sh), F32)],
        compiler_params=_params("arbitrary"),
    )(dps, dps, p, p, mu)


def _tri(n, strict):
    ri = lax.broadcasted_iota(jnp.int32, (n, n), 0)
    ci = lax.broadcasted_iota(jnp.int32, (n, n), 1)
    return (ri > ci) if strict else (ri >= ci)


def _unit_lower_inverse(a):
    n = a.shape[-1]
    ri = lax.broadcasted_iota(jnp.int32, (n, n), 0)
    ci = lax.broadcasted_iota(jnp.int32, (n, n), 1)
    eye = (ri == ci).astype(F32)
    blk = lambda s: (ri // s) == (ci // s)
    ad = jnp.where(blk(16), a, 0.0)
    p = eye + ad
    for _ in range(3):
        ad = _bmm(ad, ad, P_SOLVE)
        p = p + _bmm(p, ad, P_SOLVE)
    s = 16
    while s < n:
        off = jnp.where(blk(2 * s) & ~blk(s), a, 0.0)
        p = p + _bmm(_bmm(p, off, P_SOLVE), p, P_SOLVE)
        s *= 2
    return p


P_SOLVE, P_STATE, P_OUT, P_GRAD, P_DECAY = 1, 3, 1, 1, 3


def _chunk_common(r, lw, k, a, b):
    n = r.shape[1]
    tri_incl = jnp.broadcast_to(_tri(n, False).astype(BF16), (r.shape[0], n, n))
    cum = _bmm_01(tri_incl, lw)
    e_pos, e_neg, e_exc = jnp.exp(cum), jnp.exp(-cum), jnp.exp(cum - lw)
    last = lax.broadcasted_iota(jnp.int32, (n, r.shape[2]), 0) == n - 1
    g_last = jnp.exp(jnp.sum(jnp.where(last, cum, 0.0), axis=1, keepdims=True))
    return g_last, r * e_pos, a * e_exc, b * e_neg, k * e_neg, e_pos, e_neg, e_exc


def _chunk_solve(rt, at, bt, kt, v, g0):
    strict, incl = _tri(rt.shape[1], True), _tri(rt.shape[1], False)
    a_ab = jnp.where(strict, _bmm_nt(at, bt, P_SOLVE), 0.0)
    a_ak = jnp.where(strict, _bmm_nt(at, kt, P_SOLVE), 0.0)
    a_rb = jnp.where(incl, _bmm_nt(rt, bt, P_OUT), 0.0)
    a_rk = jnp.where(incl, _bmm_nt(rt, kt, P_OUT), 0.0)
    tinv = _unit_lower_inverse(a_ab)
    u = _bmm(tinv, _bmm(at, g0, P_SOLVE) + _bmm(a_ak, v, P_SOLVE), P_SOLVE)
    return a_ab, a_ak, a_rb, a_rk, tinv, u


def _diag_col(row, n):
    ri = lax.broadcasted_iota(jnp.int32, (n, n), 0)
    ci = lax.broadcasted_iota(jnp.int32, (n, n), 1)
    return jnp.sum(jnp.where(ri == ci, row, 0.0), axis=2, keepdims=True)


def _diag_row(col, n):
    ri = lax.broadcasted_iota(jnp.int32, (n, n), 0)
    ci = lax.broadcasted_iota(jnp.int32, (n, n), 1)
    return jnp.sum(jnp.where(ri == ci, col, 0.0), axis=1, keepdims=True)


def _rwkv_scan_fwd(r, lw, k, v, a, b, hb):
    h, t, n = r.shape
    nc = t // CHUNK

    def body(r_ref, lw_ref, k_ref, v_ref, a_ref, b_ref, y_ref, st_ref, g_sc):
        @pl.when(pl.program_id(1) == 0)
        def _():
            g_sc[...] = jnp.zeros_like(g_sc)

        g0 = g_sc[...]
        st_ref[0] = g0
        vv = v_ref[...]
        g_last, rt, at, bt, kt, _, _, _ = _chunk_common(r_ref[...], lw_ref[...], k_ref[...], a_ref[...], b_ref[...])
        _, _, a_rb, a_rk, _, u = _chunk_solve(rt, at, bt, kt, vv, g0)
        y_ref[...] = _bmm(rt, g0, P_OUT) + _bmm(a_rb, u, P_OUT) + _bmm(a_rk, vv, P_OUT)
        z = g0 + _bmm_tn(bt, u, P_STATE) + _bmm_tn(kt, vv, P_STATE)
        g_sc[...] = _diag_col(g_last, n) * z

    blk = pl.BlockSpec((hb, CHUNK, n), lambda i, j: (i, j, 0))
    return pl.pallas_call(
        body, name="rwkv_scan_fwd", grid=(h // hb, nc),
        in_specs=[blk] * 6,
        out_specs=[blk, pl.BlockSpec((1, hb, n, n), lambda i, j: (j, i, 0, 0))],
        out_shape=[jax.ShapeDtypeStruct((h, t, n), F32), jax.ShapeDtypeStruct((nc, h, n, n), F32)],
        scratch_shapes=[pltpu.VMEM((hb, n, n), F32)],
        compiler_params=_params("parallel", "arbitrary"),
    )(r, lw, k, v, a, b)


def _rwkv_scan_bwd(r, lw, k, v, a, b, states, dy, hb):
    h, t, n = r.shape
    nc = t // CHUNK

    def body(r_ref, lw_ref, k_ref, v_ref, a_ref, b_ref, st_ref, dy_ref,
             dr_ref, dlw_ref, dk_ref, dv_ref, da_ref, db_ref, dg_sc):
        @pl.when(pl.program_id(1) == 0)
        def _():
            dg_sc[...] = jnp.zeros_like(dg_sc)

        g0 = st_ref[0]
        vv, dyv, dh = v_ref[...], dy_ref[...], dg_sc[...]
        lwv = lw_ref[...]
        g_last, rt, at, bt, kt, e_pos, e_neg, e_exc = _chunk_common(r_ref[...], lwv, k_ref[...], a_ref[...], b_ref[...])
        a_ab, a_ak, a_rb, a_rk, tinv, u = _chunk_solve(rt, at, bt, kt, vv, g0)
        strict, incl = _tri(CHUNK, True), _tri(CHUNK, False)
        gcol = _diag_col(g_last, n)
        z = g0 + _bmm_tn(bt, u, P_STATE) + _bmm_tn(kt, vv, P_STATE)
        dz = gcol * dh
        dc_last = _diag_row(jnp.sum(dh * gcol * z, axis=2, keepdims=True), n)
        g = P_GRAD
        du = _bmm_tn(a_rb, dyv, g) + _bmm(bt, dz, g)
        dx = _bmm_tn(tinv, du, P_SOLVE)
        dv_ref[...] = _bmm_tn(a_rk, dyv, g) + _bmm(kt, dz, g) + _bmm_tn(a_ak, dx, g)
        da_ab = jnp.where(strict, _bmm_nt(dx, u, g), 0.0)
        da_ak = jnp.where(strict, _bmm_nt(dx, vv, g), 0.0)
        da_rb = jnp.where(incl, _bmm_nt(dyv, u, g), 0.0)
        da_rk = jnp.where(incl, _bmm_nt(dyv, vv, g), 0.0)
        g = P_DECAY
        d_at = _bmm(da_ab, bt, g) + _bmm(da_ak, kt, g) + _bmm_nt(dx, g0, g)
        d_rt = _bmm(da_rb, bt, g) + _bmm(da_rk, kt, g) + _bmm_nt(dyv, g0, g)
        d_bt = _bmm_tn(da_ab, at, g) + _bmm_tn(da_rb, rt, g) + _bmm_nt(u, dz, g)
        d_kt = _bmm_tn(da_ak, at, g) + _bmm_tn(da_rk, rt, g) + _bmm_nt(vv, dz, g)
        dg_sc[...] = dz + _bmm_tn(rt, dyv, P_STATE) + _bmm_tn(at, dx, P_STATE)
        dr_ref[...] = d_rt * e_pos
        da_ref[...] = d_at * e_exc
        db_ref[...] = d_bt * e_neg
        dk_ref[...] = d_kt * e_neg
        last = lax.broadcasted_iota(jnp.int32, (CHUNK, n), 0) == CHUNK - 1
        dc = d_rt * rt - d_bt * bt - d_kt * kt + jnp.where(last, dc_last, 0.0)
        dce = d_at * at
        ri = lax.broadcasted_iota(jnp.int32, (CHUNK, CHUNK), 0)
        ci = lax.broadcasted_iota(jnp.int32, (CHUNK, CHUNK), 1)
        up_incl = jnp.broadcast_to((ri <= ci).astype(BF16), (hb, CHUNK, CHUNK))
        dlw_ref[...] = _bmm_01(up_incl, dc + dce) - dce

    rev = lambda i, j: (i, nc - 1 - j, 0)
    blk = pl.BlockSpec((hb, CHUNK, n), rev)
    return pl.pallas_call(
        body, name="rwkv_scan_bwd", grid=(h // hb, nc),
        in_specs=[blk] * 6 + [pl.BlockSpec((1, hb, n, n), lambda i, j: (nc - 1 - j, i, 0, 0)), blk],
        out_specs=[blk] * 6,
        out_shape=[jax.ShapeDtypeStruct((h, t, n), F32)] * 6,
        scratch_shapes=[pltpu.VMEM((hb, n, n), F32)],
        compiler_params=_params("parallel", "arbitrary"),
    )(r, lw, k, v, a, b, states, dy)


def _silu_grad(g):
    s = _sig(g)
    return s * (1.0 + g * (1.0 - s))


def _group_norm(ys):
    yc = ys - _head_sum(ys) * (1.0 / HEAD_DIM)
    rstd = lax.rsqrt(_head_sum(yc * yc) * (1.0 / HEAD_DIM) + GN_EPS)
    return yc * rstd, rstd


def _rwkv_post_fwd(ys, r, km, v, p, c, ln_w, ln_b, r_k):
    t = ys.shape[0]
    tm = _tile(t, 512, 8)
    goff = c.o_grw // LANES

    def body(ys_ref, r_ref, km_ref, v_ref, g_ref, lw_ref, lb_ref, rk_ref, o_ref):
        yn, _ = _group_norm(ys_ref[...])
        s = _head_sum(r_ref[...] * km_ref[...] * rk_ref[...])
        g = g_ref[...]
        o_ref[...] = ((yn * lw_ref[...] + lb_ref[...] + s * v_ref[...]) * g * _sig(g)).astype(BF16)

    blk = pl.BlockSpec((tm, LANES), lambda i, j: (i, j))
    vec = pl.BlockSpec((1, LANES), lambda i, j: (0, j))
    return pl.pallas_call(
        body, name="rwkv_post_fwd", grid=(t // tm, c.rw // LANES),
        in_specs=[blk] * 4 + [pl.BlockSpec((tm, LANES), lambda i, j: (i, goff + j)), vec, vec, vec],
        out_specs=blk, out_shape=jax.ShapeDtypeStruct((t, c.d), BF16),
        compiler_params=_params("parallel", "parallel"),
    )(ys, r, km, v, p, ln_w, ln_b, r_k)


def _rwkv_post_bwd(dyc, ys, r, km, v, p, c, ln_w, ln_b, r_k):
    t = ys.shape[0]
    tm = _tile(t, 512, 8)
    goff = c.o_grw // LANES

    def body(dy_ref, ys_ref, r_ref, km_ref, v_ref, g_ref, lw_ref, lb_ref, rk_ref,
             dys_ref, dr_ref, dkm_ref, dv_ref, dg_ref, dlw_ref, dlb_ref, drk_ref):
        @pl.when(pl.program_id(1) == 0)
        def _():
            for ref in (dlw_ref, dlb_ref, drk_ref):
                ref[...] = jnp.zeros_like(ref)

        yn, rstd = _group_norm(ys_ref[...])
        rv, kmv, vv, rk, g = r_ref[...], km_ref[...], v_ref[...], rk_ref[...], g_ref[...]
        s = _head_sum(rv * kmv * rk)
        y = yn * lw_ref[...] + lb_ref[...] + s * vv
        dyc = dy_ref[...]
        dg_ref[...] = (dyc * y * _silu_grad(g)).astype(BF16)
        dy = dyc * g * _sig(g)
        dlb_ref[...] += jnp.sum(dy, axis=0, keepdims=True)
        dlw_ref[...] += jnp.sum(dy * yn, axis=0, keepdims=True)
        dyn = dy * lw_ref[...]
        inv = 1.0 / HEAD_DIM
        dys_ref[...] = rstd * (dyn - _head_sum(dyn) * inv - yn * _head_sum(dyn * yn) * inv)
        ds = _head_sum(dy * vv)
        dv_ref[...] = dy * s
        dr_ref[...] = ds * kmv * rk
        dkm_ref[...] = ds * rv * rk
        drk_ref[...] += jnp.sum(ds * rv * kmv, axis=0, keepdims=True)

    blk = pl.BlockSpec((tm, LANES), lambda j, i: (i, j))
    vec = pl.BlockSpec((1, LANES), lambda j, i: (0, j))
    f = jax.ShapeDtypeStruct((t, c.rw), F32)
    s1 = jax.ShapeDtypeStruct((1, c.rw), F32)
    gate = pl.BlockSpec((tm, LANES), lambda j, i: (i, goff + j))
    return pl.pallas_call(
        body, name="rwkv_post_bwd", grid=(c.rw // LANES, t // tm),
        in_specs=[blk] * 5 + [gate, vec, vec, vec],
        out_specs=[blk] * 4 + [gate] + [vec] * 3,
        out_shape=[f, f, f, f, jax.ShapeDtypeStruct((t, c.wp), BF16), s1, s1, s1],
        compiler_params=_params("parallel", "arbitrary"),
    )(dyc, ys, r, km, v, p, ln_w, ln_b, r_k)


def _gate_fwd(y, p, goff, name, ycat, yoff):
    t, w = y.shape
    tm = _tile(t, 512, 8)
    gb, ob = goff // LANES, yoff // LANES

    def body(y_ref, g_ref, o_ref):
        g = g_ref[...]
        o_ref[...] = (y_ref[...] * g * _sig(g)).astype(BF16)

    blk = pl.BlockSpec((tm, LANES), lambda i, j: (i, j))
    return _pallas_into(
        body, ycat, 2, 0, name=name, grid=(t // tm, w // LANES),
        in_specs=[blk, pl.BlockSpec((tm, LANES), lambda i, j: (i, gb + j))],
        out_specs=pl.BlockSpec((tm, LANES), lambda i, j: (i, ob + j)),
        out_shape=jax.ShapeDtypeStruct(ycat.shape, BF16),
        compiler_params=_params("parallel", "parallel"),
    )(y, p)


def _gate_bwd(dyc, yoff, y, p, goff, name, dp):
    t, w = y.shape
    tm = _tile(t, 512, 8)
    gb, yb = goff // LANES, yoff // LANES

    def body(d_ref, y_ref, g_ref, dy_ref, dg_ref):
        g, d = g_ref[...], d_ref[...]
        dy_ref[...] = d * g * _sig(g)
        dg_ref[...] = (d * y_ref[...] * _silu_grad(g)).astype(BF16)

    blk = pl.BlockSpec((tm, LANES), lambda i, j: (i, j))
    gate = pl.BlockSpec((tm, LANES), lambda i, j: (i, gb + j))
    return _pallas_into(
        body, dp, 3, 1, name=name, grid=(t // tm, w // LANES),
        in_specs=[pl.BlockSpec((tm, LANES), lambda i, j: (i, yb + j)), blk, gate],
        out_specs=[blk, gate],
        out_shape=[jax.ShapeDtypeStruct((t, w), F32), jax.ShapeDtypeStruct(dp.shape, BF16)],
        compiler_params=_params("parallel", "parallel"),
    )(dyc, y, p)


NEG = -1e30


def _fox_prep(p, c, b_f):
    t = p.shape[0]
    tm = _tile(t, 512, 8)
    fb = c.o_fl // LANES

    def body(f_ref, b_ref, o_ref, carry):
        @pl.when(pl.program_id(0) == 0)
        def _():
            carry[...] = jnp.zeros_like(carry)

        logf = -_softplus(-(f_ref[...] + b_ref[...]))
        cum = _mm(_tri(tm, False).astype(F32), logf, HI) + carry[...]
        o_ref[...] = cum
        carry[...] += jnp.sum(logf, axis=0, keepdims=True)

    return pl.pallas_call(
        body, name="fox_prep", grid=(t // tm,),
        in_specs=[pl.BlockSpec((tm, LANES), lambda i: (i, fb)), pl.BlockSpec((1, LANES), lambda i: (0, 0))],
        out_specs=pl.BlockSpec((tm, LANES), lambda i: (i, 0)),
        out_shape=jax.ShapeDtypeStruct((t, LANES), F32),
        scratch_shapes=[pltpu.VMEM((1, LANES), F32)],
        compiler_params=_params("arbitrary"),
    )(p, b_f)


def _fox_logit_bwd(dcum, p, c, b_f):
    t = p.shape[0]
    tm = _tile(t, 512, 8)
    fb = c.o_fl // LANES
    nt = t // tm

    def body(d_ref, f_ref, b_ref, o_ref, db_ref, carry):
        @pl.when(pl.program_id(0) == 0)
        def _():
            carry[...] = jnp.zeros_like(carry)
            db_ref[...] = jnp.zeros_like(db_ref)

        d = d_ref[0] + d_ref[1]
        dlogf = _mm(_tri(tm, False).astype(F32).T, d, HI) + carry[...]
        carry[...] += jnp.sum(d, axis=0, keepdims=True)
        df = dlogf * _sig(-(f_ref[...] + b_ref[...]))
        o_ref[...] = df.astype(BF16)
        db_ref[...] += jnp.sum(df, axis=0, keepdims=True)

    return pl.pallas_call(
        body, name="fox_logit_bwd", grid=(nt,),
        in_specs=[pl.BlockSpec((2, tm, LANES), lambda i: (0, nt - 1 - i, 0)),
                  pl.BlockSpec((tm, LANES), lambda i: (nt - 1 - i, fb)),
                  pl.BlockSpec((1, LANES), lambda i: (0, 0))],
        out_specs=[pl.BlockSpec((tm, LANES), lambda i: (nt - 1 - i, 0)), pl.BlockSpec((1, LANES), lambda i: (0, 0))],
        out_shape=[jax.ShapeDtypeStruct((t, LANES), BF16), jax.ShapeDtypeStruct((1, LANES), F32)],
        scratch_shapes=[pltpu.VMEM((1, LANES), F32)],
        compiler_params=_params("arbitrary"),
    )(dcum, p, b_f)


def _fox_scores(q, k, cq, ck, qi, ki, tq, tk):
    s = _mm_nt((q * (HEAD_DIM ** -0.5)).astype(BF16), k.astype(BF16)) + cq - ck
    qpos = qi * tq + lax.broadcasted_iota(jnp.int32, (tq, tk), 0)
    kpos = ki * tk + lax.broadcasted_iota(jnp.int32, (tq, tk), 1)
    mask = kpos <= qpos
    return jnp.where(mask, s, NEG), mask


def _fox_fwd(q, k, v, cq, ck, hb, tb):
    h, t, n = q.shape
    tq = tk = _tile(t, tb, LANES)
    nq = t // tq

    def body(q_ref, k_ref, v_ref, cq_ref, ck_ref, o_ref, lse_ref, m_sc, l_sc, acc_sc):
        qi, ki = pl.program_id(1), pl.program_id(2)

        @pl.when(ki == 0)
        def _():
            m_sc[...] = jnp.full_like(m_sc, NEG)
            l_sc[...] = jnp.zeros_like(l_sc)
            acc_sc[...] = jnp.zeros_like(acc_sc)

        @pl.when(ki <= qi)
        def _():
            for i in range(hb):
                s, _ = _fox_scores(q_ref[i], k_ref[i], cq_ref[i], ck_ref[i], qi, ki, tq, tk)
                m_old = m_sc[i]
                m_new = jnp.maximum(m_old, jnp.max(s, axis=1, keepdims=True))
                scale = jnp.exp(m_old - m_new)
                e = jnp.exp(s - m_new)
                l_sc[i] = scale * l_sc[i] + jnp.sum(e, axis=1, keepdims=True)
                acc_sc[i] = scale * acc_sc[i] + _mm(e.astype(BF16), v_ref[i].astype(BF16))
                m_sc[i] = m_new

        @pl.when(ki == qi)
        def _():
            o_ref[...] = acc_sc[...] / l_sc[...]
            lse_ref[...] = m_sc[...] + jnp.log(l_sc[...])

    qb = pl.BlockSpec((hb, tq, n), lambda g, i, j: (g, i, 0))
    kb = pl.BlockSpec((hb, tk, n), lambda g, i, j: (g, jnp.minimum(i, j), 0))
    col = pl.BlockSpec((hb, tq, 1), lambda g, i, j: (g, i, 0))
    return pl.pallas_call(
        body, name="fox_fwd", grid=(h // hb, nq, nq),
        in_specs=[qb, kb, kb, col, pl.BlockSpec((hb, 1, tk), lambda g, i, j: (g, 0, jnp.minimum(i, j)))],
        out_specs=[qb, col],
        out_shape=[jax.ShapeDtypeStruct((h, t, n), F32), jax.ShapeDtypeStruct((h, t, 1), F32)],
        scratch_shapes=[pltpu.VMEM((hb, tq, 1), F32), pltpu.VMEM((hb, tq, 1), F32), pltpu.VMEM((hb, tq, n), F32)],
        compiler_params=_params("parallel", "parallel", "arbitrary"),
    )(q, k, v, cq, ck)


def _fox_bwd_dq(q, k, v, cq, ck, lse, o, do, hb, tb):
    h, t, n = q.shape
    tq = tk = _tile(t, tb, LANES)
    nq = t // tq

    def body(q_ref, k_ref, v_ref, cq_ref, ck_ref, lse_ref, o_ref, do_ref, dq_ref, dcq_ref, acc_sc, row_sc):
        qi, ki = pl.program_id(1), pl.program_id(2)

        @pl.when(ki == 0)
        def _():
            acc_sc[...] = jnp.zeros_like(acc_sc)
            row_sc[...] = jnp.zeros_like(row_sc)

        @pl.when(ki <= qi)
        def _():
            for i in range(hb):
                s, mask = _fox_scores(q_ref[i], k_ref[i], cq_ref[i], ck_ref[i], qi, ki, tq, tk)
                dov = do_ref[i]
                delta = jnp.sum(dov * o_ref[i], axis=1, keepdims=True)
                pm = jnp.where(mask, jnp.exp(s - lse_ref[i]), 0.0)
                dp = _mm_nt(dov.astype(BF16), v_ref[i].astype(BF16))
                ds = pm * (dp - delta)
                acc_sc[i] += _mm(ds.astype(BF16), k_ref[i].astype(BF16))
                row_sc[i] += jnp.sum(ds, axis=1, keepdims=True)

        @pl.when(ki == qi)
        def _():
            dq_ref[...] = acc_sc[...] * (HEAD_DIM ** -0.5)
            dcq_ref[...] = row_sc[...]

    qb = pl.BlockSpec((hb, tq, n), lambda g, i, j: (g, i, 0))
    kb = pl.BlockSpec((hb, tk, n), lambda g, i, j: (g, jnp.minimum(i, j), 0))
    col = pl.BlockSpec((hb, tq, 1), lambda g, i, j: (g, i, 0))
    return pl.pallas_call(
        body, name="fox_bwd_dq", grid=(h // hb, nq, nq),
        in_specs=[qb, kb, kb, col, pl.BlockSpec((hb, 1, tk), lambda g, i, j: (g, 0, jnp.minimum(i, j))), col, qb, qb],
        out_specs=[qb, col],
        out_shape=[jax.ShapeDtypeStruct((h, t, n), F32), jax.ShapeDtypeStruct((h, t, 1), F32)],
        scratch_shapes=[pltpu.VMEM((hb, tq, n), F32), pltpu.VMEM((hb, tq, 1), F32)],
        compiler_params=_params("parallel", "parallel", "arbitrary"),
    )(q, k, v, cq, ck, lse, o, do)


def _fox_bwd_dkv(q, k, v, cq, ck, lse, o, do, hb, tb):
    h, t, n = q.shape
    tq = tk = _tile(t, tb, LANES)
    nq = t // tq

    def body(q_ref, k_ref, v_ref, cq_ref, ck_ref, lse_ref, o_ref, do_ref, dk_ref, dv_ref, dck_ref, dk_sc, dv_sc, dc_sc):
        ki, qi = pl.program_id(1), pl.program_id(2)

        @pl.when(qi == 0)
        def _():
            dk_sc[...] = jnp.zeros_like(dk_sc)
            dv_sc[...] = jnp.zeros_like(dv_sc)
            dc_sc[...] = jnp.zeros_like(dc_sc)

        @pl.when(qi >= ki)
        def _():
            for i in range(hb):
                s, mask = _fox_scores(q_ref[i], k_ref[i], cq_ref[i], ck_ref[i], qi, ki, tq, tk)
                dov = do_ref[i]
                delta = jnp.sum(dov * o_ref[i], axis=1, keepdims=True)
                pm = jnp.where(mask, jnp.exp(s - lse_ref[i]), 0.0)
                dob = dov.astype(BF16)
                dp = _mm_nt(dob, v_ref[i].astype(BF16))
                ds = pm * (dp - delta)
                dv_sc[i] += _mm_tn(pm.astype(BF16), dob)
                dk_sc[i] += _mm_tn(ds.astype(BF16), q_ref[i].astype(BF16))
                dc_sc[i] -= jnp.sum(ds, axis=0, keepdims=True)

        @pl.when(qi == nq - 1)
        def _():
            dk_ref[...] = dk_sc[...] * (HEAD_DIM ** -0.5)
            dv_ref[...] = dv_sc[...]
            dck_ref[...] = dc_sc[...]

    qb = pl.BlockSpec((hb, tq, n), lambda g, j, i: (g, jnp.maximum(i, j), 0))
    kb = pl.BlockSpec((hb, tk, n), lambda g, j, i: (g, j, 0))
    col = pl.BlockSpec((hb, tq, 1), lambda g, j, i: (g, jnp.maximum(i, j), 0))
    row = pl.BlockSpec((hb, 1, tk), lambda g, j, i: (g, 0, j))
    return pl.pallas_call(
        body, name="fox_bwd_dkv", grid=(h // hb, nq, nq),
        in_specs=[qb, kb, kb, col, row, col, qb, qb],
        out_specs=[kb, kb, row],
        out_shape=[jax.ShapeDtypeStruct((h, t, n), F32), jax.ShapeDtypeStruct((h, t, n), F32),
                   jax.ShapeDtypeStruct((h, 1, t), F32)],
        scratch_shapes=[pltpu.VMEM((hb, tk, n), F32), pltpu.VMEM((hb, tk, n), F32), pltpu.VMEM((hb, 1, tk), F32)],
        compiler_params=_params("parallel", "parallel", "arbitrary"),
    )(q, k, v, cq, ck, lse, o, do)


FOX_PAIRS = 2
FOX_HEADS_STEP = 2 * FOX_PAIRS


def _lane_half(shape, upper):
    li = lax.broadcasted_iota(jnp.int32, shape, len(shape) - 1)
    return (li >= HEAD_DIM) if upper else (li < HEAD_DIM)


def _col(block, j):
    li = lax.broadcasted_iota(jnp.int32, block.shape, 1)
    return jnp.sum(jnp.where(li == j, block, 0.0), axis=1, keepdims=True)


def _from_cols(cols):
    li = lax.broadcasted_iota(jnp.int32, (cols[0].shape[0], len(cols)), 1)
    out = jnp.zeros(li.shape, F32)
    for j, cj in enumerate(cols):
        out = jnp.where(li == j, cj, out)
    return out


def _from_rows(rows):
    si = lax.broadcasted_iota(jnp.int32, (len(rows), rows[0].shape[1]), 0)
    out = jnp.zeros(si.shape, F32)
    for j, rj in enumerate(rows):
        out = jnp.where(si == j, rj, out)
    return out


def _causal(tq, tk):
    return lax.broadcasted_iota(jnp.int32, (tq, tk), 1) <= lax.broadcasted_iota(jnp.int32, (tq, tk), 0)


def _fox_prep_t(p, c, b_f):
    t = p.shape[0]
    tm = _tile(t, 512, LANES)
    fb = c.o_fl // LANES

    def body(f_ref, b_ref, o_ref, carry):
        @pl.when(pl.program_id(0) == 0)
        def _():
            carry[...] = jnp.zeros_like(carry)

        logf = -_softplus(-(f_ref[...] + b_ref[...]))
        cum = _mm(_tri(tm, False).astype(F32), logf, HI) + carry[...]
        o_ref[...] = cum.T
        carry[...] += jnp.sum(logf, axis=0, keepdims=True)

    return pl.pallas_call(
        body, name="fox_prep", grid=(t // tm,),
        in_specs=[pl.BlockSpec((tm, LANES), lambda i: (i, fb)), pl.BlockSpec((1, LANES), lambda i: (0, 0))],
        out_specs=pl.BlockSpec((LANES, tm), lambda i: (0, i)),
        out_shape=jax.ShapeDtypeStruct((LANES, t), F32),
        scratch_shapes=[pltpu.VMEM((1, LANES), F32)],
        compiler_params=_params("arbitrary"),
    )(p, b_f)


def _fox2_fwd(p, c, cum_t, tb, ycat):
    t = p.shape[0]
    tq = tk = _tile(t, tb, LANES)
    nq = t // tq
    pw, nh = FOX_PAIRS * LANES, FOX_HEADS_STEP
    qb, kb, vb, gb = (o // pw for o in (c.o_fq, c.o_fk, c.o_fv, c.o_gfox))
    scale = HEAD_DIM ** -0.5

    def body(q_ref, k_ref, v_ref, g_ref, ck_ref, o_ref, y_ref, lse_ref, m_sc, l_sc, acc_sc):
        g, qi, ki = pl.program_id(0), pl.program_id(1), pl.program_id(2)

        @pl.when(ki == 0)
        def _():
            m_sc[...] = jnp.full_like(m_sc, NEG)
            l_sc[...] = jnp.zeros_like(l_sc)
            acc_sc[...] = jnp.zeros_like(acc_sc)

        def step(diag):
            ms, ls = [m_sc[h] for h in range(nh)], [l_sc[h] for h in range(nh)]
            accs = [acc_sc[:, pi * LANES:(pi + 1) * LANES] for pi in range(FOX_PAIRS)]
            for pi in range(FOX_PAIRS):
                lanes = slice(pi * LANES, (pi + 1) * LANES)
                q2 = (q_ref[:, lanes] * scale).astype(BF16)
                k2, v2 = k_ref[:, lanes].astype(BF16), v_ref[:, lanes].astype(BF16)
                new_acc = accs[pi]
                for hh in range(2):
                    hi = 2 * pi + hh
                    mk = _lane_half((tq, LANES), hh == 1)
                    s = _mm_nt(jnp.where(mk, q2, jnp.zeros_like(q2)), k2) - ck_ref[pl.ds(g * nh + hi, 1), :]
                    if diag:
                        s = jnp.where(_causal(tq, tk), s, NEG)
                    m_new = jnp.maximum(ms[hi], jnp.max(s, axis=1, keepdims=True))
                    a = jnp.exp(ms[hi] - m_new)
                    e = jnp.exp(s - jnp.concatenate([m_new] * (tk // LANES), axis=1))
                    ls[hi] = a * ls[hi] + jnp.sum(e, axis=1, keepdims=True)
                    ms[hi] = m_new
                    new_acc = jnp.where(mk, a * accs[pi] + _mm(e.astype(BF16), v2), new_acc)
                accs[pi] = new_acc
            for h in range(nh):
                m_sc[h] = ms[h]
                l_sc[h] = ls[h]
            for pi in range(FOX_PAIRS):
                acc_sc[:, pi * LANES:(pi + 1) * LANES] = accs[pi]

        @pl.when(ki < qi)
        def _():
            step(False)

        @pl.when(ki == qi)
        def _():
            step(True)
            li = lax.broadcasted_iota(jnp.int32, (tq, LANES), 1)
            lse = jnp.zeros((tq, LANES), F32)
            for pi in range(FOX_PAIRS):
                lanes = slice(pi * LANES, (pi + 1) * LANES)
                inv = jnp.where(_lane_half((tq, LANES), False), 1.0 / l_sc[2 * pi], 1.0 / l_sc[2 * pi + 1])
                o = acc_sc[:, lanes] * inv
                gate = g_ref[:, lanes]
                o_ref[:, lanes] = o
                y_ref[:, lanes] = (o * gate * _sig(gate)).astype(BF16)
            for h in range(nh):
                lse = jnp.where(li == h, m_sc[h] + jnp.log(l_sc[h]), lse)
            lse_ref[0] = lse

    row = lambda off: pl.BlockSpec((tq, pw), lambda g, i, j: (i, off + g))
    key = lambda off: pl.BlockSpec((tk, pw), lambda g, i, j: (jnp.minimum(i, j), off + g))
    out = pl.BlockSpec((tq, pw), lambda g, i, j: (i, g))
    return _pallas_into(
        body, ycat, 5, 1, name="fox_fwd", grid=(c.rw // pw, nq, nq),
        in_specs=[row(qb), key(kb), key(vb), row(gb),
                  pl.BlockSpec((LANES, tk), lambda g, i, j: (0, jnp.minimum(i, j)))],
        out_specs=[out, row(c.rw // pw), pl.BlockSpec((1, tq, LANES), lambda g, i, j: (g, i, 0))],
        out_shape=[jax.ShapeDtypeStruct((t, c.rw), F32), jax.ShapeDtypeStruct(ycat.shape, BF16),
                   jax.ShapeDtypeStruct((c.rw // pw, t, LANES), F32)],
        scratch_shapes=[pltpu.VMEM((nh, tq, LANES), F32), pltpu.VMEM((nh, tq, LANES), F32),
                        pltpu.VMEM((tq, pw), F32)],
        compiler_params=_params("parallel", "parallel", "arbitrary"),
    )(p, p, p, p, cum_t)


def _fox2_grads(q2, k2, v2, do2, o2, lse_h, ck, mk, diag, tq, tk):
    zero = jnp.zeros_like(q2)
    s = _mm_nt(jnp.where(mk, q2, zero), k2) - ck
    if diag:
        s = jnp.where(_causal(tq, tk), s, NEG)
    wide = lambda col: jnp.concatenate([jnp.broadcast_to(col, (tq, LANES))] * (tk // LANES), axis=1)
    pm = jnp.exp(s - wide(lse_h))
    delta = jnp.sum(jnp.where(mk, do2 * o2, 0.0), axis=1, keepdims=True)
    dob = do2.astype(BF16)
    dp = _mm_nt(jnp.where(mk, dob, zero), v2)
    return pm, pm * (dp - wide(delta)), dob


def _fox2_bwd_dq(p, c, cum_t, lse, o, do, tb, dp):
    t = p.shape[0]
    tq = tk = _tile(t, tb, LANES)
    nq = t // tq
    pw, nh = FOX_PAIRS * LANES, FOX_HEADS_STEP
    qb, kb, vb = (o_ // pw for o_ in (c.o_fq, c.o_fk, c.o_fv))
    scale = HEAD_DIM ** -0.5

    def body(q_ref, k_ref, v_ref, ck_ref, lse_ref, o_ref, do_ref, dq_ref, dcq_ref, acc_sc, row_sc):
        g, qi, ki = pl.program_id(0), pl.program_id(1), pl.program_id(2)

        @pl.when(ki == 0)
        def _():
            acc_sc[...] = jnp.zeros_like(acc_sc)
            row_sc[...] = jnp.zeros_like(row_sc)

        def step(diag):
            lse_blk = lse_ref[0]
            rows = [row_sc[h] for h in range(nh)]
            accs = [acc_sc[:, pi * LANES:(pi + 1) * LANES] for pi in range(FOX_PAIRS)]
            for pi in range(FOX_PAIRS):
                lanes = slice(pi * LANES, (pi + 1) * LANES)
                q2 = (q_ref[:, lanes] * scale).astype(BF16)
                k2, v2 = k_ref[:, lanes].astype(BF16), v_ref[:, lanes].astype(BF16)
                do2, o2 = do_ref[:, lanes], o_ref[:, lanes]
                new_acc = accs[pi]
                for hh in range(2):
                    hi = 2 * pi + hh
                    mk = _lane_half((tq, LANES), hh == 1)
                    _, ds, _ = _fox2_grads(q2, k2, v2, do2, o2, _col(lse_blk, hi),
                                           ck_ref[pl.ds(g * nh + hi, 1), :], mk, diag, tq, tk)
                    rows[hi] = rows[hi] + jnp.sum(ds, axis=1, keepdims=True)
                    new_acc = jnp.where(mk, accs[pi] + _mm(ds.astype(BF16), k2), new_acc)
                accs[pi] = new_acc
            for h in range(nh):
                row_sc[h] = rows[h]
            for pi in range(FOX_PAIRS):
                acc_sc[:, pi * LANES:(pi + 1) * LANES] = accs[pi]

        @pl.when(ki < qi)
        def _():
            step(False)

        @pl.when(ki == qi)
        def _():
            step(True)
            dq_ref[...] = (acc_sc[...] * scale).astype(BF16)
            dcq_ref[0] = _from_cols([row_sc[h] for h in range(nh)])

    row = lambda off: pl.BlockSpec((tq, pw), lambda g, i, j: (i, off + g))
    key = lambda off: pl.BlockSpec((tk, pw), lambda g, i, j: (jnp.minimum(i, j), off + g))
    stat = pl.BlockSpec((1, tq, nh), lambda g, i, j: (g, i, 0))
    return _pallas_into(
        body, dp, 7, 0, name="fox_bwd_dq", grid=(c.rw // pw, nq, nq),
        in_specs=[row(qb), key(kb), key(vb), pl.BlockSpec((LANES, tk), lambda g, i, j: (0, jnp.minimum(i, j))),
                  pl.BlockSpec((1, tq, LANES), lambda g, i, j: (g, i, 0)), row(0), row(0)],
        out_specs=[row(qb), stat],
        out_shape=[jax.ShapeDtypeStruct(dp.shape, BF16), jax.ShapeDtypeStruct((c.rw // pw, t, nh), F32)],
        scratch_shapes=[pltpu.VMEM((tq, pw), F32), pltpu.VMEM((nh, tq, 1), F32)],
        compiler_params=_params("parallel", "parallel", "arbitrary"),
    )(p, p, p, cum_t, lse, o, do)


def _fox2_bwd_dkv(p, c, cum_t, lse, o, do, tb, dp):
    t = p.shape[0]
    tq = tk = _tile(t, tb, LANES)
    nq = t // tq
    pw, nh = FOX_PAIRS * LANES, FOX_HEADS_STEP
    qb, kb, vb = (o_ // pw for o_ in (c.o_fq, c.o_fk, c.o_fv))
    scale = HEAD_DIM ** -0.5

    def body(q_ref, k_ref, v_ref, ck_ref, lse_ref, o_ref, do_ref, dk_ref, dv_ref, dck_ref, dk_sc, dv_sc, dc_sc):
        g, ki, qi = pl.program_id(0), pl.program_id(1), pl.program_id(2)

        @pl.when(qi == 0)
        def _():
            dk_sc[...] = jnp.zeros_like(dk_sc)
            dv_sc[...] = jnp.zeros_like(dv_sc)
            dc_sc[...] = jnp.zeros_like(dc_sc)

        def step(diag):
            lse_blk = lse_ref[0]
            dcs = [dc_sc[h] for h in range(nh)]
            dks = [dk_sc[:, pi * LANES:(pi + 1) * LANES] for pi in range(FOX_PAIRS)]
            dvs = [dv_sc[:, pi * LANES:(pi + 1) * LANES] for pi in range(FOX_PAIRS)]
            for pi in range(FOX_PAIRS):
                lanes = slice(pi * LANES, (pi + 1) * LANES)
                q2 = (q_ref[:, lanes] * scale).astype(BF16)
                k2, v2 = k_ref[:, lanes].astype(BF16), v_ref[:, lanes].astype(BF16)
                do2, o2 = do_ref[:, lanes], o_ref[:, lanes]
                new_dk, new_dv = dks[pi], dvs[pi]
                for hh in range(2):
                    hi = 2 * pi + hh
                    mk = _lane_half((tk, LANES), hh == 1)
                    pm, ds, dob = _fox2_grads(q2, k2, v2, do2, o2, _col(lse_blk, hi),
                                              ck_ref[pl.ds(g * nh + hi, 1), :], mk, diag, tq, tk)
                    dcs[hi] = dcs[hi] - jnp.sum(ds, axis=0, keepdims=True)
                    new_dv = jnp.where(mk, dvs[pi] + _mm_tn(pm.astype(BF16), dob), new_dv)
                    new_dk = jnp.where(mk, dks[pi] + _mm_tn(ds.astype(BF16), q2), new_dk)
                dks[pi], dvs[pi] = new_dk, new_dv
            for h in range(nh):
                dc_sc[h] = dcs[h]
            for pi in range(FOX_PAIRS):
                dk_sc[:, pi * LANES:(pi + 1) * LANES] = dks[pi]
                dv_sc[:, pi * LANES:(pi + 1) * LANES] = dvs[pi]

        @pl.when(qi > ki)
        def _():
            step(False)

        @pl.when(qi == ki)
        def _():
            step(True)

        @pl.when(qi == nq - 1)
        def _():
            dk_ref[...] = dk_sc[...].astype(BF16)
            dv_ref[...] = dv_sc[...].astype(BF16)
            dck_ref[0] = _from_rows([dc_sc[h] for h in range(nh)])

    row = lambda off: pl.BlockSpec((tq, pw), lambda g, j, i: (jnp.maximum(i, j), off + g))
    key = lambda off: pl.BlockSpec((tk, pw), lambda g, j, i: (j, off + g))
    return _pallas_into(
        body, dp, 7, 0, name="fox_bwd_dkv", grid=(c.rw // pw, nq, nq),
        in_specs=[row(qb), key(kb), key(vb), pl.BlockSpec((LANES, tk), lambda g, j, i: (0, j)),
                  pl.BlockSpec((1, tq, LANES), lambda g, j, i: (g, jnp.maximum(i, j), 0)), row(0), row(0)],
        out_specs=[key(kb), key(0), pl.BlockSpec((1, nh, tk), lambda g, j, i: (g, 0, j))],
        out_shape=[jax.ShapeDtypeStruct(dp.shape, BF16), jax.ShapeDtypeStruct((t, c.rw), BF16),
                   jax.ShapeDtypeStruct((c.rw // pw, nh, t), F32)],
        scratch_shapes=[pltpu.VMEM((tk, pw), F32), pltpu.VMEM((tk, pw), F32), pltpu.VMEM((nh, 1, tk), F32)],
        compiler_params=_params("parallel", "parallel", "arbitrary"),
    )(p, p, p, cum_t, lse, o, do)


def _mem_probs(q, mk, scale):
    s = _mm_nt(q.astype(BF16), mk.astype(BF16)) * scale
    e = jnp.exp(s - jnp.max(s, axis=1, keepdims=True))
    return e / jnp.sum(e, axis=1, keepdims=True)


def _mem_attn_fwd(p, c, mkv):
    t = p.shape[0]
    tm = _tile(t, 512, 8)
    dh = c.mhd
    qb = c.o_mq // dh
    scale = dh ** -0.5

    def body(q_ref, mk_ref, mv_ref, o_ref):
        pm = _mem_probs(q_ref[...], mk_ref[...], scale)
        o_ref[...] = _mm(pm.astype(BF16), mv_ref[...].astype(BF16))

    m = mkv.shape[0]
    return pl.pallas_call(
        body, name="mem_attn_fwd", grid=(t // tm, MEM_HEADS),
        in_specs=[pl.BlockSpec((tm, dh), lambda i, j: (i, qb + j)),
                  pl.BlockSpec((m, dh), lambda i, j: (0, j)),
                  pl.BlockSpec((m, dh), lambda i, j: (0, MEM_HEADS + j))],
        out_specs=pl.BlockSpec((tm, dh), lambda i, j: (i, j)),
        out_shape=jax.ShapeDtypeStruct((t, c.mw), F32),
        compiler_params=_params("parallel", "parallel"),
    )(p, mkv, mkv)


def _mem_attn_bwd(p, c, mkv, do):
    t = p.shape[0]
    tm = _tile(t, 512, 8)
    dh = c.mhd
    qb = c.o_mq // dh
    scale = dh ** -0.5
    m = mkv.shape[0]

    def body(q_ref, mk_ref, mv_ref, do_ref, dq_ref, dmk_ref, dmv_ref):
        @pl.when(pl.program_id(1) == 0)
        def _():
            dmk_ref[...] = jnp.zeros_like(dmk_ref)
            dmv_ref[...] = jnp.zeros_like(dmv_ref)

        qv = q_ref[...].astype(BF16)
        pm = _mem_probs(qv, mk_ref[...], scale)
        dob = do_ref[...].astype(BF16)
        dmv_ref[...] += _mm_tn(pm.astype(BF16), dob)
        dp = _mm_nt(dob, mv_ref[...].astype(BF16))
        ds = (pm * (dp - jnp.sum(pm * dp, axis=1, keepdims=True)) * scale).astype(BF16)
        dq_ref[...] = _mm(ds, mk_ref[...].astype(BF16)).astype(BF16)
        dmk_ref[...] += _mm_tn(ds, qv)

    kvb = lambda off: pl.BlockSpec((m, dh), lambda j, i: (0, off + j))
    return pl.pallas_call(
        body, name="mem_attn_bwd", grid=(MEM_HEADS, t // tm),
        in_specs=[pl.BlockSpec((tm, dh), lambda j, i: (i, qb + j)), kvb(0), kvb(MEM_HEADS),
                  pl.BlockSpec((tm, dh), lambda j, i: (i, j))],
        out_specs=[pl.BlockSpec((tm, dh), lambda j, i: (i, j)), kvb(0), kvb(0)],
        out_shape=[jax.ShapeDtypeStruct((t, c.mw), BF16), jax.ShapeDtypeStruct((m, c.mw), F32),
                   jax.ShapeDtypeStruct((m, c.mw), F32)],
        compiler_params=_params("parallel", "arbitrary"),
    )(p, mkv, mkv, do)


def _adamw(w, g, m, v, name):
    rows, cols = w.shape
    tm = _tile(rows, max(8, (1 << 18) // cols // 8 * 8), 8)
    bc1 = 1.0 - ADAM_B1 ** ADAM_STEP
    bc2 = 1.0 - ADAM_B2 ** ADAM_STEP

    def body(w_ref, g_ref, m_ref, v_ref, go_ref, d_ref, mo_ref, vo_ref):
        gv = g_ref[:, 0:cols]
        mn = ADAM_B1 * m_ref[...] + (1.0 - ADAM_B1) * gv
        vn = ADAM_B2 * v_ref[...] + (1.0 - ADAM_B2) * (gv * gv)
        go_ref[...] = gv
        mo_ref[...] = mn
        vo_ref[...] = vn
        d_ref[...] = -ADAM_LR * ((mn / bc1) / (jnp.sqrt(vn / bc2) + ADAM_EPS) + ADAM_WD * w_ref[...])

    blk = pl.BlockSpec((tm, cols), lambda i: (i, 0))
    shp = jax.ShapeDtypeStruct((rows, cols), F32)
    return pl.pallas_call(
        body, name=name, grid=(rows // tm,),
        in_specs=[blk, pl.BlockSpec((tm, g.shape[1]), lambda i: (i, 0)), blk, blk],
        out_specs=[blk] * 4, out_shape=[shp] * 4,
        compiler_params=_params("parallel"),
    )(w, g, m, v)


SCAN_HEADS = 12
FOX_BLOCK = 512


def _local_step(c, x, mem, tgt, w, riders=None):
    t = x.shape[0]
    rw = c.rw
    riders = riders or {}
    carried = {}
    hd = lambda z: z.reshape(t, c.h, HEAD_DIM).transpose(1, 0, 2)
    uh = lambda z: z.transpose(1, 0, 2).reshape(t, rw)
    vecs = (w["mu"], w["w0"], w["a0"], w["k_k"], w["k_a"], w["wd"], w["wi"])

    h, rinv = _rms_fwd(x, w["g_pre"], "rms_pre")
    if "in_proj" in riders:
        groups, finish = riders["in_proj"]
        p, late = _matmul(h, w["wp"], name="in_proj", tk=4096, attach=groups)
        w = dict(w, **finish(late))
    else:
        p = _matmul(h, w["wp"], name="in_proj", tk=4096)
    r, lw, km, v, a, b = _rwkv_pre_fwd(p, c, *vecs)
    scan_in = tuple(hd(z) for z in (r, lw, km, v, a, b))
    hb = max(n for n in range(1, SCAN_HEADS + 1) if c.h % n == 0)
    ysh, states = _rwkv_scan_fwd(*scan_in, hb)
    ys = uh(ysh)
    ycat = _rwkv_post_fwd(ys, r, km, v, p, c, w["ln_w"], w["ln_b"], w["r_k"])

    cum_t = _fox_prep_t(p, c, w["b_f"])
    yfox, ycat, lse = _fox2_fwd(p, c, cum_t, FOX_BLOCK, ycat)

    memn, rinv_m = _rms_fwd(mem, w["g_mem"], "rms_mem")
    mkv = _matmul(memn, w["w_mem_kv"], name="mem_kv")
    ymem = _mem_attn_fwd(p, c, mkv)
    ycat = _gate_fwd(ymem, p, c.o_gmq, "gate_mem", ycat, 2 * rw)
    yo =_matmul(ycat, w["w_out"], name="out_proj", tn=512, tk=4096)
    loss, dout, dyo, dg_post = _post_loss(yo, x, tgt, w["g_post"], "post_loss")

    dyc = _matmul(dyo, w["w_out"], tb=True, name="d_ycat", tn=512, tk=4096)
    dw_out = _matmul(ycat, dyo, ta=True, name="d_w_out", tn=512, tk=4096, out_dtype=BF16)
    dys, dr2, dkm2, dv2, dp, dln_w, dln_b, dr_k = _rwkv_post_bwd(
        dyc, ys, r, km, v, p, c, w["ln_w"], w["ln_b"], w["r_k"])
    dyf, dp = _gate_bwd(dyc, rw, yfox, p, c.o_gfox, "gate_fox_bwd", dp)
    dym, dp = _gate_bwd(dyc, 2 * rw, ymem, p, c.o_gmq, "gate_mem_bwd", dp)

    scan_g = _rwkv_scan_bwd(*scan_in, states, hd(dys), hb)
    dps, dzw, dza, twb, alb, dw0, da0, dk_k, dk_a = _rwkv_pre_bwd(
        p, c, *vecs, *(uh(z) for z in scan_g), dr2, dkm2, dv2)
    dwd = _matmul(twb, dzw, ta=True, name="d_w_decay", out_dtype=BF16)
    dwi = _matmul(alb, dza, ta=True, name="d_w_iclr", out_dtype=BF16)
    dp, dmu = _shift_bwd(dps, p, c, w["mu"], dp)

    dp, dcq = _fox2_bwd_dq(p, c, cum_t, lse, yfox, dyf, FOX_BLOCK, dp)
    dp, dfv, dck = _fox2_bwd_dkv(p, c, cum_t, lse, yfox, dyf, FOX_BLOCK, dp)
    dcum = jnp.pad(jnp.stack([dcq.transpose(1, 0, 2).reshape(t, c.h), dck.reshape(c.h, t).T]),
                   ((0, 0), (0, 0), (0, LANES - c.h)))
    dfl, db_f = _fox_logit_bwd(dcum, p, c, w["b_f"])

    dmq, dmk, dmv = _mem_attn_bwd(p, c, mkv, dym)
    dmkv = jnp.concatenate([dmk, dmv], axis=1)
    dw_mkv = _matmul(memn, dmkv, ta=True, name="d_w_mem_kv", out_dtype=BF16)
    dmemn = _matmul(dmkv, w["w_mem_kv"], tb=True, name="d_memn")
    _, dg_mem = _rms_bwd(dmemn, mem, rinv_m, w["g_mem"], jnp.zeros_like(mem), "rms_mem_bwd")

    for off, piece in ((c.o_fv, dfv), (c.o_mq, dmq), (c.o_fl, dfl)):
        dp = lax.dynamic_update_slice(dp, piece, (0, off))
    rest = dict(wd=dwd, wi=dwi, w_mem_kv=dw_mkv, w_out=dw_out)
    if "d_w_in" in riders:
        dwp, carried["rest"] = _matmul(h, dp, ta=True, name="d_w_in", tk=4096, out_dtype=BF16,
                                       attach=riders["d_w_in"](rest))
    else:
        dwp = _matmul(h, dp, ta=True, name="d_w_in", tk=4096, out_dtype=BF16)
    if "d_h" in riders:
        dh, carried["wp"] = _matmul(dp, w["wp"], tb=True, name="d_h", tk=2944, attach=riders["d_h"](dwp))
    else:
        dh = _matmul(dp, w["wp"], tb=True, name="d_h", tk=2944)
    grad_x, dg_pre = _rms_bwd(dh, x, rinv, w["g_pre"], dout, "rms_pre_bwd")

    small = dict(g_pre=dg_pre, mu=dmu, w0=dw0, a0=da0, k_k=dk_k, k_a=dk_a, r_k=dr_k, ln_w=dln_w, ln_b=dln_b,
                 b_f=db_f, g_mem=dg_mem, g_post=dg_post)
    return loss, grad_x, dict(wp=dwp, **rest), small, carried


CHIPS = ((1, 0, 0), (0, 1, 0), (1, 1, 0))
SIBLING = ((0, 0, 1),)
ALL_PEERS = tuple((i, j, k) for i in (0, 1) for j in (0, 1) for k in (0, 1))[1:]


def _chip_of(pos):
    return 2 * pos[0] + pos[1]


DMA_CHUNK = 4 << 20


def _pieces(shape, itemsize):
    lead, (rows, cols) = shape[:-2], shape[-2:]
    k = 1
    if rows % 16 == 0:
        k = max(1, min(rows // 16, -(-rows * cols * itemsize // DMA_CHUNK)))
        while rows % k or (rows // k) % 16:
            k -= 1
    band = rows // k
    idxs = [()]
    for n in lead:
        idxs = [i + (j,) for i in idxs for j in range(n)]
    return [i + (pl.ds(j * band, band),) for i in idxs for j in range(k)]


def _peer_of(me, mask):
    return tuple(1 - v if f else v for v, f in zip(me, mask))


def _exchange(name, groups):
    n = len(groups)
    plan = _plan(groups)

    def body(*refs):
        copies = _copies(groups, plan, refs[:n], refs[n:2 * n], refs[2 * n], refs[2 * n + 1])
        for cp in copies:
            cp.start()
        for cp in copies:
            cp.wait()

    any_spec = pl.BlockSpec(memory_space=pl.ANY)
    return pl.pallas_call(
        body, name=name,
        in_specs=[any_spec] * n, out_specs=[any_spec] * n,
        out_shape=_exchange_shapes(groups),
        scratch_shapes=[pltpu.SemaphoreType.DMA((len(plan),)), pltpu.SemaphoreType.DMA((len(plan),))],
    )(*[g["src"] for g in groups])


def _plan(groups):
    return [(gi, ti, idx) for gi, g in enumerate(groups) for ti in range(len(g["transfers"]))
            for idx in _pieces(tuple(g["piece"]), g["src"].dtype.itemsize)]


def _exchange_shapes(groups):
    return [jax.ShapeDtypeStruct((g["slots"],) + tuple(g["piece"]), g["src"].dtype) for g in groups]


def _copies(groups, plan, srcs, outs, send_sems, recv_sems):
    me = (lax.axis_index("x"), lax.axis_index("y"), lax.axis_index("c"))
    copies = []
    for k, (gi, ti, idx) in enumerate(plan):
        mask, view, slot = groups[gi]["transfers"][ti]
        peer = _peer_of(me, mask)
        copies.append(pltpu.make_async_remote_copy(
            src_ref=view(srcs[gi], me, peer).at[idx], dst_ref=outs[gi].at[slot(me, peer)].at[idx],
            send_sem=send_sems.at[k], recv_sem=recv_sems.at[k],
            device_id=peer, device_id_type=MESH))
    return copies


def _my_chip():
    return 2 * lax.axis_index("x") + lax.axis_index("y")


def _put(buf, block, slot):
    return lax.dynamic_update_slice(buf, block[None], (slot,) + (0,) * block.ndim)


def _sum_slots(recv, own, k, out_dtype, name):
    s, rows, cols = recv.shape
    budget = max(16, ((4 << 20) // ((s + 1) * cols * 4)) // 16 * 16)
    tr = _tile(rows, budget, 16)
    own_many = own.shape[0] > 1

    def body(k_ref, *refs):
        out_ref = refs[s + 1]
        mine = refs[s][0].astype(F32)
        acc = None
        for i in range(s):
            term = jnp.where(k_ref[0] == i, mine, refs[i][0].astype(F32))
            acc = term if acc is None else acc + term
        out_ref[...] = acc.astype(out_ref.dtype)

    def slot_spec(i):
        return pl.BlockSpec((1, tr, cols), lambda j, kr: (jnp.where(kr[0] == i, (i + 1) % s, i), j, 0))

    grid_spec = pltpu.PrefetchScalarGridSpec(
        num_scalar_prefetch=1, grid=(rows // tr,),
        in_specs=[slot_spec(i) for i in range(s)]
                 + [pl.BlockSpec((1, tr, cols), lambda j, kr: (kr[0] if own_many else 0, j, 0))],
        out_specs=pl.BlockSpec((tr, cols), lambda j, kr: (j, 0)))
    return pl.pallas_call(
        body, name=name, grid_spec=grid_spec,
        out_shape=jax.ShapeDtypeStruct((rows, cols), out_dtype),
        compiler_params=_params("parallel"),
    )(k, *([recv] * s), own)


def _all_gather(shards):
    return _gather_finish(shards, _exchange("gather_chips", _gather_groups(shards)), "gather_pair")


def _gather_groups(shards):
    halves = [s.reshape(2, s.shape[0] // 2, s.shape[1]) for s in shards]
    return [dict(src=q, slots=4, piece=q.shape[1:],
                 transfers=[(m, lambda ref, me, peer: ref.at[me[2]], lambda me, peer: _chip_of(me)) for m in CHIPS])
            for q in halves]


def _gather_finish(shards, first, name):
    core, chip = lax.axis_index("c"), _my_chip()
    other_chip = lambda m: (lambda me: _chip_of(_peer_of(me, m)))
    second = _exchange(name, [
        dict(src=q, slots=4, piece=q.shape[1:],
             transfers=[(SIBLING[0], (lambda f: lambda ref, me, peer: ref.at[f(me)])(other_chip(m)),
                         (lambda f: lambda me, peer: f(me))(other_chip(m))) for m in CHIPS])
        for q in first])
    out = []
    for s, a, b in zip(shards, first, second):
        full = jnp.concatenate([jnp.where(core == 0, a, b), jnp.where(core == 0, b, a)], axis=1)
        out.append(_put(full, s, chip))
    return out


def _reduce_pair(partials, tag):
    core1 = lax.axis_index("c").reshape(1).astype(jnp.int32)
    halves = [q.reshape(4, 2, q.shape[1] // 2, q.shape[2]).transpose(1, 0, 2, 3) for q in partials]
    pair = _exchange("reduce_pair_" + tag, [
        dict(src=q, slots=2, piece=q.shape[1:],
             transfers=[(SIBLING[0], lambda ref, me, peer: ref.at[peer[2]], lambda me, peer: me[2])])
        for q in halves])
    flat = lambda e: e.reshape(2, -1, e.shape[-1])
    return [_sum_slots(flat(e), flat(q), core1, BF16, "reduce_pair_sum_" + tag).reshape(q.shape[1:])
            for e, q in zip(pair, halves)]


def _reduce_chips_groups(chip_sums):
    return [dict(src=q, slots=4, piece=q.shape[1:],
                 transfers=[(m, lambda ref, me, peer: ref.at[_chip_of(peer)], lambda me, peer: _chip_of(me))
                            for m in CHIPS])
            for q in chip_sums]


def _reduce_finish(crossed, chip_sums, tag):
    core = lax.axis_index("c")
    chip1 = _my_chip().reshape(1).astype(jnp.int32)
    sums = [_sum_slots(e, q, chip1, F32, "reduce_chips_sum_" + tag) for e, q in zip(crossed, chip_sums)]
    swapped = _exchange("reduce_swap_" + tag, [
        dict(src=q, slots=2, piece=q.shape, transfers=[(SIBLING[0], lambda ref, me, peer: ref, lambda me, peer: me[2])])
        for q in sums])
    return [_put(e, q, core).reshape(-1, e.shape[-1]) for e, q in zip(swapped, sums)]


def _reduce_scatter(partials):
    chip_sums = _reduce_pair(partials, "all")
    return _reduce_finish(_exchange("reduce_chips", _reduce_chips_groups(chip_sums)), chip_sums, "all")


def _all_reduce_small(vec):
    dev = 4 * lax.axis_index("x") + 2 * lax.axis_index("y") + lax.axis_index("c")
    got = _exchange("reduce_small", [
        dict(src=vec, slots=8, piece=vec.shape,
             transfers=[(m, lambda ref, me, peer: ref, lambda me, peer: 4 * me[0] + 2 * me[1] + me[2])
                        for m in ALL_PEERS])])[0]
    return _sum_slots(got, vec[None], dev.reshape(1).astype(jnp.int32), F32, "reduce_small_sum")


SMALL = ("g_pre", "mu", "w0", "a0", "k_k", "k_a", "r_k", "ln_w", "ln_b", "b_f", "g_mem", "g_post")


def _pad_cols(a, n):
    return jnp.pad(a, ((0, 0),) * (a.ndim - 1) + ((0, n - a.shape[-1]),))


def kernel(x, mem, g_pre, w_in, mu_rwkv, w0, w_decay_up, a0, w_iclr_up, k_k, k_a, r_k, ln_x_w, ln_x_b, b_f, g_mem, w_mem_kv, w_out, g_post, loss_target, m_g_pre, m_w_in, m_mu_rwkv, m_w0, m_w_decay_up, m_a0, m_w_iclr_up, m_k_k, m_k_a, m_r_k, m_ln_x_w, m_ln_x_b, m_b_f, m_g_mem, m_w_mem_kv, m_w_out, m_g_post, v_g_pre, v_w_in, v_mu_rwkv, v_w0, v_w_decay_up, v_a0, v_w_iclr_up, v_k_k, v_k_a, v_r_k, v_ln_x_w, v_ln_x_b, v_b_f, v_g_mem, v_w_mem_kv, v_w_out, v_g_post):
    d = x.shape[-1]
    c = Cfg(d)
    ws = w_in.shape[-1]
    wpad = -(-ws // LANES) * LANES
    nh = c.h

    g_in, g_wd, g_wi = _all_gather([
        _pad_cols(w_in[0].astype(BF16), wpad), w_decay_up[0].astype(BF16), w_iclr_up[0].astype(BF16)])
    fl = c.ref_fl
    runs = [(0, fl, 0), (fl, fl + nh, c.o_fl), (fl + nh, c.in_width, fl)]
    pieces = []
    for lo, hi, _ in sorted(runs, key=lambda r: r[2]):
        for s in range(4):
            a, b = max(lo, s * ws), min(hi, (s + 1) * ws)
            if a < b:
                pieces.append(g_in[s, :, a - s * ws:b - s * ws])
    wp = jnp.concatenate(pieces + [jnp.zeros((d, LANES - nh), BF16)], axis=1)
    unshard = lambda g: g.transpose(1, 0, 2).reshape(g.shape[1], -1)
    weights = dict(wp=wp, wd=unshard(g_wd), wi=unshard(g_wi),
                   g_pre=g_pre, mu=mu_rwkv, w0=w0, a0=a0, k_k=k_k, k_a=k_a, r_k=r_k.reshape(1, -1),
                   ln_w=ln_x_w, ln_b=ln_x_b, b_f=_pad_cols(b_f, LANES), g_mem=g_mem, g_post=g_post)
    late_shards = [w_out[0].astype(BF16), w_mem_kv[0].astype(BF16)]

    def late_weights(first):
        g_out, g_mkv = _gather_finish(late_shards, first, "gather_pair_late")
        return dict(w_out=g_out.reshape(-1, d), w_mem_kv=g_mkv.reshape(d, -1))

    by_chip = lambda g: jnp.stack(jnp.split(g, 4, axis=1))
    pair_sums = {}

    def ride_rest(g):
        pair_sums["rest"] = _reduce_pair([g["w_out"].reshape(4, -1, d), g["w_mem_kv"].reshape(4, d // 4, -1),
                                          by_chip(g["wd"]), by_chip(g["wi"])], "rest")
        return _reduce_chips_groups(pair_sums["rest"])

    def ride_wp(dwp):
        shards = []
        for s in range(4):
            cols = []
            for lo, hi, at in runs:
                a, b = max(lo, s * ws), min(hi, (s + 1) * ws)
                if a < b:
                    cols.append(dwp[:, at + a - lo:at + b - lo])
            shards.append(jnp.concatenate(cols + [jnp.zeros((d, wpad - ws), BF16)], axis=1))
        pair_sums["wp"] = _reduce_pair([jnp.stack(shards)], "w_in")
        return _reduce_chips_groups(pair_sums["wp"])

    loss, grad_x, _, small, carried = _local_step(
        c, x[0], mem[0], loss_target[0], weights,
        riders={"in_proj": (_gather_groups(late_shards), late_weights), "d_w_in": ride_rest, "d_h": ride_wp})
    red = (_reduce_finish(carried["wp"], pair_sums["wp"], "w_in")
           + _reduce_finish(carried["rest"], pair_sums["rest"], "rest"))
    big_w = (w_in[0], w_out[0], w_mem_kv[0], w_decay_up[0], w_iclr_up[0])
    big_m = (m_w_in[0], m_w_out[0], m_w_mem_kv[0], m_w_decay_up[0], m_w_iclr_up[0])
    big_v = (v_w_in[0], v_w_out[0], v_w_mem_kv[0], v_w_decay_up[0], v_w_iclr_up[0])
    big_names = ("w_in", "w_out", "w_mem_kv", "w_decay_up", "w_iclr_up")
    upd = {n: _adamw(w_, g_, m_, v_, "adamw_" + n) for n, w_, g_, m_, v_ in zip(big_names, big_w, red, big_m, big_v)}

    small_w = dict(g_pre=g_pre, mu=mu_rwkv, w0=w0, a0=a0, k_k=k_k, k_a=k_a, r_k=r_k.reshape(1, -1), ln_w=ln_x_w,
                   ln_b=ln_x_b, b_f=b_f, g_mem=g_mem, g_post=g_post)
    small_m = dict(g_pre=m_g_pre, mu=m_mu_rwkv, w0=m_w0, a0=m_a0, k_k=m_k_k, k_a=m_k_a, r_k=m_r_k.reshape(1, -1),
                   ln_w=m_ln_x_w, ln_b=m_ln_x_b, b_f=m_b_f, g_mem=m_g_mem, g_post=m_g_post)
    small_v = dict(g_pre=v_g_pre, mu=v_mu_rwkv, w0=v_w0, a0=v_a0, k_k=v_k_k, k_a=v_k_a, r_k=v_r_k.reshape(1, -1),
                   ln_w=v_ln_x_w, ln_b=v_ln_x_b, b_f=v_b_f, g_mem=v_g_mem, g_post=v_g_post)
    widths = [-(-small_w[n].shape[1] // LANES) * LANES for n in SMALL]
    pack = lambda t: jnp.concatenate([_pad_cols(t[n], wd_) for n, wd_ in zip(SMALL, widths)]
                                     + [jnp.zeros((1, LANES), F32)], axis=1)
    g_packed = jnp.concatenate([_pad_cols(small[n], wd_) for n, wd_ in zip(SMALL, widths)]
                               + [_pad_cols(loss, LANES)], axis=1)
    g_sum = _all_reduce_small(g_packed)
    s_upd = _adamw(pack(small_w), g_sum, pack(small_m), pack(small_v), "adamw_small")
    offs = [sum(widths[:i]) for i in range(len(SMALL))]

    def take(kind, n):
        i = SMALL.index(n)
        piece = s_upd[kind][:, offs[i]:offs[i] + small_w[n].shape[1]]
        return piece.reshape(r_k.shape) if n == "r_k" else piece

    total_loss = g_sum[0, sum(widths)]
    order = ("g_pre", "w_in", "mu", "w0", "w_decay_up", "a0", "w_iclr_up", "k_k", "k_a", "r_k", "ln_w", "ln_b", "b_f",
             "g_mem", "w_mem_kv", "w_out", "g_post")
    outs = [total_loss, grad_x[None]]
    for kind in range(4):
        for n in order:
            outs.append(upd[n][kind][None] if n in upd else take(kind, n))
    return tuple(outs)
```

```python
import functools

import jax
import jax.numpy as jnp
from jax import lax
from jax.experimental import pallas as pl
from jax.experimental.pallas import tpu as pltpu

F32 = jnp.float32
BF16 = jnp.bfloat16
HI = lax.Precision.HIGHEST
MESH = pl.DeviceIdType.MESH

HEAD_DIM = 64
MEM_HEADS = 4
LORA = 128
CHUNK = 64
RMS_EPS = 1e-6
GN_EPS = 64e-5
LANES = 128
VMEM_LIMIT = 56 * 1024 * 1024

ADAM_LR, ADAM_B1, ADAM_B2, ADAM_EPS, ADAM_WD, ADAM_STEP = 0.001, 0.9, 0.999, 1e-08, 0.01, 10


class Cfg:
    def __init__(self, d):
        self.d = d
        self.rw = 3 * d // 8
        self.mw = d // 4
        self.h = self.rw // HEAD_DIM
        self.mhd = self.mw // MEM_HEADS
        self.shift = 3 * self.rw + 2 * LORA
        self.in_width = self.shift + 5 * self.rw + self.h + 2 * self.mw
        o = self.shift
        self.o_grw = o; o += self.rw
        self.o_fq = o; o += self.rw
        self.o_fk = o; o += self.rw
        self.o_fv = o; o += self.rw
        self.o_gfox = o; o += self.rw
        self.o_mq = o; o += self.mw
        self.o_gmq = o; o += self.mw
        self.o_fl = o; o += LANES
        self.wp = o
        self.ref_fl = self.shift + 4 * self.rw


def _tile(n, pref, align=LANES):
    if n <= pref:
        return n
    t = (pref // align) * align
    while t >= align:
        if n % t == 0:
            return t
        t -= align
    return n


def _params(*sem):
    return pltpu.CompilerParams(dimension_semantics=sem, vmem_limit_bytes=VMEM_LIMIT)


def _pallas_into(body, into, n_in, out_index, in_specs, **kw):
    if into is None:
        return pl.pallas_call(body, in_specs=in_specs, **kw)

    def body_with_alias(*refs):
        return body(*refs[:n_in], *refs[n_in + 1:])

    call = pl.pallas_call(body_with_alias, in_specs=list(in_specs) + [pl.BlockSpec(memory_space=pl.ANY)],
                          input_output_aliases={n_in: out_index}, **kw)
    return lambda *args: call(*args, into)


def _sig(x):
    return 1.0 / (1.0 + jnp.exp(-x))


def _softplus(x):
    return jnp.maximum(x, 0.0) + jnp.log(1.0 + jnp.exp(-jnp.abs(x)))


def _dot(a, b, dims, prec=None):
    return lax.dot_general(a, b, (dims, ((), ())), precision=prec, preferred_element_type=F32)


def _mm(a, b, prec=None):
    return _dot(a, b, ((1,), (0,)), prec)


def _mm_nt(a, b, prec=None):
    return _dot(a, b, ((1,), (1,)), prec)


def _mm_tn(a, b, prec=None):
    return _dot(a, b, ((0,), (0,)), prec)


def _split(a):
    hi = a.astype(BF16)
    return hi, (a - hi.astype(F32)).astype(BF16)


def _dot3(a, b, dims, passes=3):
    d = lambda x, y: lax.dot_general(x, y, dims, preferred_element_type=F32)
    if passes == 1:
        return d(a.astype(BF16), b.astype(BF16))
    (ah, al), (bh, bl) = _split(a), _split(b)
    return d(ah, bh) + (d(ah, bl) + d(al, bh))


def _bmm(a, b, passes=3):
    return _dot3(a, b, (((2,), (1,)), ((0,), (0,))), passes)


def _bmm_nt(a, b, passes=3):
    return _dot3(a, b, (((2,), (2,)), ((0,), (0,))), passes)


def _bmm_tn(a, b, passes=3):
    return _dot3(a, b, (((1,), (1,)), ((0,), (0,))), passes)


def _bmm_01(m01, x):
    x1 = x.astype(BF16)
    r1 = x - x1.astype(F32)
    x2 = r1.astype(BF16)
    x3 = (r1 - x2.astype(F32)).astype(BF16)
    d = lambda y: lax.dot_general(m01, y, (((2,), (1,)), ((0,), (0,))), preferred_element_type=F32)
    return d(x1) + (d(x2) + d(x3))


def _matmul(a, b, *, ta=False, tb=False, out_dtype=F32, name, tm=1024, tn=1024, tk=1024, attach=None):
    m, k = (a.shape[1], a.shape[0]) if ta else a.shape
    n = b.shape[0] if tb else b.shape[1]
    tm, tn, tk = _tile(m, tm), _tile(n, tn), _tile(k, tk)
    nk = k // tk
    grid = (m // tm, n // tn, nk)
    dims = ((0 if ta else 1,), (1 if tb else 0,))
    groups = attach or []
    ng = len(groups)
    plan = _plan(groups)

    def body(a_ref, b_ref, *rest):
        srcs, o_ref, outs, scratch = rest[:ng], rest[ng], rest[ng + 1:2 * ng + 1], rest[2 * ng + 1:]
        acc = scratch[0] if nk > 1 else None
        if ng:
            copies = _copies(groups, plan, srcs, outs, scratch[-2], scratch[-1])
            ids = [pl.program_id(ax) for ax in range(3)]

            @pl.when((ids[0] == 0) & (ids[1] == 0) & (ids[2] == 0))
            def _():
                for cp in copies:
                    cp.start()

        part = _dot(a_ref[...].astype(BF16), b_ref[...].astype(BF16), dims)
        if nk == 1:
            o_ref[...] = part.astype(o_ref.dtype)
        else:
            kk = pl.program_id(2)

            @pl.when(kk == 0)
            def _():
                acc[...] = part

            @pl.when(kk > 0)
            def _():
                acc[...] += part

            @pl.when(kk == nk - 1)
            def _():
                o_ref[...] = acc[...].astype(o_ref.dtype)

        if ng:
            @pl.when((ids[0] == grid[0] - 1) & (ids[1] == grid[1] - 1) & (ids[2] == grid[2] - 1))
            def _():
                for cp in copies:
                    cp.wait()

    a_spec = pl.BlockSpec((tk, tm), lambda i, j, kk: (kk, i)) if ta else pl.BlockSpec((tm, tk), lambda i, j, kk: (i, kk))
    b_spec = pl.BlockSpec((tn, tk), lambda i, j, kk: (j, kk)) if tb else pl.BlockSpec((tk, tn), lambda i, j, kk: (kk, j))
    any_spec = pl.BlockSpec(memory_space=pl.ANY)
    sems = [pltpu.SemaphoreType.DMA((len(plan),)), pltpu.SemaphoreType.DMA((len(plan),))] if ng else []
    res = pl.pallas_call(
        body, name=name, grid=grid,
        in_specs=[a_spec, b_spec] + [any_spec] * ng,
        out_specs=[pl.BlockSpec((tm, tn), lambda i, j, kk: (i, j))] + [any_spec] * ng,
        out_shape=[jax.ShapeDtypeStruct((m, n), out_dtype)] + _exchange_shapes(groups),
        scratch_shapes=([pltpu.VMEM((tm, tn), F32)] if nk > 1 else []) + sems,
        compiler_params=_params(*(("arbitrary",) * 3 if ng else ("parallel", "parallel", "arbitrary"))),
    )(a, b, *[g["src"] for g in groups])
    return (res[0], list(res[1:])) if ng else res[0]


def _rms_fwd(x, g, name):
    t, d = x.shape
    tm = _tile(t, 256, 8)

    def body(x_ref, g_ref, h_ref, r_ref):
        xv = x_ref[...]
        r = lax.rsqrt(jnp.mean(xv * xv, axis=-1, keepdims=True) + RMS_EPS)
        h_ref[...] = (xv * r * g_ref[...]).astype(BF16)
        r_ref[...] = r

    return pl.pallas_call(
        body, name=name, grid=(t // tm,),
        in_specs=[pl.BlockSpec((tm, d), lambda i: (i, 0)), pl.BlockSpec((1, d), lambda i: (0, 0))],
        out_specs=[pl.BlockSpec((tm, d), lambda i: (i, 0)), pl.BlockSpec((tm, 1), lambda i: (i, 0))],
        out_shape=[jax.ShapeDtypeStruct((t, d), BF16), jax.ShapeDtypeStruct((t, 1), F32)],
        compiler_params=_params("parallel"),
    )(x, g)


def _rms_bwd(dh, x, rinv, g, add, name):
    t, d = x.shape
    tm = _tile(t, 256, 8)

    def body(dh_ref, x_ref, r_ref, g_ref, add_ref, dx_ref, dg_ref):
        @pl.when(pl.program_id(0) == 0)
        def _():
            dg_ref[...] = jnp.zeros_like(dg_ref)

        r = r_ref[...]
        xn = x_ref[...] * r
        dhv = dh_ref[...]
        dg_ref[...] += jnp.sum(dhv * xn, axis=0, keepdims=True)
        dxn = dhv * g_ref[...]
        dx_ref[...] = add_ref[...] + r * (dxn - xn * jnp.mean(dxn * xn, axis=-1, keepdims=True))

    row = pl.BlockSpec((tm, d), lambda i: (i, 0))
    vec = pl.BlockSpec((1, d), lambda i: (0, 0))
    return pl.pallas_call(
        body, name=name, grid=(t // tm,),
        in_specs=[row, row, pl.BlockSpec((tm, 1), lambda i: (i, 0)), vec, row],
        out_specs=[row, vec],
        out_shape=[jax.ShapeDtypeStruct((t, d), F32), jax.ShapeDtypeStruct((1, d), F32)],
        compiler_params=_params("arbitrary"),
    )(dh, x, rinv, g, add)


def _post_loss(yo, x, tgt, g, name):
    t, d = x.shape
    tm = _tile(t, 256, 8)

    def body(yo_ref, x_ref, t_ref, g_ref, loss_ref, dout_ref, dyo_ref, dg_ref):
        @pl.when(pl.program_id(0) == 0)
        def _():
            dg_ref[...] = jnp.zeros_like(dg_ref)
            loss_ref[...] = jnp.zeros_like(loss_ref)

        yv = yo_ref[...]
        r = lax.rsqrt(jnp.mean(yv * yv, axis=-1, keepdims=True) + RMS_EPS)
        n = yv * r
        err = x_ref[...] + n * g_ref[...] - t_ref[...]
        loss_ref[...] += 0.5 * jnp.sum(jnp.mean(err * err, axis=-1, keepdims=True), axis=0, keepdims=True)
        dout = err * (1.0 / d)
        dout_ref[...] = dout
        dg_ref[...] += jnp.sum(dout * n, axis=0, keepdims=True)
        dn = dout * g_ref[...]
        dyo_ref[...] = (r * (dn - n * jnp.mean(dn * n, axis=-1, keepdims=True))).astype(BF16)

    row = pl.BlockSpec((tm, d), lambda i: (i, 0))
    vec = pl.BlockSpec((1, d), lambda i: (0, 0))
    return pl.pallas_call(
        body, name=name, grid=(t // tm,),
        in_specs=[row, row, row, vec],
        out_specs=[pl.BlockSpec((1, 1), lambda i: (0, 0)), row, row, vec],
        out_shape=[jax.ShapeDtypeStruct((1, 1), F32), jax.ShapeDtypeStruct((t, d), F32),
                   jax.ShapeDtypeStruct((t, d), BF16), jax.ShapeDtypeStruct((1, d), F32)],
        compiler_params=_params("arbitrary"),
    )(yo, x, tgt, g)


def _head_sum(x):
    ri = lax.broadcasted_iota(jnp.int32, (LANES, LANES), 0) // HEAD_DIM
    ci = lax.broadcasted_iota(jnp.int32, (LANES, LANES), 1) // HEAD_DIM
    e = (ri == ci).astype(BF16)
    x1 = x.astype(BF16)
    r1 = x - x1.astype(F32)
    x2 = r1.astype(BF16)
    x3 = (r1 - x2.astype(F32)).astype(BF16)
    parts = []
    for i in range(x.shape[1] // LANES):
        sl = slice(i * LANES, (i + 1) * LANES)
        parts.append(_mm(x1[:, sl], e) + (_mm(x2[:, sl], e) + _mm(x3[:, sl], e)))
    return parts[0] if len(parts) == 1 else jnp.concatenate(parts, axis=1)


def _shifted(p_cur, before, first, mu):
    rolled = pltpu.roll(p_cur, 1, 0)
    prev_row = jnp.where(first, 0.0, before)
    row0 = lax.broadcasted_iota(jnp.int32, p_cur.shape, 0) == 0
    prev = jnp.where(row0, prev_row, rolled)
    return p_cur + (prev - p_cur) * mu, prev


def _rwkv_features(ps, rw, w0, a0, k_k, k_a, wd, wi):
    r, k, v = ps[:, 0:rw], ps[:, rw:2 * rw], ps[:, 2 * rw:3 * rw]
    wl, al = ps[:, 3 * rw:3 * rw + LORA], ps[:, 3 * rw + LORA:3 * rw + 2 * LORA]
    tw = jnp.tanh(wl)
    zw = w0 + _mm(tw.astype(BF16), wd)
    logw = -jnp.exp(-_softplus(-zw) - 0.5)
    alpha = _sig(a0 + _mm(al.astype(BF16), wi))
    kkr = k * k_k
    n2 = _head_sum(kkr * kkr)
    rn = lax.rsqrt(jnp.maximum(n2, 1e-24))
    kk = kkr * rn
    kmod = k * (1.0 + (alpha - 1.0) * k_a)
    return dict(r=r, k=k, v=v, tw=tw, al=al, zw=zw, logw=logw, alpha=alpha, kk=kk, rn=rn, n2=n2, kmod=kmod)


def _rwkv_pre_fwd(p, c, mu, w0, a0, k_k, k_a, wd, wi):
    t = p.shape[0]
    tm = _tile(t, 128, 8)
    rw, sh = c.rw, c.shift

    def body(p_ref, pp_ref, mu_ref, w0_ref, a0_ref, kk_ref, ka_ref, wd_ref, wi_ref,
             r_ref, lw_ref, km_ref, v_ref, a_ref, b_ref):
        ps, _ = _shifted(p_ref[...], pp_ref[7:8, :], pl.program_id(0) == 0, mu_ref[...])
        f = _rwkv_features(ps, rw, w0_ref[...], a0_ref[...], kk_ref[...], ka_ref[...], wd_ref[...], wi_ref[...])
        r_ref[...] = f["r"]
        lw_ref[...] = f["logw"]
        km_ref[...] = f["kmod"]
        v_ref[...] = f["v"]
        a_ref[...] = -f["kk"]
        b_ref[...] = f["kk"] * f["alpha"]

    vec = lambda n: pl.BlockSpec((1, n), lambda i: (0, 0))
    out = pl.BlockSpec((tm, rw), lambda i: (i, 0))
    return pl.pallas_call(
        body, name="rwkv_pre_fwd", grid=(t // tm,),
        in_specs=[pl.BlockSpec((tm, sh), lambda i: (i, 0)),
                  pl.BlockSpec((8, sh), lambda i: (jnp.maximum(i * (tm // 8) - 1, 0), 0)),
                  vec(sh), vec(rw), vec(rw), vec(rw), vec(rw),
                  pl.BlockSpec((LORA, rw), lambda i: (0, 0)), pl.BlockSpec((LORA, rw), lambda i: (0, 0))],
        out_specs=[out] * 6,
        out_shape=[jax.ShapeDtypeStruct((t, rw), F32)] * 6,
        compiler_params=_params("parallel"),
    )(p, p, mu, w0, a0, k_k, k_a, wd, wi)


def _rwkv_pre_bwd(p, c, mu, w0, a0, k_k, k_a, wd, wi, dr, dlw, dkm, dv, da, db, dr2, dkm2, dv2):
    t = p.shape[0]
    tm = _tile(t, 128, 8)
    rw, sh = c.rw, c.shift

    def body(p_ref, pp_ref, mu_ref, w0_ref, a0_ref, kk_ref, ka_ref, wd_ref, wi_ref,
             dr_ref, dlw_ref, dkm_ref, dv_ref, da_ref, db_ref, dr2_ref, dkm2_ref, dv2_ref,
             dps_ref, dzw_ref, dza_ref, tw_ref, al_ref, dw0_ref, da0_ref, dkk_ref, dka_ref):
        @pl.when(pl.program_id(0) == 0)
        def _():
            for ref in (dw0_ref, da0_ref, dkk_ref, dka_ref):
                ref[...] = jnp.zeros_like(ref)

        ps, _ = _shifted(p_ref[...], pp_ref[7:8, :], pl.program_id(0) == 0, mu_ref[...])
        k_k, k_a = kk_ref[...], ka_ref[...]
        f = _rwkv_features(ps, rw, w0_ref[...], a0_ref[...], k_k, k_a, wd_ref[...], wi_ref[...])
        alpha, kk, k = f["alpha"], f["kk"], f["k"]
        dkm = dkm_ref[...] + dkm2_ref[...]
        db = db_ref[...]
        dkk = db * alpha - da_ref[...]
        dalpha = db * kk + dkm * k * k_a
        dk = dkm * (1.0 + (alpha - 1.0) * k_a)
        dka_ref[...] += jnp.sum(dkm * k * (alpha - 1.0), axis=0, keepdims=True)
        dkkr = f["rn"] * jnp.where(f["n2"] > 1e-24, dkk - kk * _head_sum(dkk * kk), dkk)
        dk = dk + dkkr * k_k
        dkk_ref[...] += jnp.sum(dkkr * k, axis=0, keepdims=True)
        dza = dalpha * alpha * (1.0 - alpha)
        da0_ref[...] += jnp.sum(dza, axis=0, keepdims=True)
        dzw = dlw_ref[...] * f["logw"] * _sig(-f["zw"])
        dw0_ref[...] += jnp.sum(dzw, axis=0, keepdims=True)
        dza_b, dzw_b = dza.astype(BF16), dzw.astype(BF16)
        dal = _mm_nt(dza_b, wi_ref[...])
        dwl = _mm_nt(dzw_b, wd_ref[...]) * (1.0 - f["tw"] * f["tw"])
        dps_ref[:, 0:rw] = dr_ref[...] + dr2_ref[...]
        dps_ref[:, rw:2 * rw] = dk
        dps_ref[:, 2 * rw:3 * rw] = dv_ref[...] + dv2_ref[...]
        dps_ref[:, 3 * rw:3 * rw + LORA] = dwl
        dps_ref[:, 3 * rw + LORA:sh] = dal
        dzw_ref[...] = dzw_b
        dza_ref[...] = dza_b
        tw_ref[...] = f["tw"].astype(BF16)
        al_ref[...] = f["al"].astype(BF16)

    vec = lambda n: pl.BlockSpec((1, n), lambda i: (0, 0))
    blk = lambda n: pl.BlockSpec((tm, n), lambda i: (i, 0))
    return pl.pallas_call(
        body, name="rwkv_pre_bwd", grid=(t // tm,),
        in_specs=[blk(sh), pl.BlockSpec((8, sh), lambda i: (jnp.maximum(i * (tm // 8) - 1, 0), 0)),
                  vec(sh), vec(rw), vec(rw), vec(rw), vec(rw),
                  pl.BlockSpec((LORA, rw), lambda i: (0, 0)), pl.BlockSpec((LORA, rw), lambda i: (0, 0))]
                 + [blk(rw)] * 9,
        out_specs=[blk(sh), blk(rw), blk(rw), blk(LORA), blk(LORA), vec(rw), vec(rw), vec(rw), vec(rw)],
        out_shape=[jax.ShapeDtypeStruct((t, sh), F32), jax.ShapeDtypeStruct((t, rw), BF16),
                   jax.ShapeDtypeStruct((t, rw), BF16), jax.ShapeDtypeStruct((t, LORA), BF16),
                   jax.ShapeDtypeStruct((t, LORA), BF16)] + [jax.ShapeDtypeStruct((1, rw), F32)] * 4,
        compiler_params=_params("arbitrary"),
    )(p, p, mu, w0, a0, k_k, k_a, wd, wi, dr, dlw, dkm, dv, da, db, dr2, dkm2, dv2)


def _shift_bwd(dps, p, c, mu, dp):
    t = p.shape[0]
    tm = _tile(t, 256, 8)
    sh = c.shift
    nt = t // tm

    def body(d_ref, dn_ref, p_ref, pp_ref, mu_ref, dp_ref, dmu_ref):
        i = pl.program_id(0)

        @pl.when(i == 0)
        def _():
            dmu_ref[...] = jnp.zeros_like(dmu_ref)

        mu = mu_ref[...]
        d = d_ref[...]
        pc = p_ref[...]
        _, prev = _shifted(pc, pp_ref[7:8, :], i == 0, mu)
        dmu_ref[...] += jnp.sum(d * (prev - pc), axis=0, keepdims=True)
        nxt_row = jnp.where(i == nt - 1, 0.0, dn_ref[0:1, :])
        last = lax.broadcasted_iota(jnp.int32, d.shape, 0) == tm - 1
        nxt = jnp.where(last, nxt_row, pltpu.roll(d, tm - 1, 0))
        dp_ref[...] = (d * (1.0 - mu) + nxt * mu).astype(BF16)

    blk = pl.BlockSpec((tm, sh), lambda i: (i, 0))
    return _pallas_into(
        body, dp, 5, 0, name="shift_bwd", grid=(nt,),
        in_specs=[blk, pl.BlockSpec((8, sh), lambda i: (jnp.minimum((i + 1) * (tm // 8), t // 8 - 1), 0)),
                  blk, pl.BlockSpec((8, sh), lambda i: (jnp.maximum(i * (tm // 8) - 1, 0), 0)),
                  pl.BlockSpec((1, sh), lambda i: (0, 0))],
        out_specs=[blk, pl.BlockSpec((1, sh), lambda i: (0, 0))],
        out_shape=[jax.ShapeDtypeStruct((t, c.wp), BF16), jax.ShapeDtypeStruct((1, sh), F32)],
        compiler_params=_params("arbitrary"),
    )(dps, dps, p, p, mu)


def _tri(n, strict):
    ri = lax.broadcasted_iota(jnp.int32, (n, n), 0)
    ci = lax.broadcasted_iota(jnp.int32, (n, n), 1)
    return (ri > ci) if strict else (ri >= ci)


def _unit_lower_inverse(a):
    n = a.shape[-1]
    ri = lax.broadcasted_iota(jnp.int32, (n, n), 0)
    ci = lax.broadcasted_iota(jnp.int32, (n, n), 1)
    eye = (ri == ci).astype(F32)
    blk = lambda s: (ri // s) == (ci // s)
    ad = jnp.where(blk(16), a, 0.0)
    p = eye + ad
    for _ in range(3):
        ad = _bmm(ad, ad, P_SOLVE)
        p = p + _bmm(p, ad, P_SOLVE)
    s = 16
    while s < n:
        off = jnp.where(blk(2 * s) & ~blk(s), a, 0.0)
        p = p + _bmm(_bmm(p, off, P_SOLVE), p, P_SOLVE)
        s *= 2
    return p


P_SOLVE, P_STATE, P_OUT, P_GRAD, P_DECAY = 1, 3, 1, 1, 3


def _chunk_common(r, lw, k, a, b):
    n = r.shape[1]
    tri_incl = jnp.broadcast_to(_tri(n, False).astype(BF16), (r.shape[0], n, n))
    cum = _bmm_01(tri_incl, lw)
    e_pos, e_neg, e_exc = jnp.exp(cum), jnp.exp(-cum), jnp.exp(cum - lw)
    last = lax.broadcasted_iota(jnp.int32, (n, r.shape[2]), 0) == n - 1
    g_last = jnp.exp(jnp.sum(jnp.where(last, cum, 0.0), axis=1, keepdims=True))
    return g_last, r * e_pos, a * e_exc, b * e_neg, k * e_neg, e_pos, e_neg, e_exc


def _chunk_solve(rt, at, bt, kt, v, g0):
    strict, incl = _tri(rt.shape[1], True), _tri(rt.shape[1], False)
    a_ab = jnp.where(strict, _bmm_nt(at, bt, P_SOLVE), 0.0)
    a_ak = jnp.where(strict, _bmm_nt(at, kt, P_SOLVE), 0.0)
    a_rb = jnp.where(incl, _bmm_nt(rt, bt, P_OUT), 0.0)
    a_rk = jnp.where(incl, _bmm_nt(rt, kt, P_OUT), 0.0)
    tinv = _unit_lower_inverse(a_ab)
    u = _bmm(tinv, _bmm(at, g0, P_SOLVE) + _bmm(a_ak, v, P_SOLVE), P_SOLVE)
    return a_ab, a_ak, a_rb, a_rk, tinv, u


def _diag_col(row, n):
    ri = lax.broadcasted_iota(jnp.int32, (n, n), 0)
    ci = lax.broadcasted_iota(jnp.int32, (n, n), 1)
    return jnp.sum(jnp.where(ri == ci, row, 0.0), axis=2, keepdims=True)


def _diag_row(col, n):
    ri = lax.broadcasted_iota(jnp.int32, (n, n), 0)
    ci = lax.broadcasted_iota(jnp.int32, (n, n), 1)
    return jnp.sum(jnp.where(ri == ci, col, 0.0), axis=1, keepdims=True)


def _rwkv_scan_fwd(r, lw, k, v, a, b, hb):
    h, t, n = r.shape
    nc = t // CHUNK

    def body(r_ref, lw_ref, k_ref, v_ref, a_ref, b_ref, y_ref, st_ref, g_sc):
        @pl.when(pl.program_id(1) == 0)
        def _():
            g_sc[...] = jnp.zeros_like(g_sc)

        g0 = g_sc[...]
        st_ref[0] = g0
        vv = v_ref[...]
        g_last, rt, at, bt, kt, _, _, _ = _chunk_common(r_ref[...], lw_ref[...], k_ref[...], a_ref[...], b_ref[...])
        _, _, a_rb, a_rk, _, u = _chunk_solve(rt, at, bt, kt, vv, g0)
        y_ref[...] = _bmm(rt, g0, P_OUT) + _bmm(a_rb, u, P_OUT) + _bmm(a_rk, vv, P_OUT)
        z = g0 + _bmm_tn(bt, u, P_STATE) + _bmm_tn(kt, vv, P_STATE)
        g_sc[...] = _diag_col(g_last, n) * z

    blk = pl.BlockSpec((hb, CHUNK, n), lambda i, j: (i, j, 0))
    return pl.pallas_call(
        body, name="rwkv_scan_fwd", grid=(h // hb, nc),
        in_specs=[blk] * 6,
        out_specs=[blk, pl.BlockSpec((1, hb, n, n), lambda i, j: (j, i, 0, 0))],
        out_shape=[jax.ShapeDtypeStruct((h, t, n), F32), jax.ShapeDtypeStruct((nc, h, n, n), F32)],
        scratch_shapes=[pltpu.VMEM((hb, n, n), F32)],
        compiler_params=_params("parallel", "arbitrary"),
    )(r, lw, k, v, a, b)


def _rwkv_scan_bwd(r, lw, k, v, a, b, states, dy, hb):
    h, t, n = r.shape
    nc = t // CHUNK

    def body(r_ref, lw_ref, k_ref, v_ref, a_ref, b_ref, st_ref, dy_ref,
             dr_ref, dlw_ref, dk_ref, dv_ref, da_ref, db_ref, dg_sc):
        @pl.when(pl.program_id(1) == 0)
        def _():
            dg_sc[...] = jnp.zeros_like(dg_sc)

        g0 = st_ref[0]
        vv, dyv, dh = v_ref[...], dy_ref[...], dg_sc[...]
        lwv = lw_ref[...]
        g_last, rt, at, bt, kt, e_pos, e_neg, e_exc = _chunk_common(r_ref[...], lwv, k_ref[...], a_ref[...], b_ref[...])
        a_ab, a_ak, a_rb, a_rk, tinv, u = _chunk_solve(rt, at, bt, kt, vv, g0)
        strict, incl = _tri(CHUNK, True), _tri(CHUNK, False)
        gcol = _diag_col(g_last, n)
        z = g0 + _bmm_tn(bt, u, P_STATE) + _bmm_tn(kt, vv, P_STATE)
        dz = gcol * dh
        dc_last = _diag_row(jnp.sum(dh * gcol * z, axis=2, keepdims=True), n)
        g = P_GRAD
        du = _bmm_tn(a_rb, dyv, g) + _bmm(bt, dz, g)
        dx = _bmm_tn(tinv, du, P_SOLVE)
        dv_ref[...] = _bmm_tn(a_rk, dyv, g) + _bmm(kt, dz, g) + _bmm_tn(a_ak, dx, g)
        da_ab = jnp.where(strict, _bmm_nt(dx, u, g), 0.0)
        da_ak = jnp.where(strict, _bmm_nt(dx, vv, g), 0.0)
        da_rb = jnp.where(incl, _bmm_nt(dyv, u, g), 0.0)
        da_rk = jnp.where(incl, _bmm_nt(dyv, vv, g), 0.0)
        g = P_DECAY
        d_at = _bmm(da_ab, bt, g) + _bmm(da_ak, kt, g) + _bmm_nt(dx, g0, g)
        d_rt = _bmm(da_rb, bt, g) + _bmm(da_rk, kt, g) + _bmm_nt(dyv, g0, g)
        d_bt = _bmm_tn(da_ab, at, g) + _bmm_tn(da_rb, rt, g) + _bmm_nt(u, dz, g)
        d_kt = _bmm_tn(da_ak, at, g) + _bmm_tn(da_rk, rt, g) + _bmm_nt(vv, dz, g)
        dg_sc[...] = dz + _bmm_tn(rt, dyv, P_STATE) + _bmm_tn(at, dx, P_STATE)
        dr_ref[...] = d_rt * e_pos
        da_ref[...] = d_at * e_exc
        db_ref[...] = d_bt * e_neg
        dk_ref[...] = d_kt * e_neg
        last = lax.broadcasted_iota(jnp.int32, (CHUNK, n), 0) == CHUNK - 1
        dc = d_rt * rt - d_bt * bt - d_kt * kt + jnp.where(last, dc_last, 0.0)
        dce = d_at * at
        ri = lax.broadcasted_iota(jnp.int32, (CHUNK, CHUNK), 0)
        ci = lax.broadcasted_iota(jnp.int32, (CHUNK, CHUNK), 1)
        up_incl = jnp.broadcast_to((ri <= ci).astype(BF16), (hb, CHUNK, CHUNK))
        dlw_ref[...] = _bmm_01(up_incl, dc + dce) - dce

    rev = lambda i, j: (i, nc - 1 - j, 0)
    blk = pl.BlockSpec((hb, CHUNK, n), rev)
    return pl.pallas_call(
        body, name="rwkv_scan_bwd", grid=(h // hb, nc),
        in_specs=[blk] * 6 + [pl.BlockSpec((1, hb, n, n), lambda i, j: (nc - 1 - j, i, 0, 0)), blk],
        out_specs=[blk] * 6,
        out_shape=[jax.ShapeDtypeStruct((h, t, n), F32)] * 6,
        scratch_shapes=[pltpu.VMEM((hb, n, n), F32)],
        compiler_params=_params("parallel", "arbitrary"),
    )(r, lw, k, v, a, b, states, dy)


def _silu_grad(g):
    s = _sig(g)
    return s * (1.0 + g * (1.0 - s))


def _group_norm(ys):
    yc = ys - _head_sum(ys) * (1.0 / HEAD_DIM)
    rstd = lax.rsqrt(_head_sum(yc * yc) * (1.0 / HEAD_DIM) + GN_EPS)
    return yc * rstd, rstd


def _rwkv_post_fwd(ys, r, km, v, p, c, ln_w, ln_b, r_k):
    t = ys.shape[0]
    tm = _tile(t, 512, 8)
    goff = c.o_grw // LANES

    def body(ys_ref, r_ref, km_ref, v_ref, g_ref, lw_ref, lb_ref, rk_ref, o_ref):
        yn, _ = _group_norm(ys_ref[...])
        s = _head_sum(r_ref[...] * km_ref[...] * rk_ref[...])
        g = g_ref[...]
        o_ref[...] = ((yn * lw_ref[...] + lb_ref[...] + s * v_ref[...]) * g * _sig(g)).astype(BF16)

    blk = pl.BlockSpec((tm, LANES), lambda i, j: (i, j))
    vec = pl.BlockSpec((1, LANES), lambda i, j: (0, j))
    return pl.pallas_call(
        body, name="rwkv_post_fwd", grid=(t // tm, c.rw // LANES),
        in_specs=[blk] * 4 + [pl.BlockSpec((tm, LANES), lambda i, j: (i, goff + j)), vec, vec, vec],
        out_specs=blk, out_shape=jax.ShapeDtypeStruct((t, c.d), BF16),
        compiler_params=_params("parallel", "parallel"),
    )(ys, r, km, v, p, ln_w, ln_b, r_k)


def _rwkv_post_bwd(dyc, ys, r, km, v, p, c, ln_w, ln_b, r_k):
    t = ys.shape[0]
    tm = _tile(t, 512, 8)
    goff = c.o_grw // LANES

    def body(dy_ref, ys_ref, r_ref, km_ref, v_ref, g_ref, lw_ref, lb_ref, rk_ref,
             dys_ref, dr_ref, dkm_ref, dv_ref, dg_ref, dlw_ref, dlb_ref, drk_ref):
        @pl.when(pl.program_id(1) == 0)
        def _():
            for ref in (dlw_ref, dlb_ref, drk_ref):
                ref[...] = jnp.zeros_like(ref)

        yn, rstd = _group_norm(ys_ref[...])
        rv, kmv, vv, rk, g = r_ref[...], km_ref[...], v_ref[...], rk_ref[...], g_ref[...]
        s = _head_sum(rv * kmv * rk)
        y = yn * lw_ref[...] + lb_ref[...] + s * vv
        dyc = dy_ref[...]
        dg_ref[...] = (dyc * y * _silu_grad(g)).astype(BF16)
        dy = dyc * g * _sig(g)
        dlb_ref[...] += jnp.sum(dy, axis=0, keepdims=True)
        dlw_ref[...] += jnp.sum(dy * yn, axis=0, keepdims=True)
        dyn = dy * lw_ref[...]
        inv = 1.0 / HEAD_DIM
        dys_ref[...] = rstd * (dyn - _head_sum(dyn) * inv - yn * _head_sum(dyn * yn) * inv)
        ds = _head_sum(dy * vv)
        dv_ref[...] = dy * s
        dr_ref[...] = ds * kmv * rk
        dkm_ref[...] = ds * rv * rk
        drk_ref[...] += jnp.sum(ds * rv * kmv, axis=0, keepdims=True)

    blk = pl.BlockSpec((tm, LANES), lambda j, i: (i, j))
    vec = pl.BlockSpec((1, LANES), lambda j, i: (0, j))
    f = jax.ShapeDtypeStruct((t, c.rw), F32)
    s1 = jax.ShapeDtypeStruct((1, c.rw), F32)
    gate = pl.BlockSpec((tm, LANES), lambda j, i: (i, goff + j))
    return pl.pallas_call(
        body, name="rwkv_post_bwd", grid=(c.rw // LANES, t // tm),
        in_specs=[blk] * 5 + [gate, vec, vec, vec],
        out_specs=[blk] * 4 + [gate] + [vec] * 3,
        out_shape=[f, f, f, f, jax.ShapeDtypeStruct((t, c.wp), BF16), s1, s1, s1],
        compiler_params=_params("parallel", "arbitrary"),
    )(dyc, ys, r, km, v, p, ln_w, ln_b, r_k)


def _gate_fwd(y, p, goff, name, ycat, yoff):
    t, w = y.shape
    tm = _tile(t, 512, 8)
    gb, ob = goff // LANES, yoff // LANES

    def body(y_ref, g_ref, o_ref):
        g = g_ref[...]
        o_ref[...] = (y_ref[...] * g * _sig(g)).astype(BF16)

    blk = pl.BlockSpec((tm, LANES), lambda i, j: (i, j))
    return _pallas_into(
        body, ycat, 2, 0, name=name, grid=(t // tm, w // LANES),
        in_specs=[blk, pl.BlockSpec((tm, LANES), lambda i, j: (i, gb + j))],
        out_specs=pl.BlockSpec((tm, LANES), lambda i, j: (i, ob + j)),
        out_shape=jax.ShapeDtypeStruct(ycat.shape, BF16),
        compiler_params=_params("parallel", "parallel"),
    )(y, p)


def _gate_bwd(dyc, yoff, y, p, goff, name, dp):
    t, w = y.shape
    tm = _tile(t, 512, 8)
    gb, yb = goff // LANES, yoff // LANES

    def body(d_ref, y_ref, g_ref, dy_ref, dg_ref):
        g, d = g_ref[...], d_ref[...]
        dy_ref[...] = d * g * _sig(g)
        dg_ref[...] = (d * y_ref[...] * _silu_grad(g)).astype(BF16)

    blk = pl.BlockSpec((tm, LANES), lambda i, j: (i, j))
    gate = pl.BlockSpec((tm, LANES), lambda i, j: (i, gb + j))
    return _pallas_into(
        body, dp, 3, 1, name=name, grid=(t // tm, w // LANES),
        in_specs=[pl.BlockSpec((tm, LANES), lambda i, j: (i, yb + j)), blk, gate],
        out_specs=[blk, gate],
        out_shape=[jax.ShapeDtypeStruct((t, w), F32), jax.ShapeDtypeStruct(dp.shape, BF16)],
        compiler_params=_params("parallel", "parallel"),
    )(dyc, y, p)


NEG = -1e30


def _fox_prep(p, c, b_f):
    t = p.shape[0]
    tm = _tile(t, 512, 8)
    fb = c.o_fl // LANES

    def body(f_ref, b_ref, o_ref, carry):
        @pl.when(pl.program_id(0) == 0)
        def _():
            carry[...] = jnp.zeros_like(carry)

        logf = -_softplus(-(f_ref[...] + b_ref[...]))
        cum = _mm(_tri(tm, False).astype(F32), logf, HI) + carry[...]
        o_ref[...] = cum
        carry[...] += jnp.sum(logf, axis=0, keepdims=True)

    return pl.pallas_call(
        body, name="fox_prep", grid=(t // tm,),
        in_specs=[pl.BlockSpec((tm, LANES), lambda i: (i, fb)), pl.BlockSpec((1, LANES), lambda i: (0, 0))],
        out_specs=pl.BlockSpec((tm, LANES), lambda i: (i, 0)),
        out_shape=jax.ShapeDtypeStruct((t, LANES), F32),
        scratch_shapes=[pltpu.VMEM((1, LANES), F32)],
        compiler_params=_params("arbitrary"),
    )(p, b_f)


def _fox_logit_bwd(dcum, p, c, b_f):
    t = p.shape[0]
    tm = _tile(t, 512, 8)
    fb = c.o_fl // LANES
    nt = t // tm

    def body(d_ref, f_ref, b_ref, o_ref, db_ref, carry):
        @pl.when(pl.program_id(0) == 0)
        def _():
            carry[...] = jnp.zeros_like(carry)
            db_ref[...] = jnp.zeros_like(db_ref)

        d = d_ref[0] + d_ref[1]
        dlogf = _mm(_tri(tm, False).astype(F32).T, d, HI) + carry[...]
        carry[...] += jnp.sum(d, axis=0, keepdims=True)
        df = dlogf * _sig(-(f_ref[...] + b_ref[...]))
        o_ref[...] = df.astype(BF16)
        db_ref[...] += jnp.sum(df, axis=0, keepdims=True)

    return pl.pallas_call(
        body, name="fox_logit_bwd", grid=(nt,),
        in_specs=[pl.BlockSpec((2, tm, LANES), lambda i: (0, nt - 1 - i, 0)),
                  pl.BlockSpec((tm, LANES), lambda i: (nt - 1 - i, fb)),
                  pl.BlockSpec((1, LANES), lambda i: (0, 0))],
        out_specs=[pl.BlockSpec((tm, LANES), lambda i: (nt - 1 - i, 0)), pl.BlockSpec((1, LANES), lambda i: (0, 0))],
        out_shape=[jax.ShapeDtypeStruct((t, LANES), BF16), jax.ShapeDtypeStruct((1, LANES), F32)],
        scratch_shapes=[pltpu.VMEM((1, LANES), F32)],
        compiler_params=_params("arbitrary"),
    )(dcum, p, b_f)


def _fox_scores(q, k, cq, ck, qi, ki, tq, tk):
    s = _mm_nt((q * (HEAD_DIM ** -0.5)).astype(BF16), k.astype(BF16)) + cq - ck
    qpos = qi * tq + lax.broadcasted_iota(jnp.int32, (tq, tk), 0)
    kpos = ki * tk + lax.broadcasted_iota(jnp.int32, (tq, tk), 1)
    mask = kpos <= qpos
    return jnp.where(mask, s, NEG), mask


def _fox_fwd(q, k, v, cq, ck, hb, tb):
    h, t, n = q.shape
    tq = tk = _tile(t, tb, LANES)
    nq = t // tq

    def body(q_ref, k_ref, v_ref, cq_ref, ck_ref, o_ref, lse_ref, m_sc, l_sc, acc_sc):
        qi, ki = pl.program_id(1), pl.program_id(2)

        @pl.when(ki == 0)
        def _():
            m_sc[...] = jnp.full_like(m_sc, NEG)
            l_sc[...] = jnp.zeros_like(l_sc)
            acc_sc[...] = jnp.zeros_like(acc_sc)

        @pl.when(ki <= qi)
        def _():
            for i in range(hb):
                s, _ = _fox_scores(q_ref[i], k_ref[i], cq_ref[i], ck_ref[i], qi, ki, tq, tk)
                m_old = m_sc[i]
                m_new = jnp.maximum(m_old, jnp.max(s, axis=1, keepdims=True))
                scale = jnp.exp(m_old - m_new)
                e = jnp.exp(s - m_new)
                l_sc[i] = scale * l_sc[i] + jnp.sum(e, axis=1, keepdims=True)
                acc_sc[i] = scale * acc_sc[i] + _mm(e.astype(BF16), v_ref[i].astype(BF16))
                m_sc[i] = m_new

        @pl.when(ki == qi)
        def _():
            o_ref[...] = acc_sc[...] / l_sc[...]
            lse_ref[...] = m_sc[...] + jnp.log(l_sc[...])

    qb = pl.BlockSpec((hb, tq, n), lambda g, i, j: (g, i, 0))
    kb = pl.BlockSpec((hb, tk, n), lambda g, i, j: (g, jnp.minimum(i, j), 0))
    col = pl.BlockSpec((hb, tq, 1), lambda g, i, j: (g, i, 0))
    return pl.pallas_call(
        body, name="fox_fwd", grid=(h // hb, nq, nq),
        in_specs=[qb, kb, kb, col, pl.BlockSpec((hb, 1, tk), lambda g, i, j: (g, 0, jnp.minimum(i, j)))],
        out_specs=[qb, col],
        out_shape=[jax.ShapeDtypeStruct((h, t, n), F32), jax.ShapeDtypeStruct((h, t, 1), F32)],
        scratch_shapes=[pltpu.VMEM((hb, tq, 1), F32), pltpu.VMEM((hb, tq, 1), F32), pltpu.VMEM((hb, tq, n), F32)],
        compiler_params=_params("parallel", "parallel", "arbitrary"),
    )(q, k, v, cq, ck)


def _fox_bwd_dq(q, k, v, cq, ck, lse, o, do, hb, tb):
    h, t, n = q.shape
    tq = tk = _tile(t, tb, LANES)
    nq = t // tq

    def body(q_ref, k_ref, v_ref, cq_ref, ck_ref, lse_ref, o_ref, do_ref, dq_ref, dcq_ref, acc_sc, row_sc):
        qi, ki = pl.program_id(1), pl.program_id(2)

        @pl.when(ki == 0)
        def _():
            acc_sc[...] = jnp.zeros_like(acc_sc)
            row_sc[...] = jnp.zeros_like(row_sc)

        @pl.when(ki <= qi)
        def _():
            for i in range(hb):
                s, mask = _fox_scores(q_ref[i], k_ref[i], cq_ref[i], ck_ref[i], qi, ki, tq, tk)
                dov = do_ref[i]
                delta = jnp.sum(dov * o_ref[i], axis=1, keepdims=True)
                pm = jnp.where(mask, jnp.exp(s - lse_ref[i]), 0.0)
                dp = _mm_nt(dov.astype(BF16), v_ref[i].astype(BF16))
                ds = pm * (dp - delta)
                acc_sc[i] += _mm(ds.astype(BF16), k_ref[i].astype(BF16))
                row_sc[i] += jnp.sum(ds, axis=1, keepdims=True)

        @pl.when(ki == qi)
        def _():
            dq_ref[...] = acc_sc[...] * (HEAD_DIM ** -0.5)
            dcq_ref[...] = row_sc[...]

    qb = pl.BlockSpec((hb, tq, n), lambda g, i, j: (g, i, 0))
    kb = pl.BlockSpec((hb, tk, n), lambda g, i, j: (g, jnp.minimum(i, j), 0))
    col = pl.BlockSpec((hb, tq, 1), lambda g, i, j: (g, i, 0))
    return pl.pallas_call(
        body, name="fox_bwd_dq", grid=(h // hb, nq, nq),
        in_specs=[qb, kb, kb, col, pl.BlockSpec((hb, 1, tk), lambda g, i, j: (g, 0, jnp.minimum(i, j))), col, qb, qb],
        out_specs=[qb, col],
        out_shape=[jax.ShapeDtypeStruct((h, t, n), F32), jax.ShapeDtypeStruct((h, t, 1), F32)],
        scratch_shapes=[pltpu.VMEM((hb, tq, n), F32), pltpu.VMEM((hb, tq, 1), F32)],
        compiler_params=_params("parallel", "parallel", "arbitrary"),
    )(q, k, v, cq, ck, lse, o, do)


def _fox_bwd_dkv(q, k, v, cq, ck, lse, o, do, hb, tb):
    h, t, n = q.shape
    tq = tk = _tile(t, tb, LANES)
    nq = t // tq

    def body(q_ref, k_ref, v_ref, cq_ref, ck_ref, lse_ref, o_ref, do_ref, dk_ref, dv_ref, dck_ref, dk_sc, dv_sc, dc_sc):
        ki, qi = pl.program_id(1), pl.program_id(2)

        @pl.when(qi == 0)
        def _():
            dk_sc[...] = jnp.zeros_like(dk_sc)
            dv_sc[...] = jnp.zeros_like(dv_sc)
            dc_sc[...] = jnp.zeros_like(dc_sc)

        @pl.when(qi >= ki)
        def _():
            for i in range(hb):
                s, mask = _fox_scores(q_ref[i], k_ref[i], cq_ref[i], ck_ref[i], qi, ki, tq, tk)
                dov = do_ref[i]
                delta = jnp.sum(dov * o_ref[i], axis=1, keepdims=True)
                pm = jnp.where(mask, jnp.exp(s - lse_ref[i]), 0.0)
                dob = dov.astype(BF16)
                dp = _mm_nt(dob, v_ref[i].astype(BF16))
                ds = pm * (dp - delta)
                dv_sc[i] += _mm_tn(pm.astype(BF16), dob)
                dk_sc[i] += _mm_tn(ds.astype(BF16), q_ref[i].astype(BF16))
                dc_sc[i] -= jnp.sum(ds, axis=0, keepdims=True)

        @pl.when(qi == nq - 1)
        def _():
            dk_ref[...] = dk_sc[...] * (HEAD_DIM ** -0.5)
            dv_ref[...] = dv_sc[...]
            dck_ref[...] = dc_sc[...]

    qb = pl.BlockSpec((hb, tq, n), lambda g, j, i: (g, jnp.maximum(i, j), 0))
    kb = pl.BlockSpec((hb, tk, n), lambda g, j, i: (g, j, 0))
    col = pl.BlockSpec((hb, tq, 1), lambda g, j, i: (g, jnp.maximum(i, j), 0))
    row = pl.BlockSpec((hb, 1, tk), lambda g, j, i: (g, 0, j))
    return pl.pallas_call(
        body, name="fox_bwd_dkv", grid=(h // hb, nq, nq),
        in_specs=[qb, kb, kb, col, row, col, qb, qb],
        out_specs=[kb, kb, row],
        out_shape=[jax.ShapeDtypeStruct((h, t, n), F32), jax.ShapeDtypeStruct((h, t, n), F32),
                   jax.ShapeDtypeStruct((h, 1, t), F32)],
        scratch_shapes=[pltpu.VMEM((hb, tk, n), F32), pltpu.VMEM((hb, tk, n), F32), pltpu.VMEM((hb, 1, tk), F32)],
        compiler_params=_params("parallel", "parallel", "arbitrary"),
    )(q, k, v, cq, ck, lse, o, do)


FOX_PAIRS = 2
FOX_HEADS_STEP = 2 * FOX_PAIRS


def _lane_half(shape, upper):
    li = lax.broadcasted_iota(jnp.int32, shape, len(shape) - 1)
    return (li >= HEAD_DIM) if upper else (li < HEAD_DIM)


def _col(block, j):
    li = lax.broadcasted_iota(jnp.int32, block.shape, 1)
    return jnp.sum(jnp.where(li == j, block, 0.0), axis=1, keepdims=True)


def _from_cols(cols):
    li = lax.broadcasted_iota(jnp.int32, (cols[0].shape[0], len(cols)), 1)
    out = jnp.zeros(li.shape, F32)
    for j, cj in enumerate(cols):
        out = jnp.where(li == j, cj, out)
    return out


def _from_rows(rows):
    si = lax.broadcasted_iota(jnp.int32, (len(rows), rows[0].shape[1]), 0)
    out = jnp.zeros(si.shape, F32)
    for j, rj in enumerate(rows):
        out = jnp.where(si == j, rj, out)
    return out


def _causal(tq, tk):
    return lax.broadcasted_iota(jnp.int32, (tq, tk), 1) <= lax.broadcasted_iota(jnp.int32, (tq, tk), 0)


def _fox_prep_t(p, c, b_f):
    t = p.shape[0]
    tm = _tile(t, 512, LANES)
    fb = c.o_fl // LANES

    def body(f_ref, b_ref, o_ref, carry):
        @pl.when(pl.program_id(0) == 0)
        def _():
            carry[...] = jnp.zeros_like(carry)

        logf = -_softplus(-(f_ref[...] + b_ref[...]))
        cum = _mm(_tri(tm, False).astype(F32), logf, HI) + carry[...]
        o_ref[...] = cum.T
        carry[...] += jnp.sum(logf, axis=0, keepdims=True)

    return pl.pallas_call(
        body, name="fox_prep", grid=(t // tm,),
        in_specs=[pl.BlockSpec((tm, LANES), lambda i: (i, fb)), pl.BlockSpec((1, LANES), lambda i: (0, 0))],
        out_specs=pl.BlockSpec((LANES, tm), lambda i: (0, i)),
        out_shape=jax.ShapeDtypeStruct((LANES, t), F32),
        scratch_shapes=[pltpu.VMEM((1, LANES), F32)],
        compiler_params=_params("arbitrary"),
    )(p, b_f)


def _fox2_fwd(p, c, cum_t, tb, ycat):
    t = p.shape[0]
    tq = tk = _tile(t, tb, LANES)
    nq = t // tq
    pw, nh = FOX_PAIRS * LANES, FOX_HEADS_STEP
    qb, kb, vb, gb = (o // pw for o in (c.o_fq, c.o_fk, c.o_fv, c.o_gfox))
    scale = HEAD_DIM ** -0.5

    def body(q_ref, k_ref, v_ref, g_ref, ck_ref, o_ref, y_ref, lse_ref, m_sc, l_sc, acc_sc):
        g, qi, ki = pl.program_id(0), pl.program_id(1), pl.program_id(2)

        @pl.when(ki == 0)
        def _():
            m_sc[...] = jnp.full_like(m_sc, NEG)
            l_sc[...] = jnp.zeros_like(l_sc)
            acc_sc[...] = jnp.zeros_like(acc_sc)

        def step(diag):
            ms, ls = [m_sc[h] for h in range(nh)], [l_sc[h] for h in range(nh)]
            accs = [acc_sc[:, pi * LANES:(pi + 1) * LANES] for pi in range(FOX_PAIRS)]
            for pi in range(FOX_PAIRS):
                lanes = slice(pi * LANES, (pi + 1) * LANES)
                q2 = (q_ref[:, lanes] * scale).astype(BF16)
                k2, v2 = k_ref[:, lanes].astype(BF16), v_ref[:, lanes].astype(BF16)
                new_acc = accs[pi]
                for hh in range(2):
                    hi = 2 * pi + hh
                    mk = _lane_half((tq, LANES), hh == 1)
                    s = _mm_nt(jnp.where(mk, q2, jnp.zeros_like(q2)), k2) - ck_ref[pl.ds(g * nh + hi, 1), :]
                    if diag:
                        s = jnp.where(_causal(tq, tk), s, NEG)
                    m_new = jnp.maximum(ms[hi], jnp.max(s, axis=1, keepdims=True))
                    a = jnp.exp(ms[hi] - m_new)
                    e = jnp.exp(s - jnp.concatenate([m_new] * (tk // LANES), axis=1))
                    ls[hi] = a * ls[hi] + jnp.sum(e, axis=1, keepdims=True)
                    ms[hi] = m_new
                    new_acc = jnp.where(mk, a * accs[pi] + _mm(e.astype(BF16), v2), new_acc)
                accs[pi] = new_acc
            for h in range(nh):
                m_sc[h] = ms[h]
                l_sc[h] = ls[h]
            for pi in range(FOX_PAIRS):
                acc_sc[:, pi * LANES:(pi + 1) * LANES] = accs[pi]

        @pl.when(ki < qi)
        def _():
            step(False)

        @pl.when(ki == qi)
        def _():
            step(True)
            li = lax.broadcasted_iota(jnp.int32, (tq, LANES), 1)
            lse = jnp.zeros((tq, LANES), F32)
            for pi in range(FOX_PAIRS):
                lanes = slice(pi * LANES, (pi + 1) * LANES)
                inv = jnp.where(_lane_half((tq, LANES), False), 1.0 / l_sc[2 * pi], 1.0 / l_sc[2 * pi + 1])
                o = acc_sc[:, lanes] * inv
                gate = g_ref[:, lanes]
                o_ref[:, lanes] = o
                y_ref[:, lanes] = (o * gate * _sig(gate)).astype(BF16)
            for h in range(nh):
                lse = jnp.where(li == h, m_sc[h] + jnp.log(l_sc[h]), lse)
            lse_ref[0] = lse

    row = lambda off: pl.BlockSpec((tq, pw), lambda g, i, j: (i, off + g))
    key = lambda off: pl.BlockSpec((tk, pw), lambda g, i, j: (jnp.minimum(i, j), off + g))
    out = pl.BlockSpec((tq, pw), lambda g, i, j: (i, g))
    return _pallas_into(
        body, ycat, 5, 1, name="fox_fwd", grid=(c.rw // pw, nq, nq),
        in_specs=[row(qb), key(kb), key(vb), row(gb),
                  pl.BlockSpec((LANES, tk), lambda g, i, j: (0, jnp.minimum(i, j)))],
        out_specs=[out, row(c.rw // pw), pl.BlockSpec((1, tq, LANES), lambda g, i, j: (g, i, 0))],
        out_shape=[jax.ShapeDtypeStruct((t, c.rw), F32), jax.ShapeDtypeStruct(ycat.shape, BF16),
                   jax.ShapeDtypeStruct((c.rw // pw, t, LANES), F32)],
        scratch_shapes=[pltpu.VMEM((nh, tq, LANES), F32), pltpu.VMEM((nh, tq, LANES), F32),
                        pltpu.VMEM((tq, pw), F32)],
        compiler_params=_params("parallel", "parallel", "arbitrary"),
    )(p, p, p, p, cum_t)


def _fox2_grads(q2, k2, v2, do2, o2, lse_h, ck, mk, diag, tq, tk):
    zero = jnp.zeros_like(q2)
    s = _mm_nt(jnp.where(mk, q2, zero), k2) - ck
    if diag:
        s = jnp.where(_causal(tq, tk), s, NEG)
    wide = lambda col: jnp.concatenate([jnp.broadcast_to(col, (tq, LANES))] * (tk // LANES), axis=1)
    pm = jnp.exp(s - wide(lse_h))
    delta = jnp.sum(jnp.where(mk, do2 * o2, 0.0), axis=1, keepdims=True)
    dob = do2.astype(BF16)
    dp = _mm_nt(jnp.where(mk, dob, zero), v2)
    return pm, pm * (dp - wide(delta)), dob


def _fox2_bwd_dq(p, c, cum_t, lse, o, do, tb, dp):
    t = p.shape[0]
    tq = tk = _tile(t, tb, LANES)
    nq = t // tq
    pw, nh = FOX_PAIRS * LANES, FOX_HEADS_STEP
    qb, kb, vb = (o_ // pw for o_ in (c.o_fq, c.o_fk, c.o_fv))
    scale = HEAD_DIM ** -0.5

    def body(q_ref, k_ref, v_ref, ck_ref, lse_ref, o_ref, do_ref, dq_ref, dcq_ref, acc_sc, row_sc):
        g, qi, ki = pl.program_id(0), pl.program_id(1), pl.program_id(2)

        @pl.when(ki == 0)
        def _():
            acc_sc[...] = jnp.zeros_like(acc_sc)
            row_sc[...] = jnp.zeros_like(row_sc)

        def step(diag):
            lse_blk = lse_ref[0]
            rows = [row_sc[h] for h in range(nh)]
            accs = [acc_sc[:, pi * LANES:(pi + 1) * LANES] for pi in range(FOX_PAIRS)]
            for pi in range(FOX_PAIRS):
                lanes = slice(pi * LANES, (pi + 1) * LANES)
                q2 = (q_ref[:, lanes] * scale).astype(BF16)
                k2, v2 = k_ref[:, lanes].astype(BF16), v_ref[:, lanes].astype(BF16)
                do2, o2 = do_ref[:, lanes], o_ref[:, lanes]
                new_acc = accs[pi]
                for hh in range(2):
                    hi = 2 * pi + hh
                    mk = _lane_half((tq, LANES), hh == 1)
                    _, ds, _ = _fox2_grads(q2, k2, v2, do2, o2, _col(lse_blk, hi),
                                           ck_ref[pl.ds(g * nh + hi, 1), :], mk, diag, tq, tk)
                    rows[hi] = rows[hi] + jnp.sum(ds, axis=1, keepdims=True)
                    new_acc = jnp.where(mk, accs[pi] + _mm(ds.astype(BF16), k2), new_acc)
                accs[pi] = new_acc
            for h in range(nh):
                row_sc[h] = rows[h]
            for pi in range(FOX_PAIRS):
                acc_sc[:, pi * LANES:(pi + 1) * LANES] = accs[pi]

        @pl.when(ki < qi)
        def _():
            step(False)

        @pl.when(ki == qi)
        def _():
            step(True)
            dq_ref[...] = (acc_sc[...] * scale).astype(BF16)
            dcq_ref[0] = _from_cols([row_sc[h] for h in range(nh)])

    row = lambda off: pl.BlockSpec((tq, pw), lambda g, i, j: (i, off + g))
    key = lambda off: pl.BlockSpec((tk, pw), lambda g, i, j: (jnp.minimum(i, j), off + g))
    stat = pl.BlockSpec((1, tq, nh), lambda g, i, j: (g, i, 0))
    return _pallas_into(
        body, dp, 7, 0, name="fox_bwd_dq", grid=(c.rw // pw, nq, nq),
        in_specs=[row(qb), key(kb), key(vb), pl.BlockSpec((LANES, tk), lambda g, i, j: (0, jnp.minimum(i, j))),
                  pl.BlockSpec((1, tq, LANES), lambda g, i, j: (g, i, 0)), row(0), row(0)],
        out_specs=[row(qb), stat],
        out_shape=[jax.ShapeDtypeStruct(dp.shape, BF16), jax.ShapeDtypeStruct((c.rw // pw, t, nh), F32)],
        scratch_shapes=[pltpu.VMEM((tq, pw), F32), pltpu.VMEM((nh, tq, 1), F32)],
        compiler_params=_params("parallel", "parallel", "arbitrary"),
    )(p, p, p, cum_t, lse, o, do)


def _fox2_bwd_dkv(p, c, cum_t, lse, o, do, tb, dp):
    t = p.shape[0]
    tq = tk = _tile(t, tb, LANES)
    nq = t // tq
    pw, nh = FOX_PAIRS * LANES, FOX_HEADS_STEP
    qb, kb, vb = (o_ // pw for o_ in (c.o_fq, c.o_fk, c.o_fv))
    scale = HEAD_DIM ** -0.5

    def body(q_ref, k_ref, v_ref, ck_ref, lse_ref, o_ref, do_ref, dk_ref, dv_ref, dck_ref, dk_sc, dv_sc, dc_sc):
        g, ki, qi = pl.program_id(0), pl.program_id(1), pl.program_id(2)

        @pl.when(qi == 0)
        def _():
            dk_sc[...] = jnp.zeros_like(dk_sc)
            dv_sc[...] = jnp.zeros_like(dv_sc)
            dc_sc[...] = jnp.zeros_like(dc_sc)

        def step(diag):
            lse_blk = lse_ref[0]
            dcs = [dc_sc[h] for h in range(nh)]
            dks = [dk_sc[:, pi * LANES:(pi + 1) * LANES] for pi in range(FOX_PAIRS)]
            dvs = [dv_sc[:, pi * LANES:(pi + 1) * LANES] for pi in range(FOX_PAIRS)]
            for pi in range(FOX_PAIRS):
                lanes = slice(pi * LANES, (pi + 1) * LANES)
                q2 = (q_ref[:, lanes] * scale).astype(BF16)
                k2, v2 = k_ref[:, lanes].astype(BF16), v_ref[:, lanes].astype(BF16)
                do2, o2 = do_ref[:, lanes], o_ref[:, lanes]
                new_dk, new_dv = dks[pi], dvs[pi]
                for hh in range(2):
                    hi = 2 * pi + hh
                    mk = _lane_half((tk, LANES), hh == 1)
                    pm, ds, dob = _fox2_grads(q2, k2, v2, do2, o2, _col(lse_blk, hi),
                                              ck_ref[pl.ds(g * nh + hi, 1), :], mk, diag, tq, tk)
                    dcs[hi] = dcs[hi] - jnp.sum(ds, axis=0, keepdims=True)
                    new_dv = jnp.where(mk, dvs[pi] + _mm_tn(pm.astype(BF16), dob), new_dv)
                    new_dk = jnp.where(mk, dks[pi] + _mm_tn(ds.astype(BF16), q2), new_dk)
                dks[pi], dvs[pi] = new_dk, new_dv
            for h in range(nh):
                dc_sc[h] = dcs[h]
            for pi in range(FOX_PAIRS):
                dk_sc[:, pi * LANES:(pi + 1) * LANES] = dks[pi]
                dv_sc[:, pi * LANES:(pi + 1) * LANES] = dvs[pi]

        @pl.when(qi > ki)
        def _():
            step(False)

        @pl.when(qi == ki)
        def _():
            step(True)

        @pl.when(qi == nq - 1)
        def _():
            dk_ref[...] = dk_sc[...].astype(BF16)
            dv_ref[...] = dv_sc[...].astype(BF16)
            dck_ref[0] = _from_rows([dc_sc[h] for h in range(nh)])

    row = lambda off: pl.BlockSpec((tq, pw), lambda g, j, i: (jnp.maximum(i, j), off + g))
    key = lambda off: pl.BlockSpec((tk, pw), lambda g, j, i: (j, off + g))
    return _pallas_into(
        body, dp, 7, 0, name="fox_bwd_dkv", grid=(c.rw // pw, nq, nq),
        in_specs=[row(qb), key(kb), key(vb), pl.BlockSpec((LANES, tk), lambda g, j, i: (0, j)),
                  pl.BlockSpec((1, tq, LANES), lambda g, j, i: (g, jnp.maximum(i, j), 0)), row(0), row(0)],
        out_specs=[key(kb), key(0), pl.BlockSpec((1, nh, tk), lambda g, j, i: (g, 0, j))],
        out_shape=[jax.ShapeDtypeStruct(dp.shape, BF16), jax.ShapeDtypeStruct((t, c.rw), BF16),
                   jax.ShapeDtypeStruct((c.rw // pw, nh, t), F32)],
        scratch_shapes=[pltpu.VMEM((tk, pw), F32), pltpu.VMEM((tk, pw), F32), pltpu.VMEM((nh, 1, tk), F32)],
        compiler_params=_params("parallel", "parallel", "arbitrary"),
    )(p, p, p, cum_t, lse, o, do)


def _mem_probs(q, mk, scale):
    s = _mm_nt(q.astype(BF16), mk.astype(BF16)) * scale
    e = jnp.exp(s - jnp.max(s, axis=1, keepdims=True))
    return e / jnp.sum(e, axis=1, keepdims=True)


def _mem_attn_fwd(p, c, mkv):
    t = p.shape[0]
    tm = _tile(t, 512, 8)
    dh = c.mhd
    qb = c.o_mq // dh
    scale = dh ** -0.5

    def body(q_ref, mk_ref, mv_ref, o_ref):
        pm = _mem_probs(q_ref[...], mk_ref[...], scale)
        o_ref[...] = _mm(pm.astype(BF16), mv_ref[...].astype(BF16))

    m = mkv.shape[0]
    return pl.pallas_call(
        body, name="mem_attn_fwd", grid=(t // tm, MEM_HEADS),
        in_specs=[pl.BlockSpec((tm, dh), lambda i, j: (i, qb + j)),
                  pl.BlockSpec((m, dh), lambda i, j: (0, j)),
                  pl.BlockSpec((m, dh), lambda i, j: (0, MEM_HEADS + j))],
        out_specs=pl.BlockSpec((tm, dh), lambda i, j: (i, j)),
        out_shape=jax.ShapeDtypeStruct((t, c.mw), F32),
        compiler_params=_params("parallel", "parallel"),
    )(p, mkv, mkv)


def _mem_attn_bwd(p, c, mkv, do):
    t = p.shape[0]
    tm = _tile(t, 512, 8)
    dh = c.mhd
    qb = c.o_mq // dh
    scale = dh ** -0.5
    m = mkv.shape[0]

    def body(q_ref, mk_ref, mv_ref, do_ref, dq_ref, dmk_ref, dmv_ref):
        @pl.when(pl.program_id(1) == 0)
        def _():
            dmk_ref[...] = jnp.zeros_like(dmk_ref)
            dmv_ref[...] = jnp.zeros_like(dmv_ref)

        qv = q_ref[...].astype(BF16)
        pm = _mem_probs(qv, mk_ref[...], scale)
        dob = do_ref[...].astype(BF16)
        dmv_ref[...] += _mm_tn(pm.astype(BF16), dob)
        dp = _mm_nt(dob, mv_ref[...].astype(BF16))
        ds = (pm * (dp - jnp.sum(pm * dp, axis=1, keepdims=True)) * scale).astype(BF16)
        dq_ref[...] = _mm(ds, mk_ref[...].astype(BF16)).astype(BF16)
        dmk_ref[...] += _mm_tn(ds, qv)

    kvb = lambda off: pl.BlockSpec((m, dh), lambda j, i: (0, off + j))
    return pl.pallas_call(
        body, name="mem_attn_bwd", grid=(MEM_HEADS, t // tm),
        in_specs=[pl.BlockSpec((tm, dh), lambda j, i: (i, qb + j)), kvb(0), kvb(MEM_HEADS),
                  pl.BlockSpec((tm, dh), lambda j, i: (i, j))],
        out_specs=[pl.BlockSpec((tm, dh), lambda j, i: (i, j)), kvb(0), kvb(0)],
        out_shape=[jax.ShapeDtypeStruct((t, c.mw), BF16), jax.ShapeDtypeStruct((m, c.mw), F32),
                   jax.ShapeDtypeStruct((m, c.mw), F32)],
        compiler_params=_params("parallel", "arbitrary"),
    )(p, mkv, mkv, do)


def _adamw(w, g, m, v, name):
    rows, cols = w.shape
    bc1 = 1.0 - ADAM_B1 ** ADAM_STEP
    bc2 = 1.0 - ADAM_B2 ** ADAM_STEP
    if rows % 8 and rows > 8:
        blk = pl.BlockSpec((rows, LANES), lambda i: (0, i))
        g_blk = pl.BlockSpec((g.shape[0], LANES), lambda i: (0, i))
        grid = (cols // LANES,)
    else:
        tm = _tile(rows, max(8, (1 << 18) // cols // 8 * 8), 8)
        blk = pl.BlockSpec((tm, cols), lambda i: (i, 0))
        g_blk = pl.BlockSpec((tm, g.shape[1]), lambda i: (i, 0))
        grid = (rows // tm,)
    brows, bcols = blk.block_shape

    def body(w_ref, g_ref, m_ref, v_ref, go_ref, d_ref, mo_ref, vo_ref):
        gv = g_ref[0:brows, 0:bcols]
        mn = ADAM_B1 * m_ref[...] + (1.0 - ADAM_B1) * gv
        vn = ADAM_B2 * v_ref[...] + (1.0 - ADAM_B2) * (gv * gv)
        go_ref[...] = gv
        mo_ref[...] = mn
        vo_ref[...] = vn
        d_ref[...] = -ADAM_LR * ((mn / bc1) / (jnp.sqrt(vn / bc2) + ADAM_EPS) + ADAM_WD * w_ref[...])

    shp = jax.ShapeDtypeStruct((rows, cols), F32)
    return pl.pallas_call(
        body, name=name, grid=grid,
        in_specs=[blk, g_blk, blk, blk],
        out_specs=[blk] * 4, out_shape=[shp] * 4,
        compiler_params=_params("parallel"),
    )(w, g, m, v)


SCAN_HEADS = 12
FOX_BLOCK = 512


def _local_step(c, x, mem, tgt, w, riders=None):
    t = x.shape[0]
    rw = c.rw
    riders = riders or {}
    carried = {}
    hd = lambda z: z.reshape(t, c.h, HEAD_DIM).transpose(1, 0, 2)
    uh = lambda z: z.transpose(1, 0, 2).reshape(t, rw)
    vecs = (w["mu"], w["w0"], w["a0"], w["k_k"], w["k_a"], w["wd"], w["wi"])

    h, rinv = _rms_fwd(x, w["g_pre"], "rms_pre")
    if "in_proj" in riders:
        groups, finish = riders["in_proj"]
        p, late = _matmul(h, w["wpt"], tb=True, name="in_proj", tk=4096, attach=groups)
        w = dict(w, **finish(late))
    else:
        p = _matmul(h, w["wpt"], tb=True, name="in_proj", tk=4096)
    r, lw, km, v, a, b = _rwkv_pre_fwd(p, c, *vecs)
    scan_in = tuple(hd(z) for z in (r, lw, km, v, a, b))
    hb = max(n for n in range(1, SCAN_HEADS + 1) if c.h % n == 0)
    ysh, states = _rwkv_scan_fwd(*scan_in, hb)
    ys = uh(ysh)
    ycat = _rwkv_post_fwd(ys, r, km, v, p, c, w["ln_w"], w["ln_b"], w["r_k"])

    cum_t = _fox_prep_t(p, c, w["b_f"])
    yfox, ycat, lse = _fox2_fwd(p, c, cum_t, FOX_BLOCK, ycat)

    memn, rinv_m = _rms_fwd(mem, w["g_mem"], "rms_mem")
    mkv = _matmul(memn, w["w_mem_kv"], name="mem_kv")
    ymem = _mem_attn_fwd(p, c, mkv)
    ycat = _gate_fwd(ymem, p, c.o_gmq, "gate_mem", ycat, 2 * rw)
    yo =_matmul(ycat, w["w_out"], name="out_proj", tn=512, tk=4096)
    loss, dout, dyo, dg_post = _post_loss(yo, x, tgt, w["g_post"], "post_loss")

    dyc = _matmul(dyo, w["w_out"], tb=True, name="d_ycat", tn=512, tk=4096)
    dw_out = _matmul(ycat, dyo, ta=True, name="d_w_out", tn=512, tk=4096, out_dtype=BF16)
    dys, dr2, dkm2, dv2, dp, dln_w, dln_b, dr_k = _rwkv_post_bwd(
        dyc, ys, r, km, v, p, c, w["ln_w"], w["ln_b"], w["r_k"])
    dyf, dp = _gate_bwd(dyc, rw, yfox, p, c.o_gfox, "gate_fox_bwd", dp)
    dym, dp = _gate_bwd(dyc, 2 * rw, ymem, p, c.o_gmq, "gate_mem_bwd", dp)

    scan_g = _rwkv_scan_bwd(*scan_in, states, hd(dys), hb)
    dps, dzw, dza, twb, alb, dw0, da0, dk_k, dk_a = _rwkv_pre_bwd(
        p, c, *vecs, *(uh(z) for z in scan_g), dr2, dkm2, dv2)
    dwd = _matmul(twb, dzw, ta=True, name="d_w_decay", out_dtype=BF16)
    dwi = _matmul(alb, dza, ta=True, name="d_w_iclr", out_dtype=BF16)
    dp, dmu = _shift_bwd(dps, p, c, w["mu"], dp)

    dp, dcq = _fox2_bwd_dq(p, c, cum_t, lse, yfox, dyf, FOX_BLOCK, dp)
    dp, dfv, dck = _fox2_bwd_dkv(p, c, cum_t, lse, yfox, dyf, FOX_BLOCK, dp)
    dcum = jnp.pad(jnp.stack([dcq.transpose(1, 0, 2).reshape(t, c.h), dck.reshape(c.h, t).T]),
                   ((0, 0), (0, 0), (0, LANES - c.h)))
    dfl, db_f = _fox_logit_bwd(dcum, p, c, w["b_f"])

    dmq, dmk, dmv = _mem_attn_bwd(p, c, mkv, dym)
    dmkv = jnp.concatenate([dmk, dmv], axis=1)
    dw_mkv = _matmul(memn, dmkv, ta=True, name="d_w_mem_kv", out_dtype=BF16)
    dmemn = _matmul(dmkv, w["w_mem_kv"], tb=True, name="d_memn")
    _, dg_mem = _rms_bwd(dmemn, mem, rinv_m, w["g_mem"], jnp.zeros_like(mem), "rms_mem_bwd")

    for off, piece in ((c.o_fv, dfv), (c.o_mq, dmq), (c.o_fl, dfl)):
        dp = lax.dynamic_update_slice(dp, piece, (0, off))
    rest = dict(wd=dwd, wi=dwi, w_mem_kv=dw_mkv, w_out=dw_out)
    if "d_w_in" in riders:
        dwp, carried["rest"] = _matmul(dp, h, ta=True, name="d_w_in", tk=4096, out_dtype=BF16,
                                       attach=riders["d_w_in"](rest))
    else:
        dwp = _matmul(dp, h, ta=True, name="d_w_in", tk=4096, out_dtype=BF16)
    if "d_h" in riders:
        dh, carried["wp"] = _matmul(dp, w["wpt"], name="d_h", tk=2944, attach=riders["d_h"](dwp))
    else:
        dh = _matmul(dp, w["wpt"], name="d_h", tk=2944)
    grad_x, dg_pre = _rms_bwd(dh, x, rinv, w["g_pre"], dout, "rms_pre_bwd")

    small = dict(g_pre=dg_pre, mu=dmu, w0=dw0, a0=da0, k_k=dk_k, k_a=dk_a, r_k=dr_k, ln_w=dln_w, ln_b=dln_b,
                 b_f=db_f, g_mem=dg_mem, g_post=dg_post)
    return loss, grad_x, dict(wp=dwp, **rest), small, carried


CHIPS = ((1, 0, 0), (0, 1, 0), (1, 1, 0))
SIBLING = ((0, 0, 1),)
ALL_PEERS = tuple((i, j, k) for i in (0, 1) for j in (0, 1) for k in (0, 1))[1:]


def _chip_of(pos):
    return 2 * pos[0] + pos[1]


DMA_CHUNK = 4 << 20


def _pieces(shape, itemsize):
    lead, (rows, cols) = shape[:-2], shape[-2:]
    k = 1
    if rows % 16 == 0:
        k = max(1, min(rows // 16, -(-rows * cols * itemsize // DMA_CHUNK)))
        while rows % k or (rows // k) % 16:
            k -= 1
    band = rows // k
    idxs = [()]
    for n in lead:
        idxs = [i + (j,) for i in idxs for j in range(n)]
    return [i + (pl.ds(j * band, band),) for i in idxs for j in range(k)]


def _peer_of(me, mask):
    return tuple(1 - v if f else v for v, f in zip(me, mask))


def _exchange(name, groups):
    n = len(groups)
    plan = _plan(groups)

    def body(*refs):
        copies = _copies(groups, plan, refs[:n], refs[n:2 * n], refs[2 * n], refs[2 * n + 1])
        for cp in copies:
            cp.start()
        for cp in copies:
            cp.wait()

    any_spec = pl.BlockSpec(memory_space=pl.ANY)
    return pl.pallas_call(
        body, name=name,
        in_specs=[any_spec] * n, out_specs=[any_spec] * n,
        out_shape=_exchange_shapes(groups),
        input_output_aliases={gi: gi for gi, g in enumerate(groups) if g.get("inplace")},
        scratch_shapes=[pltpu.SemaphoreType.DMA((len(plan),)), pltpu.SemaphoreType.DMA((len(plan),))],
    )(*[g["src"] for g in groups])


def _plan(groups):
    return [(gi, ti, idx) for gi, g in enumerate(groups) for ti in range(len(g["transfers"]))
            for idx in _pieces(tuple(g["piece"]), g["src"].dtype.itemsize)]


def _exchange_shapes(groups):
    lead = lambda s: tuple(s) if isinstance(s, tuple) else (s,)
    return [jax.ShapeDtypeStruct(lead(g["slots"]) + tuple(g["piece"]), g["src"].dtype) for g in groups]


def _copies(groups, plan, srcs, outs, send_sems, recv_sems):
    me = (lax.axis_index("x"), lax.axis_index("y"), lax.axis_index("c"))
    copies = []
    for k, (gi, ti, idx) in enumerate(plan):
        mask, view, slot = groups[gi]["transfers"][ti]
        peer = _peer_of(me, mask)
        copies.append(pltpu.make_async_remote_copy(
            src_ref=view(srcs[gi], me, peer).at[idx], dst_ref=outs[gi].at[slot(me, peer)].at[idx],
            send_sem=send_sems.at[k], recv_sem=recv_sems.at[k],
            device_id=peer, device_id_type=MESH))
    return copies


def _my_chip():
    return 2 * lax.axis_index("x") + lax.axis_index("y")


def _put(buf, block, slot):
    return lax.dynamic_update_slice(buf, block[None], (slot,) + (0,) * block.ndim)


def _sum_slots(recv, own, k, out_dtype, name):
    s, rows, cols = recv.shape
    budget = max(16, ((4 << 20) // ((s + 1) * cols * 4)) // 16 * 16)
    tr = _tile(rows, budget, 16)
    own_many = own.shape[0] > 1

    def body(k_ref, *refs):
        out_ref = refs[s + 1]
        mine = refs[s][0].astype(F32)
        acc = None
        for i in range(s):
            term = jnp.where(k_ref[0] == i, mine, refs[i][0].astype(F32))
            acc = term if acc is None else acc + term
        out_ref[...] = acc.astype(out_ref.dtype)

    def slot_spec(i):
        return pl.BlockSpec((1, tr, cols), lambda j, kr: (jnp.where(kr[0] == i, (i + 1) % s, i), j, 0))

    grid_spec = pltpu.PrefetchScalarGridSpec(
        num_scalar_prefetch=1, grid=(rows // tr,),
        in_specs=[slot_spec(i) for i in range(s)]
                 + [pl.BlockSpec((1, tr, cols), lambda j, kr: (kr[0] if own_many else 0, j, 0))],
        out_specs=pl.BlockSpec((tr, cols), lambda j, kr: (j, 0)))
    return pl.pallas_call(
        body, name=name, grid_spec=grid_spec,
        out_shape=jax.ShapeDtypeStruct((rows, cols), out_dtype),
        compiler_params=_params("parallel"),
    )(k, *([recv] * s), own)


def _all_gather(shards):
    return _gather_finish(shards, _exchange("gather_chips", _gather_groups(shards)), "gather_pair")


def _gather_groups(shards):
    halves = [s.reshape(2, s.shape[0] // 2, s.shape[1]) for s in shards]
    return [dict(src=q, slots=(4, 2), piece=q.shape[1:],
                 transfers=[(m, lambda ref, me, peer: ref.at[me[2]], lambda me, peer: (_chip_of(me), me[2]))
                            for m in CHIPS])
            for q in halves]


def _gather_finish(shards, first, name):
    spot = lambda m: (lambda me: (_chip_of(_peer_of(me, m)), me[2]))
    both = _exchange(name, [
        dict(src=q, slots=(4, 2), piece=q.shape[2:], inplace=True,
             transfers=[(SIBLING[0], (lambda f: lambda ref, me, peer: ref.at[f(me)])(spot(m)),
                         (lambda f: lambda me, peer: f(me))(spot(m))) for m in CHIPS])
        for q in first])
    return [_put(q.reshape((4,) + s.shape), s, _my_chip()) for s, q in zip(shards, both)]


def _reduce_pair(partials, tag):
    core1 = lax.axis_index("c").reshape(1).astype(jnp.int32)
    halves = [q.reshape(4, 2, q.shape[1] // 2, q.shape[2]).transpose(1, 0, 2, 3) for q in partials]
    pair = _exchange("reduce_pair_" + tag, [
        dict(src=q, slots=2, piece=q.shape[1:],
             transfers=[(SIBLING[0], lambda ref, me, peer: ref.at[peer[2]], lambda me, peer: me[2])])
        for q in halves])
    flat = lambda e: e.reshape(2, -1, e.shape[-1])
    return [_sum_slots(flat(e), flat(q), core1, BF16, "reduce_pair_sum_" + tag).reshape(q.shape[1:])
            for e, q in zip(pair, halves)]


def _reduce_chips_groups(chip_sums):
    return [dict(src=q, slots=4, piece=q.shape[1:],
                 transfers=[(m, lambda ref, me, peer: ref.at[_chip_of(peer)], lambda me, peer: _chip_of(me))
                            for m in CHIPS])
            for q in chip_sums]


def _reduce_finish(crossed, chip_sums, tag):
    core = lax.axis_index("c")
    chip1 = _my_chip().reshape(1).astype(jnp.int32)
    sums = [_sum_slots(e, q, chip1, F32, "reduce_chips_sum_" + tag) for e, q in zip(crossed, chip_sums)]
    swapped = _exchange("reduce_swap_" + tag, [
        dict(src=q, slots=2, piece=q.shape, transfers=[(SIBLING[0], lambda ref, me, peer: ref, lambda me, peer: me[2])])
        for q in sums])
    return [_put(e, q, core).reshape(-1, e.shape[-1]) for e, q in zip(swapped, sums)]


def _reduce_scatter(partials):
    chip_sums = _reduce_pair(partials, "all")
    return _reduce_finish(_exchange("reduce_chips", _reduce_chips_groups(chip_sums)), chip_sums, "all")


def _all_reduce_small(vec):
    dev = 4 * lax.axis_index("x") + 2 * lax.axis_index("y") + lax.axis_index("c")
    got = _exchange("reduce_small", [
        dict(src=vec, slots=8, piece=vec.shape,
             transfers=[(m, lambda ref, me, peer: ref, lambda me, peer: 4 * me[0] + 2 * me[1] + me[2])
                        for m in ALL_PEERS])])[0]
    return _sum_slots(got, vec[None], dev.reshape(1).astype(jnp.int32), F32, "reduce_small_sum")


SMALL = ("g_pre", "mu", "w0", "a0", "k_k", "k_a", "r_k", "ln_w", "ln_b", "b_f", "g_mem", "g_post")


def _pad_cols(a, n):
    return jnp.pad(a, ((0, 0),) * (a.ndim - 1) + ((0, n - a.shape[-1]),))


def kernel(x, mem, g_pre, w_in, mu_rwkv, w0, w_decay_up, a0, w_iclr_up, k_k, k_a, r_k, ln_x_w, ln_x_b, b_f, g_mem, w_mem_kv, w_out, g_post, loss_target, m_g_pre, m_w_in, m_mu_rwkv, m_w0, m_w_decay_up, m_a0, m_w_iclr_up, m_k_k, m_k_a, m_r_k, m_ln_x_w, m_ln_x_b, m_b_f, m_g_mem, m_w_mem_kv, m_w_out, m_g_post, v_g_pre, v_w_in, v_mu_rwkv, v_w0, v_w_decay_up, v_a0, v_w_iclr_up, v_k_k, v_k_a, v_r_k, v_ln_x_w, v_ln_x_b, v_b_f, v_g_mem, v_w_mem_kv, v_w_out, v_g_post):
    d = x.shape[-1]
    c = Cfg(d)
    ws = w_in.shape[-1]
    wpad = -(-ws // LANES) * LANES
    nh = c.h

    pad_rows = lambda a, n: jnp.pad(a, ((0, n - a.shape[0]), (0, 0)))
    g_in, g_wd, g_wi = _all_gather([
        pad_rows(w_in[0].T.astype(BF16), wpad), w_decay_up[0].astype(BF16), w_iclr_up[0].astype(BF16)])
    fl = c.ref_fl
    runs = [(0, fl, 0), (fl, fl + nh, c.o_fl), (fl + nh, c.in_width, fl)]
    pieces = []
    for lo, hi, _ in sorted(runs, key=lambda r: r[2]):
        for s in range(4):
            a, b = max(lo, s * ws), min(hi, (s + 1) * ws)
            if a < b:
                pieces.append(g_in[s, a - s * ws:b - s * ws, :])
    wpt = jnp.concatenate(pieces + [jnp.zeros((LANES - nh, d), BF16)], axis=0)
    unshard = lambda g: g.transpose(1, 0, 2).reshape(g.shape[1], -1)
    weights = dict(wpt=wpt, wd=unshard(g_wd), wi=unshard(g_wi),
                   g_pre=g_pre, mu=mu_rwkv, w0=w0, a0=a0, k_k=k_k, k_a=k_a, r_k=r_k.reshape(1, -1),
                   ln_w=ln_x_w, ln_b=ln_x_b, b_f=_pad_cols(b_f, LANES), g_mem=g_mem, g_post=g_post)
    late_shards = [w_out[0].astype(BF16), w_mem_kv[0].astype(BF16)]

    def late_weights(first):
        g_out, g_mkv = _gather_finish(late_shards, first, "gather_pair_late")
        return dict(w_out=g_out.reshape(-1, d), w_mem_kv=g_mkv.reshape(d, -1))

    by_chip = lambda g: jnp.stack(jnp.split(g, 4, axis=1))
    pair_sums = {}

    def ride_rest(g):
        pair_sums["rest"] = _reduce_pair([g["w_out"].reshape(4, -1, d), g["w_mem_kv"].reshape(4, d // 4, -1),
                                          by_chip(g["wd"]), by_chip(g["wi"])], "rest")
        return _reduce_chips_groups(pair_sums["rest"])

    def ride_wp(dwpt):
        shards = []
        for s in range(4):
            rows = []
            for lo, hi, at in runs:
                a, b = max(lo, s * ws), min(hi, (s + 1) * ws)
                if a < b:
                    rows.append(dwpt[at + a - lo:at + b - lo, :])
            shards.append(jnp.concatenate(rows + [jnp.zeros((wpad - ws, d), BF16)], axis=0))
        pair_sums["wp"] = _reduce_pair([jnp.stack(shards)], "w_in")
        return _reduce_chips_groups(pair_sums["wp"])

    loss, grad_x, _, small, carried = _local_step(
        c, x[0], mem[0], loss_target[0], weights,
        riders={"in_proj": (_gather_groups(late_shards), late_weights), "d_w_in": ride_rest, "d_h": ride_wp})
    red = (_reduce_finish(carried["wp"], pair_sums["wp"], "w_in")
           + _reduce_finish(carried["rest"], pair_sums["rest"], "rest"))
    big_w = (w_in[0].T, w_out[0], w_mem_kv[0], w_decay_up[0], w_iclr_up[0])
    big_m = (m_w_in[0].T, m_w_out[0], m_w_mem_kv[0], m_w_decay_up[0], m_w_iclr_up[0])
    big_v = (v_w_in[0].T, v_w_out[0], v_w_mem_kv[0], v_w_decay_up[0], v_w_iclr_up[0])
    big_names = ("w_in", "w_out", "w_mem_kv", "w_decay_up", "w_iclr_up")
    upd = {n: _adamw(w_, g_, m_, v_, "adamw_" + n) for n, w_, g_, m_, v_ in zip(big_names, big_w, red, big_m, big_v)}
    upd["w_in"] = [o.T for o in upd["w_in"]]

    small_w = dict(g_pre=g_pre, mu=mu_rwkv, w0=w0, a0=a0, k_k=k_k, k_a=k_a, r_k=r_k.reshape(1, -1), ln_w=ln_x_w,
                   ln_b=ln_x_b, b_f=b_f, g_mem=g_mem, g_post=g_post)
    small_m = dict(g_pre=m_g_pre, mu=m_mu_rwkv, w0=m_w0, a0=m_a0, k_k=m_k_k, k_a=m_k_a, r_k=m_r_k.reshape(1, -1),
                   ln_w=m_ln_x_w, ln_b=m_ln_x_b, b_f=m_b_f, g_mem=m_g_mem, g_post=m_g_post)
    small_v = dict(g_pre=v_g_pre, mu=v_mu_rwkv, w0=v_w0, a0=v_a0, k_k=v_k_k, k_a=v_k_a, r_k=v_r_k.reshape(1, -1),
                   ln_w=v_ln_x_w, ln_b=v_ln_x_b, b_f=v_b_f, g_mem=v_g_mem, g_post=v_g_post)
    widths = [-(-small_w[n].shape[1] // LANES) * LANES for n in SMALL]
    pack = lambda t: jnp.concatenate([_pad_cols(t[n], wd_) for n, wd_ in zip(SMALL, widths)]
                                     + [jnp.zeros((1, LANES), F32)], axis=1)
    g_packed = jnp.concatenate([_pad_cols(small[n], wd_) for n, wd_ in zip(SMALL, widths)]
                               + [_pad_cols(loss, LANES)], axis=1)
    g_sum = _all_reduce_small(g_packed)
    s_upd = _adamw(pack(small_w), g_sum, pack(small_m), pack(small_v), "adamw_small")
    offs = [sum(widths[:i]) for i in range(len(SMALL))]

    def take(kind, n):
        i = SMALL.index(n)
        piece = s_upd[kind][:, offs[i]:offs[i] + small_w[n].shape[1]]
        return piece.reshape(r_k.shape) if n == "r_k" else piece

    total_loss = g_sum[0, sum(widths)]
    order = ("g_pre", "w_in", "mu", "w0", "w_decay_up", "a0", "w_iclr_up", "k_k", "k_a", "r_k", "ln_w", "ln_b", "b_f",
             "g_mem", "w_mem_kv", "w_out", "g_post")
    outs = [total_loss, grad_x[None]]
    for kind in range(4):
        for n in order:
            outs.append(upd[n][kind][None] if n in upd else take(kind, n))
    return tuple(outs)
```

```python
import jax
import jax.numpy as jnp
from jax import lax
from jax.experimental import pallas as pl
from jax.experimental.pallas import tpu as pltpu

F32 = jnp.float32
BF16 = jnp.bfloat16
HI = lax.Precision.HIGHEST
MESH = pl.DeviceIdType.MESH

HEAD_DIM = 64
MEM_HEADS = 4
LORA = 128
CHUNK = 64
RMS_EPS = 1e-6
GN_EPS = 64e-5
LANES = 128
VMEM_LIMIT = 56 * 1024 * 1024

ADAM_LR, ADAM_B1, ADAM_B2, ADAM_EPS, ADAM_WD, ADAM_STEP = 0.001, 0.9, 0.999, 1e-08, 0.01, 10


class Cfg:
    def __init__(self, d):
        self.d = d
        self.rw = 3 * d // 8
        self.mw = d // 4
        self.h = self.rw // HEAD_DIM
        self.mhd = self.mw // MEM_HEADS
        self.shift = 3 * self.rw + 2 * LORA
        self.in_width = self.shift + 5 * self.rw + self.h + 2 * self.mw
        o = self.shift
        self.o_grw = o; o += self.rw
        self.o_fq = o; o += self.rw
        self.o_fk = o; o += self.rw
        self.o_fv = o; o += self.rw
        self.o_gfox = o; o += self.rw
        self.o_mq = o; o += self.mw
        self.o_gmq = o; o += self.mw
        self.o_fl = o; o += LANES
        self.wp = o
        self.ref_fl = self.shift + 4 * self.rw


def _tile(n, pref, align=LANES):
    if n <= pref:
        return n
    t = (pref // align) * align
    while t >= align:
        if n % t == 0:
            return t
        t -= align
    return n


def _params(*sem):
    return pltpu.CompilerParams(dimension_semantics=sem, vmem_limit_bytes=VMEM_LIMIT)


def _pallas_into(body, into, n_in, out_index, in_specs, **kw):
    if into is None:
        return pl.pallas_call(body, in_specs=in_specs, **kw)

    def body_with_alias(*refs):
        return body(*refs[:n_in], *refs[n_in + 1:])

    call = pl.pallas_call(body_with_alias, in_specs=list(in_specs) + [pl.BlockSpec(memory_space=pl.ANY)],
                          input_output_aliases={n_in: out_index}, **kw)
    return lambda *args: call(*args, into)


def _sig(x):
    return 1.0 / (1.0 + jnp.exp(-x))


def _softplus(x):
    return jnp.maximum(x, 0.0) + jnp.log(1.0 + jnp.exp(-jnp.abs(x)))


def _dot(a, b, dims, prec=None):
    return lax.dot_general(a, b, (dims, ((), ())), precision=prec, preferred_element_type=F32)


def _mm(a, b, prec=None):
    return _dot(a, b, ((1,), (0,)), prec)


def _mm_nt(a, b, prec=None):
    return _dot(a, b, ((1,), (1,)), prec)


def _mm_tn(a, b, prec=None):
    return _dot(a, b, ((0,), (0,)), prec)


def _split(a):
    hi = a.astype(BF16)
    return hi, (a - hi.astype(F32)).astype(BF16)


def _dot3(a, b, dims, passes=3):
    d = lambda x, y: lax.dot_general(x, y, dims, preferred_element_type=F32)
    if passes == 1:
        return d(a.astype(BF16), b.astype(BF16))
    (ah, al), (bh, bl) = _split(a), _split(b)
    return d(ah, bh) + (d(ah, bl) + d(al, bh))


def _bmm(a, b, passes=3):
    return _dot3(a, b, (((2,), (1,)), ((0,), (0,))), passes)


def _bmm_nt(a, b, passes=3):
    return _dot3(a, b, (((2,), (2,)), ((0,), (0,))), passes)


def _bmm_tn(a, b, passes=3):
    return _dot3(a, b, (((1,), (1,)), ((0,), (0,))), passes)


def _bmm_01(m01, x):
    x1 = x.astype(BF16)
    r1 = x - x1.astype(F32)
    x2 = r1.astype(BF16)
    x3 = (r1 - x2.astype(F32)).astype(BF16)
    d = lambda y: lax.dot_general(m01, y, (((2,), (1,)), ((0,), (0,))), preferred_element_type=F32)
    return d(x1) + (d(x2) + d(x3))


def _matmul(a, b, *, ta=False, tb=False, out_dtype=F32, name, tm=1024, tn=1024, tk=1024, attach=None):
    m, k = (a.shape[1], a.shape[0]) if ta else a.shape
    n = b.shape[0] if tb else b.shape[1]
    tm, tn, tk = _tile(m, tm), _tile(n, tn), _tile(k, tk)
    nk = k // tk
    grid = (m // tm, n // tn, nk)
    dims = ((0 if ta else 1,), (1 if tb else 0,))
    groups = attach or []
    ng = len(groups)
    plan = _plan(groups)

    def body(a_ref, b_ref, *rest):
        srcs, o_ref, outs, scratch = rest[:ng], rest[ng], rest[ng + 1:2 * ng + 1], rest[2 * ng + 1:]
        acc = scratch[0] if nk > 1 else None
        if ng:
            copies = _copies(groups, plan, srcs, outs, scratch[-2], scratch[-1])
            ids = [pl.program_id(ax) for ax in range(3)]

            @pl.when((ids[0] == 0) & (ids[1] == 0) & (ids[2] == 0))
            def _():
                for cp in copies:
                    cp.start()

        part = _dot(a_ref[...].astype(BF16), b_ref[...].astype(BF16), dims)
        if nk == 1:
            o_ref[...] = part.astype(o_ref.dtype)
        else:
            kk = pl.program_id(2)

            @pl.when(kk == 0)
            def _():
                acc[...] = part

            @pl.when(kk > 0)
            def _():
                acc[...] += part

            @pl.when(kk == nk - 1)
            def _():
                o_ref[...] = acc[...].astype(o_ref.dtype)

        if ng:
            @pl.when((ids[0] == grid[0] - 1) & (ids[1] == grid[1] - 1) & (ids[2] == grid[2] - 1))
            def _():
                for cp in copies:
                    cp.wait()

    a_spec = pl.BlockSpec((tk, tm), lambda i, j, kk: (kk, i)) if ta else pl.BlockSpec((tm, tk), lambda i, j, kk: (i, kk))
    b_spec = pl.BlockSpec((tn, tk), lambda i, j, kk: (j, kk)) if tb else pl.BlockSpec((tk, tn), lambda i, j, kk: (kk, j))
    any_spec = pl.BlockSpec(memory_space=pl.ANY)
    sems = [pltpu.SemaphoreType.DMA((len(plan),)), pltpu.SemaphoreType.DMA((len(plan),))] if ng else []
    res = pl.pallas_call(
        body, name=name, grid=grid,
        in_specs=[a_spec, b_spec] + [any_spec] * ng,
        out_specs=[pl.BlockSpec((tm, tn), lambda i, j, kk: (i, j))] + [any_spec] * ng,
        out_shape=[jax.ShapeDtypeStruct((m, n), out_dtype)] + _exchange_shapes(groups),
        scratch_shapes=([pltpu.VMEM((tm, tn), F32)] if nk > 1 else []) + sems,
        compiler_params=_params(*(("arbitrary",) * 3 if ng else ("parallel", "parallel", "arbitrary"))),
    )(a, b, *[g["src"] for g in groups])
    return (res[0], list(res[1:])) if ng else res[0]


def _rms_fwd(x, g, name):
    t, d = x.shape
    tm = _tile(t, 256, 8)

    def body(x_ref, g_ref, h_ref, r_ref):
        xv = x_ref[...]
        r = lax.rsqrt(jnp.mean(xv * xv, axis=-1, keepdims=True) + RMS_EPS)
        h_ref[...] = (xv * r * g_ref[...]).astype(BF16)
        r_ref[...] = r

    return pl.pallas_call(
        body, name=name, grid=(t // tm,),
        in_specs=[pl.BlockSpec((tm, d), lambda i: (i, 0)), pl.BlockSpec((1, d), lambda i: (0, 0))],
        out_specs=[pl.BlockSpec((tm, d), lambda i: (i, 0)), pl.BlockSpec((tm, 1), lambda i: (i, 0))],
        out_shape=[jax.ShapeDtypeStruct((t, d), BF16), jax.ShapeDtypeStruct((t, 1), F32)],
        compiler_params=_params("parallel"),
    )(x, g)


def _rms_bwd(dh, x, rinv, g, add, name):
    t, d = x.shape
    tm = _tile(t, 256, 8)

    def body(dh_ref, x_ref, r_ref, g_ref, add_ref, dx_ref, dg_ref):
        @pl.when(pl.program_id(0) == 0)
        def _():
            dg_ref[...] = jnp.zeros_like(dg_ref)

        r = r_ref[...]
        xn = x_ref[...] * r
        dhv = dh_ref[...]
        dg_ref[...] += jnp.sum(dhv * xn, axis=0, keepdims=True)
        dxn = dhv * g_ref[...]
        dx_ref[...] = add_ref[...] + r * (dxn - xn * jnp.mean(dxn * xn, axis=-1, keepdims=True))

    row = pl.BlockSpec((tm, d), lambda i: (i, 0))
    vec = pl.BlockSpec((1, d), lambda i: (0, 0))
    return pl.pallas_call(
        body, name=name, grid=(t // tm,),
        in_specs=[row, row, pl.BlockSpec((tm, 1), lambda i: (i, 0)), vec, row],
        out_specs=[row, vec],
        out_shape=[jax.ShapeDtypeStruct((t, d), F32), jax.ShapeDtypeStruct((1, d), F32)],
        compiler_params=_params("arbitrary"),
    )(dh, x, rinv, g, add)


def _post_loss(yo, x, tgt, g, name):
    t, d = x.shape
    tm = _tile(t, 256, 8)

    def body(yo_ref, x_ref, t_ref, g_ref, loss_ref, dout_ref, dyo_ref, dg_ref):
        @pl.when(pl.program_id(0) == 0)
        def _():
            dg_ref[...] = jnp.zeros_like(dg_ref)
            loss_ref[...] = jnp.zeros_like(loss_ref)

        yv = yo_ref[...]
        r = lax.rsqrt(jnp.mean(yv * yv, axis=-1, keepdims=True) + RMS_EPS)
        n = yv * r
        err = x_ref[...] + n * g_ref[...] - t_ref[...]
        loss_ref[...] += 0.5 * jnp.sum(jnp.mean(err * err, axis=-1, keepdims=True), axis=0, keepdims=True)
        dout = err * (1.0 / d)
        dout_ref[...] = dout
        dg_ref[...] += jnp.sum(dout * n, axis=0, keepdims=True)
        dn = dout * g_ref[...]
        dyo_ref[...] = (r * (dn - n * jnp.mean(dn * n, axis=-1, keepdims=True))).astype(BF16)

    row = pl.BlockSpec((tm, d), lambda i: (i, 0))
    vec = pl.BlockSpec((1, d), lambda i: (0, 0))
    return pl.pallas_call(
        body, name=name, grid=(t // tm,),
        in_specs=[row, row, row, vec],
        out_specs=[pl.BlockSpec((1, 1), lambda i: (0, 0)), row, row, vec],
        out_shape=[jax.ShapeDtypeStruct((1, 1), F32), jax.ShapeDtypeStruct((t, d), F32),
                   jax.ShapeDtypeStruct((t, d), BF16), jax.ShapeDtypeStruct((1, d), F32)],
        compiler_params=_params("arbitrary"),
    )(yo, x, tgt, g)


def _head_sum(x):
    ri = lax.broadcasted_iota(jnp.int32, (LANES, LANES), 0) // HEAD_DIM
    ci = lax.broadcasted_iota(jnp.int32, (LANES, LANES), 1) // HEAD_DIM
    e = (ri == ci).astype(BF16)
    x1 = x.astype(BF16)
    r1 = x - x1.astype(F32)
    x2 = r1.astype(BF16)
    x3 = (r1 - x2.astype(F32)).astype(BF16)
    parts = []
    for i in range(x.shape[1] // LANES):
        sl = slice(i * LANES, (i + 1) * LANES)
        parts.append(_mm(x1[:, sl], e) + (_mm(x2[:, sl], e) + _mm(x3[:, sl], e)))
    return parts[0] if len(parts) == 1 else jnp.concatenate(parts, axis=1)


def _shifted(p_cur, before, first, mu):
    rolled = pltpu.roll(p_cur, 1, 0)
    prev_row = jnp.where(first, 0.0, before)
    row0 = lax.broadcasted_iota(jnp.int32, p_cur.shape, 0) == 0
    prev = jnp.where(row0, prev_row, rolled)
    return p_cur + (prev - p_cur) * mu, prev


def _rwkv_features(ps, rw, w0, a0, k_k, k_a, wd, wi):
    r, k, v = ps[:, 0:rw], ps[:, rw:2 * rw], ps[:, 2 * rw:3 * rw]
    wl, al = ps[:, 3 * rw:3 * rw + LORA], ps[:, 3 * rw + LORA:3 * rw + 2 * LORA]
    tw = jnp.tanh(wl)
    zw = w0 + _mm(tw.astype(BF16), wd)
    logw = -jnp.exp(-_softplus(-zw) - 0.5)
    alpha = _sig(a0 + _mm(al.astype(BF16), wi))
    kkr = k * k_k
    n2 = _head_sum(kkr * kkr)
    rn = lax.rsqrt(jnp.maximum(n2, 1e-24))
    kk = kkr * rn
    kmod = k * (1.0 + (alpha - 1.0) * k_a)
    return dict(r=r, k=k, v=v, tw=tw, al=al, zw=zw, logw=logw, alpha=alpha, kk=kk, rn=rn, n2=n2, kmod=kmod)


def _rwkv_pre_fwd(p, c, mu, w0, a0, k_k, k_a, wd, wi):
    t = p.shape[0]
    tm = _tile(t, 128, 8)
    rw, sh = c.rw, c.shift

    def body(p_ref, pp_ref, mu_ref, w0_ref, a0_ref, kk_ref, ka_ref, wd_ref, wi_ref,
             r_ref, lw_ref, km_ref, v_ref, a_ref, b_ref):
        ps, _ = _shifted(p_ref[...], pp_ref[7:8, :], pl.program_id(0) == 0, mu_ref[...])
        f = _rwkv_features(ps, rw, w0_ref[...], a0_ref[...], kk_ref[...], ka_ref[...], wd_ref[...], wi_ref[...])
        r_ref[...] = f["r"]
        lw_ref[...] = f["logw"]
        km_ref[...] = f["kmod"]
        v_ref[...] = f["v"]
        a_ref[...] = -f["kk"]
        b_ref[...] = f["kk"] * f["alpha"]

    vec = lambda n: pl.BlockSpec((1, n), lambda i: (0, 0))
    out = pl.BlockSpec((tm, rw), lambda i: (i, 0))
    return pl.pallas_call(
        body, name="rwkv_pre_fwd", grid=(t // tm,),
        in_specs=[pl.BlockSpec((tm, sh), lambda i: (i, 0)),
                  pl.BlockSpec((8, sh), lambda i: (jnp.maximum(i * (tm // 8) - 1, 0), 0)),
                  vec(sh), vec(rw), vec(rw), vec(rw), vec(rw),
                  pl.BlockSpec((LORA, rw), lambda i: (0, 0)), pl.BlockSpec((LORA, rw), lambda i: (0, 0))],
        out_specs=[out] * 6,
        out_shape=[jax.ShapeDtypeStruct((t, rw), F32)] * 6,
        compiler_params=_params("parallel"),
    )(p, p, mu, w0, a0, k_k, k_a, wd, wi)


def _rwkv_pre_bwd(p, c, mu, w0, a0, k_k, k_a, wd, wi, dr, dlw, dkm, dv, da, db, dr2, dkm2, dv2):
    t = p.shape[0]
    tm = _tile(t, 128, 8)
    rw, sh = c.rw, c.shift

    def body(p_ref, pp_ref, mu_ref, w0_ref, a0_ref, kk_ref, ka_ref, wd_ref, wi_ref,
             dr_ref, dlw_ref, dkm_ref, dv_ref, da_ref, db_ref, dr2_ref, dkm2_ref, dv2_ref,
             dps_ref, dzw_ref, dza_ref, tw_ref, al_ref, dw0_ref, da0_ref, dkk_ref, dka_ref):
        @pl.when(pl.program_id(0) == 0)
        def _():
            for ref in (dw0_ref, da0_ref, dkk_ref, dka_ref):
                ref[...] = jnp.zeros_like(ref)

        ps, _ = _shifted(p_ref[...], pp_ref[7:8, :], pl.program_id(0) == 0, mu_ref[...])
        k_k, k_a = kk_ref[...], ka_ref[...]
        f = _rwkv_features(ps, rw, w0_ref[...], a0_ref[...], k_k, k_a, wd_ref[...], wi_ref[...])
        alpha, kk, k = f["alpha"], f["kk"], f["k"]
        dkm = dkm_ref[...] + dkm2_ref[...]
        db = db_ref[...]
        dkk = db * alpha - da_ref[...]
        dalpha = db * kk + dkm * k * k_a
        dk = dkm * (1.0 + (alpha - 1.0) * k_a)
        dka_ref[...] += jnp.sum(dkm * k * (alpha - 1.0), axis=0, keepdims=True)
        dkkr = f["rn"] * jnp.where(f["n2"] > 1e-24, dkk - kk * _head_sum(dkk * kk), dkk)
        dk = dk + dkkr * k_k
        dkk_ref[...] += jnp.sum(dkkr * k, axis=0, keepdims=True)
        dza = dalpha * alpha * (1.0 - alpha)
        da0_ref[...] += jnp.sum(dza, axis=0, keepdims=True)
        dzw = dlw_ref[...] * f["logw"] * _sig(-f["zw"])
        dw0_ref[...] += jnp.sum(dzw, axis=0, keepdims=True)
        dza_b, dzw_b = dza.astype(BF16), dzw.astype(BF16)
        dal = _mm_nt(dza_b, wi_ref[...])
        dwl = _mm_nt(dzw_b, wd_ref[...]) * (1.0 - f["tw"] * f["tw"])
        dps_ref[:, 0:rw] = dr_ref[...] + dr2_ref[...]
        dps_ref[:, rw:2 * rw] = dk
        dps_ref[:, 2 * rw:3 * rw] = dv_ref[...] + dv2_ref[...]
        dps_ref[:, 3 * rw:3 * rw + LORA] = dwl
        dps_ref[:, 3 * rw + LORA:sh] = dal
        dzw_ref[...] = dzw_b
        dza_ref[...] = dza_b
        tw_ref[...] = f["tw"].astype(BF16)
        al_ref[...] = f["al"].astype(BF16)

    vec = lambda n: pl.BlockSpec((1, n), lambda i: (0, 0))
    blk = lambda n: pl.BlockSpec((tm, n), lambda i: (i, 0))
    return pl.pallas_call(
        body, name="rwkv_pre_bwd", grid=(t // tm,),
        in_specs=[blk(sh), pl.BlockSpec((8, sh), lambda i: (jnp.maximum(i * (tm // 8) - 1, 0), 0)),
                  vec(sh), vec(rw), vec(rw), vec(rw), vec(rw),
                  pl.BlockSpec((LORA, rw), lambda i: (0, 0)), pl.BlockSpec((LORA, rw), lambda i: (0, 0))]
                 + [blk(rw)] * 9,
        out_specs=[blk(sh), blk(rw), blk(rw), blk(LORA), blk(LORA), vec(rw), vec(rw), vec(rw), vec(rw)],
        out_shape=[jax.ShapeDtypeStruct((t, sh), F32), jax.ShapeDtypeStruct((t, rw), BF16),
                   jax.ShapeDtypeStruct((t, rw), BF16), jax.ShapeDtypeStruct((t, LORA), BF16),
                   jax.ShapeDtypeStruct((t, LORA), BF16)] + [jax.ShapeDtypeStruct((1, rw), F32)] * 4,
        compiler_params=_params("arbitrary"),
    )(p, p, mu, w0, a0, k_k, k_a, wd, wi, dr, dlw, dkm, dv, da, db, dr2, dkm2, dv2)


def _shift_bwd(dps, p, c, mu, dp):
    t = p.shape[0]
    tm = _tile(t, 256, 8)
    sh = c.shift
    nt = t // tm

    def body(d_ref, dn_ref, p_ref, pp_ref, mu_ref, dp_ref, dmu_ref):
        i = pl.program_id(0)

        @pl.when(i == 0)
        def _():
            dmu_ref[...] = jnp.zeros_like(dmu_ref)

        mu = mu_ref[...]
        d = d_ref[...]
        pc = p_ref[...]
        _, prev = _shifted(pc, pp_ref[7:8, :], i == 0, mu)
        dmu_ref[...] += jnp.sum(d * (prev - pc), axis=0, keepdims=True)
        nxt_row = jnp.where(i == nt - 1, 0.0, dn_ref[0:1, :])
        last = lax.broadcasted_iota(jnp.int32, d.shape, 0) == tm - 1
        nxt = jnp.where(last, nxt_row, pltpu.roll(d, tm - 1, 0))
        dp_ref[...] = (d * (1.0 - mu) + nxt * mu).astype(BF16)

    blk = pl.BlockSpec((tm, sh), lambda i: (i, 0))
    return _pallas_into(
        body, dp, 5, 0, name="shift_bwd", grid=(nt,),
        in_specs=[blk, pl.BlockSpec((8, sh), lambda i: (jnp.minimum((i + 1) * (tm // 8), t // 8 - 1), 0)),
                  blk, pl.BlockSpec((8, sh), lambda i: (jnp.maximum(i * (tm // 8) - 1, 0), 0)),
                  pl.BlockSpec((1, sh), lambda i: (0, 0))],
        out_specs=[blk, pl.BlockSpec((1, sh), lambda i: (0, 0))],
        out_shape=[jax.ShapeDtypeStruct((t, c.wp), BF16), jax.ShapeDtypeStruct((1, sh), F32)],
        compiler_params=_params("arbitrary"),
    )(dps, dps, p, p, mu)


def _tri(n, strict):
    ri = lax.broadcasted_iota(jnp.int32, (n, n), 0)
    ci = lax.broadcasted_iota(jnp.int32, (n, n), 1)
    return (ri > ci) if strict else (ri >= ci)


def _unit_lower_inverse(a):
    n = a.shape[-1]
    ri = lax.broadcasted_iota(jnp.int32, (n, n), 0)
    ci = lax.broadcasted_iota(jnp.int32, (n, n), 1)
    eye = (ri == ci).astype(F32)
    blk = lambda s: (ri // s) == (ci // s)
    ad = jnp.where(blk(16), a, 0.0)
    p = eye + ad
    for _ in range(3):
        ad = _bmm(ad, ad, P_SOLVE)
        p = p + _bmm(p, ad, P_SOLVE)
    s = 16
    while s < n:
        off = jnp.where(blk(2 * s) & ~blk(s), a, 0.0)
        p = p + _bmm(_bmm(p, off, P_SOLVE), p, P_SOLVE)
        s *= 2
    return p


P_SOLVE, P_STATE, P_OUT, P_GRAD, P_DECAY = 1, 3, 1, 1, 3


def _chunk_common(r, lw, k, a, b):
    n = r.shape[1]
    tri_incl = jnp.broadcast_to(_tri(n, False).astype(BF16), (r.shape[0], n, n))
    cum = _bmm_01(tri_incl, lw)
    e_pos, e_neg, e_exc = jnp.exp(cum), jnp.exp(-cum), jnp.exp(cum - lw)
    last = lax.broadcasted_iota(jnp.int32, (n, r.shape[2]), 0) == n - 1
    g_last = jnp.exp(jnp.sum(jnp.where(last, cum, 0.0), axis=1, keepdims=True))
    return g_last, r * e_pos, a * e_exc, b * e_neg, k * e_neg, e_pos, e_neg, e_exc


def _chunk_solve(rt, at, bt, kt, v, g0):
    strict, incl = _tri(rt.shape[1], True), _tri(rt.shape[1], False)
    a_ab = jnp.where(strict, _bmm_nt(at, bt, P_SOLVE), 0.0)
    a_ak = jnp.where(strict, _bmm_nt(at, kt, P_SOLVE), 0.0)
    a_rb = jnp.where(incl, _bmm_nt(rt, bt, P_OUT), 0.0)
    a_rk = jnp.where(incl, _bmm_nt(rt, kt, P_OUT), 0.0)
    tinv = _unit_lower_inverse(a_ab)
    u = _bmm(tinv, _bmm(at, g0, P_SOLVE) + _bmm(a_ak, v, P_SOLVE), P_SOLVE)
    return a_ab, a_ak, a_rb, a_rk, tinv, u


def _diag_col(row, n):
    ri = lax.broadcasted_iota(jnp.int32, (n, n), 0)
    ci = lax.broadcasted_iota(jnp.int32, (n, n), 1)
    return jnp.sum(jnp.where(ri == ci, row, 0.0), axis=2, keepdims=True)


def _diag_row(col, n):
    ri = lax.broadcasted_iota(jnp.int32, (n, n), 0)
    ci = lax.broadcasted_iota(jnp.int32, (n, n), 1)
    return jnp.sum(jnp.where(ri == ci, col, 0.0), axis=1, keepdims=True)


def _rwkv_scan_fwd(r, lw, k, v, a, b, hb):
    h, t, n = r.shape
    nc = t // CHUNK

    def body(r_ref, lw_ref, k_ref, v_ref, a_ref, b_ref, y_ref, st_ref, g_sc):
        @pl.when(pl.program_id(1) == 0)
        def _():
            g_sc[...] = jnp.zeros_like(g_sc)

        g0 = g_sc[...]
        st_ref[0] = g0
        vv = v_ref[...]
        g_last, rt, at, bt, kt, _, _, _ = _chunk_common(r_ref[...], lw_ref[...], k_ref[...], a_ref[...], b_ref[...])
        _, _, a_rb, a_rk, _, u = _chunk_solve(rt, at, bt, kt, vv, g0)
        y_ref[...] = _bmm(rt, g0, P_OUT) + _bmm(a_rb, u, P_OUT) + _bmm(a_rk, vv, P_OUT)
        z = g0 + _bmm_tn(bt, u, P_STATE) + _bmm_tn(kt, vv, P_STATE)
        g_sc[...] = _diag_col(g_last, n) * z

    blk = pl.BlockSpec((hb, CHUNK, n), lambda i, j: (i, j, 0))
    return pl.pallas_call(
        body, name="rwkv_scan_fwd", grid=(h // hb, nc),
        in_specs=[blk] * 6,
        out_specs=[blk, pl.BlockSpec((1, hb, n, n), lambda i, j: (j, i, 0, 0))],
        out_shape=[jax.ShapeDtypeStruct((h, t, n), F32), jax.ShapeDtypeStruct((nc, h, n, n), F32)],
        scratch_shapes=[pltpu.VMEM((hb, n, n), F32)],
        compiler_params=_params("parallel", "arbitrary"),
    )(r, lw, k, v, a, b)


def _rwkv_scan_bwd(r, lw, k, v, a, b, states, dy, hb):
    h, t, n = r.shape
    nc = t // CHUNK

    def body(r_ref, lw_ref, k_ref, v_ref, a_ref, b_ref, st_ref, dy_ref,
             dr_ref, dlw_ref, dk_ref, dv_ref, da_ref, db_ref, dg_sc):
        @pl.when(pl.program_id(1) == 0)
        def _():
            dg_sc[...] = jnp.zeros_like(dg_sc)

        g0 = st_ref[0]
        vv, dyv, dh = v_ref[...], dy_ref[...], dg_sc[...]
        lwv = lw_ref[...]
        g_last, rt, at, bt, kt, e_pos, e_neg, e_exc = _chunk_common(r_ref[...], lwv, k_ref[...], a_ref[...], b_ref[...])
        a_ab, a_ak, a_rb, a_rk, tinv, u = _chunk_solve(rt, at, bt, kt, vv, g0)
        strict, incl = _tri(CHUNK, True), _tri(CHUNK, False)
        gcol = _diag_col(g_last, n)
        z = g0 + _bmm_tn(bt, u, P_STATE) + _bmm_tn(kt, vv, P_STATE)
        dz = gcol * dh
        dc_last = _diag_row(jnp.sum(dh * gcol * z, axis=2, keepdims=True), n)
        g = P_GRAD
        du = _bmm_tn(a_rb, dyv, g) + _bmm(bt, dz, g)
        dx = _bmm_tn(tinv, du, P_SOLVE)
        dv_ref[...] = _bmm_tn(a_rk, dyv, g) + _bmm(kt, dz, g) + _bmm_tn(a_ak, dx, g)
        da_ab = jnp.where(strict, _bmm_nt(dx, u, g), 0.0)
        da_ak = jnp.where(strict, _bmm_nt(dx, vv, g), 0.0)
        da_rb = jnp.where(incl, _bmm_nt(dyv, u, g), 0.0)
        da_rk = jnp.where(incl, _bmm_nt(dyv, vv, g), 0.0)
        g = P_DECAY
        d_at = _bmm(da_ab, bt, g) + _bmm(da_ak, kt, g) + _bmm_nt(dx, g0, g)
        d_rt = _bmm(da_rb, bt, g) + _bmm(da_rk, kt, g) + _bmm_nt(dyv, g0, g)
        d_bt = _bmm_tn(da_ab, at, g) + _bmm_tn(da_rb, rt, g) + _bmm_nt(u, dz, g)
        d_kt = _bmm_tn(da_ak, at, g) + _bmm_tn(da_rk, rt, g) + _bmm_nt(vv, dz, g)
        dg_sc[...] = dz + _bmm_tn(rt, dyv, P_STATE) + _bmm_tn(at, dx, P_STATE)
        dr_ref[...] = d_rt * e_pos
        da_ref[...] = d_at * e_exc
        db_ref[...] = d_bt * e_neg
        dk_ref[...] = d_kt * e_neg
        last = lax.broadcasted_iota(jnp.int32, (CHUNK, n), 0) == CHUNK - 1
        dc = d_rt * rt - d_bt * bt - d_kt * kt + jnp.where(last, dc_last, 0.0)
        dce = d_at * at
        ri = lax.broadcasted_iota(jnp.int32, (CHUNK, CHUNK), 0)
        ci = lax.broadcasted_iota(jnp.int32, (CHUNK, CHUNK), 1)
        up_incl = jnp.broadcast_to((ri <= ci).astype(BF16), (hb, CHUNK, CHUNK))
        dlw_ref[...] = _bmm_01(up_incl, dc + dce) - dce

    rev = lambda i, j: (i, nc - 1 - j, 0)
    blk = pl.BlockSpec((hb, CHUNK, n), rev)
    return pl.pallas_call(
        body, name="rwkv_scan_bwd", grid=(h // hb, nc),
        in_specs=[blk] * 6 + [pl.BlockSpec((1, hb, n, n), lambda i, j: (nc - 1 - j, i, 0, 0)), blk],
        out_specs=[blk] * 6,
        out_shape=[jax.ShapeDtypeStruct((h, t, n), F32)] * 6,
        scratch_shapes=[pltpu.VMEM((hb, n, n), F32)],
        compiler_params=_params("parallel", "arbitrary"),
    )(r, lw, k, v, a, b, states, dy)


def _silu_grad(g):
    s = _sig(g)
    return s * (1.0 + g * (1.0 - s))


def _group_norm(ys):
    yc = ys - _head_sum(ys) * (1.0 / HEAD_DIM)
    rstd = lax.rsqrt(_head_sum(yc * yc) * (1.0 / HEAD_DIM) + GN_EPS)
    return yc * rstd, rstd


def _rwkv_post_fwd(ys, r, km, v, p, c, ln_w, ln_b, r_k):
    t = ys.shape[0]
    tm = _tile(t, 512, 8)
    goff = c.o_grw // LANES

    def body(ys_ref, r_ref, km_ref, v_ref, g_ref, lw_ref, lb_ref, rk_ref, o_ref):
        yn, _ = _group_norm(ys_ref[...])
        s = _head_sum(r_ref[...] * km_ref[...] * rk_ref[...])
        g = g_ref[...]
        o_ref[...] = ((yn * lw_ref[...] + lb_ref[...] + s * v_ref[...]) * g * _sig(g)).astype(BF16)

    blk = pl.BlockSpec((tm, LANES), lambda i, j: (i, j))
    vec = pl.BlockSpec((1, LANES), lambda i, j: (0, j))
    return pl.pallas_call(
        body, name="rwkv_post_fwd", grid=(t // tm, c.rw // LANES),
        in_specs=[blk] * 4 + [pl.BlockSpec((tm, LANES), lambda i, j: (i, goff + j)), vec, vec, vec],
        out_specs=blk, out_shape=jax.ShapeDtypeStruct((t, c.d), BF16),
        compiler_params=_params("parallel", "parallel"),
    )(ys, r, km, v, p, ln_w, ln_b, r_k)


def _rwkv_post_bwd(dyc, ys, r, km, v, p, c, ln_w, ln_b, r_k):
    t = ys.shape[0]
    tm = _tile(t, 512, 8)
    goff = c.o_grw // LANES

    def body(dy_ref, ys_ref, r_ref, km_ref, v_ref, g_ref, lw_ref, lb_ref, rk_ref,
             dys_ref, dr_ref, dkm_ref, dv_ref, dg_ref, dlw_ref, dlb_ref, drk_ref):
        @pl.when(pl.program_id(1) == 0)
        def _():
            for ref in (dlw_ref, dlb_ref, drk_ref):
                ref[...] = jnp.zeros_like(ref)

        yn, rstd = _group_norm(ys_ref[...])
        rv, kmv, vv, rk, g = r_ref[...], km_ref[...], v_ref[...], rk_ref[...], g_ref[...]
        s = _head_sum(rv * kmv * rk)
        y = yn * lw_ref[...] + lb_ref[...] + s * vv
        dyc = dy_ref[...]
        dg_ref[...] = (dyc * y * _silu_grad(g)).astype(BF16)
        dy = dyc * g * _sig(g)
        dlb_ref[...] += jnp.sum(dy, axis=0, keepdims=True)
        dlw_ref[...] += jnp.sum(dy * yn, axis=0, keepdims=True)
        dyn = dy * lw_ref[...]
        inv = 1.0 / HEAD_DIM
        dys_ref[...] = rstd * (dyn - _head_sum(dyn) * inv - yn * _head_sum(dyn * yn) * inv)
        ds = _head_sum(dy * vv)
        dv_ref[...] = dy * s
        dr_ref[...] = ds * kmv * rk
        dkm_ref[...] = ds * rv * rk
        drk_ref[...] += jnp.sum(ds * rv * kmv, axis=0, keepdims=True)

    blk = pl.BlockSpec((tm, LANES), lambda j, i: (i, j))
    vec = pl.BlockSpec((1, LANES), lambda j, i: (0, j))
    f = jax.ShapeDtypeStruct((t, c.rw), F32)
    s1 = jax.ShapeDtypeStruct((1, c.rw), F32)
    gate = pl.BlockSpec((tm, LANES), lambda j, i: (i, goff + j))
    return pl.pallas_call(
        body, name="rwkv_post_bwd", grid=(c.rw // LANES, t // tm),
        in_specs=[blk] * 5 + [gate, vec, vec, vec],
        out_specs=[blk] * 4 + [gate] + [vec] * 3,
        out_shape=[f, f, f, f, jax.ShapeDtypeStruct((t, c.wp), BF16), s1, s1, s1],
        compiler_params=_params("parallel", "arbitrary"),
    )(dyc, ys, r, km, v, p, ln_w, ln_b, r_k)


def _gate_fwd(y, p, goff, name, ycat, yoff):
    t, w = y.shape
    tm = _tile(t, 512, 8)
    gb, ob = goff // LANES, yoff // LANES

    def body(y_ref, g_ref, o_ref):
        g = g_ref[...]
        o_ref[...] = (y_ref[...] * g * _sig(g)).astype(BF16)

    blk = pl.BlockSpec((tm, LANES), lambda i, j: (i, j))
    return _pallas_into(
        body, ycat, 2, 0, name=name, grid=(t // tm, w // LANES),
        in_specs=[blk, pl.BlockSpec((tm, LANES), lambda i, j: (i, gb + j))],
        out_specs=pl.BlockSpec((tm, LANES), lambda i, j: (i, ob + j)),
        out_shape=jax.ShapeDtypeStruct(ycat.shape, BF16),
        compiler_params=_params("parallel", "parallel"),
    )(y, p)


def _gate_bwd(dyc, yoff, y, p, goff, name, dp):
    t, w = y.shape
    tm = _tile(t, 512, 8)
    gb, yb = goff // LANES, yoff // LANES

    def body(d_ref, y_ref, g_ref, dy_ref, dg_ref):
        g, d = g_ref[...], d_ref[...]
        dy_ref[...] = d * g * _sig(g)
        dg_ref[...] = (d * y_ref[...] * _silu_grad(g)).astype(BF16)

    blk = pl.BlockSpec((tm, LANES), lambda i, j: (i, j))
    gate = pl.BlockSpec((tm, LANES), lambda i, j: (i, gb + j))
    return _pallas_into(
        body, dp, 3, 1, name=name, grid=(t // tm, w // LANES),
        in_specs=[pl.BlockSpec((tm, LANES), lambda i, j: (i, yb + j)), blk, gate],
        out_specs=[blk, gate],
        out_shape=[jax.ShapeDtypeStruct((t, w), F32), jax.ShapeDtypeStruct(dp.shape, BF16)],
        compiler_params=_params("parallel", "parallel"),
    )(dyc, y, p)


NEG = -1e30


def _fox_logit_bwd(dcum, p, c, b_f):
    t = p.shape[0]
    tm = _tile(t, 512, 8)
    fb = c.o_fl // LANES
    nt = t // tm

    def body(d_ref, f_ref, b_ref, o_ref, db_ref, carry):
        @pl.when(pl.program_id(0) == 0)
        def _():
            carry[...] = jnp.zeros_like(carry)
            db_ref[...] = jnp.zeros_like(db_ref)

        d = d_ref[0] + d_ref[1]
        dlogf = _mm(_tri(tm, False).astype(F32).T, d, HI) + carry[...]
        carry[...] += jnp.sum(d, axis=0, keepdims=True)
        df = dlogf * _sig(-(f_ref[...] + b_ref[...]))
        o_ref[...] = df.astype(BF16)
        db_ref[...] += jnp.sum(df, axis=0, keepdims=True)

    return pl.pallas_call(
        body, name="fox_logit_bwd", grid=(nt,),
        in_specs=[pl.BlockSpec((2, tm, LANES), lambda i: (0, nt - 1 - i, 0)),
                  pl.BlockSpec((tm, LANES), lambda i: (nt - 1 - i, fb)),
                  pl.BlockSpec((1, LANES), lambda i: (0, 0))],
        out_specs=[pl.BlockSpec((tm, LANES), lambda i: (nt - 1 - i, 0)), pl.BlockSpec((1, LANES), lambda i: (0, 0))],
        out_shape=[jax.ShapeDtypeStruct((t, LANES), BF16), jax.ShapeDtypeStruct((1, LANES), F32)],
        scratch_shapes=[pltpu.VMEM((1, LANES), F32)],
        compiler_params=_params("arbitrary"),
    )(dcum, p, b_f)


FOX_PAIRS = 2
FOX_HEADS_STEP = 2 * FOX_PAIRS


def _lane_half(shape, upper):
    li = lax.broadcasted_iota(jnp.int32, shape, len(shape) - 1)
    return (li >= HEAD_DIM) if upper else (li < HEAD_DIM)


def _col(block, j):
    li = lax.broadcasted_iota(jnp.int32, block.shape, 1)
    return jnp.sum(jnp.where(li == j, block, 0.0), axis=1, keepdims=True)


def _from_cols(cols):
    li = lax.broadcasted_iota(jnp.int32, (cols[0].shape[0], len(cols)), 1)
    out = jnp.zeros(li.shape, F32)
    for j, cj in enumerate(cols):
        out = jnp.where(li == j, cj, out)
    return out


def _from_rows(rows):
    si = lax.broadcasted_iota(jnp.int32, (len(rows), rows[0].shape[1]), 0)
    out = jnp.zeros(si.shape, F32)
    for j, rj in enumerate(rows):
        out = jnp.where(si == j, rj, out)
    return out


def _causal(tq, tk):
    return lax.broadcasted_iota(jnp.int32, (tq, tk), 1) <= lax.broadcasted_iota(jnp.int32, (tq, tk), 0)


def _fox_prep_t(p, c, b_f):
    t = p.shape[0]
    tm = _tile(t, 512, LANES)
    fb = c.o_fl // LANES

    def body(f_ref, b_ref, o_ref, carry):
        @pl.when(pl.program_id(0) == 0)
        def _():
            carry[...] = jnp.zeros_like(carry)

        logf = -_softplus(-(f_ref[...] + b_ref[...]))
        cum = _mm(_tri(tm, False).astype(F32), logf, HI) + carry[...]
        o_ref[...] = cum.T
        carry[...] += jnp.sum(logf, axis=0, keepdims=True)

    return pl.pallas_call(
        body, name="fox_prep", grid=(t // tm,),
        in_specs=[pl.BlockSpec((tm, LANES), lambda i: (i, fb)), pl.BlockSpec((1, LANES), lambda i: (0, 0))],
        out_specs=pl.BlockSpec((LANES, tm), lambda i: (0, i)),
        out_shape=jax.ShapeDtypeStruct((LANES, t), F32),
        scratch_shapes=[pltpu.VMEM((1, LANES), F32)],
        compiler_params=_params("arbitrary"),
    )(p, b_f)


def _fox2_fwd(p, c, cum_t, tb, ycat):
    t = p.shape[0]
    tq = tk = _tile(t, tb, LANES)
    nq = t // tq
    pw, nh = FOX_PAIRS * LANES, FOX_HEADS_STEP
    qb, kb, vb, gb = (o // pw for o in (c.o_fq, c.o_fk, c.o_fv, c.o_gfox))
    scale = HEAD_DIM ** -0.5

    def body(q_ref, k_ref, v_ref, g_ref, ck_ref, o_ref, y_ref, lse_ref, m_sc, l_sc, acc_sc):
        g, qi, ki = pl.program_id(0), pl.program_id(1), pl.program_id(2)

        @pl.when(ki == 0)
        def _():
            m_sc[...] = jnp.full_like(m_sc, NEG)
            l_sc[...] = jnp.zeros_like(l_sc)
            acc_sc[...] = jnp.zeros_like(acc_sc)

        def step(diag):
            ms, ls = [m_sc[h] for h in range(nh)], [l_sc[h] for h in range(nh)]
            accs = [acc_sc[:, pi * LANES:(pi + 1) * LANES] for pi in range(FOX_PAIRS)]
            for pi in range(FOX_PAIRS):
                lanes = slice(pi * LANES, (pi + 1) * LANES)
                q2 = (q_ref[:, lanes] * scale).astype(BF16)
                k2, v2 = k_ref[:, lanes].astype(BF16), v_ref[:, lanes].astype(BF16)
                new_acc = accs[pi]
                for hh in range(2):
                    hi = 2 * pi + hh
                    mk = _lane_half((tq, LANES), hh == 1)
                    s = _mm_nt(jnp.where(mk, q2, jnp.zeros_like(q2)), k2) - ck_ref[pl.ds(g * nh + hi, 1), :]
                    if diag:
                        s = jnp.where(_causal(tq, tk), s, NEG)
                    m_new = jnp.maximum(ms[hi], jnp.max(s, axis=1, keepdims=True))
                    a = jnp.exp(ms[hi] - m_new)
                    e = jnp.exp(s - jnp.concatenate([m_new] * (tk // LANES), axis=1))
                    ls[hi] = a * ls[hi] + jnp.sum(e, axis=1, keepdims=True)
                    ms[hi] = m_new
                    new_acc = jnp.where(mk, a * accs[pi] + _mm(e.astype(BF16), v2), new_acc)
                accs[pi] = new_acc
            for h in range(nh):
                m_sc[h] = ms[h]
                l_sc[h] = ls[h]
            for pi in range(FOX_PAIRS):
                acc_sc[:, pi * LANES:(pi + 1) * LANES] = accs[pi]

        @pl.when(ki < qi)
        def _():
            step(False)

        @pl.when(ki == qi)
        def _():
            step(True)
            li = lax.broadcasted_iota(jnp.int32, (tq, LANES), 1)
            lse = jnp.zeros((tq, LANES), F32)
            for pi in range(FOX_PAIRS):
                lanes = slice(pi * LANES, (pi + 1) * LANES)
                inv = jnp.where(_lane_half((tq, LANES), False), 1.0 / l_sc[2 * pi], 1.0 / l_sc[2 * pi + 1])
                o = acc_sc[:, lanes] * inv
                gate = g_ref[:, lanes]
                o_ref[:, lanes] = o
                y_ref[:, lanes] = (o * gate * _sig(gate)).astype(BF16)
            for h in range(nh):
                lse = jnp.where(li == h, m_sc[h] + jnp.log(l_sc[h]), lse)
            lse_ref[0] = lse

    row = lambda off: pl.BlockSpec((tq, pw), lambda g, i, j: (i, off + g))
    key = lambda off: pl.BlockSpec((tk, pw), lambda g, i, j: (jnp.minimum(i, j), off + g))
    out = pl.BlockSpec((tq, pw), lambda g, i, j: (i, g))
    return _pallas_into(
        body, ycat, 5, 1, name="fox_fwd", grid=(c.rw // pw, nq, nq),
        in_specs=[row(qb), key(kb), key(vb), row(gb),
                  pl.BlockSpec((LANES, tk), lambda g, i, j: (0, jnp.minimum(i, j)))],
        out_specs=[out, row(c.rw // pw), pl.BlockSpec((1, tq, LANES), lambda g, i, j: (g, i, 0))],
        out_shape=[jax.ShapeDtypeStruct((t, c.rw), F32), jax.ShapeDtypeStruct(ycat.shape, BF16),
                   jax.ShapeDtypeStruct((c.rw // pw, t, LANES), F32)],
        scratch_shapes=[pltpu.VMEM((nh, tq, LANES), F32), pltpu.VMEM((nh, tq, LANES), F32),
                        pltpu.VMEM((tq, pw), F32)],
        compiler_params=_params("parallel", "parallel", "arbitrary"),
    )(p, p, p, p, cum_t)


def _fox2_grads(q2, k2, v2, do2, o2, lse_h, ck, mk, diag, tq, tk):
    zero = jnp.zeros_like(q2)
    s = _mm_nt(jnp.where(mk, q2, zero), k2) - ck
    if diag:
        s = jnp.where(_causal(tq, tk), s, NEG)
    wide = lambda col: jnp.concatenate([jnp.broadcast_to(col, (tq, LANES))] * (tk // LANES), axis=1)
    pm = jnp.exp(s - wide(lse_h))
    delta = jnp.sum(jnp.where(mk, do2 * o2, 0.0), axis=1, keepdims=True)
    dob = do2.astype(BF16)
    dp = _mm_nt(jnp.where(mk, dob, zero), v2)
    return pm, pm * (dp - wide(delta)), dob


def _fox2_bwd_dq(p, c, cum_t, lse, o, do, tb, dp):
    t = p.shape[0]
    tq = tk = _tile(t, tb, LANES)
    nq = t // tq
    pw, nh = FOX_PAIRS * LANES, FOX_HEADS_STEP
    qb, kb, vb = (o_ // pw for o_ in (c.o_fq, c.o_fk, c.o_fv))
    scale = HEAD_DIM ** -0.5

    def body(q_ref, k_ref, v_ref, ck_ref, lse_ref, o_ref, do_ref, dq_ref, dcq_ref, acc_sc, row_sc):
        g, qi, ki = pl.program_id(0), pl.program_id(1), pl.program_id(2)

        @pl.when(ki == 0)
        def _():
            acc_sc[...] = jnp.zeros_like(acc_sc)
            row_sc[...] = jnp.zeros_like(row_sc)

        def step(diag):
            lse_blk = lse_ref[0]
            rows = [row_sc[h] for h in range(nh)]
            accs = [acc_sc[:, pi * LANES:(pi + 1) * LANES] for pi in range(FOX_PAIRS)]
            for pi in range(FOX_PAIRS):
                lanes = slice(pi * LANES, (pi + 1) * LANES)
                q2 = (q_ref[:, lanes] * scale).astype(BF16)
                k2, v2 = k_ref[:, lanes].astype(BF16), v_ref[:, lanes].astype(BF16)
                do2, o2 = do_ref[:, lanes], o_ref[:, lanes]
                new_acc = accs[pi]
                for hh in range(2):
                    hi = 2 * pi + hh
                    mk = _lane_half((tq, LANES), hh == 1)
                    _, ds, _ = _fox2_grads(q2, k2, v2, do2, o2, _col(lse_blk, hi),
                                           ck_ref[pl.ds(g * nh + hi, 1), :], mk, diag, tq, tk)
                    rows[hi] = rows[hi] + jnp.sum(ds, axis=1, keepdims=True)
                    new_acc = jnp.where(mk, accs[pi] + _mm(ds.astype(BF16), k2), new_acc)
                accs[pi] = new_acc
            for h in range(nh):
                row_sc[h] = rows[h]
            for pi in range(FOX_PAIRS):
                acc_sc[:, pi * LANES:(pi + 1) * LANES] = accs[pi]

        @pl.when(ki < qi)
        def _():
            step(False)

        @pl.when(ki == qi)
        def _():
            step(True)
            dq_ref[...] = (acc_sc[...] * scale).astype(BF16)
            dcq_ref[0] = _from_cols([row_sc[h] for h in range(nh)])

    row = lambda off: pl.BlockSpec((tq, pw), lambda g, i, j: (i, off + g))
    key = lambda off: pl.BlockSpec((tk, pw), lambda g, i, j: (jnp.minimum(i, j), off + g))
    stat = pl.BlockSpec((1, tq, nh), lambda g, i, j: (g, i, 0))
    return _pallas_into(
        body, dp, 7, 0, name="fox_bwd_dq", grid=(c.rw // pw, nq, nq),
        in_specs=[row(qb), key(kb), key(vb), pl.BlockSpec((LANES, tk), lambda g, i, j: (0, jnp.minimum(i, j))),
                  pl.BlockSpec((1, tq, LANES), lambda g, i, j: (g, i, 0)), row(0), row(0)],
        out_specs=[row(qb), stat],
        out_shape=[jax.ShapeDtypeStruct(dp.shape, BF16), jax.ShapeDtypeStruct((c.rw // pw, t, nh), F32)],
        scratch_shapes=[pltpu.VMEM((tq, pw), F32), pltpu.VMEM((nh, tq, 1), F32)],
        compiler_params=_params("parallel", "parallel", "arbitrary"),
    )(p, p, p, cum_t, lse, o, do)


def _fox2_bwd_dkv(p, c, cum_t, lse, o, do, tb, dp):
    t = p.shape[0]
    tq = tk = _tile(t, tb, LANES)
    nq = t // tq
    pw, nh = FOX_PAIRS * LANES, FOX_HEADS_STEP
    qb, kb, vb = (o_ // pw for o_ in (c.o_fq, c.o_fk, c.o_fv))
    scale = HEAD_DIM ** -0.5

    def body(q_ref, k_ref, v_ref, ck_ref, lse_ref, o_ref, do_ref, dk_ref, dv_ref, dck_ref, dk_sc, dv_sc, dc_sc):
        g, ki, qi = pl.program_id(0), pl.program_id(1), pl.program_id(2)

        @pl.when(qi == 0)
        def _():
            dk_sc[...] = jnp.zeros_like(dk_sc)
            dv_sc[...] = jnp.zeros_like(dv_sc)
            dc_sc[...] = jnp.zeros_like(dc_sc)

        def step(diag):
            lse_blk = lse_ref[0]
            dcs = [dc_sc[h] for h in range(nh)]
            dks = [dk_sc[:, pi * LANES:(pi + 1) * LANES] for pi in range(FOX_PAIRS)]
            dvs = [dv_sc[:, pi * LANES:(pi + 1) * LANES] for pi in range(FOX_PAIRS)]
            for pi in range(FOX_PAIRS):
                lanes = slice(pi * LANES, (pi + 1) * LANES)
                q2 = (q_ref[:, lanes] * scale).astype(BF16)
                k2, v2 = k_ref[:, lanes].astype(BF16), v_ref[:, lanes].astype(BF16)
                do2, o2 = do_ref[:, lanes], o_ref[:, lanes]
                new_dk, new_dv = dks[pi], dvs[pi]
                for hh in range(2):
                    hi = 2 * pi + hh
                    mk = _lane_half((tk, LANES), hh == 1)
                    pm, ds, dob = _fox2_grads(q2, k2, v2, do2, o2, _col(lse_blk, hi),
                                              ck_ref[pl.ds(g * nh + hi, 1), :], mk, diag, tq, tk)
                    dcs[hi] = dcs[hi] - jnp.sum(ds, axis=0, keepdims=True)
                    new_dv = jnp.where(mk, dvs[pi] + _mm_tn(pm.astype(BF16), dob), new_dv)
                    new_dk = jnp.where(mk, dks[pi] + _mm_tn(ds.astype(BF16), q2), new_dk)
                dks[pi], dvs[pi] = new_dk, new_dv
            for h in range(nh):
                dc_sc[h] = dcs[h]
            for pi in range(FOX_PAIRS):
                dk_sc[:, pi * LANES:(pi + 1) * LANES] = dks[pi]
                dv_sc[:, pi * LANES:(pi + 1) * LANES] = dvs[pi]

        @pl.when(qi > ki)
        def _():
            step(False)

        @pl.when(qi == ki)
        def _():
            step(True)

        @pl.when(qi == nq - 1)
        def _():
            dk_ref[...] = dk_sc[...].astype(BF16)
            dv_ref[...] = dv_sc[...].astype(BF16)
            dck_ref[0] = _from_rows([dc_sc[h] for h in range(nh)])

    row = lambda off: pl.BlockSpec((tq, pw), lambda g, j, i: (jnp.maximum(i, j), off + g))
    key = lambda off: pl.BlockSpec((tk, pw), lambda g, j, i: (j, off + g))
    return _pallas_into(
        body, dp, 7, 0, name="fox_bwd_dkv", grid=(c.rw // pw, nq, nq),
        in_specs=[row(qb), key(kb), key(vb), pl.BlockSpec((LANES, tk), lambda g, j, i: (0, j)),
                  pl.BlockSpec((1, tq, LANES), lambda g, j, i: (g, jnp.maximum(i, j), 0)), row(0), row(0)],
        out_specs=[key(kb), key(0), pl.BlockSpec((1, nh, tk), lambda g, j, i: (g, 0, j))],
        out_shape=[jax.ShapeDtypeStruct(dp.shape, BF16), jax.ShapeDtypeStruct((t, c.rw), BF16),
                   jax.ShapeDtypeStruct((c.rw // pw, nh, t), F32)],
        scratch_shapes=[pltpu.VMEM((tk, pw), F32), pltpu.VMEM((tk, pw), F32), pltpu.VMEM((nh, 1, tk), F32)],
        compiler_params=_params("parallel", "parallel", "arbitrary"),
    )(p, p, p, cum_t, lse, o, do)


def _mem_probs(q, mk, scale):
    s = _mm_nt(q.astype(BF16), mk.astype(BF16)) * scale
    e = jnp.exp(s - jnp.max(s, axis=1, keepdims=True))
    return e / jnp.sum(e, axis=1, keepdims=True)


def _mem_attn_fwd(p, c, mkv):
    t = p.shape[0]
    tm = _tile(t, 512, 8)
    dh = c.mhd
    qb = c.o_mq // dh
    scale = dh ** -0.5

    def body(q_ref, mk_ref, mv_ref, o_ref):
        pm = _mem_probs(q_ref[...], mk_ref[...], scale)
        o_ref[...] = _mm(pm.astype(BF16), mv_ref[...].astype(BF16))

    m = mkv.shape[0]
    return pl.pallas_call(
        body, name="mem_attn_fwd", grid=(t // tm, MEM_HEADS),
        in_specs=[pl.BlockSpec((tm, dh), lambda i, j: (i, qb + j)),
                  pl.BlockSpec((m, dh), lambda i, j: (0, j)),
                  pl.BlockSpec((m, dh), lambda i, j: (0, MEM_HEADS + j))],
        out_specs=pl.BlockSpec((tm, dh), lambda i, j: (i, j)),
        out_shape=jax.ShapeDtypeStruct((t, c.mw), F32),
        compiler_params=_params("parallel", "parallel"),
    )(p, mkv, mkv)


def _mem_attn_bwd(p, c, mkv, do):
    t = p.shape[0]
    tm = _tile(t, 512, 8)
    dh = c.mhd
    qb = c.o_mq // dh
    scale = dh ** -0.5
    m = mkv.shape[0]

    def body(q_ref, mk_ref, mv_ref, do_ref, dq_ref, dmk_ref, dmv_ref):
        @pl.when(pl.program_id(1) == 0)
        def _():
            dmk_ref[...] = jnp.zeros_like(dmk_ref)
            dmv_ref[...] = jnp.zeros_like(dmv_ref)

        qv = q_ref[...].astype(BF16)
        pm = _mem_probs(qv, mk_ref[...], scale)
        dob = do_ref[...].astype(BF16)
        dmv_ref[...] += _mm_tn(pm.astype(BF16), dob)
        dp = _mm_nt(dob, mv_ref[...].astype(BF16))
        ds = (pm * (dp - jnp.sum(pm * dp, axis=1, keepdims=True)) * scale).astype(BF16)
        dq_ref[...] = _mm(ds, mk_ref[...].astype(BF16)).astype(BF16)
        dmk_ref[...] += _mm_tn(ds, qv)

    kvb = lambda off: pl.BlockSpec((m, dh), lambda j, i: (0, off + j))
    return pl.pallas_call(
        body, name="mem_attn_bwd", grid=(MEM_HEADS, t // tm),
        in_specs=[pl.BlockSpec((tm, dh), lambda j, i: (i, qb + j)), kvb(0), kvb(MEM_HEADS),
                  pl.BlockSpec((tm, dh), lambda j, i: (i, j))],
        out_specs=[pl.BlockSpec((tm, dh), lambda j, i: (i, j)), kvb(0), kvb(0)],
        out_shape=[jax.ShapeDtypeStruct((t, c.mw), BF16), jax.ShapeDtypeStruct((m, c.mw), F32),
                   jax.ShapeDtypeStruct((m, c.mw), F32)],
        compiler_params=_params("parallel", "arbitrary"),
    )(p, mkv, mkv, do)


def _adamw(w, g, m, v, name):
    rows, cols = w.shape
    bc1 = 1.0 - ADAM_B1 ** ADAM_STEP
    bc2 = 1.0 - ADAM_B2 ** ADAM_STEP
    if rows % 8 and rows > 8:
        blk = pl.BlockSpec((rows, LANES), lambda i: (0, i))
        g_blk = pl.BlockSpec((g.shape[0], LANES), lambda i: (0, i))
        grid = (cols // LANES,)
    else:
        tm = _tile(rows, max(8, (1 << 18) // cols // 8 * 8), 8)
        blk = pl.BlockSpec((tm, cols), lambda i: (i, 0))
        g_blk = pl.BlockSpec((tm, g.shape[1]), lambda i: (i, 0))
        grid = (rows // tm,)
    brows, bcols = blk.block_shape

    def body(w_ref, g_ref, m_ref, v_ref, go_ref, d_ref, mo_ref, vo_ref):
        gv = g_ref[0:brows, 0:bcols]
        mn = ADAM_B1 * m_ref[...] + (1.0 - ADAM_B1) * gv
        vn = ADAM_B2 * v_ref[...] + (1.0 - ADAM_B2) * (gv * gv)
        go_ref[...] = gv
        mo_ref[...] = mn
        vo_ref[...] = vn
        d_ref[...] = -ADAM_LR * ((mn / bc1) / (jnp.sqrt(vn / bc2) + ADAM_EPS) + ADAM_WD * w_ref[...])

    shp = jax.ShapeDtypeStruct((rows, cols), F32)
    return pl.pallas_call(
        body, name=name, grid=grid,
        in_specs=[blk, g_blk, blk, blk],
        out_specs=[blk] * 4, out_shape=[shp] * 4,
        compiler_params=_params("parallel"),
    )(w, g, m, v)


SCAN_HEADS = 12
FOX_BLOCK = 512


def _local_step(c, x, mem, tgt, w, riders=None):
    t = x.shape[0]
    rw = c.rw
    riders = riders or {}
    carried = {}
    hd = lambda z: z.reshape(t, c.h, HEAD_DIM).transpose(1, 0, 2)
    uh = lambda z: z.transpose(1, 0, 2).reshape(t, rw)
    vecs = (w["mu"], w["w0"], w["a0"], w["k_k"], w["k_a"], w["wd"], w["wi"])

    h, rinv = _rms_fwd(x, w["g_pre"], "rms_pre")
    if "in_proj" in riders:
        groups, finish = riders["in_proj"]
        p, late = _matmul(h, w["wp"], name="in_proj", tk=4096, attach=groups)
        w = dict(w, **finish(late))
    else:
        p = _matmul(h, w["wp"], name="in_proj", tk=4096)
    r, lw, km, v, a, b = _rwkv_pre_fwd(p, c, *vecs)
    scan_in = tuple(hd(z) for z in (r, lw, km, v, a, b))
    hb = max(n for n in range(1, SCAN_HEADS + 1) if c.h % n == 0)
    ysh, states = _rwkv_scan_fwd(*scan_in, hb)
    ys = uh(ysh)
    ycat = _rwkv_post_fwd(ys, r, km, v, p, c, w["ln_w"], w["ln_b"], w["r_k"])

    cum_t = _fox_prep_t(p, c, w["b_f"])
    yfox, ycat, lse = _fox2_fwd(p, c, cum_t, FOX_BLOCK, ycat)

    memn, rinv_m = _rms_fwd(mem, w["g_mem"], "rms_mem")
    mkv = _matmul(memn, w["w_mem_kv"], name="mem_kv")
    ymem = _mem_attn_fwd(p, c, mkv)
    ycat = _gate_fwd(ymem, p, c.o_gmq, "gate_mem", ycat, 2 * rw)
    yo =_matmul(ycat, w["w_out"], name="out_proj", tn=512, tk=4096)
    loss, dout, dyo, dg_post = _post_loss(yo, x, tgt, w["g_post"], "post_loss")

    dyc = _matmul(dyo, w["w_out"], tb=True, name="d_ycat", tn=512, tk=4096)
    dw_out = _matmul(ycat, dyo, ta=True, name="d_w_out", tn=512, tk=4096, out_dtype=BF16)
    dys, dr2, dkm2, dv2, dp, dln_w, dln_b, dr_k = _rwkv_post_bwd(
        dyc, ys, r, km, v, p, c, w["ln_w"], w["ln_b"], w["r_k"])
    dyf, dp = _gate_bwd(dyc, rw, yfox, p, c.o_gfox, "gate_fox_bwd", dp)
    dym, dp = _gate_bwd(dyc, 2 * rw, ymem, p, c.o_gmq, "gate_mem_bwd", dp)

    scan_g = _rwkv_scan_bwd(*scan_in, states, hd(dys), hb)
    dps, dzw, dza, twb, alb, dw0, da0, dk_k, dk_a = _rwkv_pre_bwd(
        p, c, *vecs, *(uh(z) for z in scan_g), dr2, dkm2, dv2)
    dwd = _matmul(twb, dzw, ta=True, name="d_w_decay", out_dtype=BF16)
    dwi = _matmul(alb, dza, ta=True, name="d_w_iclr", out_dtype=BF16)
    dp, dmu = _shift_bwd(dps, p, c, w["mu"], dp)

    dp, dcq = _fox2_bwd_dq(p, c, cum_t, lse, yfox, dyf, FOX_BLOCK, dp)
    dp, dfv, dck = _fox2_bwd_dkv(p, c, cum_t, lse, yfox, dyf, FOX_BLOCK, dp)
    dcum = jnp.pad(jnp.stack([dcq.transpose(1, 0, 2).reshape(t, c.h), dck.reshape(c.h, t).T]),
                   ((0, 0), (0, 0), (0, LANES - c.h)))
    dfl, db_f = _fox_logit_bwd(dcum, p, c, w["b_f"])

    dmq, dmk, dmv = _mem_attn_bwd(p, c, mkv, dym)
    dmkv = jnp.concatenate([dmk, dmv], axis=1)
    dw_mkv = _matmul(memn, dmkv, ta=True, name="d_w_mem_kv", out_dtype=BF16)
    dmemn = _matmul(dmkv, w["w_mem_kv"], tb=True, name="d_memn")
    _, dg_mem = _rms_bwd(dmemn, mem, rinv_m, w["g_mem"], jnp.zeros_like(mem), "rms_mem_bwd")

    for off, piece in ((c.o_fv, dfv), (c.o_mq, dmq), (c.o_fl, dfl)):
        dp = lax.dynamic_update_slice(dp, piece, (0, off))
    rest = dict(wd=dwd, wi=dwi, w_mem_kv=dw_mkv, w_out=dw_out)
    if "d_w_in" in riders:
        dwp, carried["rest"] = _matmul(dp, h, ta=True, name="d_w_in", tk=4096, out_dtype=BF16,
                                       attach=riders["d_w_in"](rest))
    else:
        dwp = _matmul(dp, h, ta=True, name="d_w_in", tk=4096, out_dtype=BF16)
    if "d_h" in riders:
        dh, carried["wp"] = _matmul(dp, w["wp"], tb=True, name="d_h", tk=2944, attach=riders["d_h"](dwp))
    else:
        dh = _matmul(dp, w["wp"], tb=True, name="d_h", tk=2944)
    grad_x, dg_pre = _rms_bwd(dh, x, rinv, w["g_pre"], dout, "rms_pre_bwd")

    small = dict(g_pre=dg_pre, mu=dmu, w0=dw0, a0=da0, k_k=dk_k, k_a=dk_a, r_k=dr_k, ln_w=dln_w, ln_b=dln_b,
                 b_f=db_f, g_mem=dg_mem, g_post=dg_post)
    return loss, grad_x, dict(wp=dwp, **rest), small, carried


CHIPS = ((1, 0, 0), (0, 1, 0), (1, 1, 0))
SIBLING = ((0, 0, 1),)
ALL_PEERS = tuple((i, j, k) for i in (0, 1) for j in (0, 1) for k in (0, 1))[1:]


def _chip_of(pos):
    return 2 * pos[0] + pos[1]


DMA_CHUNK = 4 << 20


def _pieces(shape, itemsize):
    lead, (rows, cols) = shape[:-2], shape[-2:]
    k = 1
    if rows % 16 == 0:
        k = max(1, min(rows // 16, -(-rows * cols * itemsize // DMA_CHUNK)))
        while rows % k or (rows // k) % 16:
            k -= 1
    band = rows // k
    idxs = [()]
    for n in lead:
        idxs = [i + (j,) for i in idxs for j in range(n)]
    return [i + (pl.ds(j * band, band),) for i in idxs for j in range(k)]


def _peer_of(me, mask):
    return tuple(1 - v if f else v for v, f in zip(me, mask))


def _exchange(name, groups):
    n = len(groups)
    plan = _plan(groups)

    def body(*refs):
        copies = _copies(groups, plan, refs[:n], refs[n:2 * n], refs[2 * n], refs[2 * n + 1])
        for cp in copies:
            cp.start()
        for cp in copies:
            cp.wait()

    any_spec = pl.BlockSpec(memory_space=pl.ANY)
    return pl.pallas_call(
        body, name=name,
        in_specs=[any_spec] * n, out_specs=[any_spec] * n,
        out_shape=_exchange_shapes(groups),
        input_output_aliases={gi: gi for gi, g in enumerate(groups) if g.get("inplace")},
        scratch_shapes=[pltpu.SemaphoreType.DMA((len(plan),)), pltpu.SemaphoreType.DMA((len(plan),))],
    )(*[g["src"] for g in groups])


def _plan(groups):
    return [(gi, ti, idx) for gi, g in enumerate(groups) for ti in range(len(g["transfers"]))
            for idx in _pieces(tuple(g["piece"]), g["src"].dtype.itemsize)]


def _exchange_shapes(groups):
    lead = lambda s: tuple(s) if isinstance(s, tuple) else (s,)
    return [jax.ShapeDtypeStruct(lead(g["slots"]) + tuple(g["piece"]), g["src"].dtype) for g in groups]


def _copies(groups, plan, srcs, outs, send_sems, recv_sems):
    me = (lax.axis_index("x"), lax.axis_index("y"), lax.axis_index("c"))
    copies = []
    for k, (gi, ti, idx) in enumerate(plan):
        mask, view, slot = groups[gi]["transfers"][ti]
        peer = _peer_of(me, mask)
        copies.append(pltpu.make_async_remote_copy(
            src_ref=view(srcs[gi], me, peer).at[idx], dst_ref=outs[gi].at[slot(me, peer)].at[idx],
            send_sem=send_sems.at[k], recv_sem=recv_sems.at[k],
            device_id=peer, device_id_type=MESH))
    return copies


def _my_chip():
    return 2 * lax.axis_index("x") + lax.axis_index("y")


def _put(buf, block, slot):
    return lax.dynamic_update_slice(buf, block[None], (slot,) + (0,) * block.ndim)


def _sum_slots(recv, own, k, out_dtype, name):
    s, rows, cols = recv.shape
    budget = max(16, ((4 << 20) // ((s + 1) * cols * 4)) // 16 * 16)
    tr = _tile(rows, budget, 16)
    own_many = own.shape[0] > 1

    def body(k_ref, *refs):
        out_ref = refs[s + 1]
        mine = refs[s][0].astype(F32)
        acc = None
        for i in range(s):
            term = jnp.where(k_ref[0] == i, mine, refs[i][0].astype(F32))
            acc = term if acc is None else acc + term
        out_ref[...] = acc.astype(out_ref.dtype)

    def slot_spec(i):
        return pl.BlockSpec((1, tr, cols), lambda j, kr: (jnp.where(kr[0] == i, (i + 1) % s, i), j, 0))

    grid_spec = pltpu.PrefetchScalarGridSpec(
        num_scalar_prefetch=1, grid=(rows // tr,),
        in_specs=[slot_spec(i) for i in range(s)]
                 + [pl.BlockSpec((1, tr, cols), lambda j, kr: (kr[0] if own_many else 0, j, 0))],
        out_specs=pl.BlockSpec((tr, cols), lambda j, kr: (j, 0)))
    return pl.pallas_call(
        body, name=name, grid_spec=grid_spec,
        out_shape=jax.ShapeDtypeStruct((rows, cols), out_dtype),
        compiler_params=_params("parallel"),
    )(k, *([recv] * s), own)


def _all_gather(shards):
    return _gather_finish(shards, _exchange("gather_chips", _gather_groups(shards)), "gather_pair")


def _gather_groups(shards):
    halves = [s.reshape(2, s.shape[0] // 2, s.shape[1]) for s in shards]
    return [dict(src=q, slots=(4, 2), piece=q.shape[1:],
                 transfers=[(m, lambda ref, me, peer: ref.at[me[2]], lambda me, peer: (_chip_of(me), me[2]))
                            for m in CHIPS])
            for q in halves]


def _gather_finish(shards, first, name):
    spot = lambda m: (lambda me: (_chip_of(_peer_of(me, m)), me[2]))
    both = _exchange(name, [
        dict(src=q, slots=(4, 2), piece=q.shape[2:], inplace=True,
             transfers=[(SIBLING[0], (lambda f: lambda ref, me, peer: ref.at[f(me)])(spot(m)),
                         (lambda f: lambda me, peer: f(me))(spot(m))) for m in CHIPS])
        for q in first])
    return [_put(q.reshape((4,) + s.shape), s, _my_chip()) for s, q in zip(shards, both)]


def _reduce_pair(partials, tag):
    core1 = lax.axis_index("c").reshape(1).astype(jnp.int32)
    halves = [q.reshape(4, 2, q.shape[1] // 2, q.shape[2]).transpose(1, 0, 2, 3) for q in partials]
    pair = _exchange("reduce_pair_" + tag, [
        dict(src=q, slots=2, piece=q.shape[1:],
             transfers=[(SIBLING[0], lambda ref, me, peer: ref.at[peer[2]], lambda me, peer: me[2])])
        for q in halves])
    flat = lambda e: e.reshape(2, -1, e.shape[-1])
    return [_sum_slots(flat(e), flat(q), core1, BF16, "reduce_pair_sum_" + tag).reshape(q.shape[1:])
            for e, q in zip(pair, halves)]


def _reduce_chips_groups(chip_sums):
    return [dict(src=q, slots=4, piece=q.shape[1:],
                 transfers=[(m, lambda ref, me, peer: ref.at[_chip_of(peer)], lambda me, peer: _chip_of(me))
                            for m in CHIPS])
            for q in chip_sums]


def _reduce_finish(crossed, chip_sums, tag):
    core = lax.axis_index("c")
    chip1 = _my_chip().reshape(1).astype(jnp.int32)
    sums = [_sum_slots(e, q, chip1, F32, "reduce_chips_sum_" + tag) for e, q in zip(crossed, chip_sums)]
    swapped = _exchange("reduce_swap_" + tag, [
        dict(src=q, slots=2, piece=q.shape, transfers=[(SIBLING[0], lambda ref, me, peer: ref, lambda me, peer: me[2])])
        for q in sums])
    return [_put(e, q, core).reshape(-1, e.shape[-1]) for e, q in zip(swapped, sums)]


def _reduce_scatter(partials):
    chip_sums = _reduce_pair(partials, "all")
    return _reduce_finish(_exchange("reduce_chips", _reduce_chips_groups(chip_sums)), chip_sums, "all")


def _all_reduce_small(vec):
    dev = 4 * lax.axis_index("x") + 2 * lax.axis_index("y") + lax.axis_index("c")
    got = _exchange("reduce_small", [
        dict(src=vec, slots=8, piece=vec.shape,
             transfers=[(m, lambda ref, me, peer: ref, lambda me, peer: 4 * me[0] + 2 * me[1] + me[2])
                        for m in ALL_PEERS])])[0]
    return _sum_slots(got, vec[None], dev.reshape(1).astype(jnp.int32), F32, "reduce_small_sum")


SMALL = ("g_pre", "mu", "w0", "a0", "k_k", "k_a", "r_k", "ln_w", "ln_b", "b_f", "g_mem", "g_post")


def _pad_cols(a, n):
    return jnp.pad(a, ((0, 0),) * (a.ndim - 1) + ((0, n - a.shape[-1]),))


def kernel(x, mem, g_pre, w_in, mu_rwkv, w0, w_decay_up, a0, w_iclr_up, k_k, k_a, r_k, ln_x_w, ln_x_b, b_f, g_mem, w_mem_kv, w_out, g_post, loss_target, m_g_pre, m_w_in, m_mu_rwkv, m_w0, m_w_decay_up, m_a0, m_w_iclr_up, m_k_k, m_k_a, m_r_k, m_ln_x_w, m_ln_x_b, m_b_f, m_g_mem, m_w_mem_kv, m_w_out, m_g_post, v_g_pre, v_w_in, v_mu_rwkv, v_w0, v_w_decay_up, v_a0, v_w_iclr_up, v_k_k, v_k_a, v_r_k, v_ln_x_w, v_ln_x_b, v_b_f, v_g_mem, v_w_mem_kv, v_w_out, v_g_post):
    d = x.shape[-1]
    c = Cfg(d)
    ws = w_in.shape[-1]
    wpad = -(-ws // LANES) * LANES
    nh = c.h

    g_in, g_wd, g_wi = _all_gather([
        _pad_cols(w_in[0].astype(BF16), wpad), w_decay_up[0].astype(BF16), w_iclr_up[0].astype(BF16)])
    fl = c.ref_fl
    runs = [(0, fl, 0), (fl, fl + nh, c.o_fl), (fl + nh, c.in_width, fl)]
    pieces = []
    for lo, hi, _ in sorted(runs, key=lambda r: r[2]):
        for s in range(4):
            a, b = max(lo, s * ws), min(hi, (s + 1) * ws)
            if a < b:
                pieces.append(g_in[s, :, a - s * ws:b - s * ws])
    wp = jnp.concatenate(pieces + [jnp.zeros((d, LANES - nh), BF16)], axis=1)
    unshard = lambda g: g.transpose(1, 0, 2).reshape(g.shape[1], -1)
    weights = dict(wp=wp, wd=unshard(g_wd), wi=unshard(g_wi),
                   g_pre=g_pre, mu=mu_rwkv, w0=w0, a0=a0, k_k=k_k, k_a=k_a, r_k=r_k.reshape(1, -1),
                   ln_w=ln_x_w, ln_b=ln_x_b, b_f=_pad_cols(b_f, LANES), g_mem=g_mem, g_post=g_post)
    late_shards = [w_out[0].astype(BF16), w_mem_kv[0].astype(BF16)]

    def late_weights(first):
        g_out, g_mkv = _gather_finish(late_shards, first, "gather_pair_late")
        return dict(w_out=g_out.reshape(-1, d), w_mem_kv=g_mkv.reshape(d, -1))

    by_chip = lambda g: jnp.stack(jnp.split(g, 4, axis=1))
    pair_sums = {}

    def ride_rest(g):
        pair_sums["rest"] = _reduce_pair([g["w_out"].reshape(4, -1, d), g["w_mem_kv"].reshape(4, d // 4, -1),
                                          by_chip(g["wd"]), by_chip(g["wi"])], "rest")
        return _reduce_chips_groups(pair_sums["rest"])

    def ride_wp(dwpt):
        shards = []
        for s in range(4):
            rows = []
            for lo, hi, at in runs:
                a, b = max(lo, s * ws), min(hi, (s + 1) * ws)
                if a < b:
                    rows.append(dwpt[at + a - lo:at + b - lo, :])
            part = rows[0] if len(rows) == 1 else jnp.concatenate(rows, axis=0)
            shards.append(jnp.pad(part, ((0, wpad - ws), (0, 0))))
        pair_sums["wp"] = _reduce_pair([jnp.stack(shards)], "w_in")
        return _reduce_chips_groups(pair_sums["wp"])

    loss, grad_x, _, small, carried = _local_step(
        c, x[0], mem[0], loss_target[0], weights,
        riders={"in_proj": (_gather_groups(late_shards), late_weights), "d_w_in": ride_rest, "d_h": ride_wp})
    red = (_reduce_finish(carried["wp"], pair_sums["wp"], "w_in")
           + _reduce_finish(carried["rest"], pair_sums["rest"], "rest"))
    big_w = (w_in[0].T, w_out[0], w_mem_kv[0], w_decay_up[0], w_iclr_up[0])
    big_m = (m_w_in[0].T, m_w_out[0], m_w_mem_kv[0], m_w_decay_up[0], m_w_iclr_up[0])
    big_v = (v_w_in[0].T, v_w_out[0], v_w_mem_kv[0], v_w_decay_up[0], v_w_iclr_up[0])
    big_names = ("w_in", "w_out", "w_mem_kv", "w_decay_up", "w_iclr_up")
    upd = {n: _adamw(w_, g_, m_, v_, "adamw_" + n) for n, w_, g_, m_, v_ in zip(big_names, big_w, red, big_m, big_v)}
    upd["w_in"] = [o.T for o in upd["w_in"]]

    small_w = dict(g_pre=g_pre, mu=mu_rwkv, w0=w0, a0=a0, k_k=k_k, k_a=k_a, r_k=r_k.reshape(1, -1), ln_w=ln_x_w,
                   ln_b=ln_x_b, b_f=b_f, g_mem=g_mem, g_post=g_post)
    small_m = dict(g_pre=m_g_pre, mu=m_mu_rwkv, w0=m_w0, a0=m_a0, k_k=m_k_k, k_a=m_k_a, r_k=m_r_k.reshape(1, -1),
                   ln_w=m_ln_x_w, ln_b=m_ln_x_b, b_f=m_b_f, g_mem=m_g_mem, g_post=m_g_post)
    small_v = dict(g_pre=v_g_pre, mu=v_mu_rwkv, w0=v_w0, a0=v_a0, k_k=v_k_k, k_a=v_k_a, r_k=v_r_k.reshape(1, -1),
                   ln_w=v_ln_x_w, ln_b=v_ln_x_b, b_f=v_b_f, g_mem=v_g_mem, g_post=v_g_post)
    widths = [-(-small_w[n].shape[1] // LANES) * LANES for n in SMALL]
    pack = lambda t: jnp.concatenate([_pad_cols(t[n], wd_) for n, wd_ in zip(SMALL, widths)]
                                     + [jnp.zeros((1, LANES), F32)], axis=1)
    g_packed = jnp.concatenate([_pad_cols(small[n], wd_) for n, wd_ in zip(SMALL, widths)]
                               + [_pad_cols(loss, LANES)], axis=1)
    g_sum = _all_reduce_small(g_packed)
    s_upd = _adamw(pack(small_w), g_sum, pack(small_m), pack(small_v), "adamw_small")
    offs = [sum(widths[:i]) for i in range(len(SMALL))]

    def take(kind, n):
        i = SMALL.index(n)
        piece = s_upd[kind][:, offs[i]:offs[i] + small_w[n].shape[1]]
        return piece.reshape(r_k.shape) if n == "r_k" else piece

    total_loss = g_sum[0, sum(widths)]
    order = ("g_pre", "w_in", "mu", "w0", "w_decay_up", "a0", "w_iclr_up", "k_k", "k_a", "r_k", "ln_w", "ln_b", "b_f",
             "g_mem", "w_mem_kv", "w_out", "g_post")
    outs = [total_loss, grad_x[None]]
    for kind in range(4):
        for n in order:
            outs.append(upd[n][kind][None] if n in upd else take(kind, n))
    return tuple(outs)
```

```python
import jax
import jax.numpy as jnp
from jax import lax
from jax.experimental import pallas as pl
from jax.experimental.pallas import tpu as pltpu

F32 = jnp.float32
BF16 = jnp.bfloat16
HI = lax.Precision.HIGHEST
MESH = pl.DeviceIdType.MESH

HEAD_DIM = 64
MEM_HEADS = 4
LORA = 128
CHUNK = 64
RMS_EPS = 1e-6
GN_EPS = 64e-5
LANES = 128
VMEM_LIMIT = 56 * 1024 * 1024

ADAM_LR, ADAM_B1, ADAM_B2, ADAM_EPS, ADAM_WD, ADAM_STEP = 0.001, 0.9, 0.999, 1e-08, 0.01, 10


class Cfg:
    def __init__(self, d):
        self.d = d
        self.rw = 3 * d // 8
        self.mw = d // 4
        self.h = self.rw // HEAD_DIM
        self.mhd = self.mw // MEM_HEADS
        self.shift = 3 * self.rw + 2 * LORA
        self.in_width = self.shift + 5 * self.rw + self.h + 2 * self.mw
        o = self.shift
        self.o_grw = o; o += self.rw
        self.o_fq = o; o += self.rw
        self.o_fk = o; o += self.rw
        self.o_fv = o; o += self.rw
        self.o_gfox = o; o += self.rw
        self.o_mq = o; o += self.mw
        self.o_gmq = o; o += self.mw
        self.o_fl = o; o += LANES
        self.wp = o
        self.ref_fl = self.shift + 4 * self.rw


def _tile(n, pref, align=LANES):
    if n <= pref:
        return n
    t = (pref // align) * align
    while t >= align:
        if n % t == 0:
            return t
        t -= align
    return n


def _params(*sem):
    return pltpu.CompilerParams(dimension_semantics=sem, vmem_limit_bytes=VMEM_LIMIT)


def _pallas_into(body, into, n_in, out_index, in_specs, **kw):
    if into is None:
        return pl.pallas_call(body, in_specs=in_specs, **kw)

    def body_with_alias(*refs):
        return body(*refs[:n_in], *refs[n_in + 1:])

    call = pl.pallas_call(body_with_alias, in_specs=list(in_specs) + [pl.BlockSpec(memory_space=pl.ANY)],
                          input_output_aliases={n_in: out_index}, **kw)
    return lambda *args: call(*args, into)


def _sig(x):
    return 1.0 / (1.0 + jnp.exp(-x))


def _softplus(x):
    return jnp.maximum(x, 0.0) + jnp.log(1.0 + jnp.exp(-jnp.abs(x)))


def _dot(a, b, dims, prec=None):
    return lax.dot_general(a, b, (dims, ((), ())), precision=prec, preferred_element_type=F32)


def _mm(a, b, prec=None):
    return _dot(a, b, ((1,), (0,)), prec)


def _mm_nt(a, b, prec=None):
    return _dot(a, b, ((1,), (1,)), prec)


def _mm_tn(a, b, prec=None):
    return _dot(a, b, ((0,), (0,)), prec)


def _split(a):
    hi = a.astype(BF16)
    return hi, (a - hi.astype(F32)).astype(BF16)


def _dot3(a, b, dims, passes=3):
    d = lambda x, y: lax.dot_general(x, y, dims, preferred_element_type=F32)
    if passes == 1:
        return d(a.astype(BF16), b.astype(BF16))
    if passes == 2:
        ah, (bh, bl) = a.astype(BF16), _split(b)
        return d(ah, bh) + d(ah, bl)
    (ah, al), (bh, bl) = _split(a), _split(b)
    return d(ah, bh) + (d(ah, bl) + d(al, bh))


def _bmm(a, b, passes=3):
    return _dot3(a, b, (((2,), (1,)), ((0,), (0,))), passes)


def _bmm_nt(a, b, passes=3):
    return _dot3(a, b, (((2,), (2,)), ((0,), (0,))), passes)


def _bmm_tn(a, b, passes=3):
    return _dot3(a, b, (((1,), (1,)), ((0,), (0,))), passes)


def _bmm_01(m01, x):
    x1 = x.astype(BF16)
    r1 = x - x1.astype(F32)
    x2 = r1.astype(BF16)
    x3 = (r1 - x2.astype(F32)).astype(BF16)
    d = lambda y: lax.dot_general(m01, y, (((2,), (1,)), ((0,), (0,))), preferred_element_type=F32)
    return d(x1) + (d(x2) + d(x3))


def _matmul(a, b, *, ta=False, tb=False, out_dtype=F32, name, tm=1024, tn=1024, tk=1024, attach=None):
    m, k = (a.shape[1], a.shape[0]) if ta else a.shape
    n = b.shape[0] if tb else b.shape[1]
    tm, tn, tk = _tile(m, tm), _tile(n, tn), _tile(k, tk)
    nk = k // tk
    grid = (m // tm, n // tn, nk)
    dims = ((0 if ta else 1,), (1 if tb else 0,))
    groups = attach or []
    ng = len(groups)
    plan = _plan(groups)

    def body(a_ref, b_ref, *rest):
        srcs, o_ref, outs, scratch = rest[:ng], rest[ng], rest[ng + 1:2 * ng + 1], rest[2 * ng + 1:]
        acc = scratch[0] if nk > 1 else None
        if ng:
            copies = _copies(groups, plan, srcs, outs, scratch[-2], scratch[-1])
            ids = [pl.program_id(ax) for ax in range(3)]

            @pl.when((ids[0] == 0) & (ids[1] == 0) & (ids[2] == 0))
            def _():
                for cp in copies:
                    cp.start()

        part = _dot(a_ref[...].astype(BF16), b_ref[...].astype(BF16), dims)
        if nk == 1:
            o_ref[...] = part.astype(o_ref.dtype)
        else:
            kk = pl.program_id(2)

            @pl.when(kk == 0)
            def _():
                acc[...] = part

            @pl.when(kk > 0)
            def _():
                acc[...] += part

            @pl.when(kk == nk - 1)
            def _():
                o_ref[...] = acc[...].astype(o_ref.dtype)

        if ng:
            @pl.when((ids[0] == grid[0] - 1) & (ids[1] == grid[1] - 1) & (ids[2] == grid[2] - 1))
            def _():
                for cp in copies:
                    cp.wait()

    a_spec = pl.BlockSpec((tk, tm), lambda i, j, kk: (kk, i)) if ta else pl.BlockSpec((tm, tk), lambda i, j, kk: (i, kk))
    b_spec = pl.BlockSpec((tn, tk), lambda i, j, kk: (j, kk)) if tb else pl.BlockSpec((tk, tn), lambda i, j, kk: (kk, j))
    any_spec = pl.BlockSpec(memory_space=pl.ANY)
    sems = [pltpu.SemaphoreType.DMA((len(plan),)), pltpu.SemaphoreType.DMA((len(plan),))] if ng else []
    res = pl.pallas_call(
        body, name=name, grid=grid,
        in_specs=[a_spec, b_spec] + [any_spec] * ng,
        out_specs=[pl.BlockSpec((tm, tn), lambda i, j, kk: (i, j))] + [any_spec] * ng,
        out_shape=[jax.ShapeDtypeStruct((m, n), out_dtype)] + _exchange_shapes(groups),
        scratch_shapes=([pltpu.VMEM((tm, tn), F32)] if nk > 1 else []) + sems,
        compiler_params=_params(*(("arbitrary",) * 3 if ng else ("parallel", "parallel", "arbitrary"))),
    )(a, b, *[g["src"] for g in groups])
    return (res[0], list(res[1:])) if ng else res[0]


def _rms_fwd(x, g, name):
    t, d = x.shape
    tm = _tile(t, 256, 8)

    def body(x_ref, g_ref, h_ref, r_ref):
        xv = x_ref[...]
        r = lax.rsqrt(jnp.mean(xv * xv, axis=-1, keepdims=True) + RMS_EPS)
        h_ref[...] = (xv * r * g_ref[...]).astype(BF16)
        r_ref[...] = r

    return pl.pallas_call(
        body, name=name, grid=(t // tm,),
        in_specs=[pl.BlockSpec((tm, d), lambda i: (i, 0)), pl.BlockSpec((1, d), lambda i: (0, 0))],
        out_specs=[pl.BlockSpec((tm, d), lambda i: (i, 0)), pl.BlockSpec((tm, 1), lambda i: (i, 0))],
        out_shape=[jax.ShapeDtypeStruct((t, d), BF16), jax.ShapeDtypeStruct((t, 1), F32)],
        compiler_params=_params("parallel"),
    )(x, g)


def _rms_bwd(dh, x, rinv, g, add, name):
    t, d = x.shape
    tm = _tile(t, 256, 8)

    def body(dh_ref, x_ref, r_ref, g_ref, add_ref, dx_ref, dg_ref):
        @pl.when(pl.program_id(0) == 0)
        def _():
            dg_ref[...] = jnp.zeros_like(dg_ref)

        r = r_ref[...]
        xn = x_ref[...] * r
        dhv = dh_ref[...]
        dg_ref[...] += jnp.sum(dhv * xn, axis=0, keepdims=True)
        dxn = dhv * g_ref[...]
        dx_ref[...] = add_ref[...] + r * (dxn - xn * jnp.mean(dxn * xn, axis=-1, keepdims=True))

    row = pl.BlockSpec((tm, d), lambda i: (i, 0))
    vec = pl.BlockSpec((1, d), lambda i: (0, 0))
    return pl.pallas_call(
        body, name=name, grid=(t // tm,),
        in_specs=[row, row, pl.BlockSpec((tm, 1), lambda i: (i, 0)), vec, row],
        out_specs=[row, vec],
        out_shape=[jax.ShapeDtypeStruct((t, d), F32), jax.ShapeDtypeStruct((1, d), F32)],
        compiler_params=_params("arbitrary"),
    )(dh, x, rinv, g, add)


def _post_loss(yo, x, tgt, g, name):
    t, d = x.shape
    tm = _tile(t, 256, 8)

    def body(yo_ref, x_ref, t_ref, g_ref, loss_ref, dout_ref, dyo_ref, dg_ref):
        @pl.when(pl.program_id(0) == 0)
        def _():
            dg_ref[...] = jnp.zeros_like(dg_ref)
            loss_ref[...] = jnp.zeros_like(loss_ref)

        yv = yo_ref[...]
        r = lax.rsqrt(jnp.mean(yv * yv, axis=-1, keepdims=True) + RMS_EPS)
        n = yv * r
        err = x_ref[...] + n * g_ref[...] - t_ref[...]
        loss_ref[...] += 0.5 * jnp.sum(jnp.mean(err * err, axis=-1, keepdims=True), axis=0, keepdims=True)
        dout = err * (1.0 / d)
        dout_ref[...] = dout
        dg_ref[...] += jnp.sum(dout * n, axis=0, keepdims=True)
        dn = dout * g_ref[...]
        dyo_ref[...] = (r * (dn - n * jnp.mean(dn * n, axis=-1, keepdims=True))).astype(BF16)

    row = pl.BlockSpec((tm, d), lambda i: (i, 0))
    vec = pl.BlockSpec((1, d), lambda i: (0, 0))
    return pl.pallas_call(
        body, name=name, grid=(t // tm,),
        in_specs=[row, row, row, vec],
        out_specs=[pl.BlockSpec((1, 1), lambda i: (0, 0)), row, row, vec],
        out_shape=[jax.ShapeDtypeStruct((1, 1), F32), jax.ShapeDtypeStruct((t, d), F32),
                   jax.ShapeDtypeStruct((t, d), BF16), jax.ShapeDtypeStruct((1, d), F32)],
        compiler_params=_params("arbitrary"),
    )(yo, x, tgt, g)


def _head_sum(x):
    ri = lax.broadcasted_iota(jnp.int32, (LANES, LANES), 0) // HEAD_DIM
    ci = lax.broadcasted_iota(jnp.int32, (LANES, LANES), 1) // HEAD_DIM
    e = (ri == ci).astype(BF16)
    x1 = x.astype(BF16)
    r1 = x - x1.astype(F32)
    x2 = r1.astype(BF16)
    x3 = (r1 - x2.astype(F32)).astype(BF16)
    parts = []
    for i in range(x.shape[1] // LANES):
        sl = slice(i * LANES, (i + 1) * LANES)
        parts.append(_mm(x1[:, sl], e) + (_mm(x2[:, sl], e) + _mm(x3[:, sl], e)))
    return parts[0] if len(parts) == 1 else jnp.concatenate(parts, axis=1)


def _shifted(p_cur, before, first, mu):
    rolled = pltpu.roll(p_cur, 1, 0)
    prev_row = jnp.where(first, 0.0, before)
    row0 = lax.broadcasted_iota(jnp.int32, p_cur.shape, 0) == 0
    prev = jnp.where(row0, prev_row, rolled)
    return p_cur + (prev - p_cur) * mu, prev


def _rwkv_features(ps, rw, w0, a0, k_k, k_a, wd, wi):
    r, k, v = ps[:, 0:rw], ps[:, rw:2 * rw], ps[:, 2 * rw:3 * rw]
    wl, al = ps[:, 3 * rw:3 * rw + LORA], ps[:, 3 * rw + LORA:3 * rw + 2 * LORA]
    tw = jnp.tanh(wl)
    zw = w0 + _mm(tw.astype(BF16), wd)
    logw = -jnp.exp(-_softplus(-zw) - 0.5)
    alpha = _sig(a0 + _mm(al.astype(BF16), wi))
    kkr = k * k_k
    n2 = _head_sum(kkr * kkr)
    rn = lax.rsqrt(jnp.maximum(n2, 1e-24))
    kk = kkr * rn
    kmod = k * (1.0 + (alpha - 1.0) * k_a)
    return dict(r=r, k=k, v=v, tw=tw, al=al, zw=zw, logw=logw, alpha=alpha, kk=kk, rn=rn, n2=n2, kmod=kmod)


def _rwkv_pre_fwd(p, c, mu, w0, a0, k_k, k_a, wd, wi):
    t = p.shape[0]
    tm = _tile(t, 128, 8)
    rw, sh = c.rw, c.shift

    def body(p_ref, pp_ref, mu_ref, w0_ref, a0_ref, kk_ref, ka_ref, wd_ref, wi_ref,
             r_ref, lw_ref, km_ref, v_ref, a_ref, b_ref):
        ps, _ = _shifted(p_ref[...], pp_ref[7:8, :], pl.program_id(0) == 0, mu_ref[...])
        f = _rwkv_features(ps, rw, w0_ref[...], a0_ref[...], kk_ref[...], ka_ref[...], wd_ref[...], wi_ref[...])
        r_ref[...] = f["r"]
        lw_ref[...] = f["logw"]
        km_ref[...] = f["kmod"]
        v_ref[...] = f["v"]
        a_ref[...] = -f["kk"]
        b_ref[...] = f["kk"] * f["alpha"]

    vec = lambda n: pl.BlockSpec((1, n), lambda i: (0, 0))
    out = pl.BlockSpec((tm, rw), lambda i: (i, 0))
    return pl.pallas_call(
        body, name="rwkv_pre_fwd", grid=(t // tm,),
        in_specs=[pl.BlockSpec((tm, sh), lambda i: (i, 0)),
                  pl.BlockSpec((8, sh), lambda i: (jnp.maximum(i * (tm // 8) - 1, 0), 0)),
                  vec(sh), vec(rw), vec(rw), vec(rw), vec(rw),
                  pl.BlockSpec((LORA, rw), lambda i: (0, 0)), pl.BlockSpec((LORA, rw), lambda i: (0, 0))],
        out_specs=[out] * 6,
        out_shape=[jax.ShapeDtypeStruct((t, rw), F32)] * 6,
        compiler_params=_params("parallel"),
    )(p, p, mu, w0, a0, k_k, k_a, wd, wi)


def _rwkv_pre_bwd(p, c, mu, w0, a0, k_k, k_a, wd, wi, dr, dlw, dkm, dv, da, db, dr2, dkm2, dv2):
    t = p.shape[0]
    tm = _tile(t, 128, 8)
    rw, sh = c.rw, c.shift

    def body(p_ref, pp_ref, mu_ref, w0_ref, a0_ref, kk_ref, ka_ref, wd_ref, wi_ref,
             dr_ref, dlw_ref, dkm_ref, dv_ref, da_ref, db_ref, dr2_ref, dkm2_ref, dv2_ref,
             dps_ref, dzw_ref, dza_ref, tw_ref, al_ref, dw0_ref, da0_ref, dkk_ref, dka_ref):
        @pl.when(pl.program_id(0) == 0)
        def _():
            for ref in (dw0_ref, da0_ref, dkk_ref, dka_ref):
                ref[...] = jnp.zeros_like(ref)

        ps, _ = _shifted(p_ref[...], pp_ref[7:8, :], pl.program_id(0) == 0, mu_ref[...])
        k_k, k_a = kk_ref[...], ka_ref[...]
        f = _rwkv_features(ps, rw, w0_ref[...], a0_ref[...], k_k, k_a, wd_ref[...], wi_ref[...])
        alpha, kk, k = f["alpha"], f["kk"], f["k"]
        dkm = dkm_ref[...] + dkm2_ref[...]
        db = db_ref[...]
        dkk = db * alpha - da_ref[...]
        dalpha = db * kk + dkm * k * k_a
        dk = dkm * (1.0 + (alpha - 1.0) * k_a)
        dka_ref[...] += jnp.sum(dkm * k * (alpha - 1.0), axis=0, keepdims=True)
        dkkr = f["rn"] * jnp.where(f["n2"] > 1e-24, dkk - kk * _head_sum(dkk * kk), dkk)
        dk = dk + dkkr * k_k
        dkk_ref[...] += jnp.sum(dkkr * k, axis=0, keepdims=True)
        dza = dalpha * alpha * (1.0 - alpha)
        da0_ref[...] += jnp.sum(dza, axis=0, keepdims=True)
        dzw = dlw_ref[...] * f["logw"] * _sig(-f["zw"])
        dw0_ref[...] += jnp.sum(dzw, axis=0, keepdims=True)
        dza_b, dzw_b = dza.astype(BF16), dzw.astype(BF16)
        dal = _mm_nt(dza_b, wi_ref[...])
        dwl = _mm_nt(dzw_b, wd_ref[...]) * (1.0 - f["tw"] * f["tw"])
        dps_ref[:, 0:rw] = dr_ref[...] + dr2_ref[...]
        dps_ref[:, rw:2 * rw] = dk
        dps_ref[:, 2 * rw:3 * rw] = dv_ref[...] + dv2_ref[...]
        dps_ref[:, 3 * rw:3 * rw + LORA] = dwl
        dps_ref[:, 3 * rw + LORA:sh] = dal
        dzw_ref[...] = dzw_b
        dza_ref[...] = dza_b
        tw_ref[...] = f["tw"].astype(BF16)
        al_ref[...] = f["al"].astype(BF16)

    vec = lambda n: pl.BlockSpec((1, n), lambda i: (0, 0))
    blk = lambda n: pl.BlockSpec((tm, n), lambda i: (i, 0))
    return pl.pallas_call(
        body, name="rwkv_pre_bwd", grid=(t // tm,),
        in_specs=[blk(sh), pl.BlockSpec((8, sh), lambda i: (jnp.maximum(i * (tm // 8) - 1, 0), 0)),
                  vec(sh), vec(rw), vec(rw), vec(rw), vec(rw),
                  pl.BlockSpec((LORA, rw), lambda i: (0, 0)), pl.BlockSpec((LORA, rw), lambda i: (0, 0))]
                 + [blk(rw)] * 9,
        out_specs=[blk(sh), blk(rw), blk(rw), blk(LORA), blk(LORA), vec(rw), vec(rw), vec(rw), vec(rw)],
        out_shape=[jax.ShapeDtypeStruct((t, sh), F32), jax.ShapeDtypeStruct((t, rw), BF16),
                   jax.ShapeDtypeStruct((t, rw), BF16), jax.ShapeDtypeStruct((t, LORA), BF16),
                   jax.ShapeDtypeStruct((t, LORA), BF16)] + [jax.ShapeDtypeStruct((1, rw), F32)] * 4,
        compiler_params=_params("arbitrary"),
    )(p, p, mu, w0, a0, k_k, k_a, wd, wi, dr, dlw, dkm, dv, da, db, dr2, dkm2, dv2)


def _shift_bwd(dps, p, c, mu, dp):
    t = p.shape[0]
    tm = _tile(t, 256, 8)
    sh = c.shift
    nt = t // tm

    def body(d_ref, dn_ref, p_ref, pp_ref, mu_ref, dp_ref, dmu_ref):
        i = pl.program_id(0)

        @pl.when(i == 0)
        def _():
            dmu_ref[...] = jnp.zeros_like(dmu_ref)

        mu = mu_ref[...]
        d = d_ref[...]
        pc = p_ref[...]
        _, prev = _shifted(pc, pp_ref[7:8, :], i == 0, mu)
        dmu_ref[...] += jnp.sum(d * (prev - pc), axis=0, keepdims=True)
        nxt_row = jnp.where(i == nt - 1, 0.0, dn_ref[0:1, :])
        last = lax.broadcasted_iota(jnp.int32, d.shape, 0) == tm - 1
        nxt = jnp.where(last, nxt_row, pltpu.roll(d, tm - 1, 0))
        dp_ref[...] = (d * (1.0 - mu) + nxt * mu).astype(BF16)

    blk = pl.BlockSpec((tm, sh), lambda i: (i, 0))
    return _pallas_into(
        body, dp, 5, 0, name="shift_bwd", grid=(nt,),
        in_specs=[blk, pl.BlockSpec((8, sh), lambda i: (jnp.minimum((i + 1) * (tm // 8), t // 8 - 1), 0)),
                  blk, pl.BlockSpec((8, sh), lambda i: (jnp.maximum(i * (tm // 8) - 1, 0), 0)),
                  pl.BlockSpec((1, sh), lambda i: (0, 0))],
        out_specs=[blk, pl.BlockSpec((1, sh), lambda i: (0, 0))],
        out_shape=[jax.ShapeDtypeStruct((t, c.wp), BF16), jax.ShapeDtypeStruct((1, sh), F32)],
        compiler_params=_params("arbitrary"),
    )(dps, dps, p, p, mu)


def _tri(n, strict):
    ri = lax.broadcasted_iota(jnp.int32, (n, n), 0)
    ci = lax.broadcasted_iota(jnp.int32, (n, n), 1)
    return (ri > ci) if strict else (ri >= ci)


def _unit_lower_inverse(a):
    n = a.shape[-1]
    ri = lax.broadcasted_iota(jnp.int32, (n, n), 0)
    ci = lax.broadcasted_iota(jnp.int32, (n, n), 1)
    eye = (ri == ci).astype(F32)
    blk = lambda s: (ri // s) == (ci // s)
    ad = jnp.where(blk(16), a, 0.0)
    p = eye + ad
    for _ in range(3):
        ad = _bmm(ad, ad, P_SOLVE)
        p = p + _bmm(p, ad, P_SOLVE)
    s = 16
    while s < n:
        off = jnp.where(blk(2 * s) & ~blk(s), a, 0.0)
        p = p + _bmm(_bmm(p, off, P_SOLVE), p, P_SOLVE)
        s *= 2
    return p


P_SOLVE, P_STATE, P_OUT, P_GRAD, P_DECAY = 1, 1, 1, 1, 2


def _chunk_common(r, lw, k, a, b):
    n = r.shape[1]
    tri_incl = jnp.broadcast_to(_tri(n, False).astype(BF16), (r.shape[0], n, n))
    cum = _bmm_01(tri_incl, lw)
    e_pos, e_neg, e_exc = jnp.exp(cum), jnp.exp(-cum), jnp.exp(cum - lw)
    last = lax.broadcasted_iota(jnp.int32, (n, r.shape[2]), 0) == n - 1
    g_last = jnp.exp(jnp.sum(jnp.where(last, cum, 0.0), axis=1, keepdims=True))
    return g_last, r * e_pos, a * e_exc, b * e_neg, k * e_neg, e_pos, e_neg, e_exc


def _chunk_solve(rt, at, bt, kt, v, g0):
    strict, incl = _tri(rt.shape[1], True), _tri(rt.shape[1], False)
    a_ab = jnp.where(strict, _bmm_nt(at, bt, P_SOLVE), 0.0)
    a_ak = jnp.where(strict, _bmm_nt(at, kt, P_SOLVE), 0.0)
    a_rb = jnp.where(incl, _bmm_nt(rt, bt, P_OUT), 0.0)
    a_rk = jnp.where(incl, _bmm_nt(rt, kt, P_OUT), 0.0)
    tinv = _unit_lower_inverse(a_ab)
    u = _bmm(tinv, _bmm(at, g0, P_SOLVE) + _bmm(a_ak, v, P_SOLVE), P_SOLVE)
    return a_ab, a_ak, a_rb, a_rk, tinv, u


def _diag_col(row, n):
    ri = lax.broadcasted_iota(jnp.int32, (n, n), 0)
    ci = lax.broadcasted_iota(jnp.int32, (n, n), 1)
    return jnp.sum(jnp.where(ri == ci, row, 0.0), axis=2, keepdims=True)


def _diag_row(col, n):
    ri = lax.broadcasted_iota(jnp.int32, (n, n), 0)
    ci = lax.broadcasted_iota(jnp.int32, (n, n), 1)
    return jnp.sum(jnp.where(ri == ci, col, 0.0), axis=1, keepdims=True)


def _rwkv_scan_fwd(r, lw, k, v, a, b, hb):
    h, t, n = r.shape
    nc = t // CHUNK

    def body(r_ref, lw_ref, k_ref, v_ref, a_ref, b_ref, y_ref, st_ref, g_sc):
        @pl.when(pl.program_id(1) == 0)
        def _():
            g_sc[...] = jnp.zeros_like(g_sc)

        g0 = g_sc[...]
        st_ref[0] = g0
        vv = v_ref[...]
        g_last, rt, at, bt, kt, _, _, _ = _chunk_common(r_ref[...], lw_ref[...], k_ref[...], a_ref[...], b_ref[...])
        _, _, a_rb, a_rk, _, u = _chunk_solve(rt, at, bt, kt, vv, g0)
        y_ref[...] = _bmm(rt, g0, P_OUT) + _bmm(a_rb, u, P_OUT) + _bmm(a_rk, vv, P_OUT)
        z = g0 + _bmm_tn(bt, u, P_STATE) + _bmm_tn(kt, vv, P_STATE)
        g_sc[...] = _diag_col(g_last, n) * z

    blk = pl.BlockSpec((hb, CHUNK, n), lambda i, j: (i, j, 0))
    return pl.pallas_call(
        body, name="rwkv_scan_fwd", grid=(h // hb, nc),
        in_specs=[blk] * 6,
        out_specs=[blk, pl.BlockSpec((1, hb, n, n), lambda i, j: (j, i, 0, 0))],
        out_shape=[jax.ShapeDtypeStruct((h, t, n), F32), jax.ShapeDtypeStruct((nc, h, n, n), F32)],
        scratch_shapes=[pltpu.VMEM((hb, n, n), F32)],
        compiler_params=_params("parallel", "arbitrary"),
    )(r, lw, k, v, a, b)


def _rwkv_scan_bwd(r, lw, k, v, a, b, states, dy, hb):
    h, t, n = r.shape
    nc = t // CHUNK

    def body(r_ref, lw_ref, k_ref, v_ref, a_ref, b_ref, st_ref, dy_ref,
             dr_ref, dlw_ref, dk_ref, dv_ref, da_ref, db_ref, dg_sc):
        @pl.when(pl.program_id(1) == 0)
        def _():
            dg_sc[...] = jnp.zeros_like(dg_sc)

        g0 = st_ref[0]
        vv, dyv, dh = v_ref[...], dy_ref[...], dg_sc[...]
        lwv = lw_ref[...]
        g_last, rt, at, bt, kt, e_pos, e_neg, e_exc = _chunk_common(r_ref[...], lwv, k_ref[...], a_ref[...], b_ref[...])
        a_ab, a_ak, a_rb, a_rk, tinv, u = _chunk_solve(rt, at, bt, kt, vv, g0)
        strict, incl = _tri(CHUNK, True), _tri(CHUNK, False)
        gcol = _diag_col(g_last, n)
        z = g0 + _bmm_tn(bt, u, P_STATE) + _bmm_tn(kt, vv, P_STATE)
        dz = gcol * dh
        dc_last = _diag_row(jnp.sum(dh * gcol * z, axis=2, keepdims=True), n)
        g = P_GRAD
        du = _bmm_tn(a_rb, dyv, g) + _bmm(bt, dz, g)
        dx = _bmm_tn(tinv, du, P_SOLVE)
        dv_ref[...] = _bmm_tn(a_rk, dyv, g) + _bmm(kt, dz, g) + _bmm_tn(a_ak, dx, g)
        da_ab = jnp.where(strict, _bmm_nt(dx, u, g), 0.0)
        da_ak = jnp.where(strict, _bmm_nt(dx, vv, g), 0.0)
        da_rb = jnp.where(incl, _bmm_nt(dyv, u, g), 0.0)
        da_rk = jnp.where(incl, _bmm_nt(dyv, vv, g), 0.0)
        g = P_DECAY
        d_at = _bmm(da_ab, bt, g) + _bmm(da_ak, kt, g) + _bmm_nt(dx, g0, g)
        d_rt = _bmm(da_rb, bt, g) + _bmm(da_rk, kt, g) + _bmm_nt(dyv, g0, g)
        d_bt = _bmm_tn(da_ab, at, g) + _bmm_tn(da_rb, rt, g) + _bmm_nt(u, dz, g)
        d_kt = _bmm_tn(da_ak, at, g) + _bmm_tn(da_rk, rt, g) + _bmm_nt(vv, dz, g)
        dg_sc[...] = dz + _bmm_tn(rt, dyv, P_STATE) + _bmm_tn(at, dx, P_STATE)
        dr_ref[...] = d_rt * e_pos
        da_ref[...] = d_at * e_exc
        db_ref[...] = d_bt * e_neg
        dk_ref[...] = d_kt * e_neg
        last = lax.broadcasted_iota(jnp.int32, (CHUNK, n), 0) == CHUNK - 1
        dc = d_rt * rt - d_bt * bt - d_kt * kt + jnp.where(last, dc_last, 0.0)
        dce = d_at * at
        ri = lax.broadcasted_iota(jnp.int32, (CHUNK, CHUNK), 0)
        ci = lax.broadcasted_iota(jnp.int32, (CHUNK, CHUNK), 1)
        up_incl = jnp.broadcast_to((ri <= ci).astype(BF16), (hb, CHUNK, CHUNK))
        dlw_ref[...] = _bmm_01(up_incl, dc + dce) - dce

    rev = lambda i, j: (i, nc - 1 - j, 0)
    blk = pl.BlockSpec((hb, CHUNK, n), rev)
    return pl.pallas_call(
        body, name="rwkv_scan_bwd", grid=(h // hb, nc),
        in_specs=[blk] * 6 + [pl.BlockSpec((1, hb, n, n), lambda i, j: (nc - 1 - j, i, 0, 0)), blk],
        out_specs=[blk] * 6,
        out_shape=[jax.ShapeDtypeStruct((h, t, n), F32)] * 6,
        scratch_shapes=[pltpu.VMEM((hb, n, n), F32)],
        compiler_params=_params("parallel", "arbitrary"),
    )(r, lw, k, v, a, b, states, dy)


def _silu_grad(g):
    s = _sig(g)
    return s * (1.0 + g * (1.0 - s))


def _group_norm(ys):
    yc = ys - _head_sum(ys) * (1.0 / HEAD_DIM)
    rstd = lax.rsqrt(_head_sum(yc * yc) * (1.0 / HEAD_DIM) + GN_EPS)
    return yc * rstd, rstd


def _rwkv_post_fwd(ys, r, km, v, p, c, ln_w, ln_b, r_k):
    t = ys.shape[0]
    tm = _tile(t, 512, 8)
    goff = c.o_grw // LANES

    def body(ys_ref, r_ref, km_ref, v_ref, g_ref, lw_ref, lb_ref, rk_ref, o_ref):
        yn, _ = _group_norm(ys_ref[...])
        s = _head_sum(r_ref[...] * km_ref[...] * rk_ref[...])
        g = g_ref[...]
        o_ref[...] = ((yn * lw_ref[...] + lb_ref[...] + s * v_ref[...]) * g * _sig(g)).astype(BF16)

    blk = pl.BlockSpec((tm, LANES), lambda i, j: (i, j))
    vec = pl.BlockSpec((1, LANES), lambda i, j: (0, j))
    return pl.pallas_call(
        body, name="rwkv_post_fwd", grid=(t // tm, c.rw // LANES),
        in_specs=[blk] * 4 + [pl.BlockSpec((tm, LANES), lambda i, j: (i, goff + j)), vec, vec, vec],
        out_specs=blk, out_shape=jax.ShapeDtypeStruct((t, c.d), BF16),
        compiler_params=_params("parallel", "parallel"),
    )(ys, r, km, v, p, ln_w, ln_b, r_k)


def _rwkv_post_bwd(dyc, ys, r, km, v, p, c, ln_w, ln_b, r_k):
    t = ys.shape[0]
    tm = _tile(t, 512, 8)
    goff = c.o_grw // LANES

    def body(dy_ref, ys_ref, r_ref, km_ref, v_ref, g_ref, lw_ref, lb_ref, rk_ref,
             dys_ref, dr_ref, dkm_ref, dv_ref, dg_ref, dlw_ref, dlb_ref, drk_ref):
        @pl.when(pl.program_id(1) == 0)
        def _():
            for ref in (dlw_ref, dlb_ref, drk_ref):
                ref[...] = jnp.zeros_like(ref)

        yn, rstd = _group_norm(ys_ref[...])
        rv, kmv, vv, rk, g = r_ref[...], km_ref[...], v_ref[...], rk_ref[...], g_ref[...]
        s = _head_sum(rv * kmv * rk)
        y = yn * lw_ref[...] + lb_ref[...] + s * vv
        dyc = dy_ref[...]
        dg_ref[...] = (dyc * y * _silu_grad(g)).astype(BF16)
        dy = dyc * g * _sig(g)
        dlb_ref[...] += jnp.sum(dy, axis=0, keepdims=True)
        dlw_ref[...] += jnp.sum(dy * yn, axis=0, keepdims=True)
        dyn = dy * lw_ref[...]
        inv = 1.0 / HEAD_DIM
        dys_ref[...] = rstd * (dyn - _head_sum(dyn) * inv - yn * _head_sum(dyn * yn) * inv)
        ds = _head_sum(dy * vv)
        dv_ref[...] = dy * s
        dr_ref[...] = ds * kmv * rk
        dkm_ref[...] = ds * rv * rk
        drk_ref[...] += jnp.sum(ds * rv * kmv, axis=0, keepdims=True)

    blk = pl.BlockSpec((tm, LANES), lambda j, i: (i, j))
    vec = pl.BlockSpec((1, LANES), lambda j, i: (0, j))
    f = jax.ShapeDtypeStruct((t, c.rw), F32)
    s1 = jax.ShapeDtypeStruct((1, c.rw), F32)
    gate = pl.BlockSpec((tm, LANES), lambda j, i: (i, goff + j))
    return pl.pallas_call(
        body, name="rwkv_post_bwd", grid=(c.rw // LANES, t // tm),
        in_specs=[blk] * 5 + [gate, vec, vec, vec],
        out_specs=[blk] * 4 + [gate] + [vec] * 3,
        out_shape=[f, f, f, f, jax.ShapeDtypeStruct((t, c.wp), BF16), s1, s1, s1],
        compiler_params=_params("parallel", "arbitrary"),
    )(dyc, ys, r, km, v, p, ln_w, ln_b, r_k)


def _gate_fwd(y, p, goff, name, ycat, yoff):
    t, w = y.shape
    tm = _tile(t, 512, 8)
    gb, ob = goff // LANES, yoff // LANES

    def body(y_ref, g_ref, o_ref):
        g = g_ref[...]
        o_ref[...] = (y_ref[...] * g * _sig(g)).astype(BF16)

    blk = pl.BlockSpec((tm, LANES), lambda i, j: (i, j))
    return _pallas_into(
        body, ycat, 2, 0, name=name, grid=(t // tm, w // LANES),
        in_specs=[blk, pl.BlockSpec((tm, LANES), lambda i, j: (i, gb + j))],
        out_specs=pl.BlockSpec((tm, LANES), lambda i, j: (i, ob + j)),
        out_shape=jax.ShapeDtypeStruct(ycat.shape, BF16),
        compiler_params=_params("parallel", "parallel"),
    )(y, p)


def _gate_bwd(dyc, yoff, y, p, goff, name, dp):
    t, w = y.shape
    tm = _tile(t, 512, 8)
    gb, yb = goff // LANES, yoff // LANES

    def body(d_ref, y_ref, g_ref, dy_ref, dg_ref):
        g, d = g_ref[...], d_ref[...]
        dy_ref[...] = d * g * _sig(g)
        dg_ref[...] = (d * y_ref[...] * _silu_grad(g)).astype(BF16)

    blk = pl.BlockSpec((tm, LANES), lambda i, j: (i, j))
    gate = pl.BlockSpec((tm, LANES), lambda i, j: (i, gb + j))
    return _pallas_into(
        body, dp, 3, 1, name=name, grid=(t // tm, w // LANES),
        in_specs=[pl.BlockSpec((tm, LANES), lambda i, j: (i, yb + j)), blk, gate],
        out_specs=[blk, gate],
        out_shape=[jax.ShapeDtypeStruct((t, w), F32), jax.ShapeDtypeStruct(dp.shape, BF16)],
        compiler_params=_params("parallel", "parallel"),
    )(dyc, y, p)


NEG = -1e30


def _fox_logit_bwd(dcum, p, c, b_f):
    t = p.shape[0]
    tm = _tile(t, 512, 8)
    fb = c.o_fl // LANES
    nt = t // tm

    def body(d_ref, f_ref, b_ref, o_ref, db_ref, carry):
        @pl.when(pl.program_id(0) == 0)
        def _():
            carry[...] = jnp.zeros_like(carry)
            db_ref[...] = jnp.zeros_like(db_ref)

        d = d_ref[0] + d_ref[1]
        dlogf = _mm(_tri(tm, False).astype(F32).T, d, HI) + carry[...]
        carry[...] += jnp.sum(d, axis=0, keepdims=True)
        df = dlogf * _sig(-(f_ref[...] + b_ref[...]))
        o_ref[...] = df.astype(BF16)
        db_ref[...] += jnp.sum(df, axis=0, keepdims=True)

    return pl.pallas_call(
        body, name="fox_logit_bwd", grid=(nt,),
        in_specs=[pl.BlockSpec((2, tm, LANES), lambda i: (0, nt - 1 - i, 0)),
                  pl.BlockSpec((tm, LANES), lambda i: (nt - 1 - i, fb)),
                  pl.BlockSpec((1, LANES), lambda i: (0, 0))],
        out_specs=[pl.BlockSpec((tm, LANES), lambda i: (nt - 1 - i, 0)), pl.BlockSpec((1, LANES), lambda i: (0, 0))],
        out_shape=[jax.ShapeDtypeStruct((t, LANES), BF16), jax.ShapeDtypeStruct((1, LANES), F32)],
        scratch_shapes=[pltpu.VMEM((1, LANES), F32)],
        compiler_params=_params("arbitrary"),
    )(dcum, p, b_f)


FOX_PAIRS = 2
FOX_HEADS_STEP = 2 * FOX_PAIRS


def _lane_half(shape, upper):
    li = lax.broadcasted_iota(jnp.int32, shape, len(shape) - 1)
    return (li >= HEAD_DIM) if upper else (li < HEAD_DIM)


def _col(block, j):
    li = lax.broadcasted_iota(jnp.int32, block.shape, 1)
    return jnp.sum(jnp.where(li == j, block, 0.0), axis=1, keepdims=True)


def _from_cols(cols):
    li = lax.broadcasted_iota(jnp.int32, (cols[0].shape[0], len(cols)), 1)
    out = jnp.zeros(li.shape, F32)
    for j, cj in enumerate(cols):
        out = jnp.where(li == j, cj, out)
    return out


def _from_rows(rows):
    si = lax.broadcasted_iota(jnp.int32, (len(rows), rows[0].shape[1]), 0)
    out = jnp.zeros(si.shape, F32)
    for j, rj in enumerate(rows):
        out = jnp.where(si == j, rj, out)
    return out


def _causal(tq, tk):
    return lax.broadcasted_iota(jnp.int32, (tq, tk), 1) <= lax.broadcasted_iota(jnp.int32, (tq, tk), 0)


def _fox_prep_t(p, c, b_f):
    t = p.shape[0]
    tm = _tile(t, 512, LANES)
    fb = c.o_fl // LANES

    def body(f_ref, b_ref, o_ref, carry):
        @pl.when(pl.program_id(0) == 0)
        def _():
            carry[...] = jnp.zeros_like(carry)

        logf = -_softplus(-(f_ref[...] + b_ref[...]))
        cum = _mm(_tri(tm, False).astype(F32), logf, HI) + carry[...]
        o_ref[...] = cum.T
        carry[...] += jnp.sum(logf, axis=0, keepdims=True)

    return pl.pallas_call(
        body, name="fox_prep", grid=(t // tm,),
        in_specs=[pl.BlockSpec((tm, LANES), lambda i: (i, fb)), pl.BlockSpec((1, LANES), lambda i: (0, 0))],
        out_specs=pl.BlockSpec((LANES, tm), lambda i: (0, i)),
        out_shape=jax.ShapeDtypeStruct((LANES, t), F32),
        scratch_shapes=[pltpu.VMEM((1, LANES), F32)],
        compiler_params=_params("arbitrary"),
    )(p, b_f)


def _fox2_fwd(p, c, cum_t, tb, ycat):
    t = p.shape[0]
    tq = tk = _tile(t, tb, LANES)
    nq = t // tq
    pw, nh = FOX_PAIRS * LANES, FOX_HEADS_STEP
    qb, kb, vb, gb = (o // pw for o in (c.o_fq, c.o_fk, c.o_fv, c.o_gfox))
    scale = HEAD_DIM ** -0.5

    def body(q_ref, k_ref, v_ref, g_ref, ck_ref, o_ref, y_ref, lse_ref, m_sc, l_sc, acc_sc):
        g, qi, ki = pl.program_id(0), pl.program_id(1), pl.program_id(2)

        @pl.when(ki == 0)
        def _():
            m_sc[...] = jnp.full_like(m_sc, NEG)
            l_sc[...] = jnp.zeros_like(l_sc)
            acc_sc[...] = jnp.zeros_like(acc_sc)

        def step(diag):
            ms, ls = [m_sc[h] for h in range(nh)], [l_sc[h] for h in range(nh)]
            accs = [acc_sc[:, pi * LANES:(pi + 1) * LANES] for pi in range(FOX_PAIRS)]
            for pi in range(FOX_PAIRS):
                lanes = slice(pi * LANES, (pi + 1) * LANES)
                q2 = (q_ref[:, lanes] * scale).astype(BF16)
                k2, v2 = k_ref[:, lanes].astype(BF16), v_ref[:, lanes].astype(BF16)
                new_acc = accs[pi]
                for hh in range(2):
                    hi = 2 * pi + hh
                    mk = _lane_half((tq, LANES), hh == 1)
                    s = _mm_nt(jnp.where(mk, q2, jnp.zeros_like(q2)), k2) - ck_ref[pl.ds(g * nh + hi, 1), :]
                    if diag:
                        s = jnp.where(_causal(tq, tk), s, NEG)
                    m_new = jnp.maximum(ms[hi], jnp.max(s, axis=1, keepdims=True))
                    a = jnp.exp(ms[hi] - m_new)
                    e = jnp.exp(s - jnp.concatenate([m_new] * (tk // LANES), axis=1))
                    ls[hi] = a * ls[hi] + jnp.sum(e, axis=1, keepdims=True)
                    ms[hi] = m_new
                    new_acc = jnp.where(mk, a * accs[pi] + _mm(e.astype(BF16), v2), new_acc)
                accs[pi] = new_acc
            for h in range(nh):
                m_sc[h] = ms[h]
                l_sc[h] = ls[h]
            for pi in range(FOX_PAIRS):
                acc_sc[:, pi * LANES:(pi + 1) * LANES] = accs[pi]

        @pl.when(ki < qi)
        def _():
            step(False)

        @pl.when(ki == qi)
        def _():
            step(True)
            li = lax.broadcasted_iota(jnp.int32, (tq, LANES), 1)
            lse = jnp.zeros((tq, LANES), F32)
            for pi in range(FOX_PAIRS):
                lanes = slice(pi * LANES, (pi + 1) * LANES)
                inv = jnp.where(_lane_half((tq, LANES), False), 1.0 / l_sc[2 * pi], 1.0 / l_sc[2 * pi + 1])
                o = acc_sc[:, lanes] * inv
                gate = g_ref[:, lanes]
                o_ref[:, lanes] = o
                y_ref[:, lanes] = (o * gate * _sig(gate)).astype(BF16)
            for h in range(nh):
                lse = jnp.where(li == h, m_sc[h] + jnp.log(l_sc[h]), lse)
            lse_ref[0] = lse

    row = lambda off: pl.BlockSpec((tq, pw), lambda g, i, j: (i, off + g))
    key = lambda off: pl.BlockSpec((tk, pw), lambda g, i, j: (jnp.minimum(i, j), off + g))
    out = pl.BlockSpec((tq, pw), lambda g, i, j: (i, g))
    return _pallas_into(
        body, ycat, 5, 1, name="fox_fwd", grid=(c.rw // pw, nq, nq),
        in_specs=[row(qb), key(kb), key(vb), row(gb),
                  pl.BlockSpec((LANES, tk), lambda g, i, j: (0, jnp.minimum(i, j)))],
        out_specs=[out, row(c.rw // pw), pl.BlockSpec((1, tq, LANES), lambda g, i, j: (g, i, 0))],
        out_shape=[jax.ShapeDtypeStruct((t, c.rw), F32), jax.ShapeDtypeStruct(ycat.shape, BF16),
                   jax.ShapeDtypeStruct((c.rw // pw, t, LANES), F32)],
        scratch_shapes=[pltpu.VMEM((nh, tq, LANES), F32), pltpu.VMEM((nh, tq, LANES), F32),
                        pltpu.VMEM((tq, pw), F32)],
        compiler_params=_params("parallel", "parallel", "arbitrary"),
    )(p, p, p, p, cum_t)


def _fox2_grads(q2, k2, v2, do2, o2, lse_h, ck, mk, diag, tq, tk):
    zero = jnp.zeros_like(q2)
    s = _mm_nt(jnp.where(mk, q2, zero), k2) - ck
    if diag:
        s = jnp.where(_causal(tq, tk), s, NEG)
    wide = lambda col: jnp.concatenate([jnp.broadcast_to(col, (tq, LANES))] * (tk // LANES), axis=1)
    pm = jnp.exp(s - wide(lse_h))
    delta = jnp.sum(jnp.where(mk, do2 * o2, 0.0), axis=1, keepdims=True)
    dob = do2.astype(BF16)
    dp = _mm_nt(jnp.where(mk, dob, zero), v2)
    return pm, pm * (dp - wide(delta)), dob


def _fox2_bwd_dq(p, c, cum_t, lse, o, do, tb, dp):
    t = p.shape[0]
    tq = tk = _tile(t, tb, LANES)
    nq = t // tq
    pw, nh = FOX_PAIRS * LANES, FOX_HEADS_STEP
    qb, kb, vb = (o_ // pw for o_ in (c.o_fq, c.o_fk, c.o_fv))
    scale = HEAD_DIM ** -0.5

    def body(q_ref, k_ref, v_ref, ck_ref, lse_ref, o_ref, do_ref, dq_ref, dcq_ref, acc_sc, row_sc):
        g, qi, ki = pl.program_id(0), pl.program_id(1), pl.program_id(2)

        @pl.when(ki == 0)
        def _():
            acc_sc[...] = jnp.zeros_like(acc_sc)
            row_sc[...] = jnp.zeros_like(row_sc)

        def step(diag):
            lse_blk = lse_ref[0]
            rows = [row_sc[h] for h in range(nh)]
            accs = [acc_sc[:, pi * LANES:(pi + 1) * LANES] for pi in range(FOX_PAIRS)]
            for pi in range(FOX_PAIRS):
                lanes = slice(pi * LANES, (pi + 1) * LANES)
                q2 = (q_ref[:, lanes] * scale).astype(BF16)
                k2, v2 = k_ref[:, lanes].astype(BF16), v_ref[:, lanes].astype(BF16)
                do2, o2 = do_ref[:, lanes], o_ref[:, lanes]
                new_acc = accs[pi]
                for hh in range(2):
                    hi = 2 * pi + hh
                    mk = _lane_half((tq, LANES), hh == 1)
                    _, ds, _ = _fox2_grads(q2, k2, v2, do2, o2, _col(lse_blk, hi),
                                           ck_ref[pl.ds(g * nh + hi, 1), :], mk, diag, tq, tk)
                    rows[hi] = rows[hi] + jnp.sum(ds, axis=1, keepdims=True)
                    new_acc = jnp.where(mk, accs[pi] + _mm(ds.astype(BF16), k2), new_acc)
                accs[pi] = new_acc
            for h in range(nh):
                row_sc[h] = rows[h]
            for pi in range(FOX_PAIRS):
                acc_sc[:, pi * LANES:(pi + 1) * LANES] = accs[pi]

        @pl.when(ki < qi)
        def _():
            step(False)

        @pl.when(ki == qi)
        def _():
            step(True)
            dq_ref[...] = (acc_sc[...] * scale).astype(BF16)
            dcq_ref[0] = _from_cols([row_sc[h] for h in range(nh)])

    row = lambda off: pl.BlockSpec((tq, pw), lambda g, i, j: (i, off + g))
    key = lambda off: pl.BlockSpec((tk, pw), lambda g, i, j: (jnp.minimum(i, j), off + g))
    stat = pl.BlockSpec((1, tq, nh), lambda g, i, j: (g, i, 0))
    return _pallas_into(
        body, dp, 7, 0, name="fox_bwd_dq", grid=(c.rw // pw, nq, nq),
        in_specs=[row(qb), key(kb), key(vb), pl.BlockSpec((LANES, tk), lambda g, i, j: (0, jnp.minimum(i, j))),
                  pl.BlockSpec((1, tq, LANES), lambda g, i, j: (g, i, 0)), row(0), row(0)],
        out_specs=[row(qb), stat],
        out_shape=[jax.ShapeDtypeStruct(dp.shape, BF16), jax.ShapeDtypeStruct((c.rw // pw, t, nh), F32)],
        scratch_shapes=[pltpu.VMEM((tq, pw), F32), pltpu.VMEM((nh, tq, 1), F32)],
        compiler_params=_params("parallel", "parallel", "arbitrary"),
    )(p, p, p, cum_t, lse, o, do)


def _fox2_bwd_dkv(p, c, cum_t, lse, o, do, tb, dp):
    t = p.shape[0]
    tq = tk = _tile(t, tb, LANES)
    nq = t // tq
    pw, nh = FOX_PAIRS * LANES, FOX_HEADS_STEP
    qb, kb, vb = (o_ // pw for o_ in (c.o_fq, c.o_fk, c.o_fv))
    scale = HEAD_DIM ** -0.5

    def body(q_ref, k_ref, v_ref, ck_ref, lse_ref, o_ref, do_ref, dk_ref, dv_ref, dck_ref, dk_sc, dv_sc, dc_sc):
        g, ki, qi = pl.program_id(0), pl.program_id(1), pl.program_id(2)

        @pl.when(qi == 0)
        def _():
            dk_sc[...] = jnp.zeros_like(dk_sc)
            dv_sc[...] = jnp.zeros_like(dv_sc)
            dc_sc[...] = jnp.zeros_like(dc_sc)

        def step(diag):
            lse_blk = lse_ref[0]
            dcs = [dc_sc[h] for h in range(nh)]
            dks = [dk_sc[:, pi * LANES:(pi + 1) * LANES] for pi in range(FOX_PAIRS)]
            dvs = [dv_sc[:, pi * LANES:(pi + 1) * LANES] for pi in range(FOX_PAIRS)]
            for pi in range(FOX_PAIRS):
                lanes = slice(pi * LANES, (pi + 1) * LANES)
                q2 = (q_ref[:, lanes] * scale).astype(BF16)
                k2, v2 = k_ref[:, lanes].astype(BF16), v_ref[:, lanes].astype(BF16)
                do2, o2 = do_ref[:, lanes], o_ref[:, lanes]
                new_dk, new_dv = dks[pi], dvs[pi]
                for hh in range(2):
                    hi = 2 * pi + hh
                    mk = _lane_half((tk, LANES), hh == 1)
                    pm, ds, dob = _fox2_grads(q2, k2, v2, do2, o2, _col(lse_blk, hi),
                                              ck_ref[pl.ds(g * nh + hi, 1), :], mk, diag, tq, tk)
                    dcs[hi] = dcs[hi] - jnp.sum(ds, axis=0, keepdims=True)
                    new_dv = jnp.where(mk, dvs[pi] + _mm_tn(pm.astype(BF16), dob), new_dv)
                    new_dk = jnp.where(mk, dks[pi] + _mm_tn(ds.astype(BF16), q2), new_dk)
                dks[pi], dvs[pi] = new_dk, new_dv
            for h in range(nh):
                dc_sc[h] = dcs[h]
            for pi in range(FOX_PAIRS):
                dk_sc[:, pi * LANES:(pi + 1) * LANES] = dks[pi]
                dv_sc[:, pi * LANES:(pi + 1) * LANES] = dvs[pi]

        @pl.when(qi > ki)
        def _():
            step(False)

        @pl.when(qi == ki)
        def _():
            step(True)

        @pl.when(qi == nq - 1)
        def _():
            dk_ref[...] = dk_sc[...].astype(BF16)
            dv_ref[...] = dv_sc[...].astype(BF16)
            dck_ref[0] = _from_rows([dc_sc[h] for h in range(nh)])

    row = lambda off: pl.BlockSpec((tq, pw), lambda g, j, i: (jnp.maximum(i, j), off + g))
    key = lambda off: pl.BlockSpec((tk, pw), lambda g, j, i: (j, off + g))
    return _pallas_into(
        body, dp, 7, 0, name="fox_bwd_dkv", grid=(c.rw // pw, nq, nq),
        in_specs=[row(qb), key(kb), key(vb), pl.BlockSpec((LANES, tk), lambda g, j, i: (0, j)),
                  pl.BlockSpec((1, tq, LANES), lambda g, j, i: (g, jnp.maximum(i, j), 0)), row(0), row(0)],
        out_specs=[key(kb), key(0), pl.BlockSpec((1, nh, tk), lambda g, j, i: (g, 0, j))],
        out_shape=[jax.ShapeDtypeStruct(dp.shape, BF16), jax.ShapeDtypeStruct((t, c.rw), BF16),
                   jax.ShapeDtypeStruct((c.rw // pw, nh, t), F32)],
        scratch_shapes=[pltpu.VMEM((tk, pw), F32), pltpu.VMEM((tk, pw), F32), pltpu.VMEM((nh, 1, tk), F32)],
        compiler_params=_params("parallel", "parallel", "arbitrary"),
    )(p, p, p, cum_t, lse, o, do)


def _mem_probs(q, mk, scale):
    s = _mm_nt(q.astype(BF16), mk.astype(BF16)) * scale
    e = jnp.exp(s - jnp.max(s, axis=1, keepdims=True))
    return e / jnp.sum(e, axis=1, keepdims=True)


def _mem_attn_fwd(p, c, mkv):
    t = p.shape[0]
    tm = _tile(t, 512, 8)
    dh = c.mhd
    qb = c.o_mq // dh
    scale = dh ** -0.5

    def body(q_ref, mk_ref, mv_ref, o_ref):
        pm = _mem_probs(q_ref[...], mk_ref[...], scale)
        o_ref[...] = _mm(pm.astype(BF16), mv_ref[...].astype(BF16))

    m = mkv.shape[0]
    return pl.pallas_call(
        body, name="mem_attn_fwd", grid=(t // tm, MEM_HEADS),
        in_specs=[pl.BlockSpec((tm, dh), lambda i, j: (i, qb + j)),
                  pl.BlockSpec((m, dh), lambda i, j: (0, j)),
                  pl.BlockSpec((m, dh), lambda i, j: (0, MEM_HEADS + j))],
        out_specs=pl.BlockSpec((tm, dh), lambda i, j: (i, j)),
        out_shape=jax.ShapeDtypeStruct((t, c.mw), F32),
        compiler_params=_params("parallel", "parallel"),
    )(p, mkv, mkv)


def _mem_attn_bwd(p, c, mkv, do):
    t = p.shape[0]
    tm = _tile(t, 512, 8)
    dh = c.mhd
    qb = c.o_mq // dh
    scale = dh ** -0.5
    m = mkv.shape[0]

    def body(q_ref, mk_ref, mv_ref, do_ref, dq_ref, dmk_ref, dmv_ref):
        @pl.when(pl.program_id(1) == 0)
        def _():
            dmk_ref[...] = jnp.zeros_like(dmk_ref)
            dmv_ref[...] = jnp.zeros_like(dmv_ref)

        qv = q_ref[...].astype(BF16)
        pm = _mem_probs(qv, mk_ref[...], scale)
        dob = do_ref[...].astype(BF16)
        dmv_ref[...] += _mm_tn(pm.astype(BF16), dob)
        dp = _mm_nt(dob, mv_ref[...].astype(BF16))
        ds = (pm * (dp - jnp.sum(pm * dp, axis=1, keepdims=True)) * scale).astype(BF16)
        dq_ref[...] = _mm(ds, mk_ref[...].astype(BF16)).astype(BF16)
        dmk_ref[...] += _mm_tn(ds, qv)

    kvb = lambda off: pl.BlockSpec((m, dh), lambda j, i: (0, off + j))
    return pl.pallas_call(
        body, name="mem_attn_bwd", grid=(MEM_HEADS, t // tm),
        in_specs=[pl.BlockSpec((tm, dh), lambda j, i: (i, qb + j)), kvb(0), kvb(MEM_HEADS),
                  pl.BlockSpec((tm, dh), lambda j, i: (i, j))],
        out_specs=[pl.BlockSpec((tm, dh), lambda j, i: (i, j)), kvb(0), kvb(0)],
        out_shape=[jax.ShapeDtypeStruct((t, c.mw), BF16), jax.ShapeDtypeStruct((m, c.mw), F32),
                   jax.ShapeDtypeStruct((m, c.mw), F32)],
        compiler_params=_params("parallel", "arbitrary"),
    )(p, mkv, mkv, do)


def _adamw(w, g, m, v, name):
    rows, cols = w.shape
    bc1 = 1.0 - ADAM_B1 ** ADAM_STEP
    bc2 = 1.0 - ADAM_B2 ** ADAM_STEP
    if rows % 8 and rows > 8:
        blk = pl.BlockSpec((rows, LANES), lambda i: (0, i))
        g_blk = pl.BlockSpec((g.shape[0], LANES), lambda i: (0, i))
        grid = (cols // LANES,)
    else:
        tm = _tile(rows, max(8, (1 << 18) // cols // 8 * 8), 8)
        blk = pl.BlockSpec((tm, cols), lambda i: (i, 0))
        g_blk = pl.BlockSpec((tm, g.shape[1]), lambda i: (i, 0))
        grid = (rows // tm,)
    brows, bcols = blk.block_shape

    def body(w_ref, g_ref, m_ref, v_ref, go_ref, d_ref, mo_ref, vo_ref):
        gv = g_ref[0:brows, 0:bcols]
        mn = ADAM_B1 * m_ref[...] + (1.0 - ADAM_B1) * gv
        vn = ADAM_B2 * v_ref[...] + (1.0 - ADAM_B2) * (gv * gv)
        go_ref[...] = gv
        mo_ref[...] = mn
        vo_ref[...] = vn
        d_ref[...] = -ADAM_LR * ((mn / bc1) / (jnp.sqrt(vn / bc2) + ADAM_EPS) + ADAM_WD * w_ref[...])

    shp = jax.ShapeDtypeStruct((rows, cols), F32)
    return pl.pallas_call(
        body, name=name, grid=grid,
        in_specs=[blk, g_blk, blk, blk],
        out_specs=[blk] * 4, out_shape=[shp] * 4,
        compiler_params=_params("parallel"),
    )(w, g, m, v)


SCAN_HEADS = 12
FOX_BLOCK = 512


def _local_step(c, x, mem, tgt, w, riders=None):
    t = x.shape[0]
    rw = c.rw
    riders = riders or {}
    carried = {}
    hd = lambda z: z.reshape(t, c.h, HEAD_DIM).transpose(1, 0, 2)
    uh = lambda z: z.transpose(1, 0, 2).reshape(t, rw)
    vecs = (w["mu"], w["w0"], w["a0"], w["k_k"], w["k_a"], w["wd"], w["wi"])

    h, rinv = _rms_fwd(x, w["g_pre"], "rms_pre")
    if "in_proj" in riders:
        groups, finish = riders["in_proj"]
        p, late = _matmul(h, w["wp"], name="in_proj", tk=4096, attach=groups)
        w = dict(w, **finish(late))
    else:
        p = _matmul(h, w["wp"], name="in_proj", tk=4096)
    r, lw, km, v, a, b = _rwkv_pre_fwd(p, c, *vecs)
    scan_in = tuple(hd(z) for z in (r, lw, km, v, a, b))
    hb = max(n for n in range(1, SCAN_HEADS + 1) if c.h % n == 0)
    ysh, states = _rwkv_scan_fwd(*scan_in, hb)
    ys = uh(ysh)
    ycat = _rwkv_post_fwd(ys, r, km, v, p, c, w["ln_w"], w["ln_b"], w["r_k"])

    cum_t = _fox_prep_t(p, c, w["b_f"])
    yfox, ycat, lse = _fox2_fwd(p, c, cum_t, FOX_BLOCK, ycat)

    memn, rinv_m = _rms_fwd(mem, w["g_mem"], "rms_mem")
    mkv = _matmul(memn, w["w_mem_kv"], name="mem_kv")
    ymem = _mem_attn_fwd(p, c, mkv)
    ycat = _gate_fwd(ymem, p, c.o_gmq, "gate_mem", ycat, 2 * rw)
    yo =_matmul(ycat, w["w_out"], name="out_proj", tn=512, tk=4096)
    loss, dout, dyo, dg_post = _post_loss(yo, x, tgt, w["g_post"], "post_loss")

    dyc = _matmul(dyo, w["w_out"], tb=True, name="d_ycat", tn=512, tk=4096)
    dw_out = _matmul(ycat, dyo, ta=True, name="d_w_out", tn=512, tk=4096, out_dtype=BF16)
    dys, dr2, dkm2, dv2, dp, dln_w, dln_b, dr_k = _rwkv_post_bwd(
        dyc, ys, r, km, v, p, c, w["ln_w"], w["ln_b"], w["r_k"])
    dyf, dp = _gate_bwd(dyc, rw, yfox, p, c.o_gfox, "gate_fox_bwd", dp)
    dym, dp = _gate_bwd(dyc, 2 * rw, ymem, p, c.o_gmq, "gate_mem_bwd", dp)

    scan_g = _rwkv_scan_bwd(*scan_in, states, hd(dys), hb)
    dps, dzw, dza, twb, alb, dw0, da0, dk_k, dk_a = _rwkv_pre_bwd(
        p, c, *vecs, *(uh(z) for z in scan_g), dr2, dkm2, dv2)
    dwd = _matmul(twb, dzw, ta=True, name="d_w_decay", out_dtype=BF16)
    dwi = _matmul(alb, dza, ta=True, name="d_w_iclr", out_dtype=BF16)
    dp, dmu = _shift_bwd(dps, p, c, w["mu"], dp)

    dp, dcq = _fox2_bwd_dq(p, c, cum_t, lse, yfox, dyf, FOX_BLOCK, dp)
    dp, dfv, dck = _fox2_bwd_dkv(p, c, cum_t, lse, yfox, dyf, FOX_BLOCK, dp)
    dcum = jnp.pad(jnp.stack([dcq.transpose(1, 0, 2).reshape(t, c.h), dck.reshape(c.h, t).T]),
                   ((0, 0), (0, 0), (0, LANES - c.h)))
    dfl, db_f = _fox_logit_bwd(dcum, p, c, w["b_f"])

    dmq, dmk, dmv = _mem_attn_bwd(p, c, mkv, dym)
    dmkv = jnp.concatenate([dmk, dmv], axis=1)
    dw_mkv = _matmul(memn, dmkv, ta=True, name="d_w_mem_kv", out_dtype=BF16)
    dmemn = _matmul(dmkv, w["w_mem_kv"], tb=True, name="d_memn")
    _, dg_mem = _rms_bwd(dmemn, mem, rinv_m, w["g_mem"], jnp.zeros_like(mem), "rms_mem_bwd")

    for off, piece in ((c.o_fv, dfv), (c.o_mq, dmq), (c.o_fl, dfl)):
        dp = lax.dynamic_update_slice(dp, piece, (0, off))
    rest = dict(wd=dwd, wi=dwi, w_mem_kv=dw_mkv, w_out=dw_out)
    if "d_w_in" in riders:
        dwp, carried["rest"] = _matmul(dp, h, ta=True, name="d_w_in", tk=4096, out_dtype=BF16,
                                       attach=riders["d_w_in"](rest))
    else:
        dwp = _matmul(dp, h, ta=True, name="d_w_in", tk=4096, out_dtype=BF16)
    if "d_h" in riders:
        dh, carried["wp"] = _matmul(dp, w["wp"], tb=True, name="d_h", tk=2944, attach=riders["d_h"](dwp))
    else:
        dh = _matmul(dp, w["wp"], tb=True, name="d_h", tk=2944)
    grad_x, dg_pre = _rms_bwd(dh, x, rinv, w["g_pre"], dout, "rms_pre_bwd")

    small = dict(g_pre=dg_pre, mu=dmu, w0=dw0, a0=da0, k_k=dk_k, k_a=dk_a, r_k=dr_k, ln_w=dln_w, ln_b=dln_b,
                 b_f=db_f, g_mem=dg_mem, g_post=dg_post)
    return loss, grad_x, dict(wp=dwp, **rest), small, carried


CHIPS = ((1, 0, 0), (0, 1, 0), (1, 1, 0))
SIBLING = ((0, 0, 1),)
ALL_PEERS = tuple((i, j, k) for i in (0, 1) for j in (0, 1) for k in (0, 1))[1:]


def _chip_of(pos):
    return 2 * pos[0] + pos[1]


DMA_CHUNK = 4 << 20


def _pieces(shape, itemsize):
    lead, (rows, cols) = shape[:-2], shape[-2:]
    k = 1
    if rows % 16 == 0:
        k = max(1, min(rows // 16, -(-rows * cols * itemsize // DMA_CHUNK)))
        while rows % k or (rows // k) % 16:
            k -= 1
    band = rows // k
    idxs = [()]
    for n in lead:
        idxs = [i + (j,) for i in idxs for j in range(n)]
    return [i + (pl.ds(j * band, band),) for i in idxs for j in range(k)]


def _peer_of(me, mask):
    return tuple(1 - v if f else v for v, f in zip(me, mask))


def _exchange(name, groups):
    n = len(groups)
    plan = _plan(groups)

    def body(*refs):
        copies = _copies(groups, plan, refs[:n], refs[n:2 * n], refs[2 * n], refs[2 * n + 1])
        for cp in copies:
            cp.start()
        for cp in copies:
            cp.wait()

    any_spec = pl.BlockSpec(memory_space=pl.ANY)
    return pl.pallas_call(
        body, name=name,
        in_specs=[any_spec] * n, out_specs=[any_spec] * n,
        out_shape=_exchange_shapes(groups),
        input_output_aliases={gi: gi for gi, g in enumerate(groups) if g.get("inplace")},
        scratch_shapes=[pltpu.SemaphoreType.DMA((len(plan),)), pltpu.SemaphoreType.DMA((len(plan),))],
    )(*[g["src"] for g in groups])


def _plan(groups):
    return [(gi, ti, idx) for gi, g in enumerate(groups) for ti in range(len(g["transfers"]))
            for idx in _pieces(tuple(g["piece"]), g["src"].dtype.itemsize)]


def _exchange_shapes(groups):
    lead = lambda s: tuple(s) if isinstance(s, tuple) else (s,)
    return [jax.ShapeDtypeStruct(lead(g["slots"]) + tuple(g["piece"]), g["src"].dtype) for g in groups]


def _copies(groups, plan, srcs, outs, send_sems, recv_sems):
    me = (lax.axis_index("x"), lax.axis_index("y"), lax.axis_index("c"))
    copies = []
    for k, (gi, ti, idx) in enumerate(plan):
        mask, view, slot = groups[gi]["transfers"][ti]
        peer = _peer_of(me, mask)
        copies.append(pltpu.make_async_remote_copy(
            src_ref=view(srcs[gi], me, peer).at[idx], dst_ref=outs[gi].at[slot(me, peer)].at[idx],
            send_sem=send_sems.at[k], recv_sem=recv_sems.at[k],
            device_id=peer, device_id_type=MESH))
    return copies


def _my_chip():
    return 2 * lax.axis_index("x") + lax.axis_index("y")


def _put(buf, block, slot):
    return lax.dynamic_update_slice(buf, block[None], (slot,) + (0,) * block.ndim)


def _sum_slots(recv, own, k, out_dtype, name):
    s, rows, cols = recv.shape
    budget = max(16, ((4 << 20) // ((s + 1) * cols * 4)) // 16 * 16)
    tr = _tile(rows, budget, 16)
    own_many = own.shape[0] > 1

    def body(k_ref, *refs):
        out_ref = refs[s + 1]
        mine = refs[s][0].astype(F32)
        acc = None
        for i in range(s):
            term = jnp.where(k_ref[0] == i, mine, refs[i][0].astype(F32))
            acc = term if acc is None else acc + term
        out_ref[...] = acc.astype(out_ref.dtype)

    def slot_spec(i):
        return pl.BlockSpec((1, tr, cols), lambda j, kr: (jnp.where(kr[0] == i, (i + 1) % s, i), j, 0))

    grid_spec = pltpu.PrefetchScalarGridSpec(
        num_scalar_prefetch=1, grid=(rows // tr,),
        in_specs=[slot_spec(i) for i in range(s)]
                 + [pl.BlockSpec((1, tr, cols), lambda j, kr: (kr[0] if own_many else 0, j, 0))],
        out_specs=pl.BlockSpec((tr, cols), lambda j, kr: (j, 0)))
    return pl.pallas_call(
        body, name=name, grid_spec=grid_spec,
        out_shape=jax.ShapeDtypeStruct((rows, cols), out_dtype),
        compiler_params=_params("parallel"),
    )(k, *([recv] * s), own)


def _all_gather(shards):
    return _gather_finish(shards, _exchange("gather_chips", _gather_groups(shards)), "gather_pair")


def _gather_groups(shards):
    halves = [s.reshape(2, s.shape[0] // 2, s.shape[1]) for s in shards]
    return [dict(src=q, slots=(4, 2), piece=q.shape[1:],
                 transfers=[(m, lambda ref, me, peer: ref.at[me[2]], lambda me, peer: (_chip_of(me), me[2]))
                            for m in CHIPS])
            for q in halves]


def _gather_finish(shards, first, name):
    spot = lambda m: (lambda me: (_chip_of(_peer_of(me, m)), me[2]))
    both = _exchange(name, [
        dict(src=q, slots=(4, 2), piece=q.shape[2:], inplace=True,
             transfers=[(SIBLING[0], (lambda f: lambda ref, me, peer: ref.at[f(me)])(spot(m)),
                         (lambda f: lambda me, peer: f(me))(spot(m))) for m in CHIPS])
        for q in first])
    return [_put(q.reshape((4,) + s.shape), s, _my_chip()) for s, q in zip(shards, both)]


def _reduce_pair(partials, tag):
    core1 = lax.axis_index("c").reshape(1).astype(jnp.int32)
    halves = [q.reshape(4, 2, q.shape[1] // 2, q.shape[2]).transpose(1, 0, 2, 3) for q in partials]
    pair = _exchange("reduce_pair_" + tag, [
        dict(src=q, slots=2, piece=q.shape[1:],
             transfers=[(SIBLING[0], lambda ref, me, peer: ref.at[peer[2]], lambda me, peer: me[2])])
        for q in halves])
    flat = lambda e: e.reshape(2, -1, e.shape[-1])
    return [_sum_slots(flat(e), flat(q), core1, BF16, "reduce_pair_sum_" + tag).reshape(q.shape[1:])
            for e, q in zip(pair, halves)]


def _reduce_chips_groups(chip_sums):
    return [dict(src=q, slots=4, piece=q.shape[1:],
                 transfers=[(m, lambda ref, me, peer: ref.at[_chip_of(peer)], lambda me, peer: _chip_of(me))
                            for m in CHIPS])
            for q in chip_sums]


def _reduce_finish(crossed, chip_sums, tag):
    core = lax.axis_index("c")
    chip1 = _my_chip().reshape(1).astype(jnp.int32)
    sums = [_sum_slots(e, q, chip1, F32, "reduce_chips_sum_" + tag) for e, q in zip(crossed, chip_sums)]
    swapped = _exchange("reduce_swap_" + tag, [
        dict(src=q, slots=2, piece=q.shape, transfers=[(SIBLING[0], lambda ref, me, peer: ref, lambda me, peer: me[2])])
        for q in sums])
    return [_put(e, q, core).reshape(-1, e.shape[-1]) for e, q in zip(swapped, sums)]


def _reduce_scatter(partials):
    chip_sums = _reduce_pair(partials, "all")
    return _reduce_finish(_exchange("reduce_chips", _reduce_chips_groups(chip_sums)), chip_sums, "all")


def _all_reduce_small(vec):
    dev = 4 * lax.axis_index("x") + 2 * lax.axis_index("y") + lax.axis_index("c")
    got = _exchange("reduce_small", [
        dict(src=vec, slots=8, piece=vec.shape,
             transfers=[(m, lambda ref, me, peer: ref, lambda me, peer: 4 * me[0] + 2 * me[1] + me[2])
                        for m in ALL_PEERS])])[0]
    return _sum_slots(got, vec[None], dev.reshape(1).astype(jnp.int32), F32, "reduce_small_sum")


SMALL = ("g_pre", "mu", "w0", "a0", "k_k", "k_a", "r_k", "ln_w", "ln_b", "b_f", "g_mem", "g_post")


def _pad_cols(a, n):
    return jnp.pad(a, ((0, 0),) * (a.ndim - 1) + ((0, n - a.shape[-1]),))


def kernel(x, mem, g_pre, w_in, mu_rwkv, w0, w_decay_up, a0, w_iclr_up, k_k, k_a, r_k, ln_x_w, ln_x_b, b_f, g_mem, w_mem_kv, w_out, g_post, loss_target, m_g_pre, m_w_in, m_mu_rwkv, m_w0, m_w_decay_up, m_a0, m_w_iclr_up, m_k_k, m_k_a, m_r_k, m_ln_x_w, m_ln_x_b, m_b_f, m_g_mem, m_w_mem_kv, m_w_out, m_g_post, v_g_pre, v_w_in, v_mu_rwkv, v_w0, v_w_decay_up, v_a0, v_w_iclr_up, v_k_k, v_k_a, v_r_k, v_ln_x_w, v_ln_x_b, v_b_f, v_g_mem, v_w_mem_kv, v_w_out, v_g_post):
    d = x.shape[-1]
    c = Cfg(d)
    ws = w_in.shape[-1]
    wpad = -(-ws // LANES) * LANES
    nh = c.h

    g_in, g_wd, g_wi = _all_gather([
        _pad_cols(w_in[0].astype(BF16), wpad), w_decay_up[0].astype(BF16), w_iclr_up[0].astype(BF16)])
    fl = c.ref_fl
    runs = [(0, fl, 0), (fl, fl + nh, c.o_fl), (fl + nh, c.in_width, fl)]
    pieces = []
    for lo, hi, _ in sorted(runs, key=lambda r: r[2]):
        for s in range(4):
            a, b = max(lo, s * ws), min(hi, (s + 1) * ws)
            if a < b:
                pieces.append(g_in[s, :, a - s * ws:b - s * ws])
    wp = jnp.concatenate(pieces + [jnp.zeros((d, LANES - nh), BF16)], axis=1)
    unshard = lambda g: g.transpose(1, 0, 2).reshape(g.shape[1], -1)
    weights = dict(wp=wp, wd=unshard(g_wd), wi=unshard(g_wi),
                   g_pre=g_pre, mu=mu_rwkv, w0=w0, a0=a0, k_k=k_k, k_a=k_a, r_k=r_k.reshape(1, -1),
                   ln_w=ln_x_w, ln_b=ln_x_b, b_f=_pad_cols(b_f, LANES), g_mem=g_mem, g_post=g_post)
    late_shards = [w_out[0].astype(BF16), w_mem_kv[0].astype(BF16)]

    def late_weights(first):
        g_out, g_mkv = _gather_finish(late_shards, first, "gather_pair_late")
        return dict(w_out=g_out.reshape(-1, d), w_mem_kv=g_mkv.reshape(d, -1))

    by_chip = lambda g: jnp.stack(jnp.split(g, 4, axis=1))
    pair_sums = {}

    def ride_rest(g):
        pair_sums["rest"] = _reduce_pair([g["w_out"].reshape(4, -1, d), g["w_mem_kv"].reshape(4, d // 4, -1),
                                          by_chip(g["wd"]), by_chip(g["wi"])], "rest")
        return _reduce_chips_groups(pair_sums["rest"])

    def ride_wp(dwpt):
        shards = []
        for s in range(4):
            rows = []
            for lo, hi, at in runs:
                a, b = max(lo, s * ws), min(hi, (s + 1) * ws)
                if a < b:
                    rows.append(dwpt[at + a - lo:at + b - lo, :])
            part = rows[0] if len(rows) == 1 else jnp.concatenate(rows, axis=0)
            shards.append(jnp.pad(part, ((0, wpad - ws), (0, 0))))
        pair_sums["wp"] = _reduce_pair([jnp.stack(shards)], "w_in")
        return _reduce_chips_groups(pair_sums["wp"])

    loss, grad_x, _, small, carried = _local_step(
        c, x[0], mem[0], loss_target[0], weights,
        riders={"in_proj": (_gather_groups(late_shards), late_weights), "d_w_in": ride_rest, "d_h": ride_wp})
    red = (_reduce_finish(carried["wp"], pair_sums["wp"], "w_in")
           + _reduce_finish(carried["rest"], pair_sums["rest"], "rest"))
    big_w = (w_in[0].T, w_out[0], w_mem_kv[0], w_decay_up[0], w_iclr_up[0])
    big_m = (m_w_in[0].T, m_w_out[0], m_w_mem_kv[0], m_w_decay_up[0], m_w_iclr_up[0])
    big_v = (v_w_in[0].T, v_w_out[0], v_w_mem_kv[0], v_w_decay_up[0], v_w_iclr_up[0])
    big_names = ("w_in", "w_out", "w_mem_kv", "w_decay_up", "w_iclr_up")
    upd = {n: _adamw(w_, g_, m_, v_, "adamw_" + n) for n, w_, g_, m_, v_ in zip(big_names, big_w, red, big_m, big_v)}
    upd["w_in"] = [o.T for o in upd["w_in"]]

    small_w = dict(g_pre=g_pre, mu=mu_rwkv, w0=w0, a0=a0, k_k=k_k, k_a=k_a, r_k=r_k.reshape(1, -1), ln_w=ln_x_w,
                   ln_b=ln_x_b, b_f=b_f, g_mem=g_mem, g_post=g_post)
    small_m = dict(g_pre=m_g_pre, mu=m_mu_rwkv, w0=m_w0, a0=m_a0, k_k=m_k_k, k_a=m_k_a, r_k=m_r_k.reshape(1, -1),
                   ln_w=m_ln_x_w, ln_b=m_ln_x_b, b_f=m_b_f, g_mem=m_g_mem, g_post=m_g_post)
    small_v = dict(g_pre=v_g_pre, mu=v_mu_rwkv, w0=v_w0, a0=v_a0, k_k=v_k_k, k_a=v_k_a, r_k=v_r_k.reshape(1, -1),
                   ln_w=v_ln_x_w, ln_b=v_ln_x_b, b_f=v_b_f, g_mem=v_g_mem, g_post=v_g_post)
    widths = [-(-small_w[n].shape[1] // LANES) * LANES for n in SMALL]
    pack = lambda t: jnp.concatenate([_pad_cols(t[n], wd_) for n, wd_ in zip(SMALL, widths)]
                                     + [jnp.zeros((1, LANES), F32)], axis=1)
    g_packed = jnp.concatenate([_pad_cols(small[n], wd_) for n, wd_ in zip(SMALL, widths)]
                               + [_pad_cols(loss, LANES)], axis=1)
    g_sum = _all_reduce_small(g_packed)
    s_upd = _adamw(pack(small_w), g_sum, pack(small_m), pack(small_v), "adamw_small")
    offs = [sum(widths[:i]) for i in range(len(SMALL))]

    def take(kind, n):
        i = SMALL.index(n)
        piece = s_upd[kind][:, offs[i]:offs[i] + small_w[n].shape[1]]
        return piece.reshape(r_k.shape) if n == "r_k" else piece

    total_loss = g_sum[0, sum(widths)]
    order = ("g_pre", "w_in", "mu", "w0", "w_decay_up", "a0", "w_iclr_up", "k_k", "k_a", "r_k", "ln_w", "ln_b", "b_f",
             "g_mem", "w_mem_kv", "w_out", "g_post")
    outs = [total_loss, grad_x[None]]
    for kind in range(4):
        for n in order:
            outs.append(upd[n][kind][None] if n in upd else take(kind, n))
    return tuple(outs)
```

```python
import jax
import jax.numpy as jnp
from jax import lax
from jax.experimental import pallas as pl
from jax.experimental.pallas import tpu as pltpu

F32 = jnp.float32
BF16 = jnp.bfloat16
HI = lax.Precision.HIGHEST
MESH = pl.DeviceIdType.MESH

HEAD_DIM = 64
MEM_HEADS = 4
LORA = 128
CHUNK = 64
RMS_EPS = 1e-6
GN_EPS = 64e-5
LANES = 128
VMEM_LIMIT = 56 * 1024 * 1024

ADAM_LR, ADAM_B1, ADAM_B2, ADAM_EPS, ADAM_WD, ADAM_STEP = 0.001, 0.9, 0.999, 1e-08, 0.01, 10


class Cfg:
    def __init__(self, d):
        self.d = d
        self.rw = 3 * d // 8
        self.mw = d // 4
        self.h = self.rw // HEAD_DIM
        self.mhd = self.mw // MEM_HEADS
        self.shift = 3 * self.rw + 2 * LORA
        self.in_width = self.shift + 5 * self.rw + self.h + 2 * self.mw
        o = self.shift
        self.o_grw = o; o += self.rw
        self.o_fq = o; o += self.rw
        self.o_fk = o; o += self.rw
        self.o_fv = o; o += self.rw
        self.o_gfox = o; o += self.rw
        self.o_mq = o; o += self.mw
        self.o_gmq = o; o += self.mw
        self.o_fl = o; o += LANES
        self.wp = o
        self.ref_fl = self.shift + 4 * self.rw


def _tile(n, pref, align=LANES):
    if n <= pref:
        return n
    t = (pref // align) * align
    while t >= align:
        if n % t == 0:
            return t
        t -= align
    return n


def _params(*sem):
    return pltpu.CompilerParams(dimension_semantics=sem, vmem_limit_bytes=VMEM_LIMIT)


def _pallas_into(body, into, n_in, out_index, in_specs, **kw):
    if into is None:
        return pl.pallas_call(body, in_specs=in_specs, **kw)

    def body_with_alias(*refs):
        return body(*refs[:n_in], *refs[n_in + 1:])

    call = pl.pallas_call(body_with_alias, in_specs=list(in_specs) + [pl.BlockSpec(memory_space=pl.ANY)],
                          input_output_aliases={n_in: out_index}, **kw)
    return lambda *args: call(*args, into)


def _sig(x):
    return 1.0 / (1.0 + jnp.exp(-x))


def _softplus(x):
    return jnp.maximum(x, 0.0) + jnp.log(1.0 + jnp.exp(-jnp.abs(x)))


def _dot(a, b, dims, prec=None):
    return lax.dot_general(a, b, (dims, ((), ())), precision=prec, preferred_element_type=F32)


def _mm(a, b, prec=None):
    return _dot(a, b, ((1,), (0,)), prec)


def _mm_nt(a, b, prec=None):
    return _dot(a, b, ((1,), (1,)), prec)


def _mm_tn(a, b, prec=None):
    return _dot(a, b, ((0,), (0,)), prec)


def _split(a):
    hi = a.astype(BF16)
    return hi, (a - hi.astype(F32)).astype(BF16)


def _dot3(a, b, dims, passes=3):
    d = lambda x, y: lax.dot_general(x, y, dims, preferred_element_type=F32)
    if passes == 1:
        return d(a.astype(BF16), b.astype(BF16))
    if passes == 2:
        ah, (bh, bl) = a.astype(BF16), _split(b)
        return d(ah, bh) + d(ah, bl)
    (ah, al), (bh, bl) = _split(a), _split(b)
    return d(ah, bh) + (d(ah, bl) + d(al, bh))


def _bmm(a, b, passes=3):
    return _dot3(a, b, (((2,), (1,)), ((0,), (0,))), passes)


def _bmm_nt(a, b, passes=3):
    return _dot3(a, b, (((2,), (2,)), ((0,), (0,))), passes)


def _bmm_tn(a, b, passes=3):
    return _dot3(a, b, (((1,), (1,)), ((0,), (0,))), passes)


def _bmm_01(m01, x):
    x1 = x.astype(BF16)
    r1 = x - x1.astype(F32)
    x2 = r1.astype(BF16)
    x3 = (r1 - x2.astype(F32)).astype(BF16)
    d = lambda y: lax.dot_general(m01, y, (((2,), (1,)), ((0,), (0,))), preferred_element_type=F32)
    return d(x1) + (d(x2) + d(x3))


def _matmul(a, b, *, ta=False, tb=False, out_dtype=F32, name, tm=1024, tn=1024, tk=1024, attach=None):
    m, k = (a.shape[1], a.shape[0]) if ta else a.shape
    n = b.shape[0] if tb else b.shape[1]
    tm, tn, tk = _tile(m, tm), _tile(n, tn), _tile(k, tk)
    nk = k // tk
    grid = (m // tm, n // tn, nk)
    dims = ((0 if ta else 1,), (1 if tb else 0,))
    groups = attach or []
    ng = len(groups)
    plan = _plan(groups)

    def body(a_ref, b_ref, *rest):
        srcs, o_ref, outs, scratch = rest[:ng], rest[ng], rest[ng + 1:2 * ng + 1], rest[2 * ng + 1:]
        acc = scratch[0] if nk > 1 else None
        if ng:
            copies = _copies(groups, plan, srcs, outs, scratch[-2], scratch[-1])
            ids = [pl.program_id(ax) for ax in range(3)]

            @pl.when((ids[0] == 0) & (ids[1] == 0) & (ids[2] == 0))
            def _():
                for cp in copies:
                    cp.start()

        part = _dot(a_ref[...].astype(BF16), b_ref[...].astype(BF16), dims)
        if nk == 1:
            o_ref[...] = part.astype(o_ref.dtype)
        else:
            kk = pl.program_id(2)

            @pl.when(kk == 0)
            def _():
                acc[...] = part

            @pl.when(kk > 0)
            def _():
                acc[...] += part

            @pl.when(kk == nk - 1)
            def _():
                o_ref[...] = acc[...].astype(o_ref.dtype)

        if ng:
            @pl.when((ids[0] == grid[0] - 1) & (ids[1] == grid[1] - 1) & (ids[2] == grid[2] - 1))
            def _():
                for cp in copies:
                    cp.wait()

    a_spec = pl.BlockSpec((tk, tm), lambda i, j, kk: (kk, i)) if ta else pl.BlockSpec((tm, tk), lambda i, j, kk: (i, kk))
    b_spec = pl.BlockSpec((tn, tk), lambda i, j, kk: (j, kk)) if tb else pl.BlockSpec((tk, tn), lambda i, j, kk: (kk, j))
    any_spec = pl.BlockSpec(memory_space=pl.ANY)
    sems = [pltpu.SemaphoreType.DMA((len(plan),)), pltpu.SemaphoreType.DMA((len(plan),))] if ng else []
    res = pl.pallas_call(
        body, name=name, grid=grid,
        in_specs=[a_spec, b_spec] + [any_spec] * ng,
        out_specs=[pl.BlockSpec((tm, tn), lambda i, j, kk: (i, j))] + [any_spec] * ng,
        out_shape=[jax.ShapeDtypeStruct((m, n), out_dtype)] + _exchange_shapes(groups),
        scratch_shapes=([pltpu.VMEM((tm, tn), F32)] if nk > 1 else []) + sems,
        compiler_params=_params(*(("arbitrary",) * 3 if ng else ("parallel", "parallel", "arbitrary"))),
    )(a, b, *[g["src"] for g in groups])
    return (res[0], list(res[1:])) if ng else res[0]


def _rms_fwd(x, g, name):
    t, d = x.shape
    tm = _tile(t, 256, 8)

    def body(x_ref, g_ref, h_ref, r_ref):
        xv = x_ref[...]
        r = lax.rsqrt(jnp.mean(xv * xv, axis=-1, keepdims=True) + RMS_EPS)
        h_ref[...] = (xv * r * g_ref[...]).astype(BF16)
        r_ref[...] = r

    return pl.pallas_call(
        body, name=name, grid=(t // tm,),
        in_specs=[pl.BlockSpec((tm, d), lambda i: (i, 0)), pl.BlockSpec((1, d), lambda i: (0, 0))],
        out_specs=[pl.BlockSpec((tm, d), lambda i: (i, 0)), pl.BlockSpec((tm, 1), lambda i: (i, 0))],
        out_shape=[jax.ShapeDtypeStruct((t, d), BF16), jax.ShapeDtypeStruct((t, 1), F32)],
        compiler_params=_params("parallel"),
    )(x, g)


def _rms_bwd(dh, x, rinv, g, add, name):
    t, d = x.shape
    tm = _tile(t, 256, 8)

    def body(dh_ref, x_ref, r_ref, g_ref, add_ref, dx_ref, dg_ref):
        @pl.when(pl.program_id(0) == 0)
        def _():
            dg_ref[...] = jnp.zeros_like(dg_ref)

        r = r_ref[...]
        xn = x_ref[...] * r
        dhv = dh_ref[...]
        dg_ref[...] += jnp.sum(dhv * xn, axis=0, keepdims=True)
        dxn = dhv * g_ref[...]
        dx_ref[...] = add_ref[...] + r * (dxn - xn * jnp.mean(dxn * xn, axis=-1, keepdims=True))

    row = pl.BlockSpec((tm, d), lambda i: (i, 0))
    vec = pl.BlockSpec((1, d), lambda i: (0, 0))
    return pl.pallas_call(
        body, name=name, grid=(t // tm,),
        in_specs=[row, row, pl.BlockSpec((tm, 1), lambda i: (i, 0)), vec, row],
        out_specs=[row, vec],
        out_shape=[jax.ShapeDtypeStruct((t, d), F32), jax.ShapeDtypeStruct((1, d), F32)],
        compiler_params=_params("arbitrary"),
    )(dh, x, rinv, g, add)


def _post_loss(yo, x, tgt, g, name):
    t, d = x.shape
    tm = _tile(t, 256, 8)

    def body(yo_ref, x_ref, t_ref, g_ref, loss_ref, dout_ref, dyo_ref, dg_ref):
        @pl.when(pl.program_id(0) == 0)
        def _():
            dg_ref[...] = jnp.zeros_like(dg_ref)
            loss_ref[...] = jnp.zeros_like(loss_ref)

        yv = yo_ref[...]
        r = lax.rsqrt(jnp.mean(yv * yv, axis=-1, keepdims=True) + RMS_EPS)
        n = yv * r
        err = x_ref[...] + n * g_ref[...] - t_ref[...]
        loss_ref[...] += 0.5 * jnp.sum(jnp.mean(err * err, axis=-1, keepdims=True), axis=0, keepdims=True)
        dout = err * (1.0 / d)
        dout_ref[...] = dout
        dg_ref[...] += jnp.sum(dout * n, axis=0, keepdims=True)
        dn = dout * g_ref[...]
        dyo_ref[...] = (r * (dn - n * jnp.mean(dn * n, axis=-1, keepdims=True))).astype(BF16)

    row = pl.BlockSpec((tm, d), lambda i: (i, 0))
    vec = pl.BlockSpec((1, d), lambda i: (0, 0))
    return pl.pallas_call(
        body, name=name, grid=(t // tm,),
        in_specs=[row, row, row, vec],
        out_specs=[pl.BlockSpec((1, 1), lambda i: (0, 0)), row, row, vec],
        out_shape=[jax.ShapeDtypeStruct((1, 1), F32), jax.ShapeDtypeStruct((t, d), F32),
                   jax.ShapeDtypeStruct((t, d), BF16), jax.ShapeDtypeStruct((1, d), F32)],
        compiler_params=_params("arbitrary"),
    )(yo, x, tgt, g)


def _head_sum(x):
    ri = lax.broadcasted_iota(jnp.int32, (LANES, LANES), 0) // HEAD_DIM
    ci = lax.broadcasted_iota(jnp.int32, (LANES, LANES), 1) // HEAD_DIM
    e = (ri == ci).astype(BF16)
    x1 = x.astype(BF16)
    r1 = x - x1.astype(F32)
    x2 = r1.astype(BF16)
    x3 = (r1 - x2.astype(F32)).astype(BF16)
    parts = []
    for i in range(x.shape[1] // LANES):
        sl = slice(i * LANES, (i + 1) * LANES)
        parts.append(_mm(x1[:, sl], e) + (_mm(x2[:, sl], e) + _mm(x3[:, sl], e)))
    return parts[0] if len(parts) == 1 else jnp.concatenate(parts, axis=1)


def _shifted(p_cur, before, first, mu):
    rolled = pltpu.roll(p_cur, 1, 0)
    prev_row = jnp.where(first, 0.0, before)
    row0 = lax.broadcasted_iota(jnp.int32, p_cur.shape, 0) == 0
    prev = jnp.where(row0, prev_row, rolled)
    return p_cur + (prev - p_cur) * mu, prev


def _rwkv_features(ps, rw, w0, a0, k_k, k_a, wd, wi):
    r, k, v = ps[:, 0:rw], ps[:, rw:2 * rw], ps[:, 2 * rw:3 * rw]
    wl, al = ps[:, 3 * rw:3 * rw + LORA], ps[:, 3 * rw + LORA:3 * rw + 2 * LORA]
    tw = jnp.tanh(wl)
    zw = w0 + _mm(tw.astype(BF16), wd)
    logw = -jnp.exp(-_softplus(-zw) - 0.5)
    alpha = _sig(a0 + _mm(al.astype(BF16), wi))
    kkr = k * k_k
    n2 = _head_sum(kkr * kkr)
    rn = lax.rsqrt(jnp.maximum(n2, 1e-24))
    kk = kkr * rn
    kmod = k * (1.0 + (alpha - 1.0) * k_a)
    return dict(r=r, k=k, v=v, tw=tw, al=al, zw=zw, logw=logw, alpha=alpha, kk=kk, rn=rn, n2=n2, kmod=kmod)


def _rwkv_pre_fwd(p, c, mu, w0, a0, k_k, k_a, wd, wi):
    t = p.shape[0]
    tm = _tile(t, 128, 8)
    rw, sh = c.rw, c.shift

    def body(p_ref, pp_ref, mu_ref, w0_ref, a0_ref, kk_ref, ka_ref, wd_ref, wi_ref,
             r_ref, lw_ref, km_ref, v_ref, a_ref, b_ref):
        ps, _ = _shifted(p_ref[...], pp_ref[7:8, :], pl.program_id(0) == 0, mu_ref[...])
        f = _rwkv_features(ps, rw, w0_ref[...], a0_ref[...], kk_ref[...], ka_ref[...], wd_ref[...], wi_ref[...])
        r_ref[...] = f["r"]
        lw_ref[...] = f["logw"]
        km_ref[...] = f["kmod"]
        v_ref[...] = f["v"]
        a_ref[...] = -f["kk"]
        b_ref[...] = f["kk"] * f["alpha"]

    vec = lambda n: pl.BlockSpec((1, n), lambda i: (0, 0))
    out = pl.BlockSpec((tm, rw), lambda i: (i, 0))
    return pl.pallas_call(
        body, name="rwkv_pre_fwd", grid=(t // tm,),
        in_specs=[pl.BlockSpec((tm, sh), lambda i: (i, 0)),
                  pl.BlockSpec((8, sh), lambda i: (jnp.maximum(i * (tm // 8) - 1, 0), 0)),
                  vec(sh), vec(rw), vec(rw), vec(rw), vec(rw),
                  pl.BlockSpec((LORA, rw), lambda i: (0, 0)), pl.BlockSpec((LORA, rw), lambda i: (0, 0))],
        out_specs=[out] * 6,
        out_shape=[jax.ShapeDtypeStruct((t, rw), F32)] * 6,
        compiler_params=_params("parallel"),
    )(p, p, mu, w0, a0, k_k, k_a, wd, wi)


def _rwkv_pre_bwd(p, c, mu, w0, a0, k_k, k_a, wd, wi, dr, dlw, dkm, dv, da, db, dr2, dkm2, dv2):
    t = p.shape[0]
    tm = _tile(t, 128, 8)
    rw, sh = c.rw, c.shift

    def body(p_ref, pp_ref, mu_ref, w0_ref, a0_ref, kk_ref, ka_ref, wd_ref, wi_ref,
             dr_ref, dlw_ref, dkm_ref, dv_ref, da_ref, db_ref, dr2_ref, dkm2_ref, dv2_ref,
             dps_ref, dzw_ref, dza_ref, tw_ref, al_ref, dw0_ref, da0_ref, dkk_ref, dka_ref):
        @pl.when(pl.program_id(0) == 0)
        def _():
            for ref in (dw0_ref, da0_ref, dkk_ref, dka_ref):
                ref[...] = jnp.zeros_like(ref)

        ps, _ = _shifted(p_ref[...], pp_ref[7:8, :], pl.program_id(0) == 0, mu_ref[...])
        k_k, k_a = kk_ref[...], ka_ref[...]
        f = _rwkv_features(ps, rw, w0_ref[...], a0_ref[...], k_k, k_a, wd_ref[...], wi_ref[...])
        alpha, kk, k = f["alpha"], f["kk"], f["k"]
        dkm = dkm_ref[...] + dkm2_ref[...]
        db = db_ref[...]
        dkk = db * alpha - da_ref[...]
        dalpha = db * kk + dkm * k * k_a
        dk = dkm * (1.0 + (alpha - 1.0) * k_a)
        dka_ref[...] += jnp.sum(dkm * k * (alpha - 1.0), axis=0, keepdims=True)
        dkkr = f["rn"] * jnp.where(f["n2"] > 1e-24, dkk - kk * _head_sum(dkk * kk), dkk)
        dk = dk + dkkr * k_k
        dkk_ref[...] += jnp.sum(dkkr * k, axis=0, keepdims=True)
        dza = dalpha * alpha * (1.0 - alpha)
        da0_ref[...] += jnp.sum(dza, axis=0, keepdims=True)
        dzw = dlw_ref[...] * f["logw"] * _sig(-f["zw"])
        dw0_ref[...] += jnp.sum(dzw, axis=0, keepdims=True)
        dza_b, dzw_b = dza.astype(BF16), dzw.astype(BF16)
        dal = _mm_nt(dza_b, wi_ref[...])
        dwl = _mm_nt(dzw_b, wd_ref[...]) * (1.0 - f["tw"] * f["tw"])
        dps_ref[:, 0:rw] = dr_ref[...] + dr2_ref[...]
        dps_ref[:, rw:2 * rw] = dk
        dps_ref[:, 2 * rw:3 * rw] = dv_ref[...] + dv2_ref[...]
        dps_ref[:, 3 * rw:3 * rw + LORA] = dwl
        dps_ref[:, 3 * rw + LORA:sh] = dal
        dzw_ref[...] = dzw_b
        dza_ref[...] = dza_b
        tw_ref[...] = f["tw"].astype(BF16)
        al_ref[...] = f["al"].astype(BF16)

    vec = lambda n: pl.BlockSpec((1, n), lambda i: (0, 0))
    blk = lambda n: pl.BlockSpec((tm, n), lambda i: (i, 0))
    return pl.pallas_call(
        body, name="rwkv_pre_bwd", grid=(t // tm,),
        in_specs=[blk(sh), pl.BlockSpec((8, sh), lambda i: (jnp.maximum(i * (tm // 8) - 1, 0), 0)),
                  vec(sh), vec(rw), vec(rw), vec(rw), vec(rw),
                  pl.BlockSpec((LORA, rw), lambda i: (0, 0)), pl.BlockSpec((LORA, rw), lambda i: (0, 0))]
                 + [blk(rw)] * 9,
        out_specs=[blk(sh), blk(rw), blk(rw), blk(LORA), blk(LORA), vec(rw), vec(rw), vec(rw), vec(rw)],
        out_shape=[jax.ShapeDtypeStruct((t, sh), F32), jax.ShapeDtypeStruct((t, rw), BF16),
                   jax.ShapeDtypeStruct((t, rw), BF16), jax.ShapeDtypeStruct((t, LORA), BF16),
                   jax.ShapeDtypeStruct((t, LORA), BF16)] + [jax.ShapeDtypeStruct((1, rw), F32)] * 4,
        compiler_params=_params("arbitrary"),
    )(p, p, mu, w0, a0, k_k, k_a, wd, wi, dr, dlw, dkm, dv, da, db, dr2, dkm2, dv2)


def _shift_bwd(dps, p, c, mu, dp):
    t = p.shape[0]
    tm = _tile(t, 256, 8)
    sh = c.shift
    nt = t // tm

    def body(d_ref, dn_ref, p_ref, pp_ref, mu_ref, dp_ref, dmu_ref):
        i = pl.program_id(0)

        @pl.when(i == 0)
        def _():
            dmu_ref[...] = jnp.zeros_like(dmu_ref)

        mu = mu_ref[...]
        d = d_ref[...]
        pc = p_ref[...]
        _, prev = _shifted(pc, pp_ref[7:8, :], i == 0, mu)
        dmu_ref[...] += jnp.sum(d * (prev - pc), axis=0, keepdims=True)
        nxt_row = jnp.where(i == nt - 1, 0.0, dn_ref[0:1, :])
        last = lax.broadcasted_iota(jnp.int32, d.shape, 0) == tm - 1
        nxt = jnp.where(last, nxt_row, pltpu.roll(d, tm - 1, 0))
        dp_ref[...] = (d * (1.0 - mu) + nxt * mu).astype(BF16)

    blk = pl.BlockSpec((tm, sh), lambda i: (i, 0))
    return _pallas_into(
        body, dp, 5, 0, name="shift_bwd", grid=(nt,),
        in_specs=[blk, pl.BlockSpec((8, sh), lambda i: (jnp.minimum((i + 1) * (tm // 8), t // 8 - 1), 0)),
                  blk, pl.BlockSpec((8, sh), lambda i: (jnp.maximum(i * (tm // 8) - 1, 0), 0)),
                  pl.BlockSpec((1, sh), lambda i: (0, 0))],
        out_specs=[blk, pl.BlockSpec((1, sh), lambda i: (0, 0))],
        out_shape=[jax.ShapeDtypeStruct((t, c.wp), BF16), jax.ShapeDtypeStruct((1, sh), F32)],
        compiler_params=_params("arbitrary"),
    )(dps, dps, p, p, mu)


def _tri(n, strict):
    ri = lax.broadcasted_iota(jnp.int32, (n, n), 0)
    ci = lax.broadcasted_iota(jnp.int32, (n, n), 1)
    return (ri > ci) if strict else (ri >= ci)


def _unit_lower_inverse(a):
    n = a.shape[-1]
    ri = lax.broadcasted_iota(jnp.int32, (n, n), 0)
    ci = lax.broadcasted_iota(jnp.int32, (n, n), 1)
    eye = (ri == ci).astype(F32)
    blk = lambda s: (ri // s) == (ci // s)
    ad = jnp.where(blk(16), a, 0.0)
    p = eye + ad
    for _ in range(3):
        ad = _bmm(ad, ad, P_SOLVE)
        p = p + _bmm(p, ad, P_SOLVE)
    s = 16
    while s < n:
        off = jnp.where(blk(2 * s) & ~blk(s), a, 0.0)
        p = p + _bmm(_bmm(p, off, P_SOLVE), p, P_SOLVE)
        s *= 2
    return p


P_SOLVE, P_STATE, P_OUT, P_GRAD, P_DECAY = 1, 1, 1, 1, 2


def _chunk_common(r, lw, k, a, b):
    n = r.shape[1]
    tri_incl = jnp.broadcast_to(_tri(n, False).astype(BF16), (r.shape[0], n, n))
    cum = _bmm_01(tri_incl, lw)
    e_pos, e_neg, e_exc = jnp.exp(cum), jnp.exp(-cum), jnp.exp(cum - lw)
    last = lax.broadcasted_iota(jnp.int32, (n, r.shape[2]), 0) == n - 1
    g_last = jnp.exp(jnp.sum(jnp.where(last, cum, 0.0), axis=1, keepdims=True))
    return g_last, r * e_pos, a * e_exc, b * e_neg, k * e_neg, e_pos, e_neg, e_exc


def _chunk_solve(rt, at, bt, kt, v, g0):
    strict, incl = _tri(rt.shape[1], True), _tri(rt.shape[1], False)
    a_ab = jnp.where(strict, _bmm_nt(at, bt, P_SOLVE), 0.0)
    a_ak = jnp.where(strict, _bmm_nt(at, kt, P_SOLVE), 0.0)
    a_rb = jnp.where(incl, _bmm_nt(rt, bt, P_OUT), 0.0)
    a_rk = jnp.where(incl, _bmm_nt(rt, kt, P_OUT), 0.0)
    tinv = _unit_lower_inverse(a_ab)
    u = _bmm(tinv, _bmm(at, g0, P_SOLVE) + _bmm(a_ak, v, P_SOLVE), P_SOLVE)
    return a_ab, a_ak, a_rb, a_rk, tinv, u


def _diag_col(row, n):
    ri = lax.broadcasted_iota(jnp.int32, (n, n), 0)
    ci = lax.broadcasted_iota(jnp.int32, (n, n), 1)
    return jnp.sum(jnp.where(ri == ci, row, 0.0), axis=2, keepdims=True)


def _diag_row(col, n):
    ri = lax.broadcasted_iota(jnp.int32, (n, n), 0)
    ci = lax.broadcasted_iota(jnp.int32, (n, n), 1)
    return jnp.sum(jnp.where(ri == ci, col, 0.0), axis=1, keepdims=True)


def _rwkv_scan_fwd(r, lw, k, v, a, b, hb):
    h, t, n = r.shape
    nc = t // CHUNK

    def body(r_ref, lw_ref, k_ref, v_ref, a_ref, b_ref, y_ref, st_ref, g_sc):
        @pl.when(pl.program_id(1) == 0)
        def _():
            g_sc[...] = jnp.zeros_like(g_sc)

        g0 = g_sc[...]
        st_ref[0] = g0
        vv = v_ref[...]
        g_last, rt, at, bt, kt, _, _, _ = _chunk_common(r_ref[...], lw_ref[...], k_ref[...], a_ref[...], b_ref[...])
        _, _, a_rb, a_rk, _, u = _chunk_solve(rt, at, bt, kt, vv, g0)
        y_ref[...] = _bmm(rt, g0, P_OUT) + _bmm(a_rb, u, P_OUT) + _bmm(a_rk, vv, P_OUT)
        z = g0 + _bmm_tn(bt, u, P_STATE) + _bmm_tn(kt, vv, P_STATE)
        g_sc[...] = _diag_col(g_last, n) * z

    blk = pl.BlockSpec((hb, CHUNK, n), lambda i, j: (i, j, 0))
    return pl.pallas_call(
        body, name="rwkv_scan_fwd", grid=(h // hb, nc),
        in_specs=[blk] * 6,
        out_specs=[blk, pl.BlockSpec((1, hb, n, n), lambda i, j: (j, i, 0, 0))],
        out_shape=[jax.ShapeDtypeStruct((h, t, n), F32), jax.ShapeDtypeStruct((nc, h, n, n), F32)],
        scratch_shapes=[pltpu.VMEM((hb, n, n), F32)],
        compiler_params=_params("parallel", "arbitrary"),
    )(r, lw, k, v, a, b)


def _rwkv_scan_bwd(r, lw, k, v, a, b, states, dy, hb):
    h, t, n = r.shape
    nc = t // CHUNK

    def body(r_ref, lw_ref, k_ref, v_ref, a_ref, b_ref, st_ref, dy_ref,
             dr_ref, dlw_ref, dk_ref, dv_ref, da_ref, db_ref, dg_sc):
        @pl.when(pl.program_id(1) == 0)
        def _():
            dg_sc[...] = jnp.zeros_like(dg_sc)

        g0 = st_ref[0]
        vv, dyv, dh = v_ref[...], dy_ref[...], dg_sc[...]
        lwv = lw_ref[...]
        g_last, rt, at, bt, kt, e_pos, e_neg, e_exc = _chunk_common(r_ref[...], lwv, k_ref[...], a_ref[...], b_ref[...])
        a_ab, a_ak, a_rb, a_rk, tinv, u = _chunk_solve(rt, at, bt, kt, vv, g0)
        strict, incl = _tri(CHUNK, True), _tri(CHUNK, False)
        gcol = _diag_col(g_last, n)
        z = g0 + _bmm_tn(bt, u, P_STATE) + _bmm_tn(kt, vv, P_STATE)
        dz = gcol * dh
        dc_last = _diag_row(jnp.sum(dh * gcol * z, axis=2, keepdims=True), n)
        g = P_GRAD
        du = _bmm_tn(a_rb, dyv, g) + _bmm(bt, dz, g)
        dx = _bmm_tn(tinv, du, P_SOLVE)
        dv_ref[...] = _bmm_tn(a_rk, dyv, g) + _bmm(kt, dz, g) + _bmm_tn(a_ak, dx, g)
        da_ab = jnp.where(strict, _bmm_nt(dx, u, g), 0.0)
        da_ak = jnp.where(strict, _bmm_nt(dx, vv, g), 0.0)
        da_rb = jnp.where(incl, _bmm_nt(dyv, u, g), 0.0)
        da_rk = jnp.where(incl, _bmm_nt(dyv, vv, g), 0.0)
        g = P_DECAY
        d_at = _bmm(da_ab, bt, g) + _bmm(da_ak, kt, g) + _bmm_nt(dx, g0, g)
        d_rt = _bmm(da_rb, bt, g) + _bmm(da_rk, kt, g) + _bmm_nt(dyv, g0, g)
        d_bt = _bmm_tn(da_ab, at, g) + _bmm_tn(da_rb, rt, g) + _bmm_nt(u, dz, g)
        d_kt = _bmm_tn(da_ak, at, g) + _bmm_tn(da_rk, rt, g) + _bmm_nt(vv, dz, g)
        dg_sc[...] = dz + _bmm_tn(rt, dyv, P_STATE) + _bmm_tn(at, dx, P_STATE)
        dr_ref[...] = d_rt * e_pos
        da_ref[...] = d_at * e_exc
        db_ref[...] = d_bt * e_neg
        dk_ref[...] = d_kt * e_neg
        last = lax.broadcasted_iota(jnp.int32, (CHUNK, n), 0) == CHUNK - 1
        dc = d_rt * rt - d_bt * bt - d_kt * kt + jnp.where(last, dc_last, 0.0)
        dce = d_at * at
        ri = lax.broadcasted_iota(jnp.int32, (CHUNK, CHUNK), 0)
        ci = lax.broadcasted_iota(jnp.int32, (CHUNK, CHUNK), 1)
        up_incl = jnp.broadcast_to((ri <= ci).astype(BF16), (hb, CHUNK, CHUNK))
        dlw_ref[...] = _bmm_01(up_incl, dc + dce) - dce

    rev = lambda i, j: (i, nc - 1 - j, 0)
    blk = pl.BlockSpec((hb, CHUNK, n), rev)
    return pl.pallas_call(
        body, name="rwkv_scan_bwd", grid=(h // hb, nc),
        in_specs=[blk] * 6 + [pl.BlockSpec((1, hb, n, n), lambda i, j: (nc - 1 - j, i, 0, 0)), blk],
        out_specs=[blk] * 6,
        out_shape=[jax.ShapeDtypeStruct((h, t, n), F32)] * 6,
        scratch_shapes=[pltpu.VMEM((hb, n, n), F32)],
        compiler_params=_params("parallel", "arbitrary"),
    )(r, lw, k, v, a, b, states, dy)


def _silu_grad(g):
    s = _sig(g)
    return s * (1.0 + g * (1.0 - s))


def _group_norm(ys):
    yc = ys - _head_sum(ys) * (1.0 / HEAD_DIM)
    rstd = lax.rsqrt(_head_sum(yc * yc) * (1.0 / HEAD_DIM) + GN_EPS)
    return yc * rstd, rstd


def _rwkv_post_fwd(ys, r, km, v, p, c, ln_w, ln_b, r_k):
    t = ys.shape[0]
    tm = _tile(t, 512, 8)
    goff = c.o_grw // LANES

    def body(ys_ref, r_ref, km_ref, v_ref, g_ref, lw_ref, lb_ref, rk_ref, o_ref):
        yn, _ = _group_norm(ys_ref[...])
        s = _head_sum(r_ref[...] * km_ref[...] * rk_ref[...])
        g = g_ref[...]
        o_ref[...] = ((yn * lw_ref[...] + lb_ref[...] + s * v_ref[...]) * g * _sig(g)).astype(BF16)

    blk = pl.BlockSpec((tm, LANES), lambda i, j: (i, j))
    vec = pl.BlockSpec((1, LANES), lambda i, j: (0, j))
    return pl.pallas_call(
        body, name="rwkv_post_fwd", grid=(t // tm, c.rw // LANES),
        in_specs=[blk] * 4 + [pl.BlockSpec((tm, LANES), lambda i, j: (i, goff + j)), vec, vec, vec],
        out_specs=blk, out_shape=jax.ShapeDtypeStruct((t, c.d), BF16),
        compiler_params=_params("parallel", "parallel"),
    )(ys, r, km, v, p, ln_w, ln_b, r_k)


def _rwkv_post_bwd(dyc, ys, r, km, v, p, c, ln_w, ln_b, r_k):
    t = ys.shape[0]
    tm = _tile(t, 512, 8)
    goff = c.o_grw // LANES

    def body(dy_ref, ys_ref, r_ref, km_ref, v_ref, g_ref, lw_ref, lb_ref, rk_ref,
             dys_ref, dr_ref, dkm_ref, dv_ref, dg_ref, dlw_ref, dlb_ref, drk_ref):
        @pl.when(pl.program_id(1) == 0)
        def _():
            for ref in (dlw_ref, dlb_ref, drk_ref):
                ref[...] = jnp.zeros_like(ref)

        yn, rstd = _group_norm(ys_ref[...])
        rv, kmv, vv, rk, g = r_ref[...], km_ref[...], v_ref[...], rk_ref[...], g_ref[...]
        s = _head_sum(rv * kmv * rk)
        y = yn * lw_ref[...] + lb_ref[...] + s * vv
        dyc = dy_ref[...]
        dg_ref[...] = (dyc * y * _silu_grad(g)).astype(BF16)
        dy = dyc * g * _sig(g)
        dlb_ref[...] += jnp.sum(dy, axis=0, keepdims=True)
        dlw_ref[...] += jnp.sum(dy * yn, axis=0, keepdims=True)
        dyn = dy * lw_ref[...]
        inv = 1.0 / HEAD_DIM
        dys_ref[...] = rstd * (dyn - _head_sum(dyn) * inv - yn * _head_sum(dyn * yn) * inv)
        ds = _head_sum(dy * vv)
        dv_ref[...] = dy * s
        dr_ref[...] = ds * kmv * rk
        dkm_ref[...] = ds * rv * rk
        drk_ref[...] += jnp.sum(ds * rv * kmv, axis=0, keepdims=True)

    blk = pl.BlockSpec((tm, LANES), lambda j, i: (i, j))
    vec = pl.BlockSpec((1, LANES), lambda j, i: (0, j))
    f = jax.ShapeDtypeStruct((t, c.rw), F32)
    s1 = jax.ShapeDtypeStruct((1, c.rw), F32)
    gate = pl.BlockSpec((tm, LANES), lambda j, i: (i, goff + j))
    return pl.pallas_call(
        body, name="rwkv_post_bwd", grid=(c.rw // LANES, t // tm),
        in_specs=[blk] * 5 + [gate, vec, vec, vec],
        out_specs=[blk] * 4 + [gate] + [vec] * 3,
        out_shape=[f, f, f, f, jax.ShapeDtypeStruct((t, c.wp), BF16), s1, s1, s1],
        compiler_params=_params("parallel", "arbitrary"),
    )(dyc, ys, r, km, v, p, ln_w, ln_b, r_k)


def _gate_fwd(y, p, goff, name, ycat, yoff):
    t, w = y.shape
    tm = _tile(t, 512, 8)
    gb, ob = goff // LANES, yoff // LANES

    def body(y_ref, g_ref, o_ref):
        g = g_ref[...]
        o_ref[...] = (y_ref[...] * g * _sig(g)).astype(BF16)

    blk = pl.BlockSpec((tm, LANES), lambda i, j: (i, j))
    return _pallas_into(
        body, ycat, 2, 0, name=name, grid=(t // tm, w // LANES),
        in_specs=[blk, pl.BlockSpec((tm, LANES), lambda i, j: (i, gb + j))],
        out_specs=pl.BlockSpec((tm, LANES), lambda i, j: (i, ob + j)),
        out_shape=jax.ShapeDtypeStruct(ycat.shape, BF16),
        compiler_params=_params("parallel", "parallel"),
    )(y, p)


def _gate_bwd(dyc, yoff, y, p, goff, name, dp):
    t, w = y.shape
    tm = _tile(t, 512, 8)
    gb, yb = goff // LANES, yoff // LANES

    def body(d_ref, y_ref, g_ref, dy_ref, dg_ref):
        g, d = g_ref[...], d_ref[...]
        dy_ref[...] = d * g * _sig(g)
        dg_ref[...] = (d * y_ref[...] * _silu_grad(g)).astype(BF16)

    blk = pl.BlockSpec((tm, LANES), lambda i, j: (i, j))
    gate = pl.BlockSpec((tm, LANES), lambda i, j: (i, gb + j))
    return _pallas_into(
        body, dp, 3, 1, name=name, grid=(t // tm, w // LANES),
        in_specs=[pl.BlockSpec((tm, LANES), lambda i, j: (i, yb + j)), blk, gate],
        out_specs=[blk, gate],
        out_shape=[jax.ShapeDtypeStruct((t, w), F32), jax.ShapeDtypeStruct(dp.shape, BF16)],
        compiler_params=_params("parallel", "parallel"),
    )(dyc, y, p)


NEG = -1e30


def _fox_logit_bwd(dcum, p, c, b_f):
    t = p.shape[0]
    tm = _tile(t, 512, 8)
    fb = c.o_fl // LANES
    nt = t // tm

    def body(d_ref, f_ref, b_ref, o_ref, db_ref, carry):
        @pl.when(pl.program_id(0) == 0)
        def _():
            carry[...] = jnp.zeros_like(carry)
            db_ref[...] = jnp.zeros_like(db_ref)

        d = d_ref[0] + d_ref[1]
        dlogf = _mm(_tri(tm, False).astype(F32).T, d, HI) + carry[...]
        carry[...] += jnp.sum(d, axis=0, keepdims=True)
        df = dlogf * _sig(-(f_ref[...] + b_ref[...]))
        o_ref[...] = df.astype(BF16)
        db_ref[...] += jnp.sum(df, axis=0, keepdims=True)

    return pl.pallas_call(
        body, name="fox_logit_bwd", grid=(nt,),
        in_specs=[pl.BlockSpec((2, tm, LANES), lambda i: (0, nt - 1 - i, 0)),
                  pl.BlockSpec((tm, LANES), lambda i: (nt - 1 - i, fb)),
                  pl.BlockSpec((1, LANES), lambda i: (0, 0))],
        out_specs=[pl.BlockSpec((tm, LANES), lambda i: (nt - 1 - i, 0)), pl.BlockSpec((1, LANES), lambda i: (0, 0))],
        out_shape=[jax.ShapeDtypeStruct((t, LANES), BF16), jax.ShapeDtypeStruct((1, LANES), F32)],
        scratch_shapes=[pltpu.VMEM((1, LANES), F32)],
        compiler_params=_params("arbitrary"),
    )(dcum, p, b_f)


FOX_PAIRS = 2
FOX_HEADS_STEP = 2 * FOX_PAIRS


def _lane_half(shape, upper):
    li = lax.broadcasted_iota(jnp.int32, shape, len(shape) - 1)
    return (li >= HEAD_DIM) if upper else (li < HEAD_DIM)


def _col(block, j):
    li = lax.broadcasted_iota(jnp.int32, block.shape, 1)
    return jnp.sum(jnp.where(li == j, block, 0.0), axis=1, keepdims=True)


def _from_cols(cols):
    li = lax.broadcasted_iota(jnp.int32, (cols[0].shape[0], len(cols)), 1)
    out = jnp.zeros(li.shape, F32)
    for j, cj in enumerate(cols):
        out = jnp.where(li == j, cj, out)
    return out


def _from_rows(rows):
    si = lax.broadcasted_iota(jnp.int32, (len(rows), rows[0].shape[1]), 0)
    out = jnp.zeros(si.shape, F32)
    for j, rj in enumerate(rows):
        out = jnp.where(si == j, rj, out)
    return out


def _causal(tq, tk):
    return lax.broadcasted_iota(jnp.int32, (tq, tk), 1) <= lax.broadcasted_iota(jnp.int32, (tq, tk), 0)


def _fox_prep_t(p, c, b_f):
    t = p.shape[0]
    tm = _tile(t, 512, LANES)
    fb = c.o_fl // LANES

    def body(f_ref, b_ref, o_ref, carry):
        @pl.when(pl.program_id(0) == 0)
        def _():
            carry[...] = jnp.zeros_like(carry)

        logf = -_softplus(-(f_ref[...] + b_ref[...]))
        cum = _mm(_tri(tm, False).astype(F32), logf, HI) + carry[...]
        o_ref[...] = cum.T
        carry[...] += jnp.sum(logf, axis=0, keepdims=True)

    return pl.pallas_call(
        body, name="fox_prep", grid=(t // tm,),
        in_specs=[pl.BlockSpec((tm, LANES), lambda i: (i, fb)), pl.BlockSpec((1, LANES), lambda i: (0, 0))],
        out_specs=pl.BlockSpec((LANES, tm), lambda i: (0, i)),
        out_shape=jax.ShapeDtypeStruct((LANES, t), F32),
        scratch_shapes=[pltpu.VMEM((1, LANES), F32)],
        compiler_params=_params("arbitrary"),
    )(p, b_f)


def _fox2_fwd(p, c, cum_t, tb, ycat):
    t = p.shape[0]
    tq = tk = _tile(t, tb, LANES)
    nq = t // tq
    pw, nh = FOX_PAIRS * LANES, FOX_HEADS_STEP
    qb, kb, vb, gb = (o // pw for o in (c.o_fq, c.o_fk, c.o_fv, c.o_gfox))
    scale = HEAD_DIM ** -0.5

    def body(q_ref, k_ref, v_ref, g_ref, ck_ref, o_ref, y_ref, lse_ref, m_sc, l_sc, acc_sc):
        g, qi, ki = pl.program_id(0), pl.program_id(1), pl.program_id(2)

        @pl.when(ki == 0)
        def _():
            m_sc[...] = jnp.full_like(m_sc, NEG)
            l_sc[...] = jnp.zeros_like(l_sc)
            acc_sc[...] = jnp.zeros_like(acc_sc)

        def step(diag):
            ms, ls = [m_sc[h] for h in range(nh)], [l_sc[h] for h in range(nh)]
            accs = [acc_sc[:, pi * LANES:(pi + 1) * LANES] for pi in range(FOX_PAIRS)]
            for pi in range(FOX_PAIRS):
                lanes = slice(pi * LANES, (pi + 1) * LANES)
                q2 = (q_ref[:, lanes] * scale).astype(BF16)
                k2, v2 = k_ref[:, lanes].astype(BF16), v_ref[:, lanes].astype(BF16)
                new_acc = accs[pi]
                for hh in range(2):
                    hi = 2 * pi + hh
                    mk = _lane_half((tq, LANES), hh == 1)
                    s = _mm_nt(jnp.where(mk, q2, jnp.zeros_like(q2)), k2) - ck_ref[pl.ds(g * nh + hi, 1), :]
                    if diag:
                        s = jnp.where(_causal(tq, tk), s, NEG)
                    m_new = jnp.maximum(ms[hi], jnp.max(s, axis=1, keepdims=True))
                    a = jnp.exp(ms[hi] - m_new)
                    e = jnp.exp(s - jnp.concatenate([m_new] * (tk // LANES), axis=1))
                    ls[hi] = a * ls[hi] + jnp.sum(e, axis=1, keepdims=True)
                    ms[hi] = m_new
                    new_acc = jnp.where(mk, a * accs[pi] + _mm(e.astype(BF16), v2), new_acc)
                accs[pi] = new_acc
            for h in range(nh):
                m_sc[h] = ms[h]
                l_sc[h] = ls[h]
            for pi in range(FOX_PAIRS):
                acc_sc[:, pi * LANES:(pi + 1) * LANES] = accs[pi]

        @pl.when(ki < qi)
        def _():
            step(False)

        @pl.when(ki == qi)
        def _():
            step(True)
            li = lax.broadcasted_iota(jnp.int32, (tq, LANES), 1)
            lse = jnp.zeros((tq, LANES), F32)
            for pi in range(FOX_PAIRS):
                lanes = slice(pi * LANES, (pi + 1) * LANES)
                inv = jnp.where(_lane_half((tq, LANES), False), 1.0 / l_sc[2 * pi], 1.0 / l_sc[2 * pi + 1])
                o = acc_sc[:, lanes] * inv
                gate = g_ref[:, lanes]
                o_ref[:, lanes] = o
                y_ref[:, lanes] = (o * gate * _sig(gate)).astype(BF16)
            for h in range(nh):
                lse = jnp.where(li == h, m_sc[h] + jnp.log(l_sc[h]), lse)
            lse_ref[0] = lse

    row = lambda off: pl.BlockSpec((tq, pw), lambda g, i, j: (i, off + g))
    key = lambda off: pl.BlockSpec((tk, pw), lambda g, i, j: (jnp.minimum(i, j), off + g))
    out = pl.BlockSpec((tq, pw), lambda g, i, j: (i, g))
    return _pallas_into(
        body, ycat, 5, 1, name="fox_fwd", grid=(c.rw // pw, nq, nq),
        in_specs=[row(qb), key(kb), key(vb), row(gb),
                  pl.BlockSpec((LANES, tk), lambda g, i, j: (0, jnp.minimum(i, j)))],
        out_specs=[out, row(c.rw // pw), pl.BlockSpec((1, tq, LANES), lambda g, i, j: (g, i, 0))],
        out_shape=[jax.ShapeDtypeStruct((t, c.rw), F32), jax.ShapeDtypeStruct(ycat.shape, BF16),
                   jax.ShapeDtypeStruct((c.rw // pw, t, LANES), F32)],
        scratch_shapes=[pltpu.VMEM((nh, tq, LANES), F32), pltpu.VMEM((nh, tq, LANES), F32),
                        pltpu.VMEM((tq, pw), F32)],
        compiler_params=_params("parallel", "parallel", "arbitrary"),
    )(p, p, p, p, cum_t)


def _fox2_grads(q2, k2, v2, do2, o2, lse_h, ck, mk, diag, tq, tk):
    zero = jnp.zeros_like(q2)
    s = _mm_nt(jnp.where(mk, q2, zero), k2) - ck
    if diag:
        s = jnp.where(_causal(tq, tk), s, NEG)
    wide = lambda col: jnp.concatenate([jnp.broadcast_to(col, (tq, LANES))] * (tk // LANES), axis=1)
    pm = jnp.exp(s - wide(lse_h))
    delta = jnp.sum(jnp.where(mk, do2 * o2, 0.0), axis=1, keepdims=True)
    dob = do2.astype(BF16)
    dp = _mm_nt(jnp.where(mk, dob, zero), v2)
    return pm, pm * (dp - wide(delta)), dob


def _fox2_bwd(p, c, cum_t, lse, o, do, tb, dp):
    t = p.shape[0]
    tq = tk = _tile(t, tb, LANES)
    nq = t // tq
    pw, nh = FOX_PAIRS * LANES, FOX_HEADS_STEP
    ng = c.rw // pw
    qb, kb, vb = (o_ // pw for o_ in (c.o_fq, c.o_fk, c.o_fv))
    scale = HEAD_DIM ** -0.5

    def body(q_ref, k_ref, v_ref, ck_ref, lse_ref, o_ref, do_ref,
             dk_ref, dv_ref, dck_ref, dqp_ref, dcqp_ref, dk_sc, dv_sc, dc_sc):
        g, ki, qi = pl.program_id(0), pl.program_id(1), pl.program_id(2)

        @pl.when(qi == 0)
        def _():
            dk_sc[...] = jnp.zeros_like(dk_sc)
            dv_sc[...] = jnp.zeros_like(dv_sc)
            dc_sc[...] = jnp.zeros_like(dc_sc)

        def step(diag):
            lse_blk = lse_ref[0]
            dcs = [dc_sc[h] for h in range(nh)]
            dks = [dk_sc[:, pi * LANES:(pi + 1) * LANES] for pi in range(FOX_PAIRS)]
            dvs = [dv_sc[:, pi * LANES:(pi + 1) * LANES] for pi in range(FOX_PAIRS)]
            rows = []
            for pi in range(FOX_PAIRS):
                lanes = slice(pi * LANES, (pi + 1) * LANES)
                q2 = (q_ref[:, lanes] * scale).astype(BF16)
                k2, v2 = k_ref[:, lanes].astype(BF16), v_ref[:, lanes].astype(BF16)
                do2, o2 = do_ref[:, lanes], o_ref[:, lanes]
                new_dk, new_dv, dq2 = dks[pi], dvs[pi], None
                for hh in range(2):
                    hi = 2 * pi + hh
                    mk = _lane_half((tk, LANES), hh == 1)
                    pm, ds, dob = _fox2_grads(q2, k2, v2, do2, o2, _col(lse_blk, hi),
                                              ck_ref[pl.ds(g * nh + hi, 1), :], mk, diag, tq, tk)
                    dsb = ds.astype(BF16)
                    dcs[hi] = dcs[hi] - jnp.sum(ds, axis=0, keepdims=True)
                    rows.append(jnp.sum(ds, axis=1, keepdims=True))
                    new_dv = jnp.where(mk, dvs[pi] + _mm_tn(pm.astype(BF16), dob), new_dv)
                    new_dk = jnp.where(mk, dks[pi] + _mm_tn(dsb, q2), new_dk)
                    part = _mm(dsb, k2)
                    dq2 = part if hh == 0 else jnp.where(mk, part, dq2)
                dks[pi], dvs[pi] = new_dk, new_dv
                dqp_ref[0, :, lanes] = dq2 * scale
            dcqp_ref[0, 0] = _from_cols(rows)
            for h in range(nh):
                dc_sc[h] = dcs[h]
            for pi in range(FOX_PAIRS):
                dk_sc[:, pi * LANES:(pi + 1) * LANES] = dks[pi]
                dv_sc[:, pi * LANES:(pi + 1) * LANES] = dvs[pi]

        @pl.when(qi > ki)
        def _():
            step(False)

        @pl.when(qi == ki)
        def _():
            step(True)

        @pl.when(qi == nq - 1)
        def _():
            dk_ref[...] = dk_sc[...].astype(BF16)
            dv_ref[...] = dv_sc[...].astype(BF16)
            dck_ref[0] = _from_rows([dc_sc[h] for h in range(nh)])

    row = lambda off: pl.BlockSpec((tq, pw), lambda g, j, i: (jnp.maximum(i, j), off + g))
    key = lambda off: pl.BlockSpec((tk, pw), lambda g, j, i: (j, off + g))
    return _pallas_into(
        body, dp, 7, 0, name="fox_bwd", grid=(ng, nq, nq),
        in_specs=[row(qb), key(kb), key(vb), pl.BlockSpec((LANES, tk), lambda g, j, i: (0, j)),
                  pl.BlockSpec((1, tq, LANES), lambda g, j, i: (g, jnp.maximum(i, j), 0)), row(0), row(0)],
        out_specs=[key(kb), key(0), pl.BlockSpec((1, nh, tk), lambda g, j, i: (g, 0, j)),
                   pl.BlockSpec((1, tq, pw), lambda g, j, i: (j, jnp.maximum(i, j), g)),
                   pl.BlockSpec((1, 1, tq, nh), lambda g, j, i: (j, g, jnp.maximum(i, j), 0))],
        out_shape=[jax.ShapeDtypeStruct(dp.shape, BF16), jax.ShapeDtypeStruct((t, c.rw), BF16),
                   jax.ShapeDtypeStruct((ng, nh, t), F32), jax.ShapeDtypeStruct((nq, t, c.rw), F32),
                   jax.ShapeDtypeStruct((nq, ng, t, nh), F32)],
        scratch_shapes=[pltpu.VMEM((tk, pw), F32), pltpu.VMEM((tk, pw), F32), pltpu.VMEM((nh, 1, tk), F32)],
        compiler_params=_params("parallel", "parallel", "arbitrary"),
    )(p, p, p, cum_t, lse, o, do)


def _fox2_dq_sum(dq_part, dcq_part, tq):
    nk, t, rw = dq_part.shape
    ng, nh = dcq_part.shape[1], dcq_part.shape[3]

    def body(p_ref, c_ref, dq_ref, dcq_ref, acc, cacc):
        i, j = pl.program_id(0), pl.program_id(1)

        @pl.when(j == 0)
        def _():
            acc[...] = p_ref[0]
            cacc[...] = c_ref[0]

        @pl.when((j > 0) & (j <= i))
        def _():
            acc[...] += p_ref[0]
            cacc[...] += c_ref[0]

        @pl.when(j == nk - 1)
        def _():
            dq_ref[...] = acc[...].astype(BF16)
            dcq_ref[...] = cacc[...]

    return pl.pallas_call(
        body, name="fox_dq_sum", grid=(t // tq, nk),
        in_specs=[pl.BlockSpec((1, tq, rw), lambda i, j: (jnp.minimum(i, j), i, 0)),
                  pl.BlockSpec((1, ng, tq, nh), lambda i, j: (jnp.minimum(i, j), 0, i, 0))],
        out_specs=[pl.BlockSpec((tq, rw), lambda i, j: (i, 0)), pl.BlockSpec((ng, tq, nh), lambda i, j: (0, i, 0))],
        out_shape=[jax.ShapeDtypeStruct((t, rw), BF16), jax.ShapeDtypeStruct((ng, t, nh), F32)],
        scratch_shapes=[pltpu.VMEM((tq, rw), F32), pltpu.VMEM((ng, tq, nh), F32)],
        compiler_params=_params("parallel", "arbitrary"),
    )(dq_part, dcq_part)


def _mem_probs(q, mk, scale):
    s = _mm_nt(q.astype(BF16), mk.astype(BF16)) * scale
    e = jnp.exp(s - jnp.max(s, axis=1, keepdims=True))
    return e / jnp.sum(e, axis=1, keepdims=True)


def _mem_attn_fwd(p, c, mkv):
    t = p.shape[0]
    tm = _tile(t, 512, 8)
    dh = c.mhd
    qb = c.o_mq // dh
    scale = dh ** -0.5

    def body(q_ref, mk_ref, mv_ref, o_ref):
        pm = _mem_probs(q_ref[...], mk_ref[...], scale)
        o_ref[...] = _mm(pm.astype(BF16), mv_ref[...].astype(BF16))

    m = mkv.shape[0]
    return pl.pallas_call(
        body, name="mem_attn_fwd", grid=(t // tm, MEM_HEADS),
        in_specs=[pl.BlockSpec((tm, dh), lambda i, j: (i, qb + j)),
                  pl.BlockSpec((m, dh), lambda i, j: (0, j)),
                  pl.BlockSpec((m, dh), lambda i, j: (0, MEM_HEADS + j))],
        out_specs=pl.BlockSpec((tm, dh), lambda i, j: (i, j)),
        out_shape=jax.ShapeDtypeStruct((t, c.mw), F32),
        compiler_params=_params("parallel", "parallel"),
    )(p, mkv, mkv)


def _mem_attn_bwd(p, c, mkv, do):
    t = p.shape[0]
    tm = _tile(t, 512, 8)
    dh = c.mhd
    qb = c.o_mq // dh
    scale = dh ** -0.5
    m = mkv.shape[0]

    def body(q_ref, mk_ref, mv_ref, do_ref, dq_ref, dmk_ref, dmv_ref):
        @pl.when(pl.program_id(1) == 0)
        def _():
            dmk_ref[...] = jnp.zeros_like(dmk_ref)
            dmv_ref[...] = jnp.zeros_like(dmv_ref)

        qv = q_ref[...].astype(BF16)
        pm = _mem_probs(qv, mk_ref[...], scale)
        dob = do_ref[...].astype(BF16)
        dmv_ref[...] += _mm_tn(pm.astype(BF16), dob)
        dp = _mm_nt(dob, mv_ref[...].astype(BF16))
        ds = (pm * (dp - jnp.sum(pm * dp, axis=1, keepdims=True)) * scale).astype(BF16)
        dq_ref[...] = _mm(ds, mk_ref[...].astype(BF16)).astype(BF16)
        dmk_ref[...] += _mm_tn(ds, qv)

    kvb = lambda off: pl.BlockSpec((m, dh), lambda j, i: (0, off + j))
    return pl.pallas_call(
        body, name="mem_attn_bwd", grid=(MEM_HEADS, t // tm),
        in_specs=[pl.BlockSpec((tm, dh), lambda j, i: (i, qb + j)), kvb(0), kvb(MEM_HEADS),
                  pl.BlockSpec((tm, dh), lambda j, i: (i, j))],
        out_specs=[pl.BlockSpec((tm, dh), lambda j, i: (i, j)), kvb(0), kvb(0)],
        out_shape=[jax.ShapeDtypeStruct((t, c.mw), BF16), jax.ShapeDtypeStruct((m, c.mw), F32),
                   jax.ShapeDtypeStruct((m, c.mw), F32)],
        compiler_params=_params("parallel", "arbitrary"),
    )(p, mkv, mkv, do)


def _adamw(w, g, m, v, name):
    rows, cols = w.shape
    bc1 = 1.0 - ADAM_B1 ** ADAM_STEP
    bc2 = 1.0 - ADAM_B2 ** ADAM_STEP
    if rows % 8 and rows > 8:
        blk = pl.BlockSpec((rows, LANES), lambda i: (0, i))
        g_blk = pl.BlockSpec((g.shape[0], LANES), lambda i: (0, i))
        grid = (cols // LANES,)
    else:
        tm = _tile(rows, max(8, (1 << 18) // cols // 8 * 8), 8)
        blk = pl.BlockSpec((tm, cols), lambda i: (i, 0))
        g_blk = pl.BlockSpec((tm, g.shape[1]), lambda i: (i, 0))
        grid = (rows // tm,)
    brows, bcols = blk.block_shape

    def body(w_ref, g_ref, m_ref, v_ref, go_ref, d_ref, mo_ref, vo_ref):
        gv = g_ref[0:brows, 0:bcols]
        mn = ADAM_B1 * m_ref[...] + (1.0 - ADAM_B1) * gv
        vn = ADAM_B2 * v_ref[...] + (1.0 - ADAM_B2) * (gv * gv)
        go_ref[...] = gv
        mo_ref[...] = mn
        vo_ref[...] = vn
        d_ref[...] = -ADAM_LR * ((mn / bc1) / (jnp.sqrt(vn / bc2) + ADAM_EPS) + ADAM_WD * w_ref[...])

    shp = jax.ShapeDtypeStruct((rows, cols), F32)
    return pl.pallas_call(
        body, name=name, grid=grid,
        in_specs=[blk, g_blk, blk, blk],
        out_specs=[blk] * 4, out_shape=[shp] * 4,
        compiler_params=_params("parallel"),
    )(w, g, m, v)


SCAN_HEADS = 12
FOX_BLOCK = 512


def _local_step(c, x, mem, tgt, w, riders=None):
    t = x.shape[0]
    rw = c.rw
    riders = riders or {}
    carried = {}
    hd = lambda z: z.reshape(t, c.h, HEAD_DIM).transpose(1, 0, 2)
    uh = lambda z: z.transpose(1, 0, 2).reshape(t, rw)
    vecs = (w["mu"], w["w0"], w["a0"], w["k_k"], w["k_a"], w["wd"], w["wi"])

    h, rinv = _rms_fwd(x, w["g_pre"], "rms_pre")
    if "in_proj" in riders:
        groups, finish = riders["in_proj"]
        p, late = _matmul(h, w["wp"], name="in_proj", tk=4096, attach=groups)
        w = dict(w, **finish(late))
    else:
        p = _matmul(h, w["wp"], name="in_proj", tk=4096)
    r, lw, km, v, a, b = _rwkv_pre_fwd(p, c, *vecs)
    scan_in = tuple(hd(z) for z in (r, lw, km, v, a, b))
    hb = max(n for n in range(1, SCAN_HEADS + 1) if c.h % n == 0)
    ysh, states = _rwkv_scan_fwd(*scan_in, hb)
    ys = uh(ysh)
    ycat = _rwkv_post_fwd(ys, r, km, v, p, c, w["ln_w"], w["ln_b"], w["r_k"])

    cum_t = _fox_prep_t(p, c, w["b_f"])
    yfox, ycat, lse = _fox2_fwd(p, c, cum_t, FOX_BLOCK, ycat)

    memn, rinv_m = _rms_fwd(mem, w["g_mem"], "rms_mem")
    mkv = _matmul(memn, w["w_mem_kv"], name="mem_kv")
    ymem = _mem_attn_fwd(p, c, mkv)
    ycat = _gate_fwd(ymem, p, c.o_gmq, "gate_mem", ycat, 2 * rw)
    yo =_matmul(ycat, w["w_out"], name="out_proj", tn=512, tk=4096)
    loss, dout, dyo, dg_post = _post_loss(yo, x, tgt, w["g_post"], "post_loss")

    dyc = _matmul(dyo, w["w_out"], tb=True, name="d_ycat", tn=512, tk=4096)
    dw_out = _matmul(ycat, dyo, ta=True, name="d_w_out", tn=512, tk=4096, out_dtype=BF16)
    dys, dr2, dkm2, dv2, dp, dln_w, dln_b, dr_k = _rwkv_post_bwd(
        dyc, ys, r, km, v, p, c, w["ln_w"], w["ln_b"], w["r_k"])
    dyf, dp = _gate_bwd(dyc, rw, yfox, p, c.o_gfox, "gate_fox_bwd", dp)
    dym, dp = _gate_bwd(dyc, 2 * rw, ymem, p, c.o_gmq, "gate_mem_bwd", dp)

    scan_g = _rwkv_scan_bwd(*scan_in, states, hd(dys), hb)
    dps, dzw, dza, twb, alb, dw0, da0, dk_k, dk_a = _rwkv_pre_bwd(
        p, c, *vecs, *(uh(z) for z in scan_g), dr2, dkm2, dv2)
    dwd = _matmul(twb, dzw, ta=True, name="d_w_decay", out_dtype=BF16)
    dwi = _matmul(alb, dza, ta=True, name="d_w_iclr", out_dtype=BF16)
    dp, dmu = _shift_bwd(dps, p, c, w["mu"], dp)

    dp, dfv, dck, dq_part, dcq_part = _fox2_bwd(p, c, cum_t, lse, yfox, dyf, FOX_BLOCK, dp)
    dfq, dcq = _fox2_dq_sum(dq_part, dcq_part, _tile(t, FOX_BLOCK, LANES))
    dcum = jnp.pad(jnp.stack([dcq.transpose(1, 0, 2).reshape(t, c.h), dck.reshape(c.h, t).T]),
                   ((0, 0), (0, 0), (0, LANES - c.h)))
    dfl, db_f = _fox_logit_bwd(dcum, p, c, w["b_f"])

    dmq, dmk, dmv = _mem_attn_bwd(p, c, mkv, dym)
    dmkv = jnp.concatenate([dmk, dmv], axis=1)
    dw_mkv = _matmul(memn, dmkv, ta=True, name="d_w_mem_kv", out_dtype=BF16)
    dmemn = _matmul(dmkv, w["w_mem_kv"], tb=True, name="d_memn")
    _, dg_mem = _rms_bwd(dmemn, mem, rinv_m, w["g_mem"], jnp.zeros_like(mem), "rms_mem_bwd")

    for off, piece in ((c.o_fq, dfq), (c.o_fv, dfv), (c.o_mq, dmq), (c.o_fl, dfl)):
        dp = lax.dynamic_update_slice(dp, piece, (0, off))
    rest = dict(wd=dwd, wi=dwi, w_mem_kv=dw_mkv, w_out=dw_out)
    if "d_w_in" in riders:
        dwp, carried["rest"] = _matmul(dp, h, ta=True, name="d_w_in", tk=4096, out_dtype=BF16,
                                       attach=riders["d_w_in"](rest))
    else:
        dwp = _matmul(dp, h, ta=True, name="d_w_in", tk=4096, out_dtype=BF16)
    if "d_h" in riders:
        dh, carried["wp"] = _matmul(dp, w["wp"], tb=True, name="d_h", tk=2944, attach=riders["d_h"](dwp))
    else:
        dh = _matmul(dp, w["wp"], tb=True, name="d_h", tk=2944)
    grad_x, dg_pre = _rms_bwd(dh, x, rinv, w["g_pre"], dout, "rms_pre_bwd")

    small = dict(g_pre=dg_pre, mu=dmu, w0=dw0, a0=da0, k_k=dk_k, k_a=dk_a, r_k=dr_k, ln_w=dln_w, ln_b=dln_b,
                 b_f=db_f, g_mem=dg_mem, g_post=dg_post)
    return loss, grad_x, dict(wp=dwp, **rest), small, carried


CHIPS = ((1, 0, 0), (0, 1, 0), (1, 1, 0))
SIBLING = ((0, 0, 1),)
ALL_PEERS = tuple((i, j, k) for i in (0, 1) for j in (0, 1) for k in (0, 1))[1:]


def _chip_of(pos):
    return 2 * pos[0] + pos[1]


DMA_CHUNK = 4 << 20


def _pieces(shape, itemsize):
    lead, (rows, cols) = shape[:-2], shape[-2:]
    k = 1
    if rows % 16 == 0:
        k = max(1, min(rows // 16, -(-rows * cols * itemsize // DMA_CHUNK)))
        while rows % k or (rows // k) % 16:
            k -= 1
    band = rows // k
    idxs = [()]
    for n in lead:
        idxs = [i + (j,) for i in idxs for j in range(n)]
    return [i + (pl.ds(j * band, band),) for i in idxs for j in range(k)]


def _peer_of(me, mask):
    return tuple(1 - v if f else v for v, f in zip(me, mask))


def _exchange(name, groups):
    n = len(groups)
    plan = _plan(groups)

    def body(*refs):
        copies = _copies(groups, plan, refs[:n], refs[n:2 * n], refs[2 * n], refs[2 * n + 1])
        for cp in copies:
            cp.start()
        for cp in copies:
            cp.wait()

    any_spec = pl.BlockSpec(memory_space=pl.ANY)
    return pl.pallas_call(
        body, name=name,
        in_specs=[any_spec] * n, out_specs=[any_spec] * n,
        out_shape=_exchange_shapes(groups),
        input_output_aliases={gi: gi for gi, g in enumerate(groups) if g.get("inplace")},
        scratch_shapes=[pltpu.SemaphoreType.DMA((len(plan),)), pltpu.SemaphoreType.DMA((len(plan),))],
    )(*[g["src"] for g in groups])


def _plan(groups):
    return [(gi, ti, idx) for gi, g in enumerate(groups) for ti in range(len(g["transfers"]))
            for idx in _pieces(tuple(g["piece"]), g["src"].dtype.itemsize)]


def _exchange_shapes(groups):
    lead = lambda s: tuple(s) if isinstance(s, tuple) else (s,)
    return [jax.ShapeDtypeStruct(lead(g["slots"]) + tuple(g["piece"]), g["src"].dtype) for g in groups]


def _copies(groups, plan, srcs, outs, send_sems, recv_sems):
    me = (lax.axis_index("x"), lax.axis_index("y"), lax.axis_index("c"))
    copies = []
    for k, (gi, ti, idx) in enumerate(plan):
        mask, view, slot = groups[gi]["transfers"][ti]
        peer = _peer_of(me, mask)
        copies.append(pltpu.make_async_remote_copy(
            src_ref=view(srcs[gi], me, peer).at[idx], dst_ref=outs[gi].at[slot(me, peer)].at[idx],
            send_sem=send_sems.at[k], recv_sem=recv_sems.at[k],
            device_id=peer, device_id_type=MESH))
    return copies


def _my_chip():
    return 2 * lax.axis_index("x") + lax.axis_index("y")


def _put(buf, block, slot):
    return lax.dynamic_update_slice(buf, block[None], (slot,) + (0,) * block.ndim)


def _sum_slots(recv, own, k, out_dtype, name):
    s, rows, cols = recv.shape
    budget = max(16, ((4 << 20) // ((s + 1) * cols * 4)) // 16 * 16)
    tr = _tile(rows, budget, 16)
    own_many = own.shape[0] > 1

    def body(k_ref, *refs):
        out_ref = refs[s + 1]
        mine = refs[s][0].astype(F32)
        acc = None
        for i in range(s):
            term = jnp.where(k_ref[0] == i, mine, refs[i][0].astype(F32))
            acc = term if acc is None else acc + term
        out_ref[...] = acc.astype(out_ref.dtype)

    def slot_spec(i):
        return pl.BlockSpec((1, tr, cols), lambda j, kr: (jnp.where(kr[0] == i, (i + 1) % s, i), j, 0))

    grid_spec = pltpu.PrefetchScalarGridSpec(
        num_scalar_prefetch=1, grid=(rows // tr,),
        in_specs=[slot_spec(i) for i in range(s)]
                 + [pl.BlockSpec((1, tr, cols), lambda j, kr: (kr[0] if own_many else 0, j, 0))],
        out_specs=pl.BlockSpec((tr, cols), lambda j, kr: (j, 0)))
    return pl.pallas_call(
        body, name=name, grid_spec=grid_spec,
        out_shape=jax.ShapeDtypeStruct((rows, cols), out_dtype),
        compiler_params=_params("parallel"),
    )(k, *([recv] * s), own)


def _all_gather(shards):
    return _gather_finish(shards, _exchange("gather_chips", _gather_groups(shards)), "gather_pair")


def _gather_groups(shards):
    halves = [s.reshape(2, s.shape[0] // 2, s.shape[1]) for s in shards]
    return [dict(src=q, slots=(4, 2), piece=q.shape[1:],
                 transfers=[(m, lambda ref, me, peer: ref.at[me[2]], lambda me, peer: (_chip_of(me), me[2]))
                            for m in CHIPS])
            for q in halves]


def _gather_finish(shards, first, name):
    spot = lambda m: (lambda me: (_chip_of(_peer_of(me, m)), me[2]))
    both = _exchange(name, [
        dict(src=q, slots=(4, 2), piece=q.shape[2:], inplace=True,
             transfers=[(SIBLING[0], (lambda f: lambda ref, me, peer: ref.at[f(me)])(spot(m)),
                         (lambda f: lambda me, peer: f(me))(spot(m))) for m in CHIPS])
        for q in first])
    return [_put(q.reshape((4,) + s.shape), s, _my_chip()) for s, q in zip(shards, both)]


def _reduce_pair(partials, tag):
    core1 = lax.axis_index("c").reshape(1).astype(jnp.int32)
    halves = [q.reshape(4, 2, q.shape[1] // 2, q.shape[2]).transpose(1, 0, 2, 3) for q in partials]
    pair = _exchange("reduce_pair_" + tag, [
        dict(src=q, slots=2, piece=q.shape[1:],
             transfers=[(SIBLING[0], lambda ref, me, peer: ref.at[peer[2]], lambda me, peer: me[2])])
        for q in halves])
    flat = lambda e: e.reshape(2, -1, e.shape[-1])
    return [_sum_slots(flat(e), flat(q), core1, BF16, "reduce_pair_sum_" + tag).reshape(q.shape[1:])
            for e, q in zip(pair, halves)]


def _reduce_chips_groups(chip_sums):
    return [dict(src=q, slots=4, piece=q.shape[1:],
                 transfers=[(m, lambda ref, me, peer: ref.at[_chip_of(peer)], lambda me, peer: _chip_of(me))
                            for m in CHIPS])
            for q in chip_sums]


def _reduce_finish(crossed, chip_sums, tag):
    core = lax.axis_index("c")
    chip1 = _my_chip().reshape(1).astype(jnp.int32)
    sums = [_sum_slots(e, q, chip1, F32, "reduce_chips_sum_" + tag) for e, q in zip(crossed, chip_sums)]
    swapped = _exchange("reduce_swap_" + tag, [
        dict(src=q, slots=2, piece=q.shape, transfers=[(SIBLING[0], lambda ref, me, peer: ref, lambda me, peer: me[2])])
        for q in sums])
    return [_put(e, q, core).reshape(-1, e.shape[-1]) for e, q in zip(swapped, sums)]


def _reduce_scatter(partials):
    chip_sums = _reduce_pair(partials, "all")
    return _reduce_finish(_exchange("reduce_chips", _reduce_chips_groups(chip_sums)), chip_sums, "all")


def _all_reduce_small(vec):
    dev = 4 * lax.axis_index("x") + 2 * lax.axis_index("y") + lax.axis_index("c")
    got = _exchange("reduce_small", [
        dict(src=vec, slots=8, piece=vec.shape,
             transfers=[(m, lambda ref, me, peer: ref, lambda me, peer: 4 * me[0] + 2 * me[1] + me[2])
                        for m in ALL_PEERS])])[0]
    return _sum_slots(got, vec[None], dev.reshape(1).astype(jnp.int32), F32, "reduce_small_sum")


SMALL = ("g_pre", "mu", "w0", "a0", "k_k", "k_a", "r_k", "ln_w", "ln_b", "b_f", "g_mem", "g_post")


def _pad_cols(a, n):
    return jnp.pad(a, ((0, 0),) * (a.ndim - 1) + ((0, n - a.shape[-1]),))


def kernel(x, mem, g_pre, w_in, mu_rwkv, w0, w_decay_up, a0, w_iclr_up, k_k, k_a, r_k, ln_x_w, ln_x_b, b_f, g_mem, w_mem_kv, w_out, g_post, loss_target, m_g_pre, m_w_in, m_mu_rwkv, m_w0, m_w_decay_up, m_a0, m_w_iclr_up, m_k_k, m_k_a, m_r_k, m_ln_x_w, m_ln_x_b, m_b_f, m_g_mem, m_w_mem_kv, m_w_out, m_g_post, v_g_pre, v_w_in, v_mu_rwkv, v_w0, v_w_decay_up, v_a0, v_w_iclr_up, v_k_k, v_k_a, v_r_k, v_ln_x_w, v_ln_x_b, v_b_f, v_g_mem, v_w_mem_kv, v_w_out, v_g_post):
    d = x.shape[-1]
    c = Cfg(d)
    ws = w_in.shape[-1]
    wpad = -(-ws // LANES) * LANES
    nh = c.h

    g_in, g_wd, g_wi = _all_gather([
        _pad_cols(w_in[0].astype(BF16), wpad), w_decay_up[0].astype(BF16), w_iclr_up[0].astype(BF16)])
    fl = c.ref_fl
    runs = [(0, fl, 0), (fl, fl + nh, c.o_fl), (fl + nh, c.in_width, fl)]
    pieces = []
    for lo, hi, _ in sorted(runs, key=lambda r: r[2]):
        for s in range(4):
            a, b = max(lo, s * ws), min(hi, (s + 1) * ws)
            if a < b:
                pieces.append(g_in[s, :, a - s * ws:b - s * ws])
    wp = jnp.concatenate(pieces + [jnp.zeros((d, LANES - nh), BF16)], axis=1)
    unshard = lambda g: g.transpose(1, 0, 2).reshape(g.shape[1], -1)
    weights = dict(wp=wp, wd=unshard(g_wd), wi=unshard(g_wi),
                   g_pre=g_pre, mu=mu_rwkv, w0=w0, a0=a0, k_k=k_k, k_a=k_a, r_k=r_k.reshape(1, -1),
                   ln_w=ln_x_w, ln_b=ln_x_b, b_f=_pad_cols(b_f, LANES), g_mem=g_mem, g_post=g_post)
    late_shards = [w_out[0].astype(BF16), w_mem_kv[0].astype(BF16)]

    def late_weights(first):
        g_out, g_mkv = _gather_finish(late_shards, first, "gather_pair_late")
        return dict(w_out=g_out.reshape(-1, d), w_mem_kv=g_mkv.reshape(d, -1))

    by_chip = lambda g: jnp.stack(jnp.split(g, 4, axis=1))
    pair_sums = {}

    def ride_rest(g):
        pair_sums["rest"] = _reduce_pair([g["w_out"].reshape(4, -1, d), g["w_mem_kv"].reshape(4, d // 4, -1),
                                          by_chip(g["wd"]), by_chip(g["wi"])], "rest")
        return _reduce_chips_groups(pair_sums["rest"])

    def ride_wp(dwpt):
        shards = []
        for s in range(4):
            rows = []
            for lo, hi, at in runs:
                a, b = max(lo, s * ws), min(hi, (s + 1) * ws)
                if a < b:
                    rows.append(dwpt[at + a - lo:at + b - lo, :])
            part = rows[0] if len(rows) == 1 else jnp.concatenate(rows, axis=0)
            shards.append(jnp.pad(part, ((0, wpad - ws), (0, 0))))
        pair_sums["wp"] = _reduce_pair([jnp.stack(shards)], "w_in")
        return _reduce_chips_groups(pair_sums["wp"])

    loss, grad_x, _, small, carried = _local_step(
        c, x[0], mem[0], loss_target[0], weights,
        riders={"in_proj": (_gather_groups(late_shards), late_weights), "d_w_in": ride_rest, "d_h": ride_wp})
    red = (_reduce_finish(carried["wp"], pair_sums["wp"], "w_in")
           + _reduce_finish(carried["rest"], pair_sums["rest"], "rest"))
    big_w = (w_in[0].T, w_out[0], w_mem_kv[0], w_decay_up[0], w_iclr_up[0])
    big_m = (m_w_in[0].T, m_w_out[0], m_w_mem_kv[0], m_w_decay_up[0], m_w_iclr_up[0])
    big_v = (v_w_in[0].T, v_w_out[0], v_w_mem_kv[0], v_w_decay_up[0], v_w_iclr_up[0])
    big_names = ("w_in", "w_out", "w_mem_kv", "w_decay_up", "w_iclr_up")
    upd = {n: _adamw(w_, g_, m_, v_, "adamw_" + n) for n, w_, g_, m_, v_ in zip(big_names, big_w, red, big_m, big_v)}
    upd["w_in"] = [o.T for o in upd["w_in"]]

    small_w = dict(g_pre=g_pre, mu=mu_rwkv, w0=w0, a0=a0, k_k=k_k, k_a=k_a, r_k=r_k.reshape(1, -1), ln_w=ln_x_w,
                   ln_b=ln_x_b, b_f=b_f, g_mem=g_mem, g_post=g_post)
    small_m = dict(g_pre=m_g_pre, mu=m_mu_rwkv, w0=m_w0, a0=m_a0, k_k=m_k_k, k_a=m_k_a, r_k=m_r_k.reshape(1, -1),
                   ln_w=m_ln_x_w, ln_b=m_ln_x_b, b_f=m_b_f, g_mem=m_g_mem, g_post=m_g_post)
    small_v = dict(g_pre=v_g_pre, mu=v_mu_rwkv, w0=v_w0, a0=v_a0, k_k=v_k_k, k_a=v_k_a, r_k=v_r_k.reshape(1, -1),
                   ln_w=v_ln_x_w, ln_b=v_ln_x_b, b_f=v_b_f, g_mem=v_g_mem, g_post=v_g_post)
    widths = [-(-small_w[n].shape[1] // LANES) * LANES for n in SMALL]
    pack = lambda t: jnp.concatenate([_pad_cols(t[n], wd_) for n, wd_ in zip(SMALL, widths)]
                                     + [jnp.zeros((1, LANES), F32)], axis=1)
    g_packed = jnp.concatenate([_pad_cols(small[n], wd_) for n, wd_ in zip(SMALL, widths)]
                               + [_pad_cols(loss, LANES)], axis=1)
    g_sum = _all_reduce_small(g_packed)
    s_upd = _adamw(pack(small_w), g_sum, pack(small_m), pack(small_v), "adamw_small")
    offs = [sum(widths[:i]) for i in range(len(SMALL))]

    def take(kind, n):
        i = SMALL.index(n)
        piece = s_upd[kind][:, offs[i]:offs[i] + small_w[n].shape[1]]
        return piece.reshape(r_k.shape) if n == "r_k" else piece

    total_loss = g_sum[0, sum(widths)]
    order = ("g_pre", "w_in", "mu", "w0", "w_decay_up", "a0", "w_iclr_up", "k_k", "k_a", "r_k", "ln_w", "ln_b", "b_f",
             "g_mem", "w_mem_kv", "w_out", "g_post")
    outs = [total_loss, grad_x[None]]
    for kind in range(4):
        for n in order:
            outs.append(upd[n][kind][None] if n in upd else take(kind, n))
    return tuple(outs)
```

```python
import jax
import jax.numpy as jnp
from jax import lax
from jax.experimental import pallas as pl
from jax.experimental.pallas import tpu as pltpu

F32 = jnp.float32
BF16 = jnp.bfloat16
HI = lax.Precision.HIGHEST
MESH = pl.DeviceIdType.MESH

HEAD_DIM = 64
MEM_HEADS = 4
LORA = 128
CHUNK = 64
RMS_EPS = 1e-6
GN_EPS = 64e-5
LANES = 128
WIDE = 2 * LANES
VMEM_LIMIT = 56 * 1024 * 1024

ADAM_LR, ADAM_B1, ADAM_B2, ADAM_EPS, ADAM_WD, ADAM_STEP = 0.001, 0.9, 0.999, 1e-08, 0.01, 10


class Cfg:
    def __init__(self, d):
        self.d = d
        self.rw = 3 * d // 8
        self.mw = d // 4
        self.h = self.rw // HEAD_DIM
        self.mhd = self.mw // MEM_HEADS
        self.shift = 3 * self.rw + 2 * LORA
        self.in_width = self.shift + 5 * self.rw + self.h + 2 * self.mw
        o = self.shift
        self.o_grw = o; o += self.rw
        self.o_fq = o; o += self.rw
        self.o_fk = o; o += self.rw
        self.o_fv = o; o += self.rw
        self.o_gfox = o; o += self.rw
        self.o_mq = o; o += self.mw
        self.o_gmq = o; o += self.mw
        self.o_fl = o; o += LANES
        self.wp = o
        self.ref_fl = self.shift + 4 * self.rw


def _tile(n, pref, align=LANES):
    if n <= pref:
        return n
    t = (pref // align) * align
    while t >= align:
        if n % t == 0:
            return t
        t -= align
    return n


def _params(*sem):
    return pltpu.CompilerParams(dimension_semantics=sem, vmem_limit_bytes=VMEM_LIMIT)


def _pallas_into(body, into, n_in, out_index, in_specs, **kw):
    if into is None:
        return pl.pallas_call(body, in_specs=in_specs, **kw)

    def body_with_alias(*refs):
        return body(*refs[:n_in], *refs[n_in + 1:])

    call = pl.pallas_call(body_with_alias, in_specs=list(in_specs) + [pl.BlockSpec(memory_space=pl.ANY)],
                          input_output_aliases={n_in: out_index}, **kw)
    return lambda *args: call(*args, into)


def _sig(x):
    return 1.0 / (1.0 + jnp.exp(-x))


def _softplus(x):
    return jnp.maximum(x, 0.0) + jnp.log(1.0 + jnp.exp(-jnp.abs(x)))


def _dot(a, b, dims, prec=None):
    return lax.dot_general(a, b, (dims, ((), ())), precision=prec, preferred_element_type=F32)


def _mm(a, b, prec=None):
    return _dot(a, b, ((1,), (0,)), prec)


def _mm_nt(a, b, prec=None):
    return _dot(a, b, ((1,), (1,)), prec)


def _mm_tn(a, b, prec=None):
    return _dot(a, b, ((0,), (0,)), prec)


def _split(a):
    hi = a.astype(BF16)
    return hi, (a - hi.astype(F32)).astype(BF16)


def _dot3(a, b, dims, passes=3):
    d = lambda x, y: lax.dot_general(x, y, dims, preferred_element_type=F32)
    if passes == 1:
        return d(a.astype(BF16), b.astype(BF16))
    if passes == 2:
        ah, (bh, bl) = a.astype(BF16), _split(b)
        return d(ah, bh) + d(ah, bl)
    (ah, al), (bh, bl) = _split(a), _split(b)
    return d(ah, bh) + (d(ah, bl) + d(al, bh))


def _bmm(a, b, passes=3):
    return _dot3(a, b, (((2,), (1,)), ((0,), (0,))), passes)


def _bmm_nt(a, b, passes=3):
    return _dot3(a, b, (((2,), (2,)), ((0,), (0,))), passes)


def _bmm_tn(a, b, passes=3):
    return _dot3(a, b, (((1,), (1,)), ((0,), (0,))), passes)


def _bmm_01(m01, x):
    x1 = x.astype(BF16)
    r1 = x - x1.astype(F32)
    x2 = r1.astype(BF16)
    x3 = (r1 - x2.astype(F32)).astype(BF16)
    d = lambda y: lax.dot_general(m01, y, (((2,), (1,)), ((0,), (0,))), preferred_element_type=F32)
    return d(x1) + (d(x2) + d(x3))


def _matmul(a, b, *, ta=False, tb=False, out_dtype=F32, name, tm=1024, tn=1024, tk=1024, attach=None):
    m, k = (a.shape[1], a.shape[0]) if ta else a.shape
    n = b.shape[0] if tb else b.shape[1]
    tm, tn, tk = _tile(m, tm), _tile(n, tn), _tile(k, tk)
    nk = k // tk
    grid = (m // tm, n // tn, nk)
    dims = ((0 if ta else 1,), (1 if tb else 0,))
    groups = attach or []
    ng = len(groups)
    plan = _plan(groups)

    def body(a_ref, b_ref, *rest):
        srcs, o_ref, outs, scratch = rest[:ng], rest[ng], rest[ng + 1:2 * ng + 1], rest[2 * ng + 1:]
        acc = scratch[0] if nk > 1 else None
        if ng:
            copies = _copies(groups, plan, srcs, outs, scratch[-2], scratch[-1])
            ids = [pl.program_id(ax) for ax in range(3)]

            @pl.when((ids[0] == 0) & (ids[1] == 0) & (ids[2] == 0))
            def _():
                for cp in copies:
                    cp.start()

        part = _dot(a_ref[...].astype(BF16), b_ref[...].astype(BF16), dims)
        if nk == 1:
            o_ref[...] = part.astype(o_ref.dtype)
        else:
            kk = pl.program_id(2)

            @pl.when(kk == 0)
            def _():
                acc[...] = part

            @pl.when(kk > 0)
            def _():
                acc[...] += part

            @pl.when(kk == nk - 1)
            def _():
                o_ref[...] = acc[...].astype(o_ref.dtype)

        if ng:
            @pl.when((ids[0] == grid[0] - 1) & (ids[1] == grid[1] - 1) & (ids[2] == grid[2] - 1))
            def _():
                for cp in copies:
                    cp.wait()

    a_spec = pl.BlockSpec((tk, tm), lambda i, j, kk: (kk, i)) if ta else pl.BlockSpec((tm, tk), lambda i, j, kk: (i, kk))
    b_spec = pl.BlockSpec((tn, tk), lambda i, j, kk: (j, kk)) if tb else pl.BlockSpec((tk, tn), lambda i, j, kk: (kk, j))
    any_spec = pl.BlockSpec(memory_space=pl.ANY)
    sems = [pltpu.SemaphoreType.DMA((len(plan),)), pltpu.SemaphoreType.DMA((len(plan),))] if ng else []
    res = pl.pallas_call(
        body, name=name, grid=grid,
        in_specs=[a_spec, b_spec] + [any_spec] * ng,
        out_specs=[pl.BlockSpec((tm, tn), lambda i, j, kk: (i, j))] + [any_spec] * ng,
        out_shape=[jax.ShapeDtypeStruct((m, n), out_dtype)] + _exchange_shapes(groups),
        scratch_shapes=([pltpu.VMEM((tm, tn), F32)] if nk > 1 else []) + sems,
        compiler_params=_params(*(("arbitrary",) * 3 if ng else ("parallel", "parallel", "arbitrary"))),
    )(a, b, *[g["src"] for g in groups])
    return (res[0], list(res[1:])) if ng else res[0]


def _rms_fwd(x, g, name):
    t, d = x.shape
    tm = _tile(t, 256, 8)

    def body(x_ref, g_ref, h_ref, r_ref):
        xv = x_ref[...]
        r = lax.rsqrt(jnp.mean(xv * xv, axis=-1, keepdims=True) + RMS_EPS)
        h_ref[...] = (xv * r * g_ref[...]).astype(BF16)
        r_ref[...] = r

    return pl.pallas_call(
        body, name=name, grid=(t // tm,),
        in_specs=[pl.BlockSpec((tm, d), lambda i: (i, 0)), pl.BlockSpec((1, d), lambda i: (0, 0))],
        out_specs=[pl.BlockSpec((tm, d), lambda i: (i, 0)), pl.BlockSpec((tm, 1), lambda i: (i, 0))],
        out_shape=[jax.ShapeDtypeStruct((t, d), BF16), jax.ShapeDtypeStruct((t, 1), F32)],
        compiler_params=_params("parallel"),
    )(x, g)


def _rms_bwd(dh, x, rinv, g, add, name):
    t, d = x.shape
    tm = _tile(t, 256, 8)

    def body(dh_ref, x_ref, r_ref, g_ref, add_ref, dx_ref, dg_ref):
        @pl.when(pl.program_id(0) == 0)
        def _():
            dg_ref[...] = jnp.zeros_like(dg_ref)

        r = r_ref[...]
        xn = x_ref[...] * r
        dhv = dh_ref[...]
        dg_ref[...] += jnp.sum(dhv * xn, axis=0, keepdims=True)
        dxn = dhv * g_ref[...]
        dx_ref[...] = add_ref[...] + r * (dxn - xn * jnp.mean(dxn * xn, axis=-1, keepdims=True))

    row = pl.BlockSpec((tm, d), lambda i: (i, 0))
    vec = pl.BlockSpec((1, d), lambda i: (0, 0))
    return pl.pallas_call(
        body, name=name, grid=(t // tm,),
        in_specs=[row, row, pl.BlockSpec((tm, 1), lambda i: (i, 0)), vec, row],
        out_specs=[row, vec],
        out_shape=[jax.ShapeDtypeStruct((t, d), F32), jax.ShapeDtypeStruct((1, d), F32)],
        compiler_params=_params("arbitrary"),
    )(dh, x, rinv, g, add)


def _post_loss(yo, x, tgt, g, name):
    t, d = x.shape
    tm = _tile(t, 256, 8)

    def body(yo_ref, x_ref, t_ref, g_ref, loss_ref, dout_ref, dyo_ref, dg_ref):
        @pl.when(pl.program_id(0) == 0)
        def _():
            dg_ref[...] = jnp.zeros_like(dg_ref)
            loss_ref[...] = jnp.zeros_like(loss_ref)

        yv = yo_ref[...]
        r = lax.rsqrt(jnp.mean(yv * yv, axis=-1, keepdims=True) + RMS_EPS)
        n = yv * r
        err = x_ref[...] + n * g_ref[...] - t_ref[...]
        loss_ref[...] += 0.5 * jnp.sum(jnp.mean(err * err, axis=-1, keepdims=True), axis=0, keepdims=True)
        dout = err * (1.0 / d)
        dout_ref[...] = dout
        dg_ref[...] += jnp.sum(dout * n, axis=0, keepdims=True)
        dn = dout * g_ref[...]
        dyo_ref[...] = (r * (dn - n * jnp.mean(dn * n, axis=-1, keepdims=True))).astype(BF16)

    row = pl.BlockSpec((tm, d), lambda i: (i, 0))
    vec = pl.BlockSpec((1, d), lambda i: (0, 0))
    return pl.pallas_call(
        body, name=name, grid=(t // tm,),
        in_specs=[row, row, row, vec],
        out_specs=[pl.BlockSpec((1, 1), lambda i: (0, 0)), row, row, vec],
        out_shape=[jax.ShapeDtypeStruct((1, 1), F32), jax.ShapeDtypeStruct((t, d), F32),
                   jax.ShapeDtypeStruct((t, d), BF16), jax.ShapeDtypeStruct((1, d), F32)],
        compiler_params=_params("arbitrary"),
    )(yo, x, tgt, g)


def _head_sum(x):
    ri = lax.broadcasted_iota(jnp.int32, (LANES, LANES), 0) // HEAD_DIM
    ci = lax.broadcasted_iota(jnp.int32, (LANES, LANES), 1) // HEAD_DIM
    e = (ri == ci).astype(BF16)
    x1 = x.astype(BF16)
    r1 = x - x1.astype(F32)
    x2 = r1.astype(BF16)
    x3 = (r1 - x2.astype(F32)).astype(BF16)
    parts = []
    for i in range(x.shape[1] // LANES):
        sl = slice(i * LANES, (i + 1) * LANES)
        parts.append(_mm(x1[:, sl], e) + (_mm(x2[:, sl], e) + _mm(x3[:, sl], e)))
    return parts[0] if len(parts) == 1 else jnp.concatenate(parts, axis=1)


def _shifted(p_cur, before, first, mu):
    rolled = pltpu.roll(p_cur, 1, 0)
    prev_row = jnp.where(first, 0.0, before)
    row0 = lax.broadcasted_iota(jnp.int32, p_cur.shape, 0) == 0
    prev = jnp.where(row0, prev_row, rolled)
    return p_cur + (prev - p_cur) * mu, prev


def _rwkv_features(ps, rw, w0, a0, k_k, k_a, wd, wi):
    r, k, v = ps[:, 0:rw], ps[:, rw:2 * rw], ps[:, 2 * rw:3 * rw]
    wl, al = ps[:, 3 * rw:3 * rw + LORA], ps[:, 3 * rw + LORA:3 * rw + 2 * LORA]
    tw = jnp.tanh(wl)
    zw = w0 + _mm(tw.astype(BF16), wd)
    logw = -jnp.exp(-_softplus(-zw) - 0.5)
    alpha = _sig(a0 + _mm(al.astype(BF16), wi))
    kkr = k * k_k
    n2 = _head_sum(kkr * kkr)
    rn = lax.rsqrt(jnp.maximum(n2, 1e-24))
    kk = kkr * rn
    kmod = k * (1.0 + (alpha - 1.0) * k_a)
    return dict(r=r, k=k, v=v, tw=tw, al=al, zw=zw, logw=logw, alpha=alpha, kk=kk, rn=rn, n2=n2, kmod=kmod)


def _rwkv_pre_fwd(p, c, mu, w0, a0, k_k, k_a, wd, wi):
    t = p.shape[0]
    tm = _tile(t, 128, 8)
    rw, sh = c.rw, c.shift

    def body(p_ref, pp_ref, mu_ref, w0_ref, a0_ref, kk_ref, ka_ref, wd_ref, wi_ref,
             r_ref, lw_ref, km_ref, v_ref, a_ref, b_ref):
        ps, _ = _shifted(p_ref[...], pp_ref[7:8, :], pl.program_id(0) == 0, mu_ref[...])
        f = _rwkv_features(ps, rw, w0_ref[...], a0_ref[...], kk_ref[...], ka_ref[...], wd_ref[...], wi_ref[...])
        r_ref[...] = f["r"]
        lw_ref[...] = f["logw"]
        km_ref[...] = f["kmod"]
        v_ref[...] = f["v"]
        a_ref[...] = -f["kk"]
        b_ref[...] = f["kk"] * f["alpha"]

    vec = lambda n: pl.BlockSpec((1, n), lambda i: (0, 0))
    out = pl.BlockSpec((tm, rw), lambda i: (i, 0))
    return pl.pallas_call(
        body, name="rwkv_pre_fwd", grid=(t // tm,),
        in_specs=[pl.BlockSpec((tm, sh), lambda i: (i, 0)),
                  pl.BlockSpec((8, sh), lambda i: (jnp.maximum(i * (tm // 8) - 1, 0), 0)),
                  vec(sh), vec(rw), vec(rw), vec(rw), vec(rw),
                  pl.BlockSpec((LORA, rw), lambda i: (0, 0)), pl.BlockSpec((LORA, rw), lambda i: (0, 0))],
        out_specs=[out] * 6,
        out_shape=[jax.ShapeDtypeStruct((t, rw), F32)] * 6,
        compiler_params=_params("parallel"),
    )(p, p, mu, w0, a0, k_k, k_a, wd, wi)


def _rwkv_pre_bwd(p, c, mu, w0, a0, k_k, k_a, wd, wi, dr, dlw, dkm, dv, da, db, dr2, dkm2, dv2):
    t = p.shape[0]
    tm = _tile(t, 128, 8)
    rw, sh = c.rw, c.shift

    def body(p_ref, pp_ref, mu_ref, w0_ref, a0_ref, kk_ref, ka_ref, wd_ref, wi_ref,
             dr_ref, dlw_ref, dkm_ref, dv_ref, da_ref, db_ref, dr2_ref, dkm2_ref, dv2_ref,
             dps_ref, dzw_ref, dza_ref, tw_ref, al_ref, dw0_ref, da0_ref, dkk_ref, dka_ref):
        @pl.when(pl.program_id(0) == 0)
        def _():
            for ref in (dw0_ref, da0_ref, dkk_ref, dka_ref):
                ref[...] = jnp.zeros_like(ref)

        ps, _ = _shifted(p_ref[...], pp_ref[7:8, :], pl.program_id(0) == 0, mu_ref[...])
        k_k, k_a = kk_ref[...], ka_ref[...]
        f = _rwkv_features(ps, rw, w0_ref[...], a0_ref[...], k_k, k_a, wd_ref[...], wi_ref[...])
        alpha, kk, k = f["alpha"], f["kk"], f["k"]
        dkm = dkm_ref[...] + dkm2_ref[...]
        db = db_ref[...]
        dkk = db * alpha - da_ref[...]
        dalpha = db * kk + dkm * k * k_a
        dk = dkm * (1.0 + (alpha - 1.0) * k_a)
        dka_ref[...] += jnp.sum(dkm * k * (alpha - 1.0), axis=0, keepdims=True)
        dkkr = f["rn"] * jnp.where(f["n2"] > 1e-24, dkk - kk * _head_sum(dkk * kk), dkk)
        dk = dk + dkkr * k_k
        dkk_ref[...] += jnp.sum(dkkr * k, axis=0, keepdims=True)
        dza = dalpha * alpha * (1.0 - alpha)
        da0_ref[...] += jnp.sum(dza, axis=0, keepdims=True)
        dzw = dlw_ref[...] * f["logw"] * _sig(-f["zw"])
        dw0_ref[...] += jnp.sum(dzw, axis=0, keepdims=True)
        dza_b, dzw_b = dza.astype(BF16), dzw.astype(BF16)
        dal = _mm_nt(dza_b, wi_ref[...])
        dwl = _mm_nt(dzw_b, wd_ref[...]) * (1.0 - f["tw"] * f["tw"])
        dps_ref[:, 0:rw] = dr_ref[...] + dr2_ref[...]
        dps_ref[:, rw:2 * rw] = dk
        dps_ref[:, 2 * rw:3 * rw] = dv_ref[...] + dv2_ref[...]
        dps_ref[:, 3 * rw:3 * rw + LORA] = dwl
        dps_ref[:, 3 * rw + LORA:sh] = dal
        dzw_ref[...] = dzw_b
        dza_ref[...] = dza_b
        tw_ref[...] = f["tw"].astype(BF16)
        al_ref[...] = f["al"].astype(BF16)

    vec = lambda n: pl.BlockSpec((1, n), lambda i: (0, 0))
    blk = lambda n: pl.BlockSpec((tm, n), lambda i: (i, 0))
    return pl.pallas_call(
        body, name="rwkv_pre_bwd", grid=(t // tm,),
        in_specs=[blk(sh), pl.BlockSpec((8, sh), lambda i: (jnp.maximum(i * (tm // 8) - 1, 0), 0)),
                  vec(sh), vec(rw), vec(rw), vec(rw), vec(rw),
                  pl.BlockSpec((LORA, rw), lambda i: (0, 0)), pl.BlockSpec((LORA, rw), lambda i: (0, 0))]
                 + [blk(rw)] * 9,
        out_specs=[blk(sh), blk(rw), blk(rw), blk(LORA), blk(LORA), vec(rw), vec(rw), vec(rw), vec(rw)],
        out_shape=[jax.ShapeDtypeStruct((t, sh), F32), jax.ShapeDtypeStruct((t, rw), BF16),
                   jax.ShapeDtypeStruct((t, rw), BF16), jax.ShapeDtypeStruct((t, LORA), BF16),
                   jax.ShapeDtypeStruct((t, LORA), BF16)] + [jax.ShapeDtypeStruct((1, rw), F32)] * 4,
        compiler_params=_params("arbitrary"),
    )(p, p, mu, w0, a0, k_k, k_a, wd, wi, dr, dlw, dkm, dv, da, db, dr2, dkm2, dv2)


def _shift_bwd(dps, p, c, mu, dp):
    t = p.shape[0]
    tm = _tile(t, 256, 8)
    sh = c.shift
    nt = t // tm

    def body(d_ref, dn_ref, p_ref, pp_ref, mu_ref, dp_ref, dmu_ref):
        i = pl.program_id(0)

        @pl.when(i == 0)
        def _():
            dmu_ref[...] = jnp.zeros_like(dmu_ref)

        mu = mu_ref[...]
        d = d_ref[...]
        pc = p_ref[...]
        _, prev = _shifted(pc, pp_ref[7:8, :], i == 0, mu)
        dmu_ref[...] += jnp.sum(d * (prev - pc), axis=0, keepdims=True)
        nxt_row = jnp.where(i == nt - 1, 0.0, dn_ref[0:1, :])
        last = lax.broadcasted_iota(jnp.int32, d.shape, 0) == tm - 1
        nxt = jnp.where(last, nxt_row, pltpu.roll(d, tm - 1, 0))
        dp_ref[...] = (d * (1.0 - mu) + nxt * mu).astype(BF16)

    blk = pl.BlockSpec((tm, sh), lambda i: (i, 0))
    return _pallas_into(
        body, dp, 5, 0, name="shift_bwd", grid=(nt,),
        in_specs=[blk, pl.BlockSpec((8, sh), lambda i: (jnp.minimum((i + 1) * (tm // 8), t // 8 - 1), 0)),
                  blk, pl.BlockSpec((8, sh), lambda i: (jnp.maximum(i * (tm // 8) - 1, 0), 0)),
                  pl.BlockSpec((1, sh), lambda i: (0, 0))],
        out_specs=[blk, pl.BlockSpec((1, sh), lambda i: (0, 0))],
        out_shape=[jax.ShapeDtypeStruct((t, c.wp), BF16), jax.ShapeDtypeStruct((1, sh), F32)],
        compiler_params=_params("arbitrary"),
    )(dps, dps, p, p, mu)


def _tri(n, strict):
    ri = lax.broadcasted_iota(jnp.int32, (n, n), 0)
    ci = lax.broadcasted_iota(jnp.int32, (n, n), 1)
    return (ri > ci) if strict else (ri >= ci)


def _unit_lower_inverse(a):
    n = a.shape[-1]
    ri = lax.broadcasted_iota(jnp.int32, (n, n), 0)
    ci = lax.broadcasted_iota(jnp.int32, (n, n), 1)
    eye = (ri == ci).astype(F32)
    blk = lambda s: (ri // s) == (ci // s)
    ad = jnp.where(blk(16), a, 0.0)
    p = eye + ad
    for _ in range(3):
        ad = _bmm(ad, ad, P_SOLVE)
        p = p + _bmm(p, ad, P_SOLVE)
    s = 16
    while s < n:
        off = jnp.where(blk(2 * s) & ~blk(s), a, 0.0)
        p = p + _bmm(_bmm(p, off, P_SOLVE), p, P_SOLVE)
        s *= 2
    return p


P_SOLVE, P_STATE, P_OUT, P_GRAD, P_DECAY = 1, 1, 1, 1, 2


def _chunk_common(r, lw, k, a, b):
    n = r.shape[1]
    tri_incl = jnp.broadcast_to(_tri(n, False).astype(BF16), (r.shape[0], n, n))
    cum = _bmm_01(tri_incl, lw)
    e_pos, e_neg, e_exc = jnp.exp(cum), jnp.exp(-cum), jnp.exp(cum - lw)
    last = lax.broadcasted_iota(jnp.int32, (n, r.shape[2]), 0) == n - 1
    g_last = jnp.exp(jnp.sum(jnp.where(last, cum, 0.0), axis=1, keepdims=True))
    return g_last, r * e_pos, a * e_exc, b * e_neg, k * e_neg, e_pos, e_neg, e_exc


def _chunk_solve(rt, at, bt, kt, v, g0):
    strict, incl = _tri(rt.shape[1], True), _tri(rt.shape[1], False)
    a_ab = jnp.where(strict, _bmm_nt(at, bt, P_SOLVE), 0.0)
    a_ak = jnp.where(strict, _bmm_nt(at, kt, P_SOLVE), 0.0)
    a_rb = jnp.where(incl, _bmm_nt(rt, bt, P_OUT), 0.0)
    a_rk = jnp.where(incl, _bmm_nt(rt, kt, P_OUT), 0.0)
    tinv = _unit_lower_inverse(a_ab)
    u = _bmm(tinv, _bmm(at, g0, P_SOLVE) + _bmm(a_ak, v, P_SOLVE), P_SOLVE)
    return a_ab, a_ak, a_rb, a_rk, tinv, u


def _diag_col(row, n):
    ri = lax.broadcasted_iota(jnp.int32, (n, n), 0)
    ci = lax.broadcasted_iota(jnp.int32, (n, n), 1)
    return jnp.sum(jnp.where(ri == ci, row, 0.0), axis=2, keepdims=True)


def _diag_row(col, n):
    ri = lax.broadcasted_iota(jnp.int32, (n, n), 0)
    ci = lax.broadcasted_iota(jnp.int32, (n, n), 1)
    return jnp.sum(jnp.where(ri == ci, col, 0.0), axis=1, keepdims=True)


def _rwkv_scan_fwd(r, lw, k, v, a, b, hb):
    h, t, n = r.shape
    nc = t // CHUNK

    def body(r_ref, lw_ref, k_ref, v_ref, a_ref, b_ref, y_ref, st_ref, g_sc):
        @pl.when(pl.program_id(1) == 0)
        def _():
            g_sc[...] = jnp.zeros_like(g_sc)

        g0 = g_sc[...]
        st_ref[0] = g0
        vv = v_ref[...]
        g_last, rt, at, bt, kt, _, _, _ = _chunk_common(r_ref[...], lw_ref[...], k_ref[...], a_ref[...], b_ref[...])
        _, _, a_rb, a_rk, _, u = _chunk_solve(rt, at, bt, kt, vv, g0)
        y_ref[...] = _bmm(rt, g0, P_OUT) + _bmm(a_rb, u, P_OUT) + _bmm(a_rk, vv, P_OUT)
        z = g0 + _bmm_tn(bt, u, P_STATE) + _bmm_tn(kt, vv, P_STATE)
        g_sc[...] = _diag_col(g_last, n) * z

    blk = pl.BlockSpec((hb, CHUNK, n), lambda i, j: (i, j, 0))
    return pl.pallas_call(
        body, name="rwkv_scan_fwd", grid=(h // hb, nc),
        in_specs=[blk] * 6,
        out_specs=[blk, pl.BlockSpec((1, hb, n, n), lambda i, j: (j, i, 0, 0))],
        out_shape=[jax.ShapeDtypeStruct((h, t, n), F32), jax.ShapeDtypeStruct((nc, h, n, n), F32)],
        scratch_shapes=[pltpu.VMEM((hb, n, n), F32)],
        compiler_params=_params("parallel", "arbitrary"),
    )(r, lw, k, v, a, b)


def _rwkv_scan_bwd(r, lw, k, v, a, b, states, dy, hb):
    h, t, n = r.shape
    nc = t // CHUNK

    def body(r_ref, lw_ref, k_ref, v_ref, a_ref, b_ref, st_ref, dy_ref,
             dr_ref, dlw_ref, dk_ref, dv_ref, da_ref, db_ref, dg_sc):
        @pl.when(pl.program_id(1) == 0)
        def _():
            dg_sc[...] = jnp.zeros_like(dg_sc)

        g0 = st_ref[0]
        vv, dyv, dh = v_ref[...], dy_ref[...], dg_sc[...]
        lwv = lw_ref[...]
        g_last, rt, at, bt, kt, e_pos, e_neg, e_exc = _chunk_common(r_ref[...], lwv, k_ref[...], a_ref[...], b_ref[...])
        a_ab, a_ak, a_rb, a_rk, tinv, u = _chunk_solve(rt, at, bt, kt, vv, g0)
        strict, incl = _tri(CHUNK, True), _tri(CHUNK, False)
        gcol = _diag_col(g_last, n)
        z = g0 + _bmm_tn(bt, u, P_STATE) + _bmm_tn(kt, vv, P_STATE)
        dz = gcol * dh
        dc_last = _diag_row(jnp.sum(dh * gcol * z, axis=2, keepdims=True), n)
        g = P_GRAD
        du = _bmm_tn(a_rb, dyv, g) + _bmm(bt, dz, g)
        dx = _bmm_tn(tinv, du, P_SOLVE)
        dv_ref[...] = _bmm_tn(a_rk, dyv, g) + _bmm(kt, dz, g) + _bmm_tn(a_ak, dx, g)
        da_ab = jnp.where(strict, _bmm_nt(dx, u, g), 0.0)
        da_ak = jnp.where(strict, _bmm_nt(dx, vv, g), 0.0)
        da_rb = jnp.where(incl, _bmm_nt(dyv, u, g), 0.0)
        da_rk = jnp.where(incl, _bmm_nt(dyv, vv, g), 0.0)
        g = P_DECAY
        d_at = _bmm(da_ab, bt, g) + _bmm(da_ak, kt, g) + _bmm_nt(dx, g0, g)
        d_rt = _bmm(da_rb, bt, g) + _bmm(da_rk, kt, g) + _bmm_nt(dyv, g0, g)
        d_bt = _bmm_tn(da_ab, at, g) + _bmm_tn(da_rb, rt, g) + _bmm_nt(u, dz, g)
        d_kt = _bmm_tn(da_ak, at, g) + _bmm_tn(da_rk, rt, g) + _bmm_nt(vv, dz, g)
        dg_sc[...] = dz + _bmm_tn(rt, dyv, P_STATE) + _bmm_tn(at, dx, P_STATE)
        dr_ref[...] = d_rt * e_pos
        da_ref[...] = d_at * e_exc
        db_ref[...] = d_bt * e_neg
        dk_ref[...] = d_kt * e_neg
        last = lax.broadcasted_iota(jnp.int32, (CHUNK, n), 0) == CHUNK - 1
        dc = d_rt * rt - d_bt * bt - d_kt * kt + jnp.where(last, dc_last, 0.0)
        dce = d_at * at
        ri = lax.broadcasted_iota(jnp.int32, (CHUNK, CHUNK), 0)
        ci = lax.broadcasted_iota(jnp.int32, (CHUNK, CHUNK), 1)
        up_incl = jnp.broadcast_to((ri <= ci).astype(BF16), (hb, CHUNK, CHUNK))
        dlw_ref[...] = _bmm_01(up_incl, dc + dce) - dce

    rev = lambda i, j: (i, nc - 1 - j, 0)
    blk = pl.BlockSpec((hb, CHUNK, n), rev)
    return pl.pallas_call(
        body, name="rwkv_scan_bwd", grid=(h // hb, nc),
        in_specs=[blk] * 6 + [pl.BlockSpec((1, hb, n, n), lambda i, j: (nc - 1 - j, i, 0, 0)), blk],
        out_specs=[blk] * 6,
        out_shape=[jax.ShapeDtypeStruct((h, t, n), F32)] * 6,
        scratch_shapes=[pltpu.VMEM((hb, n, n), F32)],
        compiler_params=_params("parallel", "arbitrary"),
    )(r, lw, k, v, a, b, states, dy)


def _silu_grad(g):
    s = _sig(g)
    return s * (1.0 + g * (1.0 - s))


def _group_norm(ys):
    yc = ys - _head_sum(ys) * (1.0 / HEAD_DIM)
    rstd = lax.rsqrt(_head_sum(yc * yc) * (1.0 / HEAD_DIM) + GN_EPS)
    return yc * rstd, rstd


def _rwkv_post_fwd(ys, r, km, v, p, c, ln_w, ln_b, r_k):
    t = ys.shape[0]
    tm = _tile(t, 512, 8)
    goff = c.o_grw // WIDE

    def body(ys_ref, r_ref, km_ref, v_ref, g_ref, lw_ref, lb_ref, rk_ref, o_ref):
        yn, _ = _group_norm(ys_ref[...])
        s = _head_sum(r_ref[...] * km_ref[...] * rk_ref[...])
        g = g_ref[...]
        o_ref[...] = ((yn * lw_ref[...] + lb_ref[...] + s * v_ref[...]) * g * _sig(g)).astype(BF16)

    blk = pl.BlockSpec((tm, WIDE), lambda i, j: (i, j))
    vec = pl.BlockSpec((1, WIDE), lambda i, j: (0, j))
    return pl.pallas_call(
        body, name="rwkv_post_fwd", grid=(t // tm, c.rw // WIDE),
        in_specs=[blk] * 4 + [pl.BlockSpec((tm, WIDE), lambda i, j: (i, goff + j)), vec, vec, vec],
        out_specs=blk, out_shape=jax.ShapeDtypeStruct((t, c.d), BF16),
        compiler_params=_params("parallel", "parallel"),
    )(ys, r, km, v, p, ln_w, ln_b, r_k)


def _rwkv_post_bwd(dyc, ys, r, km, v, p, c, ln_w, ln_b, r_k):
    t = ys.shape[0]
    tm = _tile(t, 512, 8)
    goff = c.o_grw // WIDE

    def body(dy_ref, ys_ref, r_ref, km_ref, v_ref, g_ref, lw_ref, lb_ref, rk_ref,
             dys_ref, dr_ref, dkm_ref, dv_ref, dg_ref, dlw_ref, dlb_ref, drk_ref):
        @pl.when(pl.program_id(1) == 0)
        def _():
            for ref in (dlw_ref, dlb_ref, drk_ref):
                ref[...] = jnp.zeros_like(ref)

        yn, rstd = _group_norm(ys_ref[...])
        rv, kmv, vv, rk, g = r_ref[...], km_ref[...], v_ref[...], rk_ref[...], g_ref[...]
        s = _head_sum(rv * kmv * rk)
        y = yn * lw_ref[...] + lb_ref[...] + s * vv
        dyc = dy_ref[...]
        dg_ref[...] = (dyc * y * _silu_grad(g)).astype(BF16)
        dy = dyc * g * _sig(g)
        dlb_ref[...] += jnp.sum(dy, axis=0, keepdims=True)
        dlw_ref[...] += jnp.sum(dy * yn, axis=0, keepdims=True)
        dyn = dy * lw_ref[...]
        inv = 1.0 / HEAD_DIM
        dys_ref[...] = rstd * (dyn - _head_sum(dyn) * inv - yn * _head_sum(dyn * yn) * inv)
        ds = _head_sum(dy * vv)
        dv_ref[...] = dy * s
        dr_ref[...] = ds * kmv * rk
        dkm_ref[...] = ds * rv * rk
        drk_ref[...] += jnp.sum(ds * rv * kmv, axis=0, keepdims=True)

    blk = pl.BlockSpec((tm, WIDE), lambda j, i: (i, j))
    vec = pl.BlockSpec((1, WIDE), lambda j, i: (0, j))
    f = jax.ShapeDtypeStruct((t, c.rw), F32)
    s1 = jax.ShapeDtypeStruct((1, c.rw), F32)
    gate = pl.BlockSpec((tm, WIDE), lambda j, i: (i, goff + j))
    return pl.pallas_call(
        body, name="rwkv_post_bwd", grid=(c.rw // WIDE, t // tm),
        in_specs=[blk] * 5 + [gate, vec, vec, vec],
        out_specs=[blk] * 4 + [gate] + [vec] * 3,
        out_shape=[f, f, f, f, jax.ShapeDtypeStruct((t, c.wp), BF16), s1, s1, s1],
        compiler_params=_params("parallel", "arbitrary"),
    )(dyc, ys, r, km, v, p, ln_w, ln_b, r_k)


def _gate_fwd(y, p, goff, name, ycat, yoff):
    t, w = y.shape
    tm = _tile(t, 512, 8)
    gb, ob = goff // WIDE, yoff // WIDE

    def body(y_ref, g_ref, o_ref):
        g = g_ref[...]
        o_ref[...] = (y_ref[...] * g * _sig(g)).astype(BF16)

    blk = pl.BlockSpec((tm, WIDE), lambda i, j: (i, j))
    return _pallas_into(
        body, ycat, 2, 0, name=name, grid=(t // tm, w // WIDE),
        in_specs=[blk, pl.BlockSpec((tm, WIDE), lambda i, j: (i, gb + j))],
        out_specs=pl.BlockSpec((tm, WIDE), lambda i, j: (i, ob + j)),
        out_shape=jax.ShapeDtypeStruct(ycat.shape, BF16),
        compiler_params=_params("parallel", "parallel"),
    )(y, p)


def _gate_bwd(dyc, yoff, y, p, goff, name, dp):
    t, w = y.shape
    tm = _tile(t, 512, 8)
    gb, yb = goff // WIDE, yoff // WIDE

    def body(d_ref, y_ref, g_ref, dy_ref, dg_ref):
        g, d = g_ref[...], d_ref[...]
        dy_ref[...] = d * g * _sig(g)
        dg_ref[...] = (d * y_ref[...] * _silu_grad(g)).astype(BF16)

    blk = pl.BlockSpec((tm, WIDE), lambda i, j: (i, j))
    gate = pl.BlockSpec((tm, WIDE), lambda i, j: (i, gb + j))
    return _pallas_into(
        body, dp, 3, 1, name=name, grid=(t // tm, w // WIDE),
        in_specs=[pl.BlockSpec((tm, WIDE), lambda i, j: (i, yb + j)), blk, gate],
        out_specs=[blk, gate],
        out_shape=[jax.ShapeDtypeStruct((t, w), F32), jax.ShapeDtypeStruct(dp.shape, BF16)],
        compiler_params=_params("parallel", "parallel"),
    )(dyc, y, p)


NEG = -1e30


def _fox_logit_bwd(dcum, p, c, b_f):
    t = p.shape[0]
    tm = _tile(t, 512, 8)
    fb = c.o_fl // LANES
    nt = t // tm

    def body(d_ref, f_ref, b_ref, o_ref, db_ref, carry):
        @pl.when(pl.program_id(0) == 0)
        def _():
            carry[...] = jnp.zeros_like(carry)
            db_ref[...] = jnp.zeros_like(db_ref)

        d = d_ref[0] + d_ref[1]
        dlogf = _mm(_tri(tm, False).astype(F32).T, d, HI) + carry[...]
        carry[...] += jnp.sum(d, axis=0, keepdims=True)
        df = dlogf * _sig(-(f_ref[...] + b_ref[...]))
        o_ref[...] = df.astype(BF16)
        db_ref[...] += jnp.sum(df, axis=0, keepdims=True)

    return pl.pallas_call(
        body, name="fox_logit_bwd", grid=(nt,),
        in_specs=[pl.BlockSpec((2, tm, LANES), lambda i: (0, nt - 1 - i, 0)),
                  pl.BlockSpec((tm, LANES), lambda i: (nt - 1 - i, fb)),
                  pl.BlockSpec((1, LANES), lambda i: (0, 0))],
        out_specs=[pl.BlockSpec((tm, LANES), lambda i: (nt - 1 - i, 0)), pl.BlockSpec((1, LANES), lambda i: (0, 0))],
        out_shape=[jax.ShapeDtypeStruct((t, LANES), BF16), jax.ShapeDtypeStruct((1, LANES), F32)],
        scratch_shapes=[pltpu.VMEM((1, LANES), F32)],
        compiler_params=_params("arbitrary"),
    )(dcum, p, b_f)


FOX_PAIRS = 2
FOX_HEADS_STEP = 2 * FOX_PAIRS


def _lane_half(shape, upper):
    li = lax.broadcasted_iota(jnp.int32, shape, len(shape) - 1)
    return (li >= HEAD_DIM) if upper else (li < HEAD_DIM)


def _col(block, j):
    li = lax.broadcasted_iota(jnp.int32, block.shape, 1)
    return jnp.sum(jnp.where(li == j, block, 0.0), axis=1, keepdims=True)


def _from_cols(cols):
    li = lax.broadcasted_iota(jnp.int32, (cols[0].shape[0], len(cols)), 1)
    out = jnp.zeros(li.shape, F32)
    for j, cj in enumerate(cols):
        out = jnp.where(li == j, cj, out)
    return out


def _from_rows(rows):
    si = lax.broadcasted_iota(jnp.int32, (len(rows), rows[0].shape[1]), 0)
    out = jnp.zeros(si.shape, F32)
    for j, rj in enumerate(rows):
        out = jnp.where(si == j, rj, out)
    return out


def _causal(tq, tk):
    return lax.broadcasted_iota(jnp.int32, (tq, tk), 1) <= lax.broadcasted_iota(jnp.int32, (tq, tk), 0)


def _fox_prep_t(p, c, b_f):
    t = p.shape[0]
    tm = _tile(t, 512, LANES)
    fb = c.o_fl // LANES

    def body(f_ref, b_ref, o_ref, carry):
        @pl.when(pl.program_id(0) == 0)
        def _():
            carry[...] = jnp.zeros_like(carry)

        logf = -_softplus(-(f_ref[...] + b_ref[...]))
        cum = _mm(_tri(tm, False).astype(F32), logf, HI) + carry[...]
        o_ref[...] = cum.T
        carry[...] += jnp.sum(logf, axis=0, keepdims=True)

    return pl.pallas_call(
        body, name="fox_prep", grid=(t // tm,),
        in_specs=[pl.BlockSpec((tm, LANES), lambda i: (i, fb)), pl.BlockSpec((1, LANES), lambda i: (0, 0))],
        out_specs=pl.BlockSpec((LANES, tm), lambda i: (0, i)),
        out_shape=jax.ShapeDtypeStruct((LANES, t), F32),
        scratch_shapes=[pltpu.VMEM((1, LANES), F32)],
        compiler_params=_params("arbitrary"),
    )(p, b_f)


def _fox2_fwd(p, c, cum_t, tb, ycat):
    t = p.shape[0]
    tq = tk = _tile(t, tb, LANES)
    nq = t // tq
    pw, nh = FOX_PAIRS * LANES, FOX_HEADS_STEP
    qb, kb, vb, gb = (o // pw for o in (c.o_fq, c.o_fk, c.o_fv, c.o_gfox))
    scale = HEAD_DIM ** -0.5

    def body(q_ref, k_ref, v_ref, g_ref, ck_ref, o_ref, y_ref, lse_ref, m_sc, l_sc, acc_sc):
        g, qi, ki = pl.program_id(0), pl.program_id(1), pl.program_id(2)

        @pl.when(ki == 0)
        def _():
            m_sc[...] = jnp.full_like(m_sc, NEG)
            l_sc[...] = jnp.zeros_like(l_sc)
            acc_sc[...] = jnp.zeros_like(acc_sc)

        def step(diag):
            ms, ls = [m_sc[h] for h in range(nh)], [l_sc[h] for h in range(nh)]
            accs = [acc_sc[:, pi * LANES:(pi + 1) * LANES] for pi in range(FOX_PAIRS)]
            for pi in range(FOX_PAIRS):
                lanes = slice(pi * LANES, (pi + 1) * LANES)
                q2 = (q_ref[:, lanes] * scale).astype(BF16)
                k2, v2 = k_ref[:, lanes].astype(BF16), v_ref[:, lanes].astype(BF16)
                new_acc = accs[pi]
                for hh in range(2):
                    hi = 2 * pi + hh
                    mk = _lane_half((tq, LANES), hh == 1)
                    s = _mm_nt(jnp.where(mk, q2, jnp.zeros_like(q2)), k2) - ck_ref[pl.ds(g * nh + hi, 1), :]
                    if diag:
                        s = jnp.where(_causal(tq, tk), s, NEG)
                    m_new = jnp.maximum(ms[hi], jnp.max(s, axis=1, keepdims=True))
                    a = jnp.exp(ms[hi] - m_new)
                    e = jnp.exp(s - jnp.concatenate([m_new] * (tk // LANES), axis=1))
                    ls[hi] = a * ls[hi] + jnp.sum(e, axis=1, keepdims=True)
                    ms[hi] = m_new
                    new_acc = jnp.where(mk, a * accs[pi] + _mm(e.astype(BF16), v2), new_acc)
                accs[pi] = new_acc
            for h in range(nh):
                m_sc[h] = ms[h]
                l_sc[h] = ls[h]
            for pi in range(FOX_PAIRS):
                acc_sc[:, pi * LANES:(pi + 1) * LANES] = accs[pi]

        @pl.when(ki < qi)
        def _():
            step(False)

        @pl.when(ki == qi)
        def _():
            step(True)
            li = lax.broadcasted_iota(jnp.int32, (tq, LANES), 1)
            lse = jnp.zeros((tq, LANES), F32)
            for pi in range(FOX_PAIRS):
                lanes = slice(pi * LANES, (pi + 1) * LANES)
                inv = jnp.where(_lane_half((tq, LANES), False), 1.0 / l_sc[2 * pi], 1.0 / l_sc[2 * pi + 1])
                o = acc_sc[:, lanes] * inv
                gate = g_ref[:, lanes]
                o_ref[:, lanes] = o
                y_ref[:, lanes] = (o * gate * _sig(gate)).astype(BF16)
            for h in range(nh):
                lse = jnp.where(li == h, m_sc[h] + jnp.log(l_sc[h]), lse)
            lse_ref[0] = lse

    row = lambda off: pl.BlockSpec((tq, pw), lambda g, i, j: (i, off + g))
    key = lambda off: pl.BlockSpec((tk, pw), lambda g, i, j: (jnp.minimum(i, j), off + g))
    out = pl.BlockSpec((tq, pw), lambda g, i, j: (i, g))
    return _pallas_into(
        body, ycat, 5, 1, name="fox_fwd", grid=(c.rw // pw, nq, nq),
        in_specs=[row(qb), key(kb), key(vb), row(gb),
                  pl.BlockSpec((LANES, tk), lambda g, i, j: (0, jnp.minimum(i, j)))],
        out_specs=[out, row(c.rw // pw), pl.BlockSpec((1, tq, LANES), lambda g, i, j: (g, i, 0))],
        out_shape=[jax.ShapeDtypeStruct((t, c.rw), F32), jax.ShapeDtypeStruct(ycat.shape, BF16),
                   jax.ShapeDtypeStruct((c.rw // pw, t, LANES), F32)],
        scratch_shapes=[pltpu.VMEM((nh, tq, LANES), F32), pltpu.VMEM((nh, tq, LANES), F32),
                        pltpu.VMEM((tq, pw), F32)],
        compiler_params=_params("parallel", "parallel", "arbitrary"),
    )(p, p, p, p, cum_t)


def _fox2_grads(q2, k2, v2, do2, o2, lse_h, ck, mk, diag, tq, tk):
    zero = jnp.zeros_like(q2)
    s = _mm_nt(jnp.where(mk, q2, zero), k2) - ck
    if diag:
        s = jnp.where(_causal(tq, tk), s, NEG)
    wide = lambda col: jnp.concatenate([jnp.broadcast_to(col, (tq, LANES))] * (tk // LANES), axis=1)
    pm = jnp.exp(s - wide(lse_h))
    delta = jnp.sum(jnp.where(mk, do2 * o2, 0.0), axis=1, keepdims=True)
    dob = do2.astype(BF16)
    dp = _mm_nt(jnp.where(mk, dob, zero), v2)
    return pm, pm * (dp - wide(delta)), dob


def _fox2_bwd(p, c, cum_t, lse, o, do, tb, dp):
    t = p.shape[0]
    tq = tk = _tile(t, tb, LANES)
    nq = t // tq
    pw, nh = FOX_PAIRS * LANES, FOX_HEADS_STEP
    ng = c.rw // pw
    qb, kb, vb = (o_ // pw for o_ in (c.o_fq, c.o_fk, c.o_fv))
    scale = HEAD_DIM ** -0.5

    def body(q_ref, k_ref, v_ref, ck_ref, lse_ref, o_ref, do_ref,
             dk_ref, dv_ref, dck_ref, dqp_ref, dcqp_ref, dk_sc, dv_sc, dc_sc):
        g, ki, qi = pl.program_id(0), pl.program_id(1), pl.program_id(2)

        @pl.when(qi == 0)
        def _():
            dk_sc[...] = jnp.zeros_like(dk_sc)
            dv_sc[...] = jnp.zeros_like(dv_sc)
            dc_sc[...] = jnp.zeros_like(dc_sc)

        def step(diag):
            lse_blk = lse_ref[0]
            dcs = [dc_sc[h] for h in range(nh)]
            dks = [dk_sc[:, pi * LANES:(pi + 1) * LANES] for pi in range(FOX_PAIRS)]
            dvs = [dv_sc[:, pi * LANES:(pi + 1) * LANES] for pi in range(FOX_PAIRS)]
            rows = []
            for pi in range(FOX_PAIRS):
                lanes = slice(pi * LANES, (pi + 1) * LANES)
                q2 = (q_ref[:, lanes] * scale).astype(BF16)
                k2, v2 = k_ref[:, lanes].astype(BF16), v_ref[:, lanes].astype(BF16)
                do2, o2 = do_ref[:, lanes], o_ref[:, lanes]
                new_dk, new_dv, dq2 = dks[pi], dvs[pi], None
                for hh in range(2):
                    hi = 2 * pi + hh
                    mk = _lane_half((tk, LANES), hh == 1)
                    pm, ds, dob = _fox2_grads(q2, k2, v2, do2, o2, _col(lse_blk, hi),
                                              ck_ref[pl.ds(g * nh + hi, 1), :], mk, diag, tq, tk)
                    dsb = ds.astype(BF16)
                    dcs[hi] = dcs[hi] - jnp.sum(ds, axis=0, keepdims=True)
                    rows.append(jnp.sum(ds, axis=1, keepdims=True))
                    new_dv = jnp.where(mk, dvs[pi] + _mm_tn(pm.astype(BF16), dob), new_dv)
                    new_dk = jnp.where(mk, dks[pi] + _mm_tn(dsb, q2), new_dk)
                    part = _mm(dsb, k2)
                    dq2 = part if hh == 0 else jnp.where(mk, part, dq2)
                dks[pi], dvs[pi] = new_dk, new_dv
                dqp_ref[0, :, lanes] = dq2 * scale
            dcqp_ref[0, 0] = _from_cols(rows)
            for h in range(nh):
                dc_sc[h] = dcs[h]
            for pi in range(FOX_PAIRS):
                dk_sc[:, pi * LANES:(pi + 1) * LANES] = dks[pi]
                dv_sc[:, pi * LANES:(pi + 1) * LANES] = dvs[pi]

        @pl.when(qi > ki)
        def _():
            step(False)

        @pl.when(qi == ki)
        def _():
            step(True)

        @pl.when(qi == nq - 1)
        def _():
            dk_ref[...] = dk_sc[...].astype(BF16)
            dv_ref[...] = dv_sc[...].astype(BF16)
            dck_ref[0] = _from_rows([dc_sc[h] for h in range(nh)])

    row = lambda off: pl.BlockSpec((tq, pw), lambda g, j, i: (jnp.maximum(i, j), off + g))
    key = lambda off: pl.BlockSpec((tk, pw), lambda g, j, i: (j, off + g))
    return _pallas_into(
        body, dp, 7, 0, name="fox_bwd", grid=(ng, nq, nq),
        in_specs=[row(qb), key(kb), key(vb), pl.BlockSpec((LANES, tk), lambda g, j, i: (0, j)),
                  pl.BlockSpec((1, tq, LANES), lambda g, j, i: (g, jnp.maximum(i, j), 0)), row(0), row(0)],
        out_specs=[key(kb), key(0), pl.BlockSpec((1, nh, tk), lambda g, j, i: (g, 0, j)),
                   pl.BlockSpec((1, tq, pw), lambda g, j, i: (j, jnp.maximum(i, j), g)),
                   pl.BlockSpec((1, 1, tq, nh), lambda g, j, i: (j, g, jnp.maximum(i, j), 0))],
        out_shape=[jax.ShapeDtypeStruct(dp.shape, BF16), jax.ShapeDtypeStruct((t, c.rw), BF16),
                   jax.ShapeDtypeStruct((ng, nh, t), F32), jax.ShapeDtypeStruct((nq, t, c.rw), F32),
                   jax.ShapeDtypeStruct((nq, ng, t, nh), F32)],
        scratch_shapes=[pltpu.VMEM((tk, pw), F32), pltpu.VMEM((tk, pw), F32), pltpu.VMEM((nh, 1, tk), F32)],
        compiler_params=_params("parallel", "parallel", "arbitrary"),
    )(p, p, p, cum_t, lse, o, do)


def _fox2_dq_sum(dq_part, dcq_part, tq):
    nk, t, rw = dq_part.shape
    ng, nh = dcq_part.shape[1], dcq_part.shape[3]

    def body(p_ref, c_ref, dq_ref, dcq_ref, acc, cacc):
        i, j = pl.program_id(0), pl.program_id(1)

        @pl.when(j == 0)
        def _():
            acc[...] = p_ref[0]
            cacc[...] = c_ref[0]

        @pl.when((j > 0) & (j <= i))
        def _():
            acc[...] += p_ref[0]
            cacc[...] += c_ref[0]

        @pl.when(j == nk - 1)
        def _():
            dq_ref[...] = acc[...].astype(BF16)
            dcq_ref[...] = cacc[...]

    return pl.pallas_call(
        body, name="fox_dq_sum", grid=(t // tq, nk),
        in_specs=[pl.BlockSpec((1, tq, rw), lambda i, j: (jnp.minimum(i, j), i, 0)),
                  pl.BlockSpec((1, ng, tq, nh), lambda i, j: (jnp.minimum(i, j), 0, i, 0))],
        out_specs=[pl.BlockSpec((tq, rw), lambda i, j: (i, 0)), pl.BlockSpec((ng, tq, nh), lambda i, j: (0, i, 0))],
        out_shape=[jax.ShapeDtypeStruct((t, rw), BF16), jax.ShapeDtypeStruct((ng, t, nh), F32)],
        scratch_shapes=[pltpu.VMEM((tq, rw), F32), pltpu.VMEM((ng, tq, nh), F32)],
        compiler_params=_params("parallel", "arbitrary"),
    )(dq_part, dcq_part)


def _mem_probs(q, mk, scale):
    s = _mm_nt(q.astype(BF16), mk.astype(BF16)) * scale
    e = jnp.exp(s - jnp.max(s, axis=1, keepdims=True))
    return e / jnp.sum(e, axis=1, keepdims=True)


def _mem_attn_fwd(p, c, mkv):
    t = p.shape[0]
    tm = _tile(t, 512, 8)
    dh = c.mhd
    qb = c.o_mq // dh
    scale = dh ** -0.5

    def body(q_ref, mk_ref, mv_ref, o_ref):
        pm = _mem_probs(q_ref[...], mk_ref[...], scale)
        o_ref[...] = _mm(pm.astype(BF16), mv_ref[...].astype(BF16))

    m = mkv.shape[0]
    return pl.pallas_call(
        body, name="mem_attn_fwd", grid=(t // tm, MEM_HEADS),
        in_specs=[pl.BlockSpec((tm, dh), lambda i, j: (i, qb + j)),
                  pl.BlockSpec((m, dh), lambda i, j: (0, j)),
                  pl.BlockSpec((m, dh), lambda i, j: (0, MEM_HEADS + j))],
        out_specs=pl.BlockSpec((tm, dh), lambda i, j: (i, j)),
        out_shape=jax.ShapeDtypeStruct((t, c.mw), F32),
        compiler_params=_params("parallel", "parallel"),
    )(p, mkv, mkv)


def _mem_attn_bwd(p, c, mkv, do):
    t = p.shape[0]
    tm = _tile(t, 512, 8)
    dh = c.mhd
    qb = c.o_mq // dh
    scale = dh ** -0.5
    m = mkv.shape[0]

    def body(q_ref, mk_ref, mv_ref, do_ref, dq_ref, dmk_ref, dmv_ref):
        @pl.when(pl.program_id(1) == 0)
        def _():
            dmk_ref[...] = jnp.zeros_like(dmk_ref)
            dmv_ref[...] = jnp.zeros_like(dmv_ref)

        qv = q_ref[...].astype(BF16)
        pm = _mem_probs(qv, mk_ref[...], scale)
        dob = do_ref[...].astype(BF16)
        dmv_ref[...] += _mm_tn(pm.astype(BF16), dob)
        dp = _mm_nt(dob, mv_ref[...].astype(BF16))
        ds = (pm * (dp - jnp.sum(pm * dp, axis=1, keepdims=True)) * scale).astype(BF16)
        dq_ref[...] = _mm(ds, mk_ref[...].astype(BF16)).astype(BF16)
        dmk_ref[...] += _mm_tn(ds, qv)

    kvb = lambda off: pl.BlockSpec((m, dh), lambda j, i: (0, off + j))
    return pl.pallas_call(
        body, name="mem_attn_bwd", grid=(MEM_HEADS, t // tm),
        in_specs=[pl.BlockSpec((tm, dh), lambda j, i: (i, qb + j)), kvb(0), kvb(MEM_HEADS),
                  pl.BlockSpec((tm, dh), lambda j, i: (i, j))],
        out_specs=[pl.BlockSpec((tm, dh), lambda j, i: (i, j)), kvb(0), kvb(0)],
        out_shape=[jax.ShapeDtypeStruct((t, c.mw), BF16), jax.ShapeDtypeStruct((m, c.mw), F32),
                   jax.ShapeDtypeStruct((m, c.mw), F32)],
        compiler_params=_params("parallel", "arbitrary"),
    )(p, mkv, mkv, do)


def _adamw(w, g, m, v, name):
    rows, cols = w.shape
    bc1 = 1.0 - ADAM_B1 ** ADAM_STEP
    bc2 = 1.0 - ADAM_B2 ** ADAM_STEP
    if rows % 8 and rows > 8:
        blk = pl.BlockSpec((rows, LANES), lambda i: (0, i))
        g_blk = pl.BlockSpec((g.shape[0], LANES), lambda i: (0, i))
        grid = (cols // LANES,)
    else:
        tm = _tile(rows, max(8, (1 << 18) // cols // 8 * 8), 8)
        blk = pl.BlockSpec((tm, cols), lambda i: (i, 0))
        g_blk = pl.BlockSpec((tm, g.shape[1]), lambda i: (i, 0))
        grid = (rows // tm,)
    brows, bcols = blk.block_shape

    def body(w_ref, g_ref, m_ref, v_ref, go_ref, d_ref, mo_ref, vo_ref):
        gv = g_ref[0:brows, 0:bcols]
        mn = ADAM_B1 * m_ref[...] + (1.0 - ADAM_B1) * gv
        vn = ADAM_B2 * v_ref[...] + (1.0 - ADAM_B2) * (gv * gv)
        go_ref[...] = gv
        mo_ref[...] = mn
        vo_ref[...] = vn
        d_ref[...] = -ADAM_LR * ((mn / bc1) / (jnp.sqrt(vn / bc2) + ADAM_EPS) + ADAM_WD * w_ref[...])

    shp = jax.ShapeDtypeStruct((rows, cols), F32)
    return pl.pallas_call(
        body, name=name, grid=grid,
        in_specs=[blk, g_blk, blk, blk],
        out_specs=[blk] * 4, out_shape=[shp] * 4,
        compiler_params=_params("parallel"),
    )(w, g, m, v)


SCAN_HEADS = 12
FOX_BLOCK = 512


def _local_step(c, x, mem, tgt, w, riders=None):
    t = x.shape[0]
    rw = c.rw
    riders = riders or {}
    carried = {}
    hd = lambda z: z.reshape(t, c.h, HEAD_DIM).transpose(1, 0, 2)
    uh = lambda z: z.transpose(1, 0, 2).reshape(t, rw)
    vecs = (w["mu"], w["w0"], w["a0"], w["k_k"], w["k_a"], w["wd"], w["wi"])

    h, rinv = _rms_fwd(x, w["g_pre"], "rms_pre")
    if "in_proj" in riders:
        groups, finish = riders["in_proj"]
        p, late = _matmul(h, w["wp"], name="in_proj", tk=4096, attach=groups)
        w = dict(w, **finish(late))
    else:
        p = _matmul(h, w["wp"], name="in_proj", tk=4096)
    r, lw, km, v, a, b = _rwkv_pre_fwd(p, c, *vecs)
    scan_in = tuple(hd(z) for z in (r, lw, km, v, a, b))
    hb = max(n for n in range(1, SCAN_HEADS + 1) if c.h % n == 0)
    ysh, states = _rwkv_scan_fwd(*scan_in, hb)
    ys = uh(ysh)
    ycat = _rwkv_post_fwd(ys, r, km, v, p, c, w["ln_w"], w["ln_b"], w["r_k"])

    cum_t = _fox_prep_t(p, c, w["b_f"])
    yfox, ycat, lse = _fox2_fwd(p, c, cum_t, FOX_BLOCK, ycat)

    memn, rinv_m = _rms_fwd(mem, w["g_mem"], "rms_mem")
    mkv = _matmul(memn, w["w_mem_kv"], name="mem_kv")
    ymem = _mem_attn_fwd(p, c, mkv)
    ycat = _gate_fwd(ymem, p, c.o_gmq, "gate_mem", ycat, 2 * rw)
    yo =_matmul(ycat, w["w_out"], name="out_proj", tn=512, tk=4096)
    loss, dout, dyo, dg_post = _post_loss(yo, x, tgt, w["g_post"], "post_loss")

    dyc = _matmul(dyo, w["w_out"], tb=True, name="d_ycat", tn=512, tk=4096)
    dw_out = _matmul(ycat, dyo, ta=True, name="d_w_out", tn=512, tk=4096, out_dtype=BF16)
    dys, dr2, dkm2, dv2, dp, dln_w, dln_b, dr_k = _rwkv_post_bwd(
        dyc, ys, r, km, v, p, c, w["ln_w"], w["ln_b"], w["r_k"])
    dyf, dp = _gate_bwd(dyc, rw, yfox, p, c.o_gfox, "gate_fox_bwd", dp)
    dym, dp = _gate_bwd(dyc, 2 * rw, ymem, p, c.o_gmq, "gate_mem_bwd", dp)

    scan_g = _rwkv_scan_bwd(*scan_in, states, hd(dys), hb)
    dps, dzw, dza, twb, alb, dw0, da0, dk_k, dk_a = _rwkv_pre_bwd(
        p, c, *vecs, *(uh(z) for z in scan_g), dr2, dkm2, dv2)
    dwd = _matmul(twb, dzw, ta=True, name="d_w_decay", out_dtype=BF16)
    dwi = _matmul(alb, dza, ta=True, name="d_w_iclr", out_dtype=BF16)
    dp, dmu = _shift_bwd(dps, p, c, w["mu"], dp)

    dp, dfv, dck, dq_part, dcq_part = _fox2_bwd(p, c, cum_t, lse, yfox, dyf, FOX_BLOCK, dp)
    dfq, dcq = _fox2_dq_sum(dq_part, dcq_part, _tile(t, FOX_BLOCK, LANES))
    dcum = jnp.pad(jnp.stack([dcq.transpose(1, 0, 2).reshape(t, c.h), dck.reshape(c.h, t).T]),
                   ((0, 0), (0, 0), (0, LANES - c.h)))
    dfl, db_f = _fox_logit_bwd(dcum, p, c, w["b_f"])

    dmq, dmk, dmv = _mem_attn_bwd(p, c, mkv, dym)
    dmkv = jnp.concatenate([dmk, dmv], axis=1)
    dw_mkv = _matmul(memn, dmkv, ta=True, name="d_w_mem_kv", out_dtype=BF16)
    dmemn = _matmul(dmkv, w["w_mem_kv"], tb=True, name="d_memn")
    _, dg_mem = _rms_bwd(dmemn, mem, rinv_m, w["g_mem"], jnp.zeros_like(mem), "rms_mem_bwd")

    for off, piece in ((c.o_fq, dfq), (c.o_fv, dfv), (c.o_mq, dmq), (c.o_fl, dfl)):
        dp = lax.dynamic_update_slice(dp, piece, (0, off))
    rest = dict(wd=dwd, wi=dwi, w_mem_kv=dw_mkv, w_out=dw_out)
    if "d_w_in" in riders:
        dwp, carried["rest"] = _matmul(dp, h, ta=True, name="d_w_in", tk=4096, out_dtype=BF16,
                                       attach=riders["d_w_in"](rest))
    else:
        dwp = _matmul(dp, h, ta=True, name="d_w_in", tk=4096, out_dtype=BF16)
    if "d_h" in riders:
        dh, carried["wp"] = _matmul(dp, w["wp"], tb=True, name="d_h", tk=2944, attach=riders["d_h"](dwp))
    else:
        dh = _matmul(dp, w["wp"], tb=True, name="d_h", tk=2944)
    grad_x, dg_pre = _rms_bwd(dh, x, rinv, w["g_pre"], dout, "rms_pre_bwd")

    small = dict(g_pre=dg_pre, mu=dmu, w0=dw0, a0=da0, k_k=dk_k, k_a=dk_a, r_k=dr_k, ln_w=dln_w, ln_b=dln_b,
                 b_f=db_f, g_mem=dg_mem, g_post=dg_post)
    return loss, grad_x, dict(wp=dwp, **rest), small, carried


CHIPS = ((1, 0, 0), (0, 1, 0), (1, 1, 0))
SIBLING = ((0, 0, 1),)
ALL_PEERS = tuple((i, j, k) for i in (0, 1) for j in (0, 1) for k in (0, 1))[1:]


def _chip_of(pos):
    return 2 * pos[0] + pos[1]


DMA_CHUNK = 4 << 20


def _pieces(shape, itemsize):
    lead, (rows, cols) = shape[:-2], shape[-2:]
    k = 1
    if rows % 16 == 0:
        k = max(1, min(rows // 16, -(-rows * cols * itemsize // DMA_CHUNK)))
        while rows % k or (rows // k) % 16:
            k -= 1
    band = rows // k
    idxs = [()]
    for n in lead:
        idxs = [i + (j,) for i in idxs for j in range(n)]
    return [i + (pl.ds(j * band, band),) for i in idxs for j in range(k)]


def _peer_of(me, mask):
    return tuple(1 - v if f else v for v, f in zip(me, mask))


def _exchange(name, groups):
    n = len(groups)
    plan = _plan(groups)

    def body(*refs):
        copies = _copies(groups, plan, refs[:n], refs[n:2 * n], refs[2 * n], refs[2 * n + 1])
        for cp in copies:
            cp.start()
        for cp in copies:
            cp.wait()

    any_spec = pl.BlockSpec(memory_space=pl.ANY)
    return pl.pallas_call(
        body, name=name,
        in_specs=[any_spec] * n, out_specs=[any_spec] * n,
        out_shape=_exchange_shapes(groups),
        input_output_aliases={gi: gi for gi, g in enumerate(groups) if g.get("inplace")},
        scratch_shapes=[pltpu.SemaphoreType.DMA((len(plan),)), pltpu.SemaphoreType.DMA((len(plan),))],
    )(*[g["src"] for g in groups])


def _plan(groups):
    return [(gi, ti, idx) for gi, g in enumerate(groups) for ti in range(len(g["transfers"]))
            for idx in _pieces(tuple(g["piece"]), g["src"].dtype.itemsize)]


def _exchange_shapes(groups):
    lead = lambda s: tuple(s) if isinstance(s, tuple) else (s,)
    return [jax.ShapeDtypeStruct(lead(g["slots"]) + tuple(g["piece"]), g["src"].dtype) for g in groups]


def _copies(groups, plan, srcs, outs, send_sems, recv_sems):
    me = (lax.axis_index("x"), lax.axis_index("y"), lax.axis_index("c"))
    copies = []
    for k, (gi, ti, idx) in enumerate(plan):
        mask, view, slot = groups[gi]["transfers"][ti]
        peer = _peer_of(me, mask)
        copies.append(pltpu.make_async_remote_copy(
            src_ref=view(srcs[gi], me, peer).at[idx], dst_ref=outs[gi].at[slot(me, peer)].at[idx],
            send_sem=send_sems.at[k], recv_sem=recv_sems.at[k],
            device_id=peer, device_id_type=MESH))
    return copies


def _my_chip():
    return 2 * lax.axis_index("x") + lax.axis_index("y")


def _put(buf, block, slot):
    return lax.dynamic_update_slice(buf, block[None], (slot,) + (0,) * block.ndim)


def _sum_slots(recv, own, k, out_dtype, name):
    s, rows, cols = recv.shape
    budget = max(16, ((4 << 20) // ((s + 1) * cols * 4)) // 16 * 16)
    tr = _tile(rows, budget, 16)
    own_many = own.shape[0] > 1

    def body(k_ref, *refs):
        out_ref = refs[s + 1]
        mine = refs[s][0].astype(F32)
        acc = None
        for i in range(s):
            term = jnp.where(k_ref[0] == i, mine, refs[i][0].astype(F32))
            acc = term if acc is None else acc + term
        out_ref[...] = acc.astype(out_ref.dtype)

    def slot_spec(i):
        return pl.BlockSpec((1, tr, cols), lambda j, kr: (jnp.where(kr[0] == i, (i + 1) % s, i), j, 0))

    grid_spec = pltpu.PrefetchScalarGridSpec(
        num_scalar_prefetch=1, grid=(rows // tr,),
        in_specs=[slot_spec(i) for i in range(s)]
                 + [pl.BlockSpec((1, tr, cols), lambda j, kr: (kr[0] if own_many else 0, j, 0))],
        out_specs=pl.BlockSpec((tr, cols), lambda j, kr: (j, 0)))
    return pl.pallas_call(
        body, name=name, grid_spec=grid_spec,
        out_shape=jax.ShapeDtypeStruct((rows, cols), out_dtype),
        compiler_params=_params("parallel"),
    )(k, *([recv] * s), own)


def _all_gather(shards):
    return _gather_finish(shards, _exchange("gather_chips", _gather_groups(shards)), "gather_pair")


def _gather_groups(shards):
    halves = [s.reshape(2, s.shape[0] // 2, s.shape[1]) for s in shards]
    return [dict(src=q, slots=(4, 2), piece=q.shape[1:],
                 transfers=[(m, lambda ref, me, peer: ref.at[me[2]], lambda me, peer: (_chip_of(me), me[2]))
                            for m in CHIPS])
            for q in halves]


def _gather_finish(shards, first, name):
    spot = lambda m: (lambda me: (_chip_of(_peer_of(me, m)), me[2]))
    both = _exchange(name, [
        dict(src=q, slots=(4, 2), piece=q.shape[2:], inplace=True,
             transfers=[(SIBLING[0], (lambda f: lambda ref, me, peer: ref.at[f(me)])(spot(m)),
                         (lambda f: lambda me, peer: f(me))(spot(m))) for m in CHIPS])
        for q in first])
    return [_put(q.reshape((4,) + s.shape), s, _my_chip()) for s, q in zip(shards, both)]


def _reduce_pair(partials, tag):
    core1 = lax.axis_index("c").reshape(1).astype(jnp.int32)
    halves = [q.reshape(4, 2, q.shape[1] // 2, q.shape[2]).transpose(1, 0, 2, 3) for q in partials]
    pair = _exchange("reduce_pair_" + tag, [
        dict(src=q, slots=2, piece=q.shape[1:],
             transfers=[(SIBLING[0], lambda ref, me, peer: ref.at[peer[2]], lambda me, peer: me[2])])
        for q in halves])
    flat = lambda e: e.reshape(2, -1, e.shape[-1])
    return [_sum_slots(flat(e), flat(q), core1, BF16, "reduce_pair_sum_" + tag).reshape(q.shape[1:])
            for e, q in zip(pair, halves)]


def _reduce_chips_groups(chip_sums):
    return [dict(src=q, slots=4, piece=q.shape[1:],
                 transfers=[(m, lambda ref, me, peer: ref.at[_chip_of(peer)], lambda me, peer: _chip_of(me))
                            for m in CHIPS])
            for q in chip_sums]


def _reduce_finish(crossed, chip_sums, tag):
    core = lax.axis_index("c")
    chip1 = _my_chip().reshape(1).astype(jnp.int32)
    sums = [_sum_slots(e, q, chip1, F32, "reduce_chips_sum_" + tag) for e, q in zip(crossed, chip_sums)]
    swapped = _exchange("reduce_swap_" + tag, [
        dict(src=q, slots=2, piece=q.shape, transfers=[(SIBLING[0], lambda ref, me, peer: ref, lambda me, peer: me[2])])
        for q in sums])
    return [_put(e, q, core).reshape(-1, e.shape[-1]) for e, q in zip(swapped, sums)]


def _reduce_scatter(partials):
    chip_sums = _reduce_pair(partials, "all")
    return _reduce_finish(_exchange("reduce_chips", _reduce_chips_groups(chip_sums)), chip_sums, "all")


def _all_reduce_small(vec):
    dev = 4 * lax.axis_index("x") + 2 * lax.axis_index("y") + lax.axis_index("c")
    got = _exchange("reduce_small", [
        dict(src=vec, slots=8, piece=vec.shape,
             transfers=[(m, lambda ref, me, peer: ref, lambda me, peer: 4 * me[0] + 2 * me[1] + me[2])
                        for m in ALL_PEERS])])[0]
    return _sum_slots(got, vec[None], dev.reshape(1).astype(jnp.int32), F32, "reduce_small_sum")


SMALL = ("g_pre", "mu", "w0", "a0", "k_k", "k_a", "r_k", "ln_w", "ln_b", "b_f", "g_mem", "g_post")


def _pad_cols(a, n):
    return jnp.pad(a, ((0, 0),) * (a.ndim - 1) + ((0, n - a.shape[-1]),))


def kernel(x, mem, g_pre, w_in, mu_rwkv, w0, w_decay_up, a0, w_iclr_up, k_k, k_a, r_k, ln_x_w, ln_x_b, b_f, g_mem, w_mem_kv, w_out, g_post, loss_target, m_g_pre, m_w_in, m_mu_rwkv, m_w0, m_w_decay_up, m_a0, m_w_iclr_up, m_k_k, m_k_a, m_r_k, m_ln_x_w, m_ln_x_b, m_b_f, m_g_mem, m_w_mem_kv, m_w_out, m_g_post, v_g_pre, v_w_in, v_mu_rwkv, v_w0, v_w_decay_up, v_a0, v_w_iclr_up, v_k_k, v_k_a, v_r_k, v_ln_x_w, v_ln_x_b, v_b_f, v_g_mem, v_w_mem_kv, v_w_out, v_g_post):
    d = x.shape[-1]
    c = Cfg(d)
    ws = w_in.shape[-1]
    wpad = -(-ws // LANES) * LANES
    nh = c.h

    g_in, g_wd, g_wi = _all_gather([
        _pad_cols(w_in[0].astype(BF16), wpad), w_decay_up[0].astype(BF16), w_iclr_up[0].astype(BF16)])
    fl = c.ref_fl
    runs = [(0, fl, 0), (fl, fl + nh, c.o_fl), (fl + nh, c.in_width, fl)]
    pieces = []
    for lo, hi, _ in sorted(runs, key=lambda r: r[2]):
        for s in range(4):
            a, b = max(lo, s * ws), min(hi, (s + 1) * ws)
            if a < b:
                pieces.append(g_in[s, :, a - s * ws:b - s * ws])
    wp = jnp.concatenate(pieces + [jnp.zeros((d, LANES - nh), BF16)], axis=1)
    unshard = lambda g: g.transpose(1, 0, 2).reshape(g.shape[1], -1)
    weights = dict(wp=wp, wd=unshard(g_wd), wi=unshard(g_wi),
                   g_pre=g_pre, mu=mu_rwkv, w0=w0, a0=a0, k_k=k_k, k_a=k_a, r_k=r_k.reshape(1, -1),
                   ln_w=ln_x_w, ln_b=ln_x_b, b_f=_pad_cols(b_f, LANES), g_mem=g_mem, g_post=g_post)
    late_shards = [w_out[0].astype(BF16), w_mem_kv[0].astype(BF16)]

    def late_weights(first):
        g_out, g_mkv = _gather_finish(late_shards, first, "gather_pair_late")
        return dict(w_out=g_out.reshape(-1, d), w_mem_kv=g_mkv.reshape(d, -1))

    by_chip = lambda g: jnp.stack(jnp.split(g, 4, axis=1))
    pair_sums = {}

    def ride_rest(g):
        pair_sums["rest"] = _reduce_pair([g["w_out"].reshape(4, -1, d), g["w_mem_kv"].reshape(4, d // 4, -1),
                                          by_chip(g["wd"]), by_chip(g["wi"])], "rest")
        return _reduce_chips_groups(pair_sums["rest"])

    def ride_wp(dwpt):
        shards = []
        for s in range(4):
            rows = []
            for lo, hi, at in runs:
                a, b = max(lo, s * ws), min(hi, (s + 1) * ws)
                if a < b:
                    rows.append(dwpt[at + a - lo:at + b - lo, :])
            part = rows[0] if len(rows) == 1 else jnp.concatenate(rows, axis=0)
            shards.append(jnp.pad(part, ((0, wpad - ws), (0, 0))))
        pair_sums["wp"] = _reduce_pair([jnp.stack(shards)], "w_in")
        return _reduce_chips_groups(pair_sums["wp"])

    loss, grad_x, _, small, carried = _local_step(
        c, x[0], mem[0], loss_target[0], weights,
        riders={"in_proj": (_gather_groups(late_shards), late_weights), "d_w_in": ride_rest, "d_h": ride_wp})
    red = (_reduce_finish(carried["wp"], pair_sums["wp"], "w_in")
           + _reduce_finish(carried["rest"], pair_sums["rest"], "rest"))
    big_w = (w_in[0].T, w_out[0], w_mem_kv[0], w_decay_up[0], w_iclr_up[0])
    big_m = (m_w_in[0].T, m_w_out[0], m_w_mem_kv[0], m_w_decay_up[0], m_w_iclr_up[0])
    big_v = (v_w_in[0].T, v_w_out[0], v_w_mem_kv[0], v_w_decay_up[0], v_w_iclr_up[0])
    big_names = ("w_in", "w_out", "w_mem_kv", "w_decay_up", "w_iclr_up")
    upd = {n: _adamw(w_, g_, m_, v_, "adamw_" + n) for n, w_, g_, m_, v_ in zip(big_names, big_w, red, big_m, big_v)}
    upd["w_in"] = [o.T for o in upd["w_in"]]

    small_w = dict(g_pre=g_pre, mu=mu_rwkv, w0=w0, a0=a0, k_k=k_k, k_a=k_a, r_k=r_k.reshape(1, -1), ln_w=ln_x_w,
                   ln_b=ln_x_b, b_f=b_f, g_mem=g_mem, g_post=g_post)
    small_m = dict(g_pre=m_g_pre, mu=m_mu_rwkv, w0=m_w0, a0=m_a0, k_k=m_k_k, k_a=m_k_a, r_k=m_r_k.reshape(1, -1),
                   ln_w=m_ln_x_w, ln_b=m_ln_x_b, b_f=m_b_f, g_mem=m_g_mem, g_post=m_g_post)
    small_v = dict(g_pre=v_g_pre, mu=v_mu_rwkv, w0=v_w0, a0=v_a0, k_k=v_k_k, k_a=v_k_a, r_k=v_r_k.reshape(1, -1),
                   ln_w=v_ln_x_w, ln_b=v_ln_x_b, b_f=v_b_f, g_mem=v_g_mem, g_post=v_g_post)
    widths = [-(-small_w[n].shape[1] // LANES) * LANES for n in SMALL]
    pack = lambda t: jnp.concatenate([_pad_cols(t[n], wd_) for n, wd_ in zip(SMALL, widths)]
                                     + [jnp.zeros((1, LANES), F32)], axis=1)
    g_packed = jnp.concatenate([_pad_cols(small[n], wd_) for n, wd_ in zip(SMALL, widths)]
                               + [_pad_cols(loss, LANES)], axis=1)
    g_sum = _all_reduce_small(g_packed)
    s_upd = _adamw(pack(small_w), g_sum, pack(small_m), pack(small_v), "adamw_small")
    offs = [sum(widths[:i]) for i in range(len(SMALL))]

    def take(kind, n):
        i = SMALL.index(n)
        piece = s_upd[kind][:, offs[i]:offs[i] + small_w[n].shape[1]]
        return piece.reshape(r_k.shape) if n == "r_k" else piece

    total_loss = g_sum[0, sum(widths)]
    order = ("g_pre", "w_in", "mu", "w0", "w_decay_up", "a0", "w_iclr_up", "k_k", "k_a", "r_k", "ln_w", "ln_b", "b_f",
             "g_mem", "w_mem_kv", "w_out", "g_post")
    outs = [total_loss, grad_x[None]]
    for kind in range(4):
        for n in order:
            outs.append(upd[n][kind][None] if n in upd else take(kind, n))
    return tuple(outs)
```

```python
import jax
import jax.numpy as jnp
from jax import lax
from jax.experimental import pallas as pl
from jax.experimental.pallas import tpu as pltpu

F32 = jnp.float32
BF16 = jnp.bfloat16
HI = lax.Precision.HIGHEST
MESH = pl.DeviceIdType.MESH

HEAD_DIM = 64
MEM_HEADS = 4
LORA = 128
CHUNK = 64
RMS_EPS = 1e-6
GN_EPS = 64e-5
LANES = 128
WIDE = 2 * LANES
VMEM_LIMIT = 56 * 1024 * 1024

ADAM_LR, ADAM_B1, ADAM_B2, ADAM_EPS, ADAM_WD, ADAM_STEP = 0.001, 0.9, 0.999, 1e-08, 0.01, 10


class Cfg:
    def __init__(self, d):
        self.d = d
        self.rw = 3 * d // 8
        self.mw = d // 4
        self.h = self.rw // HEAD_DIM
        self.mhd = self.mw // MEM_HEADS
        self.shift = 3 * self.rw + 2 * LORA
        self.in_width = self.shift + 5 * self.rw + self.h + 2 * self.mw
        o = self.shift
        self.o_grw = o; o += self.rw
        self.o_fq = o; o += self.rw
        self.o_fk = o; o += self.rw
        self.o_fv = o; o += self.rw
        self.o_gfox = o; o += self.rw
        self.o_mq = o; o += self.mw
        self.o_gmq = o; o += self.mw
        self.o_fl = o; o += LANES
        self.wp = o
        self.ref_fl = self.shift + 4 * self.rw


def _tile(n, pref, align=LANES):
    if n <= pref:
        return n
    t = (pref // align) * align
    while t >= align:
        if n % t == 0:
            return t
        t -= align
    return n


def _params(*sem):
    return pltpu.CompilerParams(dimension_semantics=sem, vmem_limit_bytes=VMEM_LIMIT)


def _pallas_into(body, into, n_in, out_index, in_specs, **kw):
    if into is None:
        return pl.pallas_call(body, in_specs=in_specs, **kw)

    def body_with_alias(*refs):
        return body(*refs[:n_in], *refs[n_in + 1:])

    call = pl.pallas_call(body_with_alias, in_specs=list(in_specs) + [pl.BlockSpec(memory_space=pl.ANY)],
                          input_output_aliases={n_in: out_index}, **kw)
    return lambda *args: call(*args, into)


def _sig(x):
    return 1.0 / (1.0 + jnp.exp(-x))


def _softplus(x):
    return jnp.maximum(x, 0.0) + jnp.log(1.0 + jnp.exp(-jnp.abs(x)))


def _dot(a, b, dims, prec=None):
    return lax.dot_general(a, b, (dims, ((), ())), precision=prec, preferred_element_type=F32)


def _mm(a, b, prec=None):
    return _dot(a, b, ((1,), (0,)), prec)


def _mm_nt(a, b, prec=None):
    return _dot(a, b, ((1,), (1,)), prec)


def _mm_tn(a, b, prec=None):
    return _dot(a, b, ((0,), (0,)), prec)


def _split(a):
    hi = a.astype(BF16)
    return hi, (a - hi.astype(F32)).astype(BF16)


def _dot3(a, b, dims, passes=3):
    d = lambda x, y: lax.dot_general(x, y, dims, preferred_element_type=F32)
    if passes == 1:
        return d(a.astype(BF16), b.astype(BF16))
    if passes == 2:
        ah, (bh, bl) = a.astype(BF16), _split(b)
        return d(ah, bh) + d(ah, bl)
    (ah, al), (bh, bl) = _split(a), _split(b)
    return d(ah, bh) + (d(ah, bl) + d(al, bh))


def _bmm(a, b, passes=3):
    return _dot3(a, b, (((2,), (1,)), ((0,), (0,))), passes)


def _bmm_nt(a, b, passes=3):
    return _dot3(a, b, (((2,), (2,)), ((0,), (0,))), passes)


def _bmm_tn(a, b, passes=3):
    return _dot3(a, b, (((1,), (1,)), ((0,), (0,))), passes)


def _bmm_01(m01, x):
    x1 = x.astype(BF16)
    r1 = x - x1.astype(F32)
    x2 = r1.astype(BF16)
    x3 = (r1 - x2.astype(F32)).astype(BF16)
    d = lambda y: lax.dot_general(m01, y, (((2,), (1,)), ((0,), (0,))), preferred_element_type=F32)
    return d(x1) + (d(x2) + d(x3))


def _matmul(a, b, *, ta=False, tb=False, out_dtype=F32, name, tm=1024, tn=1024, tk=1024, attach=None):
    m, k = (a.shape[1], a.shape[0]) if ta else a.shape
    n = b.shape[0] if tb else b.shape[1]
    tm, tn, tk = _tile(m, tm), _tile(n, tn), _tile(k, tk)
    nk = k // tk
    grid = (m // tm, n // tn, nk)
    dims = ((0 if ta else 1,), (1 if tb else 0,))
    groups = attach or []
    ng = len(groups)
    plan = _plan(groups)

    def body(a_ref, b_ref, *rest):
        srcs, o_ref, outs, scratch = rest[:ng], rest[ng], rest[ng + 1:2 * ng + 1], rest[2 * ng + 1:]
        acc = scratch[0] if nk > 1 else None
        if ng:
            copies = _copies(groups, plan, srcs, outs, scratch[-2], scratch[-1])
            ids = [pl.program_id(ax) for ax in range(3)]

            @pl.when((ids[0] == 0) & (ids[1] == 0) & (ids[2] == 0))
            def _():
                for cp in copies:
                    cp.start()

        part = _dot(a_ref[...].astype(BF16), b_ref[...].astype(BF16), dims)
        if nk == 1:
            o_ref[...] = part.astype(o_ref.dtype)
        else:
            kk = pl.program_id(2)

            @pl.when(kk == 0)
            def _():
                acc[...] = part

            @pl.when(kk > 0)
            def _():
                acc[...] += part

            @pl.when(kk == nk - 1)
            def _():
                o_ref[...] = acc[...].astype(o_ref.dtype)

        if ng:
            @pl.when((ids[0] == grid[0] - 1) & (ids[1] == grid[1] - 1) & (ids[2] == grid[2] - 1))
            def _():
                for cp in copies:
                    cp.wait()

    a_spec = pl.BlockSpec((tk, tm), lambda i, j, kk: (kk, i)) if ta else pl.BlockSpec((tm, tk), lambda i, j, kk: (i, kk))
    b_spec = pl.BlockSpec((tn, tk), lambda i, j, kk: (j, kk)) if tb else pl.BlockSpec((tk, tn), lambda i, j, kk: (kk, j))
    any_spec = pl.BlockSpec(memory_space=pl.ANY)
    sems = [pltpu.SemaphoreType.DMA((len(plan),)), pltpu.SemaphoreType.DMA((len(plan),))] if ng else []
    res = pl.pallas_call(
        body, name=name, grid=grid,
        in_specs=[a_spec, b_spec] + [any_spec] * ng,
        out_specs=[pl.BlockSpec((tm, tn), lambda i, j, kk: (i, j))] + [any_spec] * ng,
        out_shape=[jax.ShapeDtypeStruct((m, n), out_dtype)] + _exchange_shapes(groups),
        scratch_shapes=([pltpu.VMEM((tm, tn), F32)] if nk > 1 else []) + sems,
        compiler_params=_params(*(("arbitrary",) * 3 if ng else ("parallel", "parallel", "arbitrary"))),
    )(a, b, *[g["src"] for g in groups])
    return (res[0], list(res[1:])) if ng else res[0]


def _rms_fwd(x, g, name):
    t, d = x.shape
    tm = _tile(t, 256, 8)

    def body(x_ref, g_ref, h_ref, r_ref):
        xv = x_ref[...]
        r = lax.rsqrt(jnp.mean(xv * xv, axis=-1, keepdims=True) + RMS_EPS)
        h_ref[...] = (xv * r * g_ref[...]).astype(BF16)
        r_ref[...] = r

    return pl.pallas_call(
        body, name=name, grid=(t // tm,),
        in_specs=[pl.BlockSpec((tm, d), lambda i: (i, 0)), pl.BlockSpec((1, d), lambda i: (0, 0))],
        out_specs=[pl.BlockSpec((tm, d), lambda i: (i, 0)), pl.BlockSpec((tm, 1), lambda i: (i, 0))],
        out_shape=[jax.ShapeDtypeStruct((t, d), BF16), jax.ShapeDtypeStruct((t, 1), F32)],
        compiler_params=_params("parallel"),
    )(x, g)


def _rms_bwd(dh, x, rinv, g, add, name):
    t, d = x.shape
    tm = _tile(t, 256, 8)

    def body(dh_ref, x_ref, r_ref, g_ref, add_ref, dx_ref, dg_ref):
        @pl.when(pl.program_id(0) == 0)
        def _():
            dg_ref[...] = jnp.zeros_like(dg_ref)

        r = r_ref[...]
        xn = x_ref[...] * r
        dhv = dh_ref[...]
        dg_ref[...] += jnp.sum(dhv * xn, axis=0, keepdims=True)
        dxn = dhv * g_ref[...]
        dx_ref[...] = add_ref[...] + r * (dxn - xn * jnp.mean(dxn * xn, axis=-1, keepdims=True))

    row = pl.BlockSpec((tm, d), lambda i: (i, 0))
    vec = pl.BlockSpec((1, d), lambda i: (0, 0))
    return pl.pallas_call(
        body, name=name, grid=(t // tm,),
        in_specs=[row, row, pl.BlockSpec((tm, 1), lambda i: (i, 0)), vec, row],
        out_specs=[row, vec],
        out_shape=[jax.ShapeDtypeStruct((t, d), F32), jax.ShapeDtypeStruct((1, d), F32)],
        compiler_params=_params("arbitrary"),
    )(dh, x, rinv, g, add)


def _post_loss(yo, x, tgt, g, name):
    t, d = x.shape
    tm = _tile(t, 256, 8)

    def body(yo_ref, x_ref, t_ref, g_ref, loss_ref, dout_ref, dyo_ref, dg_ref):
        @pl.when(pl.program_id(0) == 0)
        def _():
            dg_ref[...] = jnp.zeros_like(dg_ref)
            loss_ref[...] = jnp.zeros_like(loss_ref)

        yv = yo_ref[...]
        r = lax.rsqrt(jnp.mean(yv * yv, axis=-1, keepdims=True) + RMS_EPS)
        n = yv * r
        err = x_ref[...] + n * g_ref[...] - t_ref[...]
        loss_ref[...] += 0.5 * jnp.sum(jnp.mean(err * err, axis=-1, keepdims=True), axis=0, keepdims=True)
        dout = err * (1.0 / d)
        dout_ref[...] = dout
        dg_ref[...] += jnp.sum(dout * n, axis=0, keepdims=True)
        dn = dout * g_ref[...]
        dyo_ref[...] = (r * (dn - n * jnp.mean(dn * n, axis=-1, keepdims=True))).astype(BF16)

    row = pl.BlockSpec((tm, d), lambda i: (i, 0))
    vec = pl.BlockSpec((1, d), lambda i: (0, 0))
    return pl.pallas_call(
        body, name=name, grid=(t // tm,),
        in_specs=[row, row, row, vec],
        out_specs=[pl.BlockSpec((1, 1), lambda i: (0, 0)), row, row, vec],
        out_shape=[jax.ShapeDtypeStruct((1, 1), F32), jax.ShapeDtypeStruct((t, d), F32),
                   jax.ShapeDtypeStruct((t, d), BF16), jax.ShapeDtypeStruct((1, d), F32)],
        compiler_params=_params("arbitrary"),
    )(yo, x, tgt, g)


def _head_sum(x):
    ri = lax.broadcasted_iota(jnp.int32, (LANES, LANES), 0) // HEAD_DIM
    ci = lax.broadcasted_iota(jnp.int32, (LANES, LANES), 1) // HEAD_DIM
    e = (ri == ci).astype(BF16)
    x1 = x.astype(BF16)
    r1 = x - x1.astype(F32)
    x2 = r1.astype(BF16)
    x3 = (r1 - x2.astype(F32)).astype(BF16)
    parts = []
    for i in range(x.shape[1] // LANES):
        sl = slice(i * LANES, (i + 1) * LANES)
        parts.append(_mm(x1[:, sl], e) + (_mm(x2[:, sl], e) + _mm(x3[:, sl], e)))
    return parts[0] if len(parts) == 1 else jnp.concatenate(parts, axis=1)


def _shifted(p_cur, before, first, mu):
    rolled = pltpu.roll(p_cur, 1, 0)
    prev_row = jnp.where(first, 0.0, before)
    row0 = lax.broadcasted_iota(jnp.int32, p_cur.shape, 0) == 0
    prev = jnp.where(row0, prev_row, rolled)
    return p_cur + (prev - p_cur) * mu, prev


def _rwkv_features(ps, rw, w0, a0, k_k, k_a, wd, wi):
    r, k, v = ps[:, 0:rw], ps[:, rw:2 * rw], ps[:, 2 * rw:3 * rw]
    wl, al = ps[:, 3 * rw:3 * rw + LORA], ps[:, 3 * rw + LORA:3 * rw + 2 * LORA]
    tw = jnp.tanh(wl)
    zw = w0 + _mm(tw.astype(BF16), wd)
    logw = -jnp.exp(-_softplus(-zw) - 0.5)
    alpha = _sig(a0 + _mm(al.astype(BF16), wi))
    kkr = k * k_k
    n2 = _head_sum(kkr * kkr)
    rn = lax.rsqrt(jnp.maximum(n2, 1e-24))
    kk = kkr * rn
    kmod = k * (1.0 + (alpha - 1.0) * k_a)
    return dict(r=r, k=k, v=v, tw=tw, al=al, zw=zw, logw=logw, alpha=alpha, kk=kk, rn=rn, n2=n2, kmod=kmod)


def _rwkv_pre_fwd(p, c, mu, w0, a0, k_k, k_a, wd, wi):
    t = p.shape[0]
    tm = _tile(t, 128, 8)
    rw, sh = c.rw, c.shift

    def body(p_ref, pp_ref, mu_ref, w0_ref, a0_ref, kk_ref, ka_ref, wd_ref, wi_ref,
             r_ref, lw_ref, km_ref, v_ref, a_ref, b_ref):
        ps, _ = _shifted(p_ref[...], pp_ref[7:8, :], pl.program_id(0) == 0, mu_ref[...])
        f = _rwkv_features(ps, rw, w0_ref[...], a0_ref[...], kk_ref[...], ka_ref[...], wd_ref[...], wi_ref[...])
        r_ref[...] = f["r"]
        lw_ref[...] = f["logw"]
        km_ref[...] = f["kmod"]
        v_ref[...] = f["v"]
        a_ref[...] = -f["kk"]
        b_ref[...] = f["kk"] * f["alpha"]

    vec = lambda n: pl.BlockSpec((1, n), lambda i: (0, 0))
    out = pl.BlockSpec((tm, rw), lambda i: (i, 0))
    return pl.pallas_call(
        body, name="rwkv_pre_fwd", grid=(t // tm,),
        in_specs=[pl.BlockSpec((tm, sh), lambda i: (i, 0)),
                  pl.BlockSpec((8, sh), lambda i: (jnp.maximum(i * (tm // 8) - 1, 0), 0)),
                  vec(sh), vec(rw), vec(rw), vec(rw), vec(rw),
                  pl.BlockSpec((LORA, rw), lambda i: (0, 0)), pl.BlockSpec((LORA, rw), lambda i: (0, 0))],
        out_specs=[out] * 6,
        out_shape=[jax.ShapeDtypeStruct((t, rw), F32)] * 6,
        compiler_params=_params("parallel"),
    )(p, p, mu, w0, a0, k_k, k_a, wd, wi)


def _rwkv_pre_bwd(p, c, mu, w0, a0, k_k, k_a, wd, wi, dr, dlw, dkm, dv, da, db, dr2, dkm2, dv2):
    t = p.shape[0]
    tm = _tile(t, 128, 8)
    rw, sh = c.rw, c.shift

    def body(p_ref, pp_ref, mu_ref, w0_ref, a0_ref, kk_ref, ka_ref, wd_ref, wi_ref,
             dr_ref, dlw_ref, dkm_ref, dv_ref, da_ref, db_ref, dr2_ref, dkm2_ref, dv2_ref,
             dps_ref, dzw_ref, dza_ref, tw_ref, al_ref, dw0_ref, da0_ref, dkk_ref, dka_ref):
        @pl.when(pl.program_id(0) == 0)
        def _():
            for ref in (dw0_ref, da0_ref, dkk_ref, dka_ref):
                ref[...] = jnp.zeros_like(ref)

        ps, _ = _shifted(p_ref[...], pp_ref[7:8, :], pl.program_id(0) == 0, mu_ref[...])
        k_k, k_a = kk_ref[...], ka_ref[...]
        f = _rwkv_features(ps, rw, w0_ref[...], a0_ref[...], k_k, k_a, wd_ref[...], wi_ref[...])
        alpha, kk, k = f["alpha"], f["kk"], f["k"]
        dkm = dkm_ref[...] + dkm2_ref[...]
        db = db_ref[...]
        dkk = db * alpha - da_ref[...]
        dalpha = db * kk + dkm * k * k_a
        dk = dkm * (1.0 + (alpha - 1.0) * k_a)
        dka_ref[...] += jnp.sum(dkm * k * (alpha - 1.0), axis=0, keepdims=True)
        dkkr = f["rn"] * jnp.where(f["n2"] > 1e-24, dkk - kk * _head_sum(dkk * kk), dkk)
        dk = dk + dkkr * k_k
        dkk_ref[...] += jnp.sum(dkkr * k, axis=0, keepdims=True)
        dza = dalpha * alpha * (1.0 - alpha)
        da0_ref[...] += jnp.sum(dza, axis=0, keepdims=True)
        dzw = dlw_ref[...] * f["logw"] * _sig(-f["zw"])
        dw0_ref[...] += jnp.sum(dzw, axis=0, keepdims=True)
        dza_b, dzw_b = dza.astype(BF16), dzw.astype(BF16)
        dal = _mm_nt(dza_b, wi_ref[...])
        dwl = _mm_nt(dzw_b, wd_ref[...]) * (1.0 - f["tw"] * f["tw"])
        dps_ref[:, 0:rw] = dr_ref[...] + dr2_ref[...]
        dps_ref[:, rw:2 * rw] = dk
        dps_ref[:, 2 * rw:3 * rw] = dv_ref[...] + dv2_ref[...]
        dps_ref[:, 3 * rw:3 * rw + LORA] = dwl
        dps_ref[:, 3 * rw + LORA:sh] = dal
        dzw_ref[...] = dzw_b
        dza_ref[...] = dza_b
        tw_ref[...] = f["tw"].astype(BF16)
        al_ref[...] = f["al"].astype(BF16)

    vec = lambda n: pl.BlockSpec((1, n), lambda i: (0, 0))
    blk = lambda n: pl.BlockSpec((tm, n), lambda i: (i, 0))
    return pl.pallas_call(
        body, name="rwkv_pre_bwd", grid=(t // tm,),
        in_specs=[blk(sh), pl.BlockSpec((8, sh), lambda i: (jnp.maximum(i * (tm // 8) - 1, 0), 0)),
                  vec(sh), vec(rw), vec(rw), vec(rw), vec(rw),
                  pl.BlockSpec((LORA, rw), lambda i: (0, 0)), pl.BlockSpec((LORA, rw), lambda i: (0, 0))]
                 + [blk(rw)] * 9,
        out_specs=[blk(sh), blk(rw), blk(rw), blk(LORA), blk(LORA), vec(rw), vec(rw), vec(rw), vec(rw)],
        out_shape=[jax.ShapeDtypeStruct((t, sh), F32), jax.ShapeDtypeStruct((t, rw), BF16),
                   jax.ShapeDtypeStruct((t, rw), BF16), jax.ShapeDtypeStruct((t, LORA), BF16),
                   jax.ShapeDtypeStruct((t, LORA), BF16)] + [jax.ShapeDtypeStruct((1, rw), F32)] * 4,
        compiler_params=_params("arbitrary"),
    )(p, p, mu, w0, a0, k_k, k_a, wd, wi, dr, dlw, dkm, dv, da, db, dr2, dkm2, dv2)


def _shift_bwd(dps, p, c, mu, dp):
    t = p.shape[0]
    tm = _tile(t, 256, 8)
    sh = c.shift
    nt = t // tm

    def body(d_ref, dn_ref, p_ref, pp_ref, mu_ref, dp_ref, dmu_ref):
        i = pl.program_id(0)

        @pl.when(i == 0)
        def _():
            dmu_ref[...] = jnp.zeros_like(dmu_ref)

        mu = mu_ref[...]
        d = d_ref[...]
        pc = p_ref[...]
        _, prev = _shifted(pc, pp_ref[7:8, :], i == 0, mu)
        dmu_ref[...] += jnp.sum(d * (prev - pc), axis=0, keepdims=True)
        nxt_row = jnp.where(i == nt - 1, 0.0, dn_ref[0:1, :])
        last = lax.broadcasted_iota(jnp.int32, d.shape, 0) == tm - 1
        nxt = jnp.where(last, nxt_row, pltpu.roll(d, tm - 1, 0))
        dp_ref[...] = (d * (1.0 - mu) + nxt * mu).astype(BF16)

    blk = pl.BlockSpec((tm, sh), lambda i: (i, 0))
    return _pallas_into(
        body, dp, 5, 0, name="shift_bwd", grid=(nt,),
        in_specs=[blk, pl.BlockSpec((8, sh), lambda i: (jnp.minimum((i + 1) * (tm // 8), t // 8 - 1), 0)),
                  blk, pl.BlockSpec((8, sh), lambda i: (jnp.maximum(i * (tm // 8) - 1, 0), 0)),
                  pl.BlockSpec((1, sh), lambda i: (0, 0))],
        out_specs=[blk, pl.BlockSpec((1, sh), lambda i: (0, 0))],
        out_shape=[jax.ShapeDtypeStruct((t, c.wp), BF16), jax.ShapeDtypeStruct((1, sh), F32)],
        compiler_params=_params("arbitrary"),
    )(dps, dps, p, p, mu)


def _tri(n, strict):
    ri = lax.broadcasted_iota(jnp.int32, (n, n), 0)
    ci = lax.broadcasted_iota(jnp.int32, (n, n), 1)
    return (ri > ci) if strict else (ri >= ci)


def _unit_lower_inverse(a):
    n = a.shape[-1]
    ri = lax.broadcasted_iota(jnp.int32, (n, n), 0)
    ci = lax.broadcasted_iota(jnp.int32, (n, n), 1)
    eye = (ri == ci).astype(F32)
    blk = lambda s: (ri // s) == (ci // s)
    ad = jnp.where(blk(16), a, 0.0)
    p = eye + ad
    for _ in range(3):
        ad = _bmm(ad, ad, P_SOLVE)
        p = p + _bmm(p, ad, P_SOLVE)
    s = 16
    while s < n:
        off = jnp.where(blk(2 * s) & ~blk(s), a, 0.0)
        p = p + _bmm(_bmm(p, off, P_SOLVE), p, P_SOLVE)
        s *= 2
    return p


P_SOLVE, P_STATE, P_OUT, P_GRAD, P_DECAY = 1, 1, 1, 1, 2


def _chunk_common(r, lw, k, a, b):
    n = r.shape[1]
    tri_incl = jnp.broadcast_to(_tri(n, False).astype(BF16), (r.shape[0], n, n))
    cum = _bmm_01(tri_incl, lw)
    e_pos, e_neg, e_exc = jnp.exp(cum), jnp.exp(-cum), jnp.exp(cum - lw)
    last = lax.broadcasted_iota(jnp.int32, (n, r.shape[2]), 0) == n - 1
    g_last = jnp.exp(jnp.sum(jnp.where(last, cum, 0.0), axis=1, keepdims=True))
    return g_last, r * e_pos, a * e_exc, b * e_neg, k * e_neg, e_pos, e_neg, e_exc


def _chunk_solve(rt, at, bt, kt, v, g0):
    strict, incl = _tri(rt.shape[1], True), _tri(rt.shape[1], False)
    a_ab = jnp.where(strict, _bmm_nt(at, bt, P_SOLVE), 0.0)
    a_ak = jnp.where(strict, _bmm_nt(at, kt, P_SOLVE), 0.0)
    a_rb = jnp.where(incl, _bmm_nt(rt, bt, P_OUT), 0.0)
    a_rk = jnp.where(incl, _bmm_nt(rt, kt, P_OUT), 0.0)
    tinv = _unit_lower_inverse(a_ab)
    u = _bmm(tinv, _bmm(at, g0, P_SOLVE) + _bmm(a_ak, v, P_SOLVE), P_SOLVE)
    return a_ab, a_ak, a_rb, a_rk, tinv, u


def _diag_col(row, n):
    ri = lax.broadcasted_iota(jnp.int32, (n, n), 0)
    ci = lax.broadcasted_iota(jnp.int32, (n, n), 1)
    return jnp.sum(jnp.where(ri == ci, row, 0.0), axis=2, keepdims=True)


def _diag_row(col, n):
    ri = lax.broadcasted_iota(jnp.int32, (n, n), 0)
    ci = lax.broadcasted_iota(jnp.int32, (n, n), 1)
    return jnp.sum(jnp.where(ri == ci, col, 0.0), axis=1, keepdims=True)


def _rwkv_scan_fwd(r, lw, k, v, a, b, hb):
    h, t, n = r.shape
    nc = t // CHUNK

    def body(r_ref, lw_ref, k_ref, v_ref, a_ref, b_ref, y_ref, st_ref, g_sc):
        @pl.when(pl.program_id(1) == 0)
        def _():
            g_sc[...] = jnp.zeros_like(g_sc)

        g0 = g_sc[...]
        st_ref[0] = g0
        vv = v_ref[...]
        g_last, rt, at, bt, kt, _, _, _ = _chunk_common(r_ref[...], lw_ref[...], k_ref[...], a_ref[...], b_ref[...])
        _, _, a_rb, a_rk, _, u = _chunk_solve(rt, at, bt, kt, vv, g0)
        y_ref[...] = _bmm(rt, g0, P_OUT) + _bmm(a_rb, u, P_OUT) + _bmm(a_rk, vv, P_OUT)
        z = g0 + _bmm_tn(bt, u, P_STATE) + _bmm_tn(kt, vv, P_STATE)
        g_sc[...] = _diag_col(g_last, n) * z

    blk = pl.BlockSpec((hb, CHUNK, n), lambda i, j: (i, j, 0))
    return pl.pallas_call(
        body, name="rwkv_scan_fwd", grid=(h // hb, nc),
        in_specs=[blk] * 6,
        out_specs=[blk, pl.BlockSpec((1, hb, n, n), lambda i, j: (j, i, 0, 0))],
        out_shape=[jax.ShapeDtypeStruct((h, t, n), F32), jax.ShapeDtypeStruct((nc, h, n, n), F32)],
        scratch_shapes=[pltpu.VMEM((hb, n, n), F32)],
        compiler_params=_params("parallel", "arbitrary"),
    )(r, lw, k, v, a, b)


def _rwkv_scan_bwd(r, lw, k, v, a, b, states, dy, hb):
    h, t, n = r.shape
    nc = t // CHUNK

    def body(r_ref, lw_ref, k_ref, v_ref, a_ref, b_ref, st_ref, dy_ref,
             dr_ref, dlw_ref, dk_ref, dv_ref, da_ref, db_ref, dg_sc):
        @pl.when(pl.program_id(1) == 0)
        def _():
            dg_sc[...] = jnp.zeros_like(dg_sc)

        g0 = st_ref[0]
        vv, dyv, dh = v_ref[...], dy_ref[...], dg_sc[...]
        lwv = lw_ref[...]
        g_last, rt, at, bt, kt, e_pos, e_neg, e_exc = _chunk_common(r_ref[...], lwv, k_ref[...], a_ref[...], b_ref[...])
        a_ab, a_ak, a_rb, a_rk, tinv, u = _chunk_solve(rt, at, bt, kt, vv, g0)
        strict, incl = _tri(CHUNK, True), _tri(CHUNK, False)
        gcol = _diag_col(g_last, n)
        z = g0 + _bmm_tn(bt, u, P_STATE) + _bmm_tn(kt, vv, P_STATE)
        dz = gcol * dh
        dc_last = _diag_row(jnp.sum(dh * gcol * z, axis=2, keepdims=True), n)
        g = P_GRAD
        du = _bmm_tn(a_rb, dyv, g) + _bmm(bt, dz, g)
        dx = _bmm_tn(tinv, du, P_SOLVE)
        dv_ref[...] = _bmm_tn(a_rk, dyv, g) + _bmm(kt, dz, g) + _bmm_tn(a_ak, dx, g)
        da_ab = jnp.where(strict, _bmm_nt(dx, u, g), 0.0)
        da_ak = jnp.where(strict, _bmm_nt(dx, vv, g), 0.0)
        da_rb = jnp.where(incl, _bmm_nt(dyv, u, g), 0.0)
        da_rk = jnp.where(incl, _bmm_nt(dyv, vv, g), 0.0)
        g = P_DECAY
        d_at = _bmm(da_ab, bt, g) + _bmm(da_ak, kt, g) + _bmm_nt(dx, g0, g)
        d_rt = _bmm(da_rb, bt, g) + _bmm(da_rk, kt, g) + _bmm_nt(dyv, g0, g)
        d_bt = _bmm_tn(da_ab, at, g) + _bmm_tn(da_rb, rt, g) + _bmm_nt(u, dz, g)
        d_kt = _bmm_tn(da_ak, at, g) + _bmm_tn(da_rk, rt, g) + _bmm_nt(vv, dz, g)
        dg_sc[...] = dz + _bmm_tn(rt, dyv, P_STATE) + _bmm_tn(at, dx, P_STATE)
        dr_ref[...] = d_rt * e_pos
        da_ref[...] = d_at * e_exc
        db_ref[...] = d_bt * e_neg
        dk_ref[...] = d_kt * e_neg
        last = lax.broadcasted_iota(jnp.int32, (CHUNK, n), 0) == CHUNK - 1
        dc = d_rt * rt - d_bt * bt - d_kt * kt + jnp.where(last, dc_last, 0.0)
        dce = d_at * at
        ri = lax.broadcasted_iota(jnp.int32, (CHUNK, CHUNK), 0)
        ci = lax.broadcasted_iota(jnp.int32, (CHUNK, CHUNK), 1)
        up_incl = jnp.broadcast_to((ri <= ci).astype(BF16), (hb, CHUNK, CHUNK))
        dlw_ref[...] = _bmm_01(up_incl, dc + dce) - dce

    rev = lambda i, j: (i, nc - 1 - j, 0)
    blk = pl.BlockSpec((hb, CHUNK, n), rev)
    return pl.pallas_call(
        body, name="rwkv_scan_bwd", grid=(h // hb, nc),
        in_specs=[blk] * 6 + [pl.BlockSpec((1, hb, n, n), lambda i, j: (nc - 1 - j, i, 0, 0)), blk],
        out_specs=[blk] * 6,
        out_shape=[jax.ShapeDtypeStruct((h, t, n), F32)] * 6,
        scratch_shapes=[pltpu.VMEM((hb, n, n), F32)],
        compiler_params=_params("parallel", "arbitrary"),
    )(r, lw, k, v, a, b, states, dy)


def _silu_grad(g):
    s = _sig(g)
    return s * (1.0 + g * (1.0 - s))


def _group_norm(ys):
    yc = ys - _head_sum(ys) * (1.0 / HEAD_DIM)
    rstd = lax.rsqrt(_head_sum(yc * yc) * (1.0 / HEAD_DIM) + GN_EPS)
    return yc * rstd, rstd


def _rwkv_post_fwd(ys, r, km, v, p, c, ln_w, ln_b, r_k):
    t = ys.shape[0]
    tm = _tile(t, 1024, 8)
    goff = c.o_grw // WIDE

    def body(ys_ref, r_ref, km_ref, v_ref, g_ref, lw_ref, lb_ref, rk_ref, o_ref):
        yn, _ = _group_norm(ys_ref[...])
        s = _head_sum(r_ref[...] * km_ref[...] * rk_ref[...])
        g = g_ref[...]
        o_ref[...] = ((yn * lw_ref[...] + lb_ref[...] + s * v_ref[...]) * g * _sig(g)).astype(BF16)

    blk = pl.BlockSpec((tm, WIDE), lambda i, j: (i, j))
    vec = pl.BlockSpec((1, WIDE), lambda i, j: (0, j))
    return pl.pallas_call(
        body, name="rwkv_post_fwd", grid=(t // tm, c.rw // WIDE),
        in_specs=[blk] * 4 + [pl.BlockSpec((tm, WIDE), lambda i, j: (i, goff + j)), vec, vec, vec],
        out_specs=blk, out_shape=jax.ShapeDtypeStruct((t, c.d), BF16),
        compiler_params=_params("parallel", "parallel"),
    )(ys, r, km, v, p, ln_w, ln_b, r_k)


def _rwkv_post_bwd(dyc, ys, r, km, v, p, c, ln_w, ln_b, r_k):
    t = ys.shape[0]
    tm = _tile(t, 512, 8)
    goff = c.o_grw // WIDE

    def body(dy_ref, ys_ref, r_ref, km_ref, v_ref, g_ref, lw_ref, lb_ref, rk_ref,
             dys_ref, dr_ref, dkm_ref, dv_ref, dg_ref, dlw_ref, dlb_ref, drk_ref):
        @pl.when(pl.program_id(1) == 0)
        def _():
            for ref in (dlw_ref, dlb_ref, drk_ref):
                ref[...] = jnp.zeros_like(ref)

        yn, rstd = _group_norm(ys_ref[...])
        rv, kmv, vv, rk, g = r_ref[...], km_ref[...], v_ref[...], rk_ref[...], g_ref[...]
        s = _head_sum(rv * kmv * rk)
        y = yn * lw_ref[...] + lb_ref[...] + s * vv
        dyc = dy_ref[...]
        dg_ref[...] = (dyc * y * _silu_grad(g)).astype(BF16)
        dy = dyc * g * _sig(g)
        dlb_ref[...] += jnp.sum(dy, axis=0, keepdims=True)
        dlw_ref[...] += jnp.sum(dy * yn, axis=0, keepdims=True)
        dyn = dy * lw_ref[...]
        inv = 1.0 / HEAD_DIM
        dys_ref[...] = rstd * (dyn - _head_sum(dyn) * inv - yn * _head_sum(dyn * yn) * inv)
        ds = _head_sum(dy * vv)
        dv_ref[...] = dy * s
        dr_ref[...] = ds * kmv * rk
        dkm_ref[...] = ds * rv * rk
        drk_ref[...] += jnp.sum(ds * rv * kmv, axis=0, keepdims=True)

    blk = pl.BlockSpec((tm, WIDE), lambda j, i: (i, j))
    vec = pl.BlockSpec((1, WIDE), lambda j, i: (0, j))
    f = jax.ShapeDtypeStruct((t, c.rw), F32)
    s1 = jax.ShapeDtypeStruct((1, c.rw), F32)
    gate = pl.BlockSpec((tm, WIDE), lambda j, i: (i, goff + j))
    return pl.pallas_call(
        body, name="rwkv_post_bwd", grid=(c.rw // WIDE, t // tm),
        in_specs=[blk] * 5 + [gate, vec, vec, vec],
        out_specs=[blk] * 4 + [gate] + [vec] * 3,
        out_shape=[f, f, f, f, jax.ShapeDtypeStruct((t, c.wp), BF16), s1, s1, s1],
        compiler_params=_params("parallel", "arbitrary"),
    )(dyc, ys, r, km, v, p, ln_w, ln_b, r_k)


def _gate_fwd(y, p, goff, name, ycat, yoff):
    t, w = y.shape
    tm = _tile(t, 1024, 8)
    gb, ob = goff // WIDE, yoff // WIDE

    def body(y_ref, g_ref, o_ref):
        g = g_ref[...]
        o_ref[...] = (y_ref[...] * g * _sig(g)).astype(BF16)

    blk = pl.BlockSpec((tm, WIDE), lambda i, j: (i, j))
    return _pallas_into(
        body, ycat, 2, 0, name=name, grid=(t // tm, w // WIDE),
        in_specs=[blk, pl.BlockSpec((tm, WIDE), lambda i, j: (i, gb + j))],
        out_specs=pl.BlockSpec((tm, WIDE), lambda i, j: (i, ob + j)),
        out_shape=jax.ShapeDtypeStruct(ycat.shape, BF16),
        compiler_params=_params("parallel", "parallel"),
    )(y, p)


def _gate_bwd(dyc, yoff, y, p, goff, name, dp):
    t, w = y.shape
    tm = _tile(t, 1024, 8)
    gb, yb = goff // WIDE, yoff // WIDE

    def body(d_ref, y_ref, g_ref, dy_ref, dg_ref):
        g, d = g_ref[...], d_ref[...]
        dy_ref[...] = d * g * _sig(g)
        dg_ref[...] = (d * y_ref[...] * _silu_grad(g)).astype(BF16)

    blk = pl.BlockSpec((tm, WIDE), lambda i, j: (i, j))
    gate = pl.BlockSpec((tm, WIDE), lambda i, j: (i, gb + j))
    return _pallas_into(
        body, dp, 3, 1, name=name, grid=(t // tm, w // WIDE),
        in_specs=[pl.BlockSpec((tm, WIDE), lambda i, j: (i, yb + j)), blk, gate],
        out_specs=[blk, gate],
        out_shape=[jax.ShapeDtypeStruct((t, w), F32), jax.ShapeDtypeStruct(dp.shape, BF16)],
        compiler_params=_params("parallel", "parallel"),
    )(dyc, y, p)


NEG = -1e30


def _fox_logit_bwd(dcum, p, c, b_f):
    t = p.shape[0]
    tm = _tile(t, 512, 8)
    fb = c.o_fl // LANES
    nt = t // tm

    def body(d_ref, f_ref, b_ref, o_ref, db_ref, carry):
        @pl.when(pl.program_id(0) == 0)
        def _():
            carry[...] = jnp.zeros_like(carry)
            db_ref[...] = jnp.zeros_like(db_ref)

        d = d_ref[0] + d_ref[1]
        dlogf = _mm(_tri(tm, False).astype(F32).T, d, HI) + carry[...]
        carry[...] += jnp.sum(d, axis=0, keepdims=True)
        df = dlogf * _sig(-(f_ref[...] + b_ref[...]))
        o_ref[...] = df.astype(BF16)
        db_ref[...] += jnp.sum(df, axis=0, keepdims=True)

    return pl.pallas_call(
        body, name="fox_logit_bwd", grid=(nt,),
        in_specs=[pl.BlockSpec((2, tm, LANES), lambda i: (0, nt - 1 - i, 0)),
                  pl.BlockSpec((tm, LANES), lambda i: (nt - 1 - i, fb)),
                  pl.BlockSpec((1, LANES), lambda i: (0, 0))],
        out_specs=[pl.BlockSpec((tm, LANES), lambda i: (nt - 1 - i, 0)), pl.BlockSpec((1, LANES), lambda i: (0, 0))],
        out_shape=[jax.ShapeDtypeStruct((t, LANES), BF16), jax.ShapeDtypeStruct((1, LANES), F32)],
        scratch_shapes=[pltpu.VMEM((1, LANES), F32)],
        compiler_params=_params("arbitrary"),
    )(dcum, p, b_f)


FOX_PAIRS = 2
FOX_HEADS_STEP = 2 * FOX_PAIRS


def _lane_half(shape, upper):
    li = lax.broadcasted_iota(jnp.int32, shape, len(shape) - 1)
    return (li >= HEAD_DIM) if upper else (li < HEAD_DIM)


def _col(block, j):
    li = lax.broadcasted_iota(jnp.int32, block.shape, 1)
    return jnp.sum(jnp.where(li == j, block, 0.0), axis=1, keepdims=True)


def _from_cols(cols):
    li = lax.broadcasted_iota(jnp.int32, (cols[0].shape[0], len(cols)), 1)
    out = jnp.zeros(li.shape, F32)
    for j, cj in enumerate(cols):
        out = jnp.where(li == j, cj, out)
    return out


def _from_rows(rows):
    si = lax.broadcasted_iota(jnp.int32, (len(rows), rows[0].shape[1]), 0)
    out = jnp.zeros(si.shape, F32)
    for j, rj in enumerate(rows):
        out = jnp.where(si == j, rj, out)
    return out


def _causal(tq, tk):
    return lax.broadcasted_iota(jnp.int32, (tq, tk), 1) <= lax.broadcasted_iota(jnp.int32, (tq, tk), 0)


def _fox_prep_t(p, c, b_f):
    t = p.shape[0]
    tm = _tile(t, 512, LANES)
    fb = c.o_fl // LANES

    def body(f_ref, b_ref, o_ref, carry):
        @pl.when(pl.program_id(0) == 0)
        def _():
            carry[...] = jnp.zeros_like(carry)

        logf = -_softplus(-(f_ref[...] + b_ref[...]))
        cum = _mm(_tri(tm, False).astype(F32), logf, HI) + carry[...]
        o_ref[...] = cum.T
        carry[...] += jnp.sum(logf, axis=0, keepdims=True)

    return pl.pallas_call(
        body, name="fox_prep", grid=(t // tm,),
        in_specs=[pl.BlockSpec((tm, LANES), lambda i: (i, fb)), pl.BlockSpec((1, LANES), lambda i: (0, 0))],
        out_specs=pl.BlockSpec((LANES, tm), lambda i: (0, i)),
        out_shape=jax.ShapeDtypeStruct((LANES, t), F32),
        scratch_shapes=[pltpu.VMEM((1, LANES), F32)],
        compiler_params=_params("arbitrary"),
    )(p, b_f)


def _fox2_fwd(p, c, cum_t, tb, ycat):
    t = p.shape[0]
    tq = tk = _tile(t, tb, LANES)
    nq = t // tq
    pw, nh = FOX_PAIRS * LANES, FOX_HEADS_STEP
    qb, kb, vb, gb = (o // pw for o in (c.o_fq, c.o_fk, c.o_fv, c.o_gfox))
    scale = HEAD_DIM ** -0.5

    def body(q_ref, k_ref, v_ref, g_ref, ck_ref, o_ref, y_ref, lse_ref, m_sc, l_sc, acc_sc):
        g, qi, ki = pl.program_id(0), pl.program_id(1), pl.program_id(2)

        @pl.when(ki == 0)
        def _():
            m_sc[...] = jnp.full_like(m_sc, NEG)
            l_sc[...] = jnp.zeros_like(l_sc)
            acc_sc[...] = jnp.zeros_like(acc_sc)

        def step(diag):
            ms, ls = [m_sc[h] for h in range(nh)], [l_sc[h] for h in range(nh)]
            accs = [acc_sc[:, pi * LANES:(pi + 1) * LANES] for pi in range(FOX_PAIRS)]
            for pi in range(FOX_PAIRS):
                lanes = slice(pi * LANES, (pi + 1) * LANES)
                q2 = (q_ref[:, lanes] * scale).astype(BF16)
                k2, v2 = k_ref[:, lanes].astype(BF16), v_ref[:, lanes].astype(BF16)
                new_acc = accs[pi]
                for hh in range(2):
                    hi = 2 * pi + hh
                    mk = _lane_half((tq, LANES), hh == 1)
                    s = _mm_nt(jnp.where(mk, q2, jnp.zeros_like(q2)), k2) - ck_ref[pl.ds(g * nh + hi, 1), :]
                    if diag:
                        s = jnp.where(_causal(tq, tk), s, NEG)
                    m_new = jnp.maximum(ms[hi], jnp.max(s, axis=1, keepdims=True))
                    a = jnp.exp(ms[hi] - m_new)
                    e = jnp.exp(s - jnp.concatenate([m_new] * (tk // LANES), axis=1))
                    ls[hi] = a * ls[hi] + jnp.sum(e, axis=1, keepdims=True)
                    ms[hi] = m_new
                    new_acc = jnp.where(mk, a * accs[pi] + _mm(e.astype(BF16), v2), new_acc)
                accs[pi] = new_acc
            for h in range(nh):
                m_sc[h] = ms[h]
                l_sc[h] = ls[h]
            for pi in range(FOX_PAIRS):
                acc_sc[:, pi * LANES:(pi + 1) * LANES] = accs[pi]

        @pl.when(ki < qi)
        def _():
            step(False)

        @pl.when(ki == qi)
        def _():
            step(True)
            li = lax.broadcasted_iota(jnp.int32, (tq, LANES), 1)
            lse = jnp.zeros((tq, LANES), F32)
            for pi in range(FOX_PAIRS):
                lanes = slice(pi * LANES, (pi + 1) * LANES)
                inv = jnp.where(_lane_half((tq, LANES), False), 1.0 / l_sc[2 * pi], 1.0 / l_sc[2 * pi + 1])
                o = acc_sc[:, lanes] * inv
                gate = g_ref[:, lanes]
                o_ref[:, lanes] = o
                y_ref[:, lanes] = (o * gate * _sig(gate)).astype(BF16)
            for h in range(nh):
                lse = jnp.where(li == h, m_sc[h] + jnp.log(l_sc[h]), lse)
            lse_ref[0] = lse

    row = lambda off: pl.BlockSpec((tq, pw), lambda g, i, j: (i, off + g))
    key = lambda off: pl.BlockSpec((tk, pw), lambda g, i, j: (jnp.minimum(i, j), off + g))
    out = pl.BlockSpec((tq, pw), lambda g, i, j: (i, g))
    return _pallas_into(
        body, ycat, 5, 1, name="fox_fwd", grid=(c.rw // pw, nq, nq),
        in_specs=[row(qb), key(kb), key(vb), row(gb),
                  pl.BlockSpec((LANES, tk), lambda g, i, j: (0, jnp.minimum(i, j)))],
        out_specs=[out, row(c.rw // pw), pl.BlockSpec((1, tq, LANES), lambda g, i, j: (g, i, 0))],
        out_shape=[jax.ShapeDtypeStruct((t, c.rw), F32), jax.ShapeDtypeStruct(ycat.shape, BF16),
                   jax.ShapeDtypeStruct((c.rw // pw, t, LANES), F32)],
        scratch_shapes=[pltpu.VMEM((nh, tq, LANES), F32), pltpu.VMEM((nh, tq, LANES), F32),
                        pltpu.VMEM((tq, pw), F32)],
        compiler_params=_params("parallel", "parallel", "arbitrary"),
    )(p, p, p, p, cum_t)


def _fox2_grads(q2, k2, v2, do2, o2, lse_h, ck, mk, diag, tq, tk):
    zero = jnp.zeros_like(q2)
    s = _mm_nt(jnp.where(mk, q2, zero), k2) - ck
    if diag:
        s = jnp.where(_causal(tq, tk), s, NEG)
    wide = lambda col: jnp.concatenate([jnp.broadcast_to(col, (tq, LANES))] * (tk // LANES), axis=1)
    pm = jnp.exp(s - wide(lse_h))
    delta = jnp.sum(jnp.where(mk, do2 * o2, 0.0), axis=1, keepdims=True)
    dob = do2.astype(BF16)
    dp = _mm_nt(jnp.where(mk, dob, zero), v2)
    return pm, pm * (dp - wide(delta)), dob


def _fox2_bwd(p, c, cum_t, lse, o, do, tb, dp):
    t = p.shape[0]
    tq = tk = _tile(t, tb, LANES)
    nq = t // tq
    pw, nh = FOX_PAIRS * LANES, FOX_HEADS_STEP
    ng = c.rw // pw
    qb, kb, vb = (o_ // pw for o_ in (c.o_fq, c.o_fk, c.o_fv))
    scale = HEAD_DIM ** -0.5

    def body(q_ref, k_ref, v_ref, ck_ref, lse_ref, o_ref, do_ref,
             dk_ref, dv_ref, dck_ref, dqp_ref, dcqp_ref, dk_sc, dv_sc, dc_sc):
        g, ki, qi = pl.program_id(0), pl.program_id(1), pl.program_id(2)

        @pl.when(qi == 0)
        def _():
            dk_sc[...] = jnp.zeros_like(dk_sc)
            dv_sc[...] = jnp.zeros_like(dv_sc)
            dc_sc[...] = jnp.zeros_like(dc_sc)

        def step(diag):
            lse_blk = lse_ref[0]
            dcs = [dc_sc[h] for h in range(nh)]
            dks = [dk_sc[:, pi * LANES:(pi + 1) * LANES] for pi in range(FOX_PAIRS)]
            dvs = [dv_sc[:, pi * LANES:(pi + 1) * LANES] for pi in range(FOX_PAIRS)]
            rows = []
            for pi in range(FOX_PAIRS):
                lanes = slice(pi * LANES, (pi + 1) * LANES)
                q2 = (q_ref[:, lanes] * scale).astype(BF16)
                k2, v2 = k_ref[:, lanes].astype(BF16), v_ref[:, lanes].astype(BF16)
                do2, o2 = do_ref[:, lanes], o_ref[:, lanes]
                new_dk, new_dv, dq2 = dks[pi], dvs[pi], None
                for hh in range(2):
                    hi = 2 * pi + hh
                    mk = _lane_half((tk, LANES), hh == 1)
                    pm, ds, dob = _fox2_grads(q2, k2, v2, do2, o2, _col(lse_blk, hi),
                                              ck_ref[pl.ds(g * nh + hi, 1), :], mk, diag, tq, tk)
                    dsb = ds.astype(BF16)
                    dcs[hi] = dcs[hi] - jnp.sum(ds, axis=0, keepdims=True)
                    rows.append(jnp.sum(ds, axis=1, keepdims=True))
                    new_dv = jnp.where(mk, dvs[pi] + _mm_tn(pm.astype(BF16), dob), new_dv)
                    new_dk = jnp.where(mk, dks[pi] + _mm_tn(dsb, q2), new_dk)
                    part = _mm(dsb, k2)
                    dq2 = part if hh == 0 else jnp.where(mk, part, dq2)
                dks[pi], dvs[pi] = new_dk, new_dv
                dqp_ref[0, :, lanes] = dq2 * scale
            dcqp_ref[0, 0] = _from_cols(rows)
            for h in range(nh):
                dc_sc[h] = dcs[h]
            for pi in range(FOX_PAIRS):
                dk_sc[:, pi * LANES:(pi + 1) * LANES] = dks[pi]
                dv_sc[:, pi * LANES:(pi + 1) * LANES] = dvs[pi]

        @pl.when(qi > ki)
        def _():
            step(False)

        @pl.when(qi == ki)
        def _():
            step(True)

        @pl.when(qi == nq - 1)
        def _():
            dk_ref[...] = dk_sc[...].astype(BF16)
            dv_ref[...] = dv_sc[...].astype(BF16)
            dck_ref[0] = _from_rows([dc_sc[h] for h in range(nh)])

    row = lambda off: pl.BlockSpec((tq, pw), lambda g, j, i: (jnp.maximum(i, j), off + g))
    key = lambda off: pl.BlockSpec((tk, pw), lambda g, j, i: (j, off + g))
    return _pallas_into(
        body, dp, 7, 0, name="fox_bwd", grid=(ng, nq, nq),
        in_specs=[row(qb), key(kb), key(vb), pl.BlockSpec((LANES, tk), lambda g, j, i: (0, j)),
                  pl.BlockSpec((1, tq, LANES), lambda g, j, i: (g, jnp.maximum(i, j), 0)), row(0), row(0)],
        out_specs=[key(kb), key(0), pl.BlockSpec((1, nh, tk), lambda g, j, i: (g, 0, j)),
                   pl.BlockSpec((1, tq, pw), lambda g, j, i: (j, jnp.maximum(i, j), g)),
                   pl.BlockSpec((1, 1, tq, nh), lambda g, j, i: (j, g, jnp.maximum(i, j), 0))],
        out_shape=[jax.ShapeDtypeStruct(dp.shape, BF16), jax.ShapeDtypeStruct((t, c.rw), BF16),
                   jax.ShapeDtypeStruct((ng, nh, t), F32), jax.ShapeDtypeStruct((nq, t, c.rw), F32),
                   jax.ShapeDtypeStruct((nq, ng, t, nh), F32)],
        scratch_shapes=[pltpu.VMEM((tk, pw), F32), pltpu.VMEM((tk, pw), F32), pltpu.VMEM((nh, 1, tk), F32)],
        compiler_params=_params("parallel", "parallel", "arbitrary"),
    )(p, p, p, cum_t, lse, o, do)


def _fox2_dq_sum(dq_part, dcq_part, tq):
    nk, t, rw = dq_part.shape
    ng, nh = dcq_part.shape[1], dcq_part.shape[3]

    def body(p_ref, c_ref, dq_ref, dcq_ref, acc, cacc):
        i, j = pl.program_id(0), pl.program_id(1)

        @pl.when(j == 0)
        def _():
            acc[...] = p_ref[0]
            cacc[...] = c_ref[0]

        @pl.when((j > 0) & (j <= i))
        def _():
            acc[...] += p_ref[0]
            cacc[...] += c_ref[0]

        @pl.when(j == nk - 1)
        def _():
            dq_ref[...] = acc[...].astype(BF16)
            dcq_ref[...] = cacc[...]

    return pl.pallas_call(
        body, name="fox_dq_sum", grid=(t // tq, nk),
        in_specs=[pl.BlockSpec((1, tq, rw), lambda i, j: (jnp.minimum(i, j), i, 0)),
                  pl.BlockSpec((1, ng, tq, nh), lambda i, j: (jnp.minimum(i, j), 0, i, 0))],
        out_specs=[pl.BlockSpec((tq, rw), lambda i, j: (i, 0)), pl.BlockSpec((ng, tq, nh), lambda i, j: (0, i, 0))],
        out_shape=[jax.ShapeDtypeStruct((t, rw), BF16), jax.ShapeDtypeStruct((ng, t, nh), F32)],
        scratch_shapes=[pltpu.VMEM((tq, rw), F32), pltpu.VMEM((ng, tq, nh), F32)],
        compiler_params=_params("parallel", "arbitrary"),
    )(dq_part, dcq_part)


def _mem_probs(q, mk, scale):
    s = _mm_nt(q.astype(BF16), mk.astype(BF16)) * scale
    e = jnp.exp(s - jnp.max(s, axis=1, keepdims=True))
    return e / jnp.sum(e, axis=1, keepdims=True)


def _mem_attn_fwd(p, c, mkv):
    t = p.shape[0]
    tm = _tile(t, 512, 8)
    dh = c.mhd
    qb = c.o_mq // dh
    scale = dh ** -0.5

    def body(q_ref, mk_ref, mv_ref, o_ref):
        pm = _mem_probs(q_ref[...], mk_ref[...], scale)
        o_ref[...] = _mm(pm.astype(BF16), mv_ref[...].astype(BF16))

    m = mkv.shape[0]
    return pl.pallas_call(
        body, name="mem_attn_fwd", grid=(t // tm, MEM_HEADS),
        in_specs=[pl.BlockSpec((tm, dh), lambda i, j: (i, qb + j)),
                  pl.BlockSpec((m, dh), lambda i, j: (0, j)),
                  pl.BlockSpec((m, dh), lambda i, j: (0, MEM_HEADS + j))],
        out_specs=pl.BlockSpec((tm, dh), lambda i, j: (i, j)),
        out_shape=jax.ShapeDtypeStruct((t, c.mw), F32),
        compiler_params=_params("parallel", "parallel"),
    )(p, mkv, mkv)


def _mem_attn_bwd(p, c, mkv, do):
    t = p.shape[0]
    tm = _tile(t, 512, 8)
    dh = c.mhd
    qb = c.o_mq // dh
    scale = dh ** -0.5
    m = mkv.shape[0]

    def body(q_ref, mk_ref, mv_ref, do_ref, dq_ref, dmk_ref, dmv_ref):
        @pl.when(pl.program_id(1) == 0)
        def _():
            dmk_ref[...] = jnp.zeros_like(dmk_ref)
            dmv_ref[...] = jnp.zeros_like(dmv_ref)

        qv = q_ref[...].astype(BF16)
        pm = _mem_probs(qv, mk_ref[...], scale)
        dob = do_ref[...].astype(BF16)
        dmv_ref[...] += _mm_tn(pm.astype(BF16), dob)
        dp = _mm_nt(dob, mv_ref[...].astype(BF16))
        ds = (pm * (dp - jnp.sum(pm * dp, axis=1, keepdims=True)) * scale).astype(BF16)
        dq_ref[...] = _mm(ds, mk_ref[...].astype(BF16)).astype(BF16)
        dmk_ref[...] += _mm_tn(ds, qv)

    kvb = lambda off: pl.BlockSpec((m, dh), lambda j, i: (0, off + j))
    return pl.pallas_call(
        body, name="mem_attn_bwd", grid=(MEM_HEADS, t // tm),
        in_specs=[pl.BlockSpec((tm, dh), lambda j, i: (i, qb + j)), kvb(0), kvb(MEM_HEADS),
                  pl.BlockSpec((tm, dh), lambda j, i: (i, j))],
        out_specs=[pl.BlockSpec((tm, dh), lambda j, i: (i, j)), kvb(0), kvb(0)],
        out_shape=[jax.ShapeDtypeStruct((t, c.mw), BF16), jax.ShapeDtypeStruct((m, c.mw), F32),
                   jax.ShapeDtypeStruct((m, c.mw), F32)],
        compiler_params=_params("parallel", "arbitrary"),
    )(p, mkv, mkv, do)


def _adamw(w, g, m, v, name):
    rows, cols = w.shape
    bc1 = 1.0 - ADAM_B1 ** ADAM_STEP
    bc2 = 1.0 - ADAM_B2 ** ADAM_STEP
    if rows % 8 and rows > 8:
        blk = pl.BlockSpec((rows, LANES), lambda i: (0, i))
        g_blk = pl.BlockSpec((g.shape[0], LANES), lambda i: (0, i))
        grid = (cols // LANES,)
    else:
        tm = _tile(rows, max(8, (1 << 18) // cols // 8 * 8), 8)
        blk = pl.BlockSpec((tm, cols), lambda i: (i, 0))
        g_blk = pl.BlockSpec((tm, g.shape[1]), lambda i: (i, 0))
        grid = (rows // tm,)
    brows, bcols = blk.block_shape

    def body(w_ref, g_ref, m_ref, v_ref, go_ref, d_ref, mo_ref, vo_ref):
        gv = g_ref[0:brows, 0:bcols]
        mn = ADAM_B1 * m_ref[...] + (1.0 - ADAM_B1) * gv
        vn = ADAM_B2 * v_ref[...] + (1.0 - ADAM_B2) * (gv * gv)
        go_ref[...] = gv
        mo_ref[...] = mn
        vo_ref[...] = vn
        d_ref[...] = -ADAM_LR * ((mn / bc1) / (jnp.sqrt(vn / bc2) + ADAM_EPS) + ADAM_WD * w_ref[...])

    shp = jax.ShapeDtypeStruct((rows, cols), F32)
    return pl.pallas_call(
        body, name=name, grid=grid,
        in_specs=[blk, g_blk, blk, blk],
        out_specs=[blk] * 4, out_shape=[shp] * 4,
        compiler_params=_params("parallel"),
    )(w, g, m, v)


SCAN_HEADS = 12
FOX_BLOCK = 512


def _local_step(c, x, mem, tgt, w, riders=None):
    t = x.shape[0]
    rw = c.rw
    riders = riders or {}
    carried = {}
    hd = lambda z: z.reshape(t, c.h, HEAD_DIM).transpose(1, 0, 2)
    uh = lambda z: z.transpose(1, 0, 2).reshape(t, rw)
    vecs = (w["mu"], w["w0"], w["a0"], w["k_k"], w["k_a"], w["wd"], w["wi"])

    h, rinv = _rms_fwd(x, w["g_pre"], "rms_pre")
    if "in_proj" in riders:
        groups, finish = riders["in_proj"]
        p, late = _matmul(h, w["wp"], name="in_proj", tk=4096, attach=groups)
        w = dict(w, **finish(late))
    else:
        p = _matmul(h, w["wp"], name="in_proj", tk=4096)
    r, lw, km, v, a, b = _rwkv_pre_fwd(p, c, *vecs)
    scan_in = tuple(hd(z) for z in (r, lw, km, v, a, b))
    hb = max(n for n in range(1, SCAN_HEADS + 1) if c.h % n == 0)
    ysh, states = _rwkv_scan_fwd(*scan_in, hb)
    ys = uh(ysh)
    ycat = _rwkv_post_fwd(ys, r, km, v, p, c, w["ln_w"], w["ln_b"], w["r_k"])

    cum_t = _fox_prep_t(p, c, w["b_f"])
    yfox, ycat, lse = _fox2_fwd(p, c, cum_t, FOX_BLOCK, ycat)

    memn, rinv_m = _rms_fwd(mem, w["g_mem"], "rms_mem")
    mkv = _matmul(memn, w["w_mem_kv"], name="mem_kv")
    ymem = _mem_attn_fwd(p, c, mkv)
    ycat = _gate_fwd(ymem, p, c.o_gmq, "gate_mem", ycat, 2 * rw)
    yo =_matmul(ycat, w["w_out"], name="out_proj", tn=512, tk=4096)
    loss, dout, dyo, dg_post = _post_loss(yo, x, tgt, w["g_post"], "post_loss")

    dyc = _matmul(dyo, w["w_out"], tb=True, name="d_ycat", tn=512, tk=4096)
    dw_out = _matmul(ycat, dyo, ta=True, name="d_w_out", tn=512, tk=4096, out_dtype=BF16)
    dys, dr2, dkm2, dv2, dp, dln_w, dln_b, dr_k = _rwkv_post_bwd(
        dyc, ys, r, km, v, p, c, w["ln_w"], w["ln_b"], w["r_k"])
    dyf, dp = _gate_bwd(dyc, rw, yfox, p, c.o_gfox, "gate_fox_bwd", dp)
    dym, dp = _gate_bwd(dyc, 2 * rw, ymem, p, c.o_gmq, "gate_mem_bwd", dp)

    scan_g = _rwkv_scan_bwd(*scan_in, states, hd(dys), hb)
    dps, dzw, dza, twb, alb, dw0, da0, dk_k, dk_a = _rwkv_pre_bwd(
        p, c, *vecs, *(uh(z) for z in scan_g), dr2, dkm2, dv2)
    dwd = _matmul(twb, dzw, ta=True, name="d_w_decay", out_dtype=BF16)
    dwi = _matmul(alb, dza, ta=True, name="d_w_iclr", out_dtype=BF16)
    dp, dmu = _shift_bwd(dps, p, c, w["mu"], dp)

    dp, dfv, dck, dq_part, dcq_part = _fox2_bwd(p, c, cum_t, lse, yfox, dyf, FOX_BLOCK, dp)
    dfq, dcq = _fox2_dq_sum(dq_part, dcq_part, _tile(t, FOX_BLOCK, LANES))
    dcum = jnp.pad(jnp.stack([dcq.transpose(1, 0, 2).reshape(t, c.h), dck.reshape(c.h, t).T]),
                   ((0, 0), (0, 0), (0, LANES - c.h)))
    dfl, db_f = _fox_logit_bwd(dcum, p, c, w["b_f"])

    dmq, dmk, dmv = _mem_attn_bwd(p, c, mkv, dym)
    dmkv = jnp.concatenate([dmk, dmv], axis=1)
    dw_mkv = _matmul(memn, dmkv, ta=True, name="d_w_mem_kv", out_dtype=BF16)
    dmemn = _matmul(dmkv, w["w_mem_kv"], tb=True, name="d_memn")
    _, dg_mem = _rms_bwd(dmemn, mem, rinv_m, w["g_mem"], jnp.zeros_like(mem), "rms_mem_bwd")

    for off, piece in ((c.o_fq, dfq), (c.o_fv, dfv), (c.o_mq, dmq), (c.o_fl, dfl)):
        dp = lax.dynamic_update_slice(dp, piece, (0, off))
    rest = dict(wd=dwd, wi=dwi, w_mem_kv=dw_mkv, w_out=dw_out)
    if "d_w_in" in riders:
        dwp, carried["rest"] = _matmul(dp, h, ta=True, name="d_w_in", tk=4096, out_dtype=BF16,
                                       attach=riders["d_w_in"](rest))
    else:
        dwp = _matmul(dp, h, ta=True, name="d_w_in", tk=4096, out_dtype=BF16)
    if "d_h" in riders:
        dh, carried["wp"] = _matmul(dp, w["wp"], tb=True, name="d_h", tk=2944, attach=riders["d_h"](dwp))
    else:
        dh = _matmul(dp, w["wp"], tb=True, name="d_h", tk=2944)
    grad_x, dg_pre = _rms_bwd(dh, x, rinv, w["g_pre"], dout, "rms_pre_bwd")

    small = dict(g_pre=dg_pre, mu=dmu, w0=dw0, a0=da0, k_k=dk_k, k_a=dk_a, r_k=dr_k, ln_w=dln_w, ln_b=dln_b,
                 b_f=db_f, g_mem=dg_mem, g_post=dg_post)
    return loss, grad_x, dict(wp=dwp, **rest), small, carried


CHIPS = ((1, 0, 0), (0, 1, 0), (1, 1, 0))
SIBLING = ((0, 0, 1),)
ALL_PEERS = tuple((i, j, k) for i in (0, 1) for j in (0, 1) for k in (0, 1))[1:]


def _chip_of(pos):
    return 2 * pos[0] + pos[1]


DMA_CHUNK = 4 << 20


def _pieces(shape, itemsize):
    lead, (rows, cols) = shape[:-2], shape[-2:]
    k = 1
    if rows % 16 == 0:
        k = max(1, min(rows // 16, -(-rows * cols * itemsize // DMA_CHUNK)))
        while rows % k or (rows // k) % 16:
            k -= 1
    band = rows // k
    idxs = [()]
    for n in lead:
        idxs = [i + (j,) for i in idxs for j in range(n)]
    return [i + (pl.ds(j * band, band),) for i in idxs for j in range(k)]


def _peer_of(me, mask):
    return tuple(1 - v if f else v for v, f in zip(me, mask))


def _exchange(name, groups):
    n = len(groups)
    plan = _plan(groups)

    def body(*refs):
        copies = _copies(groups, plan, refs[:n], refs[n:2 * n], refs[2 * n], refs[2 * n + 1])
        for cp in copies:
            cp.start()
        for cp in copies:
            cp.wait()

    any_spec = pl.BlockSpec(memory_space=pl.ANY)
    return pl.pallas_call(
        body, name=name,
        in_specs=[any_spec] * n, out_specs=[any_spec] * n,
        out_shape=_exchange_shapes(groups),
        input_output_aliases={gi: gi for gi, g in enumerate(groups) if g.get("inplace")},
        scratch_shapes=[pltpu.SemaphoreType.DMA((len(plan),)), pltpu.SemaphoreType.DMA((len(plan),))],
    )(*[g["src"] for g in groups])


def _plan(groups):
    return [(gi, ti, idx) for gi, g in enumerate(groups) for ti in range(len(g["transfers"]))
            for idx in _pieces(tuple(g["piece"]), g["src"].dtype.itemsize)]


def _exchange_shapes(groups):
    lead = lambda s: tuple(s) if isinstance(s, tuple) else (s,)
    return [jax.ShapeDtypeStruct(lead(g["slots"]) + tuple(g["piece"]), g["src"].dtype) for g in groups]


def _copies(groups, plan, srcs, outs, send_sems, recv_sems):
    me = (lax.axis_index("x"), lax.axis_index("y"), lax.axis_index("c"))
    copies = []
    for k, (gi, ti, idx) in enumerate(plan):
        mask, view, slot = groups[gi]["transfers"][ti]
        peer = _peer_of(me, mask)
        copies.append(pltpu.make_async_remote_copy(
            src_ref=view(srcs[gi], me, peer).at[idx], dst_ref=outs[gi].at[slot(me, peer)].at[idx],
            send_sem=send_sems.at[k], recv_sem=recv_sems.at[k],
            device_id=peer, device_id_type=MESH))
    return copies


def _my_chip():
    return 2 * lax.axis_index("x") + lax.axis_index("y")


def _put(buf, block, slot):
    return lax.dynamic_update_slice(buf, block[None], (slot,) + (0,) * block.ndim)


def _sum_slots(recv, own, k, out_dtype, name):
    s, rows, cols = recv.shape
    budget = max(16, ((4 << 20) // ((s + 1) * cols * 4)) // 16 * 16)
    tr = _tile(rows, budget, 16)
    own_many = own.shape[0] > 1

    def body(k_ref, *refs):
        out_ref = refs[s + 1]
        mine = refs[s][0].astype(F32)
        acc = None
        for i in range(s):
            term = jnp.where(k_ref[0] == i, mine, refs[i][0].astype(F32))
            acc = term if acc is None else acc + term
        out_ref[...] = acc.astype(out_ref.dtype)

    def slot_spec(i):
        return pl.BlockSpec((1, tr, cols), lambda j, kr: (jnp.where(kr[0] == i, (i + 1) % s, i), j, 0))

    grid_spec = pltpu.PrefetchScalarGridSpec(
        num_scalar_prefetch=1, grid=(rows // tr,),
        in_specs=[slot_spec(i) for i in range(s)]
                 + [pl.BlockSpec((1, tr, cols), lambda j, kr: (kr[0] if own_many else 0, j, 0))],
        out_specs=pl.BlockSpec((tr, cols), lambda j, kr: (j, 0)))
    return pl.pallas_call(
        body, name=name, grid_spec=grid_spec,
        out_shape=jax.ShapeDtypeStruct((rows, cols), out_dtype),
        compiler_params=_params("parallel"),
    )(k, *([recv] * s), own)


def _all_gather(shards):
    return _gather_finish(shards, _exchange("gather_chips", _gather_groups(shards)), "gather_pair")


def _gather_groups(shards):
    halves = [s.reshape(2, s.shape[0] // 2, s.shape[1]) for s in shards]
    return [dict(src=q, slots=(4, 2), piece=q.shape[1:],
                 transfers=[(m, lambda ref, me, peer: ref.at[me[2]], lambda me, peer: (_chip_of(me), me[2]))
                            for m in CHIPS])
            for q in halves]


def _gather_finish(shards, first, name):
    spot = lambda m: (lambda me: (_chip_of(_peer_of(me, m)), me[2]))
    both = _exchange(name, [
        dict(src=q, slots=(4, 2), piece=q.shape[2:], inplace=True,
             transfers=[(SIBLING[0], (lambda f: lambda ref, me, peer: ref.at[f(me)])(spot(m)),
                         (lambda f: lambda me, peer: f(me))(spot(m))) for m in CHIPS])
        for q in first])
    return [_put(q.reshape((4,) + s.shape), s, _my_chip()) for s, q in zip(shards, both)]


def _reduce_pair(partials, tag):
    core1 = lax.axis_index("c").reshape(1).astype(jnp.int32)
    halves = [q.reshape(4, 2, q.shape[1] // 2, q.shape[2]).transpose(1, 0, 2, 3) for q in partials]
    pair = _exchange("reduce_pair_" + tag, [
        dict(src=q, slots=2, piece=q.shape[1:],
             transfers=[(SIBLING[0], lambda ref, me, peer: ref.at[peer[2]], lambda me, peer: me[2])])
        for q in halves])
    flat = lambda e: e.reshape(2, -1, e.shape[-1])
    return [_sum_slots(flat(e), flat(q), core1, BF16, "reduce_pair_sum_" + tag).reshape(q.shape[1:])
            for e, q in zip(pair, halves)]


def _reduce_chips_groups(chip_sums):
    return [dict(src=q, slots=4, piece=q.shape[1:],
                 transfers=[(m, lambda ref, me, peer: ref.at[_chip_of(peer)], lambda me, peer: _chip_of(me))
                            for m in CHIPS])
            for q in chip_sums]


def _reduce_finish(crossed, chip_sums, tag):
    core = lax.axis_index("c")
    chip1 = _my_chip().reshape(1).astype(jnp.int32)
    sums = [_sum_slots(e, q, chip1, F32, "reduce_chips_sum_" + tag) for e, q in zip(crossed, chip_sums)]
    swapped = _exchange("reduce_swap_" + tag, [
        dict(src=q, slots=2, piece=q.shape, transfers=[(SIBLING[0], lambda ref, me, peer: ref, lambda me, peer: me[2])])
        for q in sums])
    return [_put(e, q, core).reshape(-1, e.shape[-1]) for e, q in zip(swapped, sums)]


def _reduce_scatter(partials):
    chip_sums = _reduce_pair(partials, "all")
    return _reduce_finish(_exchange("reduce_chips", _reduce_chips_groups(chip_sums)), chip_sums, "all")


def _all_reduce_small(vec):
    dev = 4 * lax.axis_index("x") + 2 * lax.axis_index("y") + lax.axis_index("c")
    got = _exchange("reduce_small", [
        dict(src=vec, slots=8, piece=vec.shape,
             transfers=[(m, lambda ref, me, peer: ref, lambda me, peer: 4 * me[0] + 2 * me[1] + me[2])
                        for m in ALL_PEERS])])[0]
    return _sum_slots(got, vec[None], dev.reshape(1).astype(jnp.int32), F32, "reduce_small_sum")


SMALL = ("g_pre", "mu", "w0", "a0", "k_k", "k_a", "r_k", "ln_w", "ln_b", "b_f", "g_mem", "g_post")


def _pad_cols(a, n):
    return jnp.pad(a, ((0, 0),) * (a.ndim - 1) + ((0, n - a.shape[-1]),))


def kernel(x, mem, g_pre, w_in, mu_rwkv, w0, w_decay_up, a0, w_iclr_up, k_k, k_a, r_k, ln_x_w, ln_x_b, b_f, g_mem, w_mem_kv, w_out, g_post, loss_target, m_g_pre, m_w_in, m_mu_rwkv, m_w0, m_w_decay_up, m_a0, m_w_iclr_up, m_k_k, m_k_a, m_r_k, m_ln_x_w, m_ln_x_b, m_b_f, m_g_mem, m_w_mem_kv, m_w_out, m_g_post, v_g_pre, v_w_in, v_mu_rwkv, v_w0, v_w_decay_up, v_a0, v_w_iclr_up, v_k_k, v_k_a, v_r_k, v_ln_x_w, v_ln_x_b, v_b_f, v_g_mem, v_w_mem_kv, v_w_out, v_g_post):
    d = x.shape[-1]
    c = Cfg(d)
    ws = w_in.shape[-1]
    wpad = -(-ws // LANES) * LANES
    nh = c.h

    g_in, g_wd, g_wi = _all_gather([
        _pad_cols(w_in[0].astype(BF16), wpad), w_decay_up[0].astype(BF16), w_iclr_up[0].astype(BF16)])
    fl = c.ref_fl
    runs = [(0, fl, 0), (fl, fl + nh, c.o_fl), (fl + nh, c.in_width, fl)]
    pieces = []
    for lo, hi, _ in sorted(runs, key=lambda r: r[2]):
        for s in range(4):
            a, b = max(lo, s * ws), min(hi, (s + 1) * ws)
            if a < b:
                pieces.append(g_in[s, :, a - s * ws:b - s * ws])
    wp = jnp.concatenate(pieces + [jnp.zeros((d, LANES - nh), BF16)], axis=1)
    unshard = lambda g: g.transpose(1, 0, 2).reshape(g.shape[1], -1)
    weights = dict(wp=wp, wd=unshard(g_wd), wi=unshard(g_wi),
                   g_pre=g_pre, mu=mu_rwkv, w0=w0, a0=a0, k_k=k_k, k_a=k_a, r_k=r_k.reshape(1, -1),
                   ln_w=ln_x_w, ln_b=ln_x_b, b_f=_pad_cols(b_f, LANES), g_mem=g_mem, g_post=g_post)
    late_shards = [w_out[0].astype(BF16), w_mem_kv[0].astype(BF16)]

    def late_weights(first):
        g_out, g_mkv = _gather_finish(late_shards, first, "gather_pair_late")
        return dict(w_out=g_out.reshape(-1, d), w_mem_kv=g_mkv.reshape(d, -1))

    by_chip = lambda g: jnp.stack(jnp.split(g, 4, axis=1))
    pair_sums = {}

    def ride_rest(g):
        pair_sums["rest"] = _reduce_pair([g["w_out"].reshape(4, -1, d), g["w_mem_kv"].reshape(4, d // 4, -1),
                                          by_chip(g["wd"]), by_chip(g["wi"])], "rest")
        return _reduce_chips_groups(pair_sums["rest"])

    def ride_wp(dwpt):
        shards = []
        for s in range(4):
            rows = []
            for lo, hi, at in runs:
                a, b = max(lo, s * ws), min(hi, (s + 1) * ws)
                if a < b:
                    rows.append(dwpt[at + a - lo:at + b - lo, :])
            part = rows[0] if len(rows) == 1 else jnp.concatenate(rows, axis=0)
            shards.append(jnp.pad(part, ((0, wpad - ws), (0, 0))))
        pair_sums["wp"] = _reduce_pair([jnp.stack(shards)], "w_in")
        return _reduce_chips_groups(pair_sums["wp"])

    loss, grad_x, _, small, carried = _local_step(
        c, x[0], mem[0], loss_target[0], weights,
        riders={"in_proj": (_gather_groups(late_shards), late_weights), "d_w_in": ride_rest, "d_h": ride_wp})
    red = (_reduce_finish(carried["wp"], pair_sums["wp"], "w_in")
           + _reduce_finish(carried["rest"], pair_sums["rest"], "rest"))
    big_w = (w_in[0].T, w_out[0], w_mem_kv[0], w_decay_up[0], w_iclr_up[0])
    big_m = (m_w_in[0].T, m_w_out[0], m_w_mem_kv[0], m_w_decay_up[0], m_w_iclr_up[0])
    big_v = (v_w_in[0].T, v_w_out[0], v_w_mem_kv[0], v_w_decay_up[0], v_w_iclr_up[0])
    big_names = ("w_in", "w_out", "w_mem_kv", "w_decay_up", "w_iclr_up")
    upd = {n: _adamw(w_, g_, m_, v_, "adamw_" + n) for n, w_, g_, m_, v_ in zip(big_names, big_w, red, big_m, big_v)}
    upd["w_in"] = [o.T for o in upd["w_in"]]

    small_w = dict(g_pre=g_pre, mu=mu_rwkv, w0=w0, a0=a0, k_k=k_k, k_a=k_a, r_k=r_k.reshape(1, -1), ln_w=ln_x_w,
                   ln_b=ln_x_b, b_f=b_f, g_mem=g_mem, g_post=g_post)
    small_m = dict(g_pre=m_g_pre, mu=m_mu_rwkv, w0=m_w0, a0=m_a0, k_k=m_k_k, k_a=m_k_a, r_k=m_r_k.reshape(1, -1),
                   ln_w=m_ln_x_w, ln_b=m_ln_x_b, b_f=m_b_f, g_mem=m_g_mem, g_post=m_g_post)
    small_v = dict(g_pre=v_g_pre, mu=v_mu_rwkv, w0=v_w0, a0=v_a0, k_k=v_k_k, k_a=v_k_a, r_k=v_r_k.reshape(1, -1),
                   ln_w=v_ln_x_w, ln_b=v_ln_x_b, b_f=v_b_f, g_mem=v_g_mem, g_post=v_g_post)
    widths = [-(-small_w[n].shape[1] // LANES) * LANES for n in SMALL]
    pack = lambda t: jnp.concatenate([_pad_cols(t[n], wd_) for n, wd_ in zip(SMALL, widths)]
                                     + [jnp.zeros((1, LANES), F32)], axis=1)
    g_packed = jnp.concatenate([_pad_cols(small[n], wd_) for n, wd_ in zip(SMALL, widths)]
                               + [_pad_cols(loss, LANES)], axis=1)
    g_sum = _all_reduce_small(g_packed)
    s_upd = _adamw(pack(small_w), g_sum, pack(small_m), pack(small_v), "adamw_small")
    offs = [sum(widths[:i]) for i in range(len(SMALL))]

    def take(kind, n):
        i = SMALL.index(n)
        piece = s_upd[kind][:, offs[i]:offs[i] + small_w[n].shape[1]]
        return piece.reshape(r_k.shape) if n == "r_k" else piece

    total_loss = g_sum[0, sum(widths)]
    order = ("g_pre", "w_in", "mu", "w0", "w_decay_up", "a0", "w_iclr_up", "k_k", "k_a", "r_k", "ln_w", "ln_b", "b_f",
             "g_mem", "w_mem_kv", "w_out", "g_post")
    outs = [total_loss, grad_x[None]]
    for kind in range(4):
        for n in order:
            outs.append(upd[n][kind][None] if n in upd else take(kind, n))
    return tuple(outs)
```

```python
import jax
import jax.numpy as jnp
from jax import lax
from jax.experimental import pallas as pl
from jax.experimental.pallas import tpu as pltpu

F32 = jnp.float32
BF16 = jnp.bfloat16
HI = lax.Precision.HIGHEST
MESH = pl.DeviceIdType.MESH

HEAD_DIM = 64
MEM_HEADS = 4
LORA = 128
CHUNK = 64
RMS_EPS = 1e-6
GN_EPS = 64e-5
LANES = 128
WIDE = 2 * LANES
VMEM_LIMIT = 56 * 1024 * 1024

ADAM_LR, ADAM_B1, ADAM_B2, ADAM_EPS, ADAM_WD, ADAM_STEP = 0.001, 0.9, 0.999, 1e-08, 0.01, 10


class Cfg:
    def __init__(self, d):
        self.d = d
        self.rw = 3 * d // 8
        self.mw = d // 4
        self.h = self.rw // HEAD_DIM
        self.mhd = self.mw // MEM_HEADS
        self.shift = 3 * self.rw + 2 * LORA
        self.in_width = self.shift + 5 * self.rw + self.h + 2 * self.mw
        o = self.shift
        self.o_grw = o; o += self.rw
        self.o_fq = o; o += self.rw
        self.o_fk = o; o += self.rw
        self.o_fv = o; o += self.rw
        self.o_gfox = o; o += self.rw
        self.o_mq = o; o += self.mw
        self.o_gmq = o; o += self.mw
        self.o_fl = o; o += LANES
        self.wp = o
        self.ref_fl = self.shift + 4 * self.rw


def _tile(n, pref, align=LANES):
    if n <= pref:
        return n
    t = (pref // align) * align
    while t >= align:
        if n % t == 0:
            return t
        t -= align
    return n


def _params(*sem):
    return pltpu.CompilerParams(dimension_semantics=sem, vmem_limit_bytes=VMEM_LIMIT)


def _pallas_into(body, into, n_in, out_index, in_specs, **kw):
    if into is None:
        return pl.pallas_call(body, in_specs=in_specs, **kw)

    def body_with_alias(*refs):
        return body(*refs[:n_in], *refs[n_in + 1:])

    call = pl.pallas_call(body_with_alias, in_specs=list(in_specs) + [pl.BlockSpec(memory_space=pl.ANY)],
                          input_output_aliases={n_in: out_index}, **kw)
    return lambda *args: call(*args, into)


def _sig(x):
    return 1.0 / (1.0 + jnp.exp(-x))


def _softplus(x):
    return jnp.maximum(x, 0.0) + jnp.log(1.0 + jnp.exp(-jnp.abs(x)))


def _dot(a, b, dims, prec=None):
    return lax.dot_general(a, b, (dims, ((), ())), precision=prec, preferred_element_type=F32)


def _mm(a, b, prec=None):
    return _dot(a, b, ((1,), (0,)), prec)


def _mm_nt(a, b, prec=None):
    return _dot(a, b, ((1,), (1,)), prec)


def _mm_tn(a, b, prec=None):
    return _dot(a, b, ((0,), (0,)), prec)


def _split(a):
    hi = a.astype(BF16)
    return hi, (a - hi.astype(F32)).astype(BF16)


def _dot3(a, b, dims, passes=3):
    d = lambda x, y: lax.dot_general(x, y, dims, preferred_element_type=F32)
    if passes == 1:
        return d(a.astype(BF16), b.astype(BF16))
    if passes == 2:
        ah, (bh, bl) = a.astype(BF16), _split(b)
        return d(ah, bh) + d(ah, bl)
    (ah, al), (bh, bl) = _split(a), _split(b)
    return d(ah, bh) + (d(ah, bl) + d(al, bh))


def _bmm(a, b, passes=3):
    return _dot3(a, b, (((2,), (1,)), ((0,), (0,))), passes)


def _bmm_nt(a, b, passes=3):
    return _dot3(a, b, (((2,), (2,)), ((0,), (0,))), passes)


def _bmm_tn(a, b, passes=3):
    return _dot3(a, b, (((1,), (1,)), ((0,), (0,))), passes)


def _bmm_01(m01, x):
    x1 = x.astype(BF16)
    r1 = x - x1.astype(F32)
    x2 = r1.astype(BF16)
    x3 = (r1 - x2.astype(F32)).astype(BF16)
    d = lambda y: lax.dot_general(m01, y, (((2,), (1,)), ((0,), (0,))), preferred_element_type=F32)
    return d(x1) + (d(x2) + d(x3))


def _matmul(a, b, *, ta=False, tb=False, out_dtype=F32, name, tm=1024, tn=1024, tk=1024, attach=None):
    m, k = (a.shape[1], a.shape[0]) if ta else a.shape
    n = b.shape[0] if tb else b.shape[1]
    tm, tn, tk = _tile(m, tm), _tile(n, tn), _tile(k, tk)
    nk = k // tk
    grid = (m // tm, n // tn, nk)
    dims = ((0 if ta else 1,), (1 if tb else 0,))
    groups = attach or []
    ng = len(groups)
    plan = _plan(groups)

    def body(a_ref, b_ref, *rest):
        srcs, o_ref, outs, scratch = rest[:ng], rest[ng], rest[ng + 1:2 * ng + 1], rest[2 * ng + 1:]
        acc = scratch[0] if nk > 1 else None
        if ng:
            copies = _copies(groups, plan, srcs, outs, scratch[-2], scratch[-1])
            ids = [pl.program_id(ax) for ax in range(3)]

            @pl.when((ids[0] == 0) & (ids[1] == 0) & (ids[2] == 0))
            def _():
                for cp in copies:
                    cp.start()

        part = _dot(a_ref[...].astype(BF16), b_ref[...].astype(BF16), dims)
        if nk == 1:
            o_ref[...] = part.astype(o_ref.dtype)
        else:
            kk = pl.program_id(2)

            @pl.when(kk == 0)
            def _():
                acc[...] = part

            @pl.when(kk > 0)
            def _():
                acc[...] += part

            @pl.when(kk == nk - 1)
            def _():
                o_ref[...] = acc[...].astype(o_ref.dtype)

        if ng:
            @pl.when((ids[0] == grid[0] - 1) & (ids[1] == grid[1] - 1) & (ids[2] == grid[2] - 1))
            def _():
                for cp in copies:
                    cp.wait()

    a_spec = pl.BlockSpec((tk, tm), lambda i, j, kk: (kk, i)) if ta else pl.BlockSpec((tm, tk), lambda i, j, kk: (i, kk))
    b_spec = pl.BlockSpec((tn, tk), lambda i, j, kk: (j, kk)) if tb else pl.BlockSpec((tk, tn), lambda i, j, kk: (kk, j))
    any_spec = pl.BlockSpec(memory_space=pl.ANY)
    sems = [pltpu.SemaphoreType.DMA((len(plan),)), pltpu.SemaphoreType.DMA((len(plan),))] if ng else []
    res = pl.pallas_call(
        body, name=name, grid=grid,
        in_specs=[a_spec, b_spec] + [any_spec] * ng,
        out_specs=[pl.BlockSpec((tm, tn), lambda i, j, kk: (i, j))] + [any_spec] * ng,
        out_shape=[jax.ShapeDtypeStruct((m, n), out_dtype)] + _exchange_shapes(groups),
        scratch_shapes=([pltpu.VMEM((tm, tn), F32)] if nk > 1 else []) + sems,
        compiler_params=_params(*(("arbitrary",) * 3 if ng else ("parallel", "parallel", "arbitrary"))),
    )(a, b, *[g["src"] for g in groups])
    return (res[0], list(res[1:])) if ng else res[0]


def _rms_fwd(x, g, name):
    t, d = x.shape
    tm = _tile(t, 256, 8)

    def body(x_ref, g_ref, h_ref, r_ref):
        xv = x_ref[...]
        r = lax.rsqrt(jnp.mean(xv * xv, axis=-1, keepdims=True) + RMS_EPS)
        h_ref[...] = (xv * r * g_ref[...]).astype(BF16)
        r_ref[...] = r

    return pl.pallas_call(
        body, name=name, grid=(t // tm,),
        in_specs=[pl.BlockSpec((tm, d), lambda i: (i, 0)), pl.BlockSpec((1, d), lambda i: (0, 0))],
        out_specs=[pl.BlockSpec((tm, d), lambda i: (i, 0)), pl.BlockSpec((tm, 1), lambda i: (i, 0))],
        out_shape=[jax.ShapeDtypeStruct((t, d), BF16), jax.ShapeDtypeStruct((t, 1), F32)],
        compiler_params=_params("parallel"),
    )(x, g)


def _rms_bwd(dh, x, rinv, g, add, name):
    t, d = x.shape
    tm = _tile(t, 256, 8)

    def body(dh_ref, x_ref, r_ref, g_ref, add_ref, dx_ref, dg_ref):
        @pl.when(pl.program_id(0) == 0)
        def _():
            dg_ref[...] = jnp.zeros_like(dg_ref)

        r = r_ref[...]
        xn = x_ref[...] * r
        dhv = dh_ref[...]
        dg_ref[...] += jnp.sum(dhv * xn, axis=0, keepdims=True)
        dxn = dhv * g_ref[...]
        dx_ref[...] = add_ref[...] + r * (dxn - xn * jnp.mean(dxn * xn, axis=-1, keepdims=True))

    row = pl.BlockSpec((tm, d), lambda i: (i, 0))
    vec = pl.BlockSpec((1, d), lambda i: (0, 0))
    return pl.pallas_call(
        body, name=name, grid=(t // tm,),
        in_specs=[row, row, pl.BlockSpec((tm, 1), lambda i: (i, 0)), vec, row],
        out_specs=[row, vec],
        out_shape=[jax.ShapeDtypeStruct((t, d), F32), jax.ShapeDtypeStruct((1, d), F32)],
        compiler_params=_params("arbitrary"),
    )(dh, x, rinv, g, add)


def _post_loss(yo, x, tgt, g, name):
    t, d = x.shape
    tm = _tile(t, 256, 8)

    def body(yo_ref, x_ref, t_ref, g_ref, loss_ref, dout_ref, dyo_ref, dg_ref):
        @pl.when(pl.program_id(0) == 0)
        def _():
            dg_ref[...] = jnp.zeros_like(dg_ref)
            loss_ref[...] = jnp.zeros_like(loss_ref)

        yv = yo_ref[...]
        r = lax.rsqrt(jnp.mean(yv * yv, axis=-1, keepdims=True) + RMS_EPS)
        n = yv * r
        err = x_ref[...] + n * g_ref[...] - t_ref[...]
        loss_ref[...] += 0.5 * jnp.sum(jnp.mean(err * err, axis=-1, keepdims=True), axis=0, keepdims=True)
        dout = err * (1.0 / d)
        dout_ref[...] = dout
        dg_ref[...] += jnp.sum(dout * n, axis=0, keepdims=True)
        dn = dout * g_ref[...]
        dyo_ref[...] = (r * (dn - n * jnp.mean(dn * n, axis=-1, keepdims=True))).astype(BF16)

    row = pl.BlockSpec((tm, d), lambda i: (i, 0))
    vec = pl.BlockSpec((1, d), lambda i: (0, 0))
    return pl.pallas_call(
        body, name=name, grid=(t // tm,),
        in_specs=[row, row, row, vec],
        out_specs=[pl.BlockSpec((1, 1), lambda i: (0, 0)), row, row, vec],
        out_shape=[jax.ShapeDtypeStruct((1, 1), F32), jax.ShapeDtypeStruct((t, d), F32),
                   jax.ShapeDtypeStruct((t, d), BF16), jax.ShapeDtypeStruct((1, d), F32)],
        compiler_params=_params("arbitrary"),
    )(yo, x, tgt, g)


def _head_sum(x):
    ri = lax.broadcasted_iota(jnp.int32, (LANES, LANES), 0) // HEAD_DIM
    ci = lax.broadcasted_iota(jnp.int32, (LANES, LANES), 1) // HEAD_DIM
    e = (ri == ci).astype(BF16)
    x1 = x.astype(BF16)
    r1 = x - x1.astype(F32)
    x2 = r1.astype(BF16)
    x3 = (r1 - x2.astype(F32)).astype(BF16)
    parts = []
    for i in range(x.shape[1] // LANES):
        sl = slice(i * LANES, (i + 1) * LANES)
        parts.append(_mm(x1[:, sl], e) + (_mm(x2[:, sl], e) + _mm(x3[:, sl], e)))
    return parts[0] if len(parts) == 1 else jnp.concatenate(parts, axis=1)


def _shifted(p_cur, before, first, mu):
    rolled = pltpu.roll(p_cur, 1, 0)
    prev_row = jnp.where(first, 0.0, before)
    row0 = lax.broadcasted_iota(jnp.int32, p_cur.shape, 0) == 0
    prev = jnp.where(row0, prev_row, rolled)
    return p_cur + (prev - p_cur) * mu, prev


def _rwkv_features(ps, rw, w0, a0, k_k, k_a, wd, wi):
    r, k, v = ps[:, 0:rw], ps[:, rw:2 * rw], ps[:, 2 * rw:3 * rw]
    wl, al = ps[:, 3 * rw:3 * rw + LORA], ps[:, 3 * rw + LORA:3 * rw + 2 * LORA]
    tw = jnp.tanh(wl)
    zw = w0 + _mm(tw.astype(BF16), wd)
    logw = -jnp.exp(-_softplus(-zw) - 0.5)
    alpha = _sig(a0 + _mm(al.astype(BF16), wi))
    kkr = k * k_k
    n2 = _head_sum(kkr * kkr)
    rn = lax.rsqrt(jnp.maximum(n2, 1e-24))
    kk = kkr * rn
    kmod = k * (1.0 + (alpha - 1.0) * k_a)
    return dict(r=r, k=k, v=v, tw=tw, al=al, zw=zw, logw=logw, alpha=alpha, kk=kk, rn=rn, n2=n2, kmod=kmod)


def _rwkv_pre_fwd(p, c, mu, w0, a0, k_k, k_a, wd, wi):
    t = p.shape[0]
    tm = _tile(t, 128, 8)
    rw, sh = c.rw, c.shift

    def body(p_ref, pp_ref, mu_ref, w0_ref, a0_ref, kk_ref, ka_ref, wd_ref, wi_ref,
             r_ref, lw_ref, km_ref, v_ref, a_ref, b_ref):
        ps, _ = _shifted(p_ref[...], pp_ref[7:8, :], pl.program_id(0) == 0, mu_ref[...])
        f = _rwkv_features(ps, rw, w0_ref[...], a0_ref[...], kk_ref[...], ka_ref[...], wd_ref[...], wi_ref[...])
        r_ref[...] = f["r"]
        lw_ref[...] = f["logw"]
        km_ref[...] = f["kmod"]
        v_ref[...] = f["v"]
        a_ref[...] = -f["kk"]
        b_ref[...] = f["kk"] * f["alpha"]

    vec = lambda n: pl.BlockSpec((1, n), lambda i: (0, 0))
    out = pl.BlockSpec((tm, rw), lambda i: (i, 0))
    return pl.pallas_call(
        body, name="rwkv_pre_fwd", grid=(t // tm,),
        in_specs=[pl.BlockSpec((tm, sh), lambda i: (i, 0)),
                  pl.BlockSpec((8, sh), lambda i: (jnp.maximum(i * (tm // 8) - 1, 0), 0)),
                  vec(sh), vec(rw), vec(rw), vec(rw), vec(rw),
                  pl.BlockSpec((LORA, rw), lambda i: (0, 0)), pl.BlockSpec((LORA, rw), lambda i: (0, 0))],
        out_specs=[out] * 6,
        out_shape=[jax.ShapeDtypeStruct((t, rw), F32)] * 6,
        compiler_params=_params("parallel"),
    )(p, p, mu, w0, a0, k_k, k_a, wd, wi)


def _rwkv_pre_bwd(p, c, mu, w0, a0, k_k, k_a, wd, wi, dr, dlw, dkm, dv, da, db, dr2, dkm2, dv2):
    t = p.shape[0]
    tm = _tile(t, 128, 8)
    rw, sh = c.rw, c.shift

    def body(p_ref, pp_ref, mu_ref, w0_ref, a0_ref, kk_ref, ka_ref, wd_ref, wi_ref,
             dr_ref, dlw_ref, dkm_ref, dv_ref, da_ref, db_ref, dr2_ref, dkm2_ref, dv2_ref,
             dps_ref, dzw_ref, dza_ref, tw_ref, al_ref, dw0_ref, da0_ref, dkk_ref, dka_ref):
        @pl.when(pl.program_id(0) == 0)
        def _():
            for ref in (dw0_ref, da0_ref, dkk_ref, dka_ref):
                ref[...] = jnp.zeros_like(ref)

        ps, _ = _shifted(p_ref[...], pp_ref[7:8, :], pl.program_id(0) == 0, mu_ref[...])
        k_k, k_a = kk_ref[...], ka_ref[...]
        f = _rwkv_features(ps, rw, w0_ref[...], a0_ref[...], k_k, k_a, wd_ref[...], wi_ref[...])
        alpha, kk, k = f["alpha"], f["kk"], f["k"]
        dkm = dkm_ref[...] + dkm2_ref[...]
        db = db_ref[...]
        dkk = db * alpha - da_ref[...]
        dalpha = db * kk + dkm * k * k_a
        dk = dkm * (1.0 + (alpha - 1.0) * k_a)
        dka_ref[...] += jnp.sum(dkm * k * (alpha - 1.0), axis=0, keepdims=True)
        dkkr = f["rn"] * jnp.where(f["n2"] > 1e-24, dkk - kk * _head_sum(dkk * kk), dkk)
        dk = dk + dkkr * k_k
        dkk_ref[...] += jnp.sum(dkkr * k, axis=0, keepdims=True)
        dza = dalpha * alpha * (1.0 - alpha)
        da0_ref[...] += jnp.sum(dza, axis=0, keepdims=True)
        dzw = dlw_ref[...] * f["logw"] * _sig(-f["zw"])
        dw0_ref[...] += jnp.sum(dzw, axis=0, keepdims=True)
        dza_b, dzw_b = dza.astype(BF16), dzw.astype(BF16)
        dal = _mm_nt(dza_b, wi_ref[...])
        dwl = _mm_nt(dzw_b, wd_ref[...]) * (1.0 - f["tw"] * f["tw"])
        dps_ref[:, 0:rw] = dr_ref[...] + dr2_ref[...]
        dps_ref[:, rw:2 * rw] = dk
        dps_ref[:, 2 * rw:3 * rw] = dv_ref[...] + dv2_ref[...]
        dps_ref[:, 3 * rw:3 * rw + LORA] = dwl
        dps_ref[:, 3 * rw + LORA:sh] = dal
        dzw_ref[...] = dzw_b
        dza_ref[...] = dza_b
        tw_ref[...] = f["tw"].astype(BF16)
        al_ref[...] = f["al"].astype(BF16)

    vec = lambda n: pl.BlockSpec((1, n), lambda i: (0, 0))
    blk = lambda n: pl.BlockSpec((tm, n), lambda i: (i, 0))
    return pl.pallas_call(
        body, name="rwkv_pre_bwd", grid=(t // tm,),
        in_specs=[blk(sh), pl.BlockSpec((8, sh), lambda i: (jnp.maximum(i * (tm // 8) - 1, 0), 0)),
                  vec(sh), vec(rw), vec(rw), vec(rw), vec(rw),
                  pl.BlockSpec((LORA, rw), lambda i: (0, 0)), pl.BlockSpec((LORA, rw), lambda i: (0, 0))]
                 + [blk(rw)] * 9,
        out_specs=[blk(sh), blk(rw), blk(rw), blk(LORA), blk(LORA), vec(rw), vec(rw), vec(rw), vec(rw)],
        out_shape=[jax.ShapeDtypeStruct((t, sh), F32), jax.ShapeDtypeStruct((t, rw), BF16),
                   jax.ShapeDtypeStruct((t, rw), BF16), jax.ShapeDtypeStruct((t, LORA), BF16),
                   jax.ShapeDtypeStruct((t, LORA), BF16)] + [jax.ShapeDtypeStruct((1, rw), F32)] * 4,
        compiler_params=_params("arbitrary"),
    )(p, p, mu, w0, a0, k_k, k_a, wd, wi, dr, dlw, dkm, dv, da, db, dr2, dkm2, dv2)


def _shift_bwd(dps, p, c, mu, dp):
    t = p.shape[0]
    tm = _tile(t, 256, 8)
    sh = c.shift
    nt = t // tm

    def body(d_ref, dn_ref, p_ref, pp_ref, mu_ref, dp_ref, dmu_ref):
        i = pl.program_id(0)

        @pl.when(i == 0)
        def _():
            dmu_ref[...] = jnp.zeros_like(dmu_ref)

        mu = mu_ref[...]
        d = d_ref[...]
        pc = p_ref[...]
        _, prev = _shifted(pc, pp_ref[7:8, :], i == 0, mu)
        dmu_ref[...] += jnp.sum(d * (prev - pc), axis=0, keepdims=True)
        nxt_row = jnp.where(i == nt - 1, 0.0, dn_ref[0:1, :])
        last = lax.broadcasted_iota(jnp.int32, d.shape, 0) == tm - 1
        nxt = jnp.where(last, nxt_row, pltpu.roll(d, tm - 1, 0))
        dp_ref[...] = (d * (1.0 - mu) + nxt * mu).astype(BF16)

    blk = pl.BlockSpec((tm, sh), lambda i: (i, 0))
    return _pallas_into(
        body, dp, 5, 0, name="shift_bwd", grid=(nt,),
        in_specs=[blk, pl.BlockSpec((8, sh), lambda i: (jnp.minimum((i + 1) * (tm // 8), t // 8 - 1), 0)),
                  blk, pl.BlockSpec((8, sh), lambda i: (jnp.maximum(i * (tm // 8) - 1, 0), 0)),
                  pl.BlockSpec((1, sh), lambda i: (0, 0))],
        out_specs=[blk, pl.BlockSpec((1, sh), lambda i: (0, 0))],
        out_shape=[jax.ShapeDtypeStruct((t, c.wp), BF16), jax.ShapeDtypeStruct((1, sh), F32)],
        compiler_params=_params("arbitrary"),
    )(dps, dps, p, p, mu)


def _tri(n, strict):
    ri = lax.broadcasted_iota(jnp.int32, (n, n), 0)
    ci = lax.broadcasted_iota(jnp.int32, (n, n), 1)
    return (ri > ci) if strict else (ri >= ci)


def _unit_lower_inverse(a):
    n = a.shape[-1]
    ri = lax.broadcasted_iota(jnp.int32, (n, n), 0)
    ci = lax.broadcasted_iota(jnp.int32, (n, n), 1)
    eye = (ri == ci).astype(F32)
    blk = lambda s: (ri // s) == (ci // s)
    ad = jnp.where(blk(16), a, 0.0)
    p = eye + ad
    for _ in range(3):
        ad = _bmm(ad, ad, P_SOLVE)
        p = p + _bmm(p, ad, P_SOLVE)
    s = 16
    while s < n:
        off = jnp.where(blk(2 * s) & ~blk(s), a, 0.0)
        p = p + _bmm(_bmm(p, off, P_SOLVE), p, P_SOLVE)
        s *= 2
    return p


P_SOLVE, P_STATE, P_OUT, P_GRAD, P_DECAY = 1, 1, 1, 1, 2


def _chunk_common(r, lw, k, a, b):
    n = r.shape[1]
    tri_incl = jnp.broadcast_to(_tri(n, False).astype(BF16), (r.shape[0], n, n))
    cum = _bmm_01(tri_incl, lw)
    e_pos, e_neg, e_exc = jnp.exp(cum), jnp.exp(-cum), jnp.exp(cum - lw)
    last = lax.broadcasted_iota(jnp.int32, (n, r.shape[2]), 0) == n - 1
    g_last = jnp.exp(jnp.sum(jnp.where(last, cum, 0.0), axis=1, keepdims=True))
    return g_last, r * e_pos, a * e_exc, b * e_neg, k * e_neg, e_pos, e_neg, e_exc


def _chunk_solve(rt, at, bt, kt, v, g0):
    strict, incl = _tri(rt.shape[1], True), _tri(rt.shape[1], False)
    a_ab = jnp.where(strict, _bmm_nt(at, bt, P_SOLVE), 0.0)
    a_ak = jnp.where(strict, _bmm_nt(at, kt, P_SOLVE), 0.0)
    a_rb = jnp.where(incl, _bmm_nt(rt, bt, P_OUT), 0.0)
    a_rk = jnp.where(incl, _bmm_nt(rt, kt, P_OUT), 0.0)
    tinv = _unit_lower_inverse(a_ab)
    u = _bmm(tinv, _bmm(at, g0, P_SOLVE) + _bmm(a_ak, v, P_SOLVE), P_SOLVE)
    return a_ab, a_ak, a_rb, a_rk, tinv, u


def _diag_col(row, n):
    ri = lax.broadcasted_iota(jnp.int32, (n, n), 0)
    ci = lax.broadcasted_iota(jnp.int32, (n, n), 1)
    return jnp.sum(jnp.where(ri == ci, row, 0.0), axis=2, keepdims=True)


def _diag_row(col, n):
    ri = lax.broadcasted_iota(jnp.int32, (n, n), 0)
    ci = lax.broadcasted_iota(jnp.int32, (n, n), 1)
    return jnp.sum(jnp.where(ri == ci, col, 0.0), axis=1, keepdims=True)


def _rwkv_scan_fwd(r, lw, k, v, a, b, hb):
    h, t, n = r.shape
    nc = t // CHUNK

    def body(r_ref, lw_ref, k_ref, v_ref, a_ref, b_ref, y_ref, st_ref, g_sc):
        @pl.when(pl.program_id(1) == 0)
        def _():
            g_sc[...] = jnp.zeros_like(g_sc)

        g0 = g_sc[...]
        st_ref[0] = g0
        vv = v_ref[...]
        g_last, rt, at, bt, kt, _, _, _ = _chunk_common(r_ref[...], lw_ref[...], k_ref[...], a_ref[...], b_ref[...])
        _, _, a_rb, a_rk, _, u = _chunk_solve(rt, at, bt, kt, vv, g0)
        y_ref[...] = _bmm(rt, g0, P_OUT) + _bmm(a_rb, u, P_OUT) + _bmm(a_rk, vv, P_OUT)
        z = g0 + _bmm_tn(bt, u, P_STATE) + _bmm_tn(kt, vv, P_STATE)
        g_sc[...] = _diag_col(g_last, n) * z

    blk = pl.BlockSpec((hb, CHUNK, n), lambda i, j: (i, j, 0))
    return pl.pallas_call(
        body, name="rwkv_scan_fwd", grid=(h // hb, nc),
        in_specs=[blk] * 6,
        out_specs=[blk, pl.BlockSpec((1, hb, n, n), lambda i, j: (j, i, 0, 0))],
        out_shape=[jax.ShapeDtypeStruct((h, t, n), F32), jax.ShapeDtypeStruct((nc, h, n, n), F32)],
        scratch_shapes=[pltpu.VMEM((hb, n, n), F32)],
        compiler_params=_params("parallel", "arbitrary"),
    )(r, lw, k, v, a, b)


def _rwkv_scan_bwd(r, lw, k, v, a, b, states, dy, hb):
    h, t, n = r.shape
    nc = t // CHUNK

    def body(r_ref, lw_ref, k_ref, v_ref, a_ref, b_ref, st_ref, dy_ref,
             dr_ref, dlw_ref, dk_ref, dv_ref, da_ref, db_ref, dg_sc):
        @pl.when(pl.program_id(1) == 0)
        def _():
            dg_sc[...] = jnp.zeros_like(dg_sc)

        g0 = st_ref[0]
        vv, dyv, dh = v_ref[...], dy_ref[...], dg_sc[...]
        lwv = lw_ref[...]
        g_last, rt, at, bt, kt, e_pos, e_neg, e_exc = _chunk_common(r_ref[...], lwv, k_ref[...], a_ref[...], b_ref[...])
        a_ab, a_ak, a_rb, a_rk, tinv, u = _chunk_solve(rt, at, bt, kt, vv, g0)
        strict, incl = _tri(CHUNK, True), _tri(CHUNK, False)
        gcol = _diag_col(g_last, n)
        z = g0 + _bmm_tn(bt, u, P_STATE) + _bmm_tn(kt, vv, P_STATE)
        dz = gcol * dh
        dc_last = _diag_row(jnp.sum(dh * gcol * z, axis=2, keepdims=True), n)
        g = P_GRAD
        du = _bmm_tn(a_rb, dyv, g) + _bmm(bt, dz, g)
        dx = _bmm_tn(tinv, du, P_SOLVE)
        dv_ref[...] = _bmm_tn(a_rk, dyv, g) + _bmm(kt, dz, g) + _bmm_tn(a_ak, dx, g)
        da_ab = jnp.where(strict, _bmm_nt(dx, u, g), 0.0)
        da_ak = jnp.where(strict, _bmm_nt(dx, vv, g), 0.0)
        da_rb = jnp.where(incl, _bmm_nt(dyv, u, g), 0.0)
        da_rk = jnp.where(incl, _bmm_nt(dyv, vv, g), 0.0)
        g = P_DECAY
        d_at = _bmm(da_ab, bt, g) + _bmm(da_ak, kt, g) + _bmm_nt(dx, g0, g)
        d_rt = _bmm(da_rb, bt, g) + _bmm(da_rk, kt, g) + _bmm_nt(dyv, g0, g)
        d_bt = _bmm_tn(da_ab, at, g) + _bmm_tn(da_rb, rt, g) + _bmm_nt(u, dz, g)
        d_kt = _bmm_tn(da_ak, at, g) + _bmm_tn(da_rk, rt, g) + _bmm_nt(vv, dz, g)
        dg_sc[...] = dz + _bmm_tn(rt, dyv, P_STATE) + _bmm_tn(at, dx, P_STATE)
        dr_ref[...] = d_rt * e_pos
        da_ref[...] = d_at * e_exc
        db_ref[...] = d_bt * e_neg
        dk_ref[...] = d_kt * e_neg
        last = lax.broadcasted_iota(jnp.int32, (CHUNK, n), 0) == CHUNK - 1
        dc = d_rt * rt - d_bt * bt - d_kt * kt + jnp.where(last, dc_last, 0.0)
        dce = d_at * at
        ri = lax.broadcasted_iota(jnp.int32, (CHUNK, CHUNK), 0)
        ci = lax.broadcasted_iota(jnp.int32, (CHUNK, CHUNK), 1)
        up_incl = jnp.broadcast_to((ri <= ci).astype(BF16), (hb, CHUNK, CHUNK))
        dlw_ref[...] = _bmm_01(up_incl, dc + dce) - dce

    rev = lambda i, j: (i, nc - 1 - j, 0)
    blk = pl.BlockSpec((hb, CHUNK, n), rev)
    return pl.pallas_call(
        body, name="rwkv_scan_bwd", grid=(h // hb, nc),
        in_specs=[blk] * 6 + [pl.BlockSpec((1, hb, n, n), lambda i, j: (nc - 1 - j, i, 0, 0)), blk],
        out_specs=[blk] * 6,
        out_shape=[jax.ShapeDtypeStruct((h, t, n), F32)] * 6,
        scratch_shapes=[pltpu.VMEM((hb, n, n), F32)],
        compiler_params=_params("parallel", "arbitrary"),
    )(r, lw, k, v, a, b, states, dy)


def _silu_grad(g):
    s = _sig(g)
    return s * (1.0 + g * (1.0 - s))


def _group_norm(ys):
    yc = ys - _head_sum(ys) * (1.0 / HEAD_DIM)
    rstd = lax.rsqrt(_head_sum(yc * yc) * (1.0 / HEAD_DIM) + GN_EPS)
    return yc * rstd, rstd


def _rwkv_post_fwd(ys, r, km, v, p, c, ln_w, ln_b, r_k):
    t = ys.shape[0]
    tm = _tile(t, 1024, 8)
    goff = c.o_grw // WIDE

    def body(ys_ref, r_ref, km_ref, v_ref, g_ref, lw_ref, lb_ref, rk_ref, o_ref):
        yn, _ = _group_norm(ys_ref[...])
        s = _head_sum(r_ref[...] * km_ref[...] * rk_ref[...])
        g = g_ref[...]
        o_ref[...] = ((yn * lw_ref[...] + lb_ref[...] + s * v_ref[...]) * g * _sig(g)).astype(BF16)

    blk = pl.BlockSpec((tm, WIDE), lambda i, j: (i, j))
    vec = pl.BlockSpec((1, WIDE), lambda i, j: (0, j))
    return pl.pallas_call(
        body, name="rwkv_post_fwd", grid=(t // tm, c.rw // WIDE),
        in_specs=[blk] * 4 + [pl.BlockSpec((tm, WIDE), lambda i, j: (i, goff + j)), vec, vec, vec],
        out_specs=blk, out_shape=jax.ShapeDtypeStruct((t, c.d), BF16),
        compiler_params=_params("parallel", "parallel"),
    )(ys, r, km, v, p, ln_w, ln_b, r_k)


def _rwkv_post_bwd(dyc, ys, r, km, v, p, c, ln_w, ln_b, r_k):
    t = ys.shape[0]
    tm = _tile(t, 1024, 8)
    goff = c.o_grw // WIDE

    def body(dy_ref, ys_ref, r_ref, km_ref, v_ref, g_ref, lw_ref, lb_ref, rk_ref,
             dys_ref, dr_ref, dkm_ref, dv_ref, dg_ref, dlw_ref, dlb_ref, drk_ref):
        @pl.when(pl.program_id(1) == 0)
        def _():
            for ref in (dlw_ref, dlb_ref, drk_ref):
                ref[...] = jnp.zeros_like(ref)

        yn, rstd = _group_norm(ys_ref[...])
        rv, kmv, vv, rk, g = r_ref[...], km_ref[...], v_ref[...], rk_ref[...], g_ref[...]
        s = _head_sum(rv * kmv * rk)
        y = yn * lw_ref[...] + lb_ref[...] + s * vv
        dyc = dy_ref[...]
        dg_ref[...] = (dyc * y * _silu_grad(g)).astype(BF16)
        dy = dyc * g * _sig(g)
        dlb_ref[...] += jnp.sum(dy, axis=0, keepdims=True)
        dlw_ref[...] += jnp.sum(dy * yn, axis=0, keepdims=True)
        dyn = dy * lw_ref[...]
        inv = 1.0 / HEAD_DIM
        dys_ref[...] = rstd * (dyn - _head_sum(dyn) * inv - yn * _head_sum(dyn * yn) * inv)
        ds = _head_sum(dy * vv)
        dv_ref[...] = dy * s
        dr_ref[...] = ds * kmv * rk
        dkm_ref[...] = ds * rv * rk
        drk_ref[...] += jnp.sum(ds * rv * kmv, axis=0, keepdims=True)

    blk = pl.BlockSpec((tm, WIDE), lambda j, i: (i, j))
    vec = pl.BlockSpec((1, WIDE), lambda j, i: (0, j))
    f = jax.ShapeDtypeStruct((t, c.rw), F32)
    s1 = jax.ShapeDtypeStruct((1, c.rw), F32)
    gate = pl.BlockSpec((tm, WIDE), lambda j, i: (i, goff + j))
    return pl.pallas_call(
        body, name="rwkv_post_bwd", grid=(c.rw // WIDE, t // tm),
        in_specs=[blk] * 5 + [gate, vec, vec, vec],
        out_specs=[blk] * 4 + [gate] + [vec] * 3,
        out_shape=[f, f, f, f, jax.ShapeDtypeStruct((t, c.wp), BF16), s1, s1, s1],
        compiler_params=_params("parallel", "arbitrary"),
    )(dyc, ys, r, km, v, p, ln_w, ln_b, r_k)


def _gate_fwd(y, p, goff, name, ycat, yoff):
    t, w = y.shape
    tm = _tile(t, 1024, 8)
    gb, ob = goff // WIDE, yoff // WIDE

    def body(y_ref, g_ref, o_ref):
        g = g_ref[...]
        o_ref[...] = (y_ref[...] * g * _sig(g)).astype(BF16)

    blk = pl.BlockSpec((tm, WIDE), lambda i, j: (i, j))
    return _pallas_into(
        body, ycat, 2, 0, name=name, grid=(t // tm, w // WIDE),
        in_specs=[blk, pl.BlockSpec((tm, WIDE), lambda i, j: (i, gb + j))],
        out_specs=pl.BlockSpec((tm, WIDE), lambda i, j: (i, ob + j)),
        out_shape=jax.ShapeDtypeStruct(ycat.shape, BF16),
        compiler_params=_params("parallel", "parallel"),
    )(y, p)


def _gate_bwd(dyc, yoff, y, p, goff, name, dp):
    t, w = y.shape
    tm = _tile(t, 1024, 8)
    gb, yb = goff // WIDE, yoff // WIDE

    def body(d_ref, y_ref, g_ref, dy_ref, dg_ref):
        g, d = g_ref[...], d_ref[...]
        dy_ref[...] = d * g * _sig(g)
        dg_ref[...] = (d * y_ref[...] * _silu_grad(g)).astype(BF16)

    blk = pl.BlockSpec((tm, WIDE), lambda i, j: (i, j))
    gate = pl.BlockSpec((tm, WIDE), lambda i, j: (i, gb + j))
    return _pallas_into(
        body, dp, 3, 1, name=name, grid=(t // tm, w // WIDE),
        in_specs=[pl.BlockSpec((tm, WIDE), lambda i, j: (i, yb + j)), blk, gate],
        out_specs=[blk, gate],
        out_shape=[jax.ShapeDtypeStruct((t, w), F32), jax.ShapeDtypeStruct(dp.shape, BF16)],
        compiler_params=_params("parallel", "parallel"),
    )(dyc, y, p)


NEG = -1e30


def _fox_logit_bwd(dcum, p, c, b_f):
    t = p.shape[0]
    tm = _tile(t, 512, 8)
    fb = c.o_fl // LANES
    nt = t // tm

    def body(d_ref, f_ref, b_ref, o_ref, db_ref, carry):
        @pl.when(pl.program_id(0) == 0)
        def _():
            carry[...] = jnp.zeros_like(carry)
            db_ref[...] = jnp.zeros_like(db_ref)

        d = d_ref[0] + d_ref[1]
        dlogf = _mm(_tri(tm, False).astype(F32).T, d, HI) + carry[...]
        carry[...] += jnp.sum(d, axis=0, keepdims=True)
        df = dlogf * _sig(-(f_ref[...] + b_ref[...]))
        o_ref[...] = df.astype(BF16)
        db_ref[...] += jnp.sum(df, axis=0, keepdims=True)

    return pl.pallas_call(
        body, name="fox_logit_bwd", grid=(nt,),
        in_specs=[pl.BlockSpec((2, tm, LANES), lambda i: (0, nt - 1 - i, 0)),
                  pl.BlockSpec((tm, LANES), lambda i: (nt - 1 - i, fb)),
                  pl.BlockSpec((1, LANES), lambda i: (0, 0))],
        out_specs=[pl.BlockSpec((tm, LANES), lambda i: (nt - 1 - i, 0)), pl.BlockSpec((1, LANES), lambda i: (0, 0))],
        out_shape=[jax.ShapeDtypeStruct((t, LANES), BF16), jax.ShapeDtypeStruct((1, LANES), F32)],
        scratch_shapes=[pltpu.VMEM((1, LANES), F32)],
        compiler_params=_params("arbitrary"),
    )(dcum, p, b_f)


FOX_PAIRS = 2
FOX_HEADS_STEP = 2 * FOX_PAIRS


def _lane_half(shape, upper):
    li = lax.broadcasted_iota(jnp.int32, shape, len(shape) - 1)
    return (li >= HEAD_DIM) if upper else (li < HEAD_DIM)


def _col(block, j):
    li = lax.broadcasted_iota(jnp.int32, block.shape, 1)
    return jnp.sum(jnp.where(li == j, block, 0.0), axis=1, keepdims=True)


def _from_cols(cols):
    li = lax.broadcasted_iota(jnp.int32, (cols[0].shape[0], len(cols)), 1)
    out = jnp.zeros(li.shape, F32)
    for j, cj in enumerate(cols):
        out = jnp.where(li == j, cj, out)
    return out


def _from_rows(rows):
    si = lax.broadcasted_iota(jnp.int32, (len(rows), rows[0].shape[1]), 0)
    out = jnp.zeros(si.shape, F32)
    for j, rj in enumerate(rows):
        out = jnp.where(si == j, rj, out)
    return out


def _causal(tq, tk):
    return lax.broadcasted_iota(jnp.int32, (tq, tk), 1) <= lax.broadcasted_iota(jnp.int32, (tq, tk), 0)


def _fox_prep_t(p, c, b_f):
    t = p.shape[0]
    tm = _tile(t, 512, LANES)
    fb = c.o_fl // LANES

    def body(f_ref, b_ref, o_ref, carry):
        @pl.when(pl.program_id(0) == 0)
        def _():
            carry[...] = jnp.zeros_like(carry)

        logf = -_softplus(-(f_ref[...] + b_ref[...]))
        cum = _mm(_tri(tm, False).astype(F32), logf, HI) + carry[...]
        o_ref[...] = cum.T
        carry[...] += jnp.sum(logf, axis=0, keepdims=True)

    return pl.pallas_call(
        body, name="fox_prep", grid=(t // tm,),
        in_specs=[pl.BlockSpec((tm, LANES), lambda i: (i, fb)), pl.BlockSpec((1, LANES), lambda i: (0, 0))],
        out_specs=pl.BlockSpec((LANES, tm), lambda i: (0, i)),
        out_shape=jax.ShapeDtypeStruct((LANES, t), F32),
        scratch_shapes=[pltpu.VMEM((1, LANES), F32)],
        compiler_params=_params("arbitrary"),
    )(p, b_f)


def _fox2_fwd(p, c, cum_t, tb, ycat):
    t = p.shape[0]
    tq = tk = _tile(t, tb, LANES)
    nq = t // tq
    pw, nh = FOX_PAIRS * LANES, FOX_HEADS_STEP
    qb, kb, vb, gb = (o // pw for o in (c.o_fq, c.o_fk, c.o_fv, c.o_gfox))
    scale = HEAD_DIM ** -0.5

    def body(q_ref, k_ref, v_ref, g_ref, ck_ref, o_ref, y_ref, lse_ref, m_sc, l_sc, acc_sc):
        g, qi, ki = pl.program_id(0), pl.program_id(1), pl.program_id(2)

        @pl.when(ki == 0)
        def _():
            m_sc[...] = jnp.full_like(m_sc, NEG)
            l_sc[...] = jnp.zeros_like(l_sc)
            acc_sc[...] = jnp.zeros_like(acc_sc)

        def step(diag):
            ms, ls = [m_sc[h] for h in range(nh)], [l_sc[h] for h in range(nh)]
            accs = [acc_sc[:, pi * LANES:(pi + 1) * LANES] for pi in range(FOX_PAIRS)]
            for pi in range(FOX_PAIRS):
                lanes = slice(pi * LANES, (pi + 1) * LANES)
                q2 = (q_ref[:, lanes] * scale).astype(BF16)
                k2, v2 = k_ref[:, lanes].astype(BF16), v_ref[:, lanes].astype(BF16)
                new_acc = accs[pi]
                for hh in range(2):
                    hi = 2 * pi + hh
                    mk = _lane_half((tq, LANES), hh == 1)
                    s = _mm_nt(jnp.where(mk, q2, jnp.zeros_like(q2)), k2) - ck_ref[pl.ds(g * nh + hi, 1), :]
                    if diag:
                        s = jnp.where(_causal(tq, tk), s, NEG)
                    m_new = jnp.maximum(ms[hi], jnp.max(s, axis=1, keepdims=True))
                    a = jnp.exp(ms[hi] - m_new)
                    e = jnp.exp(s - jnp.concatenate([m_new] * (tk // LANES), axis=1))
                    ls[hi] = a * ls[hi] + jnp.sum(e, axis=1, keepdims=True)
                    ms[hi] = m_new
                    new_acc = jnp.where(mk, a * accs[pi] + _mm(e.astype(BF16), v2), new_acc)
                accs[pi] = new_acc
            for h in range(nh):
                m_sc[h] = ms[h]
                l_sc[h] = ls[h]
            for pi in range(FOX_PAIRS):
                acc_sc[:, pi * LANES:(pi + 1) * LANES] = accs[pi]

        @pl.when(ki < qi)
        def _():
            step(False)

        @pl.when(ki == qi)
        def _():
            step(True)
            li = lax.broadcasted_iota(jnp.int32, (tq, LANES), 1)
            lse = jnp.zeros((tq, LANES), F32)
            for pi in range(FOX_PAIRS):
                lanes = slice(pi * LANES, (pi + 1) * LANES)
                inv = jnp.where(_lane_half((tq, LANES), False), 1.0 / l_sc[2 * pi], 1.0 / l_sc[2 * pi + 1])
                o = acc_sc[:, lanes] * inv
                gate = g_ref[:, lanes]
                o_ref[:, lanes] = o
                y_ref[:, lanes] = (o * gate * _sig(gate)).astype(BF16)
            for h in range(nh):
                lse = jnp.where(li == h, m_sc[h] + jnp.log(l_sc[h]), lse)
            lse_ref[0] = lse

    row = lambda off: pl.BlockSpec((tq, pw), lambda g, i, j: (i, off + g))
    key = lambda off: pl.BlockSpec((tk, pw), lambda g, i, j: (jnp.minimum(i, j), off + g))
    out = pl.BlockSpec((tq, pw), lambda g, i, j: (i, g))
    return _pallas_into(
        body, ycat, 5, 1, name="fox_fwd", grid=(c.rw // pw, nq, nq),
        in_specs=[row(qb), key(kb), key(vb), row(gb),
                  pl.BlockSpec((LANES, tk), lambda g, i, j: (0, jnp.minimum(i, j)))],
        out_specs=[out, row(c.rw // pw), pl.BlockSpec((1, tq, LANES), lambda g, i, j: (g, i, 0))],
        out_shape=[jax.ShapeDtypeStruct((t, c.rw), F32), jax.ShapeDtypeStruct(ycat.shape, BF16),
                   jax.ShapeDtypeStruct((c.rw // pw, t, LANES), F32)],
        scratch_shapes=[pltpu.VMEM((nh, tq, LANES), F32), pltpu.VMEM((nh, tq, LANES), F32),
                        pltpu.VMEM((tq, pw), F32)],
        compiler_params=_params("parallel", "parallel", "arbitrary"),
    )(p, p, p, p, cum_t)


def _fox2_grads(q2, k2, v2, do2, o2, lse_h, ck, mk, diag, tq, tk):
    zero = jnp.zeros_like(q2)
    s = _mm_nt(jnp.where(mk, q2, zero), k2) - ck
    if diag:
        s = jnp.where(_causal(tq, tk), s, NEG)
    wide = lambda col: jnp.concatenate([jnp.broadcast_to(col, (tq, LANES))] * (tk // LANES), axis=1)
    pm = jnp.exp(s - wide(lse_h))
    delta = jnp.sum(jnp.where(mk, do2 * o2, 0.0), axis=1, keepdims=True)
    dob = do2.astype(BF16)
    dp = _mm_nt(jnp.where(mk, dob, zero), v2)
    return pm, pm * (dp - wide(delta)), dob


def _fox2_bwd(p, c, cum_t, lse, o, do, tb, dp):
    t = p.shape[0]
    tq = tk = _tile(t, tb, LANES)
    nq = t // tq
    pw, nh = FOX_PAIRS * LANES, FOX_HEADS_STEP
    ng = c.rw // pw
    qb, kb, vb = (o_ // pw for o_ in (c.o_fq, c.o_fk, c.o_fv))
    scale = HEAD_DIM ** -0.5

    def body(q_ref, k_ref, v_ref, ck_ref, lse_ref, o_ref, do_ref,
             dk_ref, dv_ref, dck_ref, dqp_ref, dcqp_ref, dk_sc, dv_sc, dc_sc):
        g, ki, qi = pl.program_id(0), pl.program_id(1), pl.program_id(2)

        @pl.when(qi == 0)
        def _():
            dk_sc[...] = jnp.zeros_like(dk_sc)
            dv_sc[...] = jnp.zeros_like(dv_sc)
            dc_sc[...] = jnp.zeros_like(dc_sc)

        def step(diag):
            lse_blk = lse_ref[0]
            dcs = [dc_sc[h] for h in range(nh)]
            dks = [dk_sc[:, pi * LANES:(pi + 1) * LANES] for pi in range(FOX_PAIRS)]
            dvs = [dv_sc[:, pi * LANES:(pi + 1) * LANES] for pi in range(FOX_PAIRS)]
            rows = []
            for pi in range(FOX_PAIRS):
                lanes = slice(pi * LANES, (pi + 1) * LANES)
                q2 = (q_ref[:, lanes] * scale).astype(BF16)
                k2, v2 = k_ref[:, lanes].astype(BF16), v_ref[:, lanes].astype(BF16)
                do2, o2 = do_ref[:, lanes], o_ref[:, lanes]
                new_dk, new_dv, dq2 = dks[pi], dvs[pi], None
                for hh in range(2):
                    hi = 2 * pi + hh
                    mk = _lane_half((tk, LANES), hh == 1)
                    pm, ds, dob = _fox2_grads(q2, k2, v2, do2, o2, _col(lse_blk, hi),
                                              ck_ref[pl.ds(g * nh + hi, 1), :], mk, diag, tq, tk)
                    dsb = ds.astype(BF16)
                    dcs[hi] = dcs[hi] - jnp.sum(ds, axis=0, keepdims=True)
                    rows.append(jnp.sum(ds, axis=1, keepdims=True))
                    new_dv = jnp.where(mk, dvs[pi] + _mm_tn(pm.astype(BF16), dob), new_dv)
                    new_dk = jnp.where(mk, dks[pi] + _mm_tn(dsb, q2), new_dk)
                    part = _mm(dsb, k2)
                    dq2 = part if hh == 0 else jnp.where(mk, part, dq2)
                dks[pi], dvs[pi] = new_dk, new_dv
                dqp_ref[0, :, lanes] = dq2 * scale
            dcqp_ref[0, 0] = _from_cols(rows)
            for h in range(nh):
                dc_sc[h] = dcs[h]
            for pi in range(FOX_PAIRS):
                dk_sc[:, pi * LANES:(pi + 1) * LANES] = dks[pi]
                dv_sc[:, pi * LANES:(pi + 1) * LANES] = dvs[pi]

        @pl.when(qi > ki)
        def _():
            step(False)

        @pl.when(qi == ki)
        def _():
            step(True)

        @pl.when(qi == nq - 1)
        def _():
            dk_ref[...] = dk_sc[...].astype(BF16)
            dv_ref[...] = dv_sc[...].astype(BF16)
            dck_ref[0] = _from_rows([dc_sc[h] for h in range(nh)])

    row = lambda off: pl.BlockSpec((tq, pw), lambda g, j, i: (jnp.maximum(i, j), off + g))
    key = lambda off: pl.BlockSpec((tk, pw), lambda g, j, i: (j, off + g))
    return _pallas_into(
        body, dp, 7, 0, name="fox_bwd", grid=(ng, nq, nq),
        in_specs=[row(qb), key(kb), key(vb), pl.BlockSpec((LANES, tk), lambda g, j, i: (0, j)),
                  pl.BlockSpec((1, tq, LANES), lambda g, j, i: (g, jnp.maximum(i, j), 0)), row(0), row(0)],
        out_specs=[key(kb), key(0), pl.BlockSpec((1, nh, tk), lambda g, j, i: (g, 0, j)),
                   pl.BlockSpec((1, tq, pw), lambda g, j, i: (j, jnp.maximum(i, j), g)),
                   pl.BlockSpec((1, 1, tq, nh), lambda g, j, i: (j, g, jnp.maximum(i, j), 0))],
        out_shape=[jax.ShapeDtypeStruct(dp.shape, BF16), jax.ShapeDtypeStruct((t, c.rw), BF16),
                   jax.ShapeDtypeStruct((ng, nh, t), F32), jax.ShapeDtypeStruct((nq, t, c.rw), F32),
                   jax.ShapeDtypeStruct((nq, ng, t, nh), F32)],
        scratch_shapes=[pltpu.VMEM((tk, pw), F32), pltpu.VMEM((tk, pw), F32), pltpu.VMEM((nh, 1, tk), F32)],
        compiler_params=_params("parallel", "parallel", "arbitrary"),
    )(p, p, p, cum_t, lse, o, do)


def _fox2_dq_sum(dq_part, dcq_part, tq):
    nk, t, rw = dq_part.shape
    ng, nh = dcq_part.shape[1], dcq_part.shape[3]

    def body(p_ref, c_ref, dq_ref, dcq_ref, acc, cacc):
        i, j = pl.program_id(0), pl.program_id(1)

        @pl.when(j == 0)
        def _():
            acc[...] = p_ref[0]
            cacc[...] = c_ref[0]

        @pl.when((j > 0) & (j <= i))
        def _():
            acc[...] += p_ref[0]
            cacc[...] += c_ref[0]

        @pl.when(j == nk - 1)
        def _():
            dq_ref[...] = acc[...].astype(BF16)
            dcq_ref[...] = cacc[...]

    return pl.pallas_call(
        body, name="fox_dq_sum", grid=(t // tq, nk),
        in_specs=[pl.BlockSpec((1, tq, rw), lambda i, j: (jnp.minimum(i, j), i, 0)),
                  pl.BlockSpec((1, ng, tq, nh), lambda i, j: (jnp.minimum(i, j), 0, i, 0))],
        out_specs=[pl.BlockSpec((tq, rw), lambda i, j: (i, 0)), pl.BlockSpec((ng, tq, nh), lambda i, j: (0, i, 0))],
        out_shape=[jax.ShapeDtypeStruct((t, rw), BF16), jax.ShapeDtypeStruct((ng, t, nh), F32)],
        scratch_shapes=[pltpu.VMEM((tq, rw), F32), pltpu.VMEM((ng, tq, nh), F32)],
        compiler_params=_params("parallel", "arbitrary"),
    )(dq_part, dcq_part)


def _mem_probs(q, mk, scale):
    s = _mm_nt(q.astype(BF16), mk.astype(BF16)) * scale
    e = jnp.exp(s - jnp.max(s, axis=1, keepdims=True))
    return e / jnp.sum(e, axis=1, keepdims=True)


def _mem_attn_fwd(p, c, mkv):
    t = p.shape[0]
    tm = _tile(t, 512, 8)
    dh = c.mhd
    qb = c.o_mq // dh
    scale = dh ** -0.5

    def body(q_ref, mk_ref, mv_ref, o_ref):
        pm = _mem_probs(q_ref[...], mk_ref[...], scale)
        o_ref[...] = _mm(pm.astype(BF16), mv_ref[...].astype(BF16))

    m = mkv.shape[0]
    return pl.pallas_call(
        body, name="mem_attn_fwd", grid=(t // tm, MEM_HEADS),
        in_specs=[pl.BlockSpec((tm, dh), lambda i, j: (i, qb + j)),
                  pl.BlockSpec((m, dh), lambda i, j: (0, j)),
                  pl.BlockSpec((m, dh), lambda i, j: (0, MEM_HEADS + j))],
        out_specs=pl.BlockSpec((tm, dh), lambda i, j: (i, j)),
        out_shape=jax.ShapeDtypeStruct((t, c.mw), F32),
        compiler_params=_params("parallel", "parallel"),
    )(p, mkv, mkv)


def _mem_attn_bwd(p, c, mkv, do):
    t = p.shape[0]
    tm = _tile(t, 512, 8)
    dh = c.mhd
    qb = c.o_mq // dh
    scale = dh ** -0.5
    m = mkv.shape[0]

    def body(q_ref, mk_ref, mv_ref, do_ref, dq_ref, dmk_ref, dmv_ref):
        @pl.when(pl.program_id(1) == 0)
        def _():
            dmk_ref[...] = jnp.zeros_like(dmk_ref)
            dmv_ref[...] = jnp.zeros_like(dmv_ref)

        qv = q_ref[...].astype(BF16)
        pm = _mem_probs(qv, mk_ref[...], scale)
        dob = do_ref[...].astype(BF16)
        dmv_ref[...] += _mm_tn(pm.astype(BF16), dob)
        dp = _mm_nt(dob, mv_ref[...].astype(BF16))
        ds = (pm * (dp - jnp.sum(pm * dp, axis=1, keepdims=True)) * scale).astype(BF16)
        dq_ref[...] = _mm(ds, mk_ref[...].astype(BF16)).astype(BF16)
        dmk_ref[...] += _mm_tn(ds, qv)

    kvb = lambda off: pl.BlockSpec((m, dh), lambda j, i: (0, off + j))
    return pl.pallas_call(
        body, name="mem_attn_bwd", grid=(MEM_HEADS, t // tm),
        in_specs=[pl.BlockSpec((tm, dh), lambda j, i: (i, qb + j)), kvb(0), kvb(MEM_HEADS),
                  pl.BlockSpec((tm, dh), lambda j, i: (i, j))],
        out_specs=[pl.BlockSpec((tm, dh), lambda j, i: (i, j)), kvb(0), kvb(0)],
        out_shape=[jax.ShapeDtypeStruct((t, c.mw), BF16), jax.ShapeDtypeStruct((m, c.mw), F32),
                   jax.ShapeDtypeStruct((m, c.mw), F32)],
        compiler_params=_params("parallel", "arbitrary"),
    )(p, mkv, mkv, do)


def _adamw(w, g, m, v, name):
    rows, cols = w.shape
    bc1 = 1.0 - ADAM_B1 ** ADAM_STEP
    bc2 = 1.0 - ADAM_B2 ** ADAM_STEP
    if rows % 8 and rows > 8:
        blk = pl.BlockSpec((rows, LANES), lambda i: (0, i))
        g_blk = pl.BlockSpec((g.shape[0], LANES), lambda i: (0, i))
        grid = (cols // LANES,)
    else:
        tm = _tile(rows, max(8, (1 << 18) // cols // 8 * 8), 8)
        blk = pl.BlockSpec((tm, cols), lambda i: (i, 0))
        g_blk = pl.BlockSpec((tm, g.shape[1]), lambda i: (i, 0))
        grid = (rows // tm,)
    brows, bcols = blk.block_shape

    def body(w_ref, g_ref, m_ref, v_ref, go_ref, d_ref, mo_ref, vo_ref):
        gv = g_ref[0:brows, 0:bcols]
        mn = ADAM_B1 * m_ref[...] + (1.0 - ADAM_B1) * gv
        vn = ADAM_B2 * v_ref[...] + (1.0 - ADAM_B2) * (gv * gv)
        go_ref[...] = gv
        mo_ref[...] = mn
        vo_ref[...] = vn
        d_ref[...] = -ADAM_LR * ((mn / bc1) / (jnp.sqrt(vn / bc2) + ADAM_EPS) + ADAM_WD * w_ref[...])

    shp = jax.ShapeDtypeStruct((rows, cols), F32)
    return pl.pallas_call(
        body, name=name, grid=grid,
        in_specs=[blk, g_blk, blk, blk],
        out_specs=[blk] * 4, out_shape=[shp] * 4,
        compiler_params=_params("parallel"),
    )(w, g, m, v)


SCAN_HEADS = 12
FOX_BLOCK = 512


def _local_step(c, x, mem, tgt, w, riders=None):
    t = x.shape[0]
    rw = c.rw
    riders = riders or {}
    carried = {}
    hd = lambda z: z.reshape(t, c.h, HEAD_DIM).transpose(1, 0, 2)
    uh = lambda z: z.transpose(1, 0, 2).reshape(t, rw)
    vecs = (w["mu"], w["w0"], w["a0"], w["k_k"], w["k_a"], w["wd"], w["wi"])

    h, rinv = _rms_fwd(x, w["g_pre"], "rms_pre")
    if "in_proj" in riders:
        groups, finish = riders["in_proj"]
        p, late = _matmul(h, w["wp"], name="in_proj", tk=4096, attach=groups)
        w = dict(w, **finish(late))
    else:
        p = _matmul(h, w["wp"], name="in_proj", tk=4096)
    r, lw, km, v, a, b = _rwkv_pre_fwd(p, c, *vecs)
    scan_in = tuple(hd(z) for z in (r, lw, km, v, a, b))
    hb = max(n for n in range(1, SCAN_HEADS + 1) if c.h % n == 0)
    ysh, states = _rwkv_scan_fwd(*scan_in, hb)
    ys = uh(ysh)
    ycat = _rwkv_post_fwd(ys, r, km, v, p, c, w["ln_w"], w["ln_b"], w["r_k"])

    cum_t = _fox_prep_t(p, c, w["b_f"])
    yfox, ycat, lse = _fox2_fwd(p, c, cum_t, FOX_BLOCK, ycat)

    memn, rinv_m = _rms_fwd(mem, w["g_mem"], "rms_mem")
    mkv = _matmul(memn, w["w_mem_kv"], name="mem_kv")
    ymem = _mem_attn_fwd(p, c, mkv)
    ycat = _gate_fwd(ymem, p, c.o_gmq, "gate_mem", ycat, 2 * rw)
    yo =_matmul(ycat, w["w_out"], name="out_proj", tn=512, tk=4096)
    loss, dout, dyo, dg_post = _post_loss(yo, x, tgt, w["g_post"], "post_loss")

    dyc = _matmul(dyo, w["w_out"], tb=True, name="d_ycat", tn=512, tk=4096)
    dw_out = _matmul(ycat, dyo, ta=True, name="d_w_out", tn=512, tk=4096, out_dtype=BF16)
    dys, dr2, dkm2, dv2, dp, dln_w, dln_b, dr_k = _rwkv_post_bwd(
        dyc, ys, r, km, v, p, c, w["ln_w"], w["ln_b"], w["r_k"])
    dyf, dp = _gate_bwd(dyc, rw, yfox, p, c.o_gfox, "gate_fox_bwd", dp)
    dym, dp = _gate_bwd(dyc, 2 * rw, ymem, p, c.o_gmq, "gate_mem_bwd", dp)

    scan_g = _rwkv_scan_bwd(*scan_in, states, hd(dys), hb)
    dps, dzw, dza, twb, alb, dw0, da0, dk_k, dk_a = _rwkv_pre_bwd(
        p, c, *vecs, *(uh(z) for z in scan_g), dr2, dkm2, dv2)
    dwd = _matmul(twb, dzw, ta=True, name="d_w_decay", out_dtype=BF16)
    dwi = _matmul(alb, dza, ta=True, name="d_w_iclr", out_dtype=BF16)
    dp, dmu = _shift_bwd(dps, p, c, w["mu"], dp)

    dp, dfv, dck, dq_part, dcq_part = _fox2_bwd(p, c, cum_t, lse, yfox, dyf, FOX_BLOCK, dp)
    dfq, dcq = _fox2_dq_sum(dq_part, dcq_part, _tile(t, FOX_BLOCK, LANES))
    dcum = jnp.pad(jnp.stack([dcq.transpose(1, 0, 2).reshape(t, c.h), dck.reshape(c.h, t).T]),
                   ((0, 0), (0, 0), (0, LANES - c.h)))
    dfl, db_f = _fox_logit_bwd(dcum, p, c, w["b_f"])

    dmq, dmk, dmv = _mem_attn_bwd(p, c, mkv, dym)
    dmkv = jnp.concatenate([dmk, dmv], axis=1)
    dw_mkv = _matmul(memn, dmkv, ta=True, name="d_w_mem_kv", out_dtype=BF16)
    dmemn = _matmul(dmkv, w["w_mem_kv"], tb=True, name="d_memn")
    _, dg_mem = _rms_bwd(dmemn, mem, rinv_m, w["g_mem"], jnp.zeros_like(mem), "rms_mem_bwd")

    for off, piece in ((c.o_fq, dfq), (c.o_fv, dfv), (c.o_mq, dmq), (c.o_fl, dfl)):
        dp = lax.dynamic_update_slice(dp, piece, (0, off))
    rest = dict(wd=dwd, wi=dwi, w_mem_kv=dw_mkv, w_out=dw_out)
    if "d_w_in" in riders:
        dwp, carried["rest"] = _matmul(dp, h, ta=True, name="d_w_in", tk=4096, out_dtype=BF16,
                                       attach=riders["d_w_in"](rest))
    else:
        dwp = _matmul(dp, h, ta=True, name="d_w_in", tk=4096, out_dtype=BF16)
    if "d_h" in riders:
        dh, carried["wp"] = _matmul(dp, w["wp"], tb=True, name="d_h", tk=2944, attach=riders["d_h"](dwp))
    else:
        dh = _matmul(dp, w["wp"], tb=True, name="d_h", tk=2944)
    grad_x, dg_pre = _rms_bwd(dh, x, rinv, w["g_pre"], dout, "rms_pre_bwd")

    small = dict(g_pre=dg_pre, mu=dmu, w0=dw0, a0=da0, k_k=dk_k, k_a=dk_a, r_k=dr_k, ln_w=dln_w, ln_b=dln_b,
                 b_f=db_f, g_mem=dg_mem, g_post=dg_post)
    return loss, grad_x, dict(wp=dwp, **rest), small, carried


CHIPS = ((1, 0, 0), (0, 1, 0), (1, 1, 0))
SIBLING = ((0, 0, 1),)
ALL_PEERS = tuple((i, j, k) for i in (0, 1) for j in (0, 1) for k in (0, 1))[1:]


def _chip_of(pos):
    return 2 * pos[0] + pos[1]


DMA_CHUNK = 4 << 20


def _pieces(shape, itemsize):
    lead, (rows, cols) = shape[:-2], shape[-2:]
    k = 1
    if rows % 16 == 0:
        k = max(1, min(rows // 16, -(-rows * cols * itemsize // DMA_CHUNK)))
        while rows % k or (rows // k) % 16:
            k -= 1
    band = rows // k
    idxs = [()]
    for n in lead:
        idxs = [i + (j,) for i in idxs for j in range(n)]
    return [i + (pl.ds(j * band, band),) for i in idxs for j in range(k)]


def _peer_of(me, mask):
    return tuple(1 - v if f else v for v, f in zip(me, mask))


def _exchange(name, groups):
    n = len(groups)
    plan = _plan(groups)

    def body(*refs):
        copies = _copies(groups, plan, refs[:n], refs[n:2 * n], refs[2 * n], refs[2 * n + 1])
        for cp in copies:
            cp.start()
        for cp in copies:
            cp.wait()

    any_spec = pl.BlockSpec(memory_space=pl.ANY)
    return pl.pallas_call(
        body, name=name,
        in_specs=[any_spec] * n, out_specs=[any_spec] * n,
        out_shape=_exchange_shapes(groups),
        input_output_aliases={gi: gi for gi, g in enumerate(groups) if g.get("inplace")},
        scratch_shapes=[pltpu.SemaphoreType.DMA((len(plan),)), pltpu.SemaphoreType.DMA((len(plan),))],
    )(*[g["src"] for g in groups])


def _plan(groups):
    return [(gi, ti, idx) for gi, g in enumerate(groups) for ti in range(len(g["transfers"]))
            for idx in _pieces(tuple(g["piece"]), g["src"].dtype.itemsize)]


def _exchange_shapes(groups):
    lead = lambda s: tuple(s) if isinstance(s, tuple) else (s,)
    return [jax.ShapeDtypeStruct(lead(g["slots"]) + tuple(g["piece"]), g["src"].dtype) for g in groups]


def _copies(groups, plan, srcs, outs, send_sems, recv_sems):
    me = (lax.axis_index("x"), lax.axis_index("y"), lax.axis_index("c"))
    copies = []
    for k, (gi, ti, idx) in enumerate(plan):
        mask, view, slot = groups[gi]["transfers"][ti]
        peer = _peer_of(me, mask)
        copies.append(pltpu.make_async_remote_copy(
            src_ref=view(srcs[gi], me, peer).at[idx], dst_ref=outs[gi].at[slot(me, peer)].at[idx],
            send_sem=send_sems.at[k], recv_sem=recv_sems.at[k],
            device_id=peer, device_id_type=MESH))
    return copies


def _my_chip():
    return 2 * lax.axis_index("x") + lax.axis_index("y")


def _put(buf, block, slot):
    return lax.dynamic_update_slice(buf, block[None], (slot,) + (0,) * block.ndim)


def _sum_slots(recv, own, k, out_dtype, name):
    s, rows, cols = recv.shape
    budget = max(16, ((4 << 20) // ((s + 1) * cols * 4)) // 16 * 16)
    tr = _tile(rows, budget, 16)
    own_many = own.shape[0] > 1

    def body(k_ref, *refs):
        out_ref = refs[s + 1]
        mine = refs[s][0].astype(F32)
        acc = None
        for i in range(s):
            term = jnp.where(k_ref[0] == i, mine, refs[i][0].astype(F32))
            acc = term if acc is None else acc + term
        out_ref[...] = acc.astype(out_ref.dtype)

    def slot_spec(i):
        return pl.BlockSpec((1, tr, cols), lambda j, kr: (jnp.where(kr[0] == i, (i + 1) % s, i), j, 0))

    grid_spec = pltpu.PrefetchScalarGridSpec(
        num_scalar_prefetch=1, grid=(rows // tr,),
        in_specs=[slot_spec(i) for i in range(s)]
                 + [pl.BlockSpec((1, tr, cols), lambda j, kr: (kr[0] if own_many else 0, j, 0))],
        out_specs=pl.BlockSpec((tr, cols), lambda j, kr: (j, 0)))
    return pl.pallas_call(
        body, name=name, grid_spec=grid_spec,
        out_shape=jax.ShapeDtypeStruct((rows, cols), out_dtype),
        compiler_params=_params("parallel"),
    )(k, *([recv] * s), own)


def _all_gather(shards):
    return _gather_finish(shards, _exchange("gather_chips", _gather_groups(shards)), "gather_pair")


def _gather_groups(shards):
    halves = [s.reshape(2, s.shape[0] // 2, s.shape[1]) for s in shards]
    return [dict(src=q, slots=(4, 2), piece=q.shape[1:],
                 transfers=[(m, lambda ref, me, peer: ref.at[me[2]], lambda me, peer: (_chip_of(me), me[2]))
                            for m in CHIPS])
            for q in halves]


def _gather_finish(shards, first, name):
    spot = lambda m: (lambda me: (_chip_of(_peer_of(me, m)), me[2]))
    both = _exchange(name, [
        dict(src=q, slots=(4, 2), piece=q.shape[2:], inplace=True,
             transfers=[(SIBLING[0], (lambda f: lambda ref, me, peer: ref.at[f(me)])(spot(m)),
                         (lambda f: lambda me, peer: f(me))(spot(m))) for m in CHIPS])
        for q in first])
    return [_put(q.reshape((4,) + s.shape), s, _my_chip()) for s, q in zip(shards, both)]


def _reduce_pair(partials, tag):
    core1 = lax.axis_index("c").reshape(1).astype(jnp.int32)
    halves = [q.reshape(4, 2, q.shape[1] // 2, q.shape[2]).transpose(1, 0, 2, 3) for q in partials]
    pair = _exchange("reduce_pair_" + tag, [
        dict(src=q, slots=2, piece=q.shape[1:],
             transfers=[(SIBLING[0], lambda ref, me, peer: ref.at[peer[2]], lambda me, peer: me[2])])
        for q in halves])
    flat = lambda e: e.reshape(2, -1, e.shape[-1])
    return [_sum_slots(flat(e), flat(q), core1, BF16, "reduce_pair_sum_" + tag).reshape(q.shape[1:])
            for e, q in zip(pair, halves)]


def _reduce_chips_groups(chip_sums):
    return [dict(src=q, slots=4, piece=q.shape[1:],
                 transfers=[(m, lambda ref, me, peer: ref.at[_chip_of(peer)], lambda me, peer: _chip_of(me))
                            for m in CHIPS])
            for q in chip_sums]


def _reduce_finish(crossed, chip_sums, tag):
    core = lax.axis_index("c")
    chip1 = _my_chip().reshape(1).astype(jnp.int32)
    sums = [_sum_slots(e, q, chip1, F32, "reduce_chips_sum_" + tag) for e, q in zip(crossed, chip_sums)]
    swapped = _exchange("reduce_swap_" + tag, [
        dict(src=q, slots=2, piece=q.shape, transfers=[(SIBLING[0], lambda ref, me, peer: ref, lambda me, peer: me[2])])
        for q in sums])
    return [_put(e, q, core).reshape(-1, e.shape[-1]) for e, q in zip(swapped, sums)]


def _reduce_scatter(partials):
    chip_sums = _reduce_pair(partials, "all")
    return _reduce_finish(_exchange("reduce_chips", _reduce_chips_groups(chip_sums)), chip_sums, "all")


def _all_reduce_small(vec):
    dev = 4 * lax.axis_index("x") + 2 * lax.axis_index("y") + lax.axis_index("c")
    got = _exchange("reduce_small", [
        dict(src=vec, slots=8, piece=vec.shape,
             transfers=[(m, lambda ref, me, peer: ref, lambda me, peer: 4 * me[0] + 2 * me[1] + me[2])
                        for m in ALL_PEERS])])[0]
    return _sum_slots(got, vec[None], dev.reshape(1).astype(jnp.int32), F32, "reduce_small_sum")


SMALL = ("g_pre", "mu", "w0", "a0", "k_k", "k_a", "r_k", "ln_w", "ln_b", "b_f", "g_mem", "g_post")


def _pad_cols(a, n):
    return jnp.pad(a, ((0, 0),) * (a.ndim - 1) + ((0, n - a.shape[-1]),))


def kernel(x, mem, g_pre, w_in, mu_rwkv, w0, w_decay_up, a0, w_iclr_up, k_k, k_a, r_k, ln_x_w, ln_x_b, b_f, g_mem, w_mem_kv, w_out, g_post, loss_target, m_g_pre, m_w_in, m_mu_rwkv, m_w0, m_w_decay_up, m_a0, m_w_iclr_up, m_k_k, m_k_a, m_r_k, m_ln_x_w, m_ln_x_b, m_b_f, m_g_mem, m_w_mem_kv, m_w_out, m_g_post, v_g_pre, v_w_in, v_mu_rwkv, v_w0, v_w_decay_up, v_a0, v_w_iclr_up, v_k_k, v_k_a, v_r_k, v_ln_x_w, v_ln_x_b, v_b_f, v_g_mem, v_w_mem_kv, v_w_out, v_g_post):
    d = x.shape[-1]
    c = Cfg(d)
    ws = w_in.shape[-1]
    wpad = -(-ws // LANES) * LANES
    nh = c.h

    g_in, g_wd, g_wi = _all_gather([
        _pad_cols(w_in[0].astype(BF16), wpad), w_decay_up[0].astype(BF16), w_iclr_up[0].astype(BF16)])
    fl = c.ref_fl
    runs = [(0, fl, 0), (fl, fl + nh, c.o_fl), (fl + nh, c.in_width, fl)]
    pieces = []
    for lo, hi, _ in sorted(runs, key=lambda r: r[2]):
        for s in range(4):
            a, b = max(lo, s * ws), min(hi, (s + 1) * ws)
            if a < b:
                pieces.append(g_in[s, :, a - s * ws:b - s * ws])
    wp = jnp.concatenate(pieces + [jnp.zeros((d, LANES - nh), BF16)], axis=1)
    unshard = lambda g: g.transpose(1, 0, 2).reshape(g.shape[1], -1)
    weights = dict(wp=wp, wd=unshard(g_wd), wi=unshard(g_wi),
                   g_pre=g_pre, mu=mu_rwkv, w0=w0, a0=a0, k_k=k_k, k_a=k_a, r_k=r_k.reshape(1, -1),
                   ln_w=ln_x_w, ln_b=ln_x_b, b_f=_pad_cols(b_f, LANES), g_mem=g_mem, g_post=g_post)
    late_shards = [w_out[0].astype(BF16), w_mem_kv[0].astype(BF16)]

    def late_weights(first):
        g_out, g_mkv = _gather_finish(late_shards, first, "gather_pair_late")
        return dict(w_out=g_out.reshape(-1, d), w_mem_kv=g_mkv.reshape(d, -1))

    by_chip = lambda g: jnp.stack(jnp.split(g, 4, axis=1))
    pair_sums = {}

    def ride_rest(g):
        pair_sums["rest"] = _reduce_pair([g["w_out"].reshape(4, -1, d), g["w_mem_kv"].reshape(4, d // 4, -1),
                                          by_chip(g["wd"]), by_chip(g["wi"])], "rest")
        return _reduce_chips_groups(pair_sums["rest"])

    def ride_wp(dwpt):
        shards = []
        for s in range(4):
            rows = []
            for lo, hi, at in runs:
                a, b = max(lo, s * ws), min(hi, (s + 1) * ws)
                if a < b:
                    rows.append(dwpt[at + a - lo:at + b - lo, :])
            part = rows[0] if len(rows) == 1 else jnp.concatenate(rows, axis=0)
            shards.append(jnp.pad(part, ((0, wpad - ws), (0, 0))))
        pair_sums["wp"] = _reduce_pair([jnp.stack(shards)], "w_in")
        return _reduce_chips_groups(pair_sums["wp"])

    loss, grad_x, _, small, carried = _local_step(
        c, x[0], mem[0], loss_target[0], weights,
        riders={"in_proj": (_gather_groups(late_shards), late_weights), "d_w_in": ride_rest, "d_h": ride_wp})
    red = (_reduce_finish(carried["wp"], pair_sums["wp"], "w_in")
           + _reduce_finish(carried["rest"], pair_sums["rest"], "rest"))
    big_w = (w_in[0].T, w_out[0], w_mem_kv[0], w_decay_up[0], w_iclr_up[0])
    big_m = (m_w_in[0].T, m_w_out[0], m_w_mem_kv[0], m_w_decay_up[0], m_w_iclr_up[0])
    big_v = (v_w_in[0].T, v_w_out[0], v_w_mem_kv[0], v_w_decay_up[0], v_w_iclr_up[0])
    big_names = ("w_in", "w_out", "w_mem_kv", "w_decay_up", "w_iclr_up")
    upd = {n: _adamw(w_, g_, m_, v_, "adamw_" + n) for n, w_, g_, m_, v_ in zip(big_names, big_w, red, big_m, big_v)}
    upd["w_in"] = [o.T for o in upd["w_in"]]

    small_w = dict(g_pre=g_pre, mu=mu_rwkv, w0=w0, a0=a0, k_k=k_k, k_a=k_a, r_k=r_k.reshape(1, -1), ln_w=ln_x_w,
                   ln_b=ln_x_b, b_f=b_f, g_mem=g_mem, g_post=g_post)
    small_m = dict(g_pre=m_g_pre, mu=m_mu_rwkv, w0=m_w0, a0=m_a0, k_k=m_k_k, k_a=m_k_a, r_k=m_r_k.reshape(1, -1),
                   ln_w=m_ln_x_w, ln_b=m_ln_x_b, b_f=m_b_f, g_mem=m_g_mem, g_post=m_g_post)
    small_v = dict(g_pre=v_g_pre, mu=v_mu_rwkv, w0=v_w0, a0=v_a0, k_k=v_k_k, k_a=v_k_a, r_k=v_r_k.reshape(1, -1),
                   ln_w=v_ln_x_w, ln_b=v_ln_x_b, b_f=v_b_f, g_mem=v_g_mem, g_post=v_g_post)
    widths = [-(-small_w[n].shape[1] // LANES) * LANES for n in SMALL]
    pack = lambda t: jnp.concatenate([_pad_cols(t[n], wd_) for n, wd_ in zip(SMALL, widths)]
                                     + [jnp.zeros((1, LANES), F32)], axis=1)
    g_packed = jnp.concatenate([_pad_cols(small[n], wd_) for n, wd_ in zip(SMALL, widths)]
                               + [_pad_cols(loss, LANES)], axis=1)
    g_sum = _all_reduce_small(g_packed)
    s_upd = _adamw(pack(small_w), g_sum, pack(small_m), pack(small_v), "adamw_small")
    offs = [sum(widths[:i]) for i in range(len(SMALL))]

    def take(kind, n):
        i = SMALL.index(n)
        piece = s_upd[kind][:, offs[i]:offs[i] + small_w[n].shape[1]]
        return piece.reshape(r_k.shape) if n == "r_k" else piece

    total_loss = g_sum[0, sum(widths)]
    order = ("g_pre", "w_in", "mu", "w0", "w_decay_up", "a0", "w_iclr_up", "k_k", "k_a", "r_k", "ln_w", "ln_b", "b_f",
             "g_mem", "w_mem_kv", "w_out", "g_post")
    outs = [total_loss, grad_x[None]]
    for kind in range(4):
        for n in order:
            outs.append(upd[n][kind][None] if n in upd else take(kind, n))
    return tuple(outs)
```

```python
import jax
import jax.numpy as jnp
from jax import lax
from jax.experimental import pallas as pl
from jax.experimental.pallas import tpu as pltpu

F32 = jnp.float32
BF16 = jnp.bfloat16
HI = lax.Precision.HIGHEST
MESH = pl.DeviceIdType.MESH

HEAD_DIM = 64
MEM_HEADS = 4
LORA = 128
CHUNK = 64
RMS_EPS = 1e-6
GN_EPS = 64e-5
LANES = 128
WIDE = 2 * LANES
VMEM_LIMIT = 56 * 1024 * 1024

ADAM_LR, ADAM_B1, ADAM_B2, ADAM_EPS, ADAM_WD, ADAM_STEP = 0.001, 0.9, 0.999, 1e-08, 0.01, 10


class Cfg:
    def __init__(self, d):
        self.d = d
        self.rw = 3 * d // 8
        self.mw = d // 4
        self.h = self.rw // HEAD_DIM
        self.mhd = self.mw // MEM_HEADS
        self.shift = 3 * self.rw + 2 * LORA
        self.in_width = self.shift + 5 * self.rw + self.h + 2 * self.mw
        o = self.shift
        self.o_grw = o; o += self.rw
        self.o_fq = o; o += self.rw
        self.o_fk = o; o += self.rw
        self.o_fv = o; o += self.rw
        self.o_gfox = o; o += self.rw
        self.o_mq = o; o += self.mw
        self.o_gmq = o; o += self.mw
        self.o_fl = o; o += LANES
        self.wp = o
        self.ref_fl = self.shift + 4 * self.rw


def _tile(n, pref, align=LANES):
    if n <= pref:
        return n
    t = (pref // align) * align
    while t >= align:
        if n % t == 0:
            return t
        t -= align
    return n


def _params(*sem):
    return pltpu.CompilerParams(dimension_semantics=sem, vmem_limit_bytes=VMEM_LIMIT)


def _pallas_into(body, into, n_in, out_index, in_specs, **kw):
    if into is None:
        return pl.pallas_call(body, in_specs=in_specs, **kw)

    def body_with_alias(*refs):
        return body(*refs[:n_in], *refs[n_in + 1:])

    call = pl.pallas_call(body_with_alias, in_specs=list(in_specs) + [pl.BlockSpec(memory_space=pl.ANY)],
                          input_output_aliases={n_in: out_index}, **kw)
    return lambda *args: call(*args, into)


def _sig(x):
    return 1.0 / (1.0 + jnp.exp(-x))


def _softplus(x):
    return jnp.maximum(x, 0.0) + jnp.log(1.0 + jnp.exp(-jnp.abs(x)))


def _dot(a, b, dims, prec=None):
    return lax.dot_general(a, b, (dims, ((), ())), precision=prec, preferred_element_type=F32)


def _mm(a, b, prec=None):
    return _dot(a, b, ((1,), (0,)), prec)


def _mm_nt(a, b, prec=None):
    return _dot(a, b, ((1,), (1,)), prec)


def _mm_tn(a, b, prec=None):
    return _dot(a, b, ((0,), (0,)), prec)


def _split(a):
    hi = a.astype(BF16)
    return hi, (a - hi.astype(F32)).astype(BF16)


def _dot3(a, b, dims, passes=3):
    d = lambda x, y: lax.dot_general(x, y, dims, preferred_element_type=F32)
    if passes == 1:
        return d(a.astype(BF16), b.astype(BF16))
    if passes == 2:
        ah, (bh, bl) = a.astype(BF16), _split(b)
        return d(ah, bh) + d(ah, bl)
    (ah, al), (bh, bl) = _split(a), _split(b)
    return d(ah, bh) + (d(ah, bl) + d(al, bh))


def _bmm(a, b, passes=3):
    return _dot3(a, b, (((2,), (1,)), ((0,), (0,))), passes)


def _bmm_nt(a, b, passes=3):
    return _dot3(a, b, (((2,), (2,)), ((0,), (0,))), passes)


def _bmm_tn(a, b, passes=3):
    return _dot3(a, b, (((1,), (1,)), ((0,), (0,))), passes)


def _bmm_01(m01, x):
    x1 = x.astype(BF16)
    r1 = x - x1.astype(F32)
    x2 = r1.astype(BF16)
    x3 = (r1 - x2.astype(F32)).astype(BF16)
    d = lambda y: lax.dot_general(m01, y, (((2,), (1,)), ((0,), (0,))), preferred_element_type=F32)
    return d(x1) + (d(x2) + d(x3))


def _matmul(a, b, *, ta=False, tb=False, out_dtype=F32, name, tm=1024, tn=1024, tk=1024, attach=None):
    m, k = (a.shape[1], a.shape[0]) if ta else a.shape
    n = b.shape[0] if tb else b.shape[1]
    tm, tn, tk = _tile(m, tm), _tile(n, tn), _tile(k, tk)
    nk = k // tk
    grid = (m // tm, n // tn, nk)
    dims = ((0 if ta else 1,), (1 if tb else 0,))
    groups = attach or []
    ng = len(groups)
    plan = _plan(groups)

    def body(a_ref, b_ref, *rest):
        srcs, o_ref, outs, scratch = rest[:ng], rest[ng], rest[ng + 1:2 * ng + 1], rest[2 * ng + 1:]
        acc = scratch[0] if nk > 1 else None
        if ng:
            copies = _copies(groups, plan, srcs, outs, scratch[-2], scratch[-1])
            ids = [pl.program_id(ax) for ax in range(3)]

            @pl.when((ids[0] == 0) & (ids[1] == 0) & (ids[2] == 0))
            def _():
                for cp in copies:
                    cp.start()

        part = _dot(a_ref[...].astype(BF16), b_ref[...].astype(BF16), dims)
        if nk == 1:
            o_ref[...] = part.astype(o_ref.dtype)
        else:
            kk = pl.program_id(2)

            @pl.when(kk == 0)
            def _():
                acc[...] = part

            @pl.when(kk > 0)
            def _():
                acc[...] += part

            @pl.when(kk == nk - 1)
            def _():
                o_ref[...] = acc[...].astype(o_ref.dtype)

        if ng:
            @pl.when((ids[0] == grid[0] - 1) & (ids[1] == grid[1] - 1) & (ids[2] == grid[2] - 1))
            def _():
                for cp in copies:
                    cp.wait()

    a_spec = pl.BlockSpec((tk, tm), lambda i, j, kk: (kk, i)) if ta else pl.BlockSpec((tm, tk), lambda i, j, kk: (i, kk))
    b_spec = pl.BlockSpec((tn, tk), lambda i, j, kk: (j, kk)) if tb else pl.BlockSpec((tk, tn), lambda i, j, kk: (kk, j))
    any_spec = pl.BlockSpec(memory_space=pl.ANY)
    sems = [pltpu.SemaphoreType.DMA((len(plan),)), pltpu.SemaphoreType.DMA((len(plan),))] if ng else []
    res = pl.pallas_call(
        body, name=name, grid=grid,
        in_specs=[a_spec, b_spec] + [any_spec] * ng,
        out_specs=[pl.BlockSpec((tm, tn), lambda i, j, kk: (i, j))] + [any_spec] * ng,
        out_shape=[jax.ShapeDtypeStruct((m, n), out_dtype)] + _exchange_shapes(groups),
        scratch_shapes=([pltpu.VMEM((tm, tn), F32)] if nk > 1 else []) + sems,
        compiler_params=_params(*(("arbitrary",) * 3 if ng else ("parallel", "parallel", "arbitrary"))),
    )(a, b, *[g["src"] for g in groups])
    return (res[0], list(res[1:])) if ng else res[0]


def _rms_fwd(x, g, name):
    t, d = x.shape
    tm = _tile(t, 256, 8)

    def body(x_ref, g_ref, h_ref, r_ref):
        xv = x_ref[...]
        r = lax.rsqrt(jnp.mean(xv * xv, axis=-1, keepdims=True) + RMS_EPS)
        h_ref[...] = (xv * r * g_ref[...]).astype(BF16)
        r_ref[...] = r

    return pl.pallas_call(
        body, name=name, grid=(t // tm,),
        in_specs=[pl.BlockSpec((tm, d), lambda i: (i, 0)), pl.BlockSpec((1, d), lambda i: (0, 0))],
        out_specs=[pl.BlockSpec((tm, d), lambda i: (i, 0)), pl.BlockSpec((tm, 1), lambda i: (i, 0))],
        out_shape=[jax.ShapeDtypeStruct((t, d), BF16), jax.ShapeDtypeStruct((t, 1), F32)],
        compiler_params=_params("parallel"),
    )(x, g)


def _rms_bwd(dh, x, rinv, g, add, name):
    t, d = x.shape
    tm = _tile(t, 256, 8)

    def body(dh_ref, x_ref, r_ref, g_ref, add_ref, dx_ref, dg_ref):
        @pl.when(pl.program_id(0) == 0)
        def _():
            dg_ref[...] = jnp.zeros_like(dg_ref)

        r = r_ref[...]
        xn = x_ref[...] * r
        dhv = dh_ref[...]
        dg_ref[...] += jnp.sum(dhv * xn, axis=0, keepdims=True)
        dxn = dhv * g_ref[...]
        dx_ref[...] = add_ref[...] + r * (dxn - xn * jnp.mean(dxn * xn, axis=-1, keepdims=True))

    row = pl.BlockSpec((tm, d), lambda i: (i, 0))
    vec = pl.BlockSpec((1, d), lambda i: (0, 0))
    return pl.pallas_call(
        body, name=name, grid=(t // tm,),
        in_specs=[row, row, pl.BlockSpec((tm, 1), lambda i: (i, 0)), vec, row],
        out_specs=[row, vec],
        out_shape=[jax.ShapeDtypeStruct((t, d), F32), jax.ShapeDtypeStruct((1, d), F32)],
        compiler_params=_params("arbitrary"),
    )(dh, x, rinv, g, add)


def _post_loss(yo, x, tgt, g, name):
    t, d = x.shape
    tm = _tile(t, 256, 8)

    def body(yo_ref, x_ref, t_ref, g_ref, loss_ref, dout_ref, dyo_ref, dg_ref):
        @pl.when(pl.program_id(0) == 0)
        def _():
            dg_ref[...] = jnp.zeros_like(dg_ref)
            loss_ref[...] = jnp.zeros_like(loss_ref)

        yv = yo_ref[...]
        r = lax.rsqrt(jnp.mean(yv * yv, axis=-1, keepdims=True) + RMS_EPS)
        n = yv * r
        err = x_ref[...] + n * g_ref[...] - t_ref[...]
        loss_ref[...] += 0.5 * jnp.sum(jnp.mean(err * err, axis=-1, keepdims=True), axis=0, keepdims=True)
        dout = err * (1.0 / d)
        dout_ref[...] = dout
        dg_ref[...] += jnp.sum(dout * n, axis=0, keepdims=True)
        dn = dout * g_ref[...]
        dyo_ref[...] = (r * (dn - n * jnp.mean(dn * n, axis=-1, keepdims=True))).astype(BF16)

    row = pl.BlockSpec((tm, d), lambda i: (i, 0))
    vec = pl.BlockSpec((1, d), lambda i: (0, 0))
    return pl.pallas_call(
        body, name=name, grid=(t // tm,),
        in_specs=[row, row, row, vec],
        out_specs=[pl.BlockSpec((1, 1), lambda i: (0, 0)), row, row, vec],
        out_shape=[jax.ShapeDtypeStruct((1, 1), F32), jax.ShapeDtypeStruct((t, d), F32),
                   jax.ShapeDtypeStruct((t, d), BF16), jax.ShapeDtypeStruct((1, d), F32)],
        compiler_params=_params("arbitrary"),
    )(yo, x, tgt, g)


def _head_sum(x):
    ri = lax.broadcasted_iota(jnp.int32, (LANES, LANES), 0) // HEAD_DIM
    ci = lax.broadcasted_iota(jnp.int32, (LANES, LANES), 1) // HEAD_DIM
    e = (ri == ci).astype(BF16)
    x1 = x.astype(BF16)
    r1 = x - x1.astype(F32)
    x2 = r1.astype(BF16)
    x3 = (r1 - x2.astype(F32)).astype(BF16)
    parts = []
    for i in range(x.shape[1] // LANES):
        sl = slice(i * LANES, (i + 1) * LANES)
        parts.append(_mm(x1[:, sl], e) + (_mm(x2[:, sl], e) + _mm(x3[:, sl], e)))
    return parts[0] if len(parts) == 1 else jnp.concatenate(parts, axis=1)


def _shifted(p_cur, before, first, mu):
    rolled = pltpu.roll(p_cur, 1, 0)
    prev_row = jnp.where(first, 0.0, before)
    row0 = lax.broadcasted_iota(jnp.int32, p_cur.shape, 0) == 0
    prev = jnp.where(row0, prev_row, rolled)
    return p_cur + (prev - p_cur) * mu, prev


def _rwkv_features(ps, rw, w0, a0, k_k, k_a, wd, wi):
    r, k, v = ps[:, 0:rw], ps[:, rw:2 * rw], ps[:, 2 * rw:3 * rw]
    wl, al = ps[:, 3 * rw:3 * rw + LORA], ps[:, 3 * rw + LORA:3 * rw + 2 * LORA]
    tw = jnp.tanh(wl)
    zw = w0 + _mm(tw.astype(BF16), wd)
    logw = -jnp.exp(-_softplus(-zw) - 0.5)
    alpha = _sig(a0 + _mm(al.astype(BF16), wi))
    kkr = k * k_k
    n2 = _head_sum(kkr * kkr)
    rn = lax.rsqrt(jnp.maximum(n2, 1e-24))
    kk = kkr * rn
    kmod = k * (1.0 + (alpha - 1.0) * k_a)
    return dict(r=r, k=k, v=v, tw=tw, al=al, zw=zw, logw=logw, alpha=alpha, kk=kk, rn=rn, n2=n2, kmod=kmod)


def _rwkv_pre_fwd(p, c, mu, w0, a0, k_k, k_a, wd, wi):
    t = p.shape[0]
    tm = _tile(t, 128, 8)
    rw, sh = c.rw, c.shift

    def body(p_ref, pp_ref, mu_ref, w0_ref, a0_ref, kk_ref, ka_ref, wd_ref, wi_ref,
             r_ref, lw_ref, km_ref, v_ref, a_ref, b_ref):
        ps, _ = _shifted(p_ref[...], pp_ref[7:8, :], pl.program_id(0) == 0, mu_ref[...])
        f = _rwkv_features(ps, rw, w0_ref[...], a0_ref[...], kk_ref[...], ka_ref[...], wd_ref[...], wi_ref[...])
        r_ref[...] = f["r"]
        lw_ref[...] = f["logw"]
        km_ref[...] = f["kmod"]
        v_ref[...] = f["v"]
        a_ref[...] = -f["kk"]
        b_ref[...] = f["kk"] * f["alpha"]

    vec = lambda n: pl.BlockSpec((1, n), lambda i: (0, 0))
    out = pl.BlockSpec((tm, rw), lambda i: (i, 0))
    return pl.pallas_call(
        body, name="rwkv_pre_fwd", grid=(t // tm,),
        in_specs=[pl.BlockSpec((tm, sh), lambda i: (i, 0)),
                  pl.BlockSpec((8, sh), lambda i: (jnp.maximum(i * (tm // 8) - 1, 0), 0)),
                  vec(sh), vec(rw), vec(rw), vec(rw), vec(rw),
                  pl.BlockSpec((LORA, rw), lambda i: (0, 0)), pl.BlockSpec((LORA, rw), lambda i: (0, 0))],
        out_specs=[out] * 6,
        out_shape=[jax.ShapeDtypeStruct((t, rw), F32)] * 6,
        compiler_params=_params("parallel"),
    )(p, p, mu, w0, a0, k_k, k_a, wd, wi)


def _rwkv_pre_bwd(p, c, mu, w0, a0, k_k, k_a, wd, wi, dr, dlw, dkm, dv, da, db, dr2, dkm2, dv2):
    t = p.shape[0]
    tm = _tile(t, 128, 8)
    rw, sh = c.rw, c.shift

    def body(p_ref, pp_ref, mu_ref, w0_ref, a0_ref, kk_ref, ka_ref, wd_ref, wi_ref,
             dr_ref, dlw_ref, dkm_ref, dv_ref, da_ref, db_ref, dr2_ref, dkm2_ref, dv2_ref,
             dps_ref, dzw_ref, dza_ref, tw_ref, al_ref, dw0_ref, da0_ref, dkk_ref, dka_ref):
        @pl.when(pl.program_id(0) == 0)
        def _():
            for ref in (dw0_ref, da0_ref, dkk_ref, dka_ref):
                ref[...] = jnp.zeros_like(ref)

        ps, _ = _shifted(p_ref[...], pp_ref[7:8, :], pl.program_id(0) == 0, mu_ref[...])
        k_k, k_a = kk_ref[...], ka_ref[...]
        f = _rwkv_features(ps, rw, w0_ref[...], a0_ref[...], k_k, k_a, wd_ref[...], wi_ref[...])
        alpha, kk, k = f["alpha"], f["kk"], f["k"]
        dkm = dkm_ref[...] + dkm2_ref[...]
        db = db_ref[...]
        dkk = db * alpha - da_ref[...]
        dalpha = db * kk + dkm * k * k_a
        dk = dkm * (1.0 + (alpha - 1.0) * k_a)
        dka_ref[...] += jnp.sum(dkm * k * (alpha - 1.0), axis=0, keepdims=True)
        dkkr = f["rn"] * jnp.where(f["n2"] > 1e-24, dkk - kk * _head_sum(dkk * kk), dkk)
        dk = dk + dkkr * k_k
        dkk_ref[...] += jnp.sum(dkkr * k, axis=0, keepdims=True)
        dza = dalpha * alpha * (1.0 - alpha)
        da0_ref[...] += jnp.sum(dza, axis=0, keepdims=True)
        dzw = dlw_ref[...] * f["logw"] * _sig(-f["zw"])
        dw0_ref[...] += jnp.sum(dzw, axis=0, keepdims=True)
        dza_b, dzw_b = dza.astype(BF16), dzw.astype(BF16)
        dal = _mm_nt(dza_b, wi_ref[...])
        dwl = _mm_nt(dzw_b, wd_ref[...]) * (1.0 - f["tw"] * f["tw"])
        dps_ref[:, 0:rw] = dr_ref[...] + dr2_ref[...]
        dps_ref[:, rw:2 * rw] = dk
        dps_ref[:, 2 * rw:3 * rw] = dv_ref[...] + dv2_ref[...]
        dps_ref[:, 3 * rw:3 * rw + LORA] = dwl
        dps_ref[:, 3 * rw + LORA:sh] = dal
        dzw_ref[...] = dzw_b
        dza_ref[...] = dza_b
        tw_ref[...] = f["tw"].astype(BF16)
        al_ref[...] = f["al"].astype(BF16)

    vec = lambda n: pl.BlockSpec((1, n), lambda i: (0, 0))
    blk = lambda n: pl.BlockSpec((tm, n), lambda i: (i, 0))
    return pl.pallas_call(
        body, name="rwkv_pre_bwd", grid=(t // tm,),
        in_specs=[blk(sh), pl.BlockSpec((8, sh), lambda i: (jnp.maximum(i * (tm // 8) - 1, 0), 0)),
                  vec(sh), vec(rw), vec(rw), vec(rw), vec(rw),
                  pl.BlockSpec((LORA, rw), lambda i: (0, 0)), pl.BlockSpec((LORA, rw), lambda i: (0, 0))]
                 + [blk(rw)] * 9,
        out_specs=[blk(sh), blk(rw), blk(rw), blk(LORA), blk(LORA), vec(rw), vec(rw), vec(rw), vec(rw)],
        out_shape=[jax.ShapeDtypeStruct((t, sh), F32), jax.ShapeDtypeStruct((t, rw), BF16),
                   jax.ShapeDtypeStruct((t, rw), BF16), jax.ShapeDtypeStruct((t, LORA), BF16),
                   jax.ShapeDtypeStruct((t, LORA), BF16)] + [jax.ShapeDtypeStruct((1, rw), F32)] * 4,
        compiler_params=_params("arbitrary"),
    )(p, p, mu, w0, a0, k_k, k_a, wd, wi, dr, dlw, dkm, dv, da, db, dr2, dkm2, dv2)


def _shift_bwd(dps, p, c, mu, dp):
    t = p.shape[0]
    tm = _tile(t, 256, 8)
    sh = c.shift
    nt = t // tm

    def body(d_ref, dn_ref, p_ref, pp_ref, mu_ref, dp_ref, dmu_ref):
        i = pl.program_id(0)

        @pl.when(i == 0)
        def _():
            dmu_ref[...] = jnp.zeros_like(dmu_ref)

        mu = mu_ref[...]
        d = d_ref[...]
        pc = p_ref[...]
        _, prev = _shifted(pc, pp_ref[7:8, :], i == 0, mu)
        dmu_ref[...] += jnp.sum(d * (prev - pc), axis=0, keepdims=True)
        nxt_row = jnp.where(i == nt - 1, 0.0, dn_ref[0:1, :])
        last = lax.broadcasted_iota(jnp.int32, d.shape, 0) == tm - 1
        nxt = jnp.where(last, nxt_row, pltpu.roll(d, tm - 1, 0))
        dp_ref[...] = (d * (1.0 - mu) + nxt * mu).astype(BF16)

    blk = pl.BlockSpec((tm, sh), lambda i: (i, 0))
    return _pallas_into(
        body, dp, 5, 0, name="shift_bwd", grid=(nt,),
        in_specs=[blk, pl.BlockSpec((8, sh), lambda i: (jnp.minimum((i + 1) * (tm // 8), t // 8 - 1), 0)),
                  blk, pl.BlockSpec((8, sh), lambda i: (jnp.maximum(i * (tm // 8) - 1, 0), 0)),
                  pl.BlockSpec((1, sh), lambda i: (0, 0))],
        out_specs=[blk, pl.BlockSpec((1, sh), lambda i: (0, 0))],
        out_shape=[jax.ShapeDtypeStruct((t, c.wp), BF16), jax.ShapeDtypeStruct((1, sh), F32)],
        compiler_params=_params("arbitrary"),
    )(dps, dps, p, p, mu)


def _tri(n, strict):
    ri = lax.broadcasted_iota(jnp.int32, (n, n), 0)
    ci = lax.broadcasted_iota(jnp.int32, (n, n), 1)
    return (ri > ci) if strict else (ri >= ci)


def _unit_lower_inverse(a):
    n = a.shape[-1]
    ri = lax.broadcasted_iota(jnp.int32, (n, n), 0)
    ci = lax.broadcasted_iota(jnp.int32, (n, n), 1)
    eye = (ri == ci).astype(F32)
    blk = lambda s: (ri // s) == (ci // s)
    ad = jnp.where(blk(16), a, 0.0)
    p = eye + ad
    for _ in range(3):
        ad = _bmm(ad, ad, P_SOLVE)
        p = p + _bmm(p, ad, P_SOLVE)
    s = 16
    while s < n:
        off = jnp.where(blk(2 * s) & ~blk(s), a, 0.0)
        p = p + _bmm(_bmm(p, off, P_SOLVE), p, P_SOLVE)
        s *= 2
    return p


P_SOLVE, P_STATE, P_OUT, P_GRAD, P_DECAY = 1, 1, 1, 1, 2


def _chunk_common(r, lw, k, a, b):
    n = r.shape[1]
    tri_incl = jnp.broadcast_to(_tri(n, False).astype(BF16), (r.shape[0], n, n))
    cum = _bmm_01(tri_incl, lw)
    e_pos, e_neg, e_exc = jnp.exp(cum), jnp.exp(-cum), jnp.exp(cum - lw)
    last = lax.broadcasted_iota(jnp.int32, (n, r.shape[2]), 0) == n - 1
    g_last = jnp.exp(jnp.sum(jnp.where(last, cum, 0.0), axis=1, keepdims=True))
    return g_last, r * e_pos, a * e_exc, b * e_neg, k * e_neg, e_pos, e_neg, e_exc


def _chunk_solve(rt, at, bt, kt, v, g0):
    strict, incl = _tri(rt.shape[1], True), _tri(rt.shape[1], False)
    a_ab = jnp.where(strict, _bmm_nt(at, bt, P_SOLVE), 0.0)
    a_ak = jnp.where(strict, _bmm_nt(at, kt, P_SOLVE), 0.0)
    a_rb = jnp.where(incl, _bmm_nt(rt, bt, P_OUT), 0.0)
    a_rk = jnp.where(incl, _bmm_nt(rt, kt, P_OUT), 0.0)
    tinv = _unit_lower_inverse(a_ab)
    u = _bmm(tinv, _bmm(at, g0, P_SOLVE) + _bmm(a_ak, v, P_SOLVE), P_SOLVE)
    return a_ab, a_ak, a_rb, a_rk, tinv, u


def _diag_col(row, n):
    ri = lax.broadcasted_iota(jnp.int32, (n, n), 0)
    ci = lax.broadcasted_iota(jnp.int32, (n, n), 1)
    return jnp.sum(jnp.where(ri == ci, row, 0.0), axis=2, keepdims=True)


def _diag_row(col, n):
    ri = lax.broadcasted_iota(jnp.int32, (n, n), 0)
    ci = lax.broadcasted_iota(jnp.int32, (n, n), 1)
    return jnp.sum(jnp.where(ri == ci, col, 0.0), axis=1, keepdims=True)


def _rwkv_scan_fwd(r, lw, k, v, a, b, hb):
    h, t, n = r.shape
    nc = t // CHUNK

    def body(r_ref, lw_ref, k_ref, v_ref, a_ref, b_ref, y_ref, st_ref, g_sc):
        @pl.when(pl.program_id(1) == 0)
        def _():
            g_sc[...] = jnp.zeros_like(g_sc)

        g0 = g_sc[...]
        st_ref[0] = g0
        vv = v_ref[...]
        g_last, rt, at, bt, kt, _, _, _ = _chunk_common(r_ref[...], lw_ref[...], k_ref[...], a_ref[...], b_ref[...])
        _, _, a_rb, a_rk, _, u = _chunk_solve(rt, at, bt, kt, vv, g0)
        y_ref[...] = _bmm(rt, g0, P_OUT) + _bmm(a_rb, u, P_OUT) + _bmm(a_rk, vv, P_OUT)
        z = g0 + _bmm_tn(bt, u, P_STATE) + _bmm_tn(kt, vv, P_STATE)
        g_sc[...] = _diag_col(g_last, n) * z

    blk = pl.BlockSpec((hb, CHUNK, n), lambda i, j: (i, j, 0))
    return pl.pallas_call(
        body, name="rwkv_scan_fwd", grid=(h // hb, nc),
        in_specs=[blk] * 6,
        out_specs=[blk, pl.BlockSpec((1, hb, n, n), lambda i, j: (j, i, 0, 0))],
        out_shape=[jax.ShapeDtypeStruct((h, t, n), F32), jax.ShapeDtypeStruct((nc, h, n, n), F32)],
        scratch_shapes=[pltpu.VMEM((hb, n, n), F32)],
        compiler_params=_params("parallel", "arbitrary"),
    )(r, lw, k, v, a, b)


def _rwkv_scan_bwd(r, lw, k, v, a, b, states, dy, hb):
    h, t, n = r.shape
    nc = t // CHUNK

    def body(r_ref, lw_ref, k_ref, v_ref, a_ref, b_ref, st_ref, dy_ref,
             dr_ref, dlw_ref, dk_ref, dv_ref, da_ref, db_ref, dg_sc):
        @pl.when(pl.program_id(1) == 0)
        def _():
            dg_sc[...] = jnp.zeros_like(dg_sc)

        g0 = st_ref[0]
        vv, dyv, dh = v_ref[...], dy_ref[...], dg_sc[...]
        lwv = lw_ref[...]
        g_last, rt, at, bt, kt, e_pos, e_neg, e_exc = _chunk_common(r_ref[...], lwv, k_ref[...], a_ref[...], b_ref[...])
        a_ab, a_ak, a_rb, a_rk, tinv, u = _chunk_solve(rt, at, bt, kt, vv, g0)
        strict, incl = _tri(CHUNK, True), _tri(CHUNK, False)
        gcol = _diag_col(g_last, n)
        z = g0 + _bmm_tn(bt, u, P_STATE) + _bmm_tn(kt, vv, P_STATE)
        dz = gcol * dh
        dc_last = _diag_row(jnp.sum(dh * gcol * z, axis=2, keepdims=True), n)
        g = P_GRAD
        du = _bmm_tn(a_rb, dyv, g) + _bmm(bt, dz, g)
        dx = _bmm_tn(tinv, du, P_SOLVE)
        dv_ref[...] = _bmm_tn(a_rk, dyv, g) + _bmm(kt, dz, g) + _bmm_tn(a_ak, dx, g)
        da_ab = jnp.where(strict, _bmm_nt(dx, u, g), 0.0)
        da_ak = jnp.where(strict, _bmm_nt(dx, vv, g), 0.0)
        da_rb = jnp.where(incl, _bmm_nt(dyv, u, g), 0.0)
        da_rk = jnp.where(incl, _bmm_nt(dyv, vv, g), 0.0)
        g = P_DECAY
        d_at = _bmm(da_ab, bt, g) + _bmm(da_ak, kt, g) + _bmm_nt(dx, g0, g)
        d_rt = _bmm(da_rb, bt, g) + _bmm(da_rk, kt, g) + _bmm_nt(dyv, g0, g)
        d_bt = _bmm_tn(da_ab, at, g) + _bmm_tn(da_rb, rt, g) + _bmm_nt(u, dz, g)
        d_kt = _bmm_tn(da_ak, at, g) + _bmm_tn(da_rk, rt, g) + _bmm_nt(vv, dz, g)
        dg_sc[...] = dz + _bmm_tn(rt, dyv, P_STATE) + _bmm_tn(at, dx, P_STATE)
        dr_ref[...] = d_rt * e_pos
        da_ref[...] = d_at * e_exc
        db_ref[...] = d_bt * e_neg
        dk_ref[...] = d_kt * e_neg
        last = lax.broadcasted_iota(jnp.int32, (CHUNK, n), 0) == CHUNK - 1
        dc = d_rt * rt - d_bt * bt - d_kt * kt + jnp.where(last, dc_last, 0.0)
        dce = d_at * at
        ri = lax.broadcasted_iota(jnp.int32, (CHUNK, CHUNK), 0)
        ci = lax.broadcasted_iota(jnp.int32, (CHUNK, CHUNK), 1)
        up_incl = jnp.broadcast_to((ri <= ci).astype(BF16), (hb, CHUNK, CHUNK))
        dlw_ref[...] = _bmm_01(up_incl, dc + dce) - dce

    rev = lambda i, j: (i, nc - 1 - j, 0)
    blk = pl.BlockSpec((hb, CHUNK, n), rev)
    return pl.pallas_call(
        body, name="rwkv_scan_bwd", grid=(h // hb, nc),
        in_specs=[blk] * 6 + [pl.BlockSpec((1, hb, n, n), lambda i, j: (nc - 1 - j, i, 0, 0)), blk],
        out_specs=[blk] * 6,
        out_shape=[jax.ShapeDtypeStruct((h, t, n), F32)] * 6,
        scratch_shapes=[pltpu.VMEM((hb, n, n), F32)],
        compiler_params=_params("parallel", "arbitrary"),
    )(r, lw, k, v, a, b, states, dy)


def _silu_grad(g):
    s = _sig(g)
    return s * (1.0 + g * (1.0 - s))


def _group_norm(ys):
    yc = ys - _head_sum(ys) * (1.0 / HEAD_DIM)
    rstd = lax.rsqrt(_head_sum(yc * yc) * (1.0 / HEAD_DIM) + GN_EPS)
    return yc * rstd, rstd


def _rwkv_post_fwd(ys, r, km, v, p, c, ln_w, ln_b, r_k):
    t = ys.shape[0]
    tm = _tile(t, 1024, 8)
    goff = c.o_grw // WIDE

    def body(ys_ref, r_ref, km_ref, v_ref, g_ref, lw_ref, lb_ref, rk_ref, o_ref):
        yn, _ = _group_norm(ys_ref[...])
        s = _head_sum(r_ref[...] * km_ref[...] * rk_ref[...])
        g = g_ref[...]
        o_ref[...] = ((yn * lw_ref[...] + lb_ref[...] + s * v_ref[...]) * g * _sig(g)).astype(BF16)

    blk = pl.BlockSpec((tm, WIDE), lambda i, j: (i, j))
    vec = pl.BlockSpec((1, WIDE), lambda i, j: (0, j))
    return pl.pallas_call(
        body, name="rwkv_post_fwd", grid=(t // tm, c.rw // WIDE),
        in_specs=[blk] * 4 + [pl.BlockSpec((tm, WIDE), lambda i, j: (i, goff + j)), vec, vec, vec],
        out_specs=blk, out_shape=jax.ShapeDtypeStruct((t, c.d), BF16),
        compiler_params=_params("parallel", "parallel"),
    )(ys, r, km, v, p, ln_w, ln_b, r_k)


def _rwkv_post_bwd(dyc, ys, r, km, v, p, c, ln_w, ln_b, r_k):
    t = ys.shape[0]
    tm = _tile(t, 1024, 8)
    goff = c.o_grw // WIDE

    def body(dy_ref, ys_ref, r_ref, km_ref, v_ref, g_ref, lw_ref, lb_ref, rk_ref,
             dys_ref, dr_ref, dkm_ref, dv_ref, dg_ref, dlw_ref, dlb_ref, drk_ref):
        @pl.when(pl.program_id(1) == 0)
        def _():
            for ref in (dlw_ref, dlb_ref, drk_ref):
                ref[...] = jnp.zeros_like(ref)

        yn, rstd = _group_norm(ys_ref[...])
        rv, kmv, vv, rk, g = r_ref[...], km_ref[...], v_ref[...], rk_ref[...], g_ref[...]
        s = _head_sum(rv * kmv * rk)
        y = yn * lw_ref[...] + lb_ref[...] + s * vv
        dyc = dy_ref[...]
        dg_ref[...] = (dyc * y * _silu_grad(g)).astype(BF16)
        dy = dyc * g * _sig(g)
        dlb_ref[...] += jnp.sum(dy, axis=0, keepdims=True)
        dlw_ref[...] += jnp.sum(dy * yn, axis=0, keepdims=True)
        dyn = dy * lw_ref[...]
        inv = 1.0 / HEAD_DIM
        dys_ref[...] = rstd * (dyn - _head_sum(dyn) * inv - yn * _head_sum(dyn * yn) * inv)
        ds = _head_sum(dy * vv)
        dv_ref[...] = dy * s
        dr_ref[...] = ds * kmv * rk
        dkm_ref[...] = ds * rv * rk
        drk_ref[...] += jnp.sum(ds * rv * kmv, axis=0, keepdims=True)

    blk = pl.BlockSpec((tm, WIDE), lambda j, i: (i, j))
    vec = pl.BlockSpec((1, WIDE), lambda j, i: (0, j))
    f = jax.ShapeDtypeStruct((t, c.rw), F32)
    s1 = jax.ShapeDtypeStruct((1, c.rw), F32)
    gate = pl.BlockSpec((tm, WIDE), lambda j, i: (i, goff + j))
    return pl.pallas_call(
        body, name="rwkv_post_bwd", grid=(c.rw // WIDE, t // tm),
        in_specs=[blk] * 5 + [gate, vec, vec, vec],
        out_specs=[blk] * 4 + [gate] + [vec] * 3,
        out_shape=[f, f, f, f, jax.ShapeDtypeStruct((t, c.wp), BF16), s1, s1, s1],
        compiler_params=_params("parallel", "arbitrary"),
    )(dyc, ys, r, km, v, p, ln_w, ln_b, r_k)


def _gate_fwd(y, p, goff, name, ycat, yoff):
    t, w = y.shape
    tm = _tile(t, 1024, 8)
    gb, ob = goff // WIDE, yoff // WIDE

    def body(y_ref, g_ref, o_ref):
        g = g_ref[...]
        o_ref[...] = (y_ref[...] * g * _sig(g)).astype(BF16)

    blk = pl.BlockSpec((tm, WIDE), lambda i, j: (i, j))
    return _pallas_into(
        body, ycat, 2, 0, name=name, grid=(t // tm, w // WIDE),
        in_specs=[blk, pl.BlockSpec((tm, WIDE), lambda i, j: (i, gb + j))],
        out_specs=pl.BlockSpec((tm, WIDE), lambda i, j: (i, ob + j)),
        out_shape=jax.ShapeDtypeStruct(ycat.shape, BF16),
        compiler_params=_params("parallel", "parallel"),
    )(y, p)


def _gate_bwd(dyc, yoff, y, p, goff, name, dp):
    t, w = y.shape
    tm = _tile(t, 1024, 8)
    gb, yb = goff // WIDE, yoff // WIDE

    def body(d_ref, y_ref, g_ref, dy_ref, dg_ref):
        g, d = g_ref[...], d_ref[...]
        dy_ref[...] = d * g * _sig(g)
        dg_ref[...] = (d * y_ref[...] * _silu_grad(g)).astype(BF16)

    blk = pl.BlockSpec((tm, WIDE), lambda i, j: (i, j))
    gate = pl.BlockSpec((tm, WIDE), lambda i, j: (i, gb + j))
    return _pallas_into(
        body, dp, 3, 1, name=name, grid=(t // tm, w // WIDE),
        in_specs=[pl.BlockSpec((tm, WIDE), lambda i, j: (i, yb + j)), blk, gate],
        out_specs=[blk, gate],
        out_shape=[jax.ShapeDtypeStruct((t, w), F32), jax.ShapeDtypeStruct(dp.shape, BF16)],
        compiler_params=_params("parallel", "parallel"),
    )(dyc, y, p)


NEG = -1e30


def _fox_logit_bwd(dcum, p, c, b_f):
    t = p.shape[0]
    tm = _tile(t, 512, 8)
    fb = c.o_fl // LANES
    nt = t // tm

    def body(d_ref, f_ref, b_ref, o_ref, db_ref, carry):
        @pl.when(pl.program_id(0) == 0)
        def _():
            carry[...] = jnp.zeros_like(carry)
            db_ref[...] = jnp.zeros_like(db_ref)

        d = d_ref[0] + d_ref[1]
        dlogf = _mm(_tri(tm, False).astype(F32).T, d, HI) + carry[...]
        carry[...] += jnp.sum(d, axis=0, keepdims=True)
        df = dlogf * _sig(-(f_ref[...] + b_ref[...]))
        o_ref[...] = df.astype(BF16)
        db_ref[...] += jnp.sum(df, axis=0, keepdims=True)

    return pl.pallas_call(
        body, name="fox_logit_bwd", grid=(nt,),
        in_specs=[pl.BlockSpec((2, tm, LANES), lambda i: (0, nt - 1 - i, 0)),
                  pl.BlockSpec((tm, LANES), lambda i: (nt - 1 - i, fb)),
                  pl.BlockSpec((1, LANES), lambda i: (0, 0))],
        out_specs=[pl.BlockSpec((tm, LANES), lambda i: (nt - 1 - i, 0)), pl.BlockSpec((1, LANES), lambda i: (0, 0))],
        out_shape=[jax.ShapeDtypeStruct((t, LANES), BF16), jax.ShapeDtypeStruct((1, LANES), F32)],
        scratch_shapes=[pltpu.VMEM((1, LANES), F32)],
        compiler_params=_params("arbitrary"),
    )(dcum, p, b_f)


FOX_PAIRS = 2
FOX_HEADS_STEP = 2 * FOX_PAIRS


def _lane_half(shape, upper):
    li = lax.broadcasted_iota(jnp.int32, shape, len(shape) - 1)
    return (li >= HEAD_DIM) if upper else (li < HEAD_DIM)


def _col(block, j):
    li = lax.broadcasted_iota(jnp.int32, block.shape, 1)
    return jnp.sum(jnp.where(li == j, block, 0.0), axis=1, keepdims=True)


def _from_cols(cols):
    li = lax.broadcasted_iota(jnp.int32, (cols[0].shape[0], len(cols)), 1)
    out = jnp.zeros(li.shape, F32)
    for j, cj in enumerate(cols):
        out = jnp.where(li == j, cj, out)
    return out


def _from_rows(rows):
    si = lax.broadcasted_iota(jnp.int32, (len(rows), rows[0].shape[1]), 0)
    out = jnp.zeros(si.shape, F32)
    for j, rj in enumerate(rows):
        out = jnp.where(si == j, rj, out)
    return out


def _causal(tq, tk):
    return lax.broadcasted_iota(jnp.int32, (tq, tk), 1) <= lax.broadcasted_iota(jnp.int32, (tq, tk), 0)


def _fox_prep_t(p, c, b_f):
    t = p.shape[0]
    tm = _tile(t, 512, LANES)
    fb = c.o_fl // LANES

    def body(f_ref, b_ref, o_ref, carry):
        @pl.when(pl.program_id(0) == 0)
        def _():
            carry[...] = jnp.zeros_like(carry)

        logf = -_softplus(-(f_ref[...] + b_ref[...]))
        cum = _mm(_tri(tm, False).astype(F32), logf, HI) + carry[...]
        o_ref[...] = cum.T
        carry[...] += jnp.sum(logf, axis=0, keepdims=True)

    return pl.pallas_call(
        body, name="fox_prep", grid=(t // tm,),
        in_specs=[pl.BlockSpec((tm, LANES), lambda i: (i, fb)), pl.BlockSpec((1, LANES), lambda i: (0, 0))],
        out_specs=pl.BlockSpec((LANES, tm), lambda i: (0, i)),
        out_shape=jax.ShapeDtypeStruct((LANES, t), F32),
        scratch_shapes=[pltpu.VMEM((1, LANES), F32)],
        compiler_params=_params("arbitrary"),
    )(p, b_f)


def _fox2_fwd(p, c, cum_t, tb, ycat):
    t = p.shape[0]
    tq = tk = _tile(t, tb, LANES)
    nq = t // tq
    pw, nh = FOX_PAIRS * LANES, FOX_HEADS_STEP
    qb, kb, vb, gb = (o // pw for o in (c.o_fq, c.o_fk, c.o_fv, c.o_gfox))
    scale = HEAD_DIM ** -0.5

    def body(q_ref, k_ref, v_ref, g_ref, ck_ref, o_ref, y_ref, lse_ref, m_sc, l_sc, acc_sc):
        g, qi, ki = pl.program_id(0), pl.program_id(1), pl.program_id(2)

        @pl.when(ki == 0)
        def _():
            m_sc[...] = jnp.full_like(m_sc, NEG)
            l_sc[...] = jnp.zeros_like(l_sc)
            acc_sc[...] = jnp.zeros_like(acc_sc)

        def step(diag):
            ms, ls = [m_sc[h] for h in range(nh)], [l_sc[h] for h in range(nh)]
            accs = [acc_sc[:, pi * LANES:(pi + 1) * LANES] for pi in range(FOX_PAIRS)]
            for pi in range(FOX_PAIRS):
                lanes = slice(pi * LANES, (pi + 1) * LANES)
                q2 = (q_ref[:, lanes] * scale).astype(BF16)
                k2, v2 = k_ref[:, lanes].astype(BF16), v_ref[:, lanes].astype(BF16)
                new_acc = accs[pi]
                for hh in range(2):
                    hi = 2 * pi + hh
                    mk = _lane_half((tq, LANES), hh == 1)
                    s = _mm_nt(jnp.where(mk, q2, jnp.zeros_like(q2)), k2) - ck_ref[pl.ds(g * nh + hi, 1), :]
                    if diag:
                        s = jnp.where(_causal(tq, tk), s, NEG)
                    m_new = jnp.maximum(ms[hi], jnp.max(s, axis=1, keepdims=True))
                    a = jnp.exp(ms[hi] - m_new)
                    e = jnp.exp(s - jnp.concatenate([m_new] * (tk // LANES), axis=1))
                    ls[hi] = a * ls[hi] + jnp.sum(e, axis=1, keepdims=True)
                    ms[hi] = m_new
                    new_acc = jnp.where(mk, a * accs[pi] + _mm(e.astype(BF16), v2), new_acc)
                accs[pi] = new_acc
            for h in range(nh):
                m_sc[h] = ms[h]
                l_sc[h] = ls[h]
            for pi in range(FOX_PAIRS):
                acc_sc[:, pi * LANES:(pi + 1) * LANES] = accs[pi]

        @pl.when(ki < qi)
        def _():
            step(False)

        @pl.when(ki == qi)
        def _():
            step(True)
            li = lax.broadcasted_iota(jnp.int32, (tq, LANES), 1)
            lse = jnp.zeros((tq, LANES), F32)
            for pi in range(FOX_PAIRS):
                lanes = slice(pi * LANES, (pi + 1) * LANES)
                inv = jnp.where(_lane_half((tq, LANES), False), 1.0 / l_sc[2 * pi], 1.0 / l_sc[2 * pi + 1])
                o = acc_sc[:, lanes] * inv
                gate = g_ref[:, lanes]
                o_ref[:, lanes] = o
                y_ref[:, lanes] = (o * gate * _sig(gate)).astype(BF16)
            for h in range(nh):
                lse = jnp.where(li == h, m_sc[h] + jnp.log(l_sc[h]), lse)
            lse_ref[0] = lse

    row = lambda off: pl.BlockSpec((tq, pw), lambda g, i, j: (i, off + g))
    key = lambda off: pl.BlockSpec((tk, pw), lambda g, i, j: (jnp.minimum(i, j), off + g))
    out = pl.BlockSpec((tq, pw), lambda g, i, j: (i, g))
    return _pallas_into(
        body, ycat, 5, 1, name="fox_fwd", grid=(c.rw // pw, nq, nq),
        in_specs=[row(qb), key(kb), key(vb), row(gb),
                  pl.BlockSpec((LANES, tk), lambda g, i, j: (0, jnp.minimum(i, j)))],
        out_specs=[out, row(c.rw // pw), pl.BlockSpec((1, tq, LANES), lambda g, i, j: (g, i, 0))],
        out_shape=[jax.ShapeDtypeStruct((t, c.rw), F32), jax.ShapeDtypeStruct(ycat.shape, BF16),
                   jax.ShapeDtypeStruct((c.rw // pw, t, LANES), F32)],
        scratch_shapes=[pltpu.VMEM((nh, tq, LANES), F32), pltpu.VMEM((nh, tq, LANES), F32),
                        pltpu.VMEM((tq, pw), F32)],
        compiler_params=_params("parallel", "parallel", "arbitrary"),
    )(p, p, p, p, cum_t)


def _fox2_grads(q2, k2, v2, do2, o2, lse_h, ck, mk, diag, tq, tk):
    zero = jnp.zeros_like(q2)
    s = _mm_nt(jnp.where(mk, q2, zero), k2) - ck
    if diag:
        s = jnp.where(_causal(tq, tk), s, NEG)
    wide = lambda col: jnp.concatenate([jnp.broadcast_to(col, (tq, LANES))] * (tk // LANES), axis=1)
    pm = jnp.exp(s - wide(lse_h))
    delta = jnp.sum(jnp.where(mk, do2 * o2, 0.0), axis=1, keepdims=True)
    dob = do2.astype(BF16)
    dp = _mm_nt(jnp.where(mk, dob, zero), v2)
    return pm, pm * (dp - wide(delta)), dob


def _fox2_bwd(p, c, cum_t, lse, o, do, tb, dp):
    t = p.shape[0]
    tq = tk = _tile(t, tb, LANES)
    nq = t // tq
    pw, nh = FOX_PAIRS * LANES, FOX_HEADS_STEP
    ng = c.rw // pw
    qb, kb, vb = (o_ // pw for o_ in (c.o_fq, c.o_fk, c.o_fv))
    scale = HEAD_DIM ** -0.5

    def body(q_ref, k_ref, v_ref, ck_ref, lse_ref, o_ref, do_ref,
             dk_ref, dv_ref, dck_ref, dqp_ref, dcqp_ref, dk_sc, dv_sc, dc_sc):
        g, ki, qi = pl.program_id(0), pl.program_id(1), pl.program_id(2)

        @pl.when(qi == 0)
        def _():
            dk_sc[...] = jnp.zeros_like(dk_sc)
            dv_sc[...] = jnp.zeros_like(dv_sc)
            dc_sc[...] = jnp.zeros_like(dc_sc)

        def step(diag):
            lse_blk = lse_ref[0]
            dcs = [dc_sc[h] for h in range(nh)]
            dks = [dk_sc[:, pi * LANES:(pi + 1) * LANES] for pi in range(FOX_PAIRS)]
            dvs = [dv_sc[:, pi * LANES:(pi + 1) * LANES] for pi in range(FOX_PAIRS)]
            rows = []
            for pi in range(FOX_PAIRS):
                lanes = slice(pi * LANES, (pi + 1) * LANES)
                q2 = (q_ref[:, lanes] * scale).astype(BF16)
                k2, v2 = k_ref[:, lanes].astype(BF16), v_ref[:, lanes].astype(BF16)
                do2, o2 = do_ref[:, lanes], o_ref[:, lanes]
                new_dk, new_dv, dq2 = dks[pi], dvs[pi], None
                for hh in range(2):
                    hi = 2 * pi + hh
                    mk = _lane_half((tk, LANES), hh == 1)
                    pm, ds, dob = _fox2_grads(q2, k2, v2, do2, o2, _col(lse_blk, hi),
                                              ck_ref[pl.ds(g * nh + hi, 1), :], mk, diag, tq, tk)
                    dsb = ds.astype(BF16)
                    dcs[hi] = dcs[hi] - jnp.sum(ds, axis=0, keepdims=True)
                    rows.append(jnp.sum(ds, axis=1, keepdims=True))
                    new_dv = jnp.where(mk, dvs[pi] + _mm_tn(pm.astype(BF16), dob), new_dv)
                    new_dk = jnp.where(mk, dks[pi] + _mm_tn(dsb, q2), new_dk)
                    part = _mm(dsb, k2)
                    dq2 = part if hh == 0 else jnp.where(mk, part, dq2)
                dks[pi], dvs[pi] = new_dk, new_dv
                dqp_ref[0, :, lanes] = dq2 * scale
            dcqp_ref[0, 0] = _from_cols(rows)
            for h in range(nh):
                dc_sc[h] = dcs[h]
            for pi in range(FOX_PAIRS):
                dk_sc[:, pi * LANES:(pi + 1) * LANES] = dks[pi]
                dv_sc[:, pi * LANES:(pi + 1) * LANES] = dvs[pi]

        @pl.when(qi > ki)
        def _():
            step(False)

        @pl.when(qi == ki)
        def _():
            step(True)

        @pl.when(qi == nq - 1)
        def _():
            dk_ref[...] = dk_sc[...].astype(BF16)
            dv_ref[...] = dv_sc[...].astype(BF16)
            dck_ref[0] = _from_rows([dc_sc[h] for h in range(nh)])

    row = lambda off: pl.BlockSpec((tq, pw), lambda g, j, i: (jnp.maximum(i, j), off + g))
    key = lambda off: pl.BlockSpec((tk, pw), lambda g, j, i: (j, off + g))
    return _pallas_into(
        body, dp, 7, 0, name="fox_bwd", grid=(ng, nq, nq),
        in_specs=[row(qb), key(kb), key(vb), pl.BlockSpec((LANES, tk), lambda g, j, i: (0, j)),
                  pl.BlockSpec((1, tq, LANES), lambda g, j, i: (g, jnp.maximum(i, j), 0)), row(0), row(0)],
        out_specs=[key(kb), key(0), pl.BlockSpec((1, nh, tk), lambda g, j, i: (g, 0, j)),
                   pl.BlockSpec((1, tq, pw), lambda g, j, i: (j, jnp.maximum(i, j), g)),
                   pl.BlockSpec((1, 1, tq, nh), lambda g, j, i: (j, g, jnp.maximum(i, j), 0))],
        out_shape=[jax.ShapeDtypeStruct(dp.shape, BF16), jax.ShapeDtypeStruct((t, c.rw), BF16),
                   jax.ShapeDtypeStruct((ng, nh, t), F32), jax.ShapeDtypeStruct((nq, t, c.rw), F32),
                   jax.ShapeDtypeStruct((nq, ng, t, nh), F32)],
        scratch_shapes=[pltpu.VMEM((tk, pw), F32), pltpu.VMEM((tk, pw), F32), pltpu.VMEM((nh, 1, tk), F32)],
        compiler_params=_params("parallel", "parallel", "arbitrary"),
    )(p, p, p, cum_t, lse, o, do)


def _fox2_dq_sum(dq_part, dcq_part, tq):
    nk, t, rw = dq_part.shape
    ng, nh = dcq_part.shape[1], dcq_part.shape[3]

    def body(p_ref, c_ref, dq_ref, dcq_ref, acc, cacc):
        i, j = pl.program_id(0), pl.program_id(1)

        @pl.when(j == 0)
        def _():
            acc[...] = p_ref[0]
            cacc[...] = c_ref[0]

        @pl.when((j > 0) & (j <= i))
        def _():
            acc[...] += p_ref[0]
            cacc[...] += c_ref[0]

        @pl.when(j == nk - 1)
        def _():
            dq_ref[...] = acc[...].astype(BF16)
            dcq_ref[...] = cacc[...]

    return pl.pallas_call(
        body, name="fox_dq_sum", grid=(t // tq, nk),
        in_specs=[pl.BlockSpec((1, tq, rw), lambda i, j: (jnp.minimum(i, j), i, 0)),
                  pl.BlockSpec((1, ng, tq, nh), lambda i, j: (jnp.minimum(i, j), 0, i, 0))],
        out_specs=[pl.BlockSpec((tq, rw), lambda i, j: (i, 0)), pl.BlockSpec((ng, tq, nh), lambda i, j: (0, i, 0))],
        out_shape=[jax.ShapeDtypeStruct((t, rw), BF16), jax.ShapeDtypeStruct((ng, t, nh), F32)],
        scratch_shapes=[pltpu.VMEM((tq, rw), F32), pltpu.VMEM((ng, tq, nh), F32)],
        compiler_params=_params("parallel", "arbitrary"),
    )(dq_part, dcq_part)


def _mem_probs(q, mk, scale):
    s = _mm_nt(q.astype(BF16), mk.astype(BF16)) * scale
    e = jnp.exp(s - jnp.max(s, axis=1, keepdims=True))
    return e / jnp.sum(e, axis=1, keepdims=True)


def _mem_attn_fwd(p, c, mkv):
    t = p.shape[0]
    tm = _tile(t, 512, 8)
    dh = c.mhd
    qb = c.o_mq // dh
    scale = dh ** -0.5

    def body(q_ref, mk_ref, mv_ref, o_ref):
        pm = _mem_probs(q_ref[...], mk_ref[...], scale)
        o_ref[...] = _mm(pm.astype(BF16), mv_ref[...].astype(BF16))

    m = mkv.shape[0]
    return pl.pallas_call(
        body, name="mem_attn_fwd", grid=(t // tm, MEM_HEADS),
        in_specs=[pl.BlockSpec((tm, dh), lambda i, j: (i, qb + j)),
                  pl.BlockSpec((m, dh), lambda i, j: (0, j)),
                  pl.BlockSpec((m, dh), lambda i, j: (0, MEM_HEADS + j))],
        out_specs=pl.BlockSpec((tm, dh), lambda i, j: (i, j)),
        out_shape=jax.ShapeDtypeStruct((t, c.mw), F32),
        compiler_params=_params("parallel", "parallel"),
    )(p, mkv, mkv)


def _mem_attn_bwd(p, c, mkv, do):
    t = p.shape[0]
    tm = _tile(t, 512, 8)
    dh = c.mhd
    qb = c.o_mq // dh
    scale = dh ** -0.5
    m = mkv.shape[0]

    def body(q_ref, mk_ref, mv_ref, do_ref, dq_ref, dmk_ref, dmv_ref):
        @pl.when(pl.program_id(1) == 0)
        def _():
            dmk_ref[...] = jnp.zeros_like(dmk_ref)
            dmv_ref[...] = jnp.zeros_like(dmv_ref)

        qv = q_ref[...].astype(BF16)
        pm = _mem_probs(qv, mk_ref[...], scale)
        dob = do_ref[...].astype(BF16)
        dmv_ref[...] += _mm_tn(pm.astype(BF16), dob)
        dp = _mm_nt(dob, mv_ref[...].astype(BF16))
        ds = (pm * (dp - jnp.sum(pm * dp, axis=1, keepdims=True)) * scale).astype(BF16)
        dq_ref[...] = _mm(ds, mk_ref[...].astype(BF16)).astype(BF16)
        dmk_ref[...] += _mm_tn(ds, qv)

    kvb = lambda off: pl.BlockSpec((m, dh), lambda j, i: (0, off + j))
    return pl.pallas_call(
        body, name="mem_attn_bwd", grid=(MEM_HEADS, t // tm),
        in_specs=[pl.BlockSpec((tm, dh), lambda j, i: (i, qb + j)), kvb(0), kvb(MEM_HEADS),
                  pl.BlockSpec((tm, dh), lambda j, i: (i, j))],
        out_specs=[pl.BlockSpec((tm, dh), lambda j, i: (i, j)), kvb(0), kvb(0)],
        out_shape=[jax.ShapeDtypeStruct((t, c.mw), BF16), jax.ShapeDtypeStruct((m, c.mw), F32),
                   jax.ShapeDtypeStruct((m, c.mw), F32)],
        compiler_params=_params("parallel", "arbitrary"),
    )(p, mkv, mkv, do)


def _adamw(w, g, m, v, name):
    rows, cols = w.shape
    bc1 = 1.0 - ADAM_B1 ** ADAM_STEP
    bc2 = 1.0 - ADAM_B2 ** ADAM_STEP
    if rows % 8 and rows > 8:
        blk = pl.BlockSpec((rows, LANES), lambda i: (0, i))
        g_blk = pl.BlockSpec((g.shape[0], LANES), lambda i: (0, i))
        grid = (cols // LANES,)
    else:
        tm = _tile(rows, max(8, (1 << 19) // cols // 8 * 8), 8)
        blk = pl.BlockSpec((tm, cols), lambda i: (i, 0))
        g_blk = pl.BlockSpec((tm, g.shape[1]), lambda i: (i, 0))
        grid = (rows // tm,)
    brows, bcols = blk.block_shape

    def body(w_ref, g_ref, m_ref, v_ref, go_ref, d_ref, mo_ref, vo_ref):
        gv = g_ref[0:brows, 0:bcols]
        mn = ADAM_B1 * m_ref[...] + (1.0 - ADAM_B1) * gv
        vn = ADAM_B2 * v_ref[...] + (1.0 - ADAM_B2) * (gv * gv)
        go_ref[...] = gv
        mo_ref[...] = mn
        vo_ref[...] = vn
        d_ref[...] = -ADAM_LR * ((mn / bc1) / (jnp.sqrt(vn / bc2) + ADAM_EPS) + ADAM_WD * w_ref[...])

    shp = jax.ShapeDtypeStruct((rows, cols), F32)
    return pl.pallas_call(
        body, name=name, grid=grid,
        in_specs=[blk, g_blk, blk, blk],
        out_specs=[blk] * 4, out_shape=[shp] * 4,
        compiler_params=_params("parallel"),
    )(w, g, m, v)


SCAN_HEADS = 12
FOX_BLOCK = 512


def _local_step(c, x, mem, tgt, w, riders=None):
    t = x.shape[0]
    rw = c.rw
    riders = riders or {}
    carried = {}
    hd = lambda z: z.reshape(t, c.h, HEAD_DIM).transpose(1, 0, 2)
    uh = lambda z: z.transpose(1, 0, 2).reshape(t, rw)
    vecs = (w["mu"], w["w0"], w["a0"], w["k_k"], w["k_a"], w["wd"], w["wi"])

    h, rinv = _rms_fwd(x, w["g_pre"], "rms_pre")
    if "in_proj" in riders:
        groups, finish = riders["in_proj"]
        p, late = _matmul(h, w["wp"], name="in_proj", tk=4096, attach=groups)
        w = dict(w, **finish(late))
    else:
        p = _matmul(h, w["wp"], name="in_proj", tk=4096)
    r, lw, km, v, a, b = _rwkv_pre_fwd(p, c, *vecs)
    scan_in = tuple(hd(z) for z in (r, lw, km, v, a, b))
    hb = max(n for n in range(1, SCAN_HEADS + 1) if c.h % n == 0)
    ysh, states = _rwkv_scan_fwd(*scan_in, c.h if c.h <= 2 * SCAN_HEADS else hb)
    ys = uh(ysh)
    ycat = _rwkv_post_fwd(ys, r, km, v, p, c, w["ln_w"], w["ln_b"], w["r_k"])

    cum_t = _fox_prep_t(p, c, w["b_f"])
    yfox, ycat, lse = _fox2_fwd(p, c, cum_t, FOX_BLOCK, ycat)

    memn, rinv_m = _rms_fwd(mem, w["g_mem"], "rms_mem")
    mkv = _matmul(memn, w["w_mem_kv"], name="mem_kv")
    ymem = _mem_attn_fwd(p, c, mkv)
    ycat = _gate_fwd(ymem, p, c.o_gmq, "gate_mem", ycat, 2 * rw)
    yo =_matmul(ycat, w["w_out"], name="out_proj", tn=512, tk=4096)
    loss, dout, dyo, dg_post = _post_loss(yo, x, tgt, w["g_post"], "post_loss")

    dyc = _matmul(dyo, w["w_out"], tb=True, name="d_ycat", tn=512, tk=4096)
    dw_out = _matmul(ycat, dyo, ta=True, name="d_w_out", tn=512, tk=4096, out_dtype=BF16)
    dys, dr2, dkm2, dv2, dp, dln_w, dln_b, dr_k = _rwkv_post_bwd(
        dyc, ys, r, km, v, p, c, w["ln_w"], w["ln_b"], w["r_k"])
    dyf, dp = _gate_bwd(dyc, rw, yfox, p, c.o_gfox, "gate_fox_bwd", dp)
    dym, dp = _gate_bwd(dyc, 2 * rw, ymem, p, c.o_gmq, "gate_mem_bwd", dp)

    scan_g = _rwkv_scan_bwd(*scan_in, states, hd(dys), hb)
    dps, dzw, dza, twb, alb, dw0, da0, dk_k, dk_a = _rwkv_pre_bwd(
        p, c, *vecs, *(uh(z) for z in scan_g), dr2, dkm2, dv2)
    dwd = _matmul(twb, dzw, ta=True, name="d_w_decay", out_dtype=BF16)
    dwi = _matmul(alb, dza, ta=True, name="d_w_iclr", out_dtype=BF16)
    dp, dmu = _shift_bwd(dps, p, c, w["mu"], dp)

    dp, dfv, dck, dq_part, dcq_part = _fox2_bwd(p, c, cum_t, lse, yfox, dyf, FOX_BLOCK, dp)
    dfq, dcq = _fox2_dq_sum(dq_part, dcq_part, _tile(t, FOX_BLOCK, LANES))
    dcum = jnp.pad(jnp.stack([dcq.transpose(1, 0, 2).reshape(t, c.h), dck.reshape(c.h, t).T]),
                   ((0, 0), (0, 0), (0, LANES - c.h)))
    dfl, db_f = _fox_logit_bwd(dcum, p, c, w["b_f"])

    dmq, dmk, dmv = _mem_attn_bwd(p, c, mkv, dym)
    dmkv = jnp.concatenate([dmk, dmv], axis=1)
    dw_mkv = _matmul(memn, dmkv, ta=True, name="d_w_mem_kv", out_dtype=BF16)
    dmemn = _matmul(dmkv, w["w_mem_kv"], tb=True, name="d_memn")
    _, dg_mem = _rms_bwd(dmemn, mem, rinv_m, w["g_mem"], jnp.zeros_like(mem), "rms_mem_bwd")

    for off, piece in ((c.o_fq, dfq), (c.o_fv, dfv), (c.o_mq, dmq), (c.o_fl, dfl)):
        dp = lax.dynamic_update_slice(dp, piece, (0, off))
    rest = dict(wd=dwd, wi=dwi, w_mem_kv=dw_mkv, w_out=dw_out)
    if "d_w_in" in riders:
        dwp, carried["rest"] = _matmul(dp, h, ta=True, name="d_w_in", tk=4096, out_dtype=BF16,
                                       attach=riders["d_w_in"](rest))
    else:
        dwp = _matmul(dp, h, ta=True, name="d_w_in", tk=4096, out_dtype=BF16)
    if "d_h" in riders:
        dh, carried["wp"] = _matmul(dp, w["wp"], tb=True, name="d_h", tk=2944, attach=riders["d_h"](dwp))
    else:
        dh = _matmul(dp, w["wp"], tb=True, name="d_h", tk=2944)
    grad_x, dg_pre = _rms_bwd(dh, x, rinv, w["g_pre"], dout, "rms_pre_bwd")

    small = dict(g_pre=dg_pre, mu=dmu, w0=dw0, a0=da0, k_k=dk_k, k_a=dk_a, r_k=dr_k, ln_w=dln_w, ln_b=dln_b,
                 b_f=db_f, g_mem=dg_mem, g_post=dg_post)
    return loss, grad_x, dict(wp=dwp, **rest), small, carried


CHIPS = ((1, 0, 0), (0, 1, 0), (1, 1, 0))
SIBLING = ((0, 0, 1),)
ALL_PEERS = tuple((i, j, k) for i in (0, 1) for j in (0, 1) for k in (0, 1))[1:]


def _chip_of(pos):
    return 2 * pos[0] + pos[1]


DMA_CHUNK = 4 << 20


def _pieces(shape, itemsize):
    lead, (rows, cols) = shape[:-2], shape[-2:]
    k = 1
    if rows % 16 == 0:
        k = max(1, min(rows // 16, -(-rows * cols * itemsize // DMA_CHUNK)))
        while rows % k or (rows // k) % 16:
            k -= 1
    band = rows // k
    idxs = [()]
    for n in lead:
        idxs = [i + (j,) for i in idxs for j in range(n)]
    return [i + (pl.ds(j * band, band),) for i in idxs for j in range(k)]


def _peer_of(me, mask):
    return tuple(1 - v if f else v for v, f in zip(me, mask))


def _exchange(name, groups):
    n = len(groups)
    plan = _plan(groups)

    def body(*refs):
        copies = _copies(groups, plan, refs[:n], refs[n:2 * n], refs[2 * n], refs[2 * n + 1])
        for cp in copies:
            cp.start()
        for cp in copies:
            cp.wait()

    any_spec = pl.BlockSpec(memory_space=pl.ANY)
    return pl.pallas_call(
        body, name=name,
        in_specs=[any_spec] * n, out_specs=[any_spec] * n,
        out_shape=_exchange_shapes(groups),
        input_output_aliases={gi: gi for gi, g in enumerate(groups) if g.get("inplace")},
        scratch_shapes=[pltpu.SemaphoreType.DMA((len(plan),)), pltpu.SemaphoreType.DMA((len(plan),))],
    )(*[g["src"] for g in groups])


def _plan(groups):
    return [(gi, ti, idx) for gi, g in enumerate(groups) for ti in range(len(g["transfers"]))
            for idx in _pieces(tuple(g["piece"]), g["src"].dtype.itemsize)]


def _exchange_shapes(groups):
    lead = lambda s: tuple(s) if isinstance(s, tuple) else (s,)
    return [jax.ShapeDtypeStruct(lead(g["slots"]) + tuple(g["piece"]), g["src"].dtype) for g in groups]


def _copies(groups, plan, srcs, outs, send_sems, recv_sems):
    me = (lax.axis_index("x"), lax.axis_index("y"), lax.axis_index("c"))
    copies = []
    for k, (gi, ti, idx) in enumerate(plan):
        mask, view, slot = groups[gi]["transfers"][ti]
        peer = _peer_of(me, mask)
        copies.append(pltpu.make_async_remote_copy(
            src_ref=view(srcs[gi], me, peer).at[idx], dst_ref=outs[gi].at[slot(me, peer)].at[idx],
            send_sem=send_sems.at[k], recv_sem=recv_sems.at[k],
            device_id=peer, device_id_type=MESH))
    return copies


def _my_chip():
    return 2 * lax.axis_index("x") + lax.axis_index("y")


def _put(buf, block, slot):
    return lax.dynamic_update_slice(buf, block[None], (slot,) + (0,) * block.ndim)


def _sum_slots(recv, own, k, out_dtype, name):
    s, rows, cols = recv.shape
    budget = max(16, ((4 << 20) // ((s + 1) * cols * 4)) // 16 * 16)
    tr = _tile(rows, budget, 16)
    own_many = own.shape[0] > 1

    def body(k_ref, *refs):
        out_ref = refs[s + 1]
        mine = refs[s][0].astype(F32)
        acc = None
        for i in range(s):
            term = jnp.where(k_ref[0] == i, mine, refs[i][0].astype(F32))
            acc = term if acc is None else acc + term
        out_ref[...] = acc.astype(out_ref.dtype)

    def slot_spec(i):
        return pl.BlockSpec((1, tr, cols), lambda j, kr: (jnp.where(kr[0] == i, (i + 1) % s, i), j, 0))

    grid_spec = pltpu.PrefetchScalarGridSpec(
        num_scalar_prefetch=1, grid=(rows // tr,),
        in_specs=[slot_spec(i) for i in range(s)]
                 + [pl.BlockSpec((1, tr, cols), lambda j, kr: (kr[0] if own_many else 0, j, 0))],
        out_specs=pl.BlockSpec((tr, cols), lambda j, kr: (j, 0)))
    return pl.pallas_call(
        body, name=name, grid_spec=grid_spec,
        out_shape=jax.ShapeDtypeStruct((rows, cols), out_dtype),
        compiler_params=_params("parallel"),
    )(k, *([recv] * s), own)


def _all_gather(shards):
    return _gather_finish(shards, _exchange("gather_chips", _gather_groups(shards)), "gather_pair")


def _gather_groups(shards):
    halves = [s.reshape(2, s.shape[0] // 2, s.shape[1]) for s in shards]
    return [dict(src=q, slots=(4, 2), piece=q.shape[1:],
                 transfers=[(m, lambda ref, me, peer: ref.at[me[2]], lambda me, peer: (_chip_of(me), me[2]))
                            for m in CHIPS])
            for q in halves]


def _gather_finish(shards, first, name):
    spot = lambda m: (lambda me: (_chip_of(_peer_of(me, m)), me[2]))
    both = _exchange(name, [
        dict(src=q, slots=(4, 2), piece=q.shape[2:], inplace=True,
             transfers=[(SIBLING[0], (lambda f: lambda ref, me, peer: ref.at[f(me)])(spot(m)),
                         (lambda f: lambda me, peer: f(me))(spot(m))) for m in CHIPS])
        for q in first])
    return [_put(q.reshape((4,) + s.shape), s, _my_chip()) for s, q in zip(shards, both)]


def _reduce_pair(partials, tag):
    core1 = lax.axis_index("c").reshape(1).astype(jnp.int32)
    halves = [q.reshape(4, 2, q.shape[1] // 2, q.shape[2]).transpose(1, 0, 2, 3) for q in partials]
    pair = _exchange("reduce_pair_" + tag, [
        dict(src=q, slots=2, piece=q.shape[1:],
             transfers=[(SIBLING[0], lambda ref, me, peer: ref.at[peer[2]], lambda me, peer: me[2])])
        for q in halves])
    flat = lambda e: e.reshape(2, -1, e.shape[-1])
    return [_sum_slots(flat(e), flat(q), core1, BF16, "reduce_pair_sum_" + tag).reshape(q.shape[1:])
            for e, q in zip(pair, halves)]


def _reduce_chips_groups(chip_sums):
    return [dict(src=q, slots=4, piece=q.shape[1:],
                 transfers=[(m, lambda ref, me, peer: ref.at[_chip_of(peer)], lambda me, peer: _chip_of(me))
                            for m in CHIPS])
            for q in chip_sums]


def _reduce_finish(crossed, chip_sums, tag):
    core = lax.axis_index("c")
    chip1 = _my_chip().reshape(1).astype(jnp.int32)
    sums = [_sum_slots(e, q, chip1, F32, "reduce_chips_sum_" + tag) for e, q in zip(crossed, chip_sums)]
    swapped = _exchange("reduce_swap_" + tag, [
        dict(src=q, slots=2, piece=q.shape, transfers=[(SIBLING[0], lambda ref, me, peer: ref, lambda me, peer: me[2])])
        for q in sums])
    return [_put(e, q, core).reshape(-1, e.shape[-1]) for e, q in zip(swapped, sums)]


def _reduce_scatter(partials):
    chip_sums = _reduce_pair(partials, "all")
    return _reduce_finish(_exchange("reduce_chips", _reduce_chips_groups(chip_sums)), chip_sums, "all")


def _all_reduce_small(vec):
    dev = 4 * lax.axis_index("x") + 2 * lax.axis_index("y") + lax.axis_index("c")
    got = _exchange("reduce_small", [
        dict(src=vec, slots=8, piece=vec.shape,
             transfers=[(m, lambda ref, me, peer: ref, lambda me, peer: 4 * me[0] + 2 * me[1] + me[2])
                        for m in ALL_PEERS])])[0]
    return _sum_slots(got, vec[None], dev.reshape(1).astype(jnp.int32), F32, "reduce_small_sum")


SMALL = ("g_pre", "mu", "w0", "a0", "k_k", "k_a", "r_k", "ln_w", "ln_b", "b_f", "g_mem", "g_post")


def _pad_cols(a, n):
    return jnp.pad(a, ((0, 0),) * (a.ndim - 1) + ((0, n - a.shape[-1]),))


def kernel(x, mem, g_pre, w_in, mu_rwkv, w0, w_decay_up, a0, w_iclr_up, k_k, k_a, r_k, ln_x_w, ln_x_b, b_f, g_mem, w_mem_kv, w_out, g_post, loss_target, m_g_pre, m_w_in, m_mu_rwkv, m_w0, m_w_decay_up, m_a0, m_w_iclr_up, m_k_k, m_k_a, m_r_k, m_ln_x_w, m_ln_x_b, m_b_f, m_g_mem, m_w_mem_kv, m_w_out, m_g_post, v_g_pre, v_w_in, v_mu_rwkv, v_w0, v_w_decay_up, v_a0, v_w_iclr_up, v_k_k, v_k_a, v_r_k, v_ln_x_w, v_ln_x_b, v_b_f, v_g_mem, v_w_mem_kv, v_w_out, v_g_post):
    d = x.shape[-1]
    c = Cfg(d)
    ws = w_in.shape[-1]
    wpad = -(-ws // LANES) * LANES
    nh = c.h

    g_in, g_wd, g_wi = _all_gather([
        _pad_cols(w_in[0].astype(BF16), wpad), w_decay_up[0].astype(BF16), w_iclr_up[0].astype(BF16)])
    fl = c.ref_fl
    runs = [(0, fl, 0), (fl, fl + nh, c.o_fl), (fl + nh, c.in_width, fl)]
    pieces = []
    for lo, hi, _ in sorted(runs, key=lambda r: r[2]):
        for s in range(4):
            a, b = max(lo, s * ws), min(hi, (s + 1) * ws)
            if a < b:
                pieces.append(g_in[s, :, a - s * ws:b - s * ws])
    wp = jnp.concatenate(pieces + [jnp.zeros((d, LANES - nh), BF16)], axis=1)
    unshard = lambda g: g.transpose(1, 0, 2).reshape(g.shape[1], -1)
    weights = dict(wp=wp, wd=unshard(g_wd), wi=unshard(g_wi),
                   g_pre=g_pre, mu=mu_rwkv, w0=w0, a0=a0, k_k=k_k, k_a=k_a, r_k=r_k.reshape(1, -1),
                   ln_w=ln_x_w, ln_b=ln_x_b, b_f=_pad_cols(b_f, LANES), g_mem=g_mem, g_post=g_post)
    late_shards = [w_out[0].astype(BF16), w_mem_kv[0].astype(BF16)]

    def late_weights(first):
        g_out, g_mkv = _gather_finish(late_shards, first, "gather_pair_late")
        return dict(w_out=g_out.reshape(-1, d), w_mem_kv=g_mkv.reshape(d, -1))

    by_chip = lambda g: jnp.stack(jnp.split(g, 4, axis=1))
    pair_sums = {}

    def ride_rest(g):
        pair_sums["rest"] = _reduce_pair([g["w_out"].reshape(4, -1, d), g["w_mem_kv"].reshape(4, d // 4, -1),
                                          by_chip(g["wd"]), by_chip(g["wi"])], "rest")
        return _reduce_chips_groups(pair_sums["rest"])

    def ride_wp(dwpt):
        shards = []
        for s in range(4):
            rows = []
            for lo, hi, at in runs:
                a, b = max(lo, s * ws), min(hi, (s + 1) * ws)
                if a < b:
                    rows.append(dwpt[at + a - lo:at + b - lo, :])
            part = rows[0] if len(rows) == 1 else jnp.concatenate(rows, axis=0)
            shards.append(jnp.pad(part, ((0, wpad - ws), (0, 0))))
        pair_sums["wp"] = _reduce_pair([jnp.stack(shards)], "w_in")
        return _reduce_chips_groups(pair_sums["wp"])

    loss, grad_x, _, small, carried = _local_step(
        c, x[0], mem[0], loss_target[0], weights,
        riders={"in_proj": (_gather_groups(late_shards), late_weights), "d_w_in": ride_rest, "d_h": ride_wp})
    red = (_reduce_finish(carried["wp"], pair_sums["wp"], "w_in")
           + _reduce_finish(carried["rest"], pair_sums["rest"], "rest"))
    big_w = (w_in[0].T, w_out[0], w_mem_kv[0], w_decay_up[0], w_iclr_up[0])
    big_m = (m_w_in[0].T, m_w_out[0], m_w_mem_kv[0], m_w_decay_up[0], m_w_iclr_up[0])
    big_v = (v_w_in[0].T, v_w_out[0], v_w_mem_kv[0], v_w_decay_up[0], v_w_iclr_up[0])
    big_names = ("w_in", "w_out", "w_mem_kv", "w_decay_up", "w_iclr_up")
    upd = {n: _adamw(w_, g_, m_, v_, "adamw_" + n) for n, w_, g_, m_, v_ in zip(big_names, big_w, red, big_m, big_v)}
    upd["w_in"] = [o.T for o in upd["w_in"]]

    small_w = dict(g_pre=g_pre, mu=mu_rwkv, w0=w0, a0=a0, k_k=k_k, k_a=k_a, r_k=r_k.reshape(1, -1), ln_w=ln_x_w,
                   ln_b=ln_x_b, b_f=b_f, g_mem=g_mem, g_post=g_post)
    small_m = dict(g_pre=m_g_pre, mu=m_mu_rwkv, w0=m_w0, a0=m_a0, k_k=m_k_k, k_a=m_k_a, r_k=m_r_k.reshape(1, -1),
                   ln_w=m_ln_x_w, ln_b=m_ln_x_b, b_f=m_b_f, g_mem=m_g_mem, g_post=m_g_post)
    small_v = dict(g_pre=v_g_pre, mu=v_mu_rwkv, w0=v_w0, a0=v_a0, k_k=v_k_k, k_a=v_k_a, r_k=v_r_k.reshape(1, -1),
                   ln_w=v_ln_x_w, ln_b=v_ln_x_b, b_f=v_b_f, g_mem=v_g_mem, g_post=v_g_post)
    widths = [-(-small_w[n].shape[1] // LANES) * LANES for n in SMALL]
    pack = lambda t: jnp.concatenate([_pad_cols(t[n], wd_) for n, wd_ in zip(SMALL, widths)]
                                     + [jnp.zeros((1, LANES), F32)], axis=1)
    g_packed = jnp.concatenate([_pad_cols(small[n], wd_) for n, wd_ in zip(SMALL, widths)]
                               + [_pad_cols(loss, LANES)], axis=1)
    g_sum = _all_reduce_small(g_packed)
    s_upd = _adamw(pack(small_w), g_sum, pack(small_m), pack(small_v), "adamw_small")
    offs = [sum(widths[:i]) for i in range(len(SMALL))]

    def take(kind, n):
        i = SMALL.index(n)
        piece = s_upd[kind][:, offs[i]:offs[i] + small_w[n].shape[1]]
        return piece.reshape(r_k.shape) if n == "r_k" else piece

    total_loss = g_sum[0, sum(widths)]
    order = ("g_pre", "w_in", "mu", "w0", "w_decay_up", "a0", "w_iclr_up", "k_k", "k_a", "r_k", "ln_w", "ln_b", "b_f",
             "g_mem", "w_mem_kv", "w_out", "g_post")
    outs = [total_loss, grad_x[None]]
    for kind in range(4):
        for n in order:
            outs.append(upd[n][kind][None] if n in upd else take(kind, n))
    return tuple(outs)
```
